```python
import math
import jax, jax.numpy as jnp
from jax import lax
import numpy as np

D_MODEL = 1024
BATCH = 8
SEQ = 4096
DEPTH = 4

HEAD_DIM = 64
D_MIX = 2 * D_MODEL
A_HEADS = D_MIX // 4 // HEAD_DIM
A_WIDTH = A_HEADS * HEAD_DIM
A_PATTERNS = ((128, 1), (512, 4), (2048, 16))
B_WIDTH = D_MIX // 2
B_HEADS = B_WIDTH // HEAD_DIM
B_GROUPS = 2
B_STATE = 128
B_CONV = 4
B_CHUNK = 128
C_HEADS = D_MIX // 4 // HEAD_DIM
C_KV_HEADS = 2
C_REP = C_HEADS // C_KV_HEADS
C_WIDTH = C_HEADS * HEAD_DIM
C_KV_WIDTH = C_KV_HEADS * HEAD_DIM
C_WINDOW = 128
BAND_BLOCK = 128
NORM_EPS = 1e-6

A_COLS = 4 * A_WIDTH
B_CONV_CH = B_WIDTH + 2 * B_GROUPS * B_STATE
B_COLS = B_WIDTH + B_CONV_CH + B_HEADS
C_COLS = 2 * C_WIDTH + 2 * C_KV_WIDTH
IN_COLS = A_COLS + B_COLS + C_COLS

kernel_name = "hymba_dilated_ssd_sinkswa_trunk"


def rms_norm(x, w):
    xf = x.astype(jnp.float32)
    y = xf * lax.rsqrt(jnp.mean(xf * xf, axis=-1, keepdims=True) + NORM_EPS)
    return (y * w.astype(jnp.float32)).astype(x.dtype)


def banded_window_attention(q, k, v, window, sinks=None):
    n, L, hkv, r, dh = q.shape
    blk = BAND_BLOCK
    nb = -(-L // blk)
    pad = nb * blk - L
    if pad:
        q = jnp.pad(q, ((0, 0), (0, pad), (0, 0), (0, 0), (0, 0)))
        k = jnp.pad(k, ((0, 0), (0, pad), (0, 0), (0, 0)))
        v = jnp.pad(v, ((0, 0), (0, pad), (0, 0), (0, 0)))
    qb = q.reshape(n, nb, blk, hkv, r, dh)
    kb = k.reshape(n, nb, blk, hkv, dh)
    vb = v.reshape(n, nb, blk, hkv, dh)
    zero = jnp.zeros_like(kb[:, :1])
    kk = jnp.concatenate([jnp.concatenate([zero, kb[:, :-1]], axis=1), kb], axis=2)
    vv = jnp.concatenate([jnp.concatenate([zero, vb[:, :-1]], axis=1), vb], axis=2)
    s = jnp.einsum('nbqhrd,nbkhd->nbhrqk', qb, kk,
                   preferred_element_type=jnp.float32) * (dh ** -0.5)
    qi = jnp.arange(blk)[:, None]
    kj = jnp.arange(2 * blk)[None, :]
    dist = blk + qi - kj
    band = (dist >= 0) & (dist <= window)
    has_prev = (jnp.arange(nb)[:, None, None] > 0) | (kj >= blk)[None]
    valid = band[None] & has_prev
    s = jnp.where(valid[None, :, None, None], s, -jnp.inf)
    m = jnp.max(s, axis=-1)
    if sinks is not None:
        sink = sinks.astype(jnp.float32)[None, None, :, :, None]
        m = jnp.maximum(m, sink)
    p = jnp.exp(s - m[..., None])
    l = jnp.sum(p, axis=-1)
    if sinks is not None:
        l = l + jnp.exp(sink - m)
    o = jnp.einsum('nbhrqk,nbkhd->nbqhrd', p, vv.astype(jnp.float32))
    m = jnp.moveaxis(m, -1, 2)
    l = jnp.moveaxis(l, -1, 2)
    o = o / l[..., None]
    o = o.reshape(n, nb * blk, hkv, r, dh)[:, :L]
    m = m.reshape(n, nb * blk, hkv, r)[:, :L]
    l = l.reshape(n, nb * blk, hkv, r)[:, :L]
    return o, m, l


def to_strided(t, dil):
    b, s = t.shape[0], t.shape[1]
    t = t.reshape(b, s // dil, dil, *t.shape[2:])
    t = jnp.moveaxis(t, 2, 1)
    return t.reshape(b * dil, s // dil, *t.shape[3:])


def from_strided(t, b, dil):
    L = t.shape[1]
    t = t.reshape(b, dil, L, *t.shape[2:])
    t = jnp.moveaxis(t, 1, 2)
    return t.reshape(b, L * dil, *t.shape[3:])


def dilated_attention(q, k, v):
    b = q.shape[0]
    outs, ms, ls = [], [], []
    for window, dil in A_PATTERNS:
        o, m, l = banded_window_attention(to_strided(q, dil)[:, :, :, None], to_strided(k, dil),
                                          to_strided(v, dil), window // dil)
        outs.append(from_strided(o[:, :, :, 0], b, dil))
        ms.append(from_strided(m[:, :, :, 0], b, dil))
        ls.append(from_strided(l[:, :, :, 0], b, dil))
    m_all = jnp.stack(ms)
    l_all = jnp.stack(ls)
    o_all = jnp.stack(outs)
    wts = l_all * jnp.exp(m_all - jnp.max(m_all, axis=0))
    wts = wts / jnp.sum(wts, axis=0)
    return jnp.sum(o_all * wts[..., None], axis=0)


def causal_depthwise_conv(x, w, bias):
    ch = x.shape[-1]
    y = lax.conv_general_dilated(x, w[:, None, :].astype(x.dtype), window_strides=(1,),
                                 padding=((w.shape[0] - 1, 0),),
                                 dimension_numbers=('NWC', 'WIO', 'NWC'),
                                 feature_group_count=ch)
    return y + bias.astype(x.dtype)


def ssd_scan(xs, dt, a, bm, cm):
    b, s, h, p = xs.shape
    g, n = bm.shape[2], bm.shape[3]
    hg = h // g
    q = B_CHUNK
    nc = s // q
    f32 = jnp.float32
    x = xs.astype(f32).reshape(b, nc, q, g, hg, p)
    dtc = dt.reshape(b, nc, q, g, hg)
    bc = bm.astype(f32).reshape(b, nc, q, g, n)
    cc = cm.astype(f32).reshape(b, nc, q, g, n)
    xdt = x * dtc[..., None]
    a_cum = jnp.cumsum(dtc * a.reshape(g, hg), axis=2)
    seg = a_cum[:, :, :, None] - a_cum[:, :, None, :]
    causal = jnp.tril(jnp.ones((q, q), dtype=bool))
    decay = jnp.exp(jnp.where(causal[:, :, None, None], seg, -jnp.inf))
    cb = jnp.einsum('bclgn,bcsgn->bclsg', cc, bc)
    y_diag = jnp.einsum('bclsgh,bcsghp->bclghp', cb[..., None] * decay, xdt)
    decay_to_end = jnp.exp(a_cum[:, :, -1:] - a_cum)
    states = jnp.einsum('bclgn,bclghp->bcghpn', bc, xdt * decay_to_end[..., None])
    chunk_decay = jnp.exp(a_cum[:, :, -1])

    def step(h_state, inp):
        st, dec = inp
        return h_state * dec[..., None, None] + st, h_state

    h0 = jnp.zeros((b, g, hg, p, n), f32)
    _, h_in = lax.scan(step, h0, (jnp.moveaxis(states, 1, 0), jnp.moveaxis(chunk_decay, 1, 0)))
    h_in = jnp.moveaxis(h_in, 0, 1)
    y_off = jnp.einsum('bclgn,bcghpn->bclghp', cc, h_in) * jnp.exp(a_cum)[..., None]
    return (y_diag + y_off).reshape(b, s, h, p)


def mamba2_mixer(z, xbc, dt_raw, conv_w, conv_b, dt_bias, a_log, d_skip, norm_w):
    b, s, _ = xbc.shape
    xbc = jax.nn.silu(causal_depthwise_conv(xbc, conv_w, conv_b))
    xs, bm, cm = jnp.split(xbc, [B_WIDTH, B_WIDTH + B_GROUPS * B_STATE], axis=-1)
    xs = xs.reshape(b, s, B_HEADS, HEAD_DIM)
    bm = bm.reshape(b, s, B_GROUPS, B_STATE)
    cm = cm.reshape(b, s, B_GROUPS, B_STATE)
    dt = jax.nn.softplus(dt_raw.astype(jnp.float32) + dt_bias.astype(jnp.float32))
    a = -jnp.exp(a_log.astype(jnp.float32))
    y = ssd_scan(xs, dt, a, bm, cm)
    y = y + d_skip.astype(jnp.float32)[:, None] * xs.astype(jnp.float32)
    y = y.reshape(b, s, B_WIDTH) * jax.nn.silu(z.astype(jnp.float32))
    yg = y.reshape(b, s, B_GROUPS, B_WIDTH // B_GROUPS)
    yg = yg * lax.rsqrt(jnp.mean(yg * yg, axis=-1, keepdims=True) + NORM_EPS)
    y = yg.reshape(b, s, B_WIDTH) * norm_w.astype(jnp.float32)
    return y.astype(z.dtype)


def _fwd_setup_inputs(seed: int = 0) -> dict:
    key = jax.random.key(seed)
    ks = jax.random.split(key, 16)
    f32 = jnp.float32
    x = jax.random.normal(ks[0], (BATCH, SEQ, D_MODEL), f32)
    c = jax.random.normal(ks[1], (BATCH, D_MODEL), f32)
    ada_w = jax.random.normal(ks[2], (DEPTH, D_MODEL, 3 * D_MODEL), f32) * (0.5 * D_MODEL ** -0.5)
    ada_b = jax.random.normal(ks[3], (DEPTH, 3 * D_MODEL), f32) * 0.02
    pre_norm_w = 1.0 + 0.02 * jax.random.normal(ks[4], (DEPTH, D_MODEL), f32)
    post_norm_w = 1.0 + 0.02 * jax.random.normal(ks[5], (DEPTH, D_MODEL), f32)
    w_in = jax.random.normal(ks[6], (DEPTH, D_MODEL, IN_COLS), f32) * D_MODEL ** -0.5
    conv_w = jax.random.normal(ks[7], (DEPTH, B_CONV, B_CONV_CH), f32) * B_CONV ** -0.5
    conv_b = 0.02 * jax.random.normal(ks[8], (DEPTH, B_CONV_CH), f32)
    dt0 = jnp.exp(jax.random.uniform(ks[9], (DEPTH, B_HEADS), f32,
                                     math.log(1e-3), math.log(1e-1)))
    dt_bias = dt0 + jnp.log(-jnp.expm1(-dt0))
    a_log = jnp.log(jax.random.uniform(ks[10], (DEPTH, B_HEADS), f32, 1.0, 16.0))
    d_skip = 1.0 + 0.02 * jax.random.normal(ks[11], (DEPTH, B_HEADS), f32)
    ssm_norm_w = 1.0 + 0.02 * jax.random.normal(ks[12], (DEPTH, B_WIDTH), f32)
    sinks = 0.5 * jax.random.normal(ks[13], (DEPTH, C_HEADS), f32)
    w_out = jax.random.normal(ks[14], (DEPTH, D_MIX, D_MODEL), f32) * D_MIX ** -0.5
    return {"x": x, "c": c, "ada_w": ada_w, "ada_b": ada_b, "pre_norm_w": pre_norm_w,
            "post_norm_w": post_norm_w, "w_in": w_in, "conv_w": conv_w, "conv_b": conv_b,
            "dt_bias": dt_bias, "a_log": a_log, "d_skip": d_skip, "ssm_norm_w": ssm_norm_w,
            "sinks": sinks, "w_out": w_out}


def _fwd_reference(x, c, ada_w, ada_b, pre_norm_w, post_norm_w, w_in, conv_w, conv_b,
              dt_bias, a_log, d_skip, ssm_norm_w, sinks, w_out):
    b, s, _ = x.shape
    c_act = jax.nn.silu(c)
    for i in range(DEPTH):
        mod = c_act @ ada_w[i] + ada_b[i]
        shift, scale, gate = jnp.split(mod, 3, axis=-1)
        h = rms_norm(x, pre_norm_w[i]) * (1.0 + scale[:, None]) + shift[:, None]
        proj = h @ w_in[i]
        pa, pb, pc = jnp.split(proj, [A_COLS, A_COLS + B_COLS], axis=-1)

        qa, ka, va, za = jnp.split(pa, 4, axis=-1)
        ya = dilated_attention(qa.reshape(b, s, A_HEADS, HEAD_DIM), ka.reshape(b, s, A_HEADS, HEAD_DIM),
                               va.reshape(b, s, A_HEADS, HEAD_DIM))
        ya = ya.reshape(b, s, A_WIDTH).astype(x.dtype) * jax.nn.silu(za)

        zb, xbc, dtb = jnp.split(pb, [B_WIDTH, B_WIDTH + B_CONV_CH], axis=-1)
        yb = mamba2_mixer(zb, xbc, dtb, conv_w[i], conv_b[i], dt_bias[i], a_log[i],
                          d_skip[i], ssm_norm_w[i])

        qc, zc, kc, vc = jnp.split(pc, [C_WIDTH, 2 * C_WIDTH, 2 * C_WIDTH + C_KV_WIDTH], axis=-1)
        oc, _, _ = banded_window_attention(qc.reshape(b, s, C_KV_HEADS, C_REP, HEAD_DIM),
                                           kc.reshape(b, s, C_KV_HEADS, HEAD_DIM),
                                           vc.reshape(b, s, C_KV_HEADS, HEAD_DIM),
                                           C_WINDOW, sinks[i].reshape(C_KV_HEADS, C_REP))
        yc = oc.reshape(b, s, C_WIDTH).astype(x.dtype) * jax.nn.silu(zc)

        y = jnp.concatenate([ya, yb, yc], axis=-1) @ w_out[i]
        x = x + gate[:, None] * rms_norm(y, post_norm_w[i])
    return x


import jax as _jax
import jax.numpy as _jnp

TWIN_FORMAT = 'train_step'
FWD_PARAMS = ['x', 'c', 'ada_w', 'ada_b', 'pre_norm_w', 'post_norm_w', 'w_in', 'conv_w', 'conv_b', 'dt_bias', 'a_log', 'd_skip', 'ssm_norm_w', 'sinks', 'w_out']
TWIN_WEIGHTS = ['ada_w', 'ada_b', 'pre_norm_w', 'post_norm_w', 'w_in', 'conv_w', 'conv_b', 'dt_bias', 'a_log', 'd_skip', 'ssm_norm_w', 'sinks', 'w_out']
TWIN_DIFF_INPUT = 'x'
TWIN_INPUTS = ['x', 'c', 'ada_w', 'ada_b', 'pre_norm_w', 'post_norm_w', 'w_in', 'conv_w', 'conv_b', 'dt_bias', 'a_log', 'd_skip', 'ssm_norm_w', 'sinks', 'w_out', 'loss_target', 'm_ada_w', 'm_ada_b', 'm_pre_norm_w', 'm_post_norm_w', 'm_w_in', 'm_conv_w', 'm_conv_b', 'm_dt_bias', 'm_a_log', 'm_d_skip', 'm_ssm_norm_w', 'm_sinks', 'm_w_out', 'v_ada_w', 'v_ada_b', 'v_pre_norm_w', 'v_post_norm_w', 'v_w_in', 'v_conv_w', 'v_conv_b', 'v_dt_bias', 'v_a_log', 'v_d_skip', 'v_ssm_norm_w', 'v_sinks', 'v_w_out']
TWIN_OUTPUTS = ['loss', 'grad_x', 'grad_ada_w', 'grad_ada_b', 'grad_pre_norm_w', 'grad_post_norm_w', 'grad_w_in', 'grad_conv_w', 'grad_conv_b', 'grad_dt_bias', 'grad_a_log', 'grad_d_skip', 'grad_ssm_norm_w', 'grad_sinks', 'grad_w_out', 'delta_ada_w', 'delta_ada_b', 'delta_pre_norm_w', 'delta_post_norm_w', 'delta_w_in', 'delta_conv_w', 'delta_conv_b', 'delta_dt_bias', 'delta_a_log', 'delta_d_skip', 'delta_ssm_norm_w', 'delta_sinks', 'delta_w_out', 'new_m_ada_w', 'new_m_ada_b', 'new_m_pre_norm_w', 'new_m_post_norm_w', 'new_m_w_in', 'new_m_conv_w', 'new_m_conv_b', 'new_m_dt_bias', 'new_m_a_log', 'new_m_d_skip', 'new_m_ssm_norm_w', 'new_m_sinks', 'new_m_w_out', 'new_v_ada_w', 'new_v_ada_b', 'new_v_pre_norm_w', 'new_v_post_norm_w', 'new_v_w_in', 'new_v_conv_w', 'new_v_conv_b', 'new_v_dt_bias', 'new_v_a_log', 'new_v_d_skip', 'new_v_ssm_norm_w', 'new_v_sinks', 'new_v_w_out']
TWIN_LEAF_KINDS = {'loss': 'loss', 'grad_x': 'grad_x', 'grad_ada_w': 'grad_w', 'grad_ada_b': 'grad_w', 'grad_pre_norm_w': 'grad_w', 'grad_post_norm_w': 'grad_w', 'grad_w_in': 'grad_w', 'grad_conv_w': 'grad_w', 'grad_conv_b': 'grad_w', 'grad_dt_bias': 'grad_w', 'grad_a_log': 'grad_w', 'grad_d_skip': 'grad_w', 'grad_ssm_norm_w': 'grad_w', 'grad_sinks': 'grad_w', 'grad_w_out': 'grad_w', 'delta_ada_w': 'delta_w', 'delta_ada_b': 'delta_w', 'delta_pre_norm_w': 'delta_w', 'delta_post_norm_w': 'delta_w', 'delta_w_in': 'delta_w', 'delta_conv_w': 'delta_w', 'delta_conv_b': 'delta_w', 'delta_dt_bias': 'delta_w', 'delta_a_log': 'delta_w', 'delta_d_skip': 'delta_w', 'delta_ssm_norm_w': 'delta_w', 'delta_sinks': 'delta_w', 'delta_w_out': 'delta_w', 'new_m_ada_w': 'new_m', 'new_m_ada_b': 'new_m', 'new_m_pre_norm_w': 'new_m', 'new_m_post_norm_w': 'new_m', 'new_m_w_in': 'new_m', 'new_m_conv_w': 'new_m', 'new_m_conv_b': 'new_m', 'new_m_dt_bias': 'new_m', 'new_m_a_log': 'new_m', 'new_m_d_skip': 'new_m', 'new_m_ssm_norm_w': 'new_m', 'new_m_sinks': 'new_m', 'new_m_w_out': 'new_m', 'new_v_ada_w': 'new_v', 'new_v_ada_b': 'new_v', 'new_v_pre_norm_w': 'new_v', 'new_v_post_norm_w': 'new_v', 'new_v_w_in': 'new_v', 'new_v_conv_w': 'new_v', 'new_v_conv_b': 'new_v', 'new_v_dt_bias': 'new_v', 'new_v_a_log': 'new_v', 'new_v_d_skip': 'new_v', 'new_v_ssm_norm_w': 'new_v', 'new_v_sinks': 'new_v', 'new_v_w_out': 'new_v'}


def _forward(args):
    return _fwd_reference(*[args[k] for k in FWD_PARAMS])


def _output_shape():
    out = _jax.eval_shape(lambda: _forward(_fwd_setup_inputs(0)))
    return out.shape, out.dtype

N_MICROBATCH = 1
ADAM_LR = 0.001
ADAM_B1 = 0.9
ADAM_B2 = 0.999
ADAM_EPS = 1e-08
ADAM_WD = 0.01
ADAM_STEP = 10
PER_EXAMPLE_BATCH_AXIS = {'x': 0, 'c': 0, 'loss_target': 0}
SHARED_INPUTS = []
_WEIGHT_DTYPES = {'ada_w': _jnp.float32, 'ada_b': _jnp.float32, 'pre_norm_w': _jnp.float32, 'post_norm_w': _jnp.float32, 'w_in': _jnp.float32, 'conv_w': _jnp.float32, 'conv_b': _jnp.float32, 'dt_bias': _jnp.float32, 'a_log': _jnp.float32, 'd_skip': _jnp.float32, 'ssm_norm_w': _jnp.float32, 'sinks': _jnp.float32, 'w_out': _jnp.float32}
MOMENT_SCALE = {'ada_w': 1.780872e+00, 'ada_b': 3.283691e+00, 'pre_norm_w': 1.617886e-01, 'post_norm_w': 3.732264e+00, 'w_in': 9.806001e-02, 'conv_w': 1.656826e-01, 'conv_b': 3.948455e-01, 'dt_bias': 2.589238e-01, 'a_log': 1.716984e+00, 'd_skip': 6.935639e-01, 'ssm_norm_w': 2.700752e-01, 'sinks': 1.267781e-02, 'w_out': 2.837381e-01}


def _to_microbatches(a, axis):
    t = _jnp.moveaxis(a, axis, 0)
    t = t.reshape((N_MICROBATCH, t.shape[0] // N_MICROBATCH) + t.shape[1:])
    return _jnp.moveaxis(t, 1, axis + 1)


def setup_inputs(seed: int = 0) -> dict:
    inp = _fwd_setup_inputs(seed)
    key = _jax.random.fold_in(_jax.random.key(seed), 7919)
    shape, _ = _output_shape()
    out = dict(inp)
    out["loss_target"] = _jax.random.normal(_jax.random.fold_in(key, 0), shape, _jnp.float32)
    for i, name in enumerate(TWIN_WEIGHTS):
        w = inp[name].astype(_jnp.float32)
        if MOMENT_SCALE is None:
            s = _jnp.sqrt(_jnp.mean(_jnp.square(w)) + 1e-30)
        else:
            s = MOMENT_SCALE[name]
        km, kv = _jax.random.split(_jax.random.fold_in(key, i + 1))
        out[name] = w
        out["m_" + name] = s * _jax.random.normal(km, w.shape, _jnp.float32)
        out["v_" + name] = (s * s) * _jax.random.uniform(kv, w.shape, _jnp.float32, 0.5, 1.5)
    if N_MICROBATCH > 1:
        for name, axis in PER_EXAMPLE_BATCH_AXIS.items():
            out[name] = _to_microbatches(out[name], axis)
    return {'x': out['x'], 'c': out['c'], 'ada_w': out['ada_w'], 'ada_b': out['ada_b'], 'pre_norm_w': out['pre_norm_w'], 'post_norm_w': out['post_norm_w'], 'w_in': out['w_in'], 'conv_w': out['conv_w'], 'conv_b': out['conv_b'], 'dt_bias': out['dt_bias'], 'a_log': out['a_log'], 'd_skip': out['d_skip'], 'ssm_norm_w': out['ssm_norm_w'], 'sinks': out['sinks'], 'w_out': out['w_out'], 'loss_target': out['loss_target'], 'm_ada_w': out['m_ada_w'], 'm_ada_b': out['m_ada_b'], 'm_pre_norm_w': out['m_pre_norm_w'], 'm_post_norm_w': out['m_post_norm_w'], 'm_w_in': out['m_w_in'], 'm_conv_w': out['m_conv_w'], 'm_conv_b': out['m_conv_b'], 'm_dt_bias': out['m_dt_bias'], 'm_a_log': out['m_a_log'], 'm_d_skip': out['m_d_skip'], 'm_ssm_norm_w': out['m_ssm_norm_w'], 'm_sinks': out['m_sinks'], 'm_w_out': out['m_w_out'], 'v_ada_w': out['v_ada_w'], 'v_ada_b': out['v_ada_b'], 'v_pre_norm_w': out['v_pre_norm_w'], 'v_post_norm_w': out['v_post_norm_w'], 'v_w_in': out['v_w_in'], 'v_conv_w': out['v_conv_w'], 'v_conv_b': out['v_conv_b'], 'v_dt_bias': out['v_dt_bias'], 'v_a_log': out['v_a_log'], 'v_d_skip': out['v_d_skip'], 'v_ssm_norm_w': out['v_ssm_norm_w'], 'v_sinks': out['v_sinks'], 'v_w_out': out['v_w_out']}


def _loss(weights, diff, rest, loss_target):
    with _jax.named_scope("forward"):
        args = {**rest, TWIN_DIFF_INPUT: diff, **{k: w.astype(_WEIGHT_DTYPES[k]) for k, w in weights.items()}}
        y = _forward(args)
    with _jax.named_scope("loss_head"):
        err = _jnp.square(y.astype(_jnp.float32) - loss_target)
        return 0.5 * _jnp.sum(_jnp.mean(err, axis=-1)) if err.ndim else 0.5 * err


def _adamw(w, g, m, v):
    m = ADAM_B1 * m + (1.0 - ADAM_B1) * g
    v = ADAM_B2 * v + (1.0 - ADAM_B2) * _jnp.square(g)
    m_hat = m / (1.0 - ADAM_B1 ** ADAM_STEP)
    v_hat = v / (1.0 - ADAM_B2 ** ADAM_STEP)
    delta = -ADAM_LR * (m_hat / (_jnp.sqrt(v_hat) + ADAM_EPS) + ADAM_WD * w)
    return delta, m, v


def reference(x, c, ada_w, ada_b, pre_norm_w, post_norm_w, w_in, conv_w, conv_b, dt_bias, a_log, d_skip, ssm_norm_w, sinks, w_out, loss_target, m_ada_w, m_ada_b, m_pre_norm_w, m_post_norm_w, m_w_in, m_conv_w, m_conv_b, m_dt_bias, m_a_log, m_d_skip, m_ssm_norm_w, m_sinks, m_w_out, v_ada_w, v_ada_b, v_pre_norm_w, v_post_norm_w, v_w_in, v_conv_w, v_conv_b, v_dt_bias, v_a_log, v_d_skip, v_ssm_norm_w, v_sinks, v_w_out):
    given = dict(x=x, c=c, ada_w=ada_w, ada_b=ada_b, pre_norm_w=pre_norm_w, post_norm_w=post_norm_w, w_in=w_in, conv_w=conv_w, conv_b=conv_b, dt_bias=dt_bias, a_log=a_log, d_skip=d_skip, ssm_norm_w=ssm_norm_w, sinks=sinks, w_out=w_out, loss_target=loss_target, m_ada_w=m_ada_w, m_ada_b=m_ada_b, m_pre_norm_w=m_pre_norm_w, m_post_norm_w=m_post_norm_w, m_w_in=m_w_in, m_conv_w=m_conv_w, m_conv_b=m_conv_b, m_dt_bias=m_dt_bias, m_a_log=m_a_log, m_d_skip=m_d_skip, m_ssm_norm_w=m_ssm_norm_w, m_sinks=m_sinks, m_w_out=m_w_out, v_ada_w=v_ada_w, v_ada_b=v_ada_b, v_pre_norm_w=v_pre_norm_w, v_post_norm_w=v_post_norm_w, v_w_in=v_w_in, v_conv_w=v_conv_w, v_conv_b=v_conv_b, v_dt_bias=v_dt_bias, v_a_log=v_a_log, v_d_skip=v_d_skip, v_ssm_norm_w=v_ssm_norm_w, v_sinks=v_sinks, v_w_out=v_w_out)
    weights = {n: given[n] for n in TWIN_WEIGHTS}
    shared = {n: given[n] for n in SHARED_INPUTS}
    per_example = {n: given[n] for n in ['x', 'c']}
    grad_fn = _jax.value_and_grad(_loss, argnums=(0, 1))

    def one_microbatch(ex, loss_target):
        ex = dict(ex)
        diff = ex.pop(TWIN_DIFF_INPUT)
        return grad_fn(weights, diff, {**shared, **ex}, loss_target)

    if N_MICROBATCH == 1:
        loss, (grad_w, grad_x) = one_microbatch(per_example, given["loss_target"])
    else:
        def body(carry, xs):
            loss_sum, grad_sum = carry
            l_k, (gw_k, gx_k) = one_microbatch(xs[0], xs[1])
            with _jax.named_scope("update"):
                return (loss_sum + l_k, _jax.tree.map(_jnp.add, grad_sum, gw_k)), gx_k

        init = (_jnp.zeros((), _jnp.float32), _jax.tree.map(_jnp.zeros_like, weights))
        (loss, grad_w), grad_x = _jax.lax.scan(body, init, (per_example, given["loss_target"]))
    with _jax.named_scope("update"):
        delta_w, new_m, new_v = {}, {}, {}
        for n in TWIN_WEIGHTS:
            delta_w[n], new_m[n], new_v[n] = _adamw(weights[n], grad_w[n], given["m_" + n], given["v_" + n])
    return (loss, grad_x, *[grad_w[n] for n in TWIN_WEIGHTS], *[delta_w[n] for n in TWIN_WEIGHTS],
            *[new_m[n] for n in TWIN_WEIGHTS], *[new_v[n] for n in TWIN_WEIGHTS])
```

```python
import math

import jax
import jax.numpy as jnp
from jax import lax
from jax.experimental import pallas as pl
from jax.experimental.pallas import tpu as pltpu

F32 = jnp.float32
MXU = jnp.bfloat16
HI = lax.Precision.HIGHEST
MESH = pl.DeviceIdType.MESH

SEQ = 4096
D = 1024
DEPTH = 4
HD = 64
LANES = 128
BLK = 128
DILS = (1, 4, 16)
NEG = -1e30
EPS = 1e-6
MIB = 1024 * 1024

NP = 6144
QA, KA, VA, ZA = 0, 512, 1024, 1536
ZB, XBC = 2048, 3072
QC, ZC, KC, VC = 4608, 5120, 5632, 5760
DTC = 5888
IN_COLS = 5904
SHARD_IN = IN_COLS // 4
CONV_CH = 1536
TM = 512

ADAM_LR, ADAM_B1, ADAM_B2, ADAM_EPS, ADAM_WD, ADAM_STEP = 0.001, 0.9, 0.999, 1e-08, 0.01, 10

NT = (((1,), (1,)), ((), ()))
TN = (((0,), (0,)), ((), ()))


def _cp(vmem_mib=48):
    return pltpu.CompilerParams(vmem_limit_bytes=vmem_mib * MIB)


def _sds(shape, dtype=F32):
    return jax.ShapeDtypeStruct(shape, dtype)


def _full(shape):
    n = len(shape)
    return pl.BlockSpec(shape, lambda *_: (0,) * n)


def _mm(a, b, dims=None):
    if dims is None:
        return jnp.dot(a.astype(MXU), b.astype(MXU), preferred_element_type=F32)
    return lax.dot_general(a.astype(MXU), b.astype(MXU), dims, preferred_element_type=F32)


def _sigmoid(x):
    return 1.0 / (1.0 + jnp.exp(-x))


def _silu(x):
    return x * _sigmoid(x)


def _dsilu(x):
    s = _sigmoid(x)
    return s * (1.0 + x * (1.0 - s))


def _softplus(x):
    ax = jnp.where(x >= 0, x, -x)
    return jnp.maximum(x, 0.0) + jnp.log1p(jnp.exp(-ax))


def _half_masks():
    lane = lax.broadcasted_iota(jnp.int32, (1, LANES), 1)
    m0 = (lane < HD).astype(F32)
    return m0, 1.0 - m0


def _allgather8(v, name):
    r, cc = v.shape

    def body(v_ref, out_ref, send_sems, recv_sems):
        x, y, c = lax.axis_index("x"), lax.axis_index("y"), lax.axis_index("c")
        me = 4 * x + 2 * y + c
        out_ref[me] = v_ref[...]
        peers = []
        for k in range(1, 8):
            px = 1 - x if k & 4 else x
            py = 1 - y if k & 2 else y
            pc = 1 - c if k & 1 else c
            peers.append((px, py, pc))
        sends = []
        for k, peer in enumerate(peers):
            cp = pltpu.make_async_remote_copy(src_ref=v_ref, dst_ref=out_ref.at[me], send_sem=send_sems.at[k],
                                              recv_sem=recv_sems.at[k], device_id=peer, device_id_type=MESH)
            cp.start()
            sends.append(cp)
        for k, (px, py, pc) in enumerate(peers):
            pltpu.make_async_remote_copy(src_ref=v_ref, dst_ref=out_ref.at[4 * px + 2 * py + pc], send_sem=send_sems.at[k],
                                         recv_sem=recv_sems.at[k], device_id=(px, py, pc), device_id_type=MESH).wait_recv()
        for cp in sends:
            cp.wait_send()

    return pl.pallas_call(
        body, name=name, out_shape=_sds((8, r, cc)),
        in_specs=[pl.BlockSpec(memory_space=pltpu.VMEM)], out_specs=pl.BlockSpec(memory_space=pltpu.VMEM),
        scratch_shapes=[pltpu.SemaphoreType.DMA((7,)), pltpu.SemaphoreType.DMA((7,))],
        compiler_params=_cp(32),
    )(v)


def _exchange4(arrs, scatter, name):
    n = len(arrs)
    outs = tuple(_sds(a.shape if scatter else (4,) + a.shape, a.dtype) for a in arrs)

    def body(*refs):
        ins, outs_, (send_sems, recv_sems, loc_sems) = refs[:n], refs[n:2 * n], refs[2 * n:]
        x, y, c = lax.axis_index("x"), lax.axis_index("y"), lax.axis_index("c")
        mine = 2 * x + y
        chips = [(1 - x, y), (x, 1 - y), (1 - x, 1 - y)]
        local, sent = [], []
        for i in range(n):
            loc = pltpu.make_async_copy(ins[i].at[mine] if scatter else ins[i], outs_[i].at[mine], loc_sems.at[i])
            loc.start()
            local.append(loc)
            for j, (px, py) in enumerate(chips):
                src = ins[i].at[2 * px + py] if scatter else ins[i]
                cp = pltpu.make_async_remote_copy(src_ref=src, dst_ref=outs_[i].at[mine], send_sem=send_sems.at[3 * i + j],
                                                  recv_sem=recv_sems.at[3 * i + j], device_id=(px, py, c), device_id_type=MESH)
                cp.start()
                sent.append(cp)
        for i in range(n):
            for j, (px, py) in enumerate(chips):
                src = ins[i].at[mine] if scatter else ins[i]
                pltpu.make_async_remote_copy(src_ref=src, dst_ref=outs_[i].at[2 * px + py], send_sem=send_sems.at[3 * i + j],
                                             recv_sem=recv_sems.at[3 * i + j], device_id=(px, py, c), device_id_type=MESH).wait_recv()
        for cp in sent:
            cp.wait_send()
        for loc in local:
            loc.wait()

    hbm = pl.BlockSpec(memory_space=pltpu.HBM)
    return pl.pallas_call(
        body, name=name, out_shape=outs, in_specs=[hbm] * n, out_specs=tuple([hbm] * n),
        scratch_shapes=[pltpu.SemaphoreType.DMA((3 * n,)), pltpu.SemaphoreType.DMA((3 * n,)), pltpu.SemaphoreType.DMA((n,))],
    )(*arrs)


def _sibling_swap(arrs, name):
    n = len(arrs)

    def body(*refs):
        ins, outs_, (send_sems, recv_sems) = refs[:n], refs[n:2 * n], refs[2 * n:]
        sib = (lax.axis_index("x"), lax.axis_index("y"), 1 - lax.axis_index("c"))
        cps = [pltpu.make_async_remote_copy(src_ref=ins[i], dst_ref=outs_[i], send_sem=send_sems.at[i], recv_sem=recv_sems.at[i],
                                            device_id=sib, device_id_type=MESH) for i in range(n)]
        for cp in cps:
            cp.start()
        for cp in cps:
            cp.wait_recv()
        for cp in cps:
            cp.wait_send()

    hbm = pl.BlockSpec(memory_space=pltpu.HBM)
    return pl.pallas_call(
        body, name=name, out_shape=tuple(_sds(a.shape, a.dtype) for a in arrs), in_specs=[hbm] * n, out_specs=tuple([hbm] * n),
        scratch_shapes=[pltpu.SemaphoreType.DMA((n,)), pltpu.SemaphoreType.DMA((n,))],
    )(*arrs)


def _cast_bf16(a, rows, name):
    r, cc = a.shape

    def body(a_ref, o_ref):
        o_ref[...] = a_ref[...].astype(jnp.bfloat16)

    return pl.pallas_call(body, name=name, out_shape=_sds((r, cc), jnp.bfloat16), grid=(r // rows,),
                          in_specs=[pl.BlockSpec((rows, cc), lambda i: (i, 0))],
                          out_specs=pl.BlockSpec((rows, cc), lambda i: (i, 0)), compiler_params=_cp())(a)


def _sum_blocks(a, rows, name):
    k, r, cc = a.shape

    def body(a_ref, o_ref):
        acc = a_ref[0]
        for j in range(1, k):
            acc = acc + a_ref[j]
        o_ref[...] = acc

    return pl.pallas_call(body, name=name, out_shape=_sds((r, cc)), grid=(r // rows,),
                          in_specs=[pl.BlockSpec((k, rows, cc), lambda i: (0, i, 0))],
                          out_specs=pl.BlockSpec((rows, cc), lambda i: (i, 0)), compiler_params=_cp())(a)


def _adamw(w, parts, m, v, rows, name):
    r, cc = w.shape
    np_ = len(parts)
    c1 = 1.0 / (1.0 - ADAM_B1 ** ADAM_STEP)
    c2 = 1.0 / (1.0 - ADAM_B2 ** ADAM_STEP)

    def body(*refs):
        w_ref, p_refs, (m_ref, v_ref, g_ref, d_ref, nm_ref, nv_ref) = refs[0], refs[1:1 + np_], refs[1 + np_:]
        g = p_refs[0][...]
        for p_ref in p_refs[1:]:
            g = g + p_ref[...]
        nm = ADAM_B1 * m_ref[...] + (1.0 - ADAM_B1) * g
        nv = ADAM_B2 * v_ref[...] + (1.0 - ADAM_B2) * (g * g)
        g_ref[...] = g
        nm_ref[...] = nm
        nv_ref[...] = nv
        d_ref[...] = -ADAM_LR * ((nm * c1) / (jnp.sqrt(nv * c2) + ADAM_EPS) + ADAM_WD * w_ref[...])

    spec = pl.BlockSpec((rows, cc), lambda i: (i, 0))
    return pl.pallas_call(body, name=name, out_shape=(_sds((r, cc)),) * 4, grid=(r // rows,),
                          in_specs=[spec] * (3 + np_), out_specs=(spec,) * 4, compiler_params=_cp())(w, *parts, m, v)


def _gate_fwd(o, proj, zblk, name):
    def body(o_ref, z_ref, y_ref):
        y_ref[...] = o_ref[...] * _silu(z_ref[...])

    return pl.pallas_call(body, name=name, out_shape=_sds((SEQ, 512)), grid=(SEQ // TM,),
                          in_specs=[pl.BlockSpec((TM, 512), lambda i: (i, 0)), pl.BlockSpec((TM, 512), lambda i: (i, zblk))],
                          out_specs=pl.BlockSpec((TM, 512), lambda i: (i, 0)), compiler_params=_cp())(o, proj)


def _gate_bwd(dy, o, proj, zblk, name):
    def body(dy_ref, o_ref, z_ref, do_ref, dz_ref):
        dy, z = dy_ref[...], z_ref[...]
        do_ref[...] = dy * _silu(z)
        dz_ref[...] = dy * o_ref[...] * _dsilu(z)

    return pl.pallas_call(body, name=name, out_shape=(_sds((SEQ, 512)), _sds((SEQ, 512))), grid=(SEQ // TM,),
                          in_specs=[pl.BlockSpec((TM, 512), lambda i: (i, 0)), pl.BlockSpec((TM, 512), lambda i: (i, 0)),
                                    pl.BlockSpec((TM, 512), lambda i: (i, zblk))],
                          out_specs=(pl.BlockSpec((TM, 512), lambda i: (i, 0)),) * 2, compiler_params=_cp())(dy, o, proj)


def _band_valid(lo):
    qi = lax.broadcasted_iota(jnp.int32, (BLK, 2 * BLK), 0)
    kj = lax.broadcasted_iota(jnp.int32, (BLK, 2 * BLK), 1)
    dist = BLK + qi - kj
    return (dist >= 0) & (dist <= BLK) & (kj >= lo)


def _rows(st, dil):
    if dil == 1:
        return pl.ds(pl.multiple_of(st, BLK), BLK)
    return pl.ds(st, BLK, stride=dil)


def _block_pos(n, dil):
    nb = SEQ // (dil * BLK)
    r, b = n // nb, n % nb
    hp = (b > 0).astype(jnp.int32)
    st = r + dil * BLK * b
    return st, st - dil * BLK * hp, BLK * (1 - hp)


def _attn_fwd(proj, qblk, kblk, vblk, dils, gqa, sink_x, name):
    has_sink = sink_x is not None

    def body(*refs):
        if has_sink:
            q_ref, k_ref, v_ref, s_ref, o_ref, lse_ref, m_scr, z_scr = refs
        else:
            q_ref, k_ref, v_ref, o_ref, lse_ref, m_scr, z_scr = refs
        m0, m1 = _half_masks()
        g = pl.program_id(0) // 2
        o_ref[...] = jnp.zeros_like(o_ref)
        z_scr[...] = jnp.zeros_like(z_scr)
        m_scr[...] = jnp.full_like(m_scr, NEG)
        for dil in dils:
            def step(n, carry, dil=dil):
                st, stp, lo = _block_pos(n, dil)
                rq, rp = _rows(st, dil), _rows(stp, dil)
                valid = _band_valid(lo)
                q = q_ref[rq, :]
                kk = jnp.concatenate([k_ref[rp, :], k_ref[rq, :]], axis=0)
                vv = jnp.concatenate([v_ref[rp, :], v_ref[rq, :]], axis=0)
                if gqa:
                    kr, vr = pltpu.roll(kk, HD, axis=1), pltpu.roll(vv, HD, axis=1)
                m_pair = l_pair = o_pair = 0.0
                for a, msk in enumerate((m0, m1)):
                    if gqa:
                        w = (g == a).astype(F32)
                        ka, va = kk * w + kr * (1.0 - w), vv * w + vr * (1.0 - w)
                    else:
                        ka, va = kk, vv
                    s = _mm(q * msk, ka, NT) * (HD ** -0.5)
                    s = jnp.where(valid, s, NEG)
                    m = jnp.max(s, axis=1, keepdims=True)
                    if has_sink:
                        sk = s_ref[...][:, HD * a:HD * a + 1]
                        m = jnp.maximum(m, sk)
                    p = jnp.exp(s - m)
                    l = jnp.sum(p, axis=1, keepdims=True)
                    if has_sink:
                        l = l + jnp.exp(sk - m)
                    o_pair = o_pair + _mm(p, va) * msk
                    m_pair = m_pair + m * msk
                    l_pair = l_pair + l * msk
                m_old = m_scr[rq, :]
                m_new = jnp.maximum(m_old, m_pair)
                alpha, beta = jnp.exp(m_old - m_new), jnp.exp(m_pair - m_new)
                o_ref[rq, :] = o_ref[rq, :] * alpha + o_pair * beta
                z_scr[rq, :] = z_scr[rq, :] * alpha + l_pair * beta
                m_scr[rq, :] = m_new
                return carry
            lax.fori_loop(0, SEQ // BLK, step, 0)

        def fin(t, carry):
            rt = pl.ds(pl.multiple_of(t * TM, TM), TM)
            z = z_scr[rt, :]
            o_ref[rt, :] = o_ref[rt, :] / z
            lse_ref[rt, :] = m_scr[rt, :] + jnp.log(z)
            return carry
        lax.fori_loop(0, SEQ // TM, fin, 0)

    col = lambda blk: pl.BlockSpec((SEQ, LANES), lambda p, blk=blk: (0, blk + p))
    kv = (lambda blk: pl.BlockSpec((SEQ, LANES), lambda p, blk=blk: (0, blk))) if gqa else col
    in_specs = [col(qblk), kv(kblk), kv(vblk)]
    args = [proj, proj, proj]
    if has_sink:
        in_specs.append(pl.BlockSpec((1, LANES), lambda p: (0, p)))
        args.append(sink_x)
    out = pl.BlockSpec((SEQ, LANES), lambda p: (0, p))
    return pl.pallas_call(body, name=name, out_shape=(_sds((SEQ, 512)), _sds((SEQ, 512))), grid=(4,),
                          in_specs=in_specs, out_specs=(out, out),
                          scratch_shapes=[pltpu.VMEM((SEQ, LANES), F32), pltpu.VMEM((SEQ, LANES), F32)],
                          compiler_params=_cp(48))(*args)


def _attn_bwd(proj, qblk, kblk, vblk, do, o, lse, dils, gqa, sink_x, name):
    has_sink = sink_x is not None

    def body(*refs):
        if has_sink:
            q_ref, k_ref, v_ref, do_ref, o_ref, lse_ref, s_ref, dq_ref, dk_ref, dv_ref, ds_ref = refs
        else:
            q_ref, k_ref, v_ref, do_ref, o_ref, lse_ref, dq_ref, dk_ref, dv_ref = refs
        m0, m1 = _half_masks()
        pid = pl.program_id(0)
        g = pid // 2
        dq_ref[...] = jnp.zeros_like(dq_ref)
        if gqa:
            @pl.when(pid == 0)
            def _():
                dk_ref[...] = jnp.zeros_like(dk_ref)
                dv_ref[...] = jnp.zeros_like(dv_ref)
        else:
            dk_ref[...] = jnp.zeros_like(dk_ref)
            dv_ref[...] = jnp.zeros_like(dv_ref)
        dsink = jnp.zeros((1, LANES), F32)
        for dil in dils:
            def step(n, dsink, dil=dil):
                st, stp, lo = _block_pos(n, dil)
                rq, rp = _rows(st, dil), _rows(stp, dil)
                valid = _band_valid(lo)
                q, do_, o_, lse_ = q_ref[rq, :], do_ref[rq, :], o_ref[rq, :], lse_ref[rq, :]
                kk = jnp.concatenate([k_ref[rp, :], k_ref[rq, :]], axis=0)
                vv = jnp.concatenate([v_ref[rp, :], v_ref[rq, :]], axis=0)
                if gqa:
                    kr, vr = pltpu.roll(kk, HD, axis=1), pltpu.roll(vv, HD, axis=1)
                dq_pair = 0.0
                dk_sum = dv_sum = 0.0
                for a, msk in enumerate((m0, m1)):
                    if gqa:
                        w = (g == a).astype(F32)
                        ka, va = kk * w + kr * (1.0 - w), vv * w + vr * (1.0 - w)
                    else:
                        ka, va = kk, vv
                    qa, doa = q * msk, do_ * msk
                    delta = jnp.sum(doa * o_, axis=1, keepdims=True)
                    lse_a = lse_[:, HD * a:HD * a + 1]
                    s = _mm(qa, ka, NT) * (HD ** -0.5)
                    s = jnp.where(valid, s, NEG)
                    p = jnp.exp(s - lse_a)
                    dp = _mm(doa, va, NT)
                    dsr = p * (dp - delta) * (HD ** -0.5)
                    dq_pair = dq_pair + _mm(dsr, ka) * msk
                    dka, dva = _mm(dsr, qa, TN), _mm(p, doa, TN)
                    if gqa:
                        dka = dka * w + pltpu.roll(dka, HD, axis=1) * (1.0 - w)
                        dva = dva * w + pltpu.roll(dva, HD, axis=1) * (1.0 - w)
                    dk_sum, dv_sum = dk_sum + dka, dv_sum + dva
                    if has_sink:
                        sk = s_ref[...][:, HD * a:HD * a + 1]
                        dsink = dsink - jnp.sum(jnp.exp(sk - lse_a) * delta, axis=0, keepdims=True) * msk
                dq_ref[rq, :] += dq_pair
                dk_ref[rp, :] += dk_sum[:BLK]
                dk_ref[rq, :] += dk_sum[BLK:]
                dv_ref[rp, :] += dv_sum[:BLK]
                dv_ref[rq, :] += dv_sum[BLK:]
                return dsink
            dsink = lax.fori_loop(0, SEQ // BLK, step, dsink)
        if has_sink:
            ds_ref[0] = jnp.broadcast_to(dsink, (8, LANES))

    col = lambda blk: pl.BlockSpec((SEQ, LANES), lambda p, blk=blk: (0, blk + p))
    kv = (lambda blk: pl.BlockSpec((SEQ, LANES), lambda p, blk=blk: (0, blk))) if gqa else col
    pair = pl.BlockSpec((SEQ, LANES), lambda p: (0, p))
    in_specs = [col(qblk), kv(kblk), kv(vblk), pair, pair, pair]
    args = [proj, proj, proj, do, o, lse]
    kvw = LANES if gqa else 512
    kv_out = pl.BlockSpec((SEQ, LANES), lambda p: (0, 0)) if gqa else pair
    out_shape = [_sds((SEQ, 512)), _sds((SEQ, kvw)), _sds((SEQ, kvw))]
    out_specs = [pair, kv_out, kv_out]
    if has_sink:
        in_specs.append(pl.BlockSpec((1, LANES), lambda p: (0, p)))
        args.append(sink_x)
        out_shape.append(_sds((4, 8, LANES)))
        out_specs.append(pl.BlockSpec((1, 8, LANES), lambda p: (p, 0, 0)))
    return pl.pallas_call(body, name=name, out_shape=tuple(out_shape), grid=(4,), in_specs=in_specs,
                          out_specs=tuple(out_specs), compiler_params=_cp(56))(*args)


def _shift_down(v, k):
    row = lax.broadcasted_iota(jnp.int32, v.shape, 0)
    return jnp.where(row >= k, pltpu.roll(v, k, axis=0), 0.0)


def _shift_up(v, k):
    n = v.shape[0]
    row = lax.broadcasted_iota(jnp.int32, v.shape, 0)
    return jnp.where(row < n - k, pltpu.roll(v, n - k, axis=0), 0.0)


def _conv_pre(x, w_ref, b_ref):
    u = b_ref[...] + x * w_ref[3:4, :]
    for k in range(1, 4):
        u = u + _shift_down(x, k) * w_ref[3 - k:4 - k, :]
    return u


def _conv_fwd(proj, w, b, name):
    def body(x_ref, w_ref, b_ref, o_ref):
        o_ref[...] = _silu(_conv_pre(x_ref[...], w_ref, b_ref))

    nblk = CONV_CH // LANES
    return pl.pallas_call(body, name=name, out_shape=_sds((SEQ, CONV_CH)), grid=(nblk,),
                          in_specs=[pl.BlockSpec((SEQ, LANES), lambda j: (0, XBC // LANES + j)),
                                    pl.BlockSpec((4, LANES), lambda j: (0, j)), pl.BlockSpec((1, LANES), lambda j: (0, j))],
                          out_specs=pl.BlockSpec((SEQ, LANES), lambda j: (0, j)), compiler_params=_cp())(proj, w, b)


def _conv_bwd(proj, dact, w, b, name):
    def body(x_ref, da_ref, w_ref, b_ref, dx_ref, dw_ref, db_ref):
        x = x_ref[...]
        du = da_ref[...] * _dsilu(_conv_pre(x, w_ref, b_ref))
        dx = du * w_ref[3:4, :]
        for k in range(1, 4):
            dx = dx + _shift_up(du, k) * w_ref[3 - k:4 - k, :]
        dx_ref[...] = dx
        db_ref[...] = jnp.sum(du, axis=0, keepdims=True)
        dw_ref[3:4, :] = jnp.sum(du * x, axis=0, keepdims=True)
        for k in range(1, 4):
            dw_ref[3 - k:4 - k, :] = jnp.sum(du * _shift_down(x, k), axis=0, keepdims=True)

    nblk = CONV_CH // LANES
    blk = pl.BlockSpec((SEQ, LANES), lambda j: (0, j))
    wspec, bspec = pl.BlockSpec((4, LANES), lambda j: (0, j)), pl.BlockSpec((1, LANES), lambda j: (0, j))
    return pl.pallas_call(body, name=name, out_shape=(_sds((SEQ, CONV_CH)), _sds((4, CONV_CH)), _sds((1, CONV_CH))), grid=(nblk,),
                          in_specs=[pl.BlockSpec((SEQ, LANES), lambda j: (0, XBC // LANES + j)), blk, wspec, bspec],
                          out_specs=(blk, wspec, bspec), compiler_params=_cp())(proj, dact, w, b)


def _ssd_chunk(xs, bm, cm, dtr, z, hs, alx, al16, dtb, dskx, nw):
    m0, m1 = _half_masks()
    row = lax.broadcasted_iota(jnp.int32, (BLK, BLK), 0)
    col = lax.broadcasted_iota(jnp.int32, (BLK, BLK), 1)
    causal = row >= col
    tril = causal.astype(F32)
    lane = lax.broadcasted_iota(jnp.int32, (1, LANES), 1)
    sub = lax.broadcasted_iota(jnp.int32, (BLK, 1), 0)
    last_row = (sub == BLK - 1).astype(F32)
    dt = jnp.where(lane < 16, _softplus(dtr + dtb), 0.0)
    a16 = -jnp.exp(al16)
    acum = jnp.dot(tril, dt * a16, precision=HI, preferred_element_type=F32)
    acum_t = acum.T
    gmat = [_mm(cm[g], bm[g], NT) for g in range(2)]
    ys, hn = [], []
    for p in range(8):
        g = p // 4
        expand = ((col >> 6) + 2 * p == row).astype(F32)
        dt_x = jnp.dot(dt, expand, precision=HI, preferred_element_type=F32)
        a_x = -jnp.exp(alx[p])
        ac_x = jnp.dot(tril, dt_x * a_x, precision=HI, preferred_element_type=F32)
        a_end = jnp.sum(ac_x * last_row, axis=0, keepdims=True)
        xdt = xs[p] * dt_x
        y = _mm(cm[g], hs[p]) * jnp.exp(ac_x)
        for a, msk in enumerate((m0, m1)):
            h = 2 * p + a
            col_h = jnp.sum(acum * (lane == h).astype(F32), axis=1, keepdims=True)
            row_h = jnp.sum(acum_t * (sub == h).astype(F32), axis=0, keepdims=True)
            decay = jnp.exp(jnp.where(causal, col_h - row_h, NEG))
            y = y + _mm(gmat[g] * decay, xdt * msk)
        st = _mm(bm[g], xdt * jnp.exp(a_end - ac_x), TN)
        hn.append(hs[p] * jnp.exp(a_end) + st)
        y = y + dskx[p] * xs[p]
        ys.append(y * _silu(z[p]))
    out = []
    for g in range(2):
        ms = sum(jnp.sum(ys[p] * ys[p], axis=1, keepdims=True) for p in range(4 * g, 4 * g + 4)) * (1.0 / 512)
        rstd = lax.rsqrt(ms + EPS)
        out += [ys[p] * rstd * nw[p] for p in range(4 * g, 4 * g + 4)]
    return out, hn


def _tiles(ref, n, off=0):
    return [ref[:, off + LANES * p:off + LANES * (p + 1)] for p in range(n)]


def _ssd_load(xbc_ref, z_ref, dt_ref, alx_ref, al16_ref, dtb_ref, dsk_ref, nw_ref):
    return (_tiles(xbc_ref, 8), _tiles(xbc_ref, 2, 1024), _tiles(xbc_ref, 2, 1280), dt_ref[...], _tiles(z_ref, 8)), \
           (_tiles(alx_ref, 8), al16_ref[...], dtb_ref[...], _tiles(dsk_ref, 8), _tiles(nw_ref, 8))


_NCH = SEQ // BLK


def _ssd_param_specs():
    return [_full((1, 1024)), _full((1, LANES)), _full((1, LANES)), _full((1, 1024)), _full((1, 1024))]


def _ssd_fwd(xbc_act, proj, alx, al16, dtb, dskx, nw, name):
    def body(xbc_ref, z_ref, dt_ref, alx_ref, al16_ref, dtb_ref, dsk_ref, nw_ref, y_ref, hin_ref, h_scr):
        @pl.when(pl.program_id(0) == 0)
        def _():
            h_scr[...] = jnp.zeros_like(h_scr)
        acts, params = _ssd_load(xbc_ref, z_ref, dt_ref, alx_ref, al16_ref, dtb_ref, dsk_ref, nw_ref)
        hs = _tiles(h_scr, 8)
        hin_ref[0] = h_scr[...]
        ys, hn = _ssd_chunk(*acts, hs, *params)
        for p in range(8):
            y_ref[:, LANES * p:LANES * (p + 1)] = ys[p]
            h_scr[:, LANES * p:LANES * (p + 1)] = hn[p]

    return pl.pallas_call(
        body, name=name, out_shape=(_sds((SEQ, 1024)), _sds((_NCH, BLK, 1024))), grid=(_NCH,),
        in_specs=[pl.BlockSpec((BLK, CONV_CH), lambda c: (c, 0)), pl.BlockSpec((BLK, 1024), lambda c: (c, ZB // 1024)),
                  pl.BlockSpec((BLK, LANES), lambda c: (c, DTC // LANES))] + _ssd_param_specs(),
        out_specs=(pl.BlockSpec((BLK, 1024), lambda c: (c, 0)), pl.BlockSpec((1, BLK, 1024), lambda c: (c, 0, 0))),
        scratch_shapes=[pltpu.VMEM((BLK, 1024), F32)], compiler_params=_cp())(xbc_act, proj, proj, alx, al16, dtb, dskx, nw)


def _ssd_bwd(xbc_act, proj, hin, dyb, alx, al16, dtb, dskx, nw, name):
    def body(xbc_ref, z_ref, dt_ref, hin_ref, dy_ref, alx_ref, al16_ref, dtb_ref, dsk_ref, nw_ref,
             dxbc_ref, dz_ref, ddt_ref, dalx_ref, dal16_ref, ddtb_ref, ddsk_ref, dnw_ref, dh_scr):
        @pl.when(pl.program_id(0) == 0)
        def _():
            dh_scr[...] = jnp.zeros_like(dh_scr)
            for r in (dalx_ref, dal16_ref, ddtb_ref, ddsk_ref, dnw_ref):
                r[...] = jnp.zeros_like(r)
        acts, params = _ssd_load(xbc_ref, z_ref, dt_ref, alx_ref, al16_ref, dtb_ref, dsk_ref, nw_ref)
        hs = [hin_ref[0, :, LANES * p:LANES * (p + 1)] for p in range(8)]
        _, vjp = jax.vjp(lambda a, h, q: _ssd_chunk(*a, h, *q), acts, hs, params)
        (dxs, dbm, dcm, ddt, dz), dhs, (dalx, dal16, ddtb, ddsk, dnw) = vjp((_tiles(dy_ref, 8), _tiles(dh_scr, 8)))
        for p in range(8):
            cols = slice(LANES * p, LANES * (p + 1))
            dxbc_ref[:, cols] = dxs[p]
            dz_ref[:, cols] = dz[p]
            dh_scr[:, cols] = dhs[p]
            dalx_ref[:, cols] += dalx[p]
            ddsk_ref[:, cols] += ddsk[p]
            dnw_ref[:, cols] += dnw[p]
        for g in range(2):
            dxbc_ref[:, 1024 + LANES * g:1024 + LANES * (g + 1)] = dbm[g]
            dxbc_ref[:, 1280 + LANES * g:1280 + LANES * (g + 1)] = dcm[g]
        ddt_ref[...] = ddt
        dal16_ref[...] += dal16
        ddtb_ref[...] += ddtb

    rev = lambda c: _NCH - 1 - c
    return pl.pallas_call(
        body, name=name,
        out_shape=(_sds((SEQ, CONV_CH)), _sds((SEQ, 1024)), _sds((SEQ, LANES)),
                   _sds((1, 1024)), _sds((1, LANES)), _sds((1, LANES)), _sds((1, 1024)), _sds((1, 1024))),
        grid=(_NCH,),
        in_specs=[pl.BlockSpec((BLK, CONV_CH), lambda c: (rev(c), 0)), pl.BlockSpec((BLK, 1024), lambda c: (rev(c), ZB // 1024)),
                  pl.BlockSpec((BLK, LANES), lambda c: (rev(c), DTC // LANES)), pl.BlockSpec((1, BLK, 1024), lambda c: (rev(c), 0, 0)),
                  pl.BlockSpec((BLK, 1024), lambda c: (rev(c), 0))] + _ssd_param_specs(),
        out_specs=(pl.BlockSpec((BLK, CONV_CH), lambda c: (rev(c), 0)), pl.BlockSpec((BLK, 1024), lambda c: (rev(c), 0)),
                   pl.BlockSpec((BLK, LANES), lambda c: (rev(c), 0)),
                   _full((1, 1024)), _full((1, LANES)), _full((1, LANES)), _full((1, 1024)), _full((1, 1024))),
        scratch_shapes=[pltpu.VMEM((BLK, 1024), F32)], compiler_params=_cp())(xbc_act, proj, proj, hin, dyb, alx, al16, dtb, dskx, nw)


def _rstd(v):
    return lax.rsqrt(jnp.mean(v * v, axis=1, keepdims=True) + EPS)


def _rms_bwd(dn, n, rstd):
    return rstd * (dn - n * jnp.mean(dn * n, axis=1, keepdims=True))


_VEC = _full((1, D))
_ROW = pl.BlockSpec((TM, D), lambda i, *_: (i, 0))


def _proj_fwd(x, pre_w, scale, shift, w, name):
    tn = 1024

    def body(x_ref, pw_ref, sc_ref, sh_ref, w_ref, o_ref, h_ref):
        @pl.when(pl.program_id(1) == 0)
        def _():
            xv = x_ref[...]
            h = (xv * _rstd(xv) * pw_ref[...]) * (1.0 + sc_ref[...]) + sh_ref[...]
            h_ref[...] = h.astype(h_ref.dtype)
        o_ref[...] = jnp.dot(h_ref[...], w_ref[...].astype(MXU), preferred_element_type=F32)

    return pl.pallas_call(body, name=name, out_shape=(_sds((SEQ, NP)), _sds((SEQ, D), MXU)), grid=(SEQ // TM, NP // tn),
                          in_specs=[_ROW, _VEC, _VEC, _VEC, pl.BlockSpec((D, tn), lambda i, j: (0, j))],
                          out_specs=(pl.BlockSpec((TM, tn), lambda i, j: (i, j)), _ROW), compiler_params=_cp())(x, pre_w, scale, shift, w)


def _out_fwd(ya, yb, yc, w, x, gate, post_w, name):
    def body(ya_ref, yb_ref, yc_ref, w_ref, x_ref, g_ref, pw_ref, xn_ref, y_ref):
        y = _mm(ya_ref[...], w_ref[0:512, :]) + _mm(yb_ref[...], w_ref[512:1536, :]) + _mm(yc_ref[...], w_ref[1536:2048, :])
        y_ref[...] = y
        xn_ref[...] = x_ref[...] + g_ref[...] * (y * _rstd(y) * pw_ref[...])

    half = pl.BlockSpec((TM, 512), lambda i: (i, 0))
    return pl.pallas_call(body, name=name, out_shape=(_sds((SEQ, D)), _sds((SEQ, D))), grid=(SEQ // TM,),
                          in_specs=[half, _ROW, half, _full((2048, D)), _ROW, _VEC, _VEC],
                          out_specs=(_ROW, _ROW), compiler_params=_cp())(ya, yb, yc, w, x, gate, post_w)


def _post_bwd(dxo, y, gate, post_w, name):
    def body(dx_ref, y_ref, g_ref, pw_ref, dy_ref, dg_ref, dpw_ref):
        @pl.when(pl.program_id(0) == 0)
        def _():
            dg_ref[...] = jnp.zeros_like(dg_ref)
            dpw_ref[...] = jnp.zeros_like(dpw_ref)
        dx, y = dx_ref[...], y_ref[...]
        rstd = _rstd(y)
        n = y * rstd
        dg_ref[...] += jnp.sum(dx * (n * pw_ref[...]), axis=0, keepdims=True)
        dr = dx * g_ref[...]
        dpw_ref[...] += jnp.sum(dr * n, axis=0, keepdims=True)
        dy_ref[...] = _rms_bwd(dr * pw_ref[...], n, rstd)

    return pl.pallas_call(body, name=name, out_shape=(_sds((SEQ, D)), _sds((1, D)), _sds((1, D))), grid=(SEQ // TM,),
                          in_specs=[_ROW, _ROW, _VEC, _VEC], out_specs=(_ROW, _VEC, _VEC), compiler_params=_cp())(dxo, y, gate, post_w)


def _dymix(dy, w, name):
    def body(dy_ref, w_ref, a_ref, b_ref, c_ref):
        dy = dy_ref[...]
        a_ref[...] = _mm(dy, w_ref[0:512, :], NT)
        b_ref[...] = _mm(dy, w_ref[512:1536, :], NT)
        c_ref[...] = _mm(dy, w_ref[1536:2048, :], NT)

    half = pl.BlockSpec((TM, 512), lambda i: (i, 0))
    return pl.pallas_call(body, name=name, out_shape=(_sds((SEQ, 512)), _sds((SEQ, D)), _sds((SEQ, 512))), grid=(SEQ // TM,),
                          in_specs=[_ROW, _full((2048, D))], out_specs=(half, _ROW, half), compiler_params=_cp())(dy, w)


def _dwout(ya, yb, yc, dy, name):
    def body(ya_ref, yb_ref, yc_ref, dy_ref, o_ref):
        @pl.when(pl.program_id(0) == 0)
        def _():
            o_ref[...] = jnp.zeros_like(o_ref)
        dy = dy_ref[...]
        o_ref[0:512, :] += _mm(ya_ref[...], dy, TN)
        o_ref[512:1536, :] += _mm(yb_ref[...], dy, TN)
        o_ref[1536:2048, :] += _mm(yc_ref[...], dy, TN)

    half = pl.BlockSpec((TM, 512), lambda i: (i, 0))
    return pl.pallas_call(body, name=name, out_shape=_sds((2048, D)), grid=(SEQ // TM,),
                          in_specs=[half, _ROW, half, _ROW], out_specs=_full((2048, D)), compiler_params=_cp())(ya, yb, yc, dy)


_TK = 1536


def _dwin(h, dproj, name):
    def body(h_ref, dp_ref, o_ref):
        @pl.when(pl.program_id(1) == 0)
        def _():
            o_ref[...] = jnp.zeros_like(o_ref)
        o_ref[...] += _mm(h_ref[...], dp_ref[...], TN)

    return pl.pallas_call(body, name=name, out_shape=_sds((D, NP)), grid=(NP // _TK, SEQ // TM),
                          in_specs=[pl.BlockSpec((TM, D), lambda j, k: (k, 0)), pl.BlockSpec((TM, _TK), lambda j, k: (k, j))],
                          out_specs=pl.BlockSpec((D, _TK), lambda j, k: (0, j)), compiler_params=_cp())(h, dproj)


def _dh_bwd(dproj, w, x, pre_w, scale, dxo, name):
    nk = NP // _TK

    def body(dp_ref, w_ref, x_ref, pw_ref, sc_ref, dxo_ref, dx_ref, dsh_ref, dsc_ref, dpw_ref, acc):
        i, k = pl.program_id(0), pl.program_id(1)

        @pl.when((i == 0) & (k == 0))
        def _():
            for r in (dsh_ref, dsc_ref, dpw_ref):
                r[...] = jnp.zeros_like(r)

        @pl.when(k == 0)
        def _():
            acc[...] = jnp.zeros_like(acc)
        acc[...] += _mm(dp_ref[...], w_ref[...], NT)

        @pl.when(k == nk - 1)
        def _():
            dh, xv = acc[...], x_ref[...]
            rstd = _rstd(xv)
            n = xv * rstd
            dsh_ref[...] += jnp.sum(dh, axis=0, keepdims=True)
            dsc_ref[...] += jnp.sum(dh * (n * pw_ref[...]), axis=0, keepdims=True)
            dhn = dh * (1.0 + sc_ref[...])
            dpw_ref[...] += jnp.sum(dhn * n, axis=0, keepdims=True)
            dx_ref[...] = _rms_bwd(dhn * pw_ref[...], n, rstd) + dxo_ref[...]

    return pl.pallas_call(body, name=name, out_shape=(_sds((SEQ, D)), _sds((1, D)), _sds((1, D)), _sds((1, D))),
                          grid=(SEQ // TM, nk),
                          in_specs=[pl.BlockSpec((TM, _TK), lambda i, k: (i, k)), pl.BlockSpec((D, _TK), lambda i, k: (0, k)),
                                    _ROW, _VEC, _VEC, _ROW],
                          out_specs=(_ROW, _VEC, _VEC, _VEC), scratch_shapes=[pltpu.VMEM((TM, D), F32)],
                          compiler_params=_cp())(dproj, w, x, pre_w, scale, dxo)


def _loss_bwd(xf, tgt, name):
    def body(x_ref, t_ref, dx_ref, l_ref):
        @pl.when(pl.program_id(0) == 0)
        def _():
            l_ref[...] = jnp.zeros_like(l_ref)
        e = x_ref[...] - t_ref[...]
        dx_ref[...] = e * (1.0 / D)
        l_ref[...] += 0.5 * jnp.sum(jnp.mean(e * e, axis=1, keepdims=True), axis=0, keepdims=True)

    return pl.pallas_call(body, name=name, out_shape=(_sds((SEQ, D)), _sds((8, LANES))), grid=(SEQ // TM,),
                          in_specs=[_ROW, _ROW], out_specs=(_ROW, _full((8, LANES))), compiler_params=_cp())(xf, tgt)


def _mod_part(c_all, ada_w, ada_b, name):
    def body(c_ref, w_ref, b_ref, o_ref):
        o_ref[0] = _mm(_silu(c_ref[...]), w_ref[0]) + b_ref[0]

    return pl.pallas_call(body, name=name, out_shape=_sds((DEPTH, 8, 768)), grid=(DEPTH,),
                          in_specs=[_full((8, D)), pl.BlockSpec((1, D, 768), lambda i: (i, 0, 0)), pl.BlockSpec((1, 1, 768), lambda i: (i, 0, 0))],
                          out_specs=pl.BlockSpec((1, 8, 768), lambda i: (i, 0, 0)), compiler_params=_cp())(c_all, ada_w, ada_b)


def _ada_grad(c_t, dmod, name):
    def body(c_ref, d_ref, o_ref):
        ca = _silu(c_ref[...])
        dm = d_ref[0]
        acc = ca[:, 0:1] * dm[0:1, :]
        for s in range(1, 8):
            acc = acc + ca[:, s:s + 1] * dm[s:s + 1, :]
        o_ref[0] = acc

    return pl.pallas_call(body, name=name, out_shape=_sds((DEPTH, D, 768)), grid=(DEPTH,),
                          in_specs=[_full((D, LANES)), pl.BlockSpec((1, 8, 768), lambda i: (i, 0, 0))],
                          out_specs=pl.BlockSpec((1, D, 768), lambda i: (i, 0, 0)), compiler_params=_cp())(c_t, dmod)


def _pack(parts):
    flat = []
    for p in parts:
        f = p.reshape(-1)
        flat.append(jnp.pad(f, (0, (-f.size) % LANES)))
    v = jnp.concatenate(flat)
    return jnp.pad(v, (0, (-v.size) % (8 * LANES))).reshape(-1, LANES)


def _unpack(v, shapes):
    v = v.reshape(-1)
    out, off = [], 0
    for s in shapes:
        n = math.prod(s)
        out.append(v[off:off + n].reshape(s))
        off += n + (-n) % LANES
    return out


_GIVEN_DT, _GIVEN_C = 4608, 4624


def _pad_cols(w):
    return jnp.concatenate([w[..., :_GIVEN_DT], w[..., _GIVEN_C:], w[..., _GIVEN_DT:_GIVEN_C],
                            jnp.zeros(w.shape[:-1] + (NP - IN_COLS,), w.dtype)], axis=-1)


def _unpad_cols(w):
    return jnp.concatenate([w[..., :_GIVEN_DT], w[..., DTC:DTC + 16], w[..., _GIVEN_DT:DTC]], axis=-1)


def _pad_lanes(v):
    return jnp.pad(v, (0, LANES - v.shape[0])).reshape(1, LANES)


def _local_step(x2, tgt, mod, w_p, w_o, pre_w, post_w, conv_w, conv_b, dt_bias, a_log, d_skip, nw, sinks):
    saved = []
    xcur = x2
    for i in range(DEPTH):
        shift, scale, gate = mod[i:i + 1, :D], mod[i:i + 1, D:2 * D], mod[i:i + 1, 2 * D:]
        pw, qw = pre_w[i:i + 1], post_w[i:i + 1]
        proj, h = _proj_fwd(xcur, pw, scale, shift, w_p[i], "proj_fwd")
        o_a, lse_a = _attn_fwd(proj, QA // LANES, KA // LANES, VA // LANES, DILS, False, None, "attn_a_fwd")
        ya = _gate_fwd(o_a, proj, ZA // 512, "gate_a_fwd")
        sink_x = jnp.repeat(sinks[i], HD).reshape(1, 512)
        o_c, lse_c = _attn_fwd(proj, QC // LANES, KC // LANES, VC // LANES, (1,), True, sink_x, "attn_c_fwd")
        yc = _gate_fwd(o_c, proj, ZC // 512, "gate_c_fwd")
        cw, cb = conv_w[i], conv_b[i:i + 1]
        xbc_act = _conv_fwd(proj, cw, cb, "conv_fwd")
        ssd_p = (jnp.repeat(a_log[i], HD).reshape(1, 1024), _pad_lanes(a_log[i]), _pad_lanes(dt_bias[i]),
                 jnp.repeat(d_skip[i], HD).reshape(1, 1024), nw[i:i + 1])
        yb, hin = _ssd_fwd(xbc_act, proj, *ssd_p, "ssd_fwd")
        xnew, y = _out_fwd(ya, yb, yc, w_o[i], xcur, gate, qw, "out_fwd")
        saved.append((xcur, scale, gate, pw, qw, proj, h, o_a, lse_a, ya, sink_x, o_c, lse_c, yc, cw, cb, xbc_act, ssd_p, yb, hin, y))
        xcur = xnew
    dx, ltile = _loss_bwd(xcur, tgt, "loss")
    dwi, dwo, dmod, small = [None] * DEPTH, [None] * DEPTH, [None] * DEPTH, [None] * DEPTH
    for i in reversed(range(DEPTH)):
        xin, scale, gate, pw, qw, proj, h, o_a, lse_a, ya, sink_x, o_c, lse_c, yc, cw, cb, xbc_act, ssd_p, yb, hin, y = saved[i]
        dy, dgate, dpost = _post_bwd(dx, y, gate, qw, "post_bwd")
        dya, dyb, dyc = _dymix(dy, w_o[i], "dymix")
        dwo[i] = _dwout(ya, yb, yc, dy, "dwout")
        do_a, dz_a = _gate_bwd(dya, o_a, proj, ZA // 512, "gate_a_bwd")
        dq_a, dk_a, dv_a = _attn_bwd(proj, QA // LANES, KA // LANES, VA // LANES, do_a, o_a, lse_a, DILS, False, None, "attn_a_bwd")
        do_c, dz_c = _gate_bwd(dyc, o_c, proj, ZC // 512, "gate_c_bwd")
        dq_c, dk_c, dv_c, dsk = _attn_bwd(proj, QC // LANES, KC // LANES, VC // LANES, do_c, o_c, lse_c, (1,), True, sink_x, "attn_c_bwd")
        dxbc_act, dz_b, ddt, dalx, dal16, ddtb, ddsk, dnw = _ssd_bwd(xbc_act, proj, hin, dyb, *ssd_p, "ssd_bwd")
        dxbc, dcw, dcb = _conv_bwd(proj, dxbc_act, cw, cb, "conv_bwd")
        dproj = jnp.concatenate([dq_a, dk_a, dv_a, dz_a, dz_b, dxbc, dq_c, dz_c, dk_c, dv_c, ddt,
                                 jnp.zeros((SEQ, NP - DTC - LANES), F32)], axis=1)
        dwi[i] = _dwin(h, dproj, "dwin")
        dx, dshift, dscale, dpre = _dh_bwd(dproj, w_p[i], xin, pw, scale, dx, "dh_bwd")
        dmod[i] = jnp.concatenate([dshift, dscale, dgate], axis=1)
        dal = dalx.reshape(16, HD).sum(axis=1) + dal16[0, :16]
        small[i] = (dpre, dpost, dcw, dcb, ddtb[0, :16], dal, ddsk.reshape(16, HD).sum(axis=1), dnw, dsk[:, 0, ::HD].reshape(8))
    return ltile, dx, jnp.stack(dwi), jnp.stack(dwo), jnp.concatenate(dmod, axis=0), small


_SMALL = ((1, D), (1, D), (4, CONV_CH), (1, CONV_CH), (16,), (16,), (16,), (1, D), (8,))


def kernel(x, c, ada_w, ada_b, pre_norm_w, post_norm_w, w_in, conv_w, conv_b, dt_bias, a_log, d_skip, ssm_norm_w, sinks, w_out, loss_target, m_ada_w, m_ada_b, m_pre_norm_w, m_post_norm_w, m_w_in, m_conv_w, m_conv_b, m_dt_bias, m_a_log, m_d_skip, m_ssm_norm_w, m_sinks, m_w_out, v_ada_w, v_ada_b, v_pre_norm_w, v_post_norm_w, v_w_in, v_conv_w, v_conv_b, v_dt_bias, v_a_log, v_d_skip, v_ssm_norm_w, v_sinks, v_w_out):
    xi, yi, ci = lax.axis_index("x"), lax.axis_index("y"), lax.axis_index("c")
    chip = 2 * xi + yi
    me = 2 * chip + ci

    w_in_b = _cast_bf16(w_in.reshape(DEPTH * D, SHARD_IN), 512, "cast_w_in").reshape(DEPTH, D, SHARD_IN)
    w_out_b = _cast_bf16(w_out.reshape(DEPTH * 512, D), 512, "cast_w_out").reshape(DEPTH, 512, D)
    g_in, g_out = _exchange4([w_in_b, w_out_b], False, "gather_weights")
    w_p = _pad_cols(jnp.concatenate([g_in[k] for k in range(4)], axis=-1))
    w_o = jnp.concatenate([g_out[k] for k in range(4)], axis=1)

    g0 = _allgather8(_pack([c, conv_w]), "gather_c")
    c_all = g0[:, :8, :].reshape(8, D)
    conv_w_full = jnp.concatenate([g0[2 * k, 8:56, :].reshape(DEPTH, 4, CONV_CH // 4) for k in range(4)], axis=-1)

    ada_b_mine = lax.dynamic_slice_in_dim(ada_b, 768 * chip, 768, axis=1).reshape(DEPTH, 1, 768)
    gm = _allgather8(_mod_part(c_all, ada_w, ada_b_mine, "mod_part").reshape(DEPTH * 8, 768), "gather_mod")
    gm = gm.reshape(4, 2, DEPTH, 8, 768)[:, 0]
    mod = lax.dynamic_index_in_dim(gm, me, axis=2, keepdims=False).transpose(1, 0, 2).reshape(DEPTH, 3 * D)

    ltile, dx, dwi_p, dwo, dmod, small = _local_step(x[0], loss_target[0], mod, w_p, w_o, pre_norm_w, post_norm_w, conv_w_full,
                                                     conv_b, dt_bias, a_log, d_skip, ssm_norm_w, sinks)

    packed = _pack([dmod] + [g for layer in small for g in layer] + [ltile[0]])
    gs = _allgather8(packed, "gather_small")
    tot = _sum_blocks(gs, packed.shape[0], "sum_small")
    parts = _unpack(tot, [(DEPTH, 3 * D)] + list(_SMALL) * DEPTH + [(LANES,)])
    g_ada_b, loss = parts[0], parts[-1][0]
    per_layer = [parts[1 + len(_SMALL) * i:1 + len(_SMALL) * (i + 1)] for i in range(DEPTH)]
    g_pre, g_post, g_cw, g_cb, g_dtb, g_al, g_dsk, g_nw, g_sk = [jnp.stack([per_layer[i][j] for i in range(DEPTH)]) for j in range(len(_SMALL))]
    g_pre, g_post, g_cb, g_nw = g_pre[:, 0], g_post[:, 0], g_cb[:, 0], g_nw[:, 0]
    g_cw = lax.dynamic_slice_in_dim(g_cw, (CONV_CH // 4) * chip, CONV_CH // 4, axis=2)

    dmod_all = gs[:, :(DEPTH * 3 * D) // LANES, :].reshape(8, DEPTH, 3 * D).transpose(1, 0, 2)
    dmod_mine = lax.dynamic_slice_in_dim(dmod_all, 768 * chip, 768, axis=2)
    c_t = jnp.pad(c_all.T, ((0, 0), (0, LANES - 8)))
    g_ada_w = _ada_grad(c_t, dmod_mine, "ada_grad")

    dwi = _unpad_cols(dwi_p)
    blk_in = jnp.stack([dwi[..., SHARD_IN * k:SHARD_IN * (k + 1)] for k in range(4)])
    blk_out = jnp.stack([dwo[:, 512 * k:512 * (k + 1), :] for k in range(4)])
    r_in, r_out = _exchange4([blk_in, blk_out], True, "scatter_grads")
    p_in = _sum_blocks(r_in.reshape(4, DEPTH * D, SHARD_IN), 256, "sum_w_in")
    p_out = _sum_blocks(r_out.reshape(4, DEPTH * 512, D), 512, "sum_w_out")
    s_in, s_out = _sibling_swap([p_in, p_out], "swap_partials")

    res = {}
    res["ada_w"] = [a.reshape(DEPTH, D, 768) for a in
                    _adamw(ada_w.reshape(DEPTH * D, 768), [g_ada_w.reshape(DEPTH * D, 768)], m_ada_w.reshape(DEPTH * D, 768),
                           v_ada_w.reshape(DEPTH * D, 768), 512, "adamw_ada_w")]
    res["w_in"] = [a.reshape(DEPTH, D, SHARD_IN) for a in
                   _adamw(w_in.reshape(DEPTH * D, SHARD_IN), [p_in, s_in], m_w_in.reshape(DEPTH * D, SHARD_IN),
                          v_w_in.reshape(DEPTH * D, SHARD_IN), 256, "adamw_w_in")]
    res["w_out"] = [a.reshape(DEPTH, 512, D) for a in
                    _adamw(w_out.reshape(DEPTH * 512, D), [p_out, s_out], m_w_out.reshape(DEPTH * 512, D),
                           v_w_out.reshape(DEPTH * 512, D), 512, "adamw_w_out")]
    names = ["ada_b", "pre_norm_w", "post_norm_w", "conv_w", "conv_b", "dt_bias", "a_log", "d_skip", "ssm_norm_w", "sinks"]
    ws = [ada_b, pre_norm_w, post_norm_w, conv_w, conv_b, dt_bias, a_log, d_skip, ssm_norm_w, sinks]
    gsm = [g_ada_b, g_pre, g_post, g_cw, g_cb, g_dtb, g_al, g_dsk, g_nw, g_sk]
    ms = [m_ada_b, m_pre_norm_w, m_post_norm_w, m_conv_w, m_conv_b, m_dt_bias, m_a_log, m_d_skip, m_ssm_norm_w, m_sinks]
    vs = [v_ada_b, v_pre_norm_w, v_post_norm_w, v_conv_w, v_conv_b, v_dt_bias, v_a_log, v_d_skip, v_ssm_norm_w, v_sinks]
    pw_, pg_, pm_, pv_ = _pack(ws), _pack(gsm), _pack(ms), _pack(vs)
    small_out = _adamw(pw_, [pg_], pm_, pv_, pw_.shape[0], "adamw_small")
    shapes = [w.shape for w in ws]
    for kind in range(4):
        for nm, a in zip(names, _unpack(small_out[kind], shapes)):
            res.setdefault(nm, [None] * 4)[kind] = a
    order = ["ada_w", "ada_b", "pre_norm_w", "post_norm_w", "w_in", "conv_w", "conv_b", "dt_bias", "a_log", "d_skip", "ssm_norm_w", "sinks", "w_out"]
    return (loss, dx[None], *[res[n][0] for n in order], *[res[n][1] for n in order], *[res[n][2] for n in order], *[res[n][3] for n in order])
```

```python
import math

import jax
import jax.numpy as jnp
from jax import lax
from jax.experimental import pallas as pl
from jax.experimental.pallas import tpu as pltpu

F32 = jnp.float32
MXU = jnp.bfloat16
HI = lax.Precision.HIGHEST
MESH = pl.DeviceIdType.MESH

SEQ = 4096
D = 1024
DEPTH = 4
HD = 64
LANES = 128
BLK = 128
DILS = (1, 4, 16)
NEG = -1e30
EPS = 1e-6
MIB = 1024 * 1024

NP = 6144
QA, KA, VA, ZA = 0, 512, 1024, 1536
ZB, XBC = 2048, 3072
QC, ZC, KC, VC = 4608, 5120, 5632, 5760
DTC = 5888
IN_COLS = 5904
SHARD_IN = IN_COLS // 4
CONV_CH = 1536
TM = 512

ADAM_LR, ADAM_B1, ADAM_B2, ADAM_EPS, ADAM_WD, ADAM_STEP = 0.001, 0.9, 0.999, 1e-08, 0.01, 10

NT = (((1,), (1,)), ((), ()))
TN = (((0,), (0,)), ((), ()))


def _cp(vmem_mib=48):
    return pltpu.CompilerParams(vmem_limit_bytes=vmem_mib * MIB)


def _sds(shape, dtype=F32):
    return jax.ShapeDtypeStruct(shape, dtype)


def _full(shape):
    n = len(shape)
    return pl.BlockSpec(shape, lambda *_: (0,) * n)


def _mm(a, b, dims=None):
    if dims is None:
        return jnp.dot(a.astype(MXU), b.astype(MXU), preferred_element_type=F32)
    return lax.dot_general(a.astype(MXU), b.astype(MXU), dims, preferred_element_type=F32)


def _sigmoid(x):
    return 1.0 / (1.0 + jnp.exp(-x))


def _silu(x):
    return x * _sigmoid(x)


def _dsilu(x):
    s = _sigmoid(x)
    return s * (1.0 + x * (1.0 - s))


def _softplus(x):
    ax = jnp.where(x >= 0, x, -x)
    return jnp.maximum(x, 0.0) + jnp.log1p(jnp.exp(-ax))


def _half_masks():
    lane = lax.broadcasted_iota(jnp.int32, (1, LANES), 1)
    m0 = (lane < HD).astype(F32)
    return m0, 1.0 - m0


def _allgather8(v, name):
    r, cc = v.shape

    def body(v_ref, out_ref, send_sems, recv_sems):
        x, y, c = lax.axis_index("x"), lax.axis_index("y"), lax.axis_index("c")
        me = 4 * x + 2 * y + c
        out_ref[me] = v_ref[...]
        peers = []
        for k in range(1, 8):
            px = 1 - x if k & 4 else x
            py = 1 - y if k & 2 else y
            pc = 1 - c if k & 1 else c
            peers.append((px, py, pc))
        sends = []
        for k, peer in enumerate(peers):
            cp = pltpu.make_async_remote_copy(src_ref=v_ref, dst_ref=out_ref.at[me], send_sem=send_sems.at[k],
                                              recv_sem=recv_sems.at[k], device_id=peer, device_id_type=MESH)
            cp.start()
            sends.append(cp)
        for k, (px, py, pc) in enumerate(peers):
            pltpu.make_async_remote_copy(src_ref=v_ref, dst_ref=out_ref.at[4 * px + 2 * py + pc], send_sem=send_sems.at[k],
                                         recv_sem=recv_sems.at[k], device_id=(px, py, pc), device_id_type=MESH).wait_recv()
        for cp in sends:
            cp.wait_send()

    return pl.pallas_call(
        body, name=name, out_shape=_sds((8, r, cc)),
        in_specs=[pl.BlockSpec(memory_space=pltpu.VMEM)], out_specs=pl.BlockSpec(memory_space=pltpu.VMEM),
        scratch_shapes=[pltpu.SemaphoreType.DMA((7,)), pltpu.SemaphoreType.DMA((7,))],
        compiler_params=_cp(32),
    )(v)


def _exchange4(arrs, scatter, name):
    n = len(arrs)
    outs = tuple(_sds(a.shape if scatter else (4,) + a.shape, a.dtype) for a in arrs)

    def body(*refs):
        ins, outs_, (send_sems, recv_sems, loc_sems) = refs[:n], refs[n:2 * n], refs[2 * n:]
        x, y, c = lax.axis_index("x"), lax.axis_index("y"), lax.axis_index("c")
        mine = 2 * x + y
        chips = [(1 - x, y), (x, 1 - y), (1 - x, 1 - y)]
        local, sent = [], []
        for i in range(n):
            loc = pltpu.make_async_copy(ins[i].at[mine] if scatter else ins[i], outs_[i].at[mine], loc_sems.at[i])
            loc.start()
            local.append(loc)
            for j, (px, py) in enumerate(chips):
                src = ins[i].at[2 * px + py] if scatter else ins[i]
                cp = pltpu.make_async_remote_copy(src_ref=src, dst_ref=outs_[i].at[mine], send_sem=send_sems.at[3 * i + j],
                                                  recv_sem=recv_sems.at[3 * i + j], device_id=(px, py, c), device_id_type=MESH)
                cp.start()
                sent.append(cp)
        for i in range(n):
            for j, (px, py) in enumerate(chips):
                src = ins[i].at[mine] if scatter else ins[i]
                pltpu.make_async_remote_copy(src_ref=src, dst_ref=outs_[i].at[2 * px + py], send_sem=send_sems.at[3 * i + j],
                                             recv_sem=recv_sems.at[3 * i + j], device_id=(px, py, c), device_id_type=MESH).wait_recv()
        for cp in sent:
            cp.wait_send()
        for loc in local:
            loc.wait()

    hbm = pl.BlockSpec(memory_space=pltpu.HBM)
    return pl.pallas_call(
        body, name=name, out_shape=outs, in_specs=[hbm] * n, out_specs=tuple([hbm] * n),
        scratch_shapes=[pltpu.SemaphoreType.DMA((3 * n,)), pltpu.SemaphoreType.DMA((3 * n,)), pltpu.SemaphoreType.DMA((n,))],
    )(*arrs)


def _sibling_swap(arrs, name):
    n = len(arrs)

    def body(*refs):
        ins, outs_, (send_sems, recv_sems) = refs[:n], refs[n:2 * n], refs[2 * n:]
        sib = (lax.axis_index("x"), lax.axis_index("y"), 1 - lax.axis_index("c"))
        cps = [pltpu.make_async_remote_copy(src_ref=ins[i], dst_ref=outs_[i], send_sem=send_sems.at[i], recv_sem=recv_sems.at[i],
                                            device_id=sib, device_id_type=MESH) for i in range(n)]
        for cp in cps:
            cp.start()
        for cp in cps:
            cp.wait_recv()
        for cp in cps:
            cp.wait_send()

    hbm = pl.BlockSpec(memory_space=pltpu.HBM)
    return pl.pallas_call(
        body, name=name, out_shape=tuple(_sds(a.shape, a.dtype) for a in arrs), in_specs=[hbm] * n, out_specs=tuple([hbm] * n),
        scratch_shapes=[pltpu.SemaphoreType.DMA((n,)), pltpu.SemaphoreType.DMA((n,))],
    )(*arrs)


def _cast_bf16(a, rows, name):
    r, cc = a.shape

    def body(a_ref, o_ref):
        o_ref[...] = a_ref[...].astype(jnp.bfloat16)

    return pl.pallas_call(body, name=name, out_shape=_sds((r, cc), jnp.bfloat16), grid=(r // rows,),
                          in_specs=[pl.BlockSpec((rows, cc), lambda i: (i, 0))],
                          out_specs=pl.BlockSpec((rows, cc), lambda i: (i, 0)), compiler_params=_cp())(a)


def _sum_blocks(a, rows, name):
    k, r, cc = a.shape

    def body(a_ref, o_ref):
        acc = a_ref[0].astype(F32)
        for j in range(1, k):
            acc = acc + a_ref[j].astype(F32)
        o_ref[...] = acc

    return pl.pallas_call(body, name=name, out_shape=_sds((r, cc)), grid=(r // rows,),
                          in_specs=[pl.BlockSpec((k, rows, cc), lambda i: (0, i, 0))],
                          out_specs=pl.BlockSpec((rows, cc), lambda i: (i, 0)), compiler_params=_cp())(a)


def _adamw(w, parts, m, v, rows, name):
    r, cc = w.shape
    np_ = len(parts)
    c1 = 1.0 / (1.0 - ADAM_B1 ** ADAM_STEP)
    c2 = 1.0 / (1.0 - ADAM_B2 ** ADAM_STEP)

    def body(*refs):
        w_ref, p_refs, (m_ref, v_ref, g_ref, d_ref, nm_ref, nv_ref) = refs[0], refs[1:1 + np_], refs[1 + np_:]
        g = p_refs[0][...]
        for p_ref in p_refs[1:]:
            g = g + p_ref[...]
        nm = ADAM_B1 * m_ref[...] + (1.0 - ADAM_B1) * g
        nv = ADAM_B2 * v_ref[...] + (1.0 - ADAM_B2) * (g * g)
        g_ref[...] = g
        nm_ref[...] = nm
        nv_ref[...] = nv
        d_ref[...] = -ADAM_LR * ((nm * c1) / (jnp.sqrt(nv * c2) + ADAM_EPS) + ADAM_WD * w_ref[...])

    spec = pl.BlockSpec((rows, cc), lambda i: (i, 0))
    return pl.pallas_call(body, name=name, out_shape=(_sds((r, cc)),) * 4, grid=(r // rows,),
                          in_specs=[spec] * (3 + np_), out_specs=(spec,) * 4, compiler_params=_cp())(w, *parts, m, v)


def _gate_fwd(o, proj, zblk, name):
    def body(o_ref, z_ref, y_ref):
        y_ref[...] = o_ref[...] * _silu(z_ref[...])

    return pl.pallas_call(body, name=name, out_shape=_sds((SEQ, 512)), grid=(SEQ // TM,),
                          in_specs=[pl.BlockSpec((TM, 512), lambda i: (i, 0)), pl.BlockSpec((TM, 512), lambda i: (i, zblk))],
                          out_specs=pl.BlockSpec((TM, 512), lambda i: (i, 0)), compiler_params=_cp())(o, proj)


def _gate_bwd(dy, o, proj, zblk, name):
    def body(dy_ref, o_ref, z_ref, do_ref, dz_ref):
        dy, z = dy_ref[...], z_ref[...]
        do_ref[...] = dy * _silu(z)
        dz_ref[...] = dy * o_ref[...] * _dsilu(z)

    return pl.pallas_call(body, name=name, out_shape=(_sds((SEQ, 512)), _sds((SEQ, 512))), grid=(SEQ // TM,),
                          in_specs=[pl.BlockSpec((TM, 512), lambda i: (i, 0)), pl.BlockSpec((TM, 512), lambda i: (i, 0)),
                                    pl.BlockSpec((TM, 512), lambda i: (i, zblk))],
                          out_specs=(pl.BlockSpec((TM, 512), lambda i: (i, 0)),) * 2, compiler_params=_cp())(dy, o, proj)


def _band_valid(lo):
    qi = lax.broadcasted_iota(jnp.int32, (BLK, 2 * BLK), 0)
    kj = lax.broadcasted_iota(jnp.int32, (BLK, 2 * BLK), 1)
    dist = BLK + qi - kj
    return (dist >= 0) & (dist <= BLK) & (kj >= lo)


def _rows(st, dil):
    if dil == 1:
        return pl.ds(pl.multiple_of(st, BLK), BLK)
    return pl.ds(st, BLK, stride=dil)


def _block_pos(n, dil):
    nb = SEQ // (dil * BLK)
    r, b = n // nb, n % nb
    hp = (b > 0).astype(jnp.int32)
    st = r + dil * BLK * b
    return st, st - dil * BLK * hp, BLK * (1 - hp)


def _attn_fwd(proj, qblk, kblk, vblk, dils, gqa, sink_x, name):
    has_sink = sink_x is not None

    def body(*refs):
        if has_sink:
            q_ref, k_ref, v_ref, s_ref, o_ref, lse_ref, m_scr, z_scr = refs
        else:
            q_ref, k_ref, v_ref, o_ref, lse_ref, m_scr, z_scr = refs
        m0, m1 = _half_masks()
        g = pl.program_id(0) // 2
        o_ref[...] = jnp.zeros_like(o_ref)
        z_scr[...] = jnp.zeros_like(z_scr)
        m_scr[...] = jnp.full_like(m_scr, NEG)
        for dil in dils:
            def step(n, carry, dil=dil):
                st, stp, lo = _block_pos(n, dil)
                rq, rp = _rows(st, dil), _rows(stp, dil)
                valid = _band_valid(lo)
                q = q_ref[rq, :]
                kk = jnp.concatenate([k_ref[rp, :], k_ref[rq, :]], axis=0)
                vv = jnp.concatenate([v_ref[rp, :], v_ref[rq, :]], axis=0)
                if gqa:
                    kr, vr = pltpu.roll(kk, HD, axis=1), pltpu.roll(vv, HD, axis=1)
                m_pair = l_pair = o_pair = 0.0
                for a, msk in enumerate((m0, m1)):
                    if gqa:
                        w = (g == a).astype(F32)
                        ka, va = kk * w + kr * (1.0 - w), vv * w + vr * (1.0 - w)
                    else:
                        ka, va = kk, vv
                    s = _mm(q * msk, ka, NT) * (HD ** -0.5)
                    s = jnp.where(valid, s, NEG)
                    m = jnp.max(s, axis=1, keepdims=True)
                    if has_sink:
                        sk = s_ref[...][:, HD * a:HD * a + 1]
                        m = jnp.maximum(m, sk)
                    p = jnp.exp(s - m)
                    l = jnp.sum(p, axis=1, keepdims=True)
                    if has_sink:
                        l = l + jnp.exp(sk - m)
                    o_pair = o_pair + _mm(p, va) * msk
                    m_pair = m_pair + m * msk
                    l_pair = l_pair + l * msk
                m_old = m_scr[rq, :]
                m_new = jnp.maximum(m_old, m_pair)
                alpha, beta = jnp.exp(m_old - m_new), jnp.exp(m_pair - m_new)
                o_ref[rq, :] = o_ref[rq, :] * alpha + o_pair * beta
                z_scr[rq, :] = z_scr[rq, :] * alpha + l_pair * beta
                m_scr[rq, :] = m_new
                return carry
            lax.fori_loop(0, SEQ // BLK, step, 0, unroll=4)

        def fin(t, carry):
            rt = pl.ds(pl.multiple_of(t * TM, TM), TM)
            z = z_scr[rt, :]
            o_ref[rt, :] = o_ref[rt, :] / z
            lse_ref[rt, :] = m_scr[rt, :] + jnp.log(z)
            return carry
        lax.fori_loop(0, SEQ // TM, fin, 0)

    col = lambda blk: pl.BlockSpec((SEQ, LANES), lambda p, blk=blk: (0, blk + p))
    kv = (lambda blk: pl.BlockSpec((SEQ, LANES), lambda p, blk=blk: (0, blk))) if gqa else col
    in_specs = [col(qblk), kv(kblk), kv(vblk)]
    args = [proj, proj, proj]
    if has_sink:
        in_specs.append(pl.BlockSpec((1, LANES), lambda p: (0, p)))
        args.append(sink_x)
    out = pl.BlockSpec((SEQ, LANES), lambda p: (0, p))
    return pl.pallas_call(body, name=name, out_shape=(_sds((SEQ, 512)), _sds((SEQ, 512))), grid=(4,),
                          in_specs=in_specs, out_specs=(out, out),
                          scratch_shapes=[pltpu.VMEM((SEQ, LANES), F32), pltpu.VMEM((SEQ, LANES), F32)],
                          compiler_params=_cp(48))(*args)


def _attn_bwd(proj, qblk, kblk, vblk, do, o, lse, dils, gqa, sink_x, name):
    has_sink = sink_x is not None

    def body(*refs):
        if has_sink:
            q_ref, k_ref, v_ref, do_ref, o_ref, lse_ref, s_ref, dq_ref, dk_ref, dv_ref, ds_ref = refs
        else:
            q_ref, k_ref, v_ref, do_ref, o_ref, lse_ref, dq_ref, dk_ref, dv_ref = refs
        m0, m1 = _half_masks()
        pid = pl.program_id(0)
        g = pid // 2
        dq_ref[...] = jnp.zeros_like(dq_ref)
        if gqa:
            @pl.when(pid == 0)
            def _():
                dk_ref[...] = jnp.zeros_like(dk_ref)
                dv_ref[...] = jnp.zeros_like(dv_ref)
        else:
            dk_ref[...] = jnp.zeros_like(dk_ref)
            dv_ref[...] = jnp.zeros_like(dv_ref)
        dsink = jnp.zeros((1, LANES), F32)
        for dil in dils:
            def step(n, dsink, dil=dil):
                st, stp, lo = _block_pos(n, dil)
                rq, rp = _rows(st, dil), _rows(stp, dil)
                valid = _band_valid(lo)
                q, do_, o_, lse_ = q_ref[rq, :], do_ref[rq, :], o_ref[rq, :], lse_ref[rq, :]
                kk = jnp.concatenate([k_ref[rp, :], k_ref[rq, :]], axis=0)
                vv = jnp.concatenate([v_ref[rp, :], v_ref[rq, :]], axis=0)
                if gqa:
                    kr, vr = pltpu.roll(kk, HD, axis=1), pltpu.roll(vv, HD, axis=1)
                dq_pair = 0.0
                dk_sum = dv_sum = 0.0
                for a, msk in enumerate((m0, m1)):
                    if gqa:
                        w = (g == a).astype(F32)
                        ka, va = kk * w + kr * (1.0 - w), vv * w + vr * (1.0 - w)
                    else:
                        ka, va = kk, vv
                    qa, doa = q * msk, do_ * msk
                    delta = jnp.sum(doa * o_, axis=1, keepdims=True)
                    lse_a = lse_[:, HD * a:HD * a + 1]
                    s = _mm(qa, ka, NT) * (HD ** -0.5)
                    s = jnp.where(valid, s, NEG)
                    p = jnp.exp(s - lse_a)
                    dp = _mm(doa, va, NT)
                    dsr = p * (dp - delta) * (HD ** -0.5)
                    dq_pair = dq_pair + _mm(dsr, ka) * msk
                    dka, dva = _mm(dsr, qa, TN), _mm(p, doa, TN)
                    if gqa:
                        dka = dka * w + pltpu.roll(dka, HD, axis=1) * (1.0 - w)
                        dva = dva * w + pltpu.roll(dva, HD, axis=1) * (1.0 - w)
                    dk_sum, dv_sum = dk_sum + dka, dv_sum + dva
                    if has_sink:
                        sk = s_ref[...][:, HD * a:HD * a + 1]
                        dsink = dsink - jnp.sum(jnp.exp(sk - lse_a) * delta, axis=0, keepdims=True) * msk
                dq_ref[rq, :] += dq_pair
                dk_ref[rp, :] += dk_sum[:BLK]
                dk_ref[rq, :] += dk_sum[BLK:]
                dv_ref[rp, :] += dv_sum[:BLK]
                dv_ref[rq, :] += dv_sum[BLK:]
                return dsink
            dsink = lax.fori_loop(0, SEQ // BLK, step, dsink, unroll=2)
        if has_sink:
            ds_ref[0] = jnp.broadcast_to(dsink, (8, LANES))

    col = lambda blk: pl.BlockSpec((SEQ, LANES), lambda p, blk=blk: (0, blk + p))
    kv = (lambda blk: pl.BlockSpec((SEQ, LANES), lambda p, blk=blk: (0, blk))) if gqa else col
    pair = pl.BlockSpec((SEQ, LANES), lambda p: (0, p))
    in_specs = [col(qblk), kv(kblk), kv(vblk), pair, pair, pair]
    args = [proj, proj, proj, do, o, lse]
    kvw = LANES if gqa else 512
    kv_out = pl.BlockSpec((SEQ, LANES), lambda p: (0, 0)) if gqa else pair
    out_shape = [_sds((SEQ, 512)), _sds((SEQ, kvw)), _sds((SEQ, kvw))]
    out_specs = [pair, kv_out, kv_out]
    if has_sink:
        in_specs.append(pl.BlockSpec((1, LANES), lambda p: (0, p)))
        args.append(sink_x)
        out_shape.append(_sds((4, 8, LANES)))
        out_specs.append(pl.BlockSpec((1, 8, LANES), lambda p: (p, 0, 0)))
    return pl.pallas_call(body, name=name, out_shape=tuple(out_shape), grid=(4,), in_specs=in_specs,
                          out_specs=tuple(out_specs), compiler_params=_cp(56))(*args)


def _shift_down(v, k):
    row = lax.broadcasted_iota(jnp.int32, v.shape, 0)
    return jnp.where(row >= k, pltpu.roll(v, k, axis=0), 0.0)


def _shift_up(v, k):
    n = v.shape[0]
    row = lax.broadcasted_iota(jnp.int32, v.shape, 0)
    return jnp.where(row < n - k, pltpu.roll(v, n - k, axis=0), 0.0)


def _conv_pre(x, w_ref, b_ref):
    u = b_ref[...] + x * w_ref[3:4, :]
    for k in range(1, 4):
        u = u + _shift_down(x, k) * w_ref[3 - k:4 - k, :]
    return u


def _conv_fwd(proj, w, b, name):
    def body(x_ref, w_ref, b_ref, o_ref):
        o_ref[...] = _silu(_conv_pre(x_ref[...], w_ref, b_ref))

    nblk = CONV_CH // LANES
    return pl.pallas_call(body, name=name, out_shape=_sds((SEQ, CONV_CH)), grid=(nblk,),
                          in_specs=[pl.BlockSpec((SEQ, LANES), lambda j: (0, XBC // LANES + j)),
                                    pl.BlockSpec((4, LANES), lambda j: (0, j)), pl.BlockSpec((1, LANES), lambda j: (0, j))],
                          out_specs=pl.BlockSpec((SEQ, LANES), lambda j: (0, j)), compiler_params=_cp())(proj, w, b)


def _conv_bwd(proj, dact, w, b, name):
    def body(x_ref, da_ref, w_ref, b_ref, dx_ref, dw_ref, db_ref):
        x = x_ref[...]
        du = da_ref[...] * _dsilu(_conv_pre(x, w_ref, b_ref))
        dx = du * w_ref[3:4, :]
        for k in range(1, 4):
            dx = dx + _shift_up(du, k) * w_ref[3 - k:4 - k, :]
        dx_ref[...] = dx
        db_ref[...] = jnp.sum(du, axis=0, keepdims=True)
        dw_ref[3:4, :] = jnp.sum(du * x, axis=0, keepdims=True)
        for k in range(1, 4):
            dw_ref[3 - k:4 - k, :] = jnp.sum(du * _shift_down(x, k), axis=0, keepdims=True)

    nblk = CONV_CH // LANES
    blk = pl.BlockSpec((SEQ, LANES), lambda j: (0, j))
    wspec, bspec = pl.BlockSpec((4, LANES), lambda j: (0, j)), pl.BlockSpec((1, LANES), lambda j: (0, j))
    return pl.pallas_call(body, name=name, out_shape=(_sds((SEQ, CONV_CH)), _sds((4, CONV_CH)), _sds((1, CONV_CH))), grid=(nblk,),
                          in_specs=[pl.BlockSpec((SEQ, LANES), lambda j: (0, XBC // LANES + j)), blk, wspec, bspec],
                          out_specs=(blk, wspec, bspec), compiler_params=_cp())(proj, dact, w, b)


def _ssd_chunk(xs, bm, cm, dtr, z, hs, alx, al16, dtb, dskx, nw):
    m0, m1 = _half_masks()
    row = lax.broadcasted_iota(jnp.int32, (BLK, BLK), 0)
    col = lax.broadcasted_iota(jnp.int32, (BLK, BLK), 1)
    causal = row >= col
    tril = causal.astype(F32)
    lane = lax.broadcasted_iota(jnp.int32, (1, LANES), 1)
    sub = lax.broadcasted_iota(jnp.int32, (BLK, 1), 0)
    last_row = (sub == BLK - 1).astype(F32)
    dt = jnp.where(lane < 16, _softplus(dtr + dtb), 0.0)
    a16 = -jnp.exp(al16)
    acum = jnp.dot(tril, dt * a16, precision=HI, preferred_element_type=F32)
    acum_t = acum.T
    gmat = [_mm(cm[g], bm[g], NT) for g in range(2)]
    ys, hn = [], []
    for p in range(8):
        g = p // 4
        expand = ((col >> 6) + 2 * p == row).astype(F32)
        dt_x = jnp.dot(dt, expand, precision=HI, preferred_element_type=F32)
        a_x = -jnp.exp(alx[p])
        ac_x = jnp.dot(tril, dt_x * a_x, precision=HI, preferred_element_type=F32)
        a_end = jnp.sum(ac_x * last_row, axis=0, keepdims=True)
        xdt = xs[p] * dt_x
        y = _mm(cm[g], hs[p]) * jnp.exp(ac_x)
        for a, msk in enumerate((m0, m1)):
            h = 2 * p + a
            col_h = jnp.sum(acum * (lane == h).astype(F32), axis=1, keepdims=True)
            row_h = jnp.sum(acum_t * (sub == h).astype(F32), axis=0, keepdims=True)
            decay = jnp.exp(jnp.where(causal, col_h - row_h, NEG))
            y = y + _mm(gmat[g] * decay, xdt * msk)
        st = _mm(bm[g], xdt * jnp.exp(a_end - ac_x), TN)
        hn.append(hs[p] * jnp.exp(a_end) + st)
        y = y + dskx[p] * xs[p]
        ys.append(y * _silu(z[p]))
    out = []
    for g in range(2):
        ms = sum(jnp.sum(ys[p] * ys[p], axis=1, keepdims=True) for p in range(4 * g, 4 * g + 4)) * (1.0 / 512)
        rstd = lax.rsqrt(ms + EPS)
        out += [ys[p] * rstd * nw[p] for p in range(4 * g, 4 * g + 4)]
    return out, hn


def _tiles(ref, n, off=0):
    return [ref[:, off + LANES * p:off + LANES * (p + 1)] for p in range(n)]


def _ssd_load(xbc_ref, z_ref, dt_ref, alx_ref, al16_ref, dtb_ref, dsk_ref, nw_ref):
    return (_tiles(xbc_ref, 8), _tiles(xbc_ref, 2, 1024), _tiles(xbc_ref, 2, 1280), dt_ref[...], _tiles(z_ref, 8)), \
           (_tiles(alx_ref, 8), al16_ref[...], dtb_ref[...], _tiles(dsk_ref, 8), _tiles(nw_ref, 8))


_NCH = SEQ // BLK


def _ssd_param_specs():
    return [_full((1, 1024)), _full((1, LANES)), _full((1, LANES)), _full((1, 1024)), _full((1, 1024))]


def _ssd_fwd(xbc_act, proj, alx, al16, dtb, dskx, nw, name):
    def body(xbc_ref, z_ref, dt_ref, alx_ref, al16_ref, dtb_ref, dsk_ref, nw_ref, y_ref, hin_ref, h_scr):
        @pl.when(pl.program_id(0) == 0)
        def _():
            h_scr[...] = jnp.zeros_like(h_scr)
        acts, params = _ssd_load(xbc_ref, z_ref, dt_ref, alx_ref, al16_ref, dtb_ref, dsk_ref, nw_ref)
        hs = _tiles(h_scr, 8)
        hin_ref[0] = h_scr[...]
        ys, hn = _ssd_chunk(*acts, hs, *params)
        for p in range(8):
            y_ref[:, LANES * p:LANES * (p + 1)] = ys[p]
            h_scr[:, LANES * p:LANES * (p + 1)] = hn[p]

    return pl.pallas_call(
        body, name=name, out_shape=(_sds((SEQ, 1024)), _sds((_NCH, BLK, 1024))), grid=(_NCH,),
        in_specs=[pl.BlockSpec((BLK, CONV_CH), lambda c: (c, 0)), pl.BlockSpec((BLK, 1024), lambda c: (c, ZB // 1024)),
                  pl.BlockSpec((BLK, LANES), lambda c: (c, DTC // LANES))] + _ssd_param_specs(),
        out_specs=(pl.BlockSpec((BLK, 1024), lambda c: (c, 0)), pl.BlockSpec((1, BLK, 1024), lambda c: (c, 0, 0))),
        scratch_shapes=[pltpu.VMEM((BLK, 1024), F32)], compiler_params=_cp())(xbc_act, proj, proj, alx, al16, dtb, dskx, nw)


def _ssd_bwd(xbc_act, proj, hin, dyb, alx, al16, dtb, dskx, nw, name):
    def body(xbc_ref, z_ref, dt_ref, hin_ref, dy_ref, alx_ref, al16_ref, dtb_ref, dsk_ref, nw_ref,
             dxbc_ref, dz_ref, ddt_ref, dalx_ref, dal16_ref, ddtb_ref, ddsk_ref, dnw_ref, dh_scr):
        @pl.when(pl.program_id(0) == 0)
        def _():
            dh_scr[...] = jnp.zeros_like(dh_scr)
            for r in (dalx_ref, dal16_ref, ddtb_ref, ddsk_ref, dnw_ref):
                r[...] = jnp.zeros_like(r)
        acts, params = _ssd_load(xbc_ref, z_ref, dt_ref, alx_ref, al16_ref, dtb_ref, dsk_ref, nw_ref)
        hs = [hin_ref[0, :, LANES * p:LANES * (p + 1)] for p in range(8)]
        _, vjp = jax.vjp(lambda a, h, q: _ssd_chunk(*a, h, *q), acts, hs, params)
        (dxs, dbm, dcm, ddt, dz), dhs, (dalx, dal16, ddtb, ddsk, dnw) = vjp((_tiles(dy_ref, 8), _tiles(dh_scr, 8)))
        for p in range(8):
            cols = slice(LANES * p, LANES * (p + 1))
            dxbc_ref[:, cols] = dxs[p]
            dz_ref[:, cols] = dz[p]
            dh_scr[:, cols] = dhs[p]
            dalx_ref[:, cols] += dalx[p]
            ddsk_ref[:, cols] += ddsk[p]
            dnw_ref[:, cols] += dnw[p]
        for g in range(2):
            dxbc_ref[:, 1024 + LANES * g:1024 + LANES * (g + 1)] = dbm[g]
            dxbc_ref[:, 1280 + LANES * g:1280 + LANES * (g + 1)] = dcm[g]
        ddt_ref[...] = ddt
        dal16_ref[...] += dal16
        ddtb_ref[...] += ddtb

    rev = lambda c: _NCH - 1 - c
    return pl.pallas_call(
        body, name=name,
        out_shape=(_sds((SEQ, CONV_CH)), _sds((SEQ, 1024)), _sds((SEQ, LANES)),
                   _sds((1, 1024)), _sds((1, LANES)), _sds((1, LANES)), _sds((1, 1024)), _sds((1, 1024))),
        grid=(_NCH,),
        in_specs=[pl.BlockSpec((BLK, CONV_CH), lambda c: (rev(c), 0)), pl.BlockSpec((BLK, 1024), lambda c: (rev(c), ZB // 1024)),
                  pl.BlockSpec((BLK, LANES), lambda c: (rev(c), DTC // LANES)), pl.BlockSpec((1, BLK, 1024), lambda c: (rev(c), 0, 0)),
                  pl.BlockSpec((BLK, 1024), lambda c: (rev(c), 0))] + _ssd_param_specs(),
        out_specs=(pl.BlockSpec((BLK, CONV_CH), lambda c: (rev(c), 0)), pl.BlockSpec((BLK, 1024), lambda c: (rev(c), 0)),
                   pl.BlockSpec((BLK, LANES), lambda c: (rev(c), 0)),
                   _full((1, 1024)), _full((1, LANES)), _full((1, LANES)), _full((1, 1024)), _full((1, 1024))),
        scratch_shapes=[pltpu.VMEM((BLK, 1024), F32)], compiler_params=_cp())(xbc_act, proj, proj, hin, dyb, alx, al16, dtb, dskx, nw)


def _rstd(v):
    return lax.rsqrt(jnp.mean(v * v, axis=1, keepdims=True) + EPS)


def _rms_bwd(dn, n, rstd):
    return rstd * (dn - n * jnp.mean(dn * n, axis=1, keepdims=True))


_VEC = _full((1, D))


def _layer_spec(layer):
    return pl.BlockSpec((None, 2048, D), lambda *_: (layer, 0, 0))

_ROW = pl.BlockSpec((TM, D), lambda i, *_: (i, 0))


def _proj_fwd(x, pre_w, scale, shift, w, layer, name):
    tn = 1024

    def body(x_ref, pw_ref, sc_ref, sh_ref, w_ref, o_ref, h_ref):
        @pl.when(pl.program_id(1) == 0)
        def _():
            xv = x_ref[...]
            h = (xv * _rstd(xv) * pw_ref[...]) * (1.0 + sc_ref[...]) + sh_ref[...]
            h_ref[...] = h.astype(h_ref.dtype)
        o_ref[...] = jnp.dot(h_ref[...], w_ref[...].astype(MXU), preferred_element_type=F32)

    return pl.pallas_call(body, name=name, out_shape=(_sds((SEQ, NP)), _sds((SEQ, D), MXU)), grid=(SEQ // TM, NP // tn),
                          in_specs=[_ROW, _VEC, _VEC, _VEC, pl.BlockSpec((None, D, tn), lambda i, j: (layer, 0, j))],
                          out_specs=(pl.BlockSpec((TM, tn), lambda i, j: (i, j)), _ROW), compiler_params=_cp())(x, pre_w, scale, shift, w)


def _out_fwd(ya, yb, yc, w, layer, x, gate, post_w, name):
    def body(ya_ref, yb_ref, yc_ref, w_ref, x_ref, g_ref, pw_ref, xn_ref, y_ref):
        y = _mm(ya_ref[...], w_ref[0:512, :]) + _mm(yb_ref[...], w_ref[512:1536, :]) + _mm(yc_ref[...], w_ref[1536:2048, :])
        y_ref[...] = y
        xn_ref[...] = x_ref[...] + g_ref[...] * (y * _rstd(y) * pw_ref[...])

    half = pl.BlockSpec((TM, 512), lambda i: (i, 0))
    return pl.pallas_call(body, name=name, out_shape=(_sds((SEQ, D)), _sds((SEQ, D))), grid=(SEQ // TM,),
                          in_specs=[half, _ROW, half, _layer_spec(layer), _ROW, _VEC, _VEC],
                          out_specs=(_ROW, _ROW), compiler_params=_cp())(ya, yb, yc, w, x, gate, post_w)


def _post_bwd(dxo, y, gate, post_w, name):
    def body(dx_ref, y_ref, g_ref, pw_ref, dy_ref, dg_ref, dpw_ref):
        @pl.when(pl.program_id(0) == 0)
        def _():
            dg_ref[...] = jnp.zeros_like(dg_ref)
            dpw_ref[...] = jnp.zeros_like(dpw_ref)
        dx, y = dx_ref[...], y_ref[...]
        rstd = _rstd(y)
        n = y * rstd
        dg_ref[...] += jnp.sum(dx * (n * pw_ref[...]), axis=0, keepdims=True)
        dr = dx * g_ref[...]
        dpw_ref[...] += jnp.sum(dr * n, axis=0, keepdims=True)
        dy_ref[...] = _rms_bwd(dr * pw_ref[...], n, rstd)

    return pl.pallas_call(body, name=name, out_shape=(_sds((SEQ, D)), _sds((1, D)), _sds((1, D))), grid=(SEQ // TM,),
                          in_specs=[_ROW, _ROW, _VEC, _VEC], out_specs=(_ROW, _VEC, _VEC), compiler_params=_cp())(dxo, y, gate, post_w)


def _dymix(dy, w, layer, name):
    def body(dy_ref, w_ref, a_ref, b_ref, c_ref):
        dy = dy_ref[...]
        a_ref[...] = _mm(dy, w_ref[0:512, :], NT)
        b_ref[...] = _mm(dy, w_ref[512:1536, :], NT)
        c_ref[...] = _mm(dy, w_ref[1536:2048, :], NT)

    half = pl.BlockSpec((TM, 512), lambda i: (i, 0))
    return pl.pallas_call(body, name=name, out_shape=(_sds((SEQ, 512)), _sds((SEQ, D)), _sds((SEQ, 512))), grid=(SEQ // TM,),
                          in_specs=[_ROW, _layer_spec(layer)], out_specs=(half, _ROW, half), compiler_params=_cp())(dy, w)


def _dwout(ya, yb, yc, dy, name):
    def body(ya_ref, yb_ref, yc_ref, dy_ref, o_ref):
        @pl.when(pl.program_id(0) == 0)
        def _():
            o_ref[...] = jnp.zeros_like(o_ref)
        dy = dy_ref[...]
        o_ref[0:512, :] += _mm(ya_ref[...], dy, TN)
        o_ref[512:1536, :] += _mm(yb_ref[...], dy, TN)
        o_ref[1536:2048, :] += _mm(yc_ref[...], dy, TN)

    half = pl.BlockSpec((TM, 512), lambda i: (i, 0))
    return pl.pallas_call(body, name=name, out_shape=_sds((2048, D)), grid=(SEQ // TM,),
                          in_specs=[half, _ROW, half, _ROW], out_specs=_full((2048, D)), compiler_params=_cp())(ya, yb, yc, dy)


_TK = 1536


def _dwin(h, dproj, name):
    def body(h_ref, dp_ref, o_ref):
        @pl.when(pl.program_id(1) == 0)
        def _():
            o_ref[...] = jnp.zeros_like(o_ref)
        o_ref[...] += _mm(h_ref[...], dp_ref[...], TN)

    return pl.pallas_call(body, name=name, out_shape=_sds((D, NP)), grid=(NP // _TK, SEQ // TM),
                          in_specs=[pl.BlockSpec((TM, D), lambda j, k: (k, 0)), pl.BlockSpec((TM, _TK), lambda j, k: (k, j))],
                          out_specs=pl.BlockSpec((D, _TK), lambda j, k: (0, j)), compiler_params=_cp())(h, dproj)


def _dh_bwd(dproj, w, layer, x, pre_w, scale, dxo, name):
    nk = NP // _TK

    def body(dp_ref, w_ref, x_ref, pw_ref, sc_ref, dxo_ref, dx_ref, dsh_ref, dsc_ref, dpw_ref, acc):
        i, k = pl.program_id(0), pl.program_id(1)

        @pl.when((i == 0) & (k == 0))
        def _():
            for r in (dsh_ref, dsc_ref, dpw_ref):
                r[...] = jnp.zeros_like(r)

        @pl.when(k == 0)
        def _():
            acc[...] = jnp.zeros_like(acc)
        acc[...] += _mm(dp_ref[...], w_ref[...], NT)

        @pl.when(k == nk - 1)
        def _():
            dh, xv = acc[...], x_ref[...]
            rstd = _rstd(xv)
            n = xv * rstd
            dsh_ref[...] += jnp.sum(dh, axis=0, keepdims=True)
            dsc_ref[...] += jnp.sum(dh * (n * pw_ref[...]), axis=0, keepdims=True)
            dhn = dh * (1.0 + sc_ref[...])
            dpw_ref[...] += jnp.sum(dhn * n, axis=0, keepdims=True)
            dx_ref[...] = _rms_bwd(dhn * pw_ref[...], n, rstd) + dxo_ref[...]

    return pl.pallas_call(body, name=name, out_shape=(_sds((SEQ, D)), _sds((1, D)), _sds((1, D)), _sds((1, D))),
                          grid=(SEQ // TM, nk),
                          in_specs=[pl.BlockSpec((TM, _TK), lambda i, k: (i, k)), pl.BlockSpec((None, D, _TK), lambda i, k: (layer, 0, k)),
                                    _ROW, _VEC, _VEC, _ROW],
                          out_specs=(_ROW, _VEC, _VEC, _VEC), scratch_shapes=[pltpu.VMEM((TM, D), F32)],
                          compiler_params=_cp())(dproj, w, x, pre_w, scale, dxo)


def _loss_bwd(xf, tgt, name):
    def body(x_ref, t_ref, dx_ref, l_ref):
        @pl.when(pl.program_id(0) == 0)
        def _():
            l_ref[...] = jnp.zeros_like(l_ref)
        e = x_ref[...] - t_ref[...]
        dx_ref[...] = e * (1.0 / D)
        l_ref[...] += 0.5 * jnp.sum(jnp.mean(e * e, axis=1, keepdims=True), axis=0, keepdims=True)

    return pl.pallas_call(body, name=name, out_shape=(_sds((SEQ, D)), _sds((8, LANES))), grid=(SEQ // TM,),
                          in_specs=[_ROW, _ROW], out_specs=(_ROW, _full((8, LANES))), compiler_params=_cp())(xf, tgt)


def _mod_part(c_all, ada_w, ada_b, name):
    def body(c_ref, w_ref, b_ref, o_ref):
        o_ref[0] = _mm(_silu(c_ref[...]), w_ref[0]) + b_ref[0]

    return pl.pallas_call(body, name=name, out_shape=_sds((DEPTH, 8, 768)), grid=(DEPTH,),
                          in_specs=[_full((8, D)), pl.BlockSpec((1, D, 768), lambda i: (i, 0, 0)), pl.BlockSpec((1, 1, 768), lambda i: (i, 0, 0))],
                          out_specs=pl.BlockSpec((1, 8, 768), lambda i: (i, 0, 0)), compiler_params=_cp())(c_all, ada_w, ada_b)


def _ada_grad(c_t, dmod, name):
    def body(c_ref, d_ref, o_ref):
        ca = _silu(c_ref[...])
        dm = d_ref[0]
        acc = ca[:, 0:1] * dm[0:1, :]
        for s in range(1, 8):
            acc = acc + ca[:, s:s + 1] * dm[s:s + 1, :]
        o_ref[0] = acc

    return pl.pallas_call(body, name=name, out_shape=_sds((DEPTH, D, 768)), grid=(DEPTH,),
                          in_specs=[_full((D, LANES)), pl.BlockSpec((1, 8, 768), lambda i: (i, 0, 0))],
                          out_specs=pl.BlockSpec((1, D, 768), lambda i: (i, 0, 0)), compiler_params=_cp())(c_t, dmod)


def _pack(parts):
    flat = []
    for p in parts:
        f = p.reshape(-1)
        flat.append(jnp.pad(f, (0, (-f.size) % LANES)))
    v = jnp.concatenate(flat)
    return jnp.pad(v, (0, (-v.size) % (8 * LANES))).reshape(-1, LANES)


def _unpack(v, shapes):
    v = v.reshape(-1)
    out, off = [], 0
    for s in shapes:
        n = math.prod(s)
        out.append(v[off:off + n].reshape(s))
        off += n + (-n) % LANES
    return out


_GIVEN_DT, _GIVEN_C = 4608, 4624


def _pad_cols(w):
    return jnp.concatenate([w[..., :_GIVEN_DT], w[..., _GIVEN_C:], w[..., _GIVEN_DT:_GIVEN_C],
                            jnp.zeros(w.shape[:-1] + (NP - IN_COLS,), w.dtype)], axis=-1)


def _unpad_cols(w):
    return jnp.concatenate([w[..., :_GIVEN_DT], w[..., DTC:DTC + 16], w[..., _GIVEN_DT:DTC]], axis=-1)


def _pad_lanes(v):
    return jnp.pad(v, (0, LANES - v.shape[0])).reshape(1, LANES)


def _local_step(x2, tgt, mod, w_p, w_o, pre_w, post_w, conv_w, conv_b, dt_bias, a_log, d_skip, nw, sinks):
    saved = []
    xcur = x2
    for i in range(DEPTH):
        shift, scale, gate = mod[i:i + 1, :D], mod[i:i + 1, D:2 * D], mod[i:i + 1, 2 * D:]
        pw, qw = pre_w[i:i + 1], post_w[i:i + 1]
        proj, h = _proj_fwd(xcur, pw, scale, shift, w_p, i, "proj_fwd")
        o_a, lse_a = _attn_fwd(proj, QA // LANES, KA // LANES, VA // LANES, DILS, False, None, "attn_a_fwd")
        ya = _gate_fwd(o_a, proj, ZA // 512, "gate_a_fwd")
        sink_x = jnp.repeat(sinks[i], HD).reshape(1, 512)
        o_c, lse_c = _attn_fwd(proj, QC // LANES, KC // LANES, VC // LANES, (1,), True, sink_x, "attn_c_fwd")
        yc = _gate_fwd(o_c, proj, ZC // 512, "gate_c_fwd")
        cw, cb = conv_w[i], conv_b[i:i + 1]
        xbc_act = _conv_fwd(proj, cw, cb, "conv_fwd")
        ssd_p = (jnp.repeat(a_log[i], HD).reshape(1, 1024), _pad_lanes(a_log[i]), _pad_lanes(dt_bias[i]),
                 jnp.repeat(d_skip[i], HD).reshape(1, 1024), nw[i:i + 1])
        yb, hin = _ssd_fwd(xbc_act, proj, *ssd_p, "ssd_fwd")
        xnew, y = _out_fwd(ya, yb, yc, w_o, i, xcur, gate, qw, "out_fwd")
        saved.append((xcur, scale, gate, pw, qw, proj, h, o_a, lse_a, ya, sink_x, o_c, lse_c, yc, cw, cb, xbc_act, ssd_p, yb, hin, y))
        xcur = xnew
    dx, ltile = _loss_bwd(xcur, tgt, "loss")
    dwi, dwo, dmod, small = [None] * DEPTH, [None] * DEPTH, [None] * DEPTH, [None] * DEPTH
    for i in reversed(range(DEPTH)):
        xin, scale, gate, pw, qw, proj, h, o_a, lse_a, ya, sink_x, o_c, lse_c, yc, cw, cb, xbc_act, ssd_p, yb, hin, y = saved[i]
        dy, dgate, dpost = _post_bwd(dx, y, gate, qw, "post_bwd")
        dya, dyb, dyc = _dymix(dy, w_o, i, "dymix")
        dwo[i] = _dwout(ya, yb, yc, dy, "dwout")
        do_a, dz_a = _gate_bwd(dya, o_a, proj, ZA // 512, "gate_a_bwd")
        dq_a, dk_a, dv_a = _attn_bwd(proj, QA // LANES, KA // LANES, VA // LANES, do_a, o_a, lse_a, DILS, False, None, "attn_a_bwd")
        do_c, dz_c = _gate_bwd(dyc, o_c, proj, ZC // 512, "gate_c_bwd")
        dq_c, dk_c, dv_c, dsk = _attn_bwd(proj, QC // LANES, KC // LANES, VC // LANES, do_c, o_c, lse_c, (1,), True, sink_x, "attn_c_bwd")
        dxbc_act, dz_b, ddt, dalx, dal16, ddtb, ddsk, dnw = _ssd_bwd(xbc_act, proj, hin, dyb, *ssd_p, "ssd_bwd")
        dxbc, dcw, dcb = _conv_bwd(proj, dxbc_act, cw, cb, "conv_bwd")
        dproj = jnp.concatenate([dq_a, dk_a, dv_a, dz_a, dz_b, dxbc, dq_c, dz_c, dk_c, dv_c, ddt,
                                 jnp.zeros((SEQ, NP - DTC - LANES), F32)], axis=1)
        dwi[i] = _dwin(h, dproj, "dwin")
        dx, dshift, dscale, dpre = _dh_bwd(dproj, w_p, i, xin, pw, scale, dx, "dh_bwd")
        dmod[i] = jnp.concatenate([dshift, dscale, dgate], axis=1)
        dal = dalx.reshape(16, HD).sum(axis=1) + dal16[0, :16]
        small[i] = (dpre, dpost, dcw, dcb, ddtb[0, :16], dal, ddsk.reshape(16, HD).sum(axis=1), dnw, dsk[:, 0, ::HD].reshape(8))
    return ltile, dx, jnp.stack(dwi), jnp.stack(dwo), jnp.concatenate(dmod, axis=0), small


_SMALL = ((1, D), (1, D), (4, CONV_CH), (1, CONV_CH), (16,), (16,), (16,), (1, D), (8,))


def kernel(x, c, ada_w, ada_b, pre_norm_w, post_norm_w, w_in, conv_w, conv_b, dt_bias, a_log, d_skip, ssm_norm_w, sinks, w_out, loss_target, m_ada_w, m_ada_b, m_pre_norm_w, m_post_norm_w, m_w_in, m_conv_w, m_conv_b, m_dt_bias, m_a_log, m_d_skip, m_ssm_norm_w, m_sinks, m_w_out, v_ada_w, v_ada_b, v_pre_norm_w, v_post_norm_w, v_w_in, v_conv_w, v_conv_b, v_dt_bias, v_a_log, v_d_skip, v_ssm_norm_w, v_sinks, v_w_out):
    xi, yi, ci = lax.axis_index("x"), lax.axis_index("y"), lax.axis_index("c")
    chip = 2 * xi + yi
    me = 2 * chip + ci

    w_in_b = _cast_bf16(w_in.reshape(DEPTH * D, SHARD_IN), 512, "cast_w_in").reshape(DEPTH, D, SHARD_IN)
    w_out_b = _cast_bf16(w_out.reshape(DEPTH * 512, D), 512, "cast_w_out").reshape(DEPTH, 512, D)
    g_in, g_out = _exchange4([w_in_b, w_out_b], False, "gather_weights")
    w_p = _pad_cols(jnp.concatenate([g_in[k] for k in range(4)], axis=-1))
    w_o = jnp.concatenate([g_out[k] for k in range(4)], axis=1)

    g0 = _allgather8(_pack([c, conv_w]), "gather_c")
    c_all = g0[:, :8, :].reshape(8, D)
    conv_w_full = jnp.concatenate([g0[2 * k, 8:56, :].reshape(DEPTH, 4, CONV_CH // 4) for k in range(4)], axis=-1)

    ada_b_mine = lax.dynamic_slice_in_dim(ada_b, 768 * chip, 768, axis=1).reshape(DEPTH, 1, 768)
    gm = _allgather8(_mod_part(c_all, ada_w, ada_b_mine, "mod_part").reshape(DEPTH * 8, 768), "gather_mod")
    gm = gm.reshape(4, 2, DEPTH, 8, 768)[:, 0]
    mod = lax.dynamic_index_in_dim(gm, me, axis=2, keepdims=False).transpose(1, 0, 2).reshape(DEPTH, 3 * D)

    ltile, dx, dwi_p, dwo, dmod, small = _local_step(x[0], loss_target[0], mod, w_p, w_o, pre_norm_w, post_norm_w, conv_w_full,
                                                     conv_b, dt_bias, a_log, d_skip, ssm_norm_w, sinks)

    packed = _pack([dmod] + [g for layer in small for g in layer] + [ltile[0]])
    gs = _allgather8(packed, "gather_small")
    tot = _sum_blocks(gs, packed.shape[0], "sum_small")
    parts = _unpack(tot, [(DEPTH, 3 * D)] + list(_SMALL) * DEPTH + [(LANES,)])
    g_ada_b, loss = parts[0], parts[-1][0]
    per_layer = [parts[1 + len(_SMALL) * i:1 + len(_SMALL) * (i + 1)] for i in range(DEPTH)]
    g_pre, g_post, g_cw, g_cb, g_dtb, g_al, g_dsk, g_nw, g_sk = [jnp.stack([per_layer[i][j] for i in range(DEPTH)]) for j in range(len(_SMALL))]
    g_pre, g_post, g_cb, g_nw = g_pre[:, 0], g_post[:, 0], g_cb[:, 0], g_nw[:, 0]
    g_cw = lax.dynamic_slice_in_dim(g_cw, (CONV_CH // 4) * chip, CONV_CH // 4, axis=2)

    dmod_all = gs[:, :(DEPTH * 3 * D) // LANES, :].reshape(8, DEPTH, 3 * D).transpose(1, 0, 2)
    dmod_mine = lax.dynamic_slice_in_dim(dmod_all, 768 * chip, 768, axis=2)
    c_t = jnp.pad(c_all.T, ((0, 0), (0, LANES - 8)))
    g_ada_w = _ada_grad(c_t, dmod_mine, "ada_grad")

    dwi = _unpad_cols(dwi_p)
    blk_in = jnp.stack([dwi[..., SHARD_IN * k:SHARD_IN * (k + 1)] for k in range(4)]).astype(jnp.bfloat16)
    blk_out = jnp.stack([dwo[:, 512 * k:512 * (k + 1), :] for k in range(4)]).astype(jnp.bfloat16)
    r_in, r_out = _exchange4([blk_in, blk_out], True, "scatter_grads")
    p_in = _sum_blocks(r_in.reshape(4, DEPTH * D, SHARD_IN), 256, "sum_w_in")
    p_out = _sum_blocks(r_out.reshape(4, DEPTH * 512, D), 512, "sum_w_out")
    s_in, s_out = _sibling_swap([p_in, p_out], "swap_partials")

    res = {}
    res["ada_w"] = [a.reshape(DEPTH, D, 768) for a in
                    _adamw(ada_w.reshape(DEPTH * D, 768), [g_ada_w.reshape(DEPTH * D, 768)], m_ada_w.reshape(DEPTH * D, 768),
                           v_ada_w.reshape(DEPTH * D, 768), 512, "adamw_ada_w")]
    res["w_in"] = [a.reshape(DEPTH, D, SHARD_IN) for a in
                   _adamw(w_in.reshape(DEPTH * D, SHARD_IN), [p_in, s_in], m_w_in.reshape(DEPTH * D, SHARD_IN),
                          v_w_in.reshape(DEPTH * D, SHARD_IN), 256, "adamw_w_in")]
    res["w_out"] = [a.reshape(DEPTH, 512, D) for a in
                    _adamw(w_out.reshape(DEPTH * 512, D), [p_out, s_out], m_w_out.reshape(DEPTH * 512, D),
                           v_w_out.reshape(DEPTH * 512, D), 512, "adamw_w_out")]
    names = ["ada_b", "pre_norm_w", "post_norm_w", "conv_w", "conv_b", "dt_bias", "a_log", "d_skip", "ssm_norm_w", "sinks"]
    ws = [ada_b, pre_norm_w, post_norm_w, conv_w, conv_b, dt_bias, a_log, d_skip, ssm_norm_w, sinks]
    gsm = [g_ada_b, g_pre, g_post, g_cw, g_cb, g_dtb, g_al, g_dsk, g_nw, g_sk]
    ms = [m_ada_b, m_pre_norm_w, m_post_norm_w, m_conv_w, m_conv_b, m_dt_bias, m_a_log, m_d_skip, m_ssm_norm_w, m_sinks]
    vs = [v_ada_b, v_pre_norm_w, v_post_norm_w, v_conv_w, v_conv_b, v_dt_bias, v_a_log, v_d_skip, v_ssm_norm_w, v_sinks]
    pw_, pg_, pm_, pv_ = _pack(ws), _pack(gsm), _pack(ms), _pack(vs)
    small_out = _adamw(pw_, [pg_], pm_, pv_, pw_.shape[0], "adamw_small")
    shapes = [w.shape for w in ws]
    for kind in range(4):
        for nm, a in zip(names, _unpack(small_out[kind], shapes)):
            res.setdefault(nm, [None] * 4)[kind] = a
    order = ["ada_w", "ada_b", "pre_norm_w", "post_norm_w", "w_in", "conv_w", "conv_b", "dt_bias", "a_log", "d_skip", "ssm_norm_w", "sinks", "w_out"]
    return (loss, dx[None], *[res[n][0] for n in order], *[res[n][1] for n in order], *[res[n][2] for n in order], *[res[n][3] for n in order])
```

```python
import math

import jax
import jax.numpy as jnp
from jax import lax
from jax.experimental import pallas as pl
from jax.experimental.pallas import tpu as pltpu

F32 = jnp.float32
MXU = jnp.bfloat16
HI = lax.Precision.HIGHEST
MESH = pl.DeviceIdType.MESH

SEQ = 4096
D = 1024
DEPTH = 4
HD = 64
LANES = 128
BLK = 128
DILS = (1, 4, 16)
NEG = -1e30
EPS = 1e-6
MIB = 1024 * 1024

NP = 6144
QA, KA, VA, ZA = 0, 512, 1024, 1536
ZB, XBC = 2048, 3072
QC, ZC, KC, VC = 4608, 5120, 5632, 5760
DTC = 5888
IN_COLS = 5904
SHARD_IN = IN_COLS // 4
CONV_CH = 1536
TM = 512

ADAM_LR, ADAM_B1, ADAM_B2, ADAM_EPS, ADAM_WD, ADAM_STEP = 0.001, 0.9, 0.999, 1e-08, 0.01, 10

NT = (((1,), (1,)), ((), ()))
TN = (((0,), (0,)), ((), ()))


def _cp(vmem_mib=48):
    return pltpu.CompilerParams(vmem_limit_bytes=vmem_mib * MIB)


def _sds(shape, dtype=F32):
    return jax.ShapeDtypeStruct(shape, dtype)


def _full(shape):
    n = len(shape)
    return pl.BlockSpec(shape, lambda *_: (0,) * n)


def _mm(a, b, dims=None):
    if dims is None:
        return jnp.dot(a.astype(MXU), b.astype(MXU), preferred_element_type=F32)
    return lax.dot_general(a.astype(MXU), b.astype(MXU), dims, preferred_element_type=F32)


def _sigmoid(x):
    return 1.0 / (1.0 + jnp.exp(-x))


def _silu(x):
    return x * _sigmoid(x)


def _dsilu(x):
    s = _sigmoid(x)
    return s * (1.0 + x * (1.0 - s))


def _softplus(x):
    ax = jnp.where(x >= 0, x, -x)
    return jnp.maximum(x, 0.0) + jnp.log1p(jnp.exp(-ax))


def _half_masks():
    lane = lax.broadcasted_iota(jnp.int32, (1, LANES), 1)
    m0 = (lane < HD).astype(F32)
    return m0, 1.0 - m0


def _allgather8(v, name):
    r, cc = v.shape

    def body(v_ref, out_ref, send_sems, recv_sems):
        x, y, c = lax.axis_index("x"), lax.axis_index("y"), lax.axis_index("c")
        me = 4 * x + 2 * y + c
        out_ref[me] = v_ref[...]
        peers = []
        for k in range(1, 8):
            px = 1 - x if k & 4 else x
            py = 1 - y if k & 2 else y
            pc = 1 - c if k & 1 else c
            peers.append((px, py, pc))
        sends = []
        for k, peer in enumerate(peers):
            cp = pltpu.make_async_remote_copy(src_ref=v_ref, dst_ref=out_ref.at[me], send_sem=send_sems.at[k],
                                              recv_sem=recv_sems.at[k], device_id=peer, device_id_type=MESH)
            cp.start()
            sends.append(cp)
        for k, (px, py, pc) in enumerate(peers):
            pltpu.make_async_remote_copy(src_ref=v_ref, dst_ref=out_ref.at[4 * px + 2 * py + pc], send_sem=send_sems.at[k],
                                         recv_sem=recv_sems.at[k], device_id=(px, py, pc), device_id_type=MESH).wait_recv()
        for cp in sends:
            cp.wait_send()

    return pl.pallas_call(
        body, name=name, out_shape=_sds((8, r, cc)),
        in_specs=[pl.BlockSpec(memory_space=pltpu.VMEM)], out_specs=pl.BlockSpec(memory_space=pltpu.VMEM),
        scratch_shapes=[pltpu.SemaphoreType.DMA((7,)), pltpu.SemaphoreType.DMA((7,))],
        compiler_params=_cp(32),
    )(v)


def _exchange4(arrs, scatter, name):
    n = len(arrs)
    outs = tuple(_sds(a.shape if scatter else (4,) + a.shape, a.dtype) for a in arrs)

    def body(*refs):
        ins, outs_, (send_sems, recv_sems, loc_sems) = refs[:n], refs[n:2 * n], refs[2 * n:]
        x, y, c = lax.axis_index("x"), lax.axis_index("y"), lax.axis_index("c")
        mine = 2 * x + y
        chips = [(1 - x, y), (x, 1 - y), (1 - x, 1 - y)]
        local, sent = [], []
        for i in range(n):
            loc = pltpu.make_async_copy(ins[i].at[mine] if scatter else ins[i], outs_[i].at[mine], loc_sems.at[i])
            loc.start()
            local.append(loc)
            for j, (px, py) in enumerate(chips):
                src = ins[i].at[2 * px + py] if scatter else ins[i]
                cp = pltpu.make_async_remote_copy(src_ref=src, dst_ref=outs_[i].at[mine], send_sem=send_sems.at[3 * i + j],
                                                  recv_sem=recv_sems.at[3 * i + j], device_id=(px, py, c), device_id_type=MESH)
                cp.start()
                sent.append(cp)
        for i in range(n):
            for j, (px, py) in enumerate(chips):
                src = ins[i].at[mine] if scatter else ins[i]
                pltpu.make_async_remote_copy(src_ref=src, dst_ref=outs_[i].at[2 * px + py], send_sem=send_sems.at[3 * i + j],
                                             recv_sem=recv_sems.at[3 * i + j], device_id=(px, py, c), device_id_type=MESH).wait_recv()
        for cp in sent:
            cp.wait_send()
        for loc in local:
            loc.wait()

    hbm = pl.BlockSpec(memory_space=pltpu.HBM)
    return pl.pallas_call(
        body, name=name, out_shape=outs, in_specs=[hbm] * n, out_specs=tuple([hbm] * n),
        scratch_shapes=[pltpu.SemaphoreType.DMA((3 * n,)), pltpu.SemaphoreType.DMA((3 * n,)), pltpu.SemaphoreType.DMA((n,))],
    )(*arrs)


def _sibling_swap(arrs, name):
    n = len(arrs)

    def body(*refs):
        ins, outs_, (send_sems, recv_sems) = refs[:n], refs[n:2 * n], refs[2 * n:]
        sib = (lax.axis_index("x"), lax.axis_index("y"), 1 - lax.axis_index("c"))
        cps = [pltpu.make_async_remote_copy(src_ref=ins[i], dst_ref=outs_[i], send_sem=send_sems.at[i], recv_sem=recv_sems.at[i],
                                            device_id=sib, device_id_type=MESH) for i in range(n)]
        for cp in cps:
            cp.start()
        for cp in cps:
            cp.wait_recv()
        for cp in cps:
            cp.wait_send()

    hbm = pl.BlockSpec(memory_space=pltpu.HBM)
    return pl.pallas_call(
        body, name=name, out_shape=tuple(_sds(a.shape, a.dtype) for a in arrs), in_specs=[hbm] * n, out_specs=tuple([hbm] * n),
        scratch_shapes=[pltpu.SemaphoreType.DMA((n,)), pltpu.SemaphoreType.DMA((n,))],
    )(*arrs)


def _cast_bf16(a, rows, name):
    r, cc = a.shape

    def body(a_ref, o_ref):
        o_ref[...] = a_ref[...].astype(jnp.bfloat16)

    return pl.pallas_call(body, name=name, out_shape=_sds((r, cc), jnp.bfloat16), grid=(r // rows,),
                          in_specs=[pl.BlockSpec((rows, cc), lambda i: (i, 0))],
                          out_specs=pl.BlockSpec((rows, cc), lambda i: (i, 0)), compiler_params=_cp())(a)


def _sum_blocks(a, rows, name):
    k, r, cc = a.shape

    def body(a_ref, o_ref):
        acc = a_ref[0].astype(F32)
        for j in range(1, k):
            acc = acc + a_ref[j].astype(F32)
        o_ref[...] = acc

    return pl.pallas_call(body, name=name, out_shape=_sds((r, cc)), grid=(r // rows,),
                          in_specs=[pl.BlockSpec((k, rows, cc), lambda i: (0, i, 0))],
                          out_specs=pl.BlockSpec((rows, cc), lambda i: (i, 0)), compiler_params=_cp())(a)


def _adamw(w, parts, m, v, rows, name):
    r, cc = w.shape
    np_ = len(parts)
    c1 = 1.0 / (1.0 - ADAM_B1 ** ADAM_STEP)
    c2 = 1.0 / (1.0 - ADAM_B2 ** ADAM_STEP)

    def body(*refs):
        w_ref, p_refs, (m_ref, v_ref, g_ref, d_ref, nm_ref, nv_ref) = refs[0], refs[1:1 + np_], refs[1 + np_:]
        g = p_refs[0][...]
        for p_ref in p_refs[1:]:
            g = g + p_ref[...]
        nm = ADAM_B1 * m_ref[...] + (1.0 - ADAM_B1) * g
        nv = ADAM_B2 * v_ref[...] + (1.0 - ADAM_B2) * (g * g)
        g_ref[...] = g
        nm_ref[...] = nm
        nv_ref[...] = nv
        d_ref[...] = -ADAM_LR * ((nm * c1) / (jnp.sqrt(nv * c2) + ADAM_EPS) + ADAM_WD * w_ref[...])

    spec = pl.BlockSpec((rows, cc), lambda i: (i, 0))
    return pl.pallas_call(body, name=name, out_shape=(_sds((r, cc)),) * 4, grid=(r // rows,),
                          in_specs=[spec] * (3 + np_), out_specs=(spec,) * 4, compiler_params=_cp())(w, *parts, m, v)


def _gate_fwd(o, proj, zblk, name):
    def body(o_ref, z_ref, y_ref):
        y_ref[...] = o_ref[...] * _silu(z_ref[...])

    return pl.pallas_call(body, name=name, out_shape=_sds((SEQ, 512)), grid=(SEQ // TM,),
                          in_specs=[pl.BlockSpec((TM, 512), lambda i: (i, 0)), pl.BlockSpec((TM, 512), lambda i: (i, zblk))],
                          out_specs=pl.BlockSpec((TM, 512), lambda i: (i, 0)), compiler_params=_cp())(o, proj)


def _gate_bwd(dy, o, proj, zblk, name):
    def body(dy_ref, o_ref, z_ref, do_ref, dz_ref):
        dy, z = dy_ref[...], z_ref[...]
        do_ref[...] = dy * _silu(z)
        dz_ref[...] = dy * o_ref[...] * _dsilu(z)

    return pl.pallas_call(body, name=name, out_shape=(_sds((SEQ, 512)), _sds((SEQ, 512))), grid=(SEQ // TM,),
                          in_specs=[pl.BlockSpec((TM, 512), lambda i: (i, 0)), pl.BlockSpec((TM, 512), lambda i: (i, 0)),
                                    pl.BlockSpec((TM, 512), lambda i: (i, zblk))],
                          out_specs=(pl.BlockSpec((TM, 512), lambda i: (i, 0)),) * 2, compiler_params=_cp())(dy, o, proj)


def _band_valid(lo):
    qi = lax.broadcasted_iota(jnp.int32, (BLK, 2 * BLK), 0)
    kj = lax.broadcasted_iota(jnp.int32, (BLK, 2 * BLK), 1)
    dist = BLK + qi - kj
    return (dist >= 0) & (dist <= BLK) & (kj >= lo)


def _rows(st, dil):
    if dil == 1:
        return pl.ds(pl.multiple_of(st, BLK), BLK)
    return pl.ds(st, BLK, stride=dil)


def _block_pos(n, dil):
    nb = SEQ // (dil * BLK)
    r, b = n // nb, n % nb
    hp = (b > 0).astype(jnp.int32)
    st = r + dil * BLK * b
    return st, st - dil * BLK * hp, BLK * (1 - hp)


def _attn_fwd(proj, qblk, kblk, vblk, dils, gqa, sink_x, name):
    has_sink = sink_x is not None

    def body(*refs):
        if has_sink:
            q_ref, k_ref, v_ref, s_ref, o_ref, lse_ref, m_scr, z_scr = refs
        else:
            q_ref, k_ref, v_ref, o_ref, lse_ref, m_scr, z_scr = refs
        m0, m1 = _half_masks()
        g = pl.program_id(0) // 2
        o_ref[...] = jnp.zeros_like(o_ref)
        z_scr[...] = jnp.zeros_like(z_scr)
        m_scr[...] = jnp.full_like(m_scr, NEG)
        for dil in dils:
            def step(n, carry, dil=dil):
                st, stp, lo = _block_pos(n, dil)
                rq, rp = _rows(st, dil), _rows(stp, dil)
                valid = _band_valid(lo)
                q = q_ref[rq, :]
                kk = jnp.concatenate([k_ref[rp, :], k_ref[rq, :]], axis=0)
                vv = jnp.concatenate([v_ref[rp, :], v_ref[rq, :]], axis=0)
                if gqa:
                    kr, vr = pltpu.roll(kk, HD, axis=1), pltpu.roll(vv, HD, axis=1)
                m_pair = l_pair = o_pair = 0.0
                for a, msk in enumerate((m0, m1)):
                    if gqa:
                        w = (g == a).astype(F32)
                        ka, va = kk * w + kr * (1.0 - w), vv * w + vr * (1.0 - w)
                    else:
                        ka, va = kk, vv
                    s = _mm(q * msk, ka, NT) * (HD ** -0.5)
                    s = jnp.where(valid, s, NEG)
                    m = jnp.max(s, axis=1, keepdims=True)
                    if has_sink:
                        sk = s_ref[...][:, HD * a:HD * a + 1]
                        m = jnp.maximum(m, sk)
                    p = jnp.exp(s - m)
                    l = jnp.sum(p, axis=1, keepdims=True)
                    if has_sink:
                        l = l + jnp.exp(sk - m)
                    o_pair = o_pair + _mm(p, va) * msk
                    m_pair = m_pair + m * msk
                    l_pair = l_pair + l * msk
                m_old = m_scr[rq, :]
                m_new = jnp.maximum(m_old, m_pair)
                alpha, beta = jnp.exp(m_old - m_new), jnp.exp(m_pair - m_new)
                o_ref[rq, :] = o_ref[rq, :] * alpha + o_pair * beta
                z_scr[rq, :] = z_scr[rq, :] * alpha + l_pair * beta
                m_scr[rq, :] = m_new
                return carry
            lax.fori_loop(0, SEQ // BLK, step, 0, unroll=4)

        def fin(t, carry):
            rt = pl.ds(pl.multiple_of(t * TM, TM), TM)
            z = z_scr[rt, :]
            o_ref[rt, :] = o_ref[rt, :] / z
            lse_ref[rt, :] = m_scr[rt, :] + jnp.log(z)
            return carry
        lax.fori_loop(0, SEQ // TM, fin, 0)

    col = lambda blk: pl.BlockSpec((SEQ, LANES), lambda p, blk=blk: (0, blk + p))
    kv = (lambda blk: pl.BlockSpec((SEQ, LANES), lambda p, blk=blk: (0, blk))) if gqa else col
    in_specs = [col(qblk), kv(kblk), kv(vblk)]
    args = [proj, proj, proj]
    if has_sink:
        in_specs.append(pl.BlockSpec((1, LANES), lambda p: (0, p)))
        args.append(sink_x)
    out = pl.BlockSpec((SEQ, LANES), lambda p: (0, p))
    return pl.pallas_call(body, name=name, out_shape=(_sds((SEQ, 512)), _sds((SEQ, 512))), grid=(4,),
                          in_specs=in_specs, out_specs=(out, out),
                          scratch_shapes=[pltpu.VMEM((SEQ, LANES), F32), pltpu.VMEM((SEQ, LANES), F32)],
                          compiler_params=_cp(48))(*args)


def _attn_bwd(proj, qblk, kblk, vblk, do, o, lse, dils, gqa, sink_x, name):
    has_sink = sink_x is not None

    def body(*refs):
        if has_sink:
            q_ref, k_ref, v_ref, do_ref, o_ref, lse_ref, s_ref, dq_ref, dk_ref, dv_ref, ds_ref = refs
        else:
            q_ref, k_ref, v_ref, do_ref, o_ref, lse_ref, dq_ref, dk_ref, dv_ref = refs
        m0, m1 = _half_masks()
        pid = pl.program_id(0)
        g = pid // 2
        dq_ref[...] = jnp.zeros_like(dq_ref)
        if gqa:
            @pl.when(pid == 0)
            def _():
                dk_ref[...] = jnp.zeros_like(dk_ref)
                dv_ref[...] = jnp.zeros_like(dv_ref)
        else:
            dk_ref[...] = jnp.zeros_like(dk_ref)
            dv_ref[...] = jnp.zeros_like(dv_ref)
        dsink = jnp.zeros((1, LANES), F32)
        for dil in dils:
            def step(n, dsink, dil=dil):
                st, stp, lo = _block_pos(n, dil)
                rq, rp = _rows(st, dil), _rows(stp, dil)
                valid = _band_valid(lo)
                q, do_, o_, lse_ = q_ref[rq, :], do_ref[rq, :], o_ref[rq, :], lse_ref[rq, :]
                kk = jnp.concatenate([k_ref[rp, :], k_ref[rq, :]], axis=0)
                vv = jnp.concatenate([v_ref[rp, :], v_ref[rq, :]], axis=0)
                if gqa:
                    kr, vr = pltpu.roll(kk, HD, axis=1), pltpu.roll(vv, HD, axis=1)
                dq_pair = 0.0
                dk_sum = dv_sum = 0.0
                for a, msk in enumerate((m0, m1)):
                    if gqa:
                        w = (g == a).astype(F32)
                        ka, va = kk * w + kr * (1.0 - w), vv * w + vr * (1.0 - w)
                    else:
                        ka, va = kk, vv
                    qa, doa = q * msk, do_ * msk
                    delta = jnp.sum(doa * o_, axis=1, keepdims=True)
                    lse_a = lse_[:, HD * a:HD * a + 1]
                    s = _mm(qa, ka, NT) * (HD ** -0.5)
                    s = jnp.where(valid, s, NEG)
                    p = jnp.exp(s - lse_a)
                    dp = _mm(doa, va, NT)
                    dsr = p * (dp - delta) * (HD ** -0.5)
                    dq_pair = dq_pair + _mm(dsr, ka) * msk
                    dka, dva = _mm(dsr, qa, TN), _mm(p, doa, TN)
                    if gqa:
                        dka = dka * w + pltpu.roll(dka, HD, axis=1) * (1.0 - w)
                        dva = dva * w + pltpu.roll(dva, HD, axis=1) * (1.0 - w)
                    dk_sum, dv_sum = dk_sum + dka, dv_sum + dva
                    if has_sink:
                        sk = s_ref[...][:, HD * a:HD * a + 1]
                        dsink = dsink - jnp.sum(jnp.exp(sk - lse_a) * delta, axis=0, keepdims=True) * msk
                dq_ref[rq, :] += dq_pair
                dk_ref[rp, :] += dk_sum[:BLK]
                dk_ref[rq, :] += dk_sum[BLK:]
                dv_ref[rp, :] += dv_sum[:BLK]
                dv_ref[rq, :] += dv_sum[BLK:]
                return dsink
            dsink = lax.fori_loop(0, SEQ // BLK, step, dsink, unroll=2)
        if has_sink:
            ds_ref[0] = jnp.broadcast_to(dsink, (8, LANES))

    col = lambda blk: pl.BlockSpec((SEQ, LANES), lambda p, blk=blk: (0, blk + p))
    kv = (lambda blk: pl.BlockSpec((SEQ, LANES), lambda p, blk=blk: (0, blk))) if gqa else col
    pair = pl.BlockSpec((SEQ, LANES), lambda p: (0, p))
    in_specs = [col(qblk), kv(kblk), kv(vblk), pair, pair, pair]
    args = [proj, proj, proj, do, o, lse]
    kvw = LANES if gqa else 512
    kv_out = pl.BlockSpec((SEQ, LANES), lambda p: (0, 0)) if gqa else pair
    out_shape = [_sds((SEQ, 512)), _sds((SEQ, kvw)), _sds((SEQ, kvw))]
    out_specs = [pair, kv_out, kv_out]
    if has_sink:
        in_specs.append(pl.BlockSpec((1, LANES), lambda p: (0, p)))
        args.append(sink_x)
        out_shape.append(_sds((4, 8, LANES)))
        out_specs.append(pl.BlockSpec((1, 8, LANES), lambda p: (p, 0, 0)))
    return pl.pallas_call(body, name=name, out_shape=tuple(out_shape), grid=(4,), in_specs=in_specs,
                          out_specs=tuple(out_specs), compiler_params=_cp(56))(*args)


def _shift_down(v, k):
    row = lax.broadcasted_iota(jnp.int32, v.shape, 0)
    return jnp.where(row >= k, pltpu.roll(v, k, axis=0), 0.0)


def _shift_up(v, k):
    n = v.shape[0]
    row = lax.broadcasted_iota(jnp.int32, v.shape, 0)
    return jnp.where(row < n - k, pltpu.roll(v, n - k, axis=0), 0.0)


def _conv_pre(x, w_ref, b_ref):
    u = b_ref[...] + x * w_ref[3:4, :]
    for k in range(1, 4):
        u = u + _shift_down(x, k) * w_ref[3 - k:4 - k, :]
    return u


def _conv_fwd(proj, w, b, name):
    def body(x_ref, w_ref, b_ref, o_ref):
        o_ref[...] = _silu(_conv_pre(x_ref[...], w_ref, b_ref))

    nblk = CONV_CH // LANES
    return pl.pallas_call(body, name=name, out_shape=_sds((SEQ, CONV_CH)), grid=(nblk,),
                          in_specs=[pl.BlockSpec((SEQ, LANES), lambda j: (0, XBC // LANES + j)),
                                    pl.BlockSpec((4, LANES), lambda j: (0, j)), pl.BlockSpec((1, LANES), lambda j: (0, j))],
                          out_specs=pl.BlockSpec((SEQ, LANES), lambda j: (0, j)), compiler_params=_cp())(proj, w, b)


def _conv_bwd(proj, dact, w, b, name):
    def body(x_ref, da_ref, w_ref, b_ref, dx_ref, dw_ref, db_ref):
        x = x_ref[...]
        du = da_ref[...] * _dsilu(_conv_pre(x, w_ref, b_ref))
        dx = du * w_ref[3:4, :]
        for k in range(1, 4):
            dx = dx + _shift_up(du, k) * w_ref[3 - k:4 - k, :]
        dx_ref[...] = dx
        db_ref[...] = jnp.sum(du, axis=0, keepdims=True)
        dw_ref[3:4, :] = jnp.sum(du * x, axis=0, keepdims=True)
        for k in range(1, 4):
            dw_ref[3 - k:4 - k, :] = jnp.sum(du * _shift_down(x, k), axis=0, keepdims=True)

    nblk = CONV_CH // LANES
    blk = pl.BlockSpec((SEQ, LANES), lambda j: (0, j))
    wspec, bspec = pl.BlockSpec((4, LANES), lambda j: (0, j)), pl.BlockSpec((1, LANES), lambda j: (0, j))
    return pl.pallas_call(body, name=name, out_shape=(_sds((SEQ, CONV_CH)), _sds((4, CONV_CH)), _sds((1, CONV_CH))), grid=(nblk,),
                          in_specs=[pl.BlockSpec((SEQ, LANES), lambda j: (0, XBC // LANES + j)), blk, wspec, bspec],
                          out_specs=(blk, wspec, bspec), compiler_params=_cp())(proj, dact, w, b)


def _ssd_chunk(xs, bm, cm, dtr, z, hs, al16, dtb, dskx, nw):
    m0, m1 = _half_masks()
    row = lax.broadcasted_iota(jnp.int32, (BLK, BLK), 0)
    col = lax.broadcasted_iota(jnp.int32, (BLK, BLK), 1)
    causal = row >= col
    tril = causal.astype(F32)
    lane = lax.broadcasted_iota(jnp.int32, (1, LANES), 1)
    sub = lax.broadcasted_iota(jnp.int32, (BLK, 1), 0)
    last_row = (sub == BLK - 1).astype(F32)
    dt = jnp.where(lane < 16, _softplus(dtr + dtb), 0.0)
    a16 = -jnp.exp(al16)
    acum = jnp.dot(tril, dt * a16, precision=HI, preferred_element_type=F32)
    acum_t = acum.T
    gmat = [_mm(cm[g], bm[g], NT) for g in range(2)]
    ys, hn = [], []
    for p in range(8):
        g = p // 4
        pick = [(lane == 2 * p + a).astype(F32) for a in range(2)]
        col_h = [jnp.sum(acum * pick[a], axis=1, keepdims=True) for a in range(2)]
        dt_x = sum(jnp.sum(dt * pick[a], axis=1, keepdims=True) * msk for a, msk in enumerate((m0, m1)))
        ac_x = col_h[0] * m0 + col_h[1] * m1
        a_end = jnp.sum(ac_x * last_row, axis=0, keepdims=True)
        xdt = xs[p] * dt_x
        y = _mm(cm[g], hs[p]) * jnp.exp(ac_x)
        for a, msk in enumerate((m0, m1)):
            row_h = jnp.sum(acum_t * (sub == 2 * p + a).astype(F32), axis=0, keepdims=True)
            decay = jnp.exp(jnp.where(causal, col_h[a] - row_h, NEG))
            y = y + _mm(gmat[g] * decay, xdt * msk)
        st = _mm(bm[g], xdt * jnp.exp(a_end - ac_x), TN)
        hn.append(hs[p] * jnp.exp(a_end) + st)
        y = y + dskx[p] * xs[p]
        ys.append(y * _silu(z[p]))
    out = []
    for g in range(2):
        ms = sum(jnp.sum(ys[p] * ys[p], axis=1, keepdims=True) for p in range(4 * g, 4 * g + 4)) * (1.0 / 512)
        rstd = lax.rsqrt(ms + EPS)
        out += [ys[p] * rstd * nw[p] for p in range(4 * g, 4 * g + 4)]
    return out, hn


def _tiles(ref, n, off=0):
    return [ref[:, off + LANES * p:off + LANES * (p + 1)] for p in range(n)]


def _ssd_load(xbc_ref, z_ref, dt_ref, al16_ref, dtb_ref, dsk_ref, nw_ref):
    return (_tiles(xbc_ref, 8), _tiles(xbc_ref, 2, 1024), _tiles(xbc_ref, 2, 1280), dt_ref[...], _tiles(z_ref, 8)), \
           (al16_ref[...], dtb_ref[...], _tiles(dsk_ref, 8), _tiles(nw_ref, 8))


_NCH = SEQ // BLK


def _ssd_param_specs():
    return [_full((1, LANES)), _full((1, LANES)), _full((1, 1024)), _full((1, 1024))]


def _ssd_fwd(xbc_act, proj, al16, dtb, dskx, nw, name):
    def body(xbc_ref, z_ref, dt_ref, al16_ref, dtb_ref, dsk_ref, nw_ref, y_ref, hin_ref, h_scr):
        @pl.when(pl.program_id(0) == 0)
        def _():
            h_scr[...] = jnp.zeros_like(h_scr)
        acts, params = _ssd_load(xbc_ref, z_ref, dt_ref, al16_ref, dtb_ref, dsk_ref, nw_ref)
        hs = _tiles(h_scr, 8)
        hin_ref[0] = h_scr[...]
        ys, hn = _ssd_chunk(*acts, hs, *params)
        for p in range(8):
            y_ref[:, LANES * p:LANES * (p + 1)] = ys[p]
            h_scr[:, LANES * p:LANES * (p + 1)] = hn[p]

    return pl.pallas_call(
        body, name=name, out_shape=(_sds((SEQ, 1024)), _sds((_NCH, BLK, 1024))), grid=(_NCH,),
        in_specs=[pl.BlockSpec((BLK, CONV_CH), lambda c: (c, 0)), pl.BlockSpec((BLK, 1024), lambda c: (c, ZB // 1024)),
                  pl.BlockSpec((BLK, LANES), lambda c: (c, DTC // LANES))] + _ssd_param_specs(),
        out_specs=(pl.BlockSpec((BLK, 1024), lambda c: (c, 0)), pl.BlockSpec((1, BLK, 1024), lambda c: (c, 0, 0))),
        scratch_shapes=[pltpu.VMEM((BLK, 1024), F32)], compiler_params=_cp())(xbc_act, proj, proj, al16, dtb, dskx, nw)


def _ssd_bwd(xbc_act, proj, hin, dyb, al16, dtb, dskx, nw, name):
    def body(xbc_ref, z_ref, dt_ref, hin_ref, dy_ref, al16_ref, dtb_ref, dsk_ref, nw_ref,
             dxbc_ref, dz_ref, ddt_ref, dal16_ref, ddtb_ref, ddsk_ref, dnw_ref, dh_scr):
        @pl.when(pl.program_id(0) == 0)
        def _():
            dh_scr[...] = jnp.zeros_like(dh_scr)
            for r in (dal16_ref, ddtb_ref, ddsk_ref, dnw_ref):
                r[...] = jnp.zeros_like(r)
        acts, params = _ssd_load(xbc_ref, z_ref, dt_ref, al16_ref, dtb_ref, dsk_ref, nw_ref)
        hs = [hin_ref[0, :, LANES * p:LANES * (p + 1)] for p in range(8)]
        _, vjp = jax.vjp(lambda a, h, q: _ssd_chunk(*a, h, *q), acts, hs, params)
        (dxs, dbm, dcm, ddt, dz), dhs, (dal16, ddtb, ddsk, dnw) = vjp((_tiles(dy_ref, 8), _tiles(dh_scr, 8)))
        for p in range(8):
            cols = slice(LANES * p, LANES * (p + 1))
            dxbc_ref[:, cols] = dxs[p]
            dz_ref[:, cols] = dz[p]
            dh_scr[:, cols] = dhs[p]
            ddsk_ref[:, cols] += ddsk[p]
            dnw_ref[:, cols] += dnw[p]
        for g in range(2):
            dxbc_ref[:, 1024 + LANES * g:1024 + LANES * (g + 1)] = dbm[g]
            dxbc_ref[:, 1280 + LANES * g:1280 + LANES * (g + 1)] = dcm[g]
        ddt_ref[...] = ddt
        dal16_ref[...] += dal16
        ddtb_ref[...] += ddtb

    rev = lambda c: _NCH - 1 - c
    return pl.pallas_call(
        body, name=name,
        out_shape=(_sds((SEQ, CONV_CH)), _sds((SEQ, 1024)), _sds((SEQ, LANES)),
                   _sds((1, LANES)), _sds((1, LANES)), _sds((1, 1024)), _sds((1, 1024))),
        grid=(_NCH,),
        in_specs=[pl.BlockSpec((BLK, CONV_CH), lambda c: (rev(c), 0)), pl.BlockSpec((BLK, 1024), lambda c: (rev(c), ZB // 1024)),
                  pl.BlockSpec((BLK, LANES), lambda c: (rev(c), DTC // LANES)), pl.BlockSpec((1, BLK, 1024), lambda c: (rev(c), 0, 0)),
                  pl.BlockSpec((BLK, 1024), lambda c: (rev(c), 0))] + _ssd_param_specs(),
        out_specs=(pl.BlockSpec((BLK, CONV_CH), lambda c: (rev(c), 0)), pl.BlockSpec((BLK, 1024), lambda c: (rev(c), 0)),
                   pl.BlockSpec((BLK, LANES), lambda c: (rev(c), 0)),
                   _full((1, LANES)), _full((1, LANES)), _full((1, 1024)), _full((1, 1024))),
        scratch_shapes=[pltpu.VMEM((BLK, 1024), F32)], compiler_params=_cp())(xbc_act, proj, proj, hin, dyb, al16, dtb, dskx, nw)


def _rstd(v):
    return lax.rsqrt(jnp.mean(v * v, axis=1, keepdims=True) + EPS)


def _rms_bwd(dn, n, rstd):
    return rstd * (dn - n * jnp.mean(dn * n, axis=1, keepdims=True))


_VEC = _full((1, D))


def _layer_spec(layer):
    return pl.BlockSpec((None, 2048, D), lambda *_: (layer, 0, 0))

_ROW = pl.BlockSpec((TM, D), lambda i, *_: (i, 0))


def _proj_fwd(x, pre_w, scale, shift, w, layer, name):
    tn = 1024

    def body(x_ref, pw_ref, sc_ref, sh_ref, w_ref, o_ref, h_ref):
        @pl.when(pl.program_id(1) == 0)
        def _():
            xv = x_ref[...]
            h = (xv * _rstd(xv) * pw_ref[...]) * (1.0 + sc_ref[...]) + sh_ref[...]
            h_ref[...] = h.astype(h_ref.dtype)
        o_ref[...] = jnp.dot(h_ref[...], w_ref[...].astype(MXU), preferred_element_type=F32)

    return pl.pallas_call(body, name=name, out_shape=(_sds((SEQ, NP)), _sds((SEQ, D), MXU)), grid=(SEQ // TM, NP // tn),
                          in_specs=[_ROW, _VEC, _VEC, _VEC, pl.BlockSpec((None, D, tn), lambda i, j: (layer, 0, j))],
                          out_specs=(pl.BlockSpec((TM, tn), lambda i, j: (i, j)), _ROW), compiler_params=_cp())(x, pre_w, scale, shift, w)


def _out_fwd(ya, yb, yc, w, layer, x, gate, post_w, name):
    def body(ya_ref, yb_ref, yc_ref, w_ref, x_ref, g_ref, pw_ref, xn_ref, y_ref):
        y = _mm(ya_ref[...], w_ref[0:512, :]) + _mm(yb_ref[...], w_ref[512:1536, :]) + _mm(yc_ref[...], w_ref[1536:2048, :])
        y_ref[...] = y
        xn_ref[...] = x_ref[...] + g_ref[...] * (y * _rstd(y) * pw_ref[...])

    half = pl.BlockSpec((TM, 512), lambda i: (i, 0))
    return pl.pallas_call(body, name=name, out_shape=(_sds((SEQ, D)), _sds((SEQ, D))), grid=(SEQ // TM,),
                          in_specs=[half, _ROW, half, _layer_spec(layer), _ROW, _VEC, _VEC],
                          out_specs=(_ROW, _ROW), compiler_params=_cp())(ya, yb, yc, w, x, gate, post_w)


def _post_bwd(dxo, y, gate, post_w, name):
    def body(dx_ref, y_ref, g_ref, pw_ref, dy_ref, dg_ref, dpw_ref):
        @pl.when(pl.program_id(0) == 0)
        def _():
            dg_ref[...] = jnp.zeros_like(dg_ref)
            dpw_ref[...] = jnp.zeros_like(dpw_ref)
        dx, y = dx_ref[...], y_ref[...]
        rstd = _rstd(y)
        n = y * rstd
        dg_ref[...] += jnp.sum(dx * (n * pw_ref[...]), axis=0, keepdims=True)
        dr = dx * g_ref[...]
        dpw_ref[...] += jnp.sum(dr * n, axis=0, keepdims=True)
        dy_ref[...] = _rms_bwd(dr * pw_ref[...], n, rstd)

    return pl.pallas_call(body, name=name, out_shape=(_sds((SEQ, D)), _sds((1, D)), _sds((1, D))), grid=(SEQ // TM,),
                          in_specs=[_ROW, _ROW, _VEC, _VEC], out_specs=(_ROW, _VEC, _VEC), compiler_params=_cp())(dxo, y, gate, post_w)


def _dymix(dy, w, layer, name):
    def body(dy_ref, w_ref, a_ref, b_ref, c_ref):
        dy = dy_ref[...]
        a_ref[...] = _mm(dy, w_ref[0:512, :], NT)
        b_ref[...] = _mm(dy, w_ref[512:1536, :], NT)
        c_ref[...] = _mm(dy, w_ref[1536:2048, :], NT)

    half = pl.BlockSpec((TM, 512), lambda i: (i, 0))
    return pl.pallas_call(body, name=name, out_shape=(_sds((SEQ, 512)), _sds((SEQ, D)), _sds((SEQ, 512))), grid=(SEQ // TM,),
                          in_specs=[_ROW, _layer_spec(layer)], out_specs=(half, _ROW, half), compiler_params=_cp())(dy, w)


def _dwout(ya, yb, yc, dy, name):
    def body(ya_ref, yb_ref, yc_ref, dy_ref, o_ref):
        @pl.when(pl.program_id(0) == 0)
        def _():
            o_ref[...] = jnp.zeros_like(o_ref)
        dy = dy_ref[...]
        o_ref[0:512, :] += _mm(ya_ref[...], dy, TN)
        o_ref[512:1536, :] += _mm(yb_ref[...], dy, TN)
        o_ref[1536:2048, :] += _mm(yc_ref[...], dy, TN)

    half = pl.BlockSpec((TM, 512), lambda i: (i, 0))
    return pl.pallas_call(body, name=name, out_shape=_sds((2048, D)), grid=(SEQ // TM,),
                          in_specs=[half, _ROW, half, _ROW], out_specs=_full((2048, D)), compiler_params=_cp())(ya, yb, yc, dy)


_TK = 1536


def _dwin(h, dproj, name):
    def body(h_ref, dp_ref, o_ref):
        @pl.when(pl.program_id(1) == 0)
        def _():
            o_ref[...] = jnp.zeros_like(o_ref)
        o_ref[...] += _mm(h_ref[...], dp_ref[...], TN)

    return pl.pallas_call(body, name=name, out_shape=_sds((D, NP)), grid=(NP // _TK, SEQ // TM),
                          in_specs=[pl.BlockSpec((TM, D), lambda j, k: (k, 0)), pl.BlockSpec((TM, _TK), lambda j, k: (k, j))],
                          out_specs=pl.BlockSpec((D, _TK), lambda j, k: (0, j)), compiler_params=_cp())(h, dproj)


def _dh_bwd(dproj, w, layer, x, pre_w, scale, dxo, name):
    nk = NP // _TK

    def body(dp_ref, w_ref, x_ref, pw_ref, sc_ref, dxo_ref, dx_ref, dsh_ref, dsc_ref, dpw_ref, acc):
        i, k = pl.program_id(0), pl.program_id(1)

        @pl.when((i == 0) & (k == 0))
        def _():
            for r in (dsh_ref, dsc_ref, dpw_ref):
                r[...] = jnp.zeros_like(r)

        @pl.when(k == 0)
        def _():
            acc[...] = jnp.zeros_like(acc)
        acc[...] += _mm(dp_ref[...], w_ref[...], NT)

        @pl.when(k == nk - 1)
        def _():
            dh, xv = acc[...], x_ref[...]
            rstd = _rstd(xv)
            n = xv * rstd
            dsh_ref[...] += jnp.sum(dh, axis=0, keepdims=True)
            dsc_ref[...] += jnp.sum(dh * (n * pw_ref[...]), axis=0, keepdims=True)
            dhn = dh * (1.0 + sc_ref[...])
            dpw_ref[...] += jnp.sum(dhn * n, axis=0, keepdims=True)
            dx_ref[...] = _rms_bwd(dhn * pw_ref[...], n, rstd) + dxo_ref[...]

    return pl.pallas_call(body, name=name, out_shape=(_sds((SEQ, D)), _sds((1, D)), _sds((1, D)), _sds((1, D))),
                          grid=(SEQ // TM, nk),
                          in_specs=[pl.BlockSpec((TM, _TK), lambda i, k: (i, k)), pl.BlockSpec((None, D, _TK), lambda i, k: (layer, 0, k)),
                                    _ROW, _VEC, _VEC, _ROW],
                          out_specs=(_ROW, _VEC, _VEC, _VEC), scratch_shapes=[pltpu.VMEM((TM, D), F32)],
                          compiler_params=_cp())(dproj, w, x, pre_w, scale, dxo)


def _loss_bwd(xf, tgt, name):
    def body(x_ref, t_ref, dx_ref, l_ref):
        @pl.when(pl.program_id(0) == 0)
        def _():
            l_ref[...] = jnp.zeros_like(l_ref)
        e = x_ref[...] - t_ref[...]
        dx_ref[...] = e * (1.0 / D)
        l_ref[...] += 0.5 * jnp.sum(jnp.mean(e * e, axis=1, keepdims=True), axis=0, keepdims=True)

    return pl.pallas_call(body, name=name, out_shape=(_sds((SEQ, D)), _sds((8, LANES))), grid=(SEQ // TM,),
                          in_specs=[_ROW, _ROW], out_specs=(_ROW, _full((8, LANES))), compiler_params=_cp())(xf, tgt)


def _mod_part(c_all, ada_w, ada_b, name):
    def body(c_ref, w_ref, b_ref, o_ref):
        o_ref[0] = _mm(_silu(c_ref[...]), w_ref[0]) + b_ref[0]

    return pl.pallas_call(body, name=name, out_shape=_sds((DEPTH, 8, 768)), grid=(DEPTH,),
                          in_specs=[_full((8, D)), pl.BlockSpec((1, D, 768), lambda i: (i, 0, 0)), pl.BlockSpec((1, 1, 768), lambda i: (i, 0, 0))],
                          out_specs=pl.BlockSpec((1, 8, 768), lambda i: (i, 0, 0)), compiler_params=_cp())(c_all, ada_w, ada_b)


def _ada_grad(c_t, dmod, name):
    def body(c_ref, d_ref, o_ref):
        ca = _silu(c_ref[...])
        dm = d_ref[0]
        acc = ca[:, 0:1] * dm[0:1, :]
        for s in range(1, 8):
            acc = acc + ca[:, s:s + 1] * dm[s:s + 1, :]
        o_ref[0] = acc

    return pl.pallas_call(body, name=name, out_shape=_sds((DEPTH, D, 768)), grid=(DEPTH,),
                          in_specs=[_full((D, LANES)), pl.BlockSpec((1, 8, 768), lambda i: (i, 0, 0))],
                          out_specs=pl.BlockSpec((1, D, 768), lambda i: (i, 0, 0)), compiler_params=_cp())(c_t, dmod)


def _pack(parts):
    flat = []
    for p in parts:
        f = p.reshape(-1)
        flat.append(jnp.pad(f, (0, (-f.size) % LANES)))
    v = jnp.concatenate(flat)
    return jnp.pad(v, (0, (-v.size) % (8 * LANES))).reshape(-1, LANES)


def _unpack(v, shapes):
    v = v.reshape(-1)
    out, off = [], 0
    for s in shapes:
        n = math.prod(s)
        out.append(v[off:off + n].reshape(s))
        off += n + (-n) % LANES
    return out


_GIVEN_DT, _GIVEN_C = 4608, 4624


def _pad_cols(w):
    return jnp.concatenate([w[..., :_GIVEN_DT], w[..., _GIVEN_C:], w[..., _GIVEN_DT:_GIVEN_C],
                            jnp.zeros(w.shape[:-1] + (NP - IN_COLS,), w.dtype)], axis=-1)


def _unpad_cols(w):
    return jnp.concatenate([w[..., :_GIVEN_DT], w[..., DTC:DTC + 16], w[..., _GIVEN_DT:DTC]], axis=-1)


def _pad_lanes(v):
    return jnp.pad(v, (0, LANES - v.shape[0])).reshape(1, LANES)


def _local_step(x2, tgt, mod, w_p, w_o, pre_w, post_w, conv_w, conv_b, dt_bias, a_log, d_skip, nw, sinks):
    saved = []
    xcur = x2
    for i in range(DEPTH):
        shift, scale, gate = mod[i:i + 1, :D], mod[i:i + 1, D:2 * D], mod[i:i + 1, 2 * D:]
        pw, qw = pre_w[i:i + 1], post_w[i:i + 1]
        proj, h = _proj_fwd(xcur, pw, scale, shift, w_p, i, "proj_fwd")
        o_a, lse_a = _attn_fwd(proj, QA // LANES, KA // LANES, VA // LANES, DILS, False, None, "attn_a_fwd")
        ya = _gate_fwd(o_a, proj, ZA // 512, "gate_a_fwd")
        sink_x = jnp.repeat(sinks[i], HD).reshape(1, 512)
        o_c, lse_c = _attn_fwd(proj, QC // LANES, KC // LANES, VC // LANES, (1,), True, sink_x, "attn_c_fwd")
        yc = _gate_fwd(o_c, proj, ZC // 512, "gate_c_fwd")
        cw, cb = conv_w[i], conv_b[i:i + 1]
        xbc_act = _conv_fwd(proj, cw, cb, "conv_fwd")
        ssd_p = (_pad_lanes(a_log[i]), _pad_lanes(dt_bias[i]), jnp.repeat(d_skip[i], HD).reshape(1, 1024), nw[i:i + 1])
        yb, hin = _ssd_fwd(xbc_act, proj, *ssd_p, "ssd_fwd")
        xnew, y = _out_fwd(ya, yb, yc, w_o, i, xcur, gate, qw, "out_fwd")
        saved.append((xcur, scale, gate, pw, qw, proj, h, o_a, lse_a, ya, sink_x, o_c, lse_c, yc, cw, cb, xbc_act, ssd_p, yb, hin, y))
        xcur = xnew
    dx, ltile = _loss_bwd(xcur, tgt, "loss")
    dwi, dwo, dmod, small = [None] * DEPTH, [None] * DEPTH, [None] * DEPTH, [None] * DEPTH
    for i in reversed(range(DEPTH)):
        xin, scale, gate, pw, qw, proj, h, o_a, lse_a, ya, sink_x, o_c, lse_c, yc, cw, cb, xbc_act, ssd_p, yb, hin, y = saved[i]
        dy, dgate, dpost = _post_bwd(dx, y, gate, qw, "post_bwd")
        dya, dyb, dyc = _dymix(dy, w_o, i, "dymix")
        dwo[i] = _dwout(ya, yb, yc, dy, "dwout")
        do_a, dz_a = _gate_bwd(dya, o_a, proj, ZA // 512, "gate_a_bwd")
        dq_a, dk_a, dv_a = _attn_bwd(proj, QA // LANES, KA // LANES, VA // LANES, do_a, o_a, lse_a, DILS, False, None, "attn_a_bwd")
        do_c, dz_c = _gate_bwd(dyc, o_c, proj, ZC // 512, "gate_c_bwd")
        dq_c, dk_c, dv_c, dsk = _attn_bwd(proj, QC // LANES, KC // LANES, VC // LANES, do_c, o_c, lse_c, (1,), True, sink_x, "attn_c_bwd")
        dxbc_act, dz_b, ddt, dal16, ddtb, ddsk, dnw = _ssd_bwd(xbc_act, proj, hin, dyb, *ssd_p, "ssd_bwd")
        dxbc, dcw, dcb = _conv_bwd(proj, dxbc_act, cw, cb, "conv_bwd")
        dproj = jnp.concatenate([dq_a, dk_a, dv_a, dz_a, dz_b, dxbc, dq_c, dz_c, dk_c, dv_c, ddt,
                                 jnp.zeros((SEQ, NP - DTC - LANES), F32)], axis=1)
        dwi[i] = _dwin(h, dproj, "dwin")
        dx, dshift, dscale, dpre = _dh_bwd(dproj, w_p, i, xin, pw, scale, dx, "dh_bwd")
        dmod[i] = jnp.concatenate([dshift, dscale, dgate], axis=1)
        small[i] = (dpre, dpost, dcw, dcb, ddtb[0, :16], dal16[0, :16], ddsk.reshape(16, HD).sum(axis=1), dnw, dsk[:, 0, ::HD].reshape(8))
    return ltile, dx, jnp.stack(dwi), jnp.stack(dwo), jnp.concatenate(dmod, axis=0), small


_SMALL = ((1, D), (1, D), (4, CONV_CH), (1, CONV_CH), (16,), (16,), (16,), (1, D), (8,))


def kernel(x, c, ada_w, ada_b, pre_norm_w, post_norm_w, w_in, conv_w, conv_b, dt_bias, a_log, d_skip, ssm_norm_w, sinks, w_out, loss_target, m_ada_w, m_ada_b, m_pre_norm_w, m_post_norm_w, m_w_in, m_conv_w, m_conv_b, m_dt_bias, m_a_log, m_d_skip, m_ssm_norm_w, m_sinks, m_w_out, v_ada_w, v_ada_b, v_pre_norm_w, v_post_norm_w, v_w_in, v_conv_w, v_conv_b, v_dt_bias, v_a_log, v_d_skip, v_ssm_norm_w, v_sinks, v_w_out):
    xi, yi, ci = lax.axis_index("x"), lax.axis_index("y"), lax.axis_index("c")
    chip = 2 * xi + yi
    me = 2 * chip + ci

    w_in_b = _cast_bf16(w_in.reshape(DEPTH * D, SHARD_IN), 512, "cast_w_in").reshape(DEPTH, D, SHARD_IN)
    w_out_b = _cast_bf16(w_out.reshape(DEPTH * 512, D), 512, "cast_w_out").reshape(DEPTH, 512, D)
    g_in, g_out = _exchange4([w_in_b, w_out_b], False, "gather_weights")
    w_p = _pad_cols(jnp.concatenate([g_in[k] for k in range(4)], axis=-1))
    w_o = jnp.concatenate([g_out[k] for k in range(4)], axis=1)

    g0 = _allgather8(_pack([c, conv_w]), "gather_c")
    c_all = g0[:, :8, :].reshape(8, D)
    conv_w_full = jnp.concatenate([g0[2 * k, 8:56, :].reshape(DEPTH, 4, CONV_CH // 4) for k in range(4)], axis=-1)

    ada_b_mine = lax.dynamic_slice_in_dim(ada_b, 768 * chip, 768, axis=1).reshape(DEPTH, 1, 768)
    gm = _allgather8(_mod_part(c_all, ada_w, ada_b_mine, "mod_part").reshape(DEPTH * 8, 768), "gather_mod")
    gm = gm.reshape(4, 2, DEPTH, 8, 768)[:, 0]
    mod = lax.dynamic_index_in_dim(gm, me, axis=2, keepdims=False).transpose(1, 0, 2).reshape(DEPTH, 3 * D)

    ltile, dx, dwi_p, dwo, dmod, small = _local_step(x[0], loss_target[0], mod, w_p, w_o, pre_norm_w, post_norm_w, conv_w_full,
                                                     conv_b, dt_bias, a_log, d_skip, ssm_norm_w, sinks)

    packed = _pack([dmod] + [g for layer in small for g in layer] + [ltile[0]])
    gs = _allgather8(packed, "gather_small")
    tot = _sum_blocks(gs, packed.shape[0], "sum_small")
    parts = _unpack(tot, [(DEPTH, 3 * D)] + list(_SMALL) * DEPTH + [(LANES,)])
    g_ada_b, loss = parts[0], parts[-1][0]
    per_layer = [parts[1 + len(_SMALL) * i:1 + len(_SMALL) * (i + 1)] for i in range(DEPTH)]
    g_pre, g_post, g_cw, g_cb, g_dtb, g_al, g_dsk, g_nw, g_sk = [jnp.stack([per_layer[i][j] for i in range(DEPTH)]) for j in range(len(_SMALL))]
    g_pre, g_post, g_cb, g_nw = g_pre[:, 0], g_post[:, 0], g_cb[:, 0], g_nw[:, 0]
    g_cw = lax.dynamic_slice_in_dim(g_cw, (CONV_CH // 4) * chip, CONV_CH // 4, axis=2)

    dmod_all = gs[:, :(DEPTH * 3 * D) // LANES, :].reshape(8, DEPTH, 3 * D).transpose(1, 0, 2)
    dmod_mine = lax.dynamic_slice_in_dim(dmod_all, 768 * chip, 768, axis=2)
    c_t = jnp.pad(c_all.T, ((0, 0), (0, LANES - 8)))
    g_ada_w = _ada_grad(c_t, dmod_mine, "ada_grad")

    dwi = _unpad_cols(dwi_p)
    blk_in = jnp.stack([dwi[..., SHARD_IN * k:SHARD_IN * (k + 1)] for k in range(4)]).astype(jnp.bfloat16)
    blk_out = jnp.stack([dwo[:, 512 * k:512 * (k + 1), :] for k in range(4)]).astype(jnp.bfloat16)
    r_in, r_out = _exchange4([blk_in, blk_out], True, "scatter_grads")
    p_in = _sum_blocks(r_in.reshape(4, DEPTH * D, SHARD_IN), 256, "sum_w_in")
    p_out = _sum_blocks(r_out.reshape(4, DEPTH * 512, D), 512, "sum_w_out")
    s_in, s_out = _sibling_swap([p_in, p_out], "swap_partials")

    res = {}
    res["ada_w"] = [a.reshape(DEPTH, D, 768) for a in
                    _adamw(ada_w.reshape(DEPTH * D, 768), [g_ada_w.reshape(DEPTH * D, 768)], m_ada_w.reshape(DEPTH * D, 768),
                           v_ada_w.reshape(DEPTH * D, 768), 512, "adamw_ada_w")]
    res["w_in"] = [a.reshape(DEPTH, D, SHARD_IN) for a in
                   _adamw(w_in.reshape(DEPTH * D, SHARD_IN), [p_in, s_in], m_w_in.reshape(DEPTH * D, SHARD_IN),
                          v_w_in.reshape(DEPTH * D, SHARD_IN), 256, "adamw_w_in")]
    res["w_out"] = [a.reshape(DEPTH, 512, D) for a in
                    _adamw(w_out.reshape(DEPTH * 512, D), [p_out, s_out], m_w_out.reshape(DEPTH * 512, D),
                           v_w_out.reshape(DEPTH * 512, D), 512, "adamw_w_out")]
    names = ["ada_b", "pre_norm_w", "post_norm_w", "conv_w", "conv_b", "dt_bias", "a_log", "d_skip", "ssm_norm_w", "sinks"]
    ws = [ada_b, pre_norm_w, post_norm_w, conv_w, conv_b, dt_bias, a_log, d_skip, ssm_norm_w, sinks]
    gsm = [g_ada_b, g_pre, g_post, g_cw, g_cb, g_dtb, g_al, g_dsk, g_nw, g_sk]
    ms = [m_ada_b, m_pre_norm_w, m_post_norm_w, m_conv_w, m_conv_b, m_dt_bias, m_a_log, m_d_skip, m_ssm_norm_w, m_sinks]
    vs = [v_ada_b, v_pre_norm_w, v_post_norm_w, v_conv_w, v_conv_b, v_dt_bias, v_a_log, v_d_skip, v_ssm_norm_w, v_sinks]
    pw_, pg_, pm_, pv_ = _pack(ws), _pack(gsm), _pack(ms), _pack(vs)
    small_out = _adamw(pw_, [pg_], pm_, pv_, pw_.shape[0], "adamw_small")
    shapes = [w.shape for w in ws]
    for kind in range(4):
        for nm, a in zip(names, _unpack(small_out[kind], shapes)):
            res.setdefault(nm, [None] * 4)[kind] = a
    order = ["ada_w", "ada_b", "pre_norm_w", "post_norm_w", "w_in", "conv_w", "conv_b", "dt_bias", "a_log", "d_skip", "ssm_norm_w", "sinks", "w_out"]
    return (loss, dx[None], *[res[n][0] for n in order], *[res[n][1] for n in order], *[res[n][2] for n in order], *[res[n][3] for n in order])
```

```python
import math

import jax
import jax.numpy as jnp
from jax import lax
from jax.experimental import pallas as pl
from jax.experimental.pallas import tpu as pltpu

F32 = jnp.float32
MXU = jnp.bfloat16
HI = lax.Precision.HIGHEST
MESH = pl.DeviceIdType.MESH

SEQ = 4096
D = 1024
DEPTH = 4
HD = 64
LANES = 128
BLK = 128
DILS = (1, 4, 16)
NEG = -1e30
EPS = 1e-6
MIB = 1024 * 1024

NP = 6144
QA, KA, VA, ZA = 0, 512, 1024, 1536
ZB, XBC = 2048, 3072
QC, ZC, KC, VC = 4608, 5120, 5632, 5760
DTC = 5888
IN_COLS = 5904
SHARD_IN = IN_COLS // 4
CONV_CH = 1536
TM = 512

ADAM_LR, ADAM_B1, ADAM_B2, ADAM_EPS, ADAM_WD, ADAM_STEP = 0.001, 0.9, 0.999, 1e-08, 0.01, 10

NT = (((1,), (1,)), ((), ()))
TN = (((0,), (0,)), ((), ()))


def _cp(vmem_mib=48):
    return pltpu.CompilerParams(vmem_limit_bytes=vmem_mib * MIB)


def _sds(shape, dtype=F32):
    return jax.ShapeDtypeStruct(shape, dtype)


def _full(shape):
    n = len(shape)
    return pl.BlockSpec(shape, lambda *_: (0,) * n)


def _mm(a, b, dims=None):
    if dims is None:
        return jnp.dot(a.astype(MXU), b.astype(MXU), preferred_element_type=F32)
    return lax.dot_general(a.astype(MXU), b.astype(MXU), dims, preferred_element_type=F32)


def _sigmoid(x):
    return 1.0 / (1.0 + jnp.exp(-x))


def _silu(x):
    return x * _sigmoid(x)


def _dsilu(x):
    s = _sigmoid(x)
    return s * (1.0 + x * (1.0 - s))


def _softplus(x):
    ax = jnp.where(x >= 0, x, -x)
    return jnp.maximum(x, 0.0) + jnp.log1p(jnp.exp(-ax))


def _half_masks():
    lane = lax.broadcasted_iota(jnp.int32, (1, LANES), 1)
    m0 = (lane < HD).astype(F32)
    return m0, 1.0 - m0


def _allgather8(v, name):
    r, cc = v.shape

    def body(v_ref, out_ref, send_sems, recv_sems):
        x, y, c = lax.axis_index("x"), lax.axis_index("y"), lax.axis_index("c")
        me = 4 * x + 2 * y + c
        out_ref[me] = v_ref[...]
        peers = []
        for k in range(1, 8):
            px = 1 - x if k & 4 else x
            py = 1 - y if k & 2 else y
            pc = 1 - c if k & 1 else c
            peers.append((px, py, pc))
        sends = []
        for k, peer in enumerate(peers):
            cp = pltpu.make_async_remote_copy(src_ref=v_ref, dst_ref=out_ref.at[me], send_sem=send_sems.at[k],
                                              recv_sem=recv_sems.at[k], device_id=peer, device_id_type=MESH)
            cp.start()
            sends.append(cp)
        for k, (px, py, pc) in enumerate(peers):
            pltpu.make_async_remote_copy(src_ref=v_ref, dst_ref=out_ref.at[4 * px + 2 * py + pc], send_sem=send_sems.at[k],
                                         recv_sem=recv_sems.at[k], device_id=(px, py, pc), device_id_type=MESH).wait_recv()
        for cp in sends:
            cp.wait_send()

    return pl.pallas_call(
        body, name=name, out_shape=_sds((8, r, cc)),
        in_specs=[pl.BlockSpec(memory_space=pltpu.VMEM)], out_specs=pl.BlockSpec(memory_space=pltpu.VMEM),
        scratch_shapes=[pltpu.SemaphoreType.DMA((7,)), pltpu.SemaphoreType.DMA((7,))],
        compiler_params=_cp(32),
    )(v)


_HBM = pl.BlockSpec(memory_space=pltpu.HBM)
_SEM = pl.BlockSpec(memory_space=pltpu.SEMAPHORE)
_EFFECT = pltpu.SideEffectType.DATAFLOW_SIDE_EFFECTING


def _chip_copies(src_refs, land_refs, send_sems, recv_sems):
    x, y, c = lax.axis_index("x"), lax.axis_index("y"), lax.axis_index("c")
    mine = 2 * x + y
    out = []
    for i, land in enumerate(land_refs):
        for j, (px, py) in enumerate([(1 - x, y), (x, 1 - y), (1 - x, 1 - y)]):
            src = src_refs[i].at[2 * px + py] if src_refs else land.at[mine]
            mk = lambda dst, i=i, j=j, src=src, px=px, py=py: pltpu.make_async_remote_copy(
                src_ref=src, dst_ref=dst, send_sem=send_sems.at[3 * i + j], recv_sem=recv_sems.at[3 * i + j],
                device_id=(px, py, c), device_id_type=MESH)
            out.append((mk(land.at[mine]), mk(land.at[2 * px + py])))
    return out


def _split_start(srcs, lands, name):
    ops = list(srcs or []) + list(lands)
    ns, n = len(srcs or []), len(lands)

    def body(*refs):
        src_refs, land_refs = refs[:ns], refs[ns:ns + n]
        send_sems, recv_sems = refs[ns + n], refs[ns + n + 1]
        for mine_out, _ in _chip_copies(src_refs, land_refs, send_sems, recv_sems):
            mine_out.start()
        refs[-1][...] = jnp.zeros_like(refs[-1])

    sems = pltpu.SemaphoreType.DMA((3 * n,))
    res = pl.pallas_call(
        body, name=name, out_shape=(sems, sems) + tuple(pltpu.HBM(a.shape, a.dtype) for a in ops) + (_sds((8, LANES)),),
        in_specs=[_HBM] * len(ops), out_specs=(_SEM, _SEM) + (_HBM,) * len(ops) + (pl.BlockSpec(memory_space=pltpu.VMEM),),
        input_output_aliases={k: 2 + k for k in range(len(ops))},
        compiler_params=pltpu.CompilerParams(has_side_effects=_EFFECT),
    )(*[pltpu.with_memory_space_constraint(a, pltpu.HBM) for a in ops])
    return res[0], res[1], list(res[2:2 + len(ops)]), res[-1]


def _split_wait(send_sems, recv_sems, thru, n, after, name):
    ns = len(thru) - n

    def body(*refs):
        src_refs, land_refs = refs[:ns], refs[ns:ns + n]
        for mine_out, arriving in _chip_copies(src_refs, land_refs, refs[ns + n], refs[ns + n + 1]):
            mine_out.wait_send()
            arriving.wait_recv()

    res = pl.pallas_call(
        body, name=name, out_shape=tuple(pltpu.HBM(a.shape, a.dtype) for a in thru),
        in_specs=[_HBM] * len(thru) + [_SEM, _SEM, pl.BlockSpec(memory_space=pl.ANY)], out_specs=(_HBM,) * len(thru),
        input_output_aliases={k: k for k in range(len(thru))},
        compiler_params=pltpu.CompilerParams(has_side_effects=_EFFECT),
    )(*thru, send_sems, recv_sems, after)
    return list(res)


def _sibling_swap(arrs, name):
    n = len(arrs)

    def body(*refs):
        ins, outs_, (send_sems, recv_sems) = refs[:n], refs[n:2 * n], refs[2 * n:]
        sib = (lax.axis_index("x"), lax.axis_index("y"), 1 - lax.axis_index("c"))
        cps = [pltpu.make_async_remote_copy(src_ref=ins[i], dst_ref=outs_[i], send_sem=send_sems.at[i], recv_sem=recv_sems.at[i],
                                            device_id=sib, device_id_type=MESH) for i in range(n)]
        for cp in cps:
            cp.start()
        for cp in cps:
            cp.wait_recv()
        for cp in cps:
            cp.wait_send()

    hbm = pl.BlockSpec(memory_space=pltpu.HBM)
    return pl.pallas_call(
        body, name=name, out_shape=tuple(_sds(a.shape, a.dtype) for a in arrs), in_specs=[hbm] * n, out_specs=tuple([hbm] * n),
        scratch_shapes=[pltpu.SemaphoreType.DMA((n,)), pltpu.SemaphoreType.DMA((n,))],
    )(*arrs)


def _cast_bf16(a, rows, name):
    r, cc = a.shape

    def body(a_ref, o_ref):
        o_ref[...] = a_ref[...].astype(jnp.bfloat16)

    return pl.pallas_call(body, name=name, out_shape=_sds((r, cc), jnp.bfloat16), grid=(r // rows,),
                          in_specs=[pl.BlockSpec((rows, cc), lambda i: (i, 0))],
                          out_specs=pl.BlockSpec((rows, cc), lambda i: (i, 0)), compiler_params=_cp())(a)


def _sum_blocks(a, rows, name):
    k, r, cc = a.shape

    def body(a_ref, o_ref):
        acc = a_ref[0].astype(F32)
        for j in range(1, k):
            acc = acc + a_ref[j].astype(F32)
        o_ref[...] = acc

    return pl.pallas_call(body, name=name, out_shape=_sds((r, cc)), grid=(r // rows,),
                          in_specs=[pl.BlockSpec((k, rows, cc), lambda i: (0, i, 0))],
                          out_specs=pl.BlockSpec((rows, cc), lambda i: (i, 0)), compiler_params=_cp())(a)


def _adamw(w, parts, m, v, rows, name):
    r, cc = w.shape
    np_ = len(parts)
    c1 = 1.0 / (1.0 - ADAM_B1 ** ADAM_STEP)
    c2 = 1.0 / (1.0 - ADAM_B2 ** ADAM_STEP)

    def body(*refs):
        w_ref, p_refs, (m_ref, v_ref, g_ref, d_ref, nm_ref, nv_ref) = refs[0], refs[1:1 + np_], refs[1 + np_:]
        g = p_refs[0][...]
        for p_ref in p_refs[1:]:
            g = g + p_ref[...]
        nm = ADAM_B1 * m_ref[...] + (1.0 - ADAM_B1) * g
        nv = ADAM_B2 * v_ref[...] + (1.0 - ADAM_B2) * (g * g)
        g_ref[...] = g
        nm_ref[...] = nm
        nv_ref[...] = nv
        d_ref[...] = -ADAM_LR * ((nm * c1) / (jnp.sqrt(nv * c2) + ADAM_EPS) + ADAM_WD * w_ref[...])

    spec = pl.BlockSpec((rows, cc), lambda i: (i, 0))
    return pl.pallas_call(body, name=name, out_shape=(_sds((r, cc)),) * 4, grid=(r // rows,),
                          in_specs=[spec] * (3 + np_), out_specs=(spec,) * 4, compiler_params=_cp())(w, *parts, m, v)


def _gate_fwd(o, proj, zblk, name):
    def body(o_ref, z_ref, y_ref):
        y_ref[...] = o_ref[...] * _silu(z_ref[...])

    return pl.pallas_call(body, name=name, out_shape=_sds((SEQ, 512)), grid=(SEQ // TM,),
                          in_specs=[pl.BlockSpec((TM, 512), lambda i: (i, 0)), pl.BlockSpec((TM, 512), lambda i: (i, zblk))],
                          out_specs=pl.BlockSpec((TM, 512), lambda i: (i, 0)), compiler_params=_cp())(o, proj)


def _gate_bwd(dy, o, proj, zblk, name):
    def body(dy_ref, o_ref, z_ref, do_ref, dz_ref):
        dy, z = dy_ref[...], z_ref[...]
        do_ref[...] = dy * _silu(z)
        dz_ref[...] = dy * o_ref[...] * _dsilu(z)

    return pl.pallas_call(body, name=name, out_shape=(_sds((SEQ, 512)), _sds((SEQ, 512))), grid=(SEQ // TM,),
                          in_specs=[pl.BlockSpec((TM, 512), lambda i: (i, 0)), pl.BlockSpec((TM, 512), lambda i: (i, 0)),
                                    pl.BlockSpec((TM, 512), lambda i: (i, zblk))],
                          out_specs=(pl.BlockSpec((TM, 512), lambda i: (i, 0)),) * 2, compiler_params=_cp())(dy, o, proj)


def _band_valid(lo):
    qi = lax.broadcasted_iota(jnp.int32, (BLK, 2 * BLK), 0)
    kj = lax.broadcasted_iota(jnp.int32, (BLK, 2 * BLK), 1)
    dist = BLK + qi - kj
    return (dist >= 0) & (dist <= BLK) & (kj >= lo)


def _rows(st, dil):
    if dil == 1:
        return pl.ds(pl.multiple_of(st, BLK), BLK)
    return pl.ds(st, BLK, stride=dil)


def _block_pos(n, dil):
    nb = SEQ // (dil * BLK)
    r, b = n // nb, n % nb
    hp = (b > 0).astype(jnp.int32)
    st = r + dil * BLK * b
    return st, st - dil * BLK * hp, BLK * (1 - hp)


def _attn_fwd(proj, qblk, kblk, vblk, dils, gqa, sink_x, name):
    has_sink = sink_x is not None

    def body(*refs):
        if has_sink:
            q_ref, k_ref, v_ref, s_ref, o_ref, lse_ref, m_scr, z_scr = refs
        else:
            q_ref, k_ref, v_ref, o_ref, lse_ref, m_scr, z_scr = refs
        m0, m1 = _half_masks()
        g = pl.program_id(0) // 2
        o_ref[...] = jnp.zeros_like(o_ref)
        z_scr[...] = jnp.zeros_like(z_scr)
        m_scr[...] = jnp.full_like(m_scr, NEG)
        for dil in dils:
            def step(n, carry, dil=dil):
                st, stp, lo = _block_pos(n, dil)
                rq, rp = _rows(st, dil), _rows(stp, dil)
                valid = _band_valid(lo)
                q = q_ref[rq, :]
                kk = jnp.concatenate([k_ref[rp, :], k_ref[rq, :]], axis=0)
                vv = jnp.concatenate([v_ref[rp, :], v_ref[rq, :]], axis=0)
                if gqa:
                    kr, vr = pltpu.roll(kk, HD, axis=1), pltpu.roll(vv, HD, axis=1)
                m_pair = l_pair = o_pair = 0.0
                for a, msk in enumerate((m0, m1)):
                    if gqa:
                        w = (g == a).astype(F32)
                        ka, va = kk * w + kr * (1.0 - w), vv * w + vr * (1.0 - w)
                    else:
                        ka, va = kk, vv
                    s = _mm(q * msk, ka, NT) * (HD ** -0.5)
                    s = jnp.where(valid, s, NEG)
                    m = jnp.max(s, axis=1, keepdims=True)
                    if has_sink:
                        sk = s_ref[...][:, HD * a:HD * a + 1]
                        m = jnp.maximum(m, sk)
                    p = jnp.exp(s - m)
                    l = jnp.sum(p, axis=1, keepdims=True)
                    if has_sink:
                        l = l + jnp.exp(sk - m)
                    o_pair = o_pair + _mm(p, va) * msk
                    m_pair = m_pair + m * msk
                    l_pair = l_pair + l * msk
                m_old = m_scr[rq, :]
                m_new = jnp.maximum(m_old, m_pair)
                alpha, beta = jnp.exp(m_old - m_new), jnp.exp(m_pair - m_new)
                o_ref[rq, :] = o_ref[rq, :] * alpha + o_pair * beta
                z_scr[rq, :] = z_scr[rq, :] * alpha + l_pair * beta
                m_scr[rq, :] = m_new
                return carry
            lax.fori_loop(0, SEQ // BLK, step, 0, unroll=4)

        def fin(t, carry):
            rt = pl.ds(pl.multiple_of(t * TM, TM), TM)
            z = z_scr[rt, :]
            o_ref[rt, :] = o_ref[rt, :] / z
            lse_ref[rt, :] = m_scr[rt, :] + jnp.log(z)
            return carry
        lax.fori_loop(0, SEQ // TM, fin, 0)

    col = lambda blk: pl.BlockSpec((SEQ, LANES), lambda p, blk=blk: (0, blk + p))
    kv = (lambda blk: pl.BlockSpec((SEQ, LANES), lambda p, blk=blk: (0, blk))) if gqa else col
    in_specs = [col(qblk), kv(kblk), kv(vblk)]
    args = [proj, proj, proj]
    if has_sink:
        in_specs.append(pl.BlockSpec((1, LANES), lambda p: (0, p)))
        args.append(sink_x)
    out = pl.BlockSpec((SEQ, LANES), lambda p: (0, p))
    return pl.pallas_call(body, name=name, out_shape=(_sds((SEQ, 512)), _sds((SEQ, 512))), grid=(4,),
                          in_specs=in_specs, out_specs=(out, out),
                          scratch_shapes=[pltpu.VMEM((SEQ, LANES), F32), pltpu.VMEM((SEQ, LANES), F32)],
                          compiler_params=_cp(48))(*args)


def _attn_bwd(proj, qblk, kblk, vblk, do, o, lse, dils, gqa, sink_x, name):
    has_sink = sink_x is not None

    def body(*refs):
        if has_sink:
            q_ref, k_ref, v_ref, do_ref, o_ref, lse_ref, s_ref, dq_ref, dk_ref, dv_ref, ds_ref = refs
        else:
            q_ref, k_ref, v_ref, do_ref, o_ref, lse_ref, dq_ref, dk_ref, dv_ref = refs
        m0, m1 = _half_masks()
        pid = pl.program_id(0)
        g = pid // 2
        dq_ref[...] = jnp.zeros_like(dq_ref)
        if gqa:
            @pl.when(pid == 0)
            def _():
                dk_ref[...] = jnp.zeros_like(dk_ref)
                dv_ref[...] = jnp.zeros_like(dv_ref)
        else:
            dk_ref[...] = jnp.zeros_like(dk_ref)
            dv_ref[...] = jnp.zeros_like(dv_ref)
        dsink = jnp.zeros((1, LANES), F32)
        for dil in dils:
            def step(n, dsink, dil=dil):
                st, stp, lo = _block_pos(n, dil)
                rq, rp = _rows(st, dil), _rows(stp, dil)
                valid = _band_valid(lo)
                q, do_, o_, lse_ = q_ref[rq, :], do_ref[rq, :], o_ref[rq, :], lse_ref[rq, :]
                kk = jnp.concatenate([k_ref[rp, :], k_ref[rq, :]], axis=0)
                vv = jnp.concatenate([v_ref[rp, :], v_ref[rq, :]], axis=0)
                if gqa:
                    kr, vr = pltpu.roll(kk, HD, axis=1), pltpu.roll(vv, HD, axis=1)
                dq_pair = 0.0
                dk_sum = dv_sum = 0.0
                for a, msk in enumerate((m0, m1)):
                    if gqa:
                        w = (g == a).astype(F32)
                        ka, va = kk * w + kr * (1.0 - w), vv * w + vr * (1.0 - w)
                    else:
                        ka, va = kk, vv
                    qa, doa = q * msk, do_ * msk
                    delta = jnp.sum(doa * o_, axis=1, keepdims=True)
                    lse_a = lse_[:, HD * a:HD * a + 1]
                    s = _mm(qa, ka, NT) * (HD ** -0.5)
                    s = jnp.where(valid, s, NEG)
                    p = jnp.exp(s - lse_a)
                    dp = _mm(doa, va, NT)
                    dsr = p * (dp - delta) * (HD ** -0.5)
                    dq_pair = dq_pair + _mm(dsr, ka) * msk
                    dka, dva = _mm(dsr, qa, TN), _mm(p, doa, TN)
                    if gqa:
                        dka = dka * w + pltpu.roll(dka, HD, axis=1) * (1.0 - w)
                        dva = dva * w + pltpu.roll(dva, HD, axis=1) * (1.0 - w)
                    dk_sum, dv_sum = dk_sum + dka, dv_sum + dva
                    if has_sink:
                        sk = s_ref[...][:, HD * a:HD * a + 1]
                        dsink = dsink - jnp.sum(jnp.exp(sk - lse_a) * delta, axis=0, keepdims=True) * msk
                dq_ref[rq, :] += dq_pair
                dk_ref[rp, :] += dk_sum[:BLK]
                dk_ref[rq, :] += dk_sum[BLK:]
                dv_ref[rp, :] += dv_sum[:BLK]
                dv_ref[rq, :] += dv_sum[BLK:]
                return dsink
            dsink = lax.fori_loop(0, SEQ // BLK, step, dsink, unroll=2)
        if has_sink:
            ds_ref[0] = jnp.broadcast_to(dsink, (8, LANES))

    col = lambda blk: pl.BlockSpec((SEQ, LANES), lambda p, blk=blk: (0, blk + p))
    kv = (lambda blk: pl.BlockSpec((SEQ, LANES), lambda p, blk=blk: (0, blk))) if gqa else col
    pair = pl.BlockSpec((SEQ, LANES), lambda p: (0, p))
    in_specs = [col(qblk), kv(kblk), kv(vblk), pair, pair, pair]
    args = [proj, proj, proj, do, o, lse]
    kvw = LANES if gqa else 512
    kv_out = pl.BlockSpec((SEQ, LANES), lambda p: (0, 0)) if gqa else pair
    out_shape = [_sds((SEQ, 512)), _sds((SEQ, kvw)), _sds((SEQ, kvw))]
    out_specs = [pair, kv_out, kv_out]
    if has_sink:
        in_specs.append(pl.BlockSpec((1, LANES), lambda p: (0, p)))
        args.append(sink_x)
        out_shape.append(_sds((4, 8, LANES)))
        out_specs.append(pl.BlockSpec((1, 8, LANES), lambda p: (p, 0, 0)))
    return pl.pallas_call(body, name=name, out_shape=tuple(out_shape), grid=(4,), in_specs=in_specs,
                          out_specs=tuple(out_specs), compiler_params=_cp(56))(*args)


def _shift_down(v, k):
    row = lax.broadcasted_iota(jnp.int32, v.shape, 0)
    return jnp.where(row >= k, pltpu.roll(v, k, axis=0), 0.0)


def _shift_up(v, k):
    n = v.shape[0]
    row = lax.broadcasted_iota(jnp.int32, v.shape, 0)
    return jnp.where(row < n - k, pltpu.roll(v, n - k, axis=0), 0.0)


def _conv_pre(x, w_ref, b_ref):
    u = b_ref[...] + x * w_ref[3:4, :]
    for k in range(1, 4):
        u = u + _shift_down(x, k) * w_ref[3 - k:4 - k, :]
    return u


def _conv_fwd(proj, w, b, name):
    def body(x_ref, w_ref, b_ref, o_ref):
        o_ref[...] = _silu(_conv_pre(x_ref[...], w_ref, b_ref))

    nblk = CONV_CH // LANES
    return pl.pallas_call(body, name=name, out_shape=_sds((SEQ, CONV_CH)), grid=(nblk,),
                          in_specs=[pl.BlockSpec((SEQ, LANES), lambda j: (0, XBC // LANES + j)),
                                    pl.BlockSpec((4, LANES), lambda j: (0, j)), pl.BlockSpec((1, LANES), lambda j: (0, j))],
                          out_specs=pl.BlockSpec((SEQ, LANES), lambda j: (0, j)), compiler_params=_cp())(proj, w, b)


def _conv_bwd(proj, dact, w, b, name):
    def body(x_ref, da_ref, w_ref, b_ref, dx_ref, dw_ref, db_ref):
        x = x_ref[...]
        du = da_ref[...] * _dsilu(_conv_pre(x, w_ref, b_ref))
        dx = du * w_ref[3:4, :]
        for k in range(1, 4):
            dx = dx + _shift_up(du, k) * w_ref[3 - k:4 - k, :]
        dx_ref[...] = dx
        db_ref[...] = jnp.sum(du, axis=0, keepdims=True)
        dw_ref[3:4, :] = jnp.sum(du * x, axis=0, keepdims=True)
        for k in range(1, 4):
            dw_ref[3 - k:4 - k, :] = jnp.sum(du * _shift_down(x, k), axis=0, keepdims=True)

    nblk = CONV_CH // LANES
    blk = pl.BlockSpec((SEQ, LANES), lambda j: (0, j))
    wspec, bspec = pl.BlockSpec((4, LANES), lambda j: (0, j)), pl.BlockSpec((1, LANES), lambda j: (0, j))
    return pl.pallas_call(body, name=name, out_shape=(_sds((SEQ, CONV_CH)), _sds((4, CONV_CH)), _sds((1, CONV_CH))), grid=(nblk,),
                          in_specs=[pl.BlockSpec((SEQ, LANES), lambda j: (0, XBC // LANES + j)), blk, wspec, bspec],
                          out_specs=(blk, wspec, bspec), compiler_params=_cp())(proj, dact, w, b)


def _ssd_chunk(xs, bm, cm, dtr, z, hs, al16, dtb, dskx, nw):
    m0, m1 = _half_masks()
    row = lax.broadcasted_iota(jnp.int32, (BLK, BLK), 0)
    col = lax.broadcasted_iota(jnp.int32, (BLK, BLK), 1)
    causal = row >= col
    tril = causal.astype(F32)
    lane = lax.broadcasted_iota(jnp.int32, (1, LANES), 1)
    sub = lax.broadcasted_iota(jnp.int32, (BLK, 1), 0)
    last_row = (sub == BLK - 1).astype(F32)
    dt = jnp.where(lane < 16, _softplus(dtr + dtb), 0.0)
    a16 = -jnp.exp(al16)
    acum = jnp.dot(tril, dt * a16, precision=HI, preferred_element_type=F32)
    acum_t = acum.T
    gmat = [_mm(cm[g], bm[g], NT) for g in range(2)]
    ys, hn = [], []
    for p in range(8):
        g = p // 4
        pick = [(lane == 2 * p + a).astype(F32) for a in range(2)]
        col_h = [jnp.sum(acum * pick[a], axis=1, keepdims=True) for a in range(2)]
        dt_x = sum(jnp.sum(dt * pick[a], axis=1, keepdims=True) * msk for a, msk in enumerate((m0, m1)))
        ac_x = col_h[0] * m0 + col_h[1] * m1
        a_end = jnp.sum(ac_x * last_row, axis=0, keepdims=True)
        xdt = xs[p] * dt_x
        y = _mm(cm[g], hs[p]) * jnp.exp(ac_x)
        for a, msk in enumerate((m0, m1)):
            row_h = jnp.sum(acum_t * (sub == 2 * p + a).astype(F32), axis=0, keepdims=True)
            decay = jnp.exp(jnp.where(causal, col_h[a] - row_h, NEG))
            y = y + _mm(gmat[g] * decay, xdt * msk)
        st = _mm(bm[g], xdt * jnp.exp(a_end - ac_x), TN)
        hn.append(hs[p] * jnp.exp(a_end) + st)
        y = y + dskx[p] * xs[p]
        ys.append(y * _silu(z[p]))
    out = []
    for g in range(2):
        ms = sum(jnp.sum(ys[p] * ys[p], axis=1, keepdims=True) for p in range(4 * g, 4 * g + 4)) * (1.0 / 512)
        rstd = lax.rsqrt(ms + EPS)
        out += [ys[p] * rstd * nw[p] for p in range(4 * g, 4 * g + 4)]
    return out, hn


def _tiles(ref, n, off=0):
    return [ref[:, off + LANES * p:off + LANES * (p + 1)] for p in range(n)]


def _ssd_load(xbc_ref, z_ref, dt_ref, al16_ref, dtb_ref, dsk_ref, nw_ref):
    return (_tiles(xbc_ref, 8), _tiles(xbc_ref, 2, 1024), _tiles(xbc_ref, 2, 1280), dt_ref[...], _tiles(z_ref, 8)), \
           (al16_ref[...], dtb_ref[...], _tiles(dsk_ref, 8), _tiles(nw_ref, 8))


_NCH = SEQ // BLK


def _ssd_param_specs():
    return [_full((1, LANES)), _full((1, LANES)), _full((1, 1024)), _full((1, 1024))]


def _ssd_fwd(xbc_act, proj, al16, dtb, dskx, nw, name):
    def body(xbc_ref, z_ref, dt_ref, al16_ref, dtb_ref, dsk_ref, nw_ref, y_ref, hin_ref, h_scr):
        @pl.when(pl.program_id(0) == 0)
        def _():
            h_scr[...] = jnp.zeros_like(h_scr)
        acts, params = _ssd_load(xbc_ref, z_ref, dt_ref, al16_ref, dtb_ref, dsk_ref, nw_ref)
        hs = _tiles(h_scr, 8)
        hin_ref[0] = h_scr[...]
        ys, hn = _ssd_chunk(*acts, hs, *params)
        for p in range(8):
            y_ref[:, LANES * p:LANES * (p + 1)] = ys[p]
            h_scr[:, LANES * p:LANES * (p + 1)] = hn[p]

    return pl.pallas_call(
        body, name=name, out_shape=(_sds((SEQ, 1024)), _sds((_NCH, BLK, 1024))), grid=(_NCH,),
        in_specs=[pl.BlockSpec((BLK, CONV_CH), lambda c: (c, 0)), pl.BlockSpec((BLK, 1024), lambda c: (c, ZB // 1024)),
                  pl.BlockSpec((BLK, LANES), lambda c: (c, DTC // LANES))] + _ssd_param_specs(),
        out_specs=(pl.BlockSpec((BLK, 1024), lambda c: (c, 0)), pl.BlockSpec((1, BLK, 1024), lambda c: (c, 0, 0))),
        scratch_shapes=[pltpu.VMEM((BLK, 1024), F32)], compiler_params=_cp())(xbc_act, proj, proj, al16, dtb, dskx, nw)


def _ssd_bwd(xbc_act, proj, hin, dyb, al16, dtb, dskx, nw, name):
    def body(xbc_ref, z_ref, dt_ref, hin_ref, dy_ref, al16_ref, dtb_ref, dsk_ref, nw_ref,
             dxbc_ref, dz_ref, ddt_ref, dal16_ref, ddtb_ref, ddsk_ref, dnw_ref, dh_scr):
        @pl.when(pl.program_id(0) == 0)
        def _():
            dh_scr[...] = jnp.zeros_like(dh_scr)
            for r in (dal16_ref, ddtb_ref, ddsk_ref, dnw_ref):
                r[...] = jnp.zeros_like(r)
        acts, params = _ssd_load(xbc_ref, z_ref, dt_ref, al16_ref, dtb_ref, dsk_ref, nw_ref)
        hs = [hin_ref[0, :, LANES * p:LANES * (p + 1)] for p in range(8)]
        _, vjp = jax.vjp(lambda a, h, q: _ssd_chunk(*a, h, *q), acts, hs, params)
        (dxs, dbm, dcm, ddt, dz), dhs, (dal16, ddtb, ddsk, dnw) = vjp((_tiles(dy_ref, 8), _tiles(dh_scr, 8)))
        for p in range(8):
            cols = slice(LANES * p, LANES * (p + 1))
            dxbc_ref[:, cols] = dxs[p]
            dz_ref[:, cols] = dz[p]
            dh_scr[:, cols] = dhs[p]
            ddsk_ref[:, cols] += ddsk[p]
            dnw_ref[:, cols] += dnw[p]
        for g in range(2):
            dxbc_ref[:, 1024 + LANES * g:1024 + LANES * (g + 1)] = dbm[g]
            dxbc_ref[:, 1280 + LANES * g:1280 + LANES * (g + 1)] = dcm[g]
        ddt_ref[...] = ddt
        dal16_ref[...] += dal16
        ddtb_ref[...] += ddtb

    rev = lambda c: _NCH - 1 - c
    return pl.pallas_call(
        body, name=name,
        out_shape=(_sds((SEQ, CONV_CH)), _sds((SEQ, 1024)), _sds((SEQ, LANES)),
                   _sds((1, LANES)), _sds((1, LANES)), _sds((1, 1024)), _sds((1, 1024))),
        grid=(_NCH,),
        in_specs=[pl.BlockSpec((BLK, CONV_CH), lambda c: (rev(c), 0)), pl.BlockSpec((BLK, 1024), lambda c: (rev(c), ZB // 1024)),
                  pl.BlockSpec((BLK, LANES), lambda c: (rev(c), DTC // LANES)), pl.BlockSpec((1, BLK, 1024), lambda c: (rev(c), 0, 0)),
                  pl.BlockSpec((BLK, 1024), lambda c: (rev(c), 0))] + _ssd_param_specs(),
        out_specs=(pl.BlockSpec((BLK, CONV_CH), lambda c: (rev(c), 0)), pl.BlockSpec((BLK, 1024), lambda c: (rev(c), 0)),
                   pl.BlockSpec((BLK, LANES), lambda c: (rev(c), 0)),
                   _full((1, LANES)), _full((1, LANES)), _full((1, 1024)), _full((1, 1024))),
        scratch_shapes=[pltpu.VMEM((BLK, 1024), F32)], compiler_params=_cp())(xbc_act, proj, proj, hin, dyb, al16, dtb, dskx, nw)


def _rstd(v):
    return lax.rsqrt(jnp.mean(v * v, axis=1, keepdims=True) + EPS)


def _rms_bwd(dn, n, rstd):
    return rstd * (dn - n * jnp.mean(dn * n, axis=1, keepdims=True))


_VEC = _full((1, D))


def _layer_spec(layer):
    return pl.BlockSpec((None, 2048, D), lambda *_: (layer, 0, 0))

_ROW = pl.BlockSpec((TM, D), lambda i, *_: (i, 0))


def _proj_fwd(x, pre_w, scale, shift, w, layer, name):
    tn = 1024

    def body(x_ref, pw_ref, sc_ref, sh_ref, w_ref, o_ref, h_ref):
        @pl.when(pl.program_id(1) == 0)
        def _():
            xv = x_ref[...]
            h = (xv * _rstd(xv) * pw_ref[...]) * (1.0 + sc_ref[...]) + sh_ref[...]
            h_ref[...] = h.astype(h_ref.dtype)
        o_ref[...] = jnp.dot(h_ref[...], w_ref[...].astype(MXU), preferred_element_type=F32)

    return pl.pallas_call(body, name=name, out_shape=(_sds((SEQ, NP)), _sds((SEQ, D), MXU)), grid=(SEQ // TM, NP // tn),
                          in_specs=[_ROW, _VEC, _VEC, _VEC, pl.BlockSpec((None, D, tn), lambda i, j: (layer, 0, j))],
                          out_specs=(pl.BlockSpec((TM, tn), lambda i, j: (i, j)), _ROW), compiler_params=_cp())(x, pre_w, scale, shift, w)


def _out_fwd(ya, yb, yc, w, layer, x, gate, post_w, name):
    def body(ya_ref, yb_ref, yc_ref, w_ref, x_ref, g_ref, pw_ref, xn_ref, y_ref):
        y = _mm(ya_ref[...], w_ref[0:512, :]) + _mm(yb_ref[...], w_ref[512:1536, :]) + _mm(yc_ref[...], w_ref[1536:2048, :])
        y_ref[...] = y
        xn_ref[...] = x_ref[...] + g_ref[...] * (y * _rstd(y) * pw_ref[...])

    half = pl.BlockSpec((TM, 512), lambda i: (i, 0))
    return pl.pallas_call(body, name=name, out_shape=(_sds((SEQ, D)), _sds((SEQ, D))), grid=(SEQ // TM,),
                          in_specs=[half, _ROW, half, _layer_spec(layer), _ROW, _VEC, _VEC],
                          out_specs=(_ROW, _ROW), compiler_params=_cp())(ya, yb, yc, w, x, gate, post_w)


def _post_bwd(dxo, y, gate, post_w, name):
    def body(dx_ref, y_ref, g_ref, pw_ref, dy_ref, dg_ref, dpw_ref):
        @pl.when(pl.program_id(0) == 0)
        def _():
            dg_ref[...] = jnp.zeros_like(dg_ref)
            dpw_ref[...] = jnp.zeros_like(dpw_ref)
        dx, y = dx_ref[...], y_ref[...]
        rstd = _rstd(y)
        n = y * rstd
        dg_ref[...] += jnp.sum(dx * (n * pw_ref[...]), axis=0, keepdims=True)
        dr = dx * g_ref[...]
        dpw_ref[...] += jnp.sum(dr * n, axis=0, keepdims=True)
        dy_ref[...] = _rms_bwd(dr * pw_ref[...], n, rstd)

    return pl.pallas_call(body, name=name, out_shape=(_sds((SEQ, D)), _sds((1, D)), _sds((1, D))), grid=(SEQ // TM,),
                          in_specs=[_ROW, _ROW, _VEC, _VEC], out_specs=(_ROW, _VEC, _VEC), compiler_params=_cp())(dxo, y, gate, post_w)


def _dymix(dy, w, layer, name):
    def body(dy_ref, w_ref, a_ref, b_ref, c_ref):
        dy = dy_ref[...]
        a_ref[...] = _mm(dy, w_ref[0:512, :], NT)
        b_ref[...] = _mm(dy, w_ref[512:1536, :], NT)
        c_ref[...] = _mm(dy, w_ref[1536:2048, :], NT)

    half = pl.BlockSpec((TM, 512), lambda i: (i, 0))
    return pl.pallas_call(body, name=name, out_shape=(_sds((SEQ, 512)), _sds((SEQ, D)), _sds((SEQ, 512))), grid=(SEQ // TM,),
                          in_specs=[_ROW, _layer_spec(layer)], out_specs=(half, _ROW, half), compiler_params=_cp())(dy, w)


def _dwout(ya, yb, yc, dy, name):
    def body(ya_ref, yb_ref, yc_ref, dy_ref, o_ref):
        @pl.when(pl.program_id(0) == 0)
        def _():
            o_ref[...] = jnp.zeros_like(o_ref)
        dy = dy_ref[...]
        o_ref[0:512, :] += _mm(ya_ref[...], dy, TN)
        o_ref[512:1536, :] += _mm(yb_ref[...], dy, TN)
        o_ref[1536:2048, :] += _mm(yc_ref[...], dy, TN)

    half = pl.BlockSpec((TM, 512), lambda i: (i, 0))
    return pl.pallas_call(body, name=name, out_shape=_sds((2048, D)), grid=(SEQ // TM,),
                          in_specs=[half, _ROW, half, _ROW], out_specs=_full((2048, D)), compiler_params=_cp())(ya, yb, yc, dy)


_TK = 1536


def _dwin(h, dproj, name):
    def body(h_ref, dp_ref, o_ref):
        @pl.when(pl.program_id(1) == 0)
        def _():
            o_ref[...] = jnp.zeros_like(o_ref)
        o_ref[...] += _mm(h_ref[...], dp_ref[...], TN)

    return pl.pallas_call(body, name=name, out_shape=_sds((D, NP)), grid=(NP // _TK, SEQ // TM),
                          in_specs=[pl.BlockSpec((TM, D), lambda j, k: (k, 0)), pl.BlockSpec((TM, _TK), lambda j, k: (k, j))],
                          out_specs=pl.BlockSpec((D, _TK), lambda j, k: (0, j)), compiler_params=_cp())(h, dproj)


def _dh_bwd(dproj, w, layer, x, pre_w, scale, dxo, name):
    nk = NP // _TK

    def body(dp_ref, w_ref, x_ref, pw_ref, sc_ref, dxo_ref, dx_ref, dsh_ref, dsc_ref, dpw_ref, acc):
        i, k = pl.program_id(0), pl.program_id(1)

        @pl.when((i == 0) & (k == 0))
        def _():
            for r in (dsh_ref, dsc_ref, dpw_ref):
                r[...] = jnp.zeros_like(r)

        @pl.when(k == 0)
        def _():
            acc[...] = jnp.zeros_like(acc)
        acc[...] += _mm(dp_ref[...], w_ref[...], NT)

        @pl.when(k == nk - 1)
        def _():
            dh, xv = acc[...], x_ref[...]
            rstd = _rstd(xv)
            n = xv * rstd
            dsh_ref[...] += jnp.sum(dh, axis=0, keepdims=True)
            dsc_ref[...] += jnp.sum(dh * (n * pw_ref[...]), axis=0, keepdims=True)
            dhn = dh * (1.0 + sc_ref[...])
            dpw_ref[...] += jnp.sum(dhn * n, axis=0, keepdims=True)
            dx_ref[...] = _rms_bwd(dhn * pw_ref[...], n, rstd) + dxo_ref[...]

    return pl.pallas_call(body, name=name, out_shape=(_sds((SEQ, D)), _sds((1, D)), _sds((1, D)), _sds((1, D))),
                          grid=(SEQ // TM, nk),
                          in_specs=[pl.BlockSpec((TM, _TK), lambda i, k: (i, k)), pl.BlockSpec((None, D, _TK), lambda i, k: (layer, 0, k)),
                                    _ROW, _VEC, _VEC, _ROW],
                          out_specs=(_ROW, _VEC, _VEC, _VEC), scratch_shapes=[pltpu.VMEM((TM, D), F32)],
                          compiler_params=_cp())(dproj, w, x, pre_w, scale, dxo)


def _loss_bwd(xf, tgt, name):
    def body(x_ref, t_ref, dx_ref, l_ref):
        @pl.when(pl.program_id(0) == 0)
        def _():
            l_ref[...] = jnp.zeros_like(l_ref)
        e = x_ref[...] - t_ref[...]
        dx_ref[...] = e * (1.0 / D)
        l_ref[...] += 0.5 * jnp.sum(jnp.mean(e * e, axis=1, keepdims=True), axis=0, keepdims=True)

    return pl.pallas_call(body, name=name, out_shape=(_sds((SEQ, D)), _sds((8, LANES))), grid=(SEQ // TM,),
                          in_specs=[_ROW, _ROW], out_specs=(_ROW, _full((8, LANES))), compiler_params=_cp())(xf, tgt)


def _mod_part(c_all, ada_w, ada_b, name):
    def body(c_ref, w_ref, b_ref, o_ref):
        o_ref[0] = _mm(_silu(c_ref[...]), w_ref[0]) + b_ref[0]

    return pl.pallas_call(body, name=name, out_shape=_sds((DEPTH, 8, 768)), grid=(DEPTH,),
                          in_specs=[_full((8, D)), pl.BlockSpec((1, D, 768), lambda i: (i, 0, 0)), pl.BlockSpec((1, 1, 768), lambda i: (i, 0, 0))],
                          out_specs=pl.BlockSpec((1, 8, 768), lambda i: (i, 0, 0)), compiler_params=_cp())(c_all, ada_w, ada_b)


def _ada_grad(c_t, dmod, name):
    def body(c_ref, d_ref, o_ref):
        ca = _silu(c_ref[...])
        dm = d_ref[0]
        acc = ca[:, 0:1] * dm[0:1, :]
        for s in range(1, 8):
            acc = acc + ca[:, s:s + 1] * dm[s:s + 1, :]
        o_ref[0] = acc

    return pl.pallas_call(body, name=name, out_shape=_sds((DEPTH, D, 768)), grid=(DEPTH,),
                          in_specs=[_full((D, LANES)), pl.BlockSpec((1, 8, 768), lambda i: (i, 0, 0))],
                          out_specs=pl.BlockSpec((1, D, 768), lambda i: (i, 0, 0)), compiler_params=_cp())(c_t, dmod)


def _pack(parts):
    flat = []
    for p in parts:
        f = p.reshape(-1)
        flat.append(jnp.pad(f, (0, (-f.size) % LANES)))
    v = jnp.concatenate(flat)
    return jnp.pad(v, (0, (-v.size) % (8 * LANES))).reshape(-1, LANES)


def _unpack(v, shapes):
    v = v.reshape(-1)
    out, off = [], 0
    for s in shapes:
        n = math.prod(s)
        out.append(v[off:off + n].reshape(s))
        off += n + (-n) % LANES
    return out


_GIVEN_DT, _GIVEN_C = 4608, 4624


def _pad_cols(w):
    return jnp.concatenate([w[..., :_GIVEN_DT], w[..., _GIVEN_C:], w[..., _GIVEN_DT:_GIVEN_C],
                            jnp.zeros(w.shape[:-1] + (NP - IN_COLS,), w.dtype)], axis=-1)


def _unpad_cols(w):
    return jnp.concatenate([w[..., :_GIVEN_DT], w[..., DTC:DTC + 16], w[..., _GIVEN_DT:DTC]], axis=-1)


def _pad_lanes(v):
    return jnp.pad(v, (0, LANES - v.shape[0])).reshape(1, LANES)


def _local_step(x2, tgt, mod, weights_of, grads_done, pre_w, post_w, conv_w, conv_b, dt_bias, a_log, d_skip, nw, sinks):
    saved = []
    xcur = x2
    for i in range(DEPTH):
        shift, scale, gate = mod[i:i + 1, :D], mod[i:i + 1, D:2 * D], mod[i:i + 1, 2 * D:]
        pw, qw = pre_w[i:i + 1], post_w[i:i + 1]
        w_p, w_o = weights_of(i, xcur)
        proj, h = _proj_fwd(xcur, pw, scale, shift, w_p, 0, "proj_fwd")
        o_a, lse_a = _attn_fwd(proj, QA // LANES, KA // LANES, VA // LANES, DILS, False, None, "attn_a_fwd")
        ya = _gate_fwd(o_a, proj, ZA // 512, "gate_a_fwd")
        sink_x = jnp.repeat(sinks[i], HD).reshape(1, 512)
        o_c, lse_c = _attn_fwd(proj, QC // LANES, KC // LANES, VC // LANES, (1,), True, sink_x, "attn_c_fwd")
        yc = _gate_fwd(o_c, proj, ZC // 512, "gate_c_fwd")
        cw, cb = conv_w[i], conv_b[i:i + 1]
        xbc_act = _conv_fwd(proj, cw, cb, "conv_fwd")
        ssd_p = (_pad_lanes(a_log[i]), _pad_lanes(dt_bias[i]), jnp.repeat(d_skip[i], HD).reshape(1, 1024), nw[i:i + 1])
        yb, hin = _ssd_fwd(xbc_act, proj, *ssd_p, "ssd_fwd")
        xnew, y = _out_fwd(ya, yb, yc, w_o, 0, xcur, gate, qw, "out_fwd")
        saved.append((w_p, w_o, xcur, scale, gate, pw, qw, proj, h, o_a, lse_a, ya, sink_x, o_c, lse_c, yc, cw, cb, xbc_act, ssd_p, yb, hin, y))
        xcur = xnew
    dx, ltile = _loss_bwd(xcur, tgt, "loss")
    dmod, small = [None] * DEPTH, [None] * DEPTH
    for i in reversed(range(DEPTH)):
        w_p, w_o, xin, scale, gate, pw, qw, proj, h, o_a, lse_a, ya, sink_x, o_c, lse_c, yc, cw, cb, xbc_act, ssd_p, yb, hin, y = saved[i]
        dy, dgate, dpost = _post_bwd(dx, y, gate, qw, "post_bwd")
        dya, dyb, dyc = _dymix(dy, w_o, 0, "dymix")
        dwo = _dwout(ya, yb, yc, dy, "dwout")
        do_a, dz_a = _gate_bwd(dya, o_a, proj, ZA // 512, "gate_a_bwd")
        dq_a, dk_a, dv_a = _attn_bwd(proj, QA // LANES, KA // LANES, VA // LANES, do_a, o_a, lse_a, DILS, False, None, "attn_a_bwd")
        do_c, dz_c = _gate_bwd(dyc, o_c, proj, ZC // 512, "gate_c_bwd")
        dq_c, dk_c, dv_c, dsk = _attn_bwd(proj, QC // LANES, KC // LANES, VC // LANES, do_c, o_c, lse_c, (1,), True, sink_x, "attn_c_bwd")
        dxbc_act, dz_b, ddt, dal16, ddtb, ddsk, dnw = _ssd_bwd(xbc_act, proj, hin, dyb, *ssd_p, "ssd_bwd")
        dxbc, dcw, dcb = _conv_bwd(proj, dxbc_act, cw, cb, "conv_bwd")
        dproj = jnp.concatenate([dq_a, dk_a, dv_a, dz_a, dz_b, dxbc, dq_c, dz_c, dk_c, dv_c, ddt,
                                 jnp.zeros((SEQ, NP - DTC - LANES), F32)], axis=1)
        sent = grads_done(i, _dwin(h, dproj, "dwin"), dwo)
        dx, dshift, dscale, dpre = _dh_bwd(dproj, w_p, 0, xin, pw, scale + sent[0, 0], dx, "dh_bwd")
        dmod[i] = jnp.concatenate([dshift, dscale, dgate], axis=1)
        small[i] = (dpre, dpost, dcw, dcb, ddtb[0, :16], dal16[0, :16], ddsk.reshape(16, HD).sum(axis=1), dnw, dsk[:, 0, ::HD].reshape(8))
    return ltile, dx, jnp.concatenate(dmod, axis=0), small


_SMALL = ((1, D), (1, D), (4, CONV_CH), (1, CONV_CH), (16,), (16,), (16,), (1, D), (8,))


def kernel(x, c, ada_w, ada_b, pre_norm_w, post_norm_w, w_in, conv_w, conv_b, dt_bias, a_log, d_skip, ssm_norm_w, sinks, w_out, loss_target, m_ada_w, m_ada_b, m_pre_norm_w, m_post_norm_w, m_w_in, m_conv_w, m_conv_b, m_dt_bias, m_a_log, m_d_skip, m_ssm_norm_w, m_sinks, m_w_out, v_ada_w, v_ada_b, v_pre_norm_w, v_post_norm_w, v_w_in, v_conv_w, v_conv_b, v_dt_bias, v_a_log, v_d_skip, v_ssm_norm_w, v_sinks, v_w_out):
    xi, yi, ci = lax.axis_index("x"), lax.axis_index("y"), lax.axis_index("c")
    chip = 2 * xi + yi
    me = 2 * chip + ci

    w_in_b = _cast_bf16(w_in.reshape(DEPTH * D, SHARD_IN), 512, "cast_w_in").reshape(DEPTH, D, SHARD_IN)
    w_out_b = _cast_bf16(w_out.reshape(DEPTH * 512, D), 512, "cast_w_out").reshape(DEPTH, 512, D)
    gathers = []
    for i in range(DEPTH):
        lands = [lax.dynamic_update_slice(lax.empty((4,) + a.shape[1:], a.dtype), a[i][None], (chip, 0, 0)) for a in (w_in_b, w_out_b)]
        gathers.append(_split_start(None, lands, f"gather_start{i}"))
    all_started = gathers[0][3] + gathers[1][3] + gathers[2][3] + gathers[3][3]

    def weights_of(i, after):
        send_sems, recv_sems, thru, _ = gathers[i]
        if i == 0:
            after = all_started + mod[:1, :LANES]
        g_in, g_out = _split_wait(send_sems, recv_sems, thru, 2, after, f"gather_wait{i}")
        w_p = _pad_cols(jnp.concatenate([g_in[k] for k in range(4)], axis=-1))
        return w_p[None], jnp.concatenate([g_out[k] for k in range(4)], axis=0)[None]

    scatters = [None] * DEPTH

    def grads_done(i, dwi_p, dwo):
        dwi = _unpad_cols(dwi_p)
        blk_in = jnp.stack([dwi[:, SHARD_IN * k:SHARD_IN * (k + 1)] for k in range(4)]).astype(jnp.bfloat16)
        blk_out = dwo.reshape(4, 512, D).astype(jnp.bfloat16)
        lands = [lax.empty(blk_in.shape, blk_in.dtype), lax.empty(blk_out.shape, blk_out.dtype)]
        scatters[i] = _split_start([blk_in, blk_out], lands, f"scatter_start{i}")
        return scatters[i][3]

    g0 = _allgather8(_pack([c, conv_w]), "gather_c")
    c_all = g0[:, :8, :].reshape(8, D)
    conv_w_full = jnp.concatenate([g0[2 * k, 8:56, :].reshape(DEPTH, 4, CONV_CH // 4) for k in range(4)], axis=-1)

    ada_b_mine = lax.dynamic_slice_in_dim(ada_b, 768 * chip, 768, axis=1).reshape(DEPTH, 1, 768)
    gm = _allgather8(_mod_part(c_all, ada_w, ada_b_mine, "mod_part").reshape(DEPTH * 8, 768), "gather_mod")
    gm = gm.reshape(4, 2, DEPTH, 8, 768)[:, 0]
    mod = lax.dynamic_index_in_dim(gm, me, axis=2, keepdims=False).transpose(1, 0, 2).reshape(DEPTH, 3 * D)

    ltile, dx, dmod, small = _local_step(x[0], loss_target[0], mod, weights_of, grads_done, pre_norm_w, post_norm_w, conv_w_full,
                                         conv_b, dt_bias, a_log, d_skip, ssm_norm_w, sinks)

    packed = _pack([dmod] + [g for layer in small for g in layer] + [ltile[0]])
    gs = _allgather8(packed, "gather_small")
    tot = _sum_blocks(gs, packed.shape[0], "sum_small")
    parts = _unpack(tot, [(DEPTH, 3 * D)] + list(_SMALL) * DEPTH + [(LANES,)])
    g_ada_b, loss = parts[0], parts[-1][0]
    per_layer = [parts[1 + len(_SMALL) * i:1 + len(_SMALL) * (i + 1)] for i in range(DEPTH)]
    g_pre, g_post, g_cw, g_cb, g_dtb, g_al, g_dsk, g_nw, g_sk = [jnp.stack([per_layer[i][j] for i in range(DEPTH)]) for j in range(len(_SMALL))]
    g_pre, g_post, g_cb, g_nw = g_pre[:, 0], g_post[:, 0], g_cb[:, 0], g_nw[:, 0]
    g_cw = lax.dynamic_slice_in_dim(g_cw, (CONV_CH // 4) * chip, CONV_CH // 4, axis=2)

    dmod_all = gs[:, :(DEPTH * 3 * D) // LANES, :].reshape(8, DEPTH, 3 * D).transpose(1, 0, 2)
    dmod_mine = lax.dynamic_slice_in_dim(dmod_all, 768 * chip, 768, axis=2)
    c_t = jnp.pad(c_all.T, ((0, 0), (0, LANES - 8)))
    g_ada_w = _ada_grad(c_t, dmod_mine, "ada_grad")

    r_in, r_out = [], []
    for i in range(DEPTH):
        send_sems, recv_sems, thru, _ = scatters[i]
        done = _split_wait(send_sems, recv_sems, thru, 2, dx, f"scatter_wait{i}")
        for r, land, src in zip((r_in, r_out), done[2:], done[:2]):
            own = lax.dynamic_index_in_dim(src, chip, axis=0, keepdims=True)
            r.append(lax.dynamic_update_slice(land, own, (chip, 0, 0)))
    r_in, r_out = jnp.stack(r_in, axis=1), jnp.stack(r_out, axis=1)
    p_in = _sum_blocks(r_in.reshape(4, DEPTH * D, SHARD_IN), 256, "sum_w_in")
    p_out = _sum_blocks(r_out.reshape(4, DEPTH * 512, D), 512, "sum_w_out")
    s_in, s_out = _sibling_swap([p_in, p_out], "swap_partials")

    res = {}
    res["ada_w"] = [a.reshape(DEPTH, D, 768) for a in
                    _adamw(ada_w.reshape(DEPTH * D, 768), [g_ada_w.reshape(DEPTH * D, 768)], m_ada_w.reshape(DEPTH * D, 768),
                           v_ada_w.reshape(DEPTH * D, 768), 512, "adamw_ada_w")]
    res["w_in"] = [a.reshape(DEPTH, D, SHARD_IN) for a in
                   _adamw(w_in.reshape(DEPTH * D, SHARD_IN), [p_in, s_in], m_w_in.reshape(DEPTH * D, SHARD_IN),
                          v_w_in.reshape(DEPTH * D, SHARD_IN), 256, "adamw_w_in")]
    res["w_out"] = [a.reshape(DEPTH, 512, D) for a in
                    _adamw(w_out.reshape(DEPTH * 512, D), [p_out, s_out], m_w_out.reshape(DEPTH * 512, D),
                           v_w_out.reshape(DEPTH * 512, D), 512, "adamw_w_out")]
    names = ["ada_b", "pre_norm_w", "post_norm_w", "conv_w", "conv_b", "dt_bias", "a_log", "d_skip", "ssm_norm_w", "sinks"]
    ws = [ada_b, pre_norm_w, post_norm_w, conv_w, conv_b, dt_bias, a_log, d_skip, ssm_norm_w, sinks]
    gsm = [g_ada_b, g_pre, g_post, g_cw, g_cb, g_dtb, g_al, g_dsk, g_nw, g_sk]
    ms = [m_ada_b, m_pre_norm_w, m_post_norm_w, m_conv_w, m_conv_b, m_dt_bias, m_a_log, m_d_skip, m_ssm_norm_w, m_sinks]
    vs = [v_ada_b, v_pre_norm_w, v_post_norm_w, v_conv_w, v_conv_b, v_dt_bias, v_a_log, v_d_skip, v_ssm_norm_w, v_sinks]
    pw_, pg_, pm_, pv_ = _pack(ws), _pack(gsm), _pack(ms), _pack(vs)
    small_out = _adamw(pw_, [pg_], pm_, pv_, pw_.shape[0], "adamw_small")
    shapes = [w.shape for w in ws]
    for kind in range(4):
        for nm, a in zip(names, _unpack(small_out[kind], shapes)):
            res.setdefault(nm, [None] * 4)[kind] = a
    order = ["ada_w", "ada_b", "pre_norm_w", "post_norm_w", "w_in", "conv_w", "conv_b", "dt_bias", "a_log", "d_skip", "ssm_norm_w", "sinks", "w_out"]
    return (loss, dx[None], *[res[n][0] for n in order], *[res[n][1] for n in order], *[res[n][2] for n in order], *[res[n][3] for n in order])
```

```python
import math

import jax
import jax.numpy as jnp
from jax import lax
from jax.experimental import pallas as pl
from jax.experimental.pallas import tpu as pltpu

F32 = jnp.float32
MXU = jnp.bfloat16
HI = lax.Precision.HIGHEST
MESH = pl.DeviceIdType.MESH

SEQ = 4096
D = 1024
DEPTH = 4
HD = 64
LANES = 128
BLK = 128
DILS = (1, 4, 16)
NEG = -1e30
EPS = 1e-6
MIB = 1024 * 1024

NP = 6144
QA, KA, VA, ZA = 0, 512, 1024, 1536
ZB, XBC = 2048, 3072
QC, ZC, KC, VC = 4608, 5120, 5632, 5760
DTC = 5888
IN_COLS = 5904
SHARD_IN = IN_COLS // 4
CONV_CH = 1536
TM = 512

ADAM_LR, ADAM_B1, ADAM_B2, ADAM_EPS, ADAM_WD, ADAM_STEP = 0.001, 0.9, 0.999, 1e-08, 0.01, 10

NT = (((1,), (1,)), ((), ()))
TN = (((0,), (0,)), ((), ()))


def _cp(vmem_mib=48):
    return pltpu.CompilerParams(vmem_limit_bytes=vmem_mib * MIB)


def _sds(shape, dtype=F32):
    return jax.ShapeDtypeStruct(shape, dtype)


def _full(shape):
    n = len(shape)
    return pl.BlockSpec(shape, lambda *_: (0,) * n)


def _mm(a, b, dims=None):
    if dims is None:
        return jnp.dot(a.astype(MXU), b.astype(MXU), preferred_element_type=F32)
    return lax.dot_general(a.astype(MXU), b.astype(MXU), dims, preferred_element_type=F32)


def _sigmoid(x):
    return 1.0 / (1.0 + jnp.exp(-x))


def _silu(x):
    return x * _sigmoid(x)


def _dsilu(x):
    s = _sigmoid(x)
    return s * (1.0 + x * (1.0 - s))


def _softplus(x):
    ax = jnp.where(x >= 0, x, -x)
    return jnp.maximum(x, 0.0) + jnp.log1p(jnp.exp(-ax))


def _half_masks():
    lane = lax.broadcasted_iota(jnp.int32, (1, LANES), 1)
    m0 = (lane < HD).astype(F32)
    return m0, 1.0 - m0


def _allgather8(v, name):
    r, cc = v.shape

    def body(v_ref, out_ref, send_sems, recv_sems):
        x, y, c = lax.axis_index("x"), lax.axis_index("y"), lax.axis_index("c")
        me = 4 * x + 2 * y + c
        out_ref[me] = v_ref[...]
        peers = []
        for k in range(1, 8):
            px = 1 - x if k & 4 else x
            py = 1 - y if k & 2 else y
            pc = 1 - c if k & 1 else c
            peers.append((px, py, pc))
        sends = []
        for k, peer in enumerate(peers):
            cp = pltpu.make_async_remote_copy(src_ref=v_ref, dst_ref=out_ref.at[me], send_sem=send_sems.at[k],
                                              recv_sem=recv_sems.at[k], device_id=peer, device_id_type=MESH)
            cp.start()
            sends.append(cp)
        for k, (px, py, pc) in enumerate(peers):
            pltpu.make_async_remote_copy(src_ref=v_ref, dst_ref=out_ref.at[4 * px + 2 * py + pc], send_sem=send_sems.at[k],
                                         recv_sem=recv_sems.at[k], device_id=(px, py, pc), device_id_type=MESH).wait_recv()
        for cp in sends:
            cp.wait_send()

    return pl.pallas_call(
        body, name=name, out_shape=_sds((8, r, cc)),
        in_specs=[pl.BlockSpec(memory_space=pltpu.VMEM)], out_specs=pl.BlockSpec(memory_space=pltpu.VMEM),
        scratch_shapes=[pltpu.SemaphoreType.DMA((7,)), pltpu.SemaphoreType.DMA((7,))],
        compiler_params=_cp(32),
    )(v)


_HBM = pl.BlockSpec(memory_space=pltpu.HBM)
_SEM = pl.BlockSpec(memory_space=pltpu.SEMAPHORE)
_EFFECT = pltpu.SideEffectType.DATAFLOW_SIDE_EFFECTING


def _chip_copies(src_refs, land_refs, send_sems, recv_sems):
    x, y, c = lax.axis_index("x"), lax.axis_index("y"), lax.axis_index("c")
    mine = 2 * x + y
    out = []
    for i, land in enumerate(land_refs):
        for j, (px, py) in enumerate([(1 - x, y), (x, 1 - y), (1 - x, 1 - y)]):
            src = src_refs[i].at[2 * px + py] if src_refs else land.at[mine]
            mk = lambda dst, i=i, j=j, src=src, px=px, py=py: pltpu.make_async_remote_copy(
                src_ref=src, dst_ref=dst, send_sem=send_sems.at[3 * i + j], recv_sem=recv_sems.at[3 * i + j],
                device_id=(px, py, c), device_id_type=MESH)
            out.append((mk(land.at[mine]), mk(land.at[2 * px + py])))
    return out


def _split_start(srcs, lands, name):
    ops = list(srcs or []) + list(lands)
    ns, n = len(srcs or []), len(lands)

    def body(*refs):
        src_refs, land_refs = refs[:ns], refs[ns:ns + n]
        send_sems, recv_sems = refs[ns + n], refs[ns + n + 1]
        for mine_out, _ in _chip_copies(src_refs, land_refs, send_sems, recv_sems):
            mine_out.start()
        refs[-1][...] = jnp.zeros_like(refs[-1])

    sems = pltpu.SemaphoreType.DMA((3 * n,))
    res = pl.pallas_call(
        body, name=name, out_shape=(sems, sems) + tuple(pltpu.HBM(a.shape, a.dtype) for a in ops) + (_sds((8, LANES)),),
        in_specs=[_HBM] * len(ops), out_specs=(_SEM, _SEM) + (_HBM,) * len(ops) + (pl.BlockSpec(memory_space=pltpu.VMEM),),
        input_output_aliases={k: 2 + k for k in range(len(ops))},
        compiler_params=pltpu.CompilerParams(has_side_effects=_EFFECT),
    )(*[pltpu.with_memory_space_constraint(a, pltpu.HBM) for a in ops])
    return res[0], res[1], list(res[2:2 + len(ops)]), res[-1]


def _split_wait(send_sems, recv_sems, thru, n, after, name):
    ns = len(thru) - n

    def body(*refs):
        src_refs, land_refs = refs[:ns], refs[ns:ns + n]
        for mine_out, arriving in _chip_copies(src_refs, land_refs, refs[ns + n], refs[ns + n + 1]):
            mine_out.wait_send()
            arriving.wait_recv()

    res = pl.pallas_call(
        body, name=name, out_shape=tuple(pltpu.HBM(a.shape, a.dtype) for a in thru),
        in_specs=[_HBM] * len(thru) + [_SEM, _SEM, pl.BlockSpec(memory_space=pl.ANY)], out_specs=(_HBM,) * len(thru),
        input_output_aliases={k: k for k in range(len(thru))},
        compiler_params=pltpu.CompilerParams(has_side_effects=_EFFECT),
    )(*thru, send_sems, recv_sems, after)
    return list(res)


def _sibling_swap(arrs, name):
    n = len(arrs)

    def body(*refs):
        ins, outs_, (send_sems, recv_sems) = refs[:n], refs[n:2 * n], refs[2 * n:]
        sib = (lax.axis_index("x"), lax.axis_index("y"), 1 - lax.axis_index("c"))
        cps = [pltpu.make_async_remote_copy(src_ref=ins[i], dst_ref=outs_[i], send_sem=send_sems.at[i], recv_sem=recv_sems.at[i],
                                            device_id=sib, device_id_type=MESH) for i in range(n)]
        for cp in cps:
            cp.start()
        for cp in cps:
            cp.wait_recv()
        for cp in cps:
            cp.wait_send()

    hbm = pl.BlockSpec(memory_space=pltpu.HBM)
    return pl.pallas_call(
        body, name=name, out_shape=tuple(_sds(a.shape, a.dtype) for a in arrs), in_specs=[hbm] * n, out_specs=tuple([hbm] * n),
        scratch_shapes=[pltpu.SemaphoreType.DMA((n,)), pltpu.SemaphoreType.DMA((n,))],
    )(*arrs)


def _cast_bf16(a, rows, name):
    r, cc = a.shape

    def body(a_ref, o_ref):
        o_ref[...] = a_ref[...].astype(jnp.bfloat16)

    return pl.pallas_call(body, name=name, out_shape=_sds((r, cc), jnp.bfloat16), grid=(r // rows,),
                          in_specs=[pl.BlockSpec((rows, cc), lambda i: (i, 0))],
                          out_specs=pl.BlockSpec((rows, cc), lambda i: (i, 0)), compiler_params=_cp())(a)


def _sum_blocks(a, rows, name):
    k, r, cc = a.shape

    def body(a_ref, o_ref):
        acc = a_ref[0].astype(F32)
        for j in range(1, k):
            acc = acc + a_ref[j].astype(F32)
        o_ref[...] = acc

    return pl.pallas_call(body, name=name, out_shape=_sds((r, cc)), grid=(r // rows,),
                          in_specs=[pl.BlockSpec((k, rows, cc), lambda i: (0, i, 0))],
                          out_specs=pl.BlockSpec((rows, cc), lambda i: (i, 0)), compiler_params=_cp())(a)


def _adamw(w, parts, m, v, rows, name):
    r, cc = w.shape
    np_ = len(parts)
    c1 = 1.0 / (1.0 - ADAM_B1 ** ADAM_STEP)
    c2 = 1.0 / (1.0 - ADAM_B2 ** ADAM_STEP)

    def body(*refs):
        w_ref, p_refs, (m_ref, v_ref, g_ref, d_ref, nm_ref, nv_ref) = refs[0], refs[1:1 + np_], refs[1 + np_:]
        g = p_refs[0][...]
        for p_ref in p_refs[1:]:
            g = g + p_ref[...]
        nm = ADAM_B1 * m_ref[...] + (1.0 - ADAM_B1) * g
        nv = ADAM_B2 * v_ref[...] + (1.0 - ADAM_B2) * (g * g)
        g_ref[...] = g
        nm_ref[...] = nm
        nv_ref[...] = nv
        d_ref[...] = -ADAM_LR * ((nm * c1) / (jnp.sqrt(nv * c2) + ADAM_EPS) + ADAM_WD * w_ref[...])

    spec = pl.BlockSpec((rows, cc), lambda i: (i, 0))
    return pl.pallas_call(body, name=name, out_shape=(_sds((r, cc)),) * 4, grid=(r // rows,),
                          in_specs=[spec] * (3 + np_), out_specs=(spec,) * 4, compiler_params=_cp())(w, *parts, m, v)


def _gate_fwd(o, proj, zblk, name):
    def body(o_ref, z_ref, y_ref):
        y_ref[...] = o_ref[...] * _silu(z_ref[...])

    return pl.pallas_call(body, name=name, out_shape=_sds((SEQ, 512)), grid=(SEQ // TM,),
                          in_specs=[pl.BlockSpec((TM, 512), lambda i: (i, 0)), pl.BlockSpec((TM, 512), lambda i: (i, zblk))],
                          out_specs=pl.BlockSpec((TM, 512), lambda i: (i, 0)), compiler_params=_cp())(o, proj)


def _gate_bwd(dy, o, proj, zblk, name):
    def body(dy_ref, o_ref, z_ref, do_ref, dz_ref):
        dy, z = dy_ref[...], z_ref[...]
        do_ref[...] = dy * _silu(z)
        dz_ref[...] = dy * o_ref[...] * _dsilu(z)

    return pl.pallas_call(body, name=name, out_shape=(_sds((SEQ, 512)), _sds((SEQ, 512))), grid=(SEQ // TM,),
                          in_specs=[pl.BlockSpec((TM, 512), lambda i: (i, 0)), pl.BlockSpec((TM, 512), lambda i: (i, 0)),
                                    pl.BlockSpec((TM, 512), lambda i: (i, zblk))],
                          out_specs=(pl.BlockSpec((TM, 512), lambda i: (i, 0)),) * 2, compiler_params=_cp())(dy, o, proj)


_BIAS = pltpu.VMEM((2, 2 * BLK, 2 * BLK), F32)


def _fill_band_bias(bias_ref):
    qi = lax.broadcasted_iota(jnp.int32, (2 * BLK, 2 * BLK), 0) & (BLK - 1)
    kj = lax.broadcasted_iota(jnp.int32, (2 * BLK, 2 * BLK), 1)
    dist = BLK + qi - kj
    band = (dist >= 0) & (dist <= BLK)
    bias_ref[0] = jnp.where(band, 0.0, NEG)
    bias_ref[1] = jnp.where(band & (kj >= BLK), 0.0, NEG)


class _HeadStack:
    def __init__(self, group):
        self.m0, self.m1 = _half_masks()
        self.group = group
        if group is not None:
            self.kv_mask = (self.m0, self.m1)[group]

    def _swap_half(self, t, a):
        return t if a == self.group else pltpu.roll(t, HD, axis=1)

    def stack(self, t):
        t0, t1 = t * self.m0, t * self.m1
        if self.group is not None:
            t0, t1 = self._swap_half(t0, 0), self._swap_half(t1, 1)
        return jnp.concatenate([t0, t1], axis=0)

    def unstack(self, ts):
        if self.group is None:
            return ts[:BLK] * self.m0 + ts[BLK:] * self.m1
        return self._swap_half(ts[:BLK] * self.kv_mask, 0) + self._swap_half(ts[BLK:] * self.kv_mask, 1)


def _rows(st, dil):
    if dil == 1:
        return pl.ds(pl.multiple_of(st, BLK), BLK)
    return pl.ds(st, BLK, stride=dil)


def _block_pos(n, dil):
    nb = SEQ // (dil * BLK)
    r, b = n // nb, n % nb
    hp = (b > 0).astype(jnp.int32)
    st = r + dil * BLK * b
    return st, st - dil * BLK * hp, 1 - hp


def _attn_fwd(proj, qblk, kblk, vblk, dils, gqa, sink_x, name):
    has_sink = sink_x is not None

    def body(*refs):
        if has_sink:
            q_ref, k_ref, v_ref, s_ref, o_ref, lse_ref, m_scr, z_scr, bias_scr = refs
        else:
            q_ref, k_ref, v_ref, o_ref, lse_ref, m_scr, z_scr, bias_scr = refs

        @pl.when(pl.program_id(0) == 0)
        def _():
            _fill_band_bias(bias_scr)
        o_ref[...] = jnp.zeros_like(o_ref)
        if has_sink:
            z_scr[...] = jnp.ones_like(z_scr)
            m_scr[...] = jnp.broadcast_to(s_ref[...], m_scr.shape)
        else:
            z_scr[...] = jnp.zeros_like(z_scr)
            m_scr[...] = jnp.full_like(m_scr, NEG)

        def step(n, carry, dil, heads):
            m0, m1 = heads.m0, heads.m1
            st, stp, first = _block_pos(n, dil)
            rq, rp = _rows(st, dil), _rows(stp, dil)
            kk = jnp.concatenate([k_ref[rp, :], k_ref[rq, :]], axis=0)
            vv = jnp.concatenate([v_ref[rp, :], v_ref[rq, :]], axis=0)
            s = _mm(heads.stack(q_ref[rq, :]), kk, NT) * (HD ** -0.5) + bias_scr[first]
            m = jnp.max(s, axis=1, keepdims=True)
            p = jnp.exp(s - m)
            l = jnp.sum(p, axis=1, keepdims=True)
            o_pair = heads.unstack(_mm(p, vv))
            m_pair = m[:BLK] * m0 + m[BLK:] * m1
            l_pair = l[:BLK] * m0 + l[BLK:] * m1
            m_old = m_scr[rq, :]
            m_new = jnp.maximum(m_old, m_pair)
            alpha, beta = jnp.exp(m_old - m_new), jnp.exp(m_pair - m_new)
            o_ref[rq, :] = o_ref[rq, :] * alpha + o_pair * beta
            z_scr[rq, :] = z_scr[rq, :] * alpha + l_pair * beta
            m_scr[rq, :] = m_new
            return carry

        def blocks(heads):
            for dil in dils:
                lax.fori_loop(0, SEQ // BLK, lambda n, carry, dil=dil: step(n, carry, dil, heads), 0, unroll=4)

        if gqa:
            for grp in range(2):
                pl.when(pl.program_id(0) // 2 == grp)(lambda grp=grp: blocks(_HeadStack(grp)))
        else:
            blocks(_HeadStack(None))

        def fin(t, carry):
            rt = pl.ds(pl.multiple_of(t * TM, TM), TM)
            z = z_scr[rt, :]
            o_ref[rt, :] = o_ref[rt, :] / z
            lse_ref[rt, :] = m_scr[rt, :] + jnp.log(z)
            return carry
        lax.fori_loop(0, SEQ // TM, fin, 0)

    col = lambda blk: pl.BlockSpec((SEQ, LANES), lambda p, blk=blk: (0, blk + p))
    kv = (lambda blk: pl.BlockSpec((SEQ, LANES), lambda p, blk=blk: (0, blk))) if gqa else col
    in_specs = [col(qblk), kv(kblk), kv(vblk)]
    args = [proj, proj, proj]
    if has_sink:
        in_specs.append(pl.BlockSpec((1, LANES), lambda p: (0, p)))
        args.append(sink_x)
    out = pl.BlockSpec((SEQ, LANES), lambda p: (0, p))
    return pl.pallas_call(body, name=name, out_shape=(_sds((SEQ, 512)), _sds((SEQ, 512))), grid=(4,),
                          in_specs=in_specs, out_specs=(out, out),
                          scratch_shapes=[pltpu.VMEM((SEQ, LANES), F32), pltpu.VMEM((SEQ, LANES), F32), _BIAS],
                          compiler_params=_cp(48))(*args)


def _attn_bwd(proj, qblk, kblk, vblk, do, o, lse, dils, gqa, sink_x, name):
    has_sink = sink_x is not None

    def body(*refs):
        if has_sink:
            q_ref, k_ref, v_ref, do_ref, o_ref, lse_ref, s_ref, dq_ref, dk_ref, dv_ref, ds_ref, bias_scr = refs
        else:
            q_ref, k_ref, v_ref, do_ref, o_ref, lse_ref, dq_ref, dk_ref, dv_ref, bias_scr = refs
        pid = pl.program_id(0)

        @pl.when(pid == 0)
        def _():
            _fill_band_bias(bias_scr)
        dq_ref[...] = jnp.zeros_like(dq_ref)
        if gqa:
            @pl.when(pid == 0)
            def _():
                dk_ref[...] = jnp.zeros_like(dk_ref)
                dv_ref[...] = jnp.zeros_like(dv_ref)
        else:
            dk_ref[...] = jnp.zeros_like(dk_ref)
            dv_ref[...] = jnp.zeros_like(dv_ref)

        def step(n, carry, dil, heads):
            m0, m1 = heads.m0, heads.m1
            st, stp, first = _block_pos(n, dil)
            rq, rp = _rows(st, dil), _rows(stp, dil)
            do_, lse_ = do_ref[rq, :], lse_ref[rq, :]
            kk = jnp.concatenate([k_ref[rp, :], k_ref[rq, :]], axis=0)
            vv = jnp.concatenate([v_ref[rp, :], v_ref[rq, :]], axis=0)
            qs, dos = heads.stack(q_ref[rq, :]), heads.stack(do_)
            doo = do_ * o_ref[rq, :]
            delta = jnp.concatenate([jnp.sum(doo * m0, axis=1, keepdims=True), jnp.sum(doo * m1, axis=1, keepdims=True)], axis=0)
            lse_s = jnp.concatenate([lse_[:, 0:1], lse_[:, HD:HD + 1]], axis=0)
            p = jnp.exp(_mm(qs, kk, NT) * (HD ** -0.5) + bias_scr[first] - lse_s)
            dsr = p * (_mm(dos, vv, NT) - delta) * (HD ** -0.5)
            dq_ref[rq, :] += heads.unstack(_mm(dsr, kk))
            dk_sum, dv_sum = _mm(dsr, qs, TN), _mm(p, dos, TN)
            dk_ref[rp, :] += dk_sum[:BLK]
            dk_ref[rq, :] += dk_sum[BLK:]
            dv_ref[rp, :] += dv_sum[:BLK]
            dv_ref[rq, :] += dv_sum[BLK:]
            return carry

        def blocks(heads):
            for dil in dils:
                lax.fori_loop(0, SEQ // BLK, lambda n, carry, dil=dil: step(n, carry, dil, heads), 0, unroll=2)

        if gqa:
            for grp in range(2):
                pl.when(pid // 2 == grp)(lambda grp=grp: blocks(_HeadStack(grp)))
        else:
            blocks(_HeadStack(None))

        if has_sink:
            m0, m1 = _half_masks()

            def sink_rows(t, acc):
                rt = pl.ds(pl.multiple_of(t * TM, TM), TM)
                return acc - jnp.sum(jnp.exp(s_ref[...] - lse_ref[rt, :]) * (do_ref[rt, :] * o_ref[rt, :]), axis=0, keepdims=True)
            acc = lax.fori_loop(0, SEQ // TM, sink_rows, jnp.zeros((1, LANES), F32))
            per_head = jnp.sum(acc * m0, axis=1, keepdims=True) * m0 + jnp.sum(acc * m1, axis=1, keepdims=True) * m1
            ds_ref[0] = jnp.broadcast_to(per_head, (8, LANES))

    col = lambda blk: pl.BlockSpec((SEQ, LANES), lambda p, blk=blk: (0, blk + p))
    kv = (lambda blk: pl.BlockSpec((SEQ, LANES), lambda p, blk=blk: (0, blk))) if gqa else col
    pair = pl.BlockSpec((SEQ, LANES), lambda p: (0, p))
    in_specs = [col(qblk), kv(kblk), kv(vblk), pair, pair, pair]
    args = [proj, proj, proj, do, o, lse]
    kvw = LANES if gqa else 512
    kv_out = pl.BlockSpec((SEQ, LANES), lambda p: (0, 0)) if gqa else pair
    out_shape = [_sds((SEQ, 512)), _sds((SEQ, kvw)), _sds((SEQ, kvw))]
    out_specs = [pair, kv_out, kv_out]
    if has_sink:
        in_specs.append(pl.BlockSpec((1, LANES), lambda p: (0, p)))
        args.append(sink_x)
        out_shape.append(_sds((4, 8, LANES)))
        out_specs.append(pl.BlockSpec((1, 8, LANES), lambda p: (p, 0, 0)))
    return pl.pallas_call(body, name=name, out_shape=tuple(out_shape), grid=(4,), in_specs=in_specs,
                          out_specs=tuple(out_specs), scratch_shapes=[_BIAS], compiler_params=_cp(56))(*args)


def _shift_down(v, k):
    row = lax.broadcasted_iota(jnp.int32, v.shape, 0)
    return jnp.where(row >= k, pltpu.roll(v, k, axis=0), 0.0)


def _shift_up(v, k):
    n = v.shape[0]
    row = lax.broadcasted_iota(jnp.int32, v.shape, 0)
    return jnp.where(row < n - k, pltpu.roll(v, n - k, axis=0), 0.0)


def _conv_pre(x, w_ref, b_ref):
    u = b_ref[...] + x * w_ref[3:4, :]
    for k in range(1, 4):
        u = u + _shift_down(x, k) * w_ref[3 - k:4 - k, :]
    return u


def _conv_fwd(proj, w, b, name):
    def body(x_ref, w_ref, b_ref, o_ref):
        o_ref[...] = _silu(_conv_pre(x_ref[...], w_ref, b_ref))

    nblk = CONV_CH // LANES
    return pl.pallas_call(body, name=name, out_shape=_sds((SEQ, CONV_CH)), grid=(nblk,),
                          in_specs=[pl.BlockSpec((SEQ, LANES), lambda j: (0, XBC // LANES + j)),
                                    pl.BlockSpec((4, LANES), lambda j: (0, j)), pl.BlockSpec((1, LANES), lambda j: (0, j))],
                          out_specs=pl.BlockSpec((SEQ, LANES), lambda j: (0, j)), compiler_params=_cp())(proj, w, b)


def _conv_bwd(proj, dact, w, b, name):
    def body(x_ref, da_ref, w_ref, b_ref, dx_ref, dw_ref, db_ref):
        x = x_ref[...]
        du = da_ref[...] * _dsilu(_conv_pre(x, w_ref, b_ref))
        dx = du * w_ref[3:4, :]
        for k in range(1, 4):
            dx = dx + _shift_up(du, k) * w_ref[3 - k:4 - k, :]
        dx_ref[...] = dx
        db_ref[...] = jnp.sum(du, axis=0, keepdims=True)
        dw_ref[3:4, :] = jnp.sum(du * x, axis=0, keepdims=True)
        for k in range(1, 4):
            dw_ref[3 - k:4 - k, :] = jnp.sum(du * _shift_down(x, k), axis=0, keepdims=True)

    nblk = CONV_CH // LANES
    blk = pl.BlockSpec((SEQ, LANES), lambda j: (0, j))
    wspec, bspec = pl.BlockSpec((4, LANES), lambda j: (0, j)), pl.BlockSpec((1, LANES), lambda j: (0, j))
    return pl.pallas_call(body, name=name, out_shape=(_sds((SEQ, CONV_CH)), _sds((4, CONV_CH)), _sds((1, CONV_CH))), grid=(nblk,),
                          in_specs=[pl.BlockSpec((SEQ, LANES), lambda j: (0, XBC // LANES + j)), blk, wspec, bspec],
                          out_specs=(blk, wspec, bspec), compiler_params=_cp())(proj, dact, w, b)


def _ssd_chunk(xs, bm, cm, dtr, z, hs, al16, dtb, dskx, nw):
    m0, m1 = _half_masks()
    row = lax.broadcasted_iota(jnp.int32, (BLK, BLK), 0)
    col = lax.broadcasted_iota(jnp.int32, (BLK, BLK), 1)
    causal = row >= col
    tril = causal.astype(F32)
    lane = lax.broadcasted_iota(jnp.int32, (1, LANES), 1)
    sub = lax.broadcasted_iota(jnp.int32, (BLK, 1), 0)
    last_row = (sub == BLK - 1).astype(F32)
    dt = jnp.where(lane < 16, _softplus(dtr + dtb), 0.0)
    a16 = -jnp.exp(al16)
    acum = jnp.dot(tril, dt * a16, precision=HI, preferred_element_type=F32)
    acum_t = acum.T
    gmat = [_mm(cm[g], bm[g], NT) for g in range(2)]
    ys, hn = [], []
    for p in range(8):
        g = p // 4
        pick = [(lane == 2 * p + a).astype(F32) for a in range(2)]
        col_h = [jnp.sum(acum * pick[a], axis=1, keepdims=True) for a in range(2)]
        dt_x = sum(jnp.sum(dt * pick[a], axis=1, keepdims=True) * msk for a, msk in enumerate((m0, m1)))
        ac_x = col_h[0] * m0 + col_h[1] * m1
        a_end = jnp.sum(ac_x * last_row, axis=0, keepdims=True)
        xdt = xs[p] * dt_x
        y = _mm(cm[g], hs[p]) * jnp.exp(ac_x)
        for a, msk in enumerate((m0, m1)):
            row_h = jnp.sum(acum_t * (sub == 2 * p + a).astype(F32), axis=0, keepdims=True)
            decay = jnp.exp(jnp.where(causal, col_h[a] - row_h, NEG))
            y = y + _mm(gmat[g] * decay, xdt * msk)
        st = _mm(bm[g], xdt * jnp.exp(a_end - ac_x), TN)
        hn.append(hs[p] * jnp.exp(a_end) + st)
        y = y + dskx[p] * xs[p]
        ys.append(y * _silu(z[p]))
    out = []
    for g in range(2):
        ms = sum(jnp.sum(ys[p] * ys[p], axis=1, keepdims=True) for p in range(4 * g, 4 * g + 4)) * (1.0 / 512)
        rstd = lax.rsqrt(ms + EPS)
        out += [ys[p] * rstd * nw[p] for p in range(4 * g, 4 * g + 4)]
    return out, hn


def _tiles(ref, n, off=0):
    return [ref[:, off + LANES * p:off + LANES * (p + 1)] for p in range(n)]


def _ssd_load(xbc_ref, z_ref, dt_ref, al16_ref, dtb_ref, dsk_ref, nw_ref):
    return (_tiles(xbc_ref, 8), _tiles(xbc_ref, 2, 1024), _tiles(xbc_ref, 2, 1280), dt_ref[...], _tiles(z_ref, 8)), \
           (al16_ref[...], dtb_ref[...], _tiles(dsk_ref, 8), _tiles(nw_ref, 8))


_NCH = SEQ // BLK


def _ssd_param_specs():
    return [_full((1, LANES)), _full((1, LANES)), _full((1, 1024)), _full((1, 1024))]


def _ssd_fwd(xbc_act, proj, al16, dtb, dskx, nw, name):
    def body(xbc_ref, z_ref, dt_ref, al16_ref, dtb_ref, dsk_ref, nw_ref, y_ref, hin_ref, h_scr):
        @pl.when(pl.program_id(0) == 0)
        def _():
            h_scr[...] = jnp.zeros_like(h_scr)
        acts, params = _ssd_load(xbc_ref, z_ref, dt_ref, al16_ref, dtb_ref, dsk_ref, nw_ref)
        hs = _tiles(h_scr, 8)
        hin_ref[0] = h_scr[...]
        ys, hn = _ssd_chunk(*acts, hs, *params)
        for p in range(8):
            y_ref[:, LANES * p:LANES * (p + 1)] = ys[p]
            h_scr[:, LANES * p:LANES * (p + 1)] = hn[p]

    return pl.pallas_call(
        body, name=name, out_shape=(_sds((SEQ, 1024)), _sds((_NCH, BLK, 1024))), grid=(_NCH,),
        in_specs=[pl.BlockSpec((BLK, CONV_CH), lambda c: (c, 0)), pl.BlockSpec((BLK, 1024), lambda c: (c, ZB // 1024)),
                  pl.BlockSpec((BLK, LANES), lambda c: (c, DTC // LANES))] + _ssd_param_specs(),
        out_specs=(pl.BlockSpec((BLK, 1024), lambda c: (c, 0)), pl.BlockSpec((1, BLK, 1024), lambda c: (c, 0, 0))),
        scratch_shapes=[pltpu.VMEM((BLK, 1024), F32)], compiler_params=_cp())(xbc_act, proj, proj, al16, dtb, dskx, nw)


def _ssd_bwd(xbc_act, proj, hin, dyb, al16, dtb, dskx, nw, name):
    def body(xbc_ref, z_ref, dt_ref, hin_ref, dy_ref, al16_ref, dtb_ref, dsk_ref, nw_ref,
             dxbc_ref, dz_ref, ddt_ref, dal16_ref, ddtb_ref, ddsk_ref, dnw_ref, dh_scr):
        @pl.when(pl.program_id(0) == 0)
        def _():
            dh_scr[...] = jnp.zeros_like(dh_scr)
            for r in (dal16_ref, ddtb_ref, ddsk_ref, dnw_ref):
                r[...] = jnp.zeros_like(r)
        acts, params = _ssd_load(xbc_ref, z_ref, dt_ref, al16_ref, dtb_ref, dsk_ref, nw_ref)
        hs = [hin_ref[0, :, LANES * p:LANES * (p + 1)] for p in range(8)]
        _, vjp = jax.vjp(lambda a, h, q: _ssd_chunk(*a, h, *q), acts, hs, params)
        (dxs, dbm, dcm, ddt, dz), dhs, (dal16, ddtb, ddsk, dnw) = vjp((_tiles(dy_ref, 8), _tiles(dh_scr, 8)))
        for p in range(8):
            cols = slice(LANES * p, LANES * (p + 1))
            dxbc_ref[:, cols] = dxs[p]
            dz_ref[:, cols] = dz[p]
            dh_scr[:, cols] = dhs[p]
            ddsk_ref[:, cols] += ddsk[p]
            dnw_ref[:, cols] += dnw[p]
        for g in range(2):
            dxbc_ref[:, 1024 + LANES * g:1024 + LANES * (g + 1)] = dbm[g]
            dxbc_ref[:, 1280 + LANES * g:1280 + LANES * (g + 1)] = dcm[g]
        ddt_ref[...] = ddt
        dal16_ref[...] += dal16
        ddtb_ref[...] += ddtb

    rev = lambda c: _NCH - 1 - c
    return pl.pallas_call(
        body, name=name,
        out_shape=(_sds((SEQ, CONV_CH)), _sds((SEQ, 1024)), _sds((SEQ, LANES)),
                   _sds((1, LANES)), _sds((1, LANES)), _sds((1, 1024)), _sds((1, 1024))),
        grid=(_NCH,),
        in_specs=[pl.BlockSpec((BLK, CONV_CH), lambda c: (rev(c), 0)), pl.BlockSpec((BLK, 1024), lambda c: (rev(c), ZB // 1024)),
                  pl.BlockSpec((BLK, LANES), lambda c: (rev(c), DTC // LANES)), pl.BlockSpec((1, BLK, 1024), lambda c: (rev(c), 0, 0)),
                  pl.BlockSpec((BLK, 1024), lambda c: (rev(c), 0))] + _ssd_param_specs(),
        out_specs=(pl.BlockSpec((BLK, CONV_CH), lambda c: (rev(c), 0)), pl.BlockSpec((BLK, 1024), lambda c: (rev(c), 0)),
                   pl.BlockSpec((BLK, LANES), lambda c: (rev(c), 0)),
                   _full((1, LANES)), _full((1, LANES)), _full((1, 1024)), _full((1, 1024))),
        scratch_shapes=[pltpu.VMEM((BLK, 1024), F32)], compiler_params=_cp())(xbc_act, proj, proj, hin, dyb, al16, dtb, dskx, nw)


def _rstd(v):
    return lax.rsqrt(jnp.mean(v * v, axis=1, keepdims=True) + EPS)


def _rms_bwd(dn, n, rstd):
    return rstd * (dn - n * jnp.mean(dn * n, axis=1, keepdims=True))


_VEC = _full((1, D))


def _layer_spec(layer):
    return pl.BlockSpec((None, 2048, D), lambda *_: (layer, 0, 0))

_ROW = pl.BlockSpec((TM, D), lambda i, *_: (i, 0))


def _proj_fwd(x, pre_w, scale, shift, w, layer, name):
    tn = 1024

    def body(x_ref, pw_ref, sc_ref, sh_ref, w_ref, o_ref, h_ref):
        @pl.when(pl.program_id(1) == 0)
        def _():
            xv = x_ref[...]
            h = (xv * _rstd(xv) * pw_ref[...]) * (1.0 + sc_ref[...]) + sh_ref[...]
            h_ref[...] = h.astype(h_ref.dtype)
        o_ref[...] = jnp.dot(h_ref[...], w_ref[...].astype(MXU), preferred_element_type=F32)

    return pl.pallas_call(body, name=name, out_shape=(_sds((SEQ, NP)), _sds((SEQ, D), MXU)), grid=(SEQ // TM, NP // tn),
                          in_specs=[_ROW, _VEC, _VEC, _VEC, pl.BlockSpec((None, D, tn), lambda i, j: (layer, 0, j))],
                          out_specs=(pl.BlockSpec((TM, tn), lambda i, j: (i, j)), _ROW), compiler_params=_cp())(x, pre_w, scale, shift, w)


def _out_fwd(ya, yb, yc, w, layer, x, gate, post_w, name):
    def body(ya_ref, yb_ref, yc_ref, w_ref, x_ref, g_ref, pw_ref, xn_ref, y_ref):
        y = _mm(ya_ref[...], w_ref[0:512, :]) + _mm(yb_ref[...], w_ref[512:1536, :]) + _mm(yc_ref[...], w_ref[1536:2048, :])
        y_ref[...] = y
        xn_ref[...] = x_ref[...] + g_ref[...] * (y * _rstd(y) * pw_ref[...])

    half = pl.BlockSpec((TM, 512), lambda i: (i, 0))
    return pl.pallas_call(body, name=name, out_shape=(_sds((SEQ, D)), _sds((SEQ, D))), grid=(SEQ // TM,),
                          in_specs=[half, _ROW, half, _layer_spec(layer), _ROW, _VEC, _VEC],
                          out_specs=(_ROW, _ROW), compiler_params=_cp())(ya, yb, yc, w, x, gate, post_w)


def _post_bwd(dxo, y, gate, post_w, name):
    def body(dx_ref, y_ref, g_ref, pw_ref, dy_ref, dg_ref, dpw_ref):
        @pl.when(pl.program_id(0) == 0)
        def _():
            dg_ref[...] = jnp.zeros_like(dg_ref)
            dpw_ref[...] = jnp.zeros_like(dpw_ref)
        dx, y = dx_ref[...], y_ref[...]
        rstd = _rstd(y)
        n = y * rstd
        dg_ref[...] += jnp.sum(dx * (n * pw_ref[...]), axis=0, keepdims=True)
        dr = dx * g_ref[...]
        dpw_ref[...] += jnp.sum(dr * n, axis=0, keepdims=True)
        dy_ref[...] = _rms_bwd(dr * pw_ref[...], n, rstd)

    return pl.pallas_call(body, name=name, out_shape=(_sds((SEQ, D)), _sds((1, D)), _sds((1, D))), grid=(SEQ // TM,),
                          in_specs=[_ROW, _ROW, _VEC, _VEC], out_specs=(_ROW, _VEC, _VEC), compiler_params=_cp())(dxo, y, gate, post_w)


def _dymix(dy, w, layer, name):
    def body(dy_ref, w_ref, a_ref, b_ref, c_ref):
        dy = dy_ref[...]
        a_ref[...] = _mm(dy, w_ref[0:512, :], NT)
        b_ref[...] = _mm(dy, w_ref[512:1536, :], NT)
        c_ref[...] = _mm(dy, w_ref[1536:2048, :], NT)

    half = pl.BlockSpec((TM, 512), lambda i: (i, 0))
    return pl.pallas_call(body, name=name, out_shape=(_sds((SEQ, 512)), _sds((SEQ, D)), _sds((SEQ, 512))), grid=(SEQ // TM,),
                          in_specs=[_ROW, _layer_spec(layer)], out_specs=(half, _ROW, half), compiler_params=_cp())(dy, w)


def _dwout(ya, yb, yc, dy, name):
    def body(ya_ref, yb_ref, yc_ref, dy_ref, o_ref):
        @pl.when(pl.program_id(0) == 0)
        def _():
            o_ref[...] = jnp.zeros_like(o_ref)
        dy = dy_ref[...]
        o_ref[0:512, :] += _mm(ya_ref[...], dy, TN)
        o_ref[512:1536, :] += _mm(yb_ref[...], dy, TN)
        o_ref[1536:2048, :] += _mm(yc_ref[...], dy, TN)

    half = pl.BlockSpec((TM, 512), lambda i: (i, 0))
    return pl.pallas_call(body, name=name, out_shape=_sds((2048, D)), grid=(SEQ // TM,),
                          in_specs=[half, _ROW, half, _ROW], out_specs=_full((2048, D)), compiler_params=_cp())(ya, yb, yc, dy)


_TK = 1536


def _dwin(h, dproj, name):
    def body(h_ref, dp_ref, o_ref):
        @pl.when(pl.program_id(1) == 0)
        def _():
            o_ref[...] = jnp.zeros_like(o_ref)
        o_ref[...] += _mm(h_ref[...], dp_ref[...], TN)

    return pl.pallas_call(body, name=name, out_shape=_sds((D, NP)), grid=(NP // _TK, SEQ // TM),
                          in_specs=[pl.BlockSpec((TM, D), lambda j, k: (k, 0)), pl.BlockSpec((TM, _TK), lambda j, k: (k, j))],
                          out_specs=pl.BlockSpec((D, _TK), lambda j, k: (0, j)), compiler_params=_cp())(h, dproj)


def _dh_bwd(dproj, w, layer, x, pre_w, scale, dxo, name):
    nk = NP // _TK

    def body(dp_ref, w_ref, x_ref, pw_ref, sc_ref, dxo_ref, dx_ref, dsh_ref, dsc_ref, dpw_ref, acc):
        i, k = pl.program_id(0), pl.program_id(1)

        @pl.when((i == 0) & (k == 0))
        def _():
            for r in (dsh_ref, dsc_ref, dpw_ref):
                r[...] = jnp.zeros_like(r)

        @pl.when(k == 0)
        def _():
            acc[...] = jnp.zeros_like(acc)
        acc[...] += _mm(dp_ref[...], w_ref[...], NT)

        @pl.when(k == nk - 1)
        def _():
            dh, xv = acc[...], x_ref[...]
            rstd = _rstd(xv)
            n = xv * rstd
            dsh_ref[...] += jnp.sum(dh, axis=0, keepdims=True)
            dsc_ref[...] += jnp.sum(dh * (n * pw_ref[...]), axis=0, keepdims=True)
            dhn = dh * (1.0 + sc_ref[...])
            dpw_ref[...] += jnp.sum(dhn * n, axis=0, keepdims=True)
            dx_ref[...] = _rms_bwd(dhn * pw_ref[...], n, rstd) + dxo_ref[...]

    return pl.pallas_call(body, name=name, out_shape=(_sds((SEQ, D)), _sds((1, D)), _sds((1, D)), _sds((1, D))),
                          grid=(SEQ // TM, nk),
                          in_specs=[pl.BlockSpec((TM, _TK), lambda i, k: (i, k)), pl.BlockSpec((None, D, _TK), lambda i, k: (layer, 0, k)),
                                    _ROW, _VEC, _VEC, _ROW],
                          out_specs=(_ROW, _VEC, _VEC, _VEC), scratch_shapes=[pltpu.VMEM((TM, D), F32)],
                          compiler_params=_cp())(dproj, w, x, pre_w, scale, dxo)


def _loss_bwd(xf, tgt, name):
    def body(x_ref, t_ref, dx_ref, l_ref):
        @pl.when(pl.program_id(0) == 0)
        def _():
            l_ref[...] = jnp.zeros_like(l_ref)
        e = x_ref[...] - t_ref[...]
        dx_ref[...] = e * (1.0 / D)
        l_ref[...] += 0.5 * jnp.sum(jnp.mean(e * e, axis=1, keepdims=True), axis=0, keepdims=True)

    return pl.pallas_call(body, name=name, out_shape=(_sds((SEQ, D)), _sds((8, LANES))), grid=(SEQ // TM,),
                          in_specs=[_ROW, _ROW], out_specs=(_ROW, _full((8, LANES))), compiler_params=_cp())(xf, tgt)


def _mod_part(c_all, ada_w, ada_b, name):
    def body(c_ref, w_ref, b_ref, o_ref):
        o_ref[0] = _mm(_silu(c_ref[...]), w_ref[0]) + b_ref[0]

    return pl.pallas_call(body, name=name, out_shape=_sds((DEPTH, 8, 768)), grid=(DEPTH,),
                          in_specs=[_full((8, D)), pl.BlockSpec((1, D, 768), lambda i: (i, 0, 0)), pl.BlockSpec((1, 1, 768), lambda i: (i, 0, 0))],
                          out_specs=pl.BlockSpec((1, 8, 768), lambda i: (i, 0, 0)), compiler_params=_cp())(c_all, ada_w, ada_b)


def _ada_grad(c_t, dmod, name):
    def body(c_ref, d_ref, o_ref):
        ca = _silu(c_ref[...])
        dm = d_ref[0]
        acc = ca[:, 0:1] * dm[0:1, :]
        for s in range(1, 8):
            acc = acc + ca[:, s:s + 1] * dm[s:s + 1, :]
        o_ref[0] = acc

    return pl.pallas_call(body, name=name, out_shape=_sds((DEPTH, D, 768)), grid=(DEPTH,),
                          in_specs=[_full((D, LANES)), pl.BlockSpec((1, 8, 768), lambda i: (i, 0, 0))],
                          out_specs=pl.BlockSpec((1, D, 768), lambda i: (i, 0, 0)), compiler_params=_cp())(c_t, dmod)


def _pack(parts):
    flat = []
    for p in parts:
        f = p.reshape(-1)
        flat.append(jnp.pad(f, (0, (-f.size) % LANES)))
    v = jnp.concatenate(flat)
    return jnp.pad(v, (0, (-v.size) % (8 * LANES))).reshape(-1, LANES)


def _unpack(v, shapes):
    v = v.reshape(-1)
    out, off = [], 0
    for s in shapes:
        n = math.prod(s)
        out.append(v[off:off + n].reshape(s))
        off += n + (-n) % LANES
    return out


_GIVEN_DT, _GIVEN_C = 4608, 4624


def _pad_cols(w):
    return jnp.concatenate([w[..., :_GIVEN_DT], w[..., _GIVEN_C:], w[..., _GIVEN_DT:_GIVEN_C],
                            jnp.zeros(w.shape[:-1] + (NP - IN_COLS,), w.dtype)], axis=-1)


def _unpad_cols(w):
    return jnp.concatenate([w[..., :_GIVEN_DT], w[..., DTC:DTC + 16], w[..., _GIVEN_DT:DTC]], axis=-1)


def _pad_lanes(v):
    return jnp.pad(v, (0, LANES - v.shape[0])).reshape(1, LANES)


def _local_step(x2, tgt, mod, weights_of, grads_done, pre_w, post_w, conv_w, conv_b, dt_bias, a_log, d_skip, nw, sinks):
    saved = []
    xcur = x2
    for i in range(DEPTH):
        shift, scale, gate = mod[i:i + 1, :D], mod[i:i + 1, D:2 * D], mod[i:i + 1, 2 * D:]
        pw, qw = pre_w[i:i + 1], post_w[i:i + 1]
        w_p, w_o = weights_of(i, xcur)
        proj, h = _proj_fwd(xcur, pw, scale, shift, w_p, 0, "proj_fwd")
        o_a, lse_a = _attn_fwd(proj, QA // LANES, KA // LANES, VA // LANES, DILS, False, None, "attn_a_fwd")
        ya = _gate_fwd(o_a, proj, ZA // 512, "gate_a_fwd")
        sink_x = jnp.repeat(sinks[i], HD).reshape(1, 512)
        o_c, lse_c = _attn_fwd(proj, QC // LANES, KC // LANES, VC // LANES, (1,), True, sink_x, "attn_c_fwd")
        yc = _gate_fwd(o_c, proj, ZC // 512, "gate_c_fwd")
        cw, cb = conv_w[i], conv_b[i:i + 1]
        xbc_act = _conv_fwd(proj, cw, cb, "conv_fwd")
        ssd_p = (_pad_lanes(a_log[i]), _pad_lanes(dt_bias[i]), jnp.repeat(d_skip[i], HD).reshape(1, 1024), nw[i:i + 1])
        yb, hin = _ssd_fwd(xbc_act, proj, *ssd_p, "ssd_fwd")
        xnew, y = _out_fwd(ya, yb, yc, w_o, 0, xcur, gate, qw, "out_fwd")
        saved.append((w_p, w_o, xcur, scale, gate, pw, qw, proj, h, o_a, lse_a, ya, sink_x, o_c, lse_c, yc, cw, cb, xbc_act, ssd_p, yb, hin, y))
        xcur = xnew
    dx, ltile = _loss_bwd(xcur, tgt, "loss")
    dmod, small = [None] * DEPTH, [None] * DEPTH
    for i in reversed(range(DEPTH)):
        w_p, w_o, xin, scale, gate, pw, qw, proj, h, o_a, lse_a, ya, sink_x, o_c, lse_c, yc, cw, cb, xbc_act, ssd_p, yb, hin, y = saved[i]
        dy, dgate, dpost = _post_bwd(dx, y, gate, qw, "post_bwd")
        dya, dyb, dyc = _dymix(dy, w_o, 0, "dymix")
        dwo = _dwout(ya, yb, yc, dy, "dwout")
        do_a, dz_a = _gate_bwd(dya, o_a, proj, ZA // 512, "gate_a_bwd")
        dq_a, dk_a, dv_a = _attn_bwd(proj, QA // LANES, KA // LANES, VA // LANES, do_a, o_a, lse_a, DILS, False, None, "attn_a_bwd")
        do_c, dz_c = _gate_bwd(dyc, o_c, proj, ZC // 512, "gate_c_bwd")
        dq_c, dk_c, dv_c, dsk = _attn_bwd(proj, QC // LANES, KC // LANES, VC // LANES, do_c, o_c, lse_c, (1,), True, sink_x, "attn_c_bwd")
        dxbc_act, dz_b, ddt, dal16, ddtb, ddsk, dnw = _ssd_bwd(xbc_act, proj, hin, dyb, *ssd_p, "ssd_bwd")
        dxbc, dcw, dcb = _conv_bwd(proj, dxbc_act, cw, cb, "conv_bwd")
        dproj = jnp.concatenate([dq_a, dk_a, dv_a, dz_a, dz_b, dxbc, dq_c, dz_c, dk_c, dv_c, ddt,
                                 jnp.zeros((SEQ, NP - DTC - LANES), F32)], axis=1)
        sent = grads_done(i, _dwin(h, dproj, "dwin"), dwo)
        dx, dshift, dscale, dpre = _dh_bwd(dproj, w_p, 0, xin, pw, scale + sent[0, 0], dx, "dh_bwd")
        dmod[i] = jnp.concatenate([dshift, dscale, dgate], axis=1)
        small[i] = (dpre, dpost, dcw, dcb, ddtb[0, :16], dal16[0, :16], ddsk.reshape(16, HD).sum(axis=1), dnw, dsk[:, 0, ::HD].reshape(8))
    return ltile, dx, jnp.concatenate(dmod, axis=0), small


_SMALL = ((1, D), (1, D), (4, CONV_CH), (1, CONV_CH), (16,), (16,), (16,), (1, D), (8,))


def kernel(x, c, ada_w, ada_b, pre_norm_w, post_norm_w, w_in, conv_w, conv_b, dt_bias, a_log, d_skip, ssm_norm_w, sinks, w_out, loss_target, m_ada_w, m_ada_b, m_pre_norm_w, m_post_norm_w, m_w_in, m_conv_w, m_conv_b, m_dt_bias, m_a_log, m_d_skip, m_ssm_norm_w, m_sinks, m_w_out, v_ada_w, v_ada_b, v_pre_norm_w, v_post_norm_w, v_w_in, v_conv_w, v_conv_b, v_dt_bias, v_a_log, v_d_skip, v_ssm_norm_w, v_sinks, v_w_out):
    xi, yi, ci = lax.axis_index("x"), lax.axis_index("y"), lax.axis_index("c")
    chip = 2 * xi + yi
    me = 2 * chip + ci

    w_in_b = _cast_bf16(w_in.reshape(DEPTH * D, SHARD_IN), 512, "cast_w_in").reshape(DEPTH, D, SHARD_IN)
    w_out_b = _cast_bf16(w_out.reshape(DEPTH * 512, D), 512, "cast_w_out").reshape(DEPTH, 512, D)
    gathers = []
    for i in range(DEPTH):
        lands = [lax.dynamic_update_slice(lax.empty((4,) + a.shape[1:], a.dtype), a[i][None], (chip, 0, 0)) for a in (w_in_b, w_out_b)]
        gathers.append(_split_start(None, lands, f"gather_start{i}"))
    all_started = gathers[0][3] + gathers[1][3] + gathers[2][3] + gathers[3][3]

    def weights_of(i, after):
        send_sems, recv_sems, thru, _ = gathers[i]
        if i == 0:
            after = all_started + mod[:1, :LANES]
        g_in, g_out = _split_wait(send_sems, recv_sems, thru, 2, after, f"gather_wait{i}")
        w_p = _pad_cols(jnp.concatenate([g_in[k] for k in range(4)], axis=-1))
        return w_p[None], jnp.concatenate([g_out[k] for k in range(4)], axis=0)[None]

    scatters = [None] * DEPTH

    def grads_done(i, dwi_p, dwo):
        dwi = _unpad_cols(dwi_p)
        blk_in = jnp.stack([dwi[:, SHARD_IN * k:SHARD_IN * (k + 1)] for k in range(4)]).astype(jnp.bfloat16)
        blk_out = dwo.reshape(4, 512, D).astype(jnp.bfloat16)
        lands = [lax.empty(blk_in.shape, blk_in.dtype), lax.empty(blk_out.shape, blk_out.dtype)]
        scatters[i] = _split_start([blk_in, blk_out], lands, f"scatter_start{i}")
        return scatters[i][3]

    g0 = _allgather8(_pack([c, conv_w]), "gather_c")
    c_all = g0[:, :8, :].reshape(8, D)
    conv_w_full = jnp.concatenate([g0[2 * k, 8:56, :].reshape(DEPTH, 4, CONV_CH // 4) for k in range(4)], axis=-1)

    ada_b_mine = lax.dynamic_slice_in_dim(ada_b, 768 * chip, 768, axis=1).reshape(DEPTH, 1, 768)
    gm = _allgather8(_mod_part(c_all, ada_w, ada_b_mine, "mod_part").reshape(DEPTH * 8, 768), "gather_mod")
    gm = gm.reshape(4, 2, DEPTH, 8, 768)[:, 0]
    mod = lax.dynamic_index_in_dim(gm, me, axis=2, keepdims=False).transpose(1, 0, 2).reshape(DEPTH, 3 * D)

    ltile, dx, dmod, small = _local_step(x[0], loss_target[0], mod, weights_of, grads_done, pre_norm_w, post_norm_w, conv_w_full,
                                         conv_b, dt_bias, a_log, d_skip, ssm_norm_w, sinks)

    packed = _pack([dmod] + [g for layer in small for g in layer] + [ltile[0]])
    gs = _allgather8(packed, "gather_small")
    tot = _sum_blocks(gs, packed.shape[0], "sum_small")
    parts = _unpack(tot, [(DEPTH, 3 * D)] + list(_SMALL) * DEPTH + [(LANES,)])
    g_ada_b, loss = parts[0], parts[-1][0]
    per_layer = [parts[1 + len(_SMALL) * i:1 + len(_SMALL) * (i + 1)] for i in range(DEPTH)]
    g_pre, g_post, g_cw, g_cb, g_dtb, g_al, g_dsk, g_nw, g_sk = [jnp.stack([per_layer[i][j] for i in range(DEPTH)]) for j in range(len(_SMALL))]
    g_pre, g_post, g_cb, g_nw = g_pre[:, 0], g_post[:, 0], g_cb[:, 0], g_nw[:, 0]
    g_cw = lax.dynamic_slice_in_dim(g_cw, (CONV_CH // 4) * chip, CONV_CH // 4, axis=2)

    dmod_all = gs[:, :(DEPTH * 3 * D) // LANES, :].reshape(8, DEPTH, 3 * D).transpose(1, 0, 2)
    dmod_mine = lax.dynamic_slice_in_dim(dmod_all, 768 * chip, 768, axis=2)
    c_t = jnp.pad(c_all.T, ((0, 0), (0, LANES - 8)))
    g_ada_w = _ada_grad(c_t, dmod_mine, "ada_grad")

    r_in, r_out = [], []
    for i in range(DEPTH):
        send_sems, recv_sems, thru, _ = scatters[i]
        done = _split_wait(send_sems, recv_sems, thru, 2, dx, f"scatter_wait{i}")
        for r, land, src in zip((r_in, r_out), done[2:], done[:2]):
            own = lax.dynamic_index_in_dim(src, chip, axis=0, keepdims=True)
            r.append(lax.dynamic_update_slice(land, own, (chip, 0, 0)))
    r_in, r_out = jnp.stack(r_in, axis=1), jnp.stack(r_out, axis=1)
    p_in = _sum_blocks(r_in.reshape(4, DEPTH * D, SHARD_IN), 256, "sum_w_in")
    p_out = _sum_blocks(r_out.reshape(4, DEPTH * 512, D), 512, "sum_w_out")
    s_in, s_out = _sibling_swap([p_in, p_out], "swap_partials")

    res = {}
    res["ada_w"] = [a.reshape(DEPTH, D, 768) for a in
                    _adamw(ada_w.reshape(DEPTH * D, 768), [g_ada_w.reshape(DEPTH * D, 768)], m_ada_w.reshape(DEPTH * D, 768),
                           v_ada_w.reshape(DEPTH * D, 768), 512, "adamw_ada_w")]
    res["w_in"] = [a.reshape(DEPTH, D, SHARD_IN) for a in
                   _adamw(w_in.reshape(DEPTH * D, SHARD_IN), [p_in, s_in], m_w_in.reshape(DEPTH * D, SHARD_IN),
                          v_w_in.reshape(DEPTH * D, SHARD_IN), 256, "adamw_w_in")]
    res["w_out"] = [a.reshape(DEPTH, 512, D) for a in
                    _adamw(w_out.reshape(DEPTH * 512, D), [p_out, s_out], m_w_out.reshape(DEPTH * 512, D),
                           v_w_out.reshape(DEPTH * 512, D), 512, "adamw_w_out")]
    names = ["ada_b", "pre_norm_w", "post_norm_w", "conv_w", "conv_b", "dt_bias", "a_log", "d_skip", "ssm_norm_w", "sinks"]
    ws = [ada_b, pre_norm_w, post_norm_w, conv_w, conv_b, dt_bias, a_log, d_skip, ssm_norm_w, sinks]
    gsm = [g_ada_b, g_pre, g_post, g_cw, g_cb, g_dtb, g_al, g_dsk, g_nw, g_sk]
    ms = [m_ada_b, m_pre_norm_w, m_post_norm_w, m_conv_w, m_conv_b, m_dt_bias, m_a_log, m_d_skip, m_ssm_norm_w, m_sinks]
    vs = [v_ada_b, v_pre_norm_w, v_post_norm_w, v_conv_w, v_conv_b, v_dt_bias, v_a_log, v_d_skip, v_ssm_norm_w, v_sinks]
    pw_, pg_, pm_, pv_ = _pack(ws), _pack(gsm), _pack(ms), _pack(vs)
    small_out = _adamw(pw_, [pg_], pm_, pv_, pw_.shape[0], "adamw_small")
    shapes = [w.shape for w in ws]
    for kind in range(4):
        for nm, a in zip(names, _unpack(small_out[kind], shapes)):
            res.setdefault(nm, [None] * 4)[kind] = a
    order = ["ada_w", "ada_b", "pre_norm_w", "post_norm_w", "w_in", "conv_w", "conv_b", "dt_bias", "a_log", "d_skip", "ssm_norm_w", "sinks", "w_out"]
    return (loss, dx[None], *[res[n][0] for n in order], *[res[n][1] for n in order], *[res[n][2] for n in order], *[res[n][3] for n in order])
```

```python
import math

import jax
import jax.numpy as jnp
from jax import lax
from jax.experimental import pallas as pl
from jax.experimental.pallas import tpu as pltpu

F32 = jnp.float32
MXU = jnp.bfloat16
HI = lax.Precision.HIGHEST
MESH = pl.DeviceIdType.MESH

SEQ = 4096
D = 1024
DEPTH = 4
HD = 64
LANES = 128
BLK = 128
DILS = (1, 4, 16)
NEG = -1e30
EPS = 1e-6
MIB = 1024 * 1024

NP = 6144
QA, KA, VA, ZA = 0, 512, 1024, 1536
ZB, XBC = 2048, 3072
QC, ZC, KC, VC = 4608, 5120, 5632, 5760
DTC = 5888
IN_COLS = 5904
SHARD_IN = IN_COLS // 4
CONV_CH = 1536
TM = 512

ADAM_LR, ADAM_B1, ADAM_B2, ADAM_EPS, ADAM_WD, ADAM_STEP = 0.001, 0.9, 0.999, 1e-08, 0.01, 10

NT = (((1,), (1,)), ((), ()))
TN = (((0,), (0,)), ((), ()))


def _cp(vmem_mib=48):
    return pltpu.CompilerParams(vmem_limit_bytes=vmem_mib * MIB)


def _sds(shape, dtype=F32):
    return jax.ShapeDtypeStruct(shape, dtype)


def _full(shape):
    n = len(shape)
    return pl.BlockSpec(shape, lambda *_: (0,) * n)


def _mm(a, b, dims=None):
    if dims is None:
        return jnp.dot(a.astype(MXU), b.astype(MXU), preferred_element_type=F32)
    return lax.dot_general(a.astype(MXU), b.astype(MXU), dims, preferred_element_type=F32)


def _sigmoid(x):
    return 1.0 / (1.0 + jnp.exp(-x))


def _silu(x):
    return x * _sigmoid(x)


def _dsilu(x):
    s = _sigmoid(x)
    return s * (1.0 + x * (1.0 - s))


def _softplus(x):
    ax = jnp.where(x >= 0, x, -x)
    return jnp.maximum(x, 0.0) + jnp.log1p(jnp.exp(-ax))


def _half_masks():
    lane = lax.broadcasted_iota(jnp.int32, (1, LANES), 1)
    m0 = (lane < HD).astype(F32)
    return m0, 1.0 - m0


def _allgather8(v, name):
    r, cc = v.shape

    def body(v_ref, out_ref, send_sems, recv_sems):
        x, y, c = lax.axis_index("x"), lax.axis_index("y"), lax.axis_index("c")
        me = 4 * x + 2 * y + c
        out_ref[me] = v_ref[...]
        peers = []
        for k in range(1, 8):
            px = 1 - x if k & 4 else x
            py = 1 - y if k & 2 else y
            pc = 1 - c if k & 1 else c
            peers.append((px, py, pc))
        sends = []
        for k, peer in enumerate(peers):
            cp = pltpu.make_async_remote_copy(src_ref=v_ref, dst_ref=out_ref.at[me], send_sem=send_sems.at[k],
                                              recv_sem=recv_sems.at[k], device_id=peer, device_id_type=MESH)
            cp.start()
            sends.append(cp)
        for k, (px, py, pc) in enumerate(peers):
            pltpu.make_async_remote_copy(src_ref=v_ref, dst_ref=out_ref.at[4 * px + 2 * py + pc], send_sem=send_sems.at[k],
                                         recv_sem=recv_sems.at[k], device_id=(px, py, pc), device_id_type=MESH).wait_recv()
        for cp in sends:
            cp.wait_send()

    return pl.pallas_call(
        body, name=name, out_shape=_sds((8, r, cc)),
        in_specs=[pl.BlockSpec(memory_space=pltpu.VMEM)], out_specs=pl.BlockSpec(memory_space=pltpu.VMEM),
        scratch_shapes=[pltpu.SemaphoreType.DMA((7,)), pltpu.SemaphoreType.DMA((7,))],
        compiler_params=_cp(32),
    )(v)


_HBM = pl.BlockSpec(memory_space=pltpu.HBM)
_SEM = pl.BlockSpec(memory_space=pltpu.SEMAPHORE)
_EFFECT = pltpu.SideEffectType.DATAFLOW_SIDE_EFFECTING


def _chip_copies(src_refs, land_refs, send_sems, recv_sems):
    x, y, c = lax.axis_index("x"), lax.axis_index("y"), lax.axis_index("c")
    mine = 2 * x + y
    out = []
    for i, land in enumerate(land_refs):
        for j, (px, py) in enumerate([(1 - x, y), (x, 1 - y), (1 - x, 1 - y)]):
            src = src_refs[i].at[2 * px + py] if src_refs else land.at[mine]
            mk = lambda dst, i=i, j=j, src=src, px=px, py=py: pltpu.make_async_remote_copy(
                src_ref=src, dst_ref=dst, send_sem=send_sems.at[3 * i + j], recv_sem=recv_sems.at[3 * i + j],
                device_id=(px, py, c), device_id_type=MESH)
            out.append((mk(land.at[mine]), mk(land.at[2 * px + py])))
    return out


def _split_start(srcs, lands, name):
    ops = list(srcs or []) + list(lands)
    ns, n = len(srcs or []), len(lands)

    def body(*refs):
        src_refs, land_refs = refs[:ns], refs[ns:ns + n]
        send_sems, recv_sems = refs[ns + n], refs[ns + n + 1]
        for mine_out, _ in _chip_copies(src_refs, land_refs, send_sems, recv_sems):
            mine_out.start()
        refs[-1][...] = jnp.zeros_like(refs[-1])

    sems = pltpu.SemaphoreType.DMA((3 * n,))
    res = pl.pallas_call(
        body, name=name, out_shape=(sems, sems) + tuple(pltpu.HBM(a.shape, a.dtype) for a in ops) + (_sds((8, LANES)),),
        in_specs=[_HBM] * len(ops), out_specs=(_SEM, _SEM) + (_HBM,) * len(ops) + (pl.BlockSpec(memory_space=pltpu.VMEM),),
        input_output_aliases={k: 2 + k for k in range(len(ops))},
        compiler_params=pltpu.CompilerParams(has_side_effects=_EFFECT),
    )(*[pltpu.with_memory_space_constraint(a, pltpu.HBM) for a in ops])
    return res[0], res[1], list(res[2:2 + len(ops)]), res[-1]


def _split_wait(send_sems, recv_sems, thru, n, after, name):
    ns = len(thru) - n

    def body(*refs):
        src_refs, land_refs = refs[:ns], refs[ns:ns + n]
        for mine_out, arriving in _chip_copies(src_refs, land_refs, refs[ns + n], refs[ns + n + 1]):
            mine_out.wait_send()
            arriving.wait_recv()

    res = pl.pallas_call(
        body, name=name, out_shape=tuple(pltpu.HBM(a.shape, a.dtype) for a in thru),
        in_specs=[_HBM] * len(thru) + [_SEM, _SEM, pl.BlockSpec(memory_space=pl.ANY)], out_specs=(_HBM,) * len(thru),
        input_output_aliases={k: k for k in range(len(thru))},
        compiler_params=pltpu.CompilerParams(has_side_effects=_EFFECT),
    )(*thru, send_sems, recv_sems, after)
    return list(res)


def _sibling_swap(arrs, name):
    n = len(arrs)

    def body(*refs):
        ins, outs_, (send_sems, recv_sems) = refs[:n], refs[n:2 * n], refs[2 * n:]
        sib = (lax.axis_index("x"), lax.axis_index("y"), 1 - lax.axis_index("c"))
        cps = [pltpu.make_async_remote_copy(src_ref=ins[i], dst_ref=outs_[i], send_sem=send_sems.at[i], recv_sem=recv_sems.at[i],
                                            device_id=sib, device_id_type=MESH) for i in range(n)]
        for cp in cps:
            cp.start()
        for cp in cps:
            cp.wait_recv()
        for cp in cps:
            cp.wait_send()

    hbm = pl.BlockSpec(memory_space=pltpu.HBM)
    return pl.pallas_call(
        body, name=name, out_shape=tuple(_sds(a.shape, a.dtype) for a in arrs), in_specs=[hbm] * n, out_specs=tuple([hbm] * n),
        scratch_shapes=[pltpu.SemaphoreType.DMA((n,)), pltpu.SemaphoreType.DMA((n,))],
    )(*arrs)


def _cast_bf16(a, rows, name):
    r, cc = a.shape

    def body(a_ref, o_ref):
        o_ref[...] = a_ref[...].astype(jnp.bfloat16)

    return pl.pallas_call(body, name=name, out_shape=_sds((r, cc), jnp.bfloat16), grid=(r // rows,),
                          in_specs=[pl.BlockSpec((rows, cc), lambda i: (i, 0))],
                          out_specs=pl.BlockSpec((rows, cc), lambda i: (i, 0)), compiler_params=_cp())(a)


def _sum_blocks(a, rows, name):
    k, r, cc = a.shape

    def body(a_ref, o_ref):
        acc = a_ref[0].astype(F32)
        for j in range(1, k):
            acc = acc + a_ref[j].astype(F32)
        o_ref[...] = acc

    return pl.pallas_call(body, name=name, out_shape=_sds((r, cc)), grid=(r // rows,),
                          in_specs=[pl.BlockSpec((k, rows, cc), lambda i: (0, i, 0))],
                          out_specs=pl.BlockSpec((rows, cc), lambda i: (i, 0)), compiler_params=_cp())(a)


def _adamw(w, parts, m, v, rows, name):
    r, cc = w.shape
    np_ = len(parts)
    c1 = 1.0 / (1.0 - ADAM_B1 ** ADAM_STEP)
    c2 = 1.0 / (1.0 - ADAM_B2 ** ADAM_STEP)

    def body(*refs):
        w_ref, p_refs, (m_ref, v_ref, g_ref, d_ref, nm_ref, nv_ref) = refs[0], refs[1:1 + np_], refs[1 + np_:]
        g = p_refs[0][...]
        for p_ref in p_refs[1:]:
            g = g + p_ref[...]
        nm = ADAM_B1 * m_ref[...] + (1.0 - ADAM_B1) * g
        nv = ADAM_B2 * v_ref[...] + (1.0 - ADAM_B2) * (g * g)
        g_ref[...] = g
        nm_ref[...] = nm
        nv_ref[...] = nv
        d_ref[...] = -ADAM_LR * ((nm * c1) / (jnp.sqrt(nv * c2) + ADAM_EPS) + ADAM_WD * w_ref[...])

    spec = pl.BlockSpec((rows, cc), lambda i: (i, 0))
    return pl.pallas_call(body, name=name, out_shape=(_sds((r, cc)),) * 4, grid=(r // rows,),
                          in_specs=[spec] * (3 + np_), out_specs=(spec,) * 4, compiler_params=_cp())(w, *parts, m, v)


def _gate_fwd(o, proj, zblk, name):
    def body(o_ref, z_ref, y_ref):
        y_ref[...] = o_ref[...] * _silu(z_ref[...])

    return pl.pallas_call(body, name=name, out_shape=_sds((SEQ, 512)), grid=(SEQ // TM,),
                          in_specs=[pl.BlockSpec((TM, 512), lambda i: (i, 0)), pl.BlockSpec((TM, 512), lambda i: (i, zblk))],
                          out_specs=pl.BlockSpec((TM, 512), lambda i: (i, 0)), compiler_params=_cp())(o, proj)


def _gate_bwd(dy, o, proj, zblk, name):
    def body(dy_ref, o_ref, z_ref, do_ref, dz_ref):
        dy, z = dy_ref[...], z_ref[...]
        do_ref[...] = dy * _silu(z)
        dz_ref[...] = dy * o_ref[...] * _dsilu(z)

    return pl.pallas_call(body, name=name, out_shape=(_sds((SEQ, 512)), _sds((SEQ, 512))), grid=(SEQ // TM,),
                          in_specs=[pl.BlockSpec((TM, 512), lambda i: (i, 0)), pl.BlockSpec((TM, 512), lambda i: (i, 0)),
                                    pl.BlockSpec((TM, 512), lambda i: (i, zblk))],
                          out_specs=(pl.BlockSpec((TM, 512), lambda i: (i, 0)),) * 2, compiler_params=_cp())(dy, o, proj)


_BIAS = pltpu.VMEM((2, 2 * BLK, 2 * BLK), F32)


def _fill_band_bias(bias_ref):
    qi = lax.broadcasted_iota(jnp.int32, (2 * BLK, 2 * BLK), 0) & (BLK - 1)
    kj = lax.broadcasted_iota(jnp.int32, (2 * BLK, 2 * BLK), 1)
    dist = BLK + qi - kj
    band = (dist >= 0) & (dist <= BLK)
    bias_ref[0] = jnp.where(band, 0.0, NEG)
    bias_ref[1] = jnp.where(band & (kj >= BLK), 0.0, NEG)


class _HeadStack:
    def __init__(self, group):
        self.m0, self.m1 = _half_masks()
        self.group = group
        if group is not None:
            self.kv_mask = (self.m0, self.m1)[group]

    def _swap_half(self, t, a):
        return t if a == self.group else pltpu.roll(t, HD, axis=1)

    def stack(self, t):
        t0, t1 = t * self.m0, t * self.m1
        if self.group is not None:
            t0, t1 = self._swap_half(t0, 0), self._swap_half(t1, 1)
        return jnp.concatenate([t0, t1], axis=0)

    def unstack(self, ts):
        if self.group is None:
            return ts[:BLK] * self.m0 + ts[BLK:] * self.m1
        return self._swap_half(ts[:BLK] * self.kv_mask, 0) + self._swap_half(ts[BLK:] * self.kv_mask, 1)


def _rows(st, dil):
    if dil == 1:
        return pl.ds(pl.multiple_of(st, BLK), BLK)
    return pl.ds(st, BLK, stride=dil)


def _block_pos(n, dil):
    nb = SEQ // (dil * BLK)
    r, b = n // nb, n % nb
    hp = (b > 0).astype(jnp.int32)
    st = r + dil * BLK * b
    return st, st - dil * BLK * hp, 1 - hp


def _attn_fwd(proj, qblk, kblk, vblk, dils, gqa, sink_x, name):
    has_sink = sink_x is not None

    def body(*refs):
        if has_sink:
            q_ref, k_ref, v_ref, s_ref, o_ref, lse_ref, m_scr, z_scr, bias_scr = refs
        else:
            q_ref, k_ref, v_ref, o_ref, lse_ref, m_scr, z_scr, bias_scr = refs

        @pl.when(pl.program_id(0) == 0)
        def _():
            _fill_band_bias(bias_scr)
        o_ref[...] = jnp.zeros_like(o_ref)
        if has_sink:
            z_scr[...] = jnp.ones_like(z_scr)
            m_scr[...] = jnp.broadcast_to(s_ref[...], m_scr.shape)
        else:
            z_scr[...] = jnp.zeros_like(z_scr)
            m_scr[...] = jnp.full_like(m_scr, NEG)

        def step(n, carry, dil, heads):
            m0, m1 = heads.m0, heads.m1
            st, stp, first = _block_pos(n, dil)
            rq, rp = _rows(st, dil), _rows(stp, dil)
            kk = jnp.concatenate([k_ref[rp, :], k_ref[rq, :]], axis=0)
            vv = jnp.concatenate([v_ref[rp, :], v_ref[rq, :]], axis=0)
            s = _mm(heads.stack(q_ref[rq, :]), kk, NT) * (HD ** -0.5) + bias_scr[first]
            m = jnp.max(s, axis=1, keepdims=True)
            p = jnp.exp(s - m)
            l = jnp.sum(p, axis=1, keepdims=True)
            o_pair = heads.unstack(_mm(p, vv))
            m_pair = m[:BLK] * m0 + m[BLK:] * m1
            l_pair = l[:BLK] * m0 + l[BLK:] * m1
            m_old = m_scr[rq, :]
            m_new = jnp.maximum(m_old, m_pair)
            alpha, beta = jnp.exp(m_old - m_new), jnp.exp(m_pair - m_new)
            o_ref[rq, :] = o_ref[rq, :] * alpha + o_pair * beta
            z_scr[rq, :] = z_scr[rq, :] * alpha + l_pair * beta
            m_scr[rq, :] = m_new
            return carry

        def blocks(heads):
            for dil in dils:
                lax.fori_loop(0, SEQ // BLK, lambda n, carry, dil=dil: step(n, carry, dil, heads), 0, unroll=4)

        if gqa:
            for grp in range(2):
                pl.when(pl.program_id(0) // 2 == grp)(lambda grp=grp: blocks(_HeadStack(grp)))
        else:
            blocks(_HeadStack(None))

        def fin(t, carry):
            rt = pl.ds(pl.multiple_of(t * TM, TM), TM)
            z = z_scr[rt, :]
            o_ref[rt, :] = o_ref[rt, :] / z
            lse_ref[rt, :] = m_scr[rt, :] + jnp.log(z)
            return carry
        lax.fori_loop(0, SEQ // TM, fin, 0)

    col = lambda blk: pl.BlockSpec((SEQ, LANES), lambda p, blk=blk: (0, blk + p))
    kv = (lambda blk: pl.BlockSpec((SEQ, LANES), lambda p, blk=blk: (0, blk))) if gqa else col
    in_specs = [col(qblk), kv(kblk), kv(vblk)]
    args = [proj, proj, proj]
    if has_sink:
        in_specs.append(pl.BlockSpec((1, LANES), lambda p: (0, p)))
        args.append(sink_x)
    out = pl.BlockSpec((SEQ, LANES), lambda p: (0, p))
    return pl.pallas_call(body, name=name, out_shape=(_sds((SEQ, 512)), _sds((SEQ, 512))), grid=(4,),
                          in_specs=in_specs, out_specs=(out, out),
                          scratch_shapes=[pltpu.VMEM((SEQ, LANES), F32), pltpu.VMEM((SEQ, LANES), F32), _BIAS],
                          compiler_params=_cp(48))(*args)


def _attn_bwd(proj, qblk, kblk, vblk, do, o, lse, dils, gqa, sink_x, name):
    has_sink = sink_x is not None

    def body(*refs):
        if has_sink:
            q_ref, k_ref, v_ref, do_ref, o_ref, lse_ref, s_ref, dq_ref, dk_ref, dv_ref, ds_ref, bias_scr = refs
        else:
            q_ref, k_ref, v_ref, do_ref, o_ref, lse_ref, dq_ref, dk_ref, dv_ref, bias_scr = refs
        pid = pl.program_id(0)

        @pl.when(pid == 0)
        def _():
            _fill_band_bias(bias_scr)
        dq_ref[...] = jnp.zeros_like(dq_ref)
        if gqa:
            @pl.when(pid == 0)
            def _():
                dk_ref[...] = jnp.zeros_like(dk_ref)
                dv_ref[...] = jnp.zeros_like(dv_ref)
        else:
            dk_ref[...] = jnp.zeros_like(dk_ref)
            dv_ref[...] = jnp.zeros_like(dv_ref)

        def step(n, carry, dil, heads):
            m0, m1 = heads.m0, heads.m1
            st, stp, first = _block_pos(n, dil)
            rq, rp = _rows(st, dil), _rows(stp, dil)
            do_, lse_ = do_ref[rq, :], lse_ref[rq, :]
            kk = jnp.concatenate([k_ref[rp, :], k_ref[rq, :]], axis=0)
            vv = jnp.concatenate([v_ref[rp, :], v_ref[rq, :]], axis=0)
            qs, dos = heads.stack(q_ref[rq, :]), heads.stack(do_)
            doo = do_ * o_ref[rq, :]
            delta = jnp.concatenate([jnp.sum(doo * m0, axis=1, keepdims=True), jnp.sum(doo * m1, axis=1, keepdims=True)], axis=0)
            lse_s = jnp.concatenate([lse_[:, 0:1], lse_[:, HD:HD + 1]], axis=0)
            p = jnp.exp(_mm(qs, kk, NT) * (HD ** -0.5) + bias_scr[first] - lse_s)
            dsr = p * (_mm(dos, vv, NT) - delta) * (HD ** -0.5)
            dq_ref[rq, :] += heads.unstack(_mm(dsr, kk))
            dk_sum, dv_sum = _mm(dsr, qs, TN), _mm(p, dos, TN)
            dk_ref[rp, :] += dk_sum[:BLK]
            dk_ref[rq, :] += dk_sum[BLK:]
            dv_ref[rp, :] += dv_sum[:BLK]
            dv_ref[rq, :] += dv_sum[BLK:]
            return carry

        def blocks(heads):
            for dil in dils:
                lax.fori_loop(0, SEQ // BLK, lambda n, carry, dil=dil: step(n, carry, dil, heads), 0, unroll=2)

        if gqa:
            for grp in range(2):
                pl.when(pid // 2 == grp)(lambda grp=grp: blocks(_HeadStack(grp)))
        else:
            blocks(_HeadStack(None))

        if has_sink:
            m0, m1 = _half_masks()

            def sink_rows(t, acc):
                rt = pl.ds(pl.multiple_of(t * TM, TM), TM)
                return acc - jnp.sum(jnp.exp(s_ref[...] - lse_ref[rt, :]) * (do_ref[rt, :] * o_ref[rt, :]), axis=0, keepdims=True)
            acc = lax.fori_loop(0, SEQ // TM, sink_rows, jnp.zeros((1, LANES), F32))
            per_head = jnp.sum(acc * m0, axis=1, keepdims=True) * m0 + jnp.sum(acc * m1, axis=1, keepdims=True) * m1
            ds_ref[0] = jnp.broadcast_to(per_head, (8, LANES))

    col = lambda blk: pl.BlockSpec((SEQ, LANES), lambda p, blk=blk: (0, blk + p))
    kv = (lambda blk: pl.BlockSpec((SEQ, LANES), lambda p, blk=blk: (0, blk))) if gqa else col
    pair = pl.BlockSpec((SEQ, LANES), lambda p: (0, p))
    in_specs = [col(qblk), kv(kblk), kv(vblk), pair, pair, pair]
    args = [proj, proj, proj, do, o, lse]
    kvw = LANES if gqa else 512
    kv_out = pl.BlockSpec((SEQ, LANES), lambda p: (0, 0)) if gqa else pair
    out_shape = [_sds((SEQ, 512)), _sds((SEQ, kvw)), _sds((SEQ, kvw))]
    out_specs = [pair, kv_out, kv_out]
    if has_sink:
        in_specs.append(pl.BlockSpec((1, LANES), lambda p: (0, p)))
        args.append(sink_x)
        out_shape.append(_sds((4, 8, LANES)))
        out_specs.append(pl.BlockSpec((1, 8, LANES), lambda p: (p, 0, 0)))
    return pl.pallas_call(body, name=name, out_shape=tuple(out_shape), grid=(4,), in_specs=in_specs,
                          out_specs=tuple(out_specs), scratch_shapes=[_BIAS], compiler_params=_cp(56))(*args)


def _shift_down(v, k):
    row = lax.broadcasted_iota(jnp.int32, v.shape, 0)
    return jnp.where(row >= k, pltpu.roll(v, k, axis=0), 0.0)


def _shift_up(v, k):
    n = v.shape[0]
    row = lax.broadcasted_iota(jnp.int32, v.shape, 0)
    return jnp.where(row < n - k, pltpu.roll(v, n - k, axis=0), 0.0)


def _conv_pre(x, w_ref, b_ref):
    u = b_ref[...] + x * w_ref[3:4, :]
    for k in range(1, 4):
        u = u + _shift_down(x, k) * w_ref[3 - k:4 - k, :]
    return u


def _conv_fwd(proj, w, b, name):
    def body(x_ref, w_ref, b_ref, o_ref):
        o_ref[...] = _silu(_conv_pre(x_ref[...], w_ref, b_ref))

    nblk = CONV_CH // LANES
    return pl.pallas_call(body, name=name, out_shape=_sds((SEQ, CONV_CH)), grid=(nblk,),
                          in_specs=[pl.BlockSpec((SEQ, LANES), lambda j: (0, XBC // LANES + j)),
                                    pl.BlockSpec((4, LANES), lambda j: (0, j)), pl.BlockSpec((1, LANES), lambda j: (0, j))],
                          out_specs=pl.BlockSpec((SEQ, LANES), lambda j: (0, j)), compiler_params=_cp())(proj, w, b)


def _conv_bwd(proj, dact, w, b, name):
    def body(x_ref, da_ref, w_ref, b_ref, dx_ref, dw_ref, db_ref):
        x = x_ref[...]
        du = da_ref[...] * _dsilu(_conv_pre(x, w_ref, b_ref))
        dx = du * w_ref[3:4, :]
        for k in range(1, 4):
            dx = dx + _shift_up(du, k) * w_ref[3 - k:4 - k, :]
        dx_ref[...] = dx
        db_ref[...] = jnp.sum(du, axis=0, keepdims=True)
        dw_ref[3:4, :] = jnp.sum(du * x, axis=0, keepdims=True)
        for k in range(1, 4):
            dw_ref[3 - k:4 - k, :] = jnp.sum(du * _shift_down(x, k), axis=0, keepdims=True)

    nblk = CONV_CH // LANES
    blk = pl.BlockSpec((SEQ, LANES), lambda j: (0, j))
    wspec, bspec = pl.BlockSpec((4, LANES), lambda j: (0, j)), pl.BlockSpec((1, LANES), lambda j: (0, j))
    return pl.pallas_call(body, name=name, out_shape=(_sds((SEQ, CONV_CH)), _sds((4, CONV_CH)), _sds((1, CONV_CH))), grid=(nblk,),
                          in_specs=[pl.BlockSpec((SEQ, LANES), lambda j: (0, XBC // LANES + j)), blk, wspec, bspec],
                          out_specs=(blk, wspec, bspec), compiler_params=_cp())(proj, dact, w, b)


def _ssd_chunk(xs, bm, cm, dtr, z, hs, al16, dtb, dskx, nw):
    m0, m1 = _half_masks()
    row = lax.broadcasted_iota(jnp.int32, (BLK, BLK), 0)
    col = lax.broadcasted_iota(jnp.int32, (BLK, BLK), 1)
    causal = row >= col
    tril = causal.astype(F32)
    lane = lax.broadcasted_iota(jnp.int32, (1, LANES), 1)
    sub = lax.broadcasted_iota(jnp.int32, (BLK, 1), 0)
    last_row = (sub == BLK - 1).astype(F32)
    dt = jnp.where(lane < 16, _softplus(dtr + dtb), 0.0)
    a16 = -jnp.exp(al16)
    acum = jnp.dot(tril, dt * a16, precision=HI, preferred_element_type=F32)
    acum_t = acum.T
    gmat = [_mm(cm[g], bm[g], NT) for g in range(2)]
    ys, hn = [], []
    for p in range(8):
        g = p // 4
        pick = [(lane == 2 * p + a).astype(F32) for a in range(2)]
        col_h = [jnp.sum(acum * pick[a], axis=1, keepdims=True) for a in range(2)]
        dt_x = sum(jnp.sum(dt * pick[a], axis=1, keepdims=True) * msk for a, msk in enumerate((m0, m1)))
        ac_x = col_h[0] * m0 + col_h[1] * m1
        a_end = jnp.sum(ac_x * last_row, axis=0, keepdims=True)
        xdt = xs[p] * dt_x
        y = _mm(cm[g], hs[p]) * jnp.exp(ac_x)
        for a, msk in enumerate((m0, m1)):
            row_h = jnp.sum(acum_t * (sub == 2 * p + a).astype(F32), axis=0, keepdims=True)
            decay = jnp.exp(jnp.where(causal, col_h[a] - row_h, NEG))
            y = y + _mm(gmat[g] * decay, xdt * msk)
        st = _mm(bm[g], xdt * jnp.exp(a_end - ac_x), TN)
        hn.append(hs[p] * jnp.exp(a_end) + st)
        y = y + dskx[p] * xs[p]
        ys.append(y * _silu(z[p]))
    out = []
    for g in range(2):
        ms = sum(jnp.sum(ys[p] * ys[p], axis=1, keepdims=True) for p in range(4 * g, 4 * g + 4)) * (1.0 / 512)
        rstd = lax.rsqrt(ms + EPS)
        out += [ys[p] * rstd * nw[p] for p in range(4 * g, 4 * g + 4)]
    return out, hn


def _tiles(ref, n, off=0):
    return [ref[:, off + LANES * p:off + LANES * (p + 1)] for p in range(n)]


def _ssd_load(xbc_ref, z_ref, dt_ref, al16_ref, dtb_ref, dsk_ref, nw_ref):
    return (_tiles(xbc_ref, 8), _tiles(xbc_ref, 2, 1024), _tiles(xbc_ref, 2, 1280), dt_ref[...], _tiles(z_ref, 8)), \
           (al16_ref[...], dtb_ref[...], _tiles(dsk_ref, 8), _tiles(nw_ref, 8))


_NCH = SEQ // BLK


def _ssd_param_specs():
    return [_full((1, LANES)), _full((1, LANES)), _full((1, 1024)), _full((1, 1024))]


def _ssd_fwd(xbc_act, proj, al16, dtb, dskx, nw, name):
    def body(xbc_ref, z_ref, dt_ref, al16_ref, dtb_ref, dsk_ref, nw_ref, y_ref, hin_ref, h_scr):
        @pl.when(pl.program_id(0) == 0)
        def _():
            h_scr[...] = jnp.zeros_like(h_scr)
        acts, params = _ssd_load(xbc_ref, z_ref, dt_ref, al16_ref, dtb_ref, dsk_ref, nw_ref)
        hs = _tiles(h_scr, 8)
        hin_ref[0] = h_scr[...]
        ys, hn = _ssd_chunk(*acts, hs, *params)
        for p in range(8):
            y_ref[:, LANES * p:LANES * (p + 1)] = ys[p]
            h_scr[:, LANES * p:LANES * (p + 1)] = hn[p]

    return pl.pallas_call(
        body, name=name, out_shape=(_sds((SEQ, 1024)), _sds((_NCH, BLK, 1024))), grid=(_NCH,),
        in_specs=[pl.BlockSpec((BLK, CONV_CH), lambda c: (c, 0)), pl.BlockSpec((BLK, 1024), lambda c: (c, ZB // 1024)),
                  pl.BlockSpec((BLK, LANES), lambda c: (c, DTC // LANES))] + _ssd_param_specs(),
        out_specs=(pl.BlockSpec((BLK, 1024), lambda c: (c, 0)), pl.BlockSpec((1, BLK, 1024), lambda c: (c, 0, 0))),
        scratch_shapes=[pltpu.VMEM((BLK, 1024), F32)], compiler_params=_cp())(xbc_act, proj, proj, al16, dtb, dskx, nw)


def _ssd_bwd(xbc_act, proj, hin, dyb, al16, dtb, dskx, nw, name):
    def body(xbc_ref, z_ref, dt_ref, hin_ref, dy_ref, al16_ref, dtb_ref, dsk_ref, nw_ref,
             dxbc_ref, dz_ref, ddt_ref, dal16_ref, ddtb_ref, ddsk_ref, dnw_ref, dh_scr):
        @pl.when(pl.program_id(0) == 0)
        def _():
            dh_scr[...] = jnp.zeros_like(dh_scr)
            for r in (dal16_ref, ddtb_ref, ddsk_ref, dnw_ref):
                r[...] = jnp.zeros_like(r)
        acts, params = _ssd_load(xbc_ref, z_ref, dt_ref, al16_ref, dtb_ref, dsk_ref, nw_ref)
        hs = [hin_ref[0, :, LANES * p:LANES * (p + 1)] for p in range(8)]
        _, vjp = jax.vjp(lambda a, h, q: _ssd_chunk(*a, h, *q), acts, hs, params)
        (dxs, dbm, dcm, ddt, dz), dhs, (dal16, ddtb, ddsk, dnw) = vjp((_tiles(dy_ref, 8), _tiles(dh_scr, 8)))
        for p in range(8):
            cols = slice(LANES * p, LANES * (p + 1))
            dxbc_ref[:, cols] = dxs[p]
            dz_ref[:, cols] = dz[p]
            dh_scr[:, cols] = dhs[p]
            ddsk_ref[:, cols] += ddsk[p]
            dnw_ref[:, cols] += dnw[p]
        for g in range(2):
            dxbc_ref[:, 1024 + LANES * g:1024 + LANES * (g + 1)] = dbm[g]
            dxbc_ref[:, 1280 + LANES * g:1280 + LANES * (g + 1)] = dcm[g]
        ddt_ref[...] = ddt
        dal16_ref[...] += dal16
        ddtb_ref[...] += ddtb

    rev = lambda c: _NCH - 1 - c
    return pl.pallas_call(
        body, name=name,
        out_shape=(_sds((SEQ, CONV_CH)), _sds((SEQ, 1024)), _sds((SEQ, LANES)),
                   _sds((1, LANES)), _sds((1, LANES)), _sds((1, 1024)), _sds((1, 1024))),
        grid=(_NCH,),
        in_specs=[pl.BlockSpec((BLK, CONV_CH), lambda c: (rev(c), 0)), pl.BlockSpec((BLK, 1024), lambda c: (rev(c), ZB // 1024)),
                  pl.BlockSpec((BLK, LANES), lambda c: (rev(c), DTC // LANES)), pl.BlockSpec((1, BLK, 1024), lambda c: (rev(c), 0, 0)),
                  pl.BlockSpec((BLK, 1024), lambda c: (rev(c), 0))] + _ssd_param_specs(),
        out_specs=(pl.BlockSpec((BLK, CONV_CH), lambda c: (rev(c), 0)), pl.BlockSpec((BLK, 1024), lambda c: (rev(c), 0)),
                   pl.BlockSpec((BLK, LANES), lambda c: (rev(c), 0)),
                   _full((1, LANES)), _full((1, LANES)), _full((1, 1024)), _full((1, 1024))),
        scratch_shapes=[pltpu.VMEM((BLK, 1024), F32)], compiler_params=_cp())(xbc_act, proj, proj, hin, dyb, al16, dtb, dskx, nw)


def _rstd(v):
    return lax.rsqrt(jnp.mean(v * v, axis=1, keepdims=True) + EPS)


def _rms_bwd(dn, n, rstd):
    return rstd * (dn - n * jnp.mean(dn * n, axis=1, keepdims=True))


_VEC = _full((1, D))


def _layer_spec(layer):
    return pl.BlockSpec((None, 2048, D), lambda *_: (layer, 0, 0))

_ROW = pl.BlockSpec((TM, D), lambda i, *_: (i, 0))


def _proj_fwd(x, pre_w, scale, shift, w, layer, name):
    tn = 1024

    def body(x_ref, pw_ref, sc_ref, sh_ref, w_ref, o_ref, h_ref):
        @pl.when(pl.program_id(1) == 0)
        def _():
            xv = x_ref[...]
            h = (xv * _rstd(xv) * pw_ref[...]) * (1.0 + sc_ref[...]) + sh_ref[...]
            h_ref[...] = h.astype(h_ref.dtype)
        o_ref[...] = jnp.dot(h_ref[...], w_ref[...].astype(MXU), preferred_element_type=F32)

    return pl.pallas_call(body, name=name, out_shape=(_sds((SEQ, NP)), _sds((SEQ, D), MXU)), grid=(SEQ // TM, NP // tn),
                          in_specs=[_ROW, _VEC, _VEC, _VEC, pl.BlockSpec((None, D, tn), lambda i, j: (layer, 0, j))],
                          out_specs=(pl.BlockSpec((TM, tn), lambda i, j: (i, j)), _ROW), compiler_params=_cp())(x, pre_w, scale, shift, w)


def _out_fwd(ya, yb, yc, w, layer, x, gate, post_w, name):
    def body(ya_ref, yb_ref, yc_ref, w_ref, x_ref, g_ref, pw_ref, xn_ref, y_ref):
        y = _mm(ya_ref[...], w_ref[0:512, :]) + _mm(yb_ref[...], w_ref[512:1536, :]) + _mm(yc_ref[...], w_ref[1536:2048, :])
        y_ref[...] = y
        xn_ref[...] = x_ref[...] + g_ref[...] * (y * _rstd(y) * pw_ref[...])

    half = pl.BlockSpec((TM, 512), lambda i: (i, 0))
    return pl.pallas_call(body, name=name, out_shape=(_sds((SEQ, D)), _sds((SEQ, D))), grid=(SEQ // TM,),
                          in_specs=[half, _ROW, half, _layer_spec(layer), _ROW, _VEC, _VEC],
                          out_specs=(_ROW, _ROW), compiler_params=_cp())(ya, yb, yc, w, x, gate, post_w)


def _post_bwd(dxo, y, gate, post_w, name):
    def body(dx_ref, y_ref, g_ref, pw_ref, dy_ref, dg_ref, dpw_ref):
        @pl.when(pl.program_id(0) == 0)
        def _():
            dg_ref[...] = jnp.zeros_like(dg_ref)
            dpw_ref[...] = jnp.zeros_like(dpw_ref)
        dx, y = dx_ref[...], y_ref[...]
        rstd = _rstd(y)
        n = y * rstd
        dg_ref[...] += jnp.sum(dx * (n * pw_ref[...]), axis=0, keepdims=True)
        dr = dx * g_ref[...]
        dpw_ref[...] += jnp.sum(dr * n, axis=0, keepdims=True)
        dy_ref[...] = _rms_bwd(dr * pw_ref[...], n, rstd)

    return pl.pallas_call(body, name=name, out_shape=(_sds((SEQ, D)), _sds((1, D)), _sds((1, D))), grid=(SEQ // TM,),
                          in_specs=[_ROW, _ROW, _VEC, _VEC], out_specs=(_ROW, _VEC, _VEC), compiler_params=_cp())(dxo, y, gate, post_w)


def _dymix(dy, w, layer, name):
    def body(dy_ref, w_ref, a_ref, b_ref, c_ref):
        dy = dy_ref[...]
        a_ref[...] = _mm(dy, w_ref[0:512, :], NT)
        b_ref[...] = _mm(dy, w_ref[512:1536, :], NT)
        c_ref[...] = _mm(dy, w_ref[1536:2048, :], NT)

    half = pl.BlockSpec((TM, 512), lambda i: (i, 0))
    return pl.pallas_call(body, name=name, out_shape=(_sds((SEQ, 512)), _sds((SEQ, D)), _sds((SEQ, 512))), grid=(SEQ // TM,),
                          in_specs=[_ROW, _layer_spec(layer)], out_specs=(half, _ROW, half), compiler_params=_cp())(dy, w)


def _dwout(ya, yb, yc, dy, name):
    def body(ya_ref, yb_ref, yc_ref, dy_ref, o_ref):
        @pl.when(pl.program_id(0) == 0)
        def _():
            o_ref[...] = jnp.zeros_like(o_ref)
        dy = dy_ref[...]
        o_ref[0:512, :] += _mm(ya_ref[...], dy, TN)
        o_ref[512:1536, :] += _mm(yb_ref[...], dy, TN)
        o_ref[1536:2048, :] += _mm(yc_ref[...], dy, TN)

    half = pl.BlockSpec((TM, 512), lambda i: (i, 0))
    return pl.pallas_call(body, name=name, out_shape=_sds((2048, D)), grid=(SEQ // TM,),
                          in_specs=[half, _ROW, half, _ROW], out_specs=_full((2048, D)), compiler_params=_cp())(ya, yb, yc, dy)


def _dwin(h, pieces, name):
    n = len(pieces)
    widths = [p.shape[1] for p in pieces]
    half = NP // 2

    def body(*refs):
        h_ref, p_refs, o_ref = refs[0], refs[1:1 + n], refs[1 + n]

        @pl.when(pl.program_id(0) == 0)
        def _():
            o_ref[...] = jnp.zeros_like(o_ref)
        hv, c0 = h_ref[...], 0
        for p_ref, wd in zip(p_refs, widths):
            o_ref[:, c0:c0 + wd] += _mm(hv, p_ref[...], TN)
            c0 += wd

    return pl.pallas_call(body, name=name, out_shape=_sds((D, half)), grid=(SEQ // TM,),
                          in_specs=[_ROW] + [pl.BlockSpec((TM, wd), lambda k: (k, 0)) for wd in widths],
                          out_specs=_full((D, half)), compiler_params=_cp(56))(h, *pieces)


_TMH = 256


def _dh_bwd(pieces, w, x, pre_w, scale, dxo, name):
    n = len(pieces)
    widths = [p.shape[1] for p in pieces]

    def body(*refs):
        p_refs, (w_ref, x_ref, pw_ref, sc_ref, dxo_ref, dx_ref, dsh_ref, dsc_ref, dpw_ref) = refs[:n], refs[n:]

        @pl.when(pl.program_id(0) == 0)
        def _():
            for r in (dsh_ref, dsc_ref, dpw_ref):
                r[...] = jnp.zeros_like(r)
        dh, c0 = 0.0, 0
        for p_ref, wd in zip(p_refs, widths):
            dh = dh + _mm(p_ref[...], w_ref[:, c0:c0 + wd], NT)
            c0 += wd
        xv = x_ref[...]
        rstd = _rstd(xv)
        nrm = xv * rstd
        dsh_ref[...] += jnp.sum(dh, axis=0, keepdims=True)
        dsc_ref[...] += jnp.sum(dh * (nrm * pw_ref[...]), axis=0, keepdims=True)
        dhn = dh * (1.0 + sc_ref[...])
        dpw_ref[...] += jnp.sum(dhn * nrm, axis=0, keepdims=True)
        dx_ref[...] = _rms_bwd(dhn * pw_ref[...], nrm, rstd) + dxo_ref[...]

    row = pl.BlockSpec((_TMH, D), lambda i: (i, 0))
    return pl.pallas_call(body, name=name, out_shape=(_sds((SEQ, D)), _sds((1, D)), _sds((1, D)), _sds((1, D))),
                          grid=(SEQ // _TMH,),
                          in_specs=[pl.BlockSpec((_TMH, wd), lambda i: (i, 0)) for wd in widths]
                          + [pl.BlockSpec((None, D, NP), lambda i: (0, 0, 0)), row, _VEC, _VEC, row],
                          out_specs=(row, _VEC, _VEC, _VEC), compiler_params=_cp(56))(*pieces, w, x, pre_w, scale, dxo)


def _w_in_padded(land, name):
    rows = 128

    def body(l_ref, o_ref):
        o_ref[...] = _pad_cols(jnp.concatenate([l_ref[k] for k in range(4)], axis=1))

    return pl.pallas_call(body, name=name, out_shape=_sds((D, NP), land.dtype), grid=(D // rows,),
                          in_specs=[pl.BlockSpec((4, rows, SHARD_IN), lambda i: (0, i, 0))],
                          out_specs=pl.BlockSpec((rows, NP), lambda i: (i, 0)), compiler_params=_cp())(land)


def _grad_blocks(dwa, dwb, name):
    rows = 128

    def body(a_ref, b_ref, o_ref):
        g = _unpad_cols(jnp.concatenate([a_ref[...], b_ref[...]], axis=1))
        for k in range(4):
            o_ref[k] = g[:, SHARD_IN * k:SHARD_IN * (k + 1)].astype(o_ref.dtype)

    half = pl.BlockSpec((rows, NP // 2), lambda i: (i, 0))
    return pl.pallas_call(body, name=name, out_shape=_sds((4, D, SHARD_IN), jnp.bfloat16), grid=(D // rows,),
                          in_specs=[half, half], out_specs=pl.BlockSpec((4, rows, SHARD_IN), lambda i: (0, i, 0)),
                          compiler_params=_cp())(dwa, dwb)


def _loss_bwd(xf, tgt, name):
    def body(x_ref, t_ref, dx_ref, l_ref):
        @pl.when(pl.program_id(0) == 0)
        def _():
            l_ref[...] = jnp.zeros_like(l_ref)
        e = x_ref[...] - t_ref[...]
        dx_ref[...] = e * (1.0 / D)
        l_ref[...] += 0.5 * jnp.sum(jnp.mean(e * e, axis=1, keepdims=True), axis=0, keepdims=True)

    return pl.pallas_call(body, name=name, out_shape=(_sds((SEQ, D)), _sds((8, LANES))), grid=(SEQ // TM,),
                          in_specs=[_ROW, _ROW], out_specs=(_ROW, _full((8, LANES))), compiler_params=_cp())(xf, tgt)


def _mod_part(c_all, ada_w, ada_b, name):
    def body(c_ref, w_ref, b_ref, o_ref):
        o_ref[0] = _mm(_silu(c_ref[...]), w_ref[0]) + b_ref[0]

    return pl.pallas_call(body, name=name, out_shape=_sds((DEPTH, 8, 768)), grid=(DEPTH,),
                          in_specs=[_full((8, D)), pl.BlockSpec((1, D, 768), lambda i: (i, 0, 0)), pl.BlockSpec((1, 1, 768), lambda i: (i, 0, 0))],
                          out_specs=pl.BlockSpec((1, 8, 768), lambda i: (i, 0, 0)), compiler_params=_cp())(c_all, ada_w, ada_b)


def _ada_grad(c_t, dmod, name):
    def body(c_ref, d_ref, o_ref):
        ca = _silu(c_ref[...])
        dm = d_ref[0]
        acc = ca[:, 0:1] * dm[0:1, :]
        for s in range(1, 8):
            acc = acc + ca[:, s:s + 1] * dm[s:s + 1, :]
        o_ref[0] = acc

    return pl.pallas_call(body, name=name, out_shape=_sds((DEPTH, D, 768)), grid=(DEPTH,),
                          in_specs=[_full((D, LANES)), pl.BlockSpec((1, 8, 768), lambda i: (i, 0, 0))],
                          out_specs=pl.BlockSpec((1, D, 768), lambda i: (i, 0, 0)), compiler_params=_cp())(c_t, dmod)


def _pack(parts):
    flat = []
    for p in parts:
        f = p.reshape(-1)
        flat.append(jnp.pad(f, (0, (-f.size) % LANES)))
    v = jnp.concatenate(flat)
    return jnp.pad(v, (0, (-v.size) % (8 * LANES))).reshape(-1, LANES)


def _unpack(v, shapes):
    v = v.reshape(-1)
    out, off = [], 0
    for s in shapes:
        n = math.prod(s)
        out.append(v[off:off + n].reshape(s))
        off += n + (-n) % LANES
    return out


_GIVEN_DT, _GIVEN_C = 4608, 4624


def _pad_cols(w):
    return jnp.concatenate([w[..., :_GIVEN_DT], w[..., _GIVEN_C:], w[..., _GIVEN_DT:_GIVEN_C],
                            jnp.zeros(w.shape[:-1] + (NP - IN_COLS,), w.dtype)], axis=-1)


def _unpad_cols(w):
    return jnp.concatenate([w[..., :_GIVEN_DT], w[..., DTC:DTC + 16], w[..., _GIVEN_DT:DTC]], axis=-1)


def _pad_lanes(v):
    return jnp.pad(v, (0, LANES - v.shape[0])).reshape(1, LANES)


def _local_step(x2, tgt, mod, weights_of, grads_done, pre_w, post_w, conv_w, conv_b, dt_bias, a_log, d_skip, nw, sinks):
    saved = []
    xcur = x2
    for i in range(DEPTH):
        shift, scale, gate = mod[i:i + 1, :D], mod[i:i + 1, D:2 * D], mod[i:i + 1, 2 * D:]
        pw, qw = pre_w[i:i + 1], post_w[i:i + 1]
        w_p, w_o = weights_of(i, xcur)
        proj, h = _proj_fwd(xcur, pw, scale, shift, w_p, 0, "proj_fwd")
        o_a, lse_a = _attn_fwd(proj, QA // LANES, KA // LANES, VA // LANES, DILS, False, None, "attn_a_fwd")
        ya = _gate_fwd(o_a, proj, ZA // 512, "gate_a_fwd")
        sink_x = jnp.repeat(sinks[i], HD).reshape(1, 512)
        o_c, lse_c = _attn_fwd(proj, QC // LANES, KC // LANES, VC // LANES, (1,), True, sink_x, "attn_c_fwd")
        yc = _gate_fwd(o_c, proj, ZC // 512, "gate_c_fwd")
        cw, cb = conv_w[i], conv_b[i:i + 1]
        xbc_act = _conv_fwd(proj, cw, cb, "conv_fwd")
        ssd_p = (_pad_lanes(a_log[i]), _pad_lanes(dt_bias[i]), jnp.repeat(d_skip[i], HD).reshape(1, 1024), nw[i:i + 1])
        yb, hin = _ssd_fwd(xbc_act, proj, *ssd_p, "ssd_fwd")
        xnew, y = _out_fwd(ya, yb, yc, w_o, 0, xcur, gate, qw, "out_fwd")
        saved.append((w_p, w_o, xcur, scale, gate, pw, qw, proj, h, o_a, lse_a, ya, sink_x, o_c, lse_c, yc, cw, cb, xbc_act, ssd_p, yb, hin, y))
        xcur = xnew
    dx, ltile = _loss_bwd(xcur, tgt, "loss")
    dmod, small = [None] * DEPTH, [None] * DEPTH
    for i in reversed(range(DEPTH)):
        w_p, w_o, xin, scale, gate, pw, qw, proj, h, o_a, lse_a, ya, sink_x, o_c, lse_c, yc, cw, cb, xbc_act, ssd_p, yb, hin, y = saved[i]
        dy, dgate, dpost = _post_bwd(dx, y, gate, qw, "post_bwd")
        dya, dyb, dyc = _dymix(dy, w_o, 0, "dymix")
        dwo = _dwout(ya, yb, yc, dy, "dwout")
        do_a, dz_a = _gate_bwd(dya, o_a, proj, ZA // 512, "gate_a_bwd")
        dq_a, dk_a, dv_a = _attn_bwd(proj, QA // LANES, KA // LANES, VA // LANES, do_a, o_a, lse_a, DILS, False, None, "attn_a_bwd")
        do_c, dz_c = _gate_bwd(dyc, o_c, proj, ZC // 512, "gate_c_bwd")
        dq_c, dk_c, dv_c, dsk = _attn_bwd(proj, QC // LANES, KC // LANES, VC // LANES, do_c, o_c, lse_c, (1,), True, sink_x, "attn_c_bwd")
        dxbc_act, dz_b, ddt, dal16, ddtb, ddsk, dnw = _ssd_bwd(xbc_act, proj, hin, dyb, *ssd_p, "ssd_bwd")
        dxbc, dcw, dcb = _conv_bwd(proj, dxbc_act, cw, cb, "conv_bwd")
        half_a, half_b = [dq_a, dk_a, dv_a, dz_a, dz_b], [dxbc, dq_c, dz_c, dk_c, dv_c, ddt]
        sent = grads_done(i, _dwin(h, half_a, "dwin_a"), _dwin(h, half_b, "dwin_b"), dwo)
        dx, dshift, dscale, dpre = _dh_bwd(half_a + half_b, w_p, xin, pw, scale + sent[0, 0], dx, "dh_bwd")
        dmod[i] = jnp.concatenate([dshift, dscale, dgate], axis=1)
        small[i] = (dpre, dpost, dcw, dcb, ddtb[0, :16], dal16[0, :16], ddsk.reshape(16, HD).sum(axis=1), dnw, dsk[:, 0, ::HD].reshape(8))
    return ltile, dx, jnp.concatenate(dmod, axis=0), small


_SMALL = ((1, D), (1, D), (4, CONV_CH), (1, CONV_CH), (16,), (16,), (16,), (1, D), (8,))


def kernel(x, c, ada_w, ada_b, pre_norm_w, post_norm_w, w_in, conv_w, conv_b, dt_bias, a_log, d_skip, ssm_norm_w, sinks, w_out, loss_target, m_ada_w, m_ada_b, m_pre_norm_w, m_post_norm_w, m_w_in, m_conv_w, m_conv_b, m_dt_bias, m_a_log, m_d_skip, m_ssm_norm_w, m_sinks, m_w_out, v_ada_w, v_ada_b, v_pre_norm_w, v_post_norm_w, v_w_in, v_conv_w, v_conv_b, v_dt_bias, v_a_log, v_d_skip, v_ssm_norm_w, v_sinks, v_w_out):
    xi, yi, ci = lax.axis_index("x"), lax.axis_index("y"), lax.axis_index("c")
    chip = 2 * xi + yi
    me = 2 * chip + ci

    w_in_b = _cast_bf16(w_in.reshape(DEPTH * D, SHARD_IN), 512, "cast_w_in").reshape(DEPTH, D, SHARD_IN)
    w_out_b = _cast_bf16(w_out.reshape(DEPTH * 512, D), 512, "cast_w_out").reshape(DEPTH, 512, D)
    gathers = []
    for i in range(DEPTH):
        lands = [lax.dynamic_update_slice(lax.empty((4,) + a.shape[1:], a.dtype), a[i][None], (chip, 0, 0)) for a in (w_in_b, w_out_b)]
        gathers.append(_split_start(None, lands, f"gather_start{i}"))
    all_started = gathers[0][3] + gathers[1][3] + gathers[2][3] + gathers[3][3]

    def weights_of(i, after):
        send_sems, recv_sems, thru, _ = gathers[i]
        if i == 0:
            after = all_started + mod[:1, :LANES]
        g_in, g_out = _split_wait(send_sems, recv_sems, thru, 2, after, f"gather_wait{i}")
        return _w_in_padded(g_in, "w_in_padded")[None], g_out.reshape(1, 2048, D)

    scatters = [None] * DEPTH

    def grads_done(i, dwa, dwb, dwo):
        blk_in = _grad_blocks(dwa, dwb, "grad_blocks")
        blk_out = _cast_bf16(dwo, 512, "cast_dw_out").reshape(4, 512, D)
        lands = [lax.empty(blk_in.shape, blk_in.dtype), lax.empty(blk_out.shape, blk_out.dtype)]
        scatters[i] = _split_start([blk_in, blk_out], lands, f"scatter_start{i}")
        return scatters[i][3]

    g0 = _allgather8(_pack([c, conv_w]), "gather_c")
    c_all = g0[:, :8, :].reshape(8, D)
    conv_w_full = jnp.concatenate([g0[2 * k, 8:56, :].reshape(DEPTH, 4, CONV_CH // 4) for k in range(4)], axis=-1)

    ada_b_mine = lax.dynamic_slice_in_dim(ada_b, 768 * chip, 768, axis=1).reshape(DEPTH, 1, 768)
    gm = _allgather8(_mod_part(c_all, ada_w, ada_b_mine, "mod_part").reshape(DEPTH * 8, 768), "gather_mod")
    gm = gm.reshape(4, 2, DEPTH, 8, 768)[:, 0]
    mod = lax.dynamic_index_in_dim(gm, me, axis=2, keepdims=False).transpose(1, 0, 2).reshape(DEPTH, 3 * D)

    ltile, dx, dmod, small = _local_step(x[0], loss_target[0], mod, weights_of, grads_done, pre_norm_w, post_norm_w, conv_w_full,
                                         conv_b, dt_bias, a_log, d_skip, ssm_norm_w, sinks)

    packed = _pack([dmod] + [g for layer in small for g in layer] + [ltile[0]])
    gs = _allgather8(packed, "gather_small")
    tot = _sum_blocks(gs, packed.shape[0], "sum_small")
    parts = _unpack(tot, [(DEPTH, 3 * D)] + list(_SMALL) * DEPTH + [(LANES,)])
    g_ada_b, loss = parts[0], parts[-1][0]
    per_layer = [parts[1 + len(_SMALL) * i:1 + len(_SMALL) * (i + 1)] for i in range(DEPTH)]
    g_pre, g_post, g_cw, g_cb, g_dtb, g_al, g_dsk, g_nw, g_sk = [jnp.stack([per_layer[i][j] for i in range(DEPTH)]) for j in range(len(_SMALL))]
    g_pre, g_post, g_cb, g_nw = g_pre[:, 0], g_post[:, 0], g_cb[:, 0], g_nw[:, 0]
    g_cw = lax.dynamic_slice_in_dim(g_cw, (CONV_CH // 4) * chip, CONV_CH // 4, axis=2)

    dmod_all = gs[:, :(DEPTH * 3 * D) // LANES, :].reshape(8, DEPTH, 3 * D).transpose(1, 0, 2)
    dmod_mine = lax.dynamic_slice_in_dim(dmod_all, 768 * chip, 768, axis=2)
    c_t = jnp.pad(c_all.T, ((0, 0), (0, LANES - 8)))
    g_ada_w = _ada_grad(c_t, dmod_mine, "ada_grad")

    r_in, r_out = [], []
    for i in range(DEPTH):
        send_sems, recv_sems, thru, _ = scatters[i]
        done = _split_wait(send_sems, recv_sems, thru, 2, dx, f"scatter_wait{i}")
        for r, land, src in zip((r_in, r_out), done[2:], done[:2]):
            own = lax.dynamic_index_in_dim(src, chip, axis=0, keepdims=True)
            r.append(lax.dynamic_update_slice(land, own, (chip, 0, 0)))
    r_in, r_out = jnp.stack(r_in, axis=1), jnp.stack(r_out, axis=1)
    p_in = _sum_blocks(r_in.reshape(4, DEPTH * D, SHARD_IN), 256, "sum_w_in")
    p_out = _sum_blocks(r_out.reshape(4, DEPTH * 512, D), 512, "sum_w_out")
    s_in, s_out = _sibling_swap([p_in, p_out], "swap_partials")

    res = {}
    res["ada_w"] = [a.reshape(DEPTH, D, 768) for a in
                    _adamw(ada_w.reshape(DEPTH * D, 768), [g_ada_w.reshape(DEPTH * D, 768)], m_ada_w.reshape(DEPTH * D, 768),
                           v_ada_w.reshape(DEPTH * D, 768), 512, "adamw_ada_w")]
    res["w_in"] = [a.reshape(DEPTH, D, SHARD_IN) for a in
                   _adamw(w_in.reshape(DEPTH * D, SHARD_IN), [p_in, s_in], m_w_in.reshape(DEPTH * D, SHARD_IN),
                          v_w_in.reshape(DEPTH * D, SHARD_IN), 256, "adamw_w_in")]
    res["w_out"] = [a.reshape(DEPTH, 512, D) for a in
                    _adamw(w_out.reshape(DEPTH * 512, D), [p_out, s_out], m_w_out.reshape(DEPTH * 512, D),
                           v_w_out.reshape(DEPTH * 512, D), 512, "adamw_w_out")]
    names = ["ada_b", "pre_norm_w", "post_norm_w", "conv_w", "conv_b", "dt_bias", "a_log", "d_skip", "ssm_norm_w", "sinks"]
    ws = [ada_b, pre_norm_w, post_norm_w, conv_w, conv_b, dt_bias, a_log, d_skip, ssm_norm_w, sinks]
    gsm = [g_ada_b, g_pre, g_post, g_cw, g_cb, g_dtb, g_al, g_dsk, g_nw, g_sk]
    ms = [m_ada_b, m_pre_norm_w, m_post_norm_w, m_conv_w, m_conv_b, m_dt_bias, m_a_log, m_d_skip, m_ssm_norm_w, m_sinks]
    vs = [v_ada_b, v_pre_norm_w, v_post_norm_w, v_conv_w, v_conv_b, v_dt_bias, v_a_log, v_d_skip, v_ssm_norm_w, v_sinks]
    pw_, pg_, pm_, pv_ = _pack(ws), _pack(gsm), _pack(ms), _pack(vs)
    small_out = _adamw(pw_, [pg_], pm_, pv_, pw_.shape[0], "adamw_small")
    shapes = [w.shape for w in ws]
    for kind in range(4):
        for nm, a in zip(names, _unpack(small_out[kind], shapes)):
            res.setdefault(nm, [None] * 4)[kind] = a
    order = ["ada_w", "ada_b", "pre_norm_w", "post_norm_w", "w_in", "conv_w", "conv_b", "dt_bias", "a_log", "d_skip", "ssm_norm_w", "sinks", "w_out"]
    return (loss, dx[None], *[res[n][0] for n in order], *[res[n][1] for n in order], *[res[n][2] for n in order], *[res[n][3] for n in order])
```

```python
import math

import jax
import jax.numpy as jnp
from jax import lax
from jax.experimental import pallas as pl
from jax.experimental.pallas import tpu as pltpu

F32 = jnp.float32
MXU = jnp.bfloat16
HI = lax.Precision.HIGHEST
MESH = pl.DeviceIdType.MESH

SEQ = 4096
D = 1024
DEPTH = 4
HD = 64
QK_SCALE = HD ** -0.5
LANES = 128
BLK = 128
DILS = (1, 4, 16)
NEG = -1e30
EPS = 1e-6
MIB = 1024 * 1024

NP = 6144
QA, KA, VA, ZA = 0, 512, 1024, 1536
ZB, XBC = 2048, 3072
QC, ZC, KC, VC = 4608, 5120, 5632, 5760
DTC = 5888
IN_COLS = 5904
SHARD_IN = IN_COLS // 4
CONV_CH = 1536
TM = 512

ADAM_LR, ADAM_B1, ADAM_B2, ADAM_EPS, ADAM_WD, ADAM_STEP = 0.001, 0.9, 0.999, 1e-08, 0.01, 10

NT = (((1,), (1,)), ((), ()))
TN = (((0,), (0,)), ((), ()))


def _cp(vmem_mib=48):
    return pltpu.CompilerParams(vmem_limit_bytes=vmem_mib * MIB)


def _sds(shape, dtype=F32):
    return jax.ShapeDtypeStruct(shape, dtype)


def _full(shape):
    n = len(shape)
    return pl.BlockSpec(shape, lambda *_: (0,) * n)


def _mm(a, b, dims=None):
    if dims is None:
        return jnp.dot(a.astype(MXU), b.astype(MXU), preferred_element_type=F32)
    return lax.dot_general(a.astype(MXU), b.astype(MXU), dims, preferred_element_type=F32)


def _sigmoid(x):
    return 1.0 / (1.0 + jnp.exp(-x))


def _silu(x):
    return x * _sigmoid(x)


def _dsilu(x):
    s = _sigmoid(x)
    return s * (1.0 + x * (1.0 - s))


def _softplus(x):
    ax = jnp.where(x >= 0, x, -x)
    return jnp.maximum(x, 0.0) + jnp.log1p(jnp.exp(-ax))


def _half_masks():
    lane = lax.broadcasted_iota(jnp.int32, (1, LANES), 1)
    m0 = (lane < HD).astype(F32)
    return m0, 1.0 - m0


def _allgather8(v, name):
    r, cc = v.shape

    def body(v_ref, out_ref, send_sems, recv_sems):
        x, y, c = lax.axis_index("x"), lax.axis_index("y"), lax.axis_index("c")
        me = 4 * x + 2 * y + c
        out_ref[me] = v_ref[...]
        peers = []
        for k in range(1, 8):
            px = 1 - x if k & 4 else x
            py = 1 - y if k & 2 else y
            pc = 1 - c if k & 1 else c
            peers.append((px, py, pc))
        sends = []
        for k, peer in enumerate(peers):
            cp = pltpu.make_async_remote_copy(src_ref=v_ref, dst_ref=out_ref.at[me], send_sem=send_sems.at[k],
                                              recv_sem=recv_sems.at[k], device_id=peer, device_id_type=MESH)
            cp.start()
            sends.append(cp)
        for k, (px, py, pc) in enumerate(peers):
            pltpu.make_async_remote_copy(src_ref=v_ref, dst_ref=out_ref.at[4 * px + 2 * py + pc], send_sem=send_sems.at[k],
                                         recv_sem=recv_sems.at[k], device_id=(px, py, pc), device_id_type=MESH).wait_recv()
        for cp in sends:
            cp.wait_send()

    return pl.pallas_call(
        body, name=name, out_shape=_sds((8, r, cc)),
        in_specs=[pl.BlockSpec(memory_space=pltpu.VMEM)], out_specs=pl.BlockSpec(memory_space=pltpu.VMEM),
        scratch_shapes=[pltpu.SemaphoreType.DMA((7,)), pltpu.SemaphoreType.DMA((7,))],
        compiler_params=_cp(32),
    )(v)


_HBM = pl.BlockSpec(memory_space=pltpu.HBM)
_SEM = pl.BlockSpec(memory_space=pltpu.SEMAPHORE)
_EFFECT = pltpu.SideEffectType.DATAFLOW_SIDE_EFFECTING


def _chip_copies(src_refs, land_refs, send_sems, recv_sems):
    x, y, c = lax.axis_index("x"), lax.axis_index("y"), lax.axis_index("c")
    mine = 2 * x + y
    out = []
    for i, land in enumerate(land_refs):
        for j, (px, py) in enumerate([(1 - x, y), (x, 1 - y), (1 - x, 1 - y)]):
            src = src_refs[i].at[2 * px + py] if src_refs else land.at[mine]
            mk = lambda dst, i=i, j=j, src=src, px=px, py=py: pltpu.make_async_remote_copy(
                src_ref=src, dst_ref=dst, send_sem=send_sems.at[3 * i + j], recv_sem=recv_sems.at[3 * i + j],
                device_id=(px, py, c), device_id_type=MESH)
            out.append((mk(land.at[mine]), mk(land.at[2 * px + py])))
    return out


def _split_start(srcs, lands, name):
    ops = list(srcs or []) + list(lands)
    ns, n = len(srcs or []), len(lands)

    def body(*refs):
        src_refs, land_refs = refs[:ns], refs[ns:ns + n]
        send_sems, recv_sems = refs[ns + n], refs[ns + n + 1]
        for mine_out, _ in _chip_copies(src_refs, land_refs, send_sems, recv_sems):
            mine_out.start()
        refs[-1][...] = jnp.zeros_like(refs[-1])

    sems = pltpu.SemaphoreType.DMA((3 * n,))
    res = pl.pallas_call(
        body, name=name, out_shape=(sems, sems) + tuple(pltpu.HBM(a.shape, a.dtype) for a in ops) + (_sds((8, LANES)),),
        in_specs=[_HBM] * len(ops), out_specs=(_SEM, _SEM) + (_HBM,) * len(ops) + (pl.BlockSpec(memory_space=pltpu.VMEM),),
        input_output_aliases={k: 2 + k for k in range(len(ops))},
        compiler_params=pltpu.CompilerParams(has_side_effects=_EFFECT),
    )(*[pltpu.with_memory_space_constraint(a, pltpu.HBM) for a in ops])
    return res[0], res[1], list(res[2:2 + len(ops)]), res[-1]


def _split_wait(send_sems, recv_sems, thru, n, after, name):
    ns = len(thru) - n

    def body(*refs):
        src_refs, land_refs = refs[:ns], refs[ns:ns + n]
        for mine_out, arriving in _chip_copies(src_refs, land_refs, refs[ns + n], refs[ns + n + 1]):
            mine_out.wait_send()
            arriving.wait_recv()

    res = pl.pallas_call(
        body, name=name, out_shape=tuple(pltpu.HBM(a.shape, a.dtype) for a in thru),
        in_specs=[_HBM] * len(thru) + [_SEM, _SEM, pl.BlockSpec(memory_space=pl.ANY)], out_specs=(_HBM,) * len(thru),
        input_output_aliases={k: k for k in range(len(thru))},
        compiler_params=pltpu.CompilerParams(has_side_effects=_EFFECT),
    )(*thru, send_sems, recv_sems, after)
    return list(res)


def _sibling_swap(arrs, name):
    n = len(arrs)

    def body(*refs):
        ins, outs_, (send_sems, recv_sems) = refs[:n], refs[n:2 * n], refs[2 * n:]
        sib = (lax.axis_index("x"), lax.axis_index("y"), 1 - lax.axis_index("c"))
        cps = [pltpu.make_async_remote_copy(src_ref=ins[i], dst_ref=outs_[i], send_sem=send_sems.at[i], recv_sem=recv_sems.at[i],
                                            device_id=sib, device_id_type=MESH) for i in range(n)]
        for cp in cps:
            cp.start()
        for cp in cps:
            cp.wait_recv()
        for cp in cps:
            cp.wait_send()

    hbm = pl.BlockSpec(memory_space=pltpu.HBM)
    return pl.pallas_call(
        body, name=name, out_shape=tuple(_sds(a.shape, a.dtype) for a in arrs), in_specs=[hbm] * n, out_specs=tuple([hbm] * n),
        scratch_shapes=[pltpu.SemaphoreType.DMA((n,)), pltpu.SemaphoreType.DMA((n,))],
    )(*arrs)


def _tile_spec(rows, cc):
    return pl.BlockSpec((None, rows, cc), lambda l, i: (l, i, 0))


def _cast_bf16(a, rows, name):
    nl, r, cc = a.shape

    def body(a_ref, o_ref):
        o_ref[...] = a_ref[...].astype(jnp.bfloat16)

    return pl.pallas_call(body, name=name, out_shape=_sds((nl, r, cc), jnp.bfloat16), grid=(nl, r // rows),
                          in_specs=[_tile_spec(rows, cc)], out_specs=_tile_spec(rows, cc), compiler_params=_cp())(a)


def _sum_blocks(a, rows, name):
    k, nl, r, cc = a.shape

    def body(a_ref, o_ref):
        acc = a_ref[0].astype(F32)
        for j in range(1, k):
            acc = acc + a_ref[j].astype(F32)
        o_ref[...] = acc

    return pl.pallas_call(body, name=name, out_shape=_sds((nl, r, cc)), grid=(nl, r // rows),
                          in_specs=[pl.BlockSpec((k, None, rows, cc), lambda l, i: (0, l, i, 0))],
                          out_specs=_tile_spec(rows, cc), compiler_params=_cp())(a)


def _adamw(w, parts, m, v, rows, name):
    nl, r, cc = w.shape
    np_ = len(parts)
    c1 = 1.0 / (1.0 - ADAM_B1 ** ADAM_STEP)
    c2 = 1.0 / (1.0 - ADAM_B2 ** ADAM_STEP)

    def body(*refs):
        w_ref, p_refs, (m_ref, v_ref, g_ref, d_ref, nm_ref, nv_ref) = refs[0], refs[1:1 + np_], refs[1 + np_:]
        g = p_refs[0][...]
        for p_ref in p_refs[1:]:
            g = g + p_ref[...]
        nm = ADAM_B1 * m_ref[...] + (1.0 - ADAM_B1) * g
        nv = ADAM_B2 * v_ref[...] + (1.0 - ADAM_B2) * (g * g)
        g_ref[...] = g
        nm_ref[...] = nm
        nv_ref[...] = nv
        d_ref[...] = -ADAM_LR * ((nm * c1) / (jnp.sqrt(nv * c2) + ADAM_EPS) + ADAM_WD * w_ref[...])

    spec = _tile_spec(rows, cc)
    return pl.pallas_call(body, name=name, out_shape=(_sds((nl, r, cc)),) * 4, grid=(nl, r // rows),
                          in_specs=[spec] * (3 + np_), out_specs=(spec,) * 4, compiler_params=_cp())(w, *parts, m, v)


def _gate_fwd(o, proj, zblk, name):
    def body(o_ref, z_ref, y_ref):
        y_ref[...] = o_ref[...] * _silu(z_ref[...])

    return pl.pallas_call(body, name=name, out_shape=_sds((SEQ, 512)), grid=(SEQ // TM,),
                          in_specs=[pl.BlockSpec((TM, 512), lambda i: (i, 0)), pl.BlockSpec((TM, 512), lambda i: (i, zblk))],
                          out_specs=pl.BlockSpec((TM, 512), lambda i: (i, 0)), compiler_params=_cp())(o, proj)


def _gate_bwd(dy, o, proj, zblk, name):
    def body(dy_ref, o_ref, z_ref, do_ref, dz_ref):
        dy, z = dy_ref[...], z_ref[...]
        do_ref[...] = dy * _silu(z)
        dz_ref[...] = dy * o_ref[...] * _dsilu(z)

    return pl.pallas_call(body, name=name, out_shape=(_sds((SEQ, 512)), _sds((SEQ, 512))), grid=(SEQ // TM,),
                          in_specs=[pl.BlockSpec((TM, 512), lambda i: (i, 0)), pl.BlockSpec((TM, 512), lambda i: (i, 0)),
                                    pl.BlockSpec((TM, 512), lambda i: (i, zblk))],
                          out_specs=(pl.BlockSpec((TM, 512), lambda i: (i, 0)),) * 2, compiler_params=_cp())(dy, o, proj)


_BIAS = pltpu.VMEM((2, 2 * BLK, 2 * BLK), F32)


def _fill_band_bias(bias_ref):
    qi = lax.broadcasted_iota(jnp.int32, (2 * BLK, 2 * BLK), 0) & (BLK - 1)
    kj = lax.broadcasted_iota(jnp.int32, (2 * BLK, 2 * BLK), 1)
    dist = BLK + qi - kj
    band = (dist >= 0) & (dist <= BLK)
    bias_ref[0] = jnp.where(band, 0.0, NEG)
    bias_ref[1] = jnp.where(band & (kj >= BLK), 0.0, NEG)


class _HeadStack:
    def __init__(self, group):
        self.m0, self.m1 = _half_masks()
        self.group = group
        if group is not None:
            self.kv_mask = (self.m0, self.m1)[group]

    def _swap_half(self, t, a):
        return t if a == self.group else pltpu.roll(t, HD, axis=1)

    def stack(self, t):
        t0, t1 = t * self.m0, t * self.m1
        if self.group is not None:
            t0, t1 = self._swap_half(t0, 0), self._swap_half(t1, 1)
        return jnp.concatenate([t0, t1], axis=0)

    def unstack(self, ts):
        if self.group is None:
            return ts[:BLK] * self.m0 + ts[BLK:] * self.m1
        return self._swap_half(ts[:BLK] * self.kv_mask, 0) + self._swap_half(ts[BLK:] * self.kv_mask, 1)


def _rows(st, dil):
    if dil == 1:
        return pl.ds(pl.multiple_of(st, BLK), BLK)
    return pl.ds(st, BLK, stride=dil)


def _block_pos(n, dil):
    nb = SEQ // (dil * BLK)
    r, b = n // nb, n % nb
    hp = (b > 0).astype(jnp.int32)
    st = r + dil * BLK * b
    return st, st - dil * BLK * hp, 1 - hp


def _attn_fwd(proj, qblk, kblk, vblk, dils, gqa, sink_x, name):
    has_sink = sink_x is not None

    def body(*refs):
        if has_sink:
            q_ref, k_ref, v_ref, s_ref, o_ref, lse_ref, m_scr, z_scr, bias_scr = refs
        else:
            q_ref, k_ref, v_ref, o_ref, lse_ref, m_scr, z_scr, bias_scr = refs

        @pl.when(pl.program_id(0) == 0)
        def _():
            _fill_band_bias(bias_scr)
        o_ref[...] = jnp.zeros_like(o_ref)
        if has_sink:
            z_scr[...] = jnp.ones_like(z_scr)
            m_scr[...] = jnp.broadcast_to(s_ref[...], m_scr.shape)
        else:
            z_scr[...] = jnp.zeros_like(z_scr)
            m_scr[...] = jnp.full_like(m_scr, NEG)

        def step(n, carry, dil, heads):
            m0, m1 = heads.m0, heads.m1
            st, stp, first = _block_pos(n, dil)
            rq, rp = _rows(st, dil), _rows(stp, dil)
            kk = jnp.concatenate([k_ref[rp, :], k_ref[rq, :]], axis=0)
            vv = jnp.concatenate([v_ref[rp, :], v_ref[rq, :]], axis=0)
            s = _mm(heads.stack(q_ref[rq, :] * QK_SCALE), kk, NT) + bias_scr[first]
            m = jnp.max(s, axis=1, keepdims=True)
            p = jnp.exp(s - m)
            l = jnp.sum(p, axis=1, keepdims=True)
            o_pair = heads.unstack(_mm(p, vv))
            m_pair = m[:BLK] * m0 + m[BLK:] * m1
            l_pair = l[:BLK] * m0 + l[BLK:] * m1
            m_old = m_scr[rq, :]
            m_new = jnp.maximum(m_old, m_pair)
            alpha, beta = jnp.exp(m_old - m_new), jnp.exp(m_pair - m_new)
            o_ref[rq, :] = o_ref[rq, :] * alpha + o_pair * beta
            z_scr[rq, :] = z_scr[rq, :] * alpha + l_pair * beta
            m_scr[rq, :] = m_new
            return carry

        def blocks(heads):
            for dil in dils:
                lax.fori_loop(0, SEQ // BLK, lambda n, carry, dil=dil: step(n, carry, dil, heads), 0, unroll=4)

        if gqa:
            for grp in range(2):
                pl.when(pl.program_id(0) // 2 == grp)(lambda grp=grp: blocks(_HeadStack(grp)))
        else:
            blocks(_HeadStack(None))

        def fin(t, carry):
            rt = pl.ds(pl.multiple_of(t * TM, TM), TM)
            z = z_scr[rt, :]
            o_ref[rt, :] = o_ref[rt, :] / z
            lse_ref[rt, :] = m_scr[rt, :] + jnp.log(z)
            return carry
        lax.fori_loop(0, SEQ // TM, fin, 0)

    col = lambda blk: pl.BlockSpec((SEQ, LANES), lambda p, blk=blk: (0, blk + p))
    kv = (lambda blk: pl.BlockSpec((SEQ, LANES), lambda p, blk=blk: (0, blk))) if gqa else col
    in_specs = [col(qblk), kv(kblk), kv(vblk)]
    args = [proj, proj, proj]
    if has_sink:
        in_specs.append(pl.BlockSpec((1, LANES), lambda p: (0, p)))
        args.append(sink_x)
    out = pl.BlockSpec((SEQ, LANES), lambda p: (0, p))
    return pl.pallas_call(body, name=name, out_shape=(_sds((SEQ, 512)), _sds((SEQ, 512))), grid=(4,),
                          in_specs=in_specs, out_specs=(out, out),
                          scratch_shapes=[pltpu.VMEM((SEQ, LANES), F32), pltpu.VMEM((SEQ, LANES), F32), _BIAS],
                          compiler_params=_cp(48))(*args)


def _attn_bwd(proj, qblk, kblk, vblk, do, o, lse, dils, gqa, sink_x, name):
    has_sink = sink_x is not None

    def body(*refs):
        if has_sink:
            q_ref, k_ref, v_ref, do_ref, o_ref, lse_ref, s_ref, dq_ref, dk_ref, dv_ref, ds_ref, bias_scr = refs
        else:
            q_ref, k_ref, v_ref, do_ref, o_ref, lse_ref, dq_ref, dk_ref, dv_ref, bias_scr = refs
        pid = pl.program_id(0)

        @pl.when(pid == 0)
        def _():
            _fill_band_bias(bias_scr)
        dq_ref[...] = jnp.zeros_like(dq_ref)
        if gqa:
            @pl.when(pid == 0)
            def _():
                dk_ref[...] = jnp.zeros_like(dk_ref)
                dv_ref[...] = jnp.zeros_like(dv_ref)
        else:
            dk_ref[...] = jnp.zeros_like(dk_ref)
            dv_ref[...] = jnp.zeros_like(dv_ref)

        def step(n, carry, dil, heads):
            m0, m1 = heads.m0, heads.m1
            st, stp, first = _block_pos(n, dil)
            rq, rp = _rows(st, dil), _rows(stp, dil)
            do_, lse_ = do_ref[rq, :], lse_ref[rq, :]
            kk = jnp.concatenate([k_ref[rp, :], k_ref[rq, :]], axis=0)
            vv = jnp.concatenate([v_ref[rp, :], v_ref[rq, :]], axis=0)
            qs, dos = heads.stack(q_ref[rq, :] * QK_SCALE), heads.stack(do_)
            doo = do_ * o_ref[rq, :]
            delta = jnp.concatenate([jnp.sum(doo * m0, axis=1, keepdims=True), jnp.sum(doo * m1, axis=1, keepdims=True)], axis=0)
            lse_s = jnp.concatenate([lse_[:, 0:1], lse_[:, HD:HD + 1]], axis=0)
            p = jnp.exp(_mm(qs, kk, NT) + bias_scr[first] - lse_s)
            ds = p * (_mm(dos, vv, NT) - delta)
            dq_ref[rq, :] += heads.unstack(_mm(ds, kk)) * QK_SCALE
            dk_sum, dv_sum = _mm(ds, qs, TN), _mm(p, dos, TN)
            dk_ref[rp, :] += dk_sum[:BLK]
            dk_ref[rq, :] += dk_sum[BLK:]
            dv_ref[rp, :] += dv_sum[:BLK]
            dv_ref[rq, :] += dv_sum[BLK:]
            return carry

        def blocks(heads):
            for dil in dils:
                lax.fori_loop(0, SEQ // BLK, lambda n, carry, dil=dil: step(n, carry, dil, heads), 0, unroll=2)

        if gqa:
            for grp in range(2):
                pl.when(pid // 2 == grp)(lambda grp=grp: blocks(_HeadStack(grp)))
        else:
            blocks(_HeadStack(None))

        if has_sink:
            m0, m1 = _half_masks()

            def sink_rows(t, acc):
                rt = pl.ds(pl.multiple_of(t * TM, TM), TM)
                return acc - jnp.sum(jnp.exp(s_ref[...] - lse_ref[rt, :]) * (do_ref[rt, :] * o_ref[rt, :]), axis=0, keepdims=True)
            acc = lax.fori_loop(0, SEQ // TM, sink_rows, jnp.zeros((1, LANES), F32))
            per_head = jnp.sum(acc * m0, axis=1, keepdims=True) * m0 + jnp.sum(acc * m1, axis=1, keepdims=True) * m1
            ds_ref[0] = jnp.broadcast_to(per_head, (8, LANES))

    col = lambda blk: pl.BlockSpec((SEQ, LANES), lambda p, blk=blk: (0, blk + p))
    kv = (lambda blk: pl.BlockSpec((SEQ, LANES), lambda p, blk=blk: (0, blk))) if gqa else col
    pair = pl.BlockSpec((SEQ, LANES), lambda p: (0, p))
    in_specs = [col(qblk), kv(kblk), kv(vblk), pair, pair, pair]
    args = [proj, proj, proj, do, o, lse]
    kvw = LANES if gqa else 512
    kv_out = pl.BlockSpec((SEQ, LANES), lambda p: (0, 0)) if gqa else pair
    out_shape = [_sds((SEQ, 512)), _sds((SEQ, kvw)), _sds((SEQ, kvw))]
    out_specs = [pair, kv_out, kv_out]
    if has_sink:
        in_specs.append(pl.BlockSpec((1, LANES), lambda p: (0, p)))
        args.append(sink_x)
        out_shape.append(_sds((4, 8, LANES)))
        out_specs.append(pl.BlockSpec((1, 8, LANES), lambda p: (p, 0, 0)))
    return pl.pallas_call(body, name=name, out_shape=tuple(out_shape), grid=(4,), in_specs=in_specs,
                          out_specs=tuple(out_specs), scratch_shapes=[_BIAS], compiler_params=_cp(56))(*args)


def _shift_down(v, k):
    row = lax.broadcasted_iota(jnp.int32, v.shape, 0)
    return jnp.where(row >= k, pltpu.roll(v, k, axis=0), 0.0)


def _shift_up(v, k):
    n = v.shape[0]
    row = lax.broadcasted_iota(jnp.int32, v.shape, 0)
    return jnp.where(row < n - k, pltpu.roll(v, n - k, axis=0), 0.0)


def _conv_pre(x, w_ref, b_ref):
    u = b_ref[...] + x * w_ref[3:4, :]
    for k in range(1, 4):
        u = u + _shift_down(x, k) * w_ref[3 - k:4 - k, :]
    return u


def _conv_fwd(proj, w, b, name):
    def body(x_ref, w_ref, b_ref, o_ref):
        o_ref[...] = _silu(_conv_pre(x_ref[...], w_ref, b_ref))

    nblk = CONV_CH // LANES
    return pl.pallas_call(body, name=name, out_shape=_sds((SEQ, CONV_CH)), grid=(nblk,),
                          in_specs=[pl.BlockSpec((SEQ, LANES), lambda j: (0, XBC // LANES + j)),
                                    pl.BlockSpec((4, LANES), lambda j: (0, j)), pl.BlockSpec((1, LANES), lambda j: (0, j))],
                          out_specs=pl.BlockSpec((SEQ, LANES), lambda j: (0, j)), compiler_params=_cp())(proj, w, b)


def _conv_bwd(proj, dact, w, b, name):
    def body(x_ref, da_ref, w_ref, b_ref, dx_ref, dw_ref, db_ref):
        x = x_ref[...]
        du = da_ref[...] * _dsilu(_conv_pre(x, w_ref, b_ref))
        dx = du * w_ref[3:4, :]
        for k in range(1, 4):
            dx = dx + _shift_up(du, k) * w_ref[3 - k:4 - k, :]
        dx_ref[...] = dx
        db_ref[...] = jnp.sum(du, axis=0, keepdims=True)
        dw_ref[3:4, :] = jnp.sum(du * x, axis=0, keepdims=True)
        for k in range(1, 4):
            dw_ref[3 - k:4 - k, :] = jnp.sum(du * _shift_down(x, k), axis=0, keepdims=True)

    nblk = CONV_CH // LANES
    blk = pl.BlockSpec((SEQ, LANES), lambda j: (0, j))
    wspec, bspec = pl.BlockSpec((4, LANES), lambda j: (0, j)), pl.BlockSpec((1, LANES), lambda j: (0, j))
    return pl.pallas_call(body, name=name, out_shape=(_sds((SEQ, CONV_CH)), _sds((4, CONV_CH)), _sds((1, CONV_CH))), grid=(nblk,),
                          in_specs=[pl.BlockSpec((SEQ, LANES), lambda j: (0, XBC // LANES + j)), blk, wspec, bspec],
                          out_specs=(blk, wspec, bspec), compiler_params=_cp())(proj, dact, w, b)


def _ssd_chunk(xs, bm, cm, dtr, z, hs, al16, dtb, dskx, nw):
    m0, m1 = _half_masks()
    row = lax.broadcasted_iota(jnp.int32, (BLK, BLK), 0)
    col = lax.broadcasted_iota(jnp.int32, (BLK, BLK), 1)
    causal = row >= col
    tril = causal.astype(F32)
    lane = lax.broadcasted_iota(jnp.int32, (1, LANES), 1)
    sub = lax.broadcasted_iota(jnp.int32, (BLK, 1), 0)
    last_row = (sub == BLK - 1).astype(F32)
    dt = jnp.where(lane < 16, _softplus(dtr + dtb), 0.0)
    a16 = -jnp.exp(al16)
    acum = jnp.dot(tril, dt * a16, precision=HI, preferred_element_type=F32)
    acum_t = acum.T
    gmat = [_mm(cm[g], bm[g], NT) for g in range(2)]
    ys, hn = [], []
    for p in range(8):
        g = p // 4
        pick = [(lane == 2 * p + a).astype(F32) for a in range(2)]
        col_h = [jnp.sum(acum * pick[a], axis=1, keepdims=True) for a in range(2)]
        dt_x = sum(jnp.sum(dt * pick[a], axis=1, keepdims=True) * msk for a, msk in enumerate((m0, m1)))
        ac_x = col_h[0] * m0 + col_h[1] * m1
        a_end = jnp.sum(ac_x * last_row, axis=0, keepdims=True)
        xdt = xs[p] * dt_x
        y = _mm(cm[g], hs[p]) * jnp.exp(ac_x)
        for a, msk in enumerate((m0, m1)):
            row_h = jnp.sum(acum_t * (sub == 2 * p + a).astype(F32), axis=0, keepdims=True)
            decay = jnp.exp(jnp.where(causal, col_h[a] - row_h, NEG))
            y = y + _mm(gmat[g] * decay, xdt * msk)
        st = _mm(bm[g], xdt * jnp.exp(a_end - ac_x), TN)
        hn.append(hs[p] * jnp.exp(a_end) + st)
        y = y + dskx[p] * xs[p]
        ys.append(y * _silu(z[p]))
    out = []
    for g in range(2):
        ms = sum(jnp.sum(ys[p] * ys[p], axis=1, keepdims=True) for p in range(4 * g, 4 * g + 4)) * (1.0 / 512)
        rstd = lax.rsqrt(ms + EPS)
        out += [ys[p] * rstd * nw[p] for p in range(4 * g, 4 * g + 4)]
    return out, hn


def _tiles(ref, n, off=0):
    return [ref[:, off + LANES * p:off + LANES * (p + 1)] for p in range(n)]


def _ssd_load(xbc_ref, z_ref, dt_ref, al16_ref, dtb_ref, dsk_ref, nw_ref):
    return (_tiles(xbc_ref, 8), _tiles(xbc_ref, 2, 1024), _tiles(xbc_ref, 2, 1280), dt_ref[...], _tiles(z_ref, 8)), \
           (al16_ref[...], dtb_ref[...], _tiles(dsk_ref, 8), _tiles(nw_ref, 8))


_NCH = SEQ // BLK


def _ssd_param_specs():
    return [_full((1, LANES)), _full((1, LANES)), _full((1, 1024)), _full((1, 1024))]


def _ssd_fwd(xbc_act, proj, al16, dtb, dskx, nw, name):
    def body(xbc_ref, z_ref, dt_ref, al16_ref, dtb_ref, dsk_ref, nw_ref, y_ref, hin_ref, h_scr):
        @pl.when(pl.program_id(0) == 0)
        def _():
            h_scr[...] = jnp.zeros_like(h_scr)
        acts, params = _ssd_load(xbc_ref, z_ref, dt_ref, al16_ref, dtb_ref, dsk_ref, nw_ref)
        hs = _tiles(h_scr, 8)
        hin_ref[0] = h_scr[...]
        ys, hn = _ssd_chunk(*acts, hs, *params)
        for p in range(8):
            y_ref[:, LANES * p:LANES * (p + 1)] = ys[p]
            h_scr[:, LANES * p:LANES * (p + 1)] = hn[p]

    return pl.pallas_call(
        body, name=name, out_shape=(_sds((SEQ, 1024)), _sds((_NCH, BLK, 1024))), grid=(_NCH,),
        in_specs=[pl.BlockSpec((BLK, CONV_CH), lambda c: (c, 0)), pl.BlockSpec((BLK, 1024), lambda c: (c, ZB // 1024)),
                  pl.BlockSpec((BLK, LANES), lambda c: (c, DTC // LANES))] + _ssd_param_specs(),
        out_specs=(pl.BlockSpec((BLK, 1024), lambda c: (c, 0)), pl.BlockSpec((1, BLK, 1024), lambda c: (c, 0, 0))),
        scratch_shapes=[pltpu.VMEM((BLK, 1024), F32)], compiler_params=_cp())(xbc_act, proj, proj, al16, dtb, dskx, nw)


def _ssd_bwd(xbc_act, proj, hin, dyb, al16, dtb, dskx, nw, name):
    def body(xbc_ref, z_ref, dt_ref, hin_ref, dy_ref, al16_ref, dtb_ref, dsk_ref, nw_ref,
             dxbc_ref, dz_ref, ddt_ref, dal16_ref, ddtb_ref, ddsk_ref, dnw_ref, dh_scr):
        @pl.when(pl.program_id(0) == 0)
        def _():
            dh_scr[...] = jnp.zeros_like(dh_scr)
            for r in (dal16_ref, ddtb_ref, ddsk_ref, dnw_ref):
                r[...] = jnp.zeros_like(r)
        acts, params = _ssd_load(xbc_ref, z_ref, dt_ref, al16_ref, dtb_ref, dsk_ref, nw_ref)
        hs = [hin_ref[0, :, LANES * p:LANES * (p + 1)] for p in range(8)]
        _, vjp = jax.vjp(lambda a, h, q: _ssd_chunk(*a, h, *q), acts, hs, params)
        (dxs, dbm, dcm, ddt, dz), dhs, (dal16, ddtb, ddsk, dnw) = vjp((_tiles(dy_ref, 8), _tiles(dh_scr, 8)))
        for p in range(8):
            cols = slice(LANES * p, LANES * (p + 1))
            dxbc_ref[:, cols] = dxs[p]
            dz_ref[:, cols] = dz[p]
            dh_scr[:, cols] = dhs[p]
            ddsk_ref[:, cols] += ddsk[p]
            dnw_ref[:, cols] += dnw[p]
        for g in range(2):
            dxbc_ref[:, 1024 + LANES * g:1024 + LANES * (g + 1)] = dbm[g]
            dxbc_ref[:, 1280 + LANES * g:1280 + LANES * (g + 1)] = dcm[g]
        ddt_ref[...] = ddt
        dal16_ref[...] += dal16
        ddtb_ref[...] += ddtb

    rev = lambda c: _NCH - 1 - c
    return pl.pallas_call(
        body, name=name,
        out_shape=(_sds((SEQ, CONV_CH)), _sds((SEQ, 1024)), _sds((SEQ, LANES)),
                   _sds((1, LANES)), _sds((1, LANES)), _sds((1, 1024)), _sds((1, 1024))),
        grid=(_NCH,),
        in_specs=[pl.BlockSpec((BLK, CONV_CH), lambda c: (rev(c), 0)), pl.BlockSpec((BLK, 1024), lambda c: (rev(c), ZB // 1024)),
                  pl.BlockSpec((BLK, LANES), lambda c: (rev(c), DTC // LANES)), pl.BlockSpec((1, BLK, 1024), lambda c: (rev(c), 0, 0)),
                  pl.BlockSpec((BLK, 1024), lambda c: (rev(c), 0))] + _ssd_param_specs(),
        out_specs=(pl.BlockSpec((BLK, CONV_CH), lambda c: (rev(c), 0)), pl.BlockSpec((BLK, 1024), lambda c: (rev(c), 0)),
                   pl.BlockSpec((BLK, LANES), lambda c: (rev(c), 0)),
                   _full((1, LANES)), _full((1, LANES)), _full((1, 1024)), _full((1, 1024))),
        scratch_shapes=[pltpu.VMEM((BLK, 1024), F32)], compiler_params=_cp())(xbc_act, proj, proj, hin, dyb, al16, dtb, dskx, nw)


def _rstd(v):
    return lax.rsqrt(jnp.mean(v * v, axis=1, keepdims=True) + EPS)


def _rms_bwd(dn, n, rstd):
    return rstd * (dn - n * jnp.mean(dn * n, axis=1, keepdims=True))


_VEC = _full((1, D))


def _layer_spec(layer):
    return pl.BlockSpec((None, 2048, D), lambda *_: (layer, 0, 0))

_ROW = pl.BlockSpec((TM, D), lambda i, *_: (i, 0))


def _proj_fwd(x, pre_w, scale, shift, w, layer, name):
    tn, ni = 1024, SEQ // TM

    def body(x_ref, pw_ref, sc_ref, sh_ref, w_ref, o_ref, h_ref, h_scr):
        rows = pl.ds(pl.multiple_of(pl.program_id(1) * TM, TM), TM)

        @pl.when(pl.program_id(0) == 0)
        def _():
            xv = x_ref[...]
            h = ((xv * _rstd(xv) * pw_ref[...]) * (1.0 + sc_ref[...]) + sh_ref[...]).astype(h_ref.dtype)
            h_scr[rows, :] = h
            h_ref[...] = h
        o_ref[...] = jnp.dot(h_scr[rows, :], w_ref[...].astype(MXU), preferred_element_type=F32)

    first_pass = pl.BlockSpec((TM, D), lambda j, i: (jnp.where(j == 0, i, ni - 1), 0))
    return pl.pallas_call(body, name=name, out_shape=(_sds((SEQ, NP)), _sds((SEQ, D), MXU)), grid=(NP // tn, ni),
                          in_specs=[first_pass, _VEC, _VEC, _VEC, pl.BlockSpec((None, D, tn), lambda j, i: (layer, 0, j))],
                          out_specs=(pl.BlockSpec((TM, tn), lambda j, i: (i, j)), first_pass),
                          scratch_shapes=[pltpu.VMEM((SEQ, D), MXU)], compiler_params=_cp())(x, pre_w, scale, shift, w)


def _out_fwd(ya, yb, yc, w, layer, x, gate, post_w, name):
    def body(ya_ref, yb_ref, yc_ref, w_ref, x_ref, g_ref, pw_ref, xn_ref, y_ref):
        y = _mm(ya_ref[...], w_ref[0:512, :]) + _mm(yb_ref[...], w_ref[512:1536, :]) + _mm(yc_ref[...], w_ref[1536:2048, :])
        y_ref[...] = y
        xn_ref[...] = x_ref[...] + g_ref[...] * (y * _rstd(y) * pw_ref[...])

    half = pl.BlockSpec((TM, 512), lambda i: (i, 0))
    return pl.pallas_call(body, name=name, out_shape=(_sds((SEQ, D)), _sds((SEQ, D))), grid=(SEQ // TM,),
                          in_specs=[half, _ROW, half, _layer_spec(layer), _ROW, _VEC, _VEC],
                          out_specs=(_ROW, _ROW), compiler_params=_cp())(ya, yb, yc, w, x, gate, post_w)


def _post_bwd(dxo, y, gate, post_w, name):
    def body(dx_ref, y_ref, g_ref, pw_ref, dy_ref, dg_ref, dpw_ref):
        @pl.when(pl.program_id(0) == 0)
        def _():
            dg_ref[...] = jnp.zeros_like(dg_ref)
            dpw_ref[...] = jnp.zeros_like(dpw_ref)
        dx, y = dx_ref[...], y_ref[...]
        rstd = _rstd(y)
        n = y * rstd
        dg_ref[...] += jnp.sum(dx * (n * pw_ref[...]), axis=0, keepdims=True)
        dr = dx * g_ref[...]
        dpw_ref[...] += jnp.sum(dr * n, axis=0, keepdims=True)
        dy_ref[...] = _rms_bwd(dr * pw_ref[...], n, rstd)

    return pl.pallas_call(body, name=name, out_shape=(_sds((SEQ, D)), _sds((1, D)), _sds((1, D))), grid=(SEQ // TM,),
                          in_specs=[_ROW, _ROW, _VEC, _VEC], out_specs=(_ROW, _VEC, _VEC), compiler_params=_cp())(dxo, y, gate, post_w)


def _dymix(dy, w, layer, name):
    def body(dy_ref, w_ref, a_ref, b_ref, c_ref):
        dy = dy_ref[...]
        a_ref[...] = _mm(dy, w_ref[0:512, :], NT)
        b_ref[...] = _mm(dy, w_ref[512:1536, :], NT)
        c_ref[...] = _mm(dy, w_ref[1536:2048, :], NT)

    half = pl.BlockSpec((TM, 512), lambda i: (i, 0))
    return pl.pallas_call(body, name=name, out_shape=(_sds((SEQ, 512)), _sds((SEQ, D)), _sds((SEQ, 512))), grid=(SEQ // TM,),
                          in_specs=[_ROW, _layer_spec(layer)], out_specs=(half, _ROW, half), compiler_params=_cp())(dy, w)


def _dwout(ya, yb, yc, dy, name):
    def body(ya_ref, yb_ref, yc_ref, dy_ref, o_ref):
        @pl.when(pl.program_id(0) == 0)
        def _():
            o_ref[...] = jnp.zeros_like(o_ref)
        dy = dy_ref[...]
        o_ref[0:512, :] += _mm(ya_ref[...], dy, TN)
        o_ref[512:1536, :] += _mm(yb_ref[...], dy, TN)
        o_ref[1536:2048, :] += _mm(yc_ref[...], dy, TN)

    half = pl.BlockSpec((TM, 512), lambda i: (i, 0))
    return pl.pallas_call(body, name=name, out_shape=_sds((2048, D)), grid=(SEQ // TM,),
                          in_specs=[half, _ROW, half, _ROW], out_specs=_full((2048, D)), compiler_params=_cp())(ya, yb, yc, dy)


def _dwin(h, pieces, name):
    n = len(pieces)
    widths = [p.shape[1] for p in pieces]
    half = NP // 2

    def body(*refs):
        h_ref, p_refs, o_ref = refs[0], refs[1:1 + n], refs[1 + n]

        @pl.when(pl.program_id(0) == 0)
        def _():
            o_ref[...] = jnp.zeros_like(o_ref)
        hv, c0 = h_ref[...], 0
        for p_ref, wd in zip(p_refs, widths):
            o_ref[:, c0:c0 + wd] += _mm(hv, p_ref[...], TN)
            c0 += wd

    return pl.pallas_call(body, name=name, out_shape=_sds((D, half)), grid=(SEQ // TM,),
                          in_specs=[_ROW] + [pl.BlockSpec((TM, wd), lambda k: (k, 0)) for wd in widths],
                          out_specs=_full((D, half)), compiler_params=_cp(56))(h, *pieces)


_TMH = 256


def _dh_bwd(pieces, w, x, pre_w, scale, dxo, name):
    n = len(pieces)
    widths = [p.shape[1] for p in pieces]

    def body(*refs):
        p_refs, (w_ref, x_ref, pw_ref, sc_ref, dxo_ref, dx_ref, dsh_ref, dsc_ref, dpw_ref) = refs[:n], refs[n:]

        @pl.when(pl.program_id(0) == 0)
        def _():
            for r in (dsh_ref, dsc_ref, dpw_ref):
                r[...] = jnp.zeros_like(r)
        dh, c0 = 0.0, 0
        for p_ref, wd in zip(p_refs, widths):
            dh = dh + _mm(p_ref[...], w_ref[:, c0:c0 + wd], NT)
            c0 += wd
        xv = x_ref[...]
        rstd = _rstd(xv)
        nrm = xv * rstd
        dsh_ref[...] += jnp.sum(dh, axis=0, keepdims=True)
        dsc_ref[...] += jnp.sum(dh * (nrm * pw_ref[...]), axis=0, keepdims=True)
        dhn = dh * (1.0 + sc_ref[...])
        dpw_ref[...] += jnp.sum(dhn * nrm, axis=0, keepdims=True)
        dx_ref[...] = _rms_bwd(dhn * pw_ref[...], nrm, rstd) + dxo_ref[...]

    row = pl.BlockSpec((_TMH, D), lambda i: (i, 0))
    return pl.pallas_call(body, name=name, out_shape=(_sds((SEQ, D)), _sds((1, D)), _sds((1, D)), _sds((1, D))),
                          grid=(SEQ // _TMH,),
                          in_specs=[pl.BlockSpec((_TMH, wd), lambda i: (i, 0)) for wd in widths]
                          + [pl.BlockSpec((None, D, NP), lambda i: (0, 0, 0)), row, _VEC, _VEC, row],
                          out_specs=(row, _VEC, _VEC, _VEC), compiler_params=_cp(56))(*pieces, w, x, pre_w, scale, dxo)


def _w_in_padded(land, name):
    rows = 128

    def body(l_ref, o_ref):
        o_ref[...] = _pad_cols(jnp.concatenate([l_ref[k] for k in range(4)], axis=1))

    return pl.pallas_call(body, name=name, out_shape=_sds((D, NP), land.dtype), grid=(D // rows,),
                          in_specs=[pl.BlockSpec((4, rows, SHARD_IN), lambda i: (0, i, 0))],
                          out_specs=pl.BlockSpec((rows, NP), lambda i: (i, 0)), compiler_params=_cp())(land)


def _grad_blocks(dwa, dwb, name):
    rows = 128

    def body(a_ref, b_ref, o_ref):
        g = _unpad_cols(jnp.concatenate([a_ref[...], b_ref[...]], axis=1))
        for k in range(4):
            o_ref[k] = g[:, SHARD_IN * k:SHARD_IN * (k + 1)].astype(o_ref.dtype)

    half = pl.BlockSpec((rows, NP // 2), lambda i: (i, 0))
    return pl.pallas_call(body, name=name, out_shape=_sds((4, D, SHARD_IN), jnp.bfloat16), grid=(D // rows,),
                          in_specs=[half, half], out_specs=pl.BlockSpec((4, rows, SHARD_IN), lambda i: (0, i, 0)),
                          compiler_params=_cp())(dwa, dwb)


def _loss_bwd(xf, tgt, name):
    def body(x_ref, t_ref, dx_ref, l_ref):
        @pl.when(pl.program_id(0) == 0)
        def _():
            l_ref[...] = jnp.zeros_like(l_ref)
        e = x_ref[...] - t_ref[...]
        dx_ref[...] = e * (1.0 / D)
        l_ref[...] += 0.5 * jnp.sum(jnp.mean(e * e, axis=1, keepdims=True), axis=0, keepdims=True)

    return pl.pallas_call(body, name=name, out_shape=(_sds((SEQ, D)), _sds((8, LANES))), grid=(SEQ // TM,),
                          in_specs=[_ROW, _ROW], out_specs=(_ROW, _full((8, LANES))), compiler_params=_cp())(xf, tgt)


def _mod_part(c_all, ada_w, ada_b, name):
    def body(c_ref, w_ref, b_ref, o_ref):
        o_ref[0] = _mm(_silu(c_ref[...]), w_ref[0]) + b_ref[0]

    return pl.pallas_call(body, name=name, out_shape=_sds((DEPTH, 8, 768)), grid=(DEPTH,),
                          in_specs=[_full((8, D)), pl.BlockSpec((1, D, 768), lambda i: (i, 0, 0)), pl.BlockSpec((1, 1, 768), lambda i: (i, 0, 0))],
                          out_specs=pl.BlockSpec((1, 8, 768), lambda i: (i, 0, 0)), compiler_params=_cp())(c_all, ada_w, ada_b)


def _ada_grad(c_t, dmod, name):
    def body(c_ref, d_ref, o_ref):
        ca = _silu(c_ref[...])
        dm = d_ref[0]
        acc = ca[:, 0:1] * dm[0:1, :]
        for s in range(1, 8):
            acc = acc + ca[:, s:s + 1] * dm[s:s + 1, :]
        o_ref[0] = acc

    return pl.pallas_call(body, name=name, out_shape=_sds((DEPTH, D, 768)), grid=(DEPTH,),
                          in_specs=[_full((D, LANES)), pl.BlockSpec((1, 8, 768), lambda i: (i, 0, 0))],
                          out_specs=pl.BlockSpec((1, D, 768), lambda i: (i, 0, 0)), compiler_params=_cp())(c_t, dmod)


def _pack(parts):
    flat = []
    for p in parts:
        f = p.reshape(-1)
        flat.append(jnp.pad(f, (0, (-f.size) % LANES)))
    v = jnp.concatenate(flat)
    return jnp.pad(v, (0, (-v.size) % (8 * LANES))).reshape(-1, LANES)


def _unpack(v, shapes):
    v = v.reshape(-1)
    out, off = [], 0
    for s in shapes:
        n = math.prod(s)
        out.append(v[off:off + n].reshape(s))
        off += n + (-n) % LANES
    return out


_GIVEN_DT, _GIVEN_C = 4608, 4624


def _pad_cols(w):
    return jnp.concatenate([w[..., :_GIVEN_DT], w[..., _GIVEN_C:], w[..., _GIVEN_DT:_GIVEN_C],
                            jnp.zeros(w.shape[:-1] + (NP - IN_COLS,), w.dtype)], axis=-1)


def _unpad_cols(w):
    return jnp.concatenate([w[..., :_GIVEN_DT], w[..., DTC:DTC + 16], w[..., _GIVEN_DT:DTC]], axis=-1)


def _pad_lanes(v):
    return jnp.pad(v, (0, LANES - v.shape[0])).reshape(1, LANES)


def _local_step(x2, tgt, mod, weights_of, grads_done, pre_w, post_w, conv_w, conv_b, dt_bias, a_log, d_skip, nw, sinks):
    saved = []
    xcur = x2
    for i in range(DEPTH):
        shift, scale, gate = mod[i:i + 1, :D], mod[i:i + 1, D:2 * D], mod[i:i + 1, 2 * D:]
        pw, qw = pre_w[i:i + 1], post_w[i:i + 1]
        w_p, w_o = weights_of(i, xcur)
        proj, h = _proj_fwd(xcur, pw, scale, shift, w_p, 0, "proj_fwd")
        o_a, lse_a = _attn_fwd(proj, QA // LANES, KA // LANES, VA // LANES, DILS, False, None, "attn_a_fwd")
        ya = _gate_fwd(o_a, proj, ZA // 512, "gate_a_fwd")
        sink_x = jnp.repeat(sinks[i], HD).reshape(1, 512)
        o_c, lse_c = _attn_fwd(proj, QC // LANES, KC // LANES, VC // LANES, (1,), True, sink_x, "attn_c_fwd")
        yc = _gate_fwd(o_c, proj, ZC // 512, "gate_c_fwd")
        cw, cb = conv_w[i], conv_b[i:i + 1]
        xbc_act = _conv_fwd(proj, cw, cb, "conv_fwd")
        ssd_p = (_pad_lanes(a_log[i]), _pad_lanes(dt_bias[i]), jnp.repeat(d_skip[i], HD).reshape(1, 1024), nw[i:i + 1])
        yb, hin = _ssd_fwd(xbc_act, proj, *ssd_p, "ssd_fwd")
        xnew, y = _out_fwd(ya, yb, yc, w_o, 0, xcur, gate, qw, "out_fwd")
        saved.append((w_p, w_o, xcur, scale, gate, pw, qw, proj, h, o_a, lse_a, ya, sink_x, o_c, lse_c, yc, cw, cb, xbc_act, ssd_p, yb, hin, y))
        xcur = xnew
    dx, ltile = _loss_bwd(xcur, tgt, "loss")
    dmod, small = [None] * DEPTH, [None] * DEPTH
    for i in reversed(range(DEPTH)):
        w_p, w_o, xin, scale, gate, pw, qw, proj, h, o_a, lse_a, ya, sink_x, o_c, lse_c, yc, cw, cb, xbc_act, ssd_p, yb, hin, y = saved[i]
        dy, dgate, dpost = _post_bwd(dx, y, gate, qw, "post_bwd")
        dya, dyb, dyc = _dymix(dy, w_o, 0, "dymix")
        dwo = _dwout(ya, yb, yc, dy, "dwout")
        do_a, dz_a = _gate_bwd(dya, o_a, proj, ZA // 512, "gate_a_bwd")
        dq_a, dk_a, dv_a = _attn_bwd(proj, QA // LANES, KA // LANES, VA // LANES, do_a, o_a, lse_a, DILS, False, None, "attn_a_bwd")
        do_c, dz_c = _gate_bwd(dyc, o_c, proj, ZC // 512, "gate_c_bwd")
        dq_c, dk_c, dv_c, dsk = _attn_bwd(proj, QC // LANES, KC // LANES, VC // LANES, do_c, o_c, lse_c, (1,), True, sink_x, "attn_c_bwd")
        dxbc_act, dz_b, ddt, dal16, ddtb, ddsk, dnw = _ssd_bwd(xbc_act, proj, hin, dyb, *ssd_p, "ssd_bwd")
        dxbc, dcw, dcb = _conv_bwd(proj, dxbc_act, cw, cb, "conv_bwd")
        half_a, half_b = [dq_a, dk_a, dv_a, dz_a, dz_b], [dxbc, dq_c, dz_c, dk_c, dv_c, ddt]
        sent = grads_done(i, _dwin(h, half_a, "dwin_a"), _dwin(h, half_b, "dwin_b"), dwo)
        dx, dshift, dscale, dpre = _dh_bwd(half_a + half_b, w_p, xin, pw, scale + sent[0, 0], dx, "dh_bwd")
        dmod[i] = jnp.concatenate([dshift, dscale, dgate], axis=1)
        small[i] = (dpre, dpost, dcw, dcb, ddtb[0, :16], dal16[0, :16], ddsk.reshape(16, HD).sum(axis=1), dnw, dsk[:, 0, ::HD].reshape(8))
    return ltile, dx, jnp.concatenate(dmod, axis=0), small


_SMALL = ((1, D), (1, D), (4, CONV_CH), (1, CONV_CH), (16,), (16,), (16,), (1, D), (8,))


def kernel(x, c, ada_w, ada_b, pre_norm_w, post_norm_w, w_in, conv_w, conv_b, dt_bias, a_log, d_skip, ssm_norm_w, sinks, w_out, loss_target, m_ada_w, m_ada_b, m_pre_norm_w, m_post_norm_w, m_w_in, m_conv_w, m_conv_b, m_dt_bias, m_a_log, m_d_skip, m_ssm_norm_w, m_sinks, m_w_out, v_ada_w, v_ada_b, v_pre_norm_w, v_post_norm_w, v_w_in, v_conv_w, v_conv_b, v_dt_bias, v_a_log, v_d_skip, v_ssm_norm_w, v_sinks, v_w_out):
    xi, yi, ci = lax.axis_index("x"), lax.axis_index("y"), lax.axis_index("c")
    chip = 2 * xi + yi
    me = 2 * chip + ci

    w_in_b = _cast_bf16(w_in, 512, "cast_w_in")
    w_out_b = _cast_bf16(w_out, 512, "cast_w_out")
    gathers = []
    for i in range(DEPTH):
        lands = [lax.dynamic_update_slice(lax.empty((4,) + a.shape[1:], a.dtype), a[i][None], (chip, 0, 0)) for a in (w_in_b, w_out_b)]
        gathers.append(_split_start(None, lands, f"gather_start{i}"))
    all_started = gathers[0][3] + gathers[1][3] + gathers[2][3] + gathers[3][3]

    def weights_of(i, after):
        send_sems, recv_sems, thru, _ = gathers[i]
        if i == 0:
            after = all_started + mod[:1, :LANES]
        g_in, g_out = _split_wait(send_sems, recv_sems, thru, 2, after, f"gather_wait{i}")
        return _w_in_padded(g_in, "w_in_padded")[None], g_out.reshape(1, 2048, D)

    scatters = [None] * DEPTH

    def grads_done(i, dwa, dwb, dwo):
        blk_in = _grad_blocks(dwa, dwb, "grad_blocks")
        blk_out = _cast_bf16(dwo.reshape(4, 512, D), 512, "cast_dw_out")
        lands = [lax.empty(blk_in.shape, blk_in.dtype), lax.empty(blk_out.shape, blk_out.dtype)]
        scatters[i] = _split_start([blk_in, blk_out], lands, f"scatter_start{i}")
        return scatters[i][3]

    g0 = _allgather8(_pack([c, conv_w]), "gather_c")
    c_all = g0[:, :8, :].reshape(8, D)
    conv_w_full = jnp.concatenate([g0[2 * k, 8:56, :].reshape(DEPTH, 4, CONV_CH // 4) for k in range(4)], axis=-1)

    ada_b_mine = lax.dynamic_slice_in_dim(ada_b, 768 * chip, 768, axis=1).reshape(DEPTH, 1, 768)
    gm = _allgather8(_mod_part(c_all, ada_w, ada_b_mine, "mod_part").reshape(DEPTH * 8, 768), "gather_mod")
    gm = gm.reshape(4, 2, DEPTH, 8, 768)[:, 0]
    mod = lax.dynamic_index_in_dim(gm, me, axis=2, keepdims=False).transpose(1, 0, 2).reshape(DEPTH, 3 * D)

    ltile, dx, dmod, small = _local_step(x[0], loss_target[0], mod, weights_of, grads_done, pre_norm_w, post_norm_w, conv_w_full,
                                         conv_b, dt_bias, a_log, d_skip, ssm_norm_w, sinks)

    packed = _pack([dmod] + [g for layer in small for g in layer] + [ltile[0]])
    gs = _allgather8(packed, "gather_small")
    tot = _sum_blocks(gs[:, None], packed.shape[0], "sum_small")[0]
    parts = _unpack(tot, [(DEPTH, 3 * D)] + list(_SMALL) * DEPTH + [(LANES,)])
    g_ada_b, loss = parts[0], parts[-1][0]
    per_layer = [parts[1 + len(_SMALL) * i:1 + len(_SMALL) * (i + 1)] for i in range(DEPTH)]
    g_pre, g_post, g_cw, g_cb, g_dtb, g_al, g_dsk, g_nw, g_sk = [jnp.stack([per_layer[i][j] for i in range(DEPTH)]) for j in range(len(_SMALL))]
    g_pre, g_post, g_cb, g_nw = g_pre[:, 0], g_post[:, 0], g_cb[:, 0], g_nw[:, 0]
    g_cw = lax.dynamic_slice_in_dim(g_cw, (CONV_CH // 4) * chip, CONV_CH // 4, axis=2)

    dmod_all = gs[:, :(DEPTH * 3 * D) // LANES, :].reshape(8, DEPTH, 3 * D).transpose(1, 0, 2)
    dmod_mine = lax.dynamic_slice_in_dim(dmod_all, 768 * chip, 768, axis=2)
    c_t = jnp.pad(c_all.T, ((0, 0), (0, LANES - 8)))
    g_ada_w = _ada_grad(c_t, dmod_mine, "ada_grad")

    r_in, r_out = [], []
    for i in range(DEPTH):
        send_sems, recv_sems, thru, _ = scatters[i]
        done = _split_wait(send_sems, recv_sems, thru, 2, dx, f"scatter_wait{i}")
        for r, land, src in zip((r_in, r_out), done[2:], done[:2]):
            own = lax.dynamic_index_in_dim(src, chip, axis=0, keepdims=True)
            r.append(lax.dynamic_update_slice(land, own, (chip, 0, 0)))
    r_in, r_out = jnp.stack(r_in, axis=1), jnp.stack(r_out, axis=1)
    p_in = _sum_blocks(r_in, 256, "sum_w_in")
    p_out = _sum_blocks(r_out, 512, "sum_w_out")
    s_in, s_out = _sibling_swap([p_in, p_out], "swap_partials")

    res = {}
    res["ada_w"] = _adamw(ada_w, [g_ada_w], m_ada_w, v_ada_w, 512, "adamw_ada_w")
    res["w_in"] = _adamw(w_in, [p_in, s_in], m_w_in, v_w_in, 256, "adamw_w_in")
    res["w_out"] = _adamw(w_out, [p_out, s_out], m_w_out, v_w_out, 512, "adamw_w_out")
    names = ["ada_b", "pre_norm_w", "post_norm_w", "conv_w", "conv_b", "dt_bias", "a_log", "d_skip", "ssm_norm_w", "sinks"]
    ws = [ada_b, pre_norm_w, post_norm_w, conv_w, conv_b, dt_bias, a_log, d_skip, ssm_norm_w, sinks]
    gsm = [g_ada_b, g_pre, g_post, g_cw, g_cb, g_dtb, g_al, g_dsk, g_nw, g_sk]
    ms = [m_ada_b, m_pre_norm_w, m_post_norm_w, m_conv_w, m_conv_b, m_dt_bias, m_a_log, m_d_skip, m_ssm_norm_w, m_sinks]
    vs = [v_ada_b, v_pre_norm_w, v_post_norm_w, v_conv_w, v_conv_b, v_dt_bias, v_a_log, v_d_skip, v_ssm_norm_w, v_sinks]
    pw_, pg_, pm_, pv_ = _pack(ws), _pack(gsm), _pack(ms), _pack(vs)
    small_out = _adamw(pw_[None], [pg_[None]], pm_[None], pv_[None], pw_.shape[0], "adamw_small")
    shapes = [w.shape for w in ws]
    for kind in range(4):
        for nm, a in zip(names, _unpack(small_out[kind][0], shapes)):
            res.setdefault(nm, [None] * 4)[kind] = a
    order = ["ada_w", "ada_b", "pre_norm_w", "post_norm_w", "w_in", "conv_w", "conv_b", "dt_bias", "a_log", "d_skip", "ssm_norm_w", "sinks", "w_out"]
    return (loss, dx[None], *[res[n][0] for n in order], *[res[n][1] for n in order], *[res[n][2] for n in order], *[res[n][3] for n in order])
```

```python
import math

import jax
import jax.numpy as jnp
from jax import lax
from jax.experimental import pallas as pl
from jax.experimental.pallas import tpu as pltpu

F32 = jnp.float32
MXU = jnp.bfloat16
HI = lax.Precision.HIGHEST
MESH = pl.DeviceIdType.MESH

SEQ = 4096
D = 1024
DEPTH = 4
HD = 64
QK_SCALE = HD ** -0.5
LANES = 128
BLK = 128
DILS = (1, 4, 16)
NEG = -1e30
EPS = 1e-6
MIB = 1024 * 1024

NP = 6144
QA, KA, VA, ZA = 0, 512, 1024, 1536
ZB, XBC = 2048, 3072
QC, ZC, KC, VC = 4608, 5120, 5632, 5760
DTC = 5888
IN_COLS = 5904
SHARD_IN = IN_COLS // 4
CONV_CH = 1536
TM = 512

ADAM_LR, ADAM_B1, ADAM_B2, ADAM_EPS, ADAM_WD, ADAM_STEP = 0.001, 0.9, 0.999, 1e-08, 0.01, 10

NT = (((1,), (1,)), ((), ()))
TN = (((0,), (0,)), ((), ()))


def _cp(vmem_mib=48):
    return pltpu.CompilerParams(vmem_limit_bytes=vmem_mib * MIB)


def _sds(shape, dtype=F32):
    return jax.ShapeDtypeStruct(shape, dtype)


def _full(shape):
    n = len(shape)
    return pl.BlockSpec(shape, lambda *_: (0,) * n)


def _mm(a, b, dims=None):
    if dims is None:
        return jnp.dot(a.astype(MXU), b.astype(MXU), preferred_element_type=F32)
    return lax.dot_general(a.astype(MXU), b.astype(MXU), dims, preferred_element_type=F32)


def _sigmoid(x):
    return 1.0 / (1.0 + jnp.exp(-x))


def _silu(x):
    return x * _sigmoid(x)


def _dsilu(x):
    s = _sigmoid(x)
    return s * (1.0 + x * (1.0 - s))


def _softplus(x):
    ax = jnp.where(x >= 0, x, -x)
    return jnp.maximum(x, 0.0) + jnp.log1p(jnp.exp(-ax))


def _half_masks():
    lane = lax.broadcasted_iota(jnp.int32, (1, LANES), 1)
    m0 = (lane < HD).astype(F32)
    return m0, 1.0 - m0


def _allgather8(v, name):
    r, cc = v.shape

    def body(v_ref, out_ref, send_sems, recv_sems):
        x, y, c = lax.axis_index("x"), lax.axis_index("y"), lax.axis_index("c")
        me = 4 * x + 2 * y + c
        out_ref[me] = v_ref[...]
        peers = []
        for k in range(1, 8):
            px = 1 - x if k & 4 else x
            py = 1 - y if k & 2 else y
            pc = 1 - c if k & 1 else c
            peers.append((px, py, pc))
        sends = []
        for k, peer in enumerate(peers):
            cp = pltpu.make_async_remote_copy(src_ref=v_ref, dst_ref=out_ref.at[me], send_sem=send_sems.at[k],
                                              recv_sem=recv_sems.at[k], device_id=peer, device_id_type=MESH)
            cp.start()
            sends.append(cp)
        for k, (px, py, pc) in enumerate(peers):
            pltpu.make_async_remote_copy(src_ref=v_ref, dst_ref=out_ref.at[4 * px + 2 * py + pc], send_sem=send_sems.at[k],
                                         recv_sem=recv_sems.at[k], device_id=(px, py, pc), device_id_type=MESH).wait_recv()
        for cp in sends:
            cp.wait_send()

    return pl.pallas_call(
        body, name=name, out_shape=_sds((8, r, cc)),
        in_specs=[pl.BlockSpec(memory_space=pltpu.VMEM)], out_specs=pl.BlockSpec(memory_space=pltpu.VMEM),
        scratch_shapes=[pltpu.SemaphoreType.DMA((7,)), pltpu.SemaphoreType.DMA((7,))],
        compiler_params=_cp(32),
    )(v)


_HBM = pl.BlockSpec(memory_space=pltpu.HBM)
_SEM = pl.BlockSpec(memory_space=pltpu.SEMAPHORE)
_EFFECT = pltpu.SideEffectType.DATAFLOW_SIDE_EFFECTING


def _chip_copies(src_refs, land_refs, send_sems, recv_sems):
    x, y, c = lax.axis_index("x"), lax.axis_index("y"), lax.axis_index("c")
    mine = 2 * x + y
    out = []
    for i, land in enumerate(land_refs):
        for j, (px, py) in enumerate([(1 - x, y), (x, 1 - y), (1 - x, 1 - y)]):
            src = src_refs[i].at[2 * px + py] if src_refs else land.at[mine]
            mk = lambda dst, i=i, j=j, src=src, px=px, py=py: pltpu.make_async_remote_copy(
                src_ref=src, dst_ref=dst, send_sem=send_sems.at[3 * i + j], recv_sem=recv_sems.at[3 * i + j],
                device_id=(px, py, c), device_id_type=MESH)
            out.append((mk(land.at[mine]), mk(land.at[2 * px + py])))
    return out


def _split_start(srcs, lands, name):
    ops = list(srcs or []) + list(lands)
    ns, n = len(srcs or []), len(lands)

    def body(*refs):
        src_refs, land_refs = refs[:ns], refs[ns:ns + n]
        send_sems, recv_sems = refs[ns + n], refs[ns + n + 1]
        for mine_out, _ in _chip_copies(src_refs, land_refs, send_sems, recv_sems):
            mine_out.start()
        refs[-1][...] = jnp.zeros_like(refs[-1])

    sems = pltpu.SemaphoreType.DMA((3 * n,))
    res = pl.pallas_call(
        body, name=name, out_shape=(sems, sems) + tuple(pltpu.HBM(a.shape, a.dtype) for a in ops) + (_sds((8, LANES)),),
        in_specs=[_HBM] * len(ops), out_specs=(_SEM, _SEM) + (_HBM,) * len(ops) + (pl.BlockSpec(memory_space=pltpu.VMEM),),
        input_output_aliases={k: 2 + k for k in range(len(ops))},
        compiler_params=pltpu.CompilerParams(has_side_effects=_EFFECT),
    )(*[pltpu.with_memory_space_constraint(a, pltpu.HBM) for a in ops])
    return res[0], res[1], list(res[2:2 + len(ops)]), res[-1]


def _split_wait(send_sems, recv_sems, thru, n, after, name):
    ns = len(thru) - n

    def body(*refs):
        src_refs, land_refs = refs[:ns], refs[ns:ns + n]
        for mine_out, arriving in _chip_copies(src_refs, land_refs, refs[ns + n], refs[ns + n + 1]):
            mine_out.wait_send()
            arriving.wait_recv()

    res = pl.pallas_call(
        body, name=name, out_shape=tuple(pltpu.HBM(a.shape, a.dtype) for a in thru),
        in_specs=[_HBM] * len(thru) + [_SEM, _SEM, pl.BlockSpec(memory_space=pl.ANY)], out_specs=(_HBM,) * len(thru),
        input_output_aliases={k: k for k in range(len(thru))},
        compiler_params=pltpu.CompilerParams(has_side_effects=_EFFECT),
    )(*thru, send_sems, recv_sems, after)
    return list(res)


def _sibling_swap(arrs, name):
    n = len(arrs)

    def body(*refs):
        ins, outs_, (send_sems, recv_sems) = refs[:n], refs[n:2 * n], refs[2 * n:]
        sib = (lax.axis_index("x"), lax.axis_index("y"), 1 - lax.axis_index("c"))
        cps = [pltpu.make_async_remote_copy(src_ref=ins[i], dst_ref=outs_[i], send_sem=send_sems.at[i], recv_sem=recv_sems.at[i],
                                            device_id=sib, device_id_type=MESH) for i in range(n)]
        for cp in cps:
            cp.start()
        for cp in cps:
            cp.wait_recv()
        for cp in cps:
            cp.wait_send()

    hbm = pl.BlockSpec(memory_space=pltpu.HBM)
    return pl.pallas_call(
        body, name=name, out_shape=tuple(_sds(a.shape, a.dtype) for a in arrs), in_specs=[hbm] * n, out_specs=tuple([hbm] * n),
        scratch_shapes=[pltpu.SemaphoreType.DMA((n,)), pltpu.SemaphoreType.DMA((n,))],
    )(*arrs)


def _tile_spec(rows, cc):
    return pl.BlockSpec((None, rows, cc), lambda l, i: (l, i, 0))


def _cast_bf16(a, rows, name):
    nl, r, cc = a.shape

    def body(a_ref, o_ref):
        o_ref[...] = a_ref[...].astype(jnp.bfloat16)

    return pl.pallas_call(body, name=name, out_shape=_sds((nl, r, cc), jnp.bfloat16), grid=(nl, r // rows),
                          in_specs=[_tile_spec(rows, cc)], out_specs=_tile_spec(rows, cc), compiler_params=_cp())(a)


def _sum_blocks(a, rows, name):
    k, nl, r, cc = a.shape

    def body(a_ref, o_ref):
        acc = a_ref[0].astype(F32)
        for j in range(1, k):
            acc = acc + a_ref[j].astype(F32)
        o_ref[...] = acc

    return pl.pallas_call(body, name=name, out_shape=_sds((nl, r, cc)), grid=(nl, r // rows),
                          in_specs=[pl.BlockSpec((k, None, rows, cc), lambda l, i: (0, l, i, 0))],
                          out_specs=_tile_spec(rows, cc), compiler_params=_cp())(a)


def _adamw(w, parts, m, v, rows, name):
    nl, r, cc = w.shape
    np_ = len(parts)
    c1 = 1.0 / (1.0 - ADAM_B1 ** ADAM_STEP)
    c2 = 1.0 / (1.0 - ADAM_B2 ** ADAM_STEP)

    def body(*refs):
        w_ref, p_refs, (m_ref, v_ref, g_ref, d_ref, nm_ref, nv_ref) = refs[0], refs[1:1 + np_], refs[1 + np_:]
        g = p_refs[0][...]
        for p_ref in p_refs[1:]:
            g = g + p_ref[...]
        nm = ADAM_B1 * m_ref[...] + (1.0 - ADAM_B1) * g
        nv = ADAM_B2 * v_ref[...] + (1.0 - ADAM_B2) * (g * g)
        g_ref[...] = g
        nm_ref[...] = nm
        nv_ref[...] = nv
        d_ref[...] = -ADAM_LR * ((nm * c1) / (jnp.sqrt(nv * c2) + ADAM_EPS) + ADAM_WD * w_ref[...])

    spec = _tile_spec(rows, cc)
    return pl.pallas_call(body, name=name, out_shape=(_sds((nl, r, cc)),) * 4, grid=(nl, r // rows),
                          in_specs=[spec] * (3 + np_), out_specs=(spec,) * 4, compiler_params=_cp())(w, *parts, m, v)


_BIAS = pltpu.VMEM((2, 2 * BLK, 2 * BLK), F32)


def _fill_band_bias(bias_ref):
    qi = lax.broadcasted_iota(jnp.int32, (2 * BLK, 2 * BLK), 0) & (BLK - 1)
    kj = lax.broadcasted_iota(jnp.int32, (2 * BLK, 2 * BLK), 1)
    dist = BLK + qi - kj
    band = (dist >= 0) & (dist <= BLK)
    bias_ref[0] = jnp.where(band, 0.0, NEG)
    bias_ref[1] = jnp.where(band & (kj >= BLK), 0.0, NEG)


class _HeadStack:
    def __init__(self, group):
        self.m0, self.m1 = _half_masks()
        self.group = group
        if group is not None:
            self.kv_mask = (self.m0, self.m1)[group]

    def _swap_half(self, t, a):
        return t if a == self.group else pltpu.roll(t, HD, axis=1)

    def stack(self, t):
        t0, t1 = t * self.m0, t * self.m1
        if self.group is not None:
            t0, t1 = self._swap_half(t0, 0), self._swap_half(t1, 1)
        return jnp.concatenate([t0, t1], axis=0)

    def unstack(self, ts):
        if self.group is None:
            return ts[:BLK] * self.m0 + ts[BLK:] * self.m1
        return self._swap_half(ts[:BLK] * self.kv_mask, 0) + self._swap_half(ts[BLK:] * self.kv_mask, 1)


def _rows(st, dil):
    if dil == 1:
        return pl.ds(pl.multiple_of(st, BLK), BLK)
    return pl.ds(st, BLK, stride=dil)


def _block_pos(n, dil):
    nb = SEQ // (dil * BLK)
    r, b = n // nb, n % nb
    hp = (b > 0).astype(jnp.int32)
    st = r + dil * BLK * b
    return st, st - dil * BLK * hp, 1 - hp


def _attn_fwd(proj, qblk, kblk, vblk, dils, gqa, sink_x, name):
    has_sink = sink_x is not None

    def body(*refs):
        if has_sink:
            q_ref, k_ref, v_ref, s_ref, o_ref, lse_ref, m_scr, z_scr, bias_scr = refs
        else:
            q_ref, k_ref, v_ref, o_ref, lse_ref, m_scr, z_scr, bias_scr = refs

        @pl.when(pl.program_id(0) == 0)
        def _():
            _fill_band_bias(bias_scr)
        o_ref[...] = jnp.zeros_like(o_ref)
        if has_sink:
            z_scr[...] = jnp.ones_like(z_scr)
            m_scr[...] = jnp.broadcast_to(s_ref[...], m_scr.shape)
        else:
            z_scr[...] = jnp.zeros_like(z_scr)
            m_scr[...] = jnp.full_like(m_scr, NEG)

        def step(n, carry, dil, heads):
            m0, m1 = heads.m0, heads.m1
            st, stp, first = _block_pos(n, dil)
            rq, rp = _rows(st, dil), _rows(stp, dil)
            kk = jnp.concatenate([k_ref[rp, :], k_ref[rq, :]], axis=0)
            vv = jnp.concatenate([v_ref[rp, :], v_ref[rq, :]], axis=0)
            s = _mm(heads.stack(q_ref[rq, :] * QK_SCALE), kk, NT) + bias_scr[first]
            m = jnp.max(s, axis=1, keepdims=True)
            p = jnp.exp(s - m)
            l = jnp.sum(p, axis=1, keepdims=True)
            o_pair = heads.unstack(_mm(p, vv))
            m_pair = m[:BLK] * m0 + m[BLK:] * m1
            l_pair = l[:BLK] * m0 + l[BLK:] * m1
            m_old = m_scr[rq, :]
            m_new = jnp.maximum(m_old, m_pair)
            alpha, beta = jnp.exp(m_old - m_new), jnp.exp(m_pair - m_new)
            o_ref[rq, :] = o_ref[rq, :] * alpha + o_pair * beta
            z_scr[rq, :] = z_scr[rq, :] * alpha + l_pair * beta
            m_scr[rq, :] = m_new
            return carry

        def blocks(heads):
            for dil in dils:
                lax.fori_loop(0, SEQ // BLK, lambda n, carry, dil=dil: step(n, carry, dil, heads), 0, unroll=4)

        if gqa:
            for grp in range(2):
                pl.when(pl.program_id(0) // 2 == grp)(lambda grp=grp: blocks(_HeadStack(grp)))
        else:
            blocks(_HeadStack(None))

        def fin(t, carry):
            rt = pl.ds(pl.multiple_of(t * TM, TM), TM)
            z = z_scr[rt, :]
            o_ref[rt, :] = o_ref[rt, :] / z
            lse_ref[rt, :] = m_scr[rt, :] + jnp.log(z)
            return carry
        lax.fori_loop(0, SEQ // TM, fin, 0)

    col = lambda blk: pl.BlockSpec((SEQ, LANES), lambda p, blk=blk: (0, blk + p))
    kv = (lambda blk: pl.BlockSpec((SEQ, LANES), lambda p, blk=blk: (0, blk))) if gqa else col
    in_specs = [col(qblk), kv(kblk), kv(vblk)]
    args = [proj, proj, proj]
    if has_sink:
        in_specs.append(pl.BlockSpec((1, LANES), lambda p: (0, p)))
        args.append(sink_x)
    out = pl.BlockSpec((SEQ, LANES), lambda p: (0, p))
    return pl.pallas_call(body, name=name, out_shape=(_sds((SEQ, 512)), _sds((SEQ, 512))), grid=(4,),
                          in_specs=in_specs, out_specs=(out, out),
                          scratch_shapes=[pltpu.VMEM((SEQ, LANES), F32), pltpu.VMEM((SEQ, LANES), F32), _BIAS],
                          compiler_params=_cp(48))(*args)


def _attn_bwd(proj, qblk, kblk, vblk, do, o, lse, dils, gqa, sink_x, name):
    has_sink = sink_x is not None

    def body(*refs):
        if has_sink:
            q_ref, k_ref, v_ref, do_ref, o_ref, lse_ref, s_ref, dq_ref, dk_ref, dv_ref, ds_ref, bias_scr = refs
        else:
            q_ref, k_ref, v_ref, do_ref, o_ref, lse_ref, dq_ref, dk_ref, dv_ref, bias_scr = refs
        pid = pl.program_id(0)

        @pl.when(pid == 0)
        def _():
            _fill_band_bias(bias_scr)
        dq_ref[...] = jnp.zeros_like(dq_ref)
        if gqa:
            @pl.when(pid == 0)
            def _():
                dk_ref[...] = jnp.zeros_like(dk_ref)
                dv_ref[...] = jnp.zeros_like(dv_ref)
        else:
            dk_ref[...] = jnp.zeros_like(dk_ref)
            dv_ref[...] = jnp.zeros_like(dv_ref)

        def step(n, carry, dil, heads):
            m0, m1 = heads.m0, heads.m1
            st, stp, first = _block_pos(n, dil)
            rq, rp = _rows(st, dil), _rows(stp, dil)
            do_, lse_ = do_ref[rq, :], lse_ref[rq, :]
            kk = jnp.concatenate([k_ref[rp, :], k_ref[rq, :]], axis=0)
            vv = jnp.concatenate([v_ref[rp, :], v_ref[rq, :]], axis=0)
            qs, dos = heads.stack(q_ref[rq, :] * QK_SCALE), heads.stack(do_)
            doo = do_ * o_ref[rq, :]
            delta = jnp.concatenate([jnp.sum(doo * m0, axis=1, keepdims=True), jnp.sum(doo * m1, axis=1, keepdims=True)], axis=0)
            lse_s = jnp.concatenate([lse_[:, 0:1], lse_[:, HD:HD + 1]], axis=0)
            p = jnp.exp(_mm(qs, kk, NT) + bias_scr[first] - lse_s)
            ds = p * (_mm(dos, vv, NT) - delta)
            dq_ref[rq, :] += heads.unstack(_mm(ds, kk)) * QK_SCALE
            dk_sum, dv_sum = _mm(ds, qs, TN), _mm(p, dos, TN)
            dk_ref[rp, :] += dk_sum[:BLK]
            dk_ref[rq, :] += dk_sum[BLK:]
            dv_ref[rp, :] += dv_sum[:BLK]
            dv_ref[rq, :] += dv_sum[BLK:]
            return carry

        def blocks(heads):
            for dil in dils:
                lax.fori_loop(0, SEQ // BLK, lambda n, carry, dil=dil: step(n, carry, dil, heads), 0, unroll=2)

        if gqa:
            for grp in range(2):
                pl.when(pid // 2 == grp)(lambda grp=grp: blocks(_HeadStack(grp)))
        else:
            blocks(_HeadStack(None))

        if has_sink:
            m0, m1 = _half_masks()

            def sink_rows(t, acc):
                rt = pl.ds(pl.multiple_of(t * TM, TM), TM)
                return acc - jnp.sum(jnp.exp(s_ref[...] - lse_ref[rt, :]) * (do_ref[rt, :] * o_ref[rt, :]), axis=0, keepdims=True)
            acc = lax.fori_loop(0, SEQ // TM, sink_rows, jnp.zeros((1, LANES), F32))
            per_head = jnp.sum(acc * m0, axis=1, keepdims=True) * m0 + jnp.sum(acc * m1, axis=1, keepdims=True) * m1
            ds_ref[0] = jnp.broadcast_to(per_head, (8, LANES))

    col = lambda blk: pl.BlockSpec((SEQ, LANES), lambda p, blk=blk: (0, blk + p))
    kv = (lambda blk: pl.BlockSpec((SEQ, LANES), lambda p, blk=blk: (0, blk))) if gqa else col
    pair = pl.BlockSpec((SEQ, LANES), lambda p: (0, p))
    in_specs = [col(qblk), kv(kblk), kv(vblk), pair, pair, pair]
    args = [proj, proj, proj, do, o, lse]
    kvw = LANES if gqa else 512
    kv_out = pl.BlockSpec((SEQ, LANES), lambda p: (0, 0)) if gqa else pair
    out_shape = [_sds((SEQ, 512)), _sds((SEQ, kvw)), _sds((SEQ, kvw))]
    out_specs = [pair, kv_out, kv_out]
    if has_sink:
        in_specs.append(pl.BlockSpec((1, LANES), lambda p: (0, p)))
        args.append(sink_x)
        out_shape.append(_sds((4, 8, LANES)))
        out_specs.append(pl.BlockSpec((1, 8, LANES), lambda p: (p, 0, 0)))
    return pl.pallas_call(body, name=name, out_shape=tuple(out_shape), grid=(4,), in_specs=in_specs,
                          out_specs=tuple(out_specs), scratch_shapes=[_BIAS], compiler_params=_cp(56))(*args)


def _shift_down(v, k):
    row = lax.broadcasted_iota(jnp.int32, v.shape, 0)
    return jnp.where(row >= k, pltpu.roll(v, k, axis=0), 0.0)


def _shift_up(v, k):
    n = v.shape[0]
    row = lax.broadcasted_iota(jnp.int32, v.shape, 0)
    return jnp.where(row < n - k, pltpu.roll(v, n - k, axis=0), 0.0)


def _conv_pre(x, w_ref, b_ref):
    u = b_ref[...] + x * w_ref[3:4, :]
    for k in range(1, 4):
        u = u + _shift_down(x, k) * w_ref[3 - k:4 - k, :]
    return u


def _conv_fwd(proj, w, b, name):
    def body(x_ref, w_ref, b_ref, o_ref):
        o_ref[...] = _silu(_conv_pre(x_ref[...], w_ref, b_ref))

    nblk = CONV_CH // LANES
    return pl.pallas_call(body, name=name, out_shape=_sds((SEQ, CONV_CH)), grid=(nblk,),
                          in_specs=[pl.BlockSpec((SEQ, LANES), lambda j: (0, XBC // LANES + j)),
                                    pl.BlockSpec((4, LANES), lambda j: (0, j)), pl.BlockSpec((1, LANES), lambda j: (0, j))],
                          out_specs=pl.BlockSpec((SEQ, LANES), lambda j: (0, j)), compiler_params=_cp())(proj, w, b)


def _conv_bwd(proj, dact, w, b, name):
    def body(x_ref, da_ref, w_ref, b_ref, dx_ref, dw_ref, db_ref):
        x = x_ref[...]
        du = da_ref[...] * _dsilu(_conv_pre(x, w_ref, b_ref))
        dx = du * w_ref[3:4, :]
        for k in range(1, 4):
            dx = dx + _shift_up(du, k) * w_ref[3 - k:4 - k, :]
        dx_ref[...] = dx
        db_ref[...] = jnp.sum(du, axis=0, keepdims=True)
        dw_ref[3:4, :] = jnp.sum(du * x, axis=0, keepdims=True)
        for k in range(1, 4):
            dw_ref[3 - k:4 - k, :] = jnp.sum(du * _shift_down(x, k), axis=0, keepdims=True)

    nblk = CONV_CH // LANES
    blk = pl.BlockSpec((SEQ, LANES), lambda j: (0, j))
    wspec, bspec = pl.BlockSpec((4, LANES), lambda j: (0, j)), pl.BlockSpec((1, LANES), lambda j: (0, j))
    return pl.pallas_call(body, name=name, out_shape=(_sds((SEQ, CONV_CH)), _sds((4, CONV_CH)), _sds((1, CONV_CH))), grid=(nblk,),
                          in_specs=[pl.BlockSpec((SEQ, LANES), lambda j: (0, XBC // LANES + j)), blk, wspec, bspec],
                          out_specs=(blk, wspec, bspec), compiler_params=_cp())(proj, dact, w, b)


def _ssd_chunk(xs, bm, cm, dtr, z, hs, al16, dtb, dskx, nw):
    m0, m1 = _half_masks()
    row = lax.broadcasted_iota(jnp.int32, (BLK, BLK), 0)
    col = lax.broadcasted_iota(jnp.int32, (BLK, BLK), 1)
    causal = row >= col
    tril = causal.astype(F32)
    lane = lax.broadcasted_iota(jnp.int32, (1, LANES), 1)
    sub = lax.broadcasted_iota(jnp.int32, (BLK, 1), 0)
    last_row = (sub == BLK - 1).astype(F32)
    dt = jnp.where(lane < 16, _softplus(dtr + dtb), 0.0)
    a16 = -jnp.exp(al16)
    acum = jnp.dot(tril, dt * a16, precision=HI, preferred_element_type=F32)
    acum_t = acum.T
    gmat = [_mm(cm[g], bm[g], NT) for g in range(2)]
    ys, hn = [], []
    for p in range(8):
        g = p // 4
        pick = [(lane == 2 * p + a).astype(F32) for a in range(2)]
        col_h = [jnp.sum(acum * pick[a], axis=1, keepdims=True) for a in range(2)]
        dt_x = sum(jnp.sum(dt * pick[a], axis=1, keepdims=True) * msk for a, msk in enumerate((m0, m1)))
        ac_x = col_h[0] * m0 + col_h[1] * m1
        a_end = jnp.sum(ac_x * last_row, axis=0, keepdims=True)
        xdt = xs[p] * dt_x
        y = _mm(cm[g], hs[p]) * jnp.exp(ac_x)
        for a, msk in enumerate((m0, m1)):
            row_h = jnp.sum(acum_t * (sub == 2 * p + a).astype(F32), axis=0, keepdims=True)
            decay = jnp.exp(jnp.where(causal, col_h[a] - row_h, NEG))
            y = y + _mm(gmat[g] * decay, xdt * msk)
        st = _mm(bm[g], xdt * jnp.exp(a_end - ac_x), TN)
        hn.append(hs[p] * jnp.exp(a_end) + st)
        y = y + dskx[p] * xs[p]
        ys.append(y * _silu(z[p]))
    out = []
    for g in range(2):
        ms = sum(jnp.sum(ys[p] * ys[p], axis=1, keepdims=True) for p in range(4 * g, 4 * g + 4)) * (1.0 / 512)
        rstd = lax.rsqrt(ms + EPS)
        out += [ys[p] * rstd * nw[p] for p in range(4 * g, 4 * g + 4)]
    return out, hn


def _tiles(ref, n, off=0):
    return [ref[:, off + LANES * p:off + LANES * (p + 1)] for p in range(n)]


def _ssd_load(xbc_ref, z_ref, dt_ref, al16_ref, dtb_ref, dsk_ref, nw_ref):
    return (_tiles(xbc_ref, 8), _tiles(xbc_ref, 2, 1024), _tiles(xbc_ref, 2, 1280), dt_ref[...], _tiles(z_ref, 8)), \
           (al16_ref[...], dtb_ref[...], _tiles(dsk_ref, 8), _tiles(nw_ref, 8))


_NCH = SEQ // BLK


def _ssd_param_specs():
    return [_full((1, LANES)), _full((1, LANES)), _full((1, 1024)), _full((1, 1024))]


def _ssd_fwd(xbc_act, proj, al16, dtb, dskx, nw, name):
    def body(xbc_ref, z_ref, dt_ref, al16_ref, dtb_ref, dsk_ref, nw_ref, y_ref, hin_ref, h_scr):
        @pl.when(pl.program_id(0) == 0)
        def _():
            h_scr[...] = jnp.zeros_like(h_scr)
        acts, params = _ssd_load(xbc_ref, z_ref, dt_ref, al16_ref, dtb_ref, dsk_ref, nw_ref)
        hs = _tiles(h_scr, 8)
        hin_ref[0] = h_scr[...]
        ys, hn = _ssd_chunk(*acts, hs, *params)
        for p in range(8):
            y_ref[:, LANES * p:LANES * (p + 1)] = ys[p]
            h_scr[:, LANES * p:LANES * (p + 1)] = hn[p]

    return pl.pallas_call(
        body, name=name, out_shape=(_sds((SEQ, 1024)), _sds((_NCH, BLK, 1024))), grid=(_NCH,),
        in_specs=[pl.BlockSpec((BLK, CONV_CH), lambda c: (c, 0)), pl.BlockSpec((BLK, 1024), lambda c: (c, ZB // 1024)),
                  pl.BlockSpec((BLK, LANES), lambda c: (c, DTC // LANES))] + _ssd_param_specs(),
        out_specs=(pl.BlockSpec((BLK, 1024), lambda c: (c, 0)), pl.BlockSpec((1, BLK, 1024), lambda c: (c, 0, 0))),
        scratch_shapes=[pltpu.VMEM((BLK, 1024), F32)], compiler_params=_cp())(xbc_act, proj, proj, al16, dtb, dskx, nw)


def _ssd_bwd(xbc_act, proj, hin, dyb, al16, dtb, dskx, nw, name):
    def body(xbc_ref, z_ref, dt_ref, hin_ref, dy_ref, al16_ref, dtb_ref, dsk_ref, nw_ref,
             dxbc_ref, dz_ref, ddt_ref, dal16_ref, ddtb_ref, ddsk_ref, dnw_ref, dh_scr):
        @pl.when(pl.program_id(0) == 0)
        def _():
            dh_scr[...] = jnp.zeros_like(dh_scr)
            for r in (dal16_ref, ddtb_ref, ddsk_ref, dnw_ref):
                r[...] = jnp.zeros_like(r)
        acts, params = _ssd_load(xbc_ref, z_ref, dt_ref, al16_ref, dtb_ref, dsk_ref, nw_ref)
        hs = [hin_ref[0, :, LANES * p:LANES * (p + 1)] for p in range(8)]
        _, vjp = jax.vjp(lambda a, h, q: _ssd_chunk(*a, h, *q), acts, hs, params)
        (dxs, dbm, dcm, ddt, dz), dhs, (dal16, ddtb, ddsk, dnw) = vjp((_tiles(dy_ref, 8), _tiles(dh_scr, 8)))
        for p in range(8):
            cols = slice(LANES * p, LANES * (p + 1))
            dxbc_ref[:, cols] = dxs[p]
            dz_ref[:, cols] = dz[p]
            dh_scr[:, cols] = dhs[p]
            ddsk_ref[:, cols] += ddsk[p]
            dnw_ref[:, cols] += dnw[p]
        for g in range(2):
            dxbc_ref[:, 1024 + LANES * g:1024 + LANES * (g + 1)] = dbm[g]
            dxbc_ref[:, 1280 + LANES * g:1280 + LANES * (g + 1)] = dcm[g]
        ddt_ref[...] = ddt
        dal16_ref[...] += dal16
        ddtb_ref[...] += ddtb

    rev = lambda c: _NCH - 1 - c
    return pl.pallas_call(
        body, name=name,
        out_shape=(_sds((SEQ, CONV_CH)), _sds((SEQ, 1024)), _sds((SEQ, LANES)),
                   _sds((1, LANES)), _sds((1, LANES)), _sds((1, 1024)), _sds((1, 1024))),
        grid=(_NCH,),
        in_specs=[pl.BlockSpec((BLK, CONV_CH), lambda c: (rev(c), 0)), pl.BlockSpec((BLK, 1024), lambda c: (rev(c), ZB // 1024)),
                  pl.BlockSpec((BLK, LANES), lambda c: (rev(c), DTC // LANES)), pl.BlockSpec((1, BLK, 1024), lambda c: (rev(c), 0, 0)),
                  pl.BlockSpec((BLK, 1024), lambda c: (rev(c), 0))] + _ssd_param_specs(),
        out_specs=(pl.BlockSpec((BLK, CONV_CH), lambda c: (rev(c), 0)), pl.BlockSpec((BLK, 1024), lambda c: (rev(c), 0)),
                   pl.BlockSpec((BLK, LANES), lambda c: (rev(c), 0)),
                   _full((1, LANES)), _full((1, LANES)), _full((1, 1024)), _full((1, 1024))),
        scratch_shapes=[pltpu.VMEM((BLK, 1024), F32)], compiler_params=_cp())(xbc_act, proj, proj, hin, dyb, al16, dtb, dskx, nw)


def _rstd(v):
    return lax.rsqrt(jnp.mean(v * v, axis=1, keepdims=True) + EPS)


def _rms_bwd(dn, n, rstd):
    return rstd * (dn - n * jnp.mean(dn * n, axis=1, keepdims=True))


_VEC = _full((1, D))


def _layer_spec(layer):
    return pl.BlockSpec((None, 2048, D), lambda *_: (layer, 0, 0))

_ROW = pl.BlockSpec((TM, D), lambda i, *_: (i, 0))


def _proj_fwd(x, pre_w, scale, shift, w, layer, name):
    tn, ni = 1024, SEQ // TM

    def body(x_ref, pw_ref, sc_ref, sh_ref, w_ref, o_ref, h_ref, h_scr):
        rows = pl.ds(pl.multiple_of(pl.program_id(1) * TM, TM), TM)

        @pl.when(pl.program_id(0) == 0)
        def _():
            xv = x_ref[...]
            h = ((xv * _rstd(xv) * pw_ref[...]) * (1.0 + sc_ref[...]) + sh_ref[...]).astype(h_ref.dtype)
            h_scr[rows, :] = h
            h_ref[...] = h
        o_ref[...] = jnp.dot(h_scr[rows, :], w_ref[...].astype(MXU), preferred_element_type=F32)

    first_pass = pl.BlockSpec((TM, D), lambda j, i: (jnp.where(j == 0, i, ni - 1), 0))
    return pl.pallas_call(body, name=name, out_shape=(_sds((SEQ, NP)), _sds((SEQ, D), MXU)), grid=(NP // tn, ni),
                          in_specs=[first_pass, _VEC, _VEC, _VEC, pl.BlockSpec((None, D, tn), lambda j, i: (layer, 0, j))],
                          out_specs=(pl.BlockSpec((TM, tn), lambda j, i: (i, j)), first_pass),
                          scratch_shapes=[pltpu.VMEM((SEQ, D), MXU)], compiler_params=_cp())(x, pre_w, scale, shift, w)


_HALF = pl.BlockSpec((TM, 512), lambda i: (i, 0))
_Z_A = pl.BlockSpec((TM, 512), lambda i: (i, ZA // 512))
_Z_C = pl.BlockSpec((TM, 512), lambda i: (i, ZC // 512))


def _out_fwd(o_a, yb, o_c, proj, w, layer, x, gate, post_w, name):
    def body(oa_ref, yb_ref, oc_ref, za_ref, zc_ref, w_ref, x_ref, g_ref, pw_ref, xn_ref, y_ref):
        y = (_mm(oa_ref[...] * _silu(za_ref[...]), w_ref[0:512, :]) + _mm(yb_ref[...], w_ref[512:1536, :])
             + _mm(oc_ref[...] * _silu(zc_ref[...]), w_ref[1536:2048, :]))
        y_ref[...] = y
        xn_ref[...] = x_ref[...] + g_ref[...] * (y * _rstd(y) * pw_ref[...])

    return pl.pallas_call(body, name=name, out_shape=(_sds((SEQ, D)), _sds((SEQ, D))), grid=(SEQ // TM,),
                          in_specs=[_HALF, _ROW, _HALF, _Z_A, _Z_C, _layer_spec(layer), _ROW, _VEC, _VEC],
                          out_specs=(_ROW, _ROW), compiler_params=_cp())(o_a, yb, o_c, proj, proj, w, x, gate, post_w)


def _post_bwd(dxo, y, gate, post_w, name):
    def body(dx_ref, y_ref, g_ref, pw_ref, dy_ref, dg_ref, dpw_ref):
        @pl.when(pl.program_id(0) == 0)
        def _():
            dg_ref[...] = jnp.zeros_like(dg_ref)
            dpw_ref[...] = jnp.zeros_like(dpw_ref)
        dx, y = dx_ref[...], y_ref[...]
        rstd = _rstd(y)
        n = y * rstd
        dg_ref[...] += jnp.sum(dx * (n * pw_ref[...]), axis=0, keepdims=True)
        dr = dx * g_ref[...]
        dpw_ref[...] += jnp.sum(dr * n, axis=0, keepdims=True)
        dy_ref[...] = _rms_bwd(dr * pw_ref[...], n, rstd)

    return pl.pallas_call(body, name=name, out_shape=(_sds((SEQ, D)), _sds((1, D)), _sds((1, D))), grid=(SEQ // TM,),
                          in_specs=[_ROW, _ROW, _VEC, _VEC], out_specs=(_ROW, _VEC, _VEC), compiler_params=_cp())(dxo, y, gate, post_w)


def _dymix(dy, w, layer, o_a, o_c, proj, name):
    def body(dy_ref, w_ref, oa_ref, oc_ref, za_ref, zc_ref, doa_ref, dza_ref, b_ref, doc_ref, dzc_ref):
        dy = dy_ref[...]
        b_ref[...] = _mm(dy, w_ref[512:1536, :], NT)
        for rows, o_ref, z_ref, do_ref, dz_ref in ((slice(0, 512), oa_ref, za_ref, doa_ref, dza_ref),
                                                   (slice(1536, 2048), oc_ref, zc_ref, doc_ref, dzc_ref)):
            dyg, z = _mm(dy, w_ref[rows, :], NT), z_ref[...]
            do_ref[...] = dyg * _silu(z)
            dz_ref[...] = dyg * o_ref[...] * _dsilu(z)

    return pl.pallas_call(body, name=name, out_shape=(_sds((SEQ, 512)), _sds((SEQ, 512)), _sds((SEQ, D)), _sds((SEQ, 512)), _sds((SEQ, 512))),
                          grid=(SEQ // TM,), in_specs=[_ROW, _layer_spec(layer), _HALF, _HALF, _Z_A, _Z_C],
                          out_specs=(_HALF, _HALF, _ROW, _HALF, _HALF), compiler_params=_cp())(dy, w, o_a, o_c, proj, proj)


def _dwout(o_a, yb, o_c, proj, dy, name):
    def body(oa_ref, yb_ref, oc_ref, za_ref, zc_ref, dy_ref, o_ref):
        @pl.when(pl.program_id(0) == 0)
        def _():
            o_ref[...] = jnp.zeros_like(o_ref)
        dy = dy_ref[...]
        o_ref[0:512, :] += _mm(oa_ref[...] * _silu(za_ref[...]), dy, TN)
        o_ref[512:1536, :] += _mm(yb_ref[...], dy, TN)
        o_ref[1536:2048, :] += _mm(oc_ref[...] * _silu(zc_ref[...]), dy, TN)

    return pl.pallas_call(body, name=name, out_shape=_sds((2048, D)), grid=(SEQ // TM,),
                          in_specs=[_HALF, _ROW, _HALF, _Z_A, _Z_C, _ROW], out_specs=_full((2048, D)),
                          compiler_params=_cp())(o_a, yb, o_c, proj, proj, dy)


def _dwin(h, pieces, name):
    n = len(pieces)
    widths = [p.shape[1] for p in pieces]
    half = NP // 2

    def body(*refs):
        h_ref, p_refs, o_ref = refs[0], refs[1:1 + n], refs[1 + n]

        @pl.when(pl.program_id(0) == 0)
        def _():
            o_ref[...] = jnp.zeros_like(o_ref)
        hv, c0 = h_ref[...], 0
        for p_ref, wd in zip(p_refs, widths):
            o_ref[:, c0:c0 + wd] += _mm(hv, p_ref[...], TN)
            c0 += wd

    return pl.pallas_call(body, name=name, out_shape=_sds((D, half)), grid=(SEQ // TM,),
                          in_specs=[_ROW] + [pl.BlockSpec((TM, wd), lambda k: (k, 0)) for wd in widths],
                          out_specs=_full((D, half)), compiler_params=_cp(56))(h, *pieces)


_TMH = 256


def _dh_bwd(pieces, w, x, pre_w, scale, dxo, name):
    n = len(pieces)
    widths = [p.shape[1] for p in pieces]

    def body(*refs):
        p_refs, (w_ref, x_ref, pw_ref, sc_ref, dxo_ref, dx_ref, dsh_ref, dsc_ref, dpw_ref) = refs[:n], refs[n:]

        @pl.when(pl.program_id(0) == 0)
        def _():
            for r in (dsh_ref, dsc_ref, dpw_ref):
                r[...] = jnp.zeros_like(r)
        dh, c0 = 0.0, 0
        for p_ref, wd in zip(p_refs, widths):
            dh = dh + _mm(p_ref[...], w_ref[:, c0:c0 + wd], NT)
            c0 += wd
        xv = x_ref[...]
        rstd = _rstd(xv)
        nrm = xv * rstd
        dsh_ref[...] += jnp.sum(dh, axis=0, keepdims=True)
        dsc_ref[...] += jnp.sum(dh * (nrm * pw_ref[...]), axis=0, keepdims=True)
        dhn = dh * (1.0 + sc_ref[...])
        dpw_ref[...] += jnp.sum(dhn * nrm, axis=0, keepdims=True)
        dx_ref[...] = _rms_bwd(dhn * pw_ref[...], nrm, rstd) + dxo_ref[...]

    row = pl.BlockSpec((_TMH, D), lambda i: (i, 0))
    return pl.pallas_call(body, name=name, out_shape=(_sds((SEQ, D)), _sds((1, D)), _sds((1, D)), _sds((1, D))),
                          grid=(SEQ // _TMH,),
                          in_specs=[pl.BlockSpec((_TMH, wd), lambda i: (i, 0)) for wd in widths]
                          + [pl.BlockSpec((None, D, NP), lambda i: (0, 0, 0)), row, _VEC, _VEC, row],
                          out_specs=(row, _VEC, _VEC, _VEC), compiler_params=_cp(56))(*pieces, w, x, pre_w, scale, dxo)


def _w_in_padded(land, name):
    rows = 128

    def body(l_ref, o_ref):
        o_ref[...] = _pad_cols(jnp.concatenate([l_ref[k] for k in range(4)], axis=1))

    return pl.pallas_call(body, name=name, out_shape=_sds((D, NP), land.dtype), grid=(D // rows,),
                          in_specs=[pl.BlockSpec((4, rows, SHARD_IN), lambda i: (0, i, 0))],
                          out_specs=pl.BlockSpec((rows, NP), lambda i: (i, 0)), compiler_params=_cp())(land)


def _grad_blocks(dwa, dwb, name):
    rows = 128

    def body(a_ref, b_ref, o_ref):
        g = _unpad_cols(jnp.concatenate([a_ref[...], b_ref[...]], axis=1))
        for k in range(4):
            o_ref[k] = g[:, SHARD_IN * k:SHARD_IN * (k + 1)].astype(o_ref.dtype)

    half = pl.BlockSpec((rows, NP // 2), lambda i: (i, 0))
    return pl.pallas_call(body, name=name, out_shape=_sds((4, D, SHARD_IN), jnp.bfloat16), grid=(D // rows,),
                          in_specs=[half, half], out_specs=pl.BlockSpec((4, rows, SHARD_IN), lambda i: (0, i, 0)),
                          compiler_params=_cp())(dwa, dwb)


def _loss_bwd(xf, tgt, name):
    def body(x_ref, t_ref, dx_ref, l_ref):
        @pl.when(pl.program_id(0) == 0)
        def _():
            l_ref[...] = jnp.zeros_like(l_ref)
        e = x_ref[...] - t_ref[...]
        dx_ref[...] = e * (1.0 / D)
        l_ref[...] += 0.5 * jnp.sum(jnp.mean(e * e, axis=1, keepdims=True), axis=0, keepdims=True)

    return pl.pallas_call(body, name=name, out_shape=(_sds((SEQ, D)), _sds((8, LANES))), grid=(SEQ // TM,),
                          in_specs=[_ROW, _ROW], out_specs=(_ROW, _full((8, LANES))), compiler_params=_cp())(xf, tgt)


def _mod_part(c_all, ada_w, ada_b, name):
    def body(c_ref, w_ref, b_ref, o_ref):
        o_ref[0] = _mm(_silu(c_ref[...]), w_ref[0]) + b_ref[0]

    return pl.pallas_call(body, name=name, out_shape=_sds((DEPTH, 8, 768)), grid=(DEPTH,),
                          in_specs=[_full((8, D)), pl.BlockSpec((1, D, 768), lambda i: (i, 0, 0)), pl.BlockSpec((1, 1, 768), lambda i: (i, 0, 0))],
                          out_specs=pl.BlockSpec((1, 8, 768), lambda i: (i, 0, 0)), compiler_params=_cp())(c_all, ada_w, ada_b)


def _ada_grad(c_t, dmod, name):
    def body(c_ref, d_ref, o_ref):
        ca = _silu(c_ref[...])
        dm = d_ref[0]
        acc = ca[:, 0:1] * dm[0:1, :]
        for s in range(1, 8):
            acc = acc + ca[:, s:s + 1] * dm[s:s + 1, :]
        o_ref[0] = acc

    return pl.pallas_call(body, name=name, out_shape=_sds((DEPTH, D, 768)), grid=(DEPTH,),
                          in_specs=[_full((D, LANES)), pl.BlockSpec((1, 8, 768), lambda i: (i, 0, 0))],
                          out_specs=pl.BlockSpec((1, D, 768), lambda i: (i, 0, 0)), compiler_params=_cp())(c_t, dmod)


def _pack(parts):
    flat = []
    for p in parts:
        f = p.reshape(-1)
        flat.append(jnp.pad(f, (0, (-f.size) % LANES)))
    v = jnp.concatenate(flat)
    return jnp.pad(v, (0, (-v.size) % (8 * LANES))).reshape(-1, LANES)


def _unpack(v, shapes):
    v = v.reshape(-1)
    out, off = [], 0
    for s in shapes:
        n = math.prod(s)
        out.append(v[off:off + n].reshape(s))
        off += n + (-n) % LANES
    return out


_GIVEN_DT, _GIVEN_C = 4608, 4624


def _pad_cols(w):
    return jnp.concatenate([w[..., :_GIVEN_DT], w[..., _GIVEN_C:], w[..., _GIVEN_DT:_GIVEN_C],
                            jnp.zeros(w.shape[:-1] + (NP - IN_COLS,), w.dtype)], axis=-1)


def _unpad_cols(w):
    return jnp.concatenate([w[..., :_GIVEN_DT], w[..., DTC:DTC + 16], w[..., _GIVEN_DT:DTC]], axis=-1)


def _pad_lanes(v):
    return jnp.pad(v, (0, LANES - v.shape[0])).reshape(1, LANES)


def _local_step(x2, tgt, mod, weights_of, grads_done, pre_w, post_w, conv_w, conv_b, dt_bias, a_log, d_skip, nw, sinks):
    saved = []
    xcur = x2
    for i in range(DEPTH):
        shift, scale, gate = mod[i:i + 1, :D], mod[i:i + 1, D:2 * D], mod[i:i + 1, 2 * D:]
        pw, qw = pre_w[i:i + 1], post_w[i:i + 1]
        w_p, w_o = weights_of(i, xcur)
        proj, h = _proj_fwd(xcur, pw, scale, shift, w_p, 0, "proj_fwd")
        o_a, lse_a = _attn_fwd(proj, QA // LANES, KA // LANES, VA // LANES, DILS, False, None, "attn_a_fwd")
        sink_x = jnp.repeat(sinks[i], HD).reshape(1, 512)
        o_c, lse_c = _attn_fwd(proj, QC // LANES, KC // LANES, VC // LANES, (1,), True, sink_x, "attn_c_fwd")
        cw, cb = conv_w[i], conv_b[i:i + 1]
        xbc_act = _conv_fwd(proj, cw, cb, "conv_fwd")
        ssd_p = (_pad_lanes(a_log[i]), _pad_lanes(dt_bias[i]), jnp.repeat(d_skip[i], HD).reshape(1, 1024), nw[i:i + 1])
        yb, hin = _ssd_fwd(xbc_act, proj, *ssd_p, "ssd_fwd")
        xnew, y = _out_fwd(o_a, yb, o_c, proj, w_o, 0, xcur, gate, qw, "out_fwd")
        saved.append((w_p, w_o, xcur, scale, gate, pw, qw, proj, h, o_a, lse_a, sink_x, o_c, lse_c, cw, cb, xbc_act, ssd_p, yb, hin, y))
        xcur = xnew
    dx, ltile = _loss_bwd(xcur, tgt, "loss")
    dmod, small = [None] * DEPTH, [None] * DEPTH
    for i in reversed(range(DEPTH)):
        w_p, w_o, xin, scale, gate, pw, qw, proj, h, o_a, lse_a, sink_x, o_c, lse_c, cw, cb, xbc_act, ssd_p, yb, hin, y = saved[i]
        dy, dgate, dpost = _post_bwd(dx, y, gate, qw, "post_bwd")
        do_a, dz_a, dyb, do_c, dz_c = _dymix(dy, w_o, 0, o_a, o_c, proj, "dymix")
        dwo = _dwout(o_a, yb, o_c, proj, dy, "dwout")
        dq_a, dk_a, dv_a = _attn_bwd(proj, QA // LANES, KA // LANES, VA // LANES, do_a, o_a, lse_a, DILS, False, None, "attn_a_bwd")
        dq_c, dk_c, dv_c, dsk = _attn_bwd(proj, QC // LANES, KC // LANES, VC // LANES, do_c, o_c, lse_c, (1,), True, sink_x, "attn_c_bwd")
        dxbc_act, dz_b, ddt, dal16, ddtb, ddsk, dnw = _ssd_bwd(xbc_act, proj, hin, dyb, *ssd_p, "ssd_bwd")
        dxbc, dcw, dcb = _conv_bwd(proj, dxbc_act, cw, cb, "conv_bwd")
        half_a, half_b = [dq_a, dk_a, dv_a, dz_a, dz_b], [dxbc, dq_c, dz_c, dk_c, dv_c, ddt]
        sent = grads_done(i, _dwin(h, half_a, "dwin_a"), _dwin(h, half_b, "dwin_b"), dwo)
        dx, dshift, dscale, dpre = _dh_bwd(half_a + half_b, w_p, xin, pw, scale + sent[0, 0], dx, "dh_bwd")
        dmod[i] = jnp.concatenate([dshift, dscale, dgate], axis=1)
        small[i] = (dpre, dpost, dcw, dcb, ddtb[0, :16], dal16[0, :16], ddsk.reshape(16, HD).sum(axis=1), dnw, dsk[:, 0, ::HD].reshape(8))
    return ltile, dx, jnp.concatenate(dmod, axis=0), small


_SMALL = ((1, D), (1, D), (4, CONV_CH), (1, CONV_CH), (16,), (16,), (16,), (1, D), (8,))


def kernel(x, c, ada_w, ada_b, pre_norm_w, post_norm_w, w_in, conv_w, conv_b, dt_bias, a_log, d_skip, ssm_norm_w, sinks, w_out, loss_target, m_ada_w, m_ada_b, m_pre_norm_w, m_post_norm_w, m_w_in, m_conv_w, m_conv_b, m_dt_bias, m_a_log, m_d_skip, m_ssm_norm_w, m_sinks, m_w_out, v_ada_w, v_ada_b, v_pre_norm_w, v_post_norm_w, v_w_in, v_conv_w, v_conv_b, v_dt_bias, v_a_log, v_d_skip, v_ssm_norm_w, v_sinks, v_w_out):
    xi, yi, ci = lax.axis_index("x"), lax.axis_index("y"), lax.axis_index("c")
    chip = 2 * xi + yi
    me = 2 * chip + ci

    w_in_b = _cast_bf16(w_in, 512, "cast_w_in")
    w_out_b = _cast_bf16(w_out, 512, "cast_w_out")
    gathers = []
    for i in range(DEPTH):
        lands = [lax.dynamic_update_slice(lax.empty((4,) + a.shape[1:], a.dtype), a[i][None], (chip, 0, 0)) for a in (w_in_b, w_out_b)]
        gathers.append(_split_start(None, lands, f"gather_start{i}"))
    all_started = gathers[0][3] + gathers[1][3] + gathers[2][3] + gathers[3][3]

    def weights_of(i, after):
        send_sems, recv_sems, thru, _ = gathers[i]
        if i == 0:
            after = all_started + mod[:1, :LANES]
        g_in, g_out = _split_wait(send_sems, recv_sems, thru, 2, after, f"gather_wait{i}")
        return _w_in_padded(g_in, "w_in_padded")[None], g_out.reshape(1, 2048, D)

    scatters = [None] * DEPTH

    def grads_done(i, dwa, dwb, dwo):
        blk_in = _grad_blocks(dwa, dwb, "grad_blocks")
        blk_out = _cast_bf16(dwo.reshape(4, 512, D), 512, "cast_dw_out")
        lands = [lax.empty(blk_in.shape, blk_in.dtype), lax.empty(blk_out.shape, blk_out.dtype)]
        scatters[i] = _split_start([blk_in, blk_out], lands, f"scatter_start{i}")
        return scatters[i][3]

    g0 = _allgather8(_pack([c, conv_w]), "gather_c")
    c_all = g0[:, :8, :].reshape(8, D)
    conv_w_full = jnp.concatenate([g0[2 * k, 8:56, :].reshape(DEPTH, 4, CONV_CH // 4) for k in range(4)], axis=-1)

    ada_b_mine = lax.dynamic_slice_in_dim(ada_b, 768 * chip, 768, axis=1).reshape(DEPTH, 1, 768)
    gm = _allgather8(_mod_part(c_all, ada_w, ada_b_mine, "mod_part").reshape(DEPTH * 8, 768), "gather_mod")
    gm = gm.reshape(4, 2, DEPTH, 8, 768)[:, 0]
    mod = lax.dynamic_index_in_dim(gm, me, axis=2, keepdims=False).transpose(1, 0, 2).reshape(DEPTH, 3 * D)

    ltile, dx, dmod, small = _local_step(x[0], loss_target[0], mod, weights_of, grads_done, pre_norm_w, post_norm_w, conv_w_full,
                                         conv_b, dt_bias, a_log, d_skip, ssm_norm_w, sinks)

    packed = _pack([dmod] + [g for layer in small for g in layer] + [ltile[0]])
    gs = _allgather8(packed, "gather_small")
    tot = _sum_blocks(gs[:, None], packed.shape[0], "sum_small")[0]
    parts = _unpack(tot, [(DEPTH, 3 * D)] + list(_SMALL) * DEPTH + [(LANES,)])
    g_ada_b, loss = parts[0], parts[-1][0]
    per_layer = [parts[1 + len(_SMALL) * i:1 + len(_SMALL) * (i + 1)] for i in range(DEPTH)]
    g_pre, g_post, g_cw, g_cb, g_dtb, g_al, g_dsk, g_nw, g_sk = [jnp.stack([per_layer[i][j] for i in range(DEPTH)]) for j in range(len(_SMALL))]
    g_pre, g_post, g_cb, g_nw = g_pre[:, 0], g_post[:, 0], g_cb[:, 0], g_nw[:, 0]
    g_cw = lax.dynamic_slice_in_dim(g_cw, (CONV_CH // 4) * chip, CONV_CH // 4, axis=2)

    dmod_all = gs[:, :(DEPTH * 3 * D) // LANES, :].reshape(8, DEPTH, 3 * D).transpose(1, 0, 2)
    dmod_mine = lax.dynamic_slice_in_dim(dmod_all, 768 * chip, 768, axis=2)
    c_t = jnp.pad(c_all.T, ((0, 0), (0, LANES - 8)))
    g_ada_w = _ada_grad(c_t, dmod_mine, "ada_grad")

    r_in, r_out = [], []
    for i in range(DEPTH):
        send_sems, recv_sems, thru, _ = scatters[i]
        done = _split_wait(send_sems, recv_sems, thru, 2, dx, f"scatter_wait{i}")
        for r, land, src in zip((r_in, r_out), done[2:], done[:2]):
            own = lax.dynamic_index_in_dim(src, chip, axis=0, keepdims=True)
            r.append(lax.dynamic_update_slice(land, own, (chip, 0, 0)))
    r_in, r_out = jnp.stack(r_in, axis=1), jnp.stack(r_out, axis=1)
    p_in = _sum_blocks(r_in, 256, "sum_w_in")
    p_out = _sum_blocks(r_out, 512, "sum_w_out")
    s_in, s_out = _sibling_swap([p_in, p_out], "swap_partials")

    res = {}
    res["ada_w"] = _adamw(ada_w, [g_ada_w], m_ada_w, v_ada_w, 512, "adamw_ada_w")
    res["w_in"] = _adamw(w_in, [p_in, s_in], m_w_in, v_w_in, 256, "adamw_w_in")
    res["w_out"] = _adamw(w_out, [p_out, s_out], m_w_out, v_w_out, 512, "adamw_w_out")
    names = ["ada_b", "pre_norm_w", "post_norm_w", "conv_w", "conv_b", "dt_bias", "a_log", "d_skip", "ssm_norm_w", "sinks"]
    ws = [ada_b, pre_norm_w, post_norm_w, conv_w, conv_b, dt_bias, a_log, d_skip, ssm_norm_w, sinks]
    gsm = [g_ada_b, g_pre, g_post, g_cw, g_cb, g_dtb, g_al, g_dsk, g_nw, g_sk]
    ms = [m_ada_b, m_pre_norm_w, m_post_norm_w, m_conv_w, m_conv_b, m_dt_bias, m_a_log, m_d_skip, m_ssm_norm_w, m_sinks]
    vs = [v_ada_b, v_pre_norm_w, v_post_norm_w, v_conv_w, v_conv_b, v_dt_bias, v_a_log, v_d_skip, v_ssm_norm_w, v_sinks]
    pw_, pg_, pm_, pv_ = _pack(ws), _pack(gsm), _pack(ms), _pack(vs)
    small_out = _adamw(pw_[None], [pg_[None]], pm_[None], pv_[None], pw_.shape[0], "adamw_small")
    shapes = [w.shape for w in ws]
    for kind in range(4):
        for nm, a in zip(names, _unpack(small_out[kind][0], shapes)):
            res.setdefault(nm, [None] * 4)[kind] = a
    order = ["ada_w", "ada_b", "pre_norm_w", "post_norm_w", "w_in", "conv_w", "conv_b", "dt_bias", "a_log", "d_skip", "ssm_norm_w", "sinks", "w_out"]
    return (loss, dx[None], *[res[n][0] for n in order], *[res[n][1] for n in order], *[res[n][2] for n in order], *[res[n][3] for n in order])
```

```python
import math

import jax
import jax.numpy as jnp
from jax import lax
from jax.experimental import pallas as pl
from jax.experimental.pallas import tpu as pltpu

F32 = jnp.float32
MXU = jnp.bfloat16
HI = lax.Precision.HIGHEST
MESH = pl.DeviceIdType.MESH

SEQ = 4096
D = 1024
DEPTH = 4
HD = 64
QK_SCALE = HD ** -0.5
LANES = 128
BLK = 128
DILS = (1, 4, 16)
NEG = -1e30
EPS = 1e-6
MIB = 1024 * 1024

NP = 6144
QA, KA, VA, ZA = 0, 512, 1024, 1536
ZB, XBC = 2048, 3072
QC, ZC, KC, VC = 4608, 5120, 5632, 5760
DTC = 5888
IN_COLS = 5904
SHARD_IN = IN_COLS // 4
CONV_CH = 1536
TM = 512

ADAM_LR, ADAM_B1, ADAM_B2, ADAM_EPS, ADAM_WD, ADAM_STEP = 0.001, 0.9, 0.999, 1e-08, 0.01, 10

NT = (((1,), (1,)), ((), ()))
TN = (((0,), (0,)), ((), ()))


def _cp(vmem_mib=48):
    return pltpu.CompilerParams(vmem_limit_bytes=vmem_mib * MIB)


def _sds(shape, dtype=F32):
    return jax.ShapeDtypeStruct(shape, dtype)


def _full(shape):
    n = len(shape)
    return pl.BlockSpec(shape, lambda *_: (0,) * n)


def _mm(a, b, dims=None):
    if dims is None:
        return jnp.dot(a.astype(MXU), b.astype(MXU), preferred_element_type=F32)
    return lax.dot_general(a.astype(MXU), b.astype(MXU), dims, preferred_element_type=F32)


def _sigmoid(x):
    return 1.0 / (1.0 + jnp.exp(-x))


def _silu(x):
    return x * _sigmoid(x)


def _dsilu(x):
    s = _sigmoid(x)
    return s * (1.0 + x * (1.0 - s))


def _softplus(x):
    ax = jnp.where(x >= 0, x, -x)
    return jnp.maximum(x, 0.0) + jnp.log1p(jnp.exp(-ax))


def _half_masks():
    lane = lax.broadcasted_iota(jnp.int32, (1, LANES), 1)
    m0 = (lane < HD).astype(F32)
    return m0, 1.0 - m0


def _allgather8(v, name):
    r, cc = v.shape

    def body(v_ref, out_ref, send_sems, recv_sems):
        x, y, c = lax.axis_index("x"), lax.axis_index("y"), lax.axis_index("c")
        me = 4 * x + 2 * y + c
        out_ref[me] = v_ref[...]
        peers = []
        for k in range(1, 8):
            px = 1 - x if k & 4 else x
            py = 1 - y if k & 2 else y
            pc = 1 - c if k & 1 else c
            peers.append((px, py, pc))
        sends = []
        for k, peer in enumerate(peers):
            cp = pltpu.make_async_remote_copy(src_ref=v_ref, dst_ref=out_ref.at[me], send_sem=send_sems.at[k],
                                              recv_sem=recv_sems.at[k], device_id=peer, device_id_type=MESH)
            cp.start()
            sends.append(cp)
        for k, (px, py, pc) in enumerate(peers):
            pltpu.make_async_remote_copy(src_ref=v_ref, dst_ref=out_ref.at[4 * px + 2 * py + pc], send_sem=send_sems.at[k],
                                         recv_sem=recv_sems.at[k], device_id=(px, py, pc), device_id_type=MESH).wait_recv()
        for cp in sends:
            cp.wait_send()

    return pl.pallas_call(
        body, name=name, out_shape=_sds((8, r, cc)),
        in_specs=[pl.BlockSpec(memory_space=pltpu.VMEM)], out_specs=pl.BlockSpec(memory_space=pltpu.VMEM),
        scratch_shapes=[pltpu.SemaphoreType.DMA((7,)), pltpu.SemaphoreType.DMA((7,))],
        compiler_params=_cp(32),
    )(v)


_HBM = pl.BlockSpec(memory_space=pltpu.HBM)
_SEM = pl.BlockSpec(memory_space=pltpu.SEMAPHORE)
_EFFECT = pltpu.SideEffectType.DATAFLOW_SIDE_EFFECTING


def _chip_copies(src_refs, land_refs, send_sems, recv_sems):
    x, y, c = lax.axis_index("x"), lax.axis_index("y"), lax.axis_index("c")
    mine = 2 * x + y
    out = []
    for i, land in enumerate(land_refs):
        for j, (px, py) in enumerate([(1 - x, y), (x, 1 - y), (1 - x, 1 - y)]):
            src = src_refs[i].at[2 * px + py] if src_refs else land.at[mine]
            mk = lambda dst, i=i, j=j, src=src, px=px, py=py: pltpu.make_async_remote_copy(
                src_ref=src, dst_ref=dst, send_sem=send_sems.at[3 * i + j], recv_sem=recv_sems.at[3 * i + j],
                device_id=(px, py, c), device_id_type=MESH)
            out.append((mk(land.at[mine]), mk(land.at[2 * px + py])))
    return out


def _split_start(srcs, lands, name):
    ops = list(srcs or []) + list(lands)
    ns, n = len(srcs or []), len(lands)

    def body(*refs):
        src_refs, land_refs = refs[:ns], refs[ns:ns + n]
        send_sems, recv_sems = refs[ns + n], refs[ns + n + 1]
        for mine_out, _ in _chip_copies(src_refs, land_refs, send_sems, recv_sems):
            mine_out.start()
        refs[-1][...] = jnp.zeros_like(refs[-1])

    sems = pltpu.SemaphoreType.DMA((3 * n,))
    res = pl.pallas_call(
        body, name=name, out_shape=(sems, sems) + tuple(pltpu.HBM(a.shape, a.dtype) for a in ops) + (_sds((8, LANES)),),
        in_specs=[_HBM] * len(ops), out_specs=(_SEM, _SEM) + (_HBM,) * len(ops) + (pl.BlockSpec(memory_space=pltpu.VMEM),),
        input_output_aliases={k: 2 + k for k in range(len(ops))},
        compiler_params=pltpu.CompilerParams(has_side_effects=_EFFECT),
    )(*[pltpu.with_memory_space_constraint(a, pltpu.HBM) for a in ops])
    return res[0], res[1], list(res[2:2 + len(ops)]), res[-1]


def _split_wait(send_sems, recv_sems, thru, n, after, name):
    ns = len(thru) - n

    def body(*refs):
        src_refs, land_refs = refs[:ns], refs[ns:ns + n]
        for mine_out, arriving in _chip_copies(src_refs, land_refs, refs[ns + n], refs[ns + n + 1]):
            mine_out.wait_send()
            arriving.wait_recv()

    res = pl.pallas_call(
        body, name=name, out_shape=tuple(pltpu.HBM(a.shape, a.dtype) for a in thru),
        in_specs=[_HBM] * len(thru) + [_SEM, _SEM, pl.BlockSpec(memory_space=pl.ANY)], out_specs=(_HBM,) * len(thru),
        input_output_aliases={k: k for k in range(len(thru))},
        compiler_params=pltpu.CompilerParams(has_side_effects=_EFFECT),
    )(*thru, send_sems, recv_sems, after)
    return list(res)


def _sibling_swap(arrs, name):
    n = len(arrs)

    def body(*refs):
        ins, outs_, (send_sems, recv_sems) = refs[:n], refs[n:2 * n], refs[2 * n:]
        sib = (lax.axis_index("x"), lax.axis_index("y"), 1 - lax.axis_index("c"))
        cps = [pltpu.make_async_remote_copy(src_ref=ins[i], dst_ref=outs_[i], send_sem=send_sems.at[i], recv_sem=recv_sems.at[i],
                                            device_id=sib, device_id_type=MESH) for i in range(n)]
        for cp in cps:
            cp.start()
        for cp in cps:
            cp.wait_recv()
        for cp in cps:
            cp.wait_send()

    hbm = pl.BlockSpec(memory_space=pltpu.HBM)
    return pl.pallas_call(
        body, name=name, out_shape=tuple(_sds(a.shape, a.dtype) for a in arrs), in_specs=[hbm] * n, out_specs=tuple([hbm] * n),
        scratch_shapes=[pltpu.SemaphoreType.DMA((n,)), pltpu.SemaphoreType.DMA((n,))],
    )(*arrs)


def _tile_spec(rows, cc):
    return pl.BlockSpec((None, rows, cc), lambda l, i: (l, i, 0))


def _cast_bf16(a, rows, name):
    nl, r, cc = a.shape

    def body(a_ref, o_ref):
        o_ref[...] = a_ref[...].astype(jnp.bfloat16)

    return pl.pallas_call(body, name=name, out_shape=_sds((nl, r, cc), jnp.bfloat16), grid=(nl, r // rows),
                          in_specs=[_tile_spec(rows, cc)], out_specs=_tile_spec(rows, cc), compiler_params=_cp())(a)


def _sum_blocks(a, rows, name):
    k, nl, r, cc = a.shape

    def body(a_ref, o_ref):
        acc = a_ref[0].astype(F32)
        for j in range(1, k):
            acc = acc + a_ref[j].astype(F32)
        o_ref[...] = acc

    return pl.pallas_call(body, name=name, out_shape=_sds((nl, r, cc)), grid=(nl, r // rows),
                          in_specs=[pl.BlockSpec((k, None, rows, cc), lambda l, i: (0, l, i, 0))],
                          out_specs=_tile_spec(rows, cc), compiler_params=_cp())(a)


def _sum_chips(lands, srcs, rows, name):
    nl = len(lands)
    _, r, cc = lands[0].shape

    def body(*refs):
        land_refs, src_refs, o_ref = refs[:nl], refs[nl:2 * nl], refs[2 * nl]
        mine = 2 * lax.axis_index("x") + lax.axis_index("y")
        for j in range(nl):
            @pl.when(pl.program_id(0) == j)
            def _(j=j):
                own = src_refs[j][mine].astype(F32)
                acc = None
                for k in range(4):
                    term = jnp.where(mine == k, own, land_refs[j][k].astype(F32))
                    acc = term if acc is None else acc + term
                o_ref[...] = acc

    specs = [pl.BlockSpec((4, rows, cc), lambda l, i, j=j: (0, jnp.where(l == j, i, 0), 0)) for j in range(nl)]
    return pl.pallas_call(body, name=name, out_shape=_sds((nl, r, cc)), grid=(nl, r // rows),
                          in_specs=specs + specs, out_specs=_tile_spec(rows, cc), compiler_params=_cp())(*lands, *srcs)


def _adamw(w, parts, m, v, rows, name):
    nl, r, cc = w.shape
    np_ = len(parts)
    c1 = 1.0 / (1.0 - ADAM_B1 ** ADAM_STEP)
    c2 = 1.0 / (1.0 - ADAM_B2 ** ADAM_STEP)

    def body(*refs):
        w_ref, p_refs, (m_ref, v_ref, g_ref, d_ref, nm_ref, nv_ref) = refs[0], refs[1:1 + np_], refs[1 + np_:]
        g = p_refs[0][...]
        for p_ref in p_refs[1:]:
            g = g + p_ref[...]
        nm = ADAM_B1 * m_ref[...] + (1.0 - ADAM_B1) * g
        nv = ADAM_B2 * v_ref[...] + (1.0 - ADAM_B2) * (g * g)
        g_ref[...] = g
        nm_ref[...] = nm
        nv_ref[...] = nv
        d_ref[...] = -ADAM_LR * ((nm * c1) / (jnp.sqrt(nv * c2) + ADAM_EPS) + ADAM_WD * w_ref[...])

    spec = _tile_spec(rows, cc)
    return pl.pallas_call(body, name=name, out_shape=(_sds((nl, r, cc)),) * 4, grid=(nl, r // rows),
                          in_specs=[spec] * (3 + np_), out_specs=(spec,) * 4, compiler_params=_cp())(w, *parts, m, v)


_BIAS = pltpu.VMEM((2, 2 * BLK, 2 * BLK), F32)


def _fill_band_bias(bias_ref):
    qi = lax.broadcasted_iota(jnp.int32, (2 * BLK, 2 * BLK), 0) & (BLK - 1)
    kj = lax.broadcasted_iota(jnp.int32, (2 * BLK, 2 * BLK), 1)
    dist = BLK + qi - kj
    band = (dist >= 0) & (dist <= BLK)
    bias_ref[0] = jnp.where(band, 0.0, NEG)
    bias_ref[1] = jnp.where(band & (kj >= BLK), 0.0, NEG)


class _HeadStack:
    def __init__(self, group):
        self.m0, self.m1 = _half_masks()
        self.group = group
        if group is not None:
            self.kv_mask = (self.m0, self.m1)[group]

    def _swap_half(self, t, a):
        return t if a == self.group else pltpu.roll(t, HD, axis=1)

    def stack(self, t):
        t0, t1 = t * self.m0, t * self.m1
        if self.group is not None:
            t0, t1 = self._swap_half(t0, 0), self._swap_half(t1, 1)
        return jnp.concatenate([t0, t1], axis=0)

    def unstack(self, ts):
        if self.group is None:
            return ts[:BLK] * self.m0 + ts[BLK:] * self.m1
        return self._swap_half(ts[:BLK] * self.kv_mask, 0) + self._swap_half(ts[BLK:] * self.kv_mask, 1)


def _rows(st, dil):
    if dil == 1:
        return pl.ds(pl.multiple_of(st, BLK), BLK)
    return pl.ds(st, BLK, stride=dil)


def _block_pos(n, dil):
    nb = SEQ // (dil * BLK)
    r, b = n // nb, n % nb
    hp = (b > 0).astype(jnp.int32)
    st = r + dil * BLK * b
    return st, st - dil * BLK * hp, 1 - hp


def _attn_fwd(proj, qblk, kblk, vblk, dils, gqa, sink_x, name):
    has_sink = sink_x is not None

    def body(*refs):
        if has_sink:
            q_ref, k_ref, v_ref, s_ref, o_ref, lse_ref, m_scr, z_scr, bias_scr = refs
        else:
            q_ref, k_ref, v_ref, o_ref, lse_ref, m_scr, z_scr, bias_scr = refs

        @pl.when(pl.program_id(0) == 0)
        def _():
            _fill_band_bias(bias_scr)
        o_ref[...] = jnp.zeros_like(o_ref)
        if has_sink:
            z_scr[...] = jnp.ones_like(z_scr)
            m_scr[...] = jnp.broadcast_to(s_ref[...], m_scr.shape)
        else:
            z_scr[...] = jnp.zeros_like(z_scr)
            m_scr[...] = jnp.full_like(m_scr, NEG)

        def step(n, carry, dil, heads):
            m0, m1 = heads.m0, heads.m1
            st, stp, first = _block_pos(n, dil)
            rq, rp = _rows(st, dil), _rows(stp, dil)
            kk = jnp.concatenate([k_ref[rp, :], k_ref[rq, :]], axis=0)
            vv = jnp.concatenate([v_ref[rp, :], v_ref[rq, :]], axis=0)
            s = _mm(heads.stack(q_ref[rq, :] * QK_SCALE), kk, NT) + bias_scr[first]
            m = jnp.max(s, axis=1, keepdims=True)
            p = jnp.exp(s - m)
            l = jnp.sum(p, axis=1, keepdims=True)
            o_pair = heads.unstack(_mm(p, vv))
            m_pair = m[:BLK] * m0 + m[BLK:] * m1
            l_pair = l[:BLK] * m0 + l[BLK:] * m1
            m_old = m_scr[rq, :]
            m_new = jnp.maximum(m_old, m_pair)
            alpha, beta = jnp.exp(m_old - m_new), jnp.exp(m_pair - m_new)
            o_ref[rq, :] = o_ref[rq, :] * alpha + o_pair * beta
            z_scr[rq, :] = z_scr[rq, :] * alpha + l_pair * beta
            m_scr[rq, :] = m_new
            return carry

        def blocks(heads):
            for dil in dils:
                lax.fori_loop(0, SEQ // BLK, lambda n, carry, dil=dil: step(n, carry, dil, heads), 0, unroll=4)

        if gqa:
            for grp in range(2):
                pl.when(pl.program_id(0) // 2 == grp)(lambda grp=grp: blocks(_HeadStack(grp)))
        else:
            blocks(_HeadStack(None))

        def fin(t, carry):
            rt = pl.ds(pl.multiple_of(t * TM, TM), TM)
            z = z_scr[rt, :]
            o_ref[rt, :] = o_ref[rt, :] / z
            lse_ref[rt, :] = m_scr[rt, :] + jnp.log(z)
            return carry
        lax.fori_loop(0, SEQ // TM, fin, 0)

    col = lambda blk: pl.BlockSpec((SEQ, LANES), lambda p, blk=blk: (0, blk + p))
    kv = (lambda blk: pl.BlockSpec((SEQ, LANES), lambda p, blk=blk: (0, blk))) if gqa else col
    in_specs = [col(qblk), kv(kblk), kv(vblk)]
    args = [proj, proj, proj]
    if has_sink:
        in_specs.append(pl.BlockSpec((1, LANES), lambda p: (0, p)))
        args.append(sink_x)
    out = pl.BlockSpec((SEQ, LANES), lambda p: (0, p))
    return pl.pallas_call(body, name=name, out_shape=(_sds((SEQ, 512)), _sds((SEQ, 512))), grid=(4,),
                          in_specs=in_specs, out_specs=(out, out),
                          scratch_shapes=[pltpu.VMEM((SEQ, LANES), F32), pltpu.VMEM((SEQ, LANES), F32), _BIAS],
                          compiler_params=_cp(48))(*args)


def _attn_bwd(proj, qblk, kblk, vblk, do, o, lse, dils, gqa, sink_x, name):
    has_sink = sink_x is not None

    def body(*refs):
        if has_sink:
            q_ref, k_ref, v_ref, do_ref, o_ref, lse_ref, s_ref, dq_ref, dk_ref, dv_ref, ds_ref, bias_scr = refs
        else:
            q_ref, k_ref, v_ref, do_ref, o_ref, lse_ref, dq_ref, dk_ref, dv_ref, bias_scr = refs
        pid = pl.program_id(0)

        @pl.when(pid == 0)
        def _():
            _fill_band_bias(bias_scr)
        dq_ref[...] = jnp.zeros_like(dq_ref)
        if gqa:
            @pl.when(pid == 0)
            def _():
                dk_ref[...] = jnp.zeros_like(dk_ref)
                dv_ref[...] = jnp.zeros_like(dv_ref)
        else:
            dk_ref[...] = jnp.zeros_like(dk_ref)
            dv_ref[...] = jnp.zeros_like(dv_ref)

        def step(n, carry, dil, heads):
            m0, m1 = heads.m0, heads.m1
            st, stp, first = _block_pos(n, dil)
            rq, rp = _rows(st, dil), _rows(stp, dil)
            do_, lse_ = do_ref[rq, :], lse_ref[rq, :]
            kk = jnp.concatenate([k_ref[rp, :], k_ref[rq, :]], axis=0)
            vv = jnp.concatenate([v_ref[rp, :], v_ref[rq, :]], axis=0)
            qs, dos = heads.stack(q_ref[rq, :] * QK_SCALE), heads.stack(do_)
            doo = do_ * o_ref[rq, :]
            delta = jnp.concatenate([jnp.sum(doo * m0, axis=1, keepdims=True), jnp.sum(doo * m1, axis=1, keepdims=True)], axis=0)
            lse_s = jnp.concatenate([lse_[:, 0:1], lse_[:, HD:HD + 1]], axis=0)
            p = jnp.exp(_mm(qs, kk, NT) + bias_scr[first] - lse_s)
            ds = p * (_mm(dos, vv, NT) - delta)
            dq_ref[rq, :] += heads.unstack(_mm(ds, kk)) * QK_SCALE
            dk_sum, dv_sum = _mm(ds, qs, TN), _mm(p, dos, TN)
            dk_ref[rp, :] += dk_sum[:BLK]
            dk_ref[rq, :] += dk_sum[BLK:]
            dv_ref[rp, :] += dv_sum[:BLK]
            dv_ref[rq, :] += dv_sum[BLK:]
            return carry

        def blocks(heads):
            for dil in dils:
                lax.fori_loop(0, SEQ // BLK, lambda n, carry, dil=dil: step(n, carry, dil, heads), 0, unroll=2)

        if gqa:
            for grp in range(2):
                pl.when(pid // 2 == grp)(lambda grp=grp: blocks(_HeadStack(grp)))
        else:
            blocks(_HeadStack(None))

        if has_sink:
            m0, m1 = _half_masks()

            def sink_rows(t, acc):
                rt = pl.ds(pl.multiple_of(t * TM, TM), TM)
                return acc - jnp.sum(jnp.exp(s_ref[...] - lse_ref[rt, :]) * (do_ref[rt, :] * o_ref[rt, :]), axis=0, keepdims=True)
            acc = lax.fori_loop(0, SEQ // TM, sink_rows, jnp.zeros((1, LANES), F32))
            per_head = jnp.sum(acc * m0, axis=1, keepdims=True) * m0 + jnp.sum(acc * m1, axis=1, keepdims=True) * m1
            ds_ref[0] = jnp.broadcast_to(per_head, (8, LANES))

    col = lambda blk: pl.BlockSpec((SEQ, LANES), lambda p, blk=blk: (0, blk + p))
    kv = (lambda blk: pl.BlockSpec((SEQ, LANES), lambda p, blk=blk: (0, blk))) if gqa else col
    pair = pl.BlockSpec((SEQ, LANES), lambda p: (0, p))
    in_specs = [col(qblk), kv(kblk), kv(vblk), pair, pair, pair]
    args = [proj, proj, proj, do, o, lse]
    kvw = LANES if gqa else 512
    kv_out = pl.BlockSpec((SEQ, LANES), lambda p: (0, 0)) if gqa else pair
    out_shape = [_sds((SEQ, 512)), _sds((SEQ, kvw)), _sds((SEQ, kvw))]
    out_specs = [pair, kv_out, kv_out]
    if has_sink:
        in_specs.append(pl.BlockSpec((1, LANES), lambda p: (0, p)))
        args.append(sink_x)
        out_shape.append(_sds((4, 8, LANES)))
        out_specs.append(pl.BlockSpec((1, 8, LANES), lambda p: (p, 0, 0)))
    return pl.pallas_call(body, name=name, out_shape=tuple(out_shape), grid=(4,), in_specs=in_specs,
                          out_specs=tuple(out_specs), scratch_shapes=[_BIAS], compiler_params=_cp(56))(*args)


def _shift_down(v, k):
    row = lax.broadcasted_iota(jnp.int32, v.shape, 0)
    return jnp.where(row >= k, pltpu.roll(v, k, axis=0), 0.0)


def _shift_up(v, k):
    n = v.shape[0]
    row = lax.broadcasted_iota(jnp.int32, v.shape, 0)
    return jnp.where(row < n - k, pltpu.roll(v, n - k, axis=0), 0.0)


def _conv_pre(x, w_ref, b_ref):
    u = b_ref[...] + x * w_ref[3:4, :]
    for k in range(1, 4):
        u = u + _shift_down(x, k) * w_ref[3 - k:4 - k, :]
    return u


def _conv_fwd(proj, w, b, name):
    def body(x_ref, w_ref, b_ref, o_ref):
        o_ref[...] = _silu(_conv_pre(x_ref[...], w_ref, b_ref))

    nblk = CONV_CH // LANES
    return pl.pallas_call(body, name=name, out_shape=_sds((SEQ, CONV_CH)), grid=(nblk,),
                          in_specs=[pl.BlockSpec((SEQ, LANES), lambda j: (0, XBC // LANES + j)),
                                    pl.BlockSpec((4, LANES), lambda j: (0, j)), pl.BlockSpec((1, LANES), lambda j: (0, j))],
                          out_specs=pl.BlockSpec((SEQ, LANES), lambda j: (0, j)), compiler_params=_cp())(proj, w, b)


def _conv_bwd(proj, dact, w, b, name):
    def body(x_ref, da_ref, w_ref, b_ref, dx_ref, dw_ref, db_ref):
        x = x_ref[...]
        du = da_ref[...] * _dsilu(_conv_pre(x, w_ref, b_ref))
        dx = du * w_ref[3:4, :]
        for k in range(1, 4):
            dx = dx + _shift_up(du, k) * w_ref[3 - k:4 - k, :]
        dx_ref[...] = dx
        db_ref[...] = jnp.sum(du, axis=0, keepdims=True)
        dw_ref[3:4, :] = jnp.sum(du * x, axis=0, keepdims=True)
        for k in range(1, 4):
            dw_ref[3 - k:4 - k, :] = jnp.sum(du * _shift_down(x, k), axis=0, keepdims=True)

    nblk = CONV_CH // LANES
    blk = pl.BlockSpec((SEQ, LANES), lambda j: (0, j))
    wspec, bspec = pl.BlockSpec((4, LANES), lambda j: (0, j)), pl.BlockSpec((1, LANES), lambda j: (0, j))
    return pl.pallas_call(body, name=name, out_shape=(_sds((SEQ, CONV_CH)), _sds((4, CONV_CH)), _sds((1, CONV_CH))), grid=(nblk,),
                          in_specs=[pl.BlockSpec((SEQ, LANES), lambda j: (0, XBC // LANES + j)), blk, wspec, bspec],
                          out_specs=(blk, wspec, bspec), compiler_params=_cp())(proj, dact, w, b)


def _ssd_chunk(xs, bm, cm, dtr, z, hs, al16, dtb, dskx, nw):
    m0, m1 = _half_masks()
    row = lax.broadcasted_iota(jnp.int32, (BLK, BLK), 0)
    col = lax.broadcasted_iota(jnp.int32, (BLK, BLK), 1)
    causal = row >= col
    tril = causal.astype(F32)
    lane = lax.broadcasted_iota(jnp.int32, (1, LANES), 1)
    sub = lax.broadcasted_iota(jnp.int32, (BLK, 1), 0)
    last_row = (sub == BLK - 1).astype(F32)
    dt = jnp.where(lane < 16, _softplus(dtr + dtb), 0.0)
    a16 = -jnp.exp(al16)
    acum = jnp.dot(tril, dt * a16, precision=HI, preferred_element_type=F32)
    acum_t = acum.T
    gmat = [_mm(cm[g], bm[g], NT) for g in range(2)]
    ys, hn = [], []
    for p in range(8):
        g = p // 4
        pick = [(lane == 2 * p + a).astype(F32) for a in range(2)]
        col_h = [jnp.sum(acum * pick[a], axis=1, keepdims=True) for a in range(2)]
        dt_x = sum(jnp.sum(dt * pick[a], axis=1, keepdims=True) * msk for a, msk in enumerate((m0, m1)))
        ac_x = col_h[0] * m0 + col_h[1] * m1
        a_end = jnp.sum(ac_x * last_row, axis=0, keepdims=True)
        xdt = xs[p] * dt_x
        y = _mm(cm[g], hs[p]) * jnp.exp(ac_x)
        for a, msk in enumerate((m0, m1)):
            row_h = jnp.sum(acum_t * (sub == 2 * p + a).astype(F32), axis=0, keepdims=True)
            decay = jnp.exp(jnp.where(causal, col_h[a] - row_h, NEG))
            y = y + _mm(gmat[g] * decay, xdt * msk)
        st = _mm(bm[g], xdt * jnp.exp(a_end - ac_x), TN)
        hn.append(hs[p] * jnp.exp(a_end) + st)
        y = y + dskx[p] * xs[p]
        ys.append(y * _silu(z[p]))
    out = []
    for g in range(2):
        ms = sum(jnp.sum(ys[p] * ys[p], axis=1, keepdims=True) for p in range(4 * g, 4 * g + 4)) * (1.0 / 512)
        rstd = lax.rsqrt(ms + EPS)
        out += [ys[p] * rstd * nw[p] for p in range(4 * g, 4 * g + 4)]
    return out, hn


def _tiles(ref, n, off=0):
    return [ref[:, off + LANES * p:off + LANES * (p + 1)] for p in range(n)]


def _ssd_load(xbc_ref, z_ref, dt_ref, al16_ref, dtb_ref, dsk_ref, nw_ref):
    return (_tiles(xbc_ref, 8), _tiles(xbc_ref, 2, 1024), _tiles(xbc_ref, 2, 1280), dt_ref[...], _tiles(z_ref, 8)), \
           (al16_ref[...], dtb_ref[...], _tiles(dsk_ref, 8), _tiles(nw_ref, 8))


_NCH = SEQ // BLK


def _ssd_param_specs():
    return [_full((1, LANES)), _full((1, LANES)), _full((1, 1024)), _full((1, 1024))]


def _ssd_fwd(xbc_act, proj, al16, dtb, dskx, nw, name):
    def body(xbc_ref, z_ref, dt_ref, al16_ref, dtb_ref, dsk_ref, nw_ref, y_ref, hin_ref, h_scr):
        @pl.when(pl.program_id(0) == 0)
        def _():
            h_scr[...] = jnp.zeros_like(h_scr)
        acts, params = _ssd_load(xbc_ref, z_ref, dt_ref, al16_ref, dtb_ref, dsk_ref, nw_ref)
        hs = _tiles(h_scr, 8)
        hin_ref[0] = h_scr[...]
        ys, hn = _ssd_chunk(*acts, hs, *params)
        for p in range(8):
            y_ref[:, LANES * p:LANES * (p + 1)] = ys[p]
            h_scr[:, LANES * p:LANES * (p + 1)] = hn[p]

    return pl.pallas_call(
        body, name=name, out_shape=(_sds((SEQ, 1024)), _sds((_NCH, BLK, 1024))), grid=(_NCH,),
        in_specs=[pl.BlockSpec((BLK, CONV_CH), lambda c: (c, 0)), pl.BlockSpec((BLK, 1024), lambda c: (c, ZB // 1024)),
                  pl.BlockSpec((BLK, LANES), lambda c: (c, DTC // LANES))] + _ssd_param_specs(),
        out_specs=(pl.BlockSpec((BLK, 1024), lambda c: (c, 0)), pl.BlockSpec((1, BLK, 1024), lambda c: (c, 0, 0))),
        scratch_shapes=[pltpu.VMEM((BLK, 1024), F32)], compiler_params=_cp())(xbc_act, proj, proj, al16, dtb, dskx, nw)


def _ssd_bwd(xbc_act, proj, hin, dyb, al16, dtb, dskx, nw, name):
    def body(xbc_ref, z_ref, dt_ref, hin_ref, dy_ref, al16_ref, dtb_ref, dsk_ref, nw_ref,
             dxbc_ref, dz_ref, ddt_ref, dal16_ref, ddtb_ref, ddsk_ref, dnw_ref, dh_scr):
        @pl.when(pl.program_id(0) == 0)
        def _():
            dh_scr[...] = jnp.zeros_like(dh_scr)
            for r in (dal16_ref, ddtb_ref, ddsk_ref, dnw_ref):
                r[...] = jnp.zeros_like(r)
        acts, params = _ssd_load(xbc_ref, z_ref, dt_ref, al16_ref, dtb_ref, dsk_ref, nw_ref)
        hs = [hin_ref[0, :, LANES * p:LANES * (p + 1)] for p in range(8)]
        _, vjp = jax.vjp(lambda a, h, q: _ssd_chunk(*a, h, *q), acts, hs, params)
        (dxs, dbm, dcm, ddt, dz), dhs, (dal16, ddtb, ddsk, dnw) = vjp((_tiles(dy_ref, 8), _tiles(dh_scr, 8)))
        for p in range(8):
            cols = slice(LANES * p, LANES * (p + 1))
            dxbc_ref[:, cols] = dxs[p]
            dz_ref[:, cols] = dz[p]
            dh_scr[:, cols] = dhs[p]
            ddsk_ref[:, cols] += ddsk[p]
            dnw_ref[:, cols] += dnw[p]
        for g in range(2):
            dxbc_ref[:, 1024 + LANES * g:1024 + LANES * (g + 1)] = dbm[g]
            dxbc_ref[:, 1280 + LANES * g:1280 + LANES * (g + 1)] = dcm[g]
        ddt_ref[...] = ddt
        dal16_ref[...] += dal16
        ddtb_ref[...] += ddtb

    rev = lambda c: _NCH - 1 - c
    return pl.pallas_call(
        body, name=name,
        out_shape=(_sds((SEQ, CONV_CH)), _sds((SEQ, 1024)), _sds((SEQ, LANES)),
                   _sds((1, LANES)), _sds((1, LANES)), _sds((1, 1024)), _sds((1, 1024))),
        grid=(_NCH,),
        in_specs=[pl.BlockSpec((BLK, CONV_CH), lambda c: (rev(c), 0)), pl.BlockSpec((BLK, 1024), lambda c: (rev(c), ZB // 1024)),
                  pl.BlockSpec((BLK, LANES), lambda c: (rev(c), DTC // LANES)), pl.BlockSpec((1, BLK, 1024), lambda c: (rev(c), 0, 0)),
                  pl.BlockSpec((BLK, 1024), lambda c: (rev(c), 0))] + _ssd_param_specs(),
        out_specs=(pl.BlockSpec((BLK, CONV_CH), lambda c: (rev(c), 0)), pl.BlockSpec((BLK, 1024), lambda c: (rev(c), 0)),
                   pl.BlockSpec((BLK, LANES), lambda c: (rev(c), 0)),
                   _full((1, LANES)), _full((1, LANES)), _full((1, 1024)), _full((1, 1024))),
        scratch_shapes=[pltpu.VMEM((BLK, 1024), F32)], compiler_params=_cp())(xbc_act, proj, proj, hin, dyb, al16, dtb, dskx, nw)


def _rstd(v):
    return lax.rsqrt(jnp.mean(v * v, axis=1, keepdims=True) + EPS)


def _rms_bwd(dn, n, rstd):
    return rstd * (dn - n * jnp.mean(dn * n, axis=1, keepdims=True))


_VEC = _full((1, D))


def _layer_spec(layer):
    return pl.BlockSpec((None, 2048, D), lambda *_: (layer, 0, 0))

_ROW = pl.BlockSpec((TM, D), lambda i, *_: (i, 0))


def _proj_fwd(x, pre_w, scale, shift, w, layer, name):
    tn, ni = 1024, SEQ // TM

    def body(x_ref, pw_ref, sc_ref, sh_ref, w_ref, o_ref, h_ref, h_scr):
        rows = pl.ds(pl.multiple_of(pl.program_id(1) * TM, TM), TM)

        @pl.when(pl.program_id(0) == 0)
        def _():
            xv = x_ref[...]
            h = ((xv * _rstd(xv) * pw_ref[...]) * (1.0 + sc_ref[...]) + sh_ref[...]).astype(h_ref.dtype)
            h_scr[rows, :] = h
            h_ref[...] = h
        o_ref[...] = jnp.dot(h_scr[rows, :], w_ref[...].astype(MXU), preferred_element_type=F32)

    first_pass = pl.BlockSpec((TM, D), lambda j, i: (jnp.where(j == 0, i, ni - 1), 0))
    return pl.pallas_call(body, name=name, out_shape=(_sds((SEQ, NP)), _sds((SEQ, D), MXU)), grid=(NP // tn, ni),
                          in_specs=[first_pass, _VEC, _VEC, _VEC, pl.BlockSpec((None, D, tn), lambda j, i: (layer, 0, j))],
                          out_specs=(pl.BlockSpec((TM, tn), lambda j, i: (i, j)), first_pass),
                          scratch_shapes=[pltpu.VMEM((SEQ, D), MXU)], compiler_params=_cp())(x, pre_w, scale, shift, w)


_HALF = pl.BlockSpec((TM, 512), lambda i: (i, 0))
_Z_A = pl.BlockSpec((TM, 512), lambda i: (i, ZA // 512))
_Z_C = pl.BlockSpec((TM, 512), lambda i: (i, ZC // 512))


def _out_fwd(o_a, yb, o_c, proj, w, layer, x, gate, post_w, name):
    def body(oa_ref, yb_ref, oc_ref, za_ref, zc_ref, w_ref, x_ref, g_ref, pw_ref, xn_ref, y_ref):
        y = (_mm(oa_ref[...] * _silu(za_ref[...]), w_ref[0:512, :]) + _mm(yb_ref[...], w_ref[512:1536, :])
             + _mm(oc_ref[...] * _silu(zc_ref[...]), w_ref[1536:2048, :]))
        y_ref[...] = y
        xn_ref[...] = x_ref[...] + g_ref[...] * (y * _rstd(y) * pw_ref[...])

    return pl.pallas_call(body, name=name, out_shape=(_sds((SEQ, D)), _sds((SEQ, D))), grid=(SEQ // TM,),
                          in_specs=[_HALF, _ROW, _HALF, _Z_A, _Z_C, _layer_spec(layer), _ROW, _VEC, _VEC],
                          out_specs=(_ROW, _ROW), compiler_params=_cp())(o_a, yb, o_c, proj, proj, w, x, gate, post_w)


def _post_bwd(dxo, y, gate, post_w, name):
    def body(dx_ref, y_ref, g_ref, pw_ref, dy_ref, dg_ref, dpw_ref):
        @pl.when(pl.program_id(0) == 0)
        def _():
            dg_ref[...] = jnp.zeros_like(dg_ref)
            dpw_ref[...] = jnp.zeros_like(dpw_ref)
        dx, y = dx_ref[...], y_ref[...]
        rstd = _rstd(y)
        n = y * rstd
        dg_ref[...] += jnp.sum(dx * (n * pw_ref[...]), axis=0, keepdims=True)
        dr = dx * g_ref[...]
        dpw_ref[...] += jnp.sum(dr * n, axis=0, keepdims=True)
        dy_ref[...] = _rms_bwd(dr * pw_ref[...], n, rstd)

    return pl.pallas_call(body, name=name, out_shape=(_sds((SEQ, D)), _sds((1, D)), _sds((1, D))), grid=(SEQ // TM,),
                          in_specs=[_ROW, _ROW, _VEC, _VEC], out_specs=(_ROW, _VEC, _VEC), compiler_params=_cp())(dxo, y, gate, post_w)


def _dymix(dy, w, layer, o_a, o_c, proj, name):
    def body(dy_ref, w_ref, oa_ref, oc_ref, za_ref, zc_ref, doa_ref, dza_ref, b_ref, doc_ref, dzc_ref):
        dy = dy_ref[...]
        b_ref[...] = _mm(dy, w_ref[512:1536, :], NT)
        for rows, o_ref, z_ref, do_ref, dz_ref in ((slice(0, 512), oa_ref, za_ref, doa_ref, dza_ref),
                                                   (slice(1536, 2048), oc_ref, zc_ref, doc_ref, dzc_ref)):
            dyg, z = _mm(dy, w_ref[rows, :], NT), z_ref[...]
            do_ref[...] = dyg * _silu(z)
            dz_ref[...] = dyg * o_ref[...] * _dsilu(z)

    return pl.pallas_call(body, name=name, out_shape=(_sds((SEQ, 512)), _sds((SEQ, 512)), _sds((SEQ, D)), _sds((SEQ, 512)), _sds((SEQ, 512))),
                          grid=(SEQ // TM,), in_specs=[_ROW, _layer_spec(layer), _HALF, _HALF, _Z_A, _Z_C],
                          out_specs=(_HALF, _HALF, _ROW, _HALF, _HALF), compiler_params=_cp())(dy, w, o_a, o_c, proj, proj)


def _dwout(o_a, yb, o_c, proj, dy, name):
    def body(oa_ref, yb_ref, oc_ref, za_ref, zc_ref, dy_ref, o_ref):
        @pl.when(pl.program_id(0) == 0)
        def _():
            o_ref[...] = jnp.zeros_like(o_ref)
        dy = dy_ref[...]
        o_ref[0:512, :] += _mm(oa_ref[...] * _silu(za_ref[...]), dy, TN)
        o_ref[512:1536, :] += _mm(yb_ref[...], dy, TN)
        o_ref[1536:2048, :] += _mm(oc_ref[...] * _silu(zc_ref[...]), dy, TN)

    return pl.pallas_call(body, name=name, out_shape=_sds((2048, D)), grid=(SEQ // TM,),
                          in_specs=[_HALF, _ROW, _HALF, _Z_A, _Z_C, _ROW], out_specs=_full((2048, D)),
                          compiler_params=_cp())(o_a, yb, o_c, proj, proj, dy)


def _dwin(h, pieces, name):
    n = len(pieces)
    widths = [p.shape[1] for p in pieces]
    half = NP // 2

    def body(*refs):
        h_ref, p_refs, o_ref = refs[0], refs[1:1 + n], refs[1 + n]

        @pl.when(pl.program_id(0) == 0)
        def _():
            o_ref[...] = jnp.zeros_like(o_ref)
        hv, c0 = h_ref[...], 0
        for p_ref, wd in zip(p_refs, widths):
            o_ref[:, c0:c0 + wd] += _mm(hv, p_ref[...], TN)
            c0 += wd

    return pl.pallas_call(body, name=name, out_shape=_sds((D, half)), grid=(SEQ // TM,),
                          in_specs=[_ROW] + [pl.BlockSpec((TM, wd), lambda k: (k, 0)) for wd in widths],
                          out_specs=_full((D, half)), compiler_params=_cp(56))(h, *pieces)


_TMH = 256


def _dh_bwd(pieces, w, x, pre_w, scale, dxo, name):
    n = len(pieces)
    widths = [p.shape[1] for p in pieces]

    def body(*refs):
        p_refs, (w_ref, x_ref, pw_ref, sc_ref, dxo_ref, dx_ref, dsh_ref, dsc_ref, dpw_ref) = refs[:n], refs[n:]

        @pl.when(pl.program_id(0) == 0)
        def _():
            for r in (dsh_ref, dsc_ref, dpw_ref):
                r[...] = jnp.zeros_like(r)
        dh, c0 = 0.0, 0
        for p_ref, wd in zip(p_refs, widths):
            dh = dh + _mm(p_ref[...], w_ref[:, c0:c0 + wd], NT)
            c0 += wd
        xv = x_ref[...]
        rstd = _rstd(xv)
        nrm = xv * rstd
        dsh_ref[...] += jnp.sum(dh, axis=0, keepdims=True)
        dsc_ref[...] += jnp.sum(dh * (nrm * pw_ref[...]), axis=0, keepdims=True)
        dhn = dh * (1.0 + sc_ref[...])
        dpw_ref[...] += jnp.sum(dhn * nrm, axis=0, keepdims=True)
        dx_ref[...] = _rms_bwd(dhn * pw_ref[...], nrm, rstd) + dxo_ref[...]

    row = pl.BlockSpec((_TMH, D), lambda i: (i, 0))
    return pl.pallas_call(body, name=name, out_shape=(_sds((SEQ, D)), _sds((1, D)), _sds((1, D)), _sds((1, D))),
                          grid=(SEQ // _TMH,),
                          in_specs=[pl.BlockSpec((_TMH, wd), lambda i: (i, 0)) for wd in widths]
                          + [pl.BlockSpec((None, D, NP), lambda i: (0, 0, 0)), row, _VEC, _VEC, row],
                          out_specs=(row, _VEC, _VEC, _VEC), compiler_params=_cp(56))(*pieces, w, x, pre_w, scale, dxo)


def _w_in_padded(land, name):
    rows = 128

    def body(l_ref, o_ref):
        o_ref[...] = _pad_cols(jnp.concatenate([l_ref[k] for k in range(4)], axis=1))

    return pl.pallas_call(body, name=name, out_shape=_sds((D, NP), land.dtype), grid=(D // rows,),
                          in_specs=[pl.BlockSpec((4, rows, SHARD_IN), lambda i: (0, i, 0))],
                          out_specs=pl.BlockSpec((rows, NP), lambda i: (i, 0)), compiler_params=_cp())(land)


def _grad_blocks(dwa, dwb, name):
    rows = 128

    def body(a_ref, b_ref, o_ref):
        g = _unpad_cols(jnp.concatenate([a_ref[...], b_ref[...]], axis=1))
        for k in range(4):
            o_ref[k] = g[:, SHARD_IN * k:SHARD_IN * (k + 1)].astype(o_ref.dtype)

    half = pl.BlockSpec((rows, NP // 2), lambda i: (i, 0))
    return pl.pallas_call(body, name=name, out_shape=_sds((4, D, SHARD_IN), jnp.bfloat16), grid=(D // rows,),
                          in_specs=[half, half], out_specs=pl.BlockSpec((4, rows, SHARD_IN), lambda i: (0, i, 0)),
                          compiler_params=_cp())(dwa, dwb)


def _loss_bwd(xf, tgt, name):
    def body(x_ref, t_ref, dx_ref, l_ref):
        @pl.when(pl.program_id(0) == 0)
        def _():
            l_ref[...] = jnp.zeros_like(l_ref)
        e = x_ref[...] - t_ref[...]
        dx_ref[...] = e * (1.0 / D)
        l_ref[...] += 0.5 * jnp.sum(jnp.mean(e * e, axis=1, keepdims=True), axis=0, keepdims=True)

    return pl.pallas_call(body, name=name, out_shape=(_sds((SEQ, D)), _sds((8, LANES))), grid=(SEQ // TM,),
                          in_specs=[_ROW, _ROW], out_specs=(_ROW, _full((8, LANES))), compiler_params=_cp())(xf, tgt)


def _mod_part(c_all, ada_w, ada_b, name):
    def body(c_ref, w_ref, b_ref, o_ref):
        o_ref[0] = _mm(_silu(c_ref[...]), w_ref[0]) + b_ref[0]

    return pl.pallas_call(body, name=name, out_shape=_sds((DEPTH, 8, 768)), grid=(DEPTH,),
                          in_specs=[_full((8, D)), pl.BlockSpec((1, D, 768), lambda i: (i, 0, 0)), pl.BlockSpec((1, 1, 768), lambda i: (i, 0, 0))],
                          out_specs=pl.BlockSpec((1, 8, 768), lambda i: (i, 0, 0)), compiler_params=_cp())(c_all, ada_w, ada_b)


def _ada_grad(c_t, dmod, name):
    def body(c_ref, d_ref, o_ref):
        ca = _silu(c_ref[...])
        dm = d_ref[0]
        acc = ca[:, 0:1] * dm[0:1, :]
        for s in range(1, 8):
            acc = acc + ca[:, s:s + 1] * dm[s:s + 1, :]
        o_ref[0] = acc

    return pl.pallas_call(body, name=name, out_shape=_sds((DEPTH, D, 768)), grid=(DEPTH,),
                          in_specs=[_full((D, LANES)), pl.BlockSpec((1, 8, 768), lambda i: (i, 0, 0))],
                          out_specs=pl.BlockSpec((1, D, 768), lambda i: (i, 0, 0)), compiler_params=_cp())(c_t, dmod)


def _pack(parts):
    flat = []
    for p in parts:
        f = p.reshape(-1)
        flat.append(jnp.pad(f, (0, (-f.size) % LANES)))
    v = jnp.concatenate(flat)
    return jnp.pad(v, (0, (-v.size) % (8 * LANES))).reshape(-1, LANES)


def _unpack(v, shapes):
    v = v.reshape(-1)
    out, off = [], 0
    for s in shapes:
        n = math.prod(s)
        out.append(v[off:off + n].reshape(s))
        off += n + (-n) % LANES
    return out


_GIVEN_DT, _GIVEN_C = 4608, 4624


def _pad_cols(w):
    return jnp.concatenate([w[..., :_GIVEN_DT], w[..., _GIVEN_C:], w[..., _GIVEN_DT:_GIVEN_C],
                            jnp.zeros(w.shape[:-1] + (NP - IN_COLS,), w.dtype)], axis=-1)


def _unpad_cols(w):
    return jnp.concatenate([w[..., :_GIVEN_DT], w[..., DTC:DTC + 16], w[..., _GIVEN_DT:DTC]], axis=-1)


def _pad_lanes(v):
    return jnp.pad(v, (0, LANES - v.shape[0])).reshape(1, LANES)


def _local_step(x2, tgt, mod, weights_of, grads_done, pre_w, post_w, conv_w, conv_b, dt_bias, a_log, d_skip, nw, sinks):
    saved = []
    xcur = x2
    for i in range(DEPTH):
        shift, scale, gate = mod[i:i + 1, :D], mod[i:i + 1, D:2 * D], mod[i:i + 1, 2 * D:]
        pw, qw = pre_w[i:i + 1], post_w[i:i + 1]
        w_p, w_o = weights_of(i, xcur)
        proj, h = _proj_fwd(xcur, pw, scale, shift, w_p, 0, "proj_fwd")
        o_a, lse_a = _attn_fwd(proj, QA // LANES, KA // LANES, VA // LANES, DILS, False, None, "attn_a_fwd")
        sink_x = jnp.repeat(sinks[i], HD).reshape(1, 512)
        o_c, lse_c = _attn_fwd(proj, QC // LANES, KC // LANES, VC // LANES, (1,), True, sink_x, "attn_c_fwd")
        cw, cb = conv_w[i], conv_b[i:i + 1]
        xbc_act = _conv_fwd(proj, cw, cb, "conv_fwd")
        ssd_p = (_pad_lanes(a_log[i]), _pad_lanes(dt_bias[i]), jnp.repeat(d_skip[i], HD).reshape(1, 1024), nw[i:i + 1])
        yb, hin = _ssd_fwd(xbc_act, proj, *ssd_p, "ssd_fwd")
        xnew, y = _out_fwd(o_a, yb, o_c, proj, w_o, 0, xcur, gate, qw, "out_fwd")
        saved.append((w_p, w_o, xcur, scale, gate, pw, qw, proj, h, o_a, lse_a, sink_x, o_c, lse_c, cw, cb, xbc_act, ssd_p, yb, hin, y))
        xcur = xnew
    dx, ltile = _loss_bwd(xcur, tgt, "loss")
    dmod, small = [None] * DEPTH, [None] * DEPTH
    for i in reversed(range(DEPTH)):
        w_p, w_o, xin, scale, gate, pw, qw, proj, h, o_a, lse_a, sink_x, o_c, lse_c, cw, cb, xbc_act, ssd_p, yb, hin, y = saved[i]
        dy, dgate, dpost = _post_bwd(dx, y, gate, qw, "post_bwd")
        do_a, dz_a, dyb, do_c, dz_c = _dymix(dy, w_o, 0, o_a, o_c, proj, "dymix")
        dwo = _dwout(o_a, yb, o_c, proj, dy, "dwout")
        dq_a, dk_a, dv_a = _attn_bwd(proj, QA // LANES, KA // LANES, VA // LANES, do_a, o_a, lse_a, DILS, False, None, "attn_a_bwd")
        dq_c, dk_c, dv_c, dsk = _attn_bwd(proj, QC // LANES, KC // LANES, VC // LANES, do_c, o_c, lse_c, (1,), True, sink_x, "attn_c_bwd")
        dxbc_act, dz_b, ddt, dal16, ddtb, ddsk, dnw = _ssd_bwd(xbc_act, proj, hin, dyb, *ssd_p, "ssd_bwd")
        dxbc, dcw, dcb = _conv_bwd(proj, dxbc_act, cw, cb, "conv_bwd")
        half_a, half_b = [dq_a, dk_a, dv_a, dz_a, dz_b], [dxbc, dq_c, dz_c, dk_c, dv_c, ddt]
        sent = grads_done(i, _dwin(h, half_a, "dwin_a"), _dwin(h, half_b, "dwin_b"), dwo)
        dx, dshift, dscale, dpre = _dh_bwd(half_a + half_b, w_p, xin, pw, scale + sent[0, 0], dx, "dh_bwd")
        dmod[i] = jnp.concatenate([dshift, dscale, dgate], axis=1)
        small[i] = (dpre, dpost, dcw, dcb, ddtb[0, :16], dal16[0, :16], ddsk.reshape(16, HD).sum(axis=1), dnw, dsk[:, 0, ::HD].reshape(8))
    return ltile, dx, jnp.concatenate(dmod, axis=0), small


_SMALL = ((1, D), (1, D), (4, CONV_CH), (1, CONV_CH), (16,), (16,), (16,), (1, D), (8,))


def kernel(x, c, ada_w, ada_b, pre_norm_w, post_norm_w, w_in, conv_w, conv_b, dt_bias, a_log, d_skip, ssm_norm_w, sinks, w_out, loss_target, m_ada_w, m_ada_b, m_pre_norm_w, m_post_norm_w, m_w_in, m_conv_w, m_conv_b, m_dt_bias, m_a_log, m_d_skip, m_ssm_norm_w, m_sinks, m_w_out, v_ada_w, v_ada_b, v_pre_norm_w, v_post_norm_w, v_w_in, v_conv_w, v_conv_b, v_dt_bias, v_a_log, v_d_skip, v_ssm_norm_w, v_sinks, v_w_out):
    xi, yi, ci = lax.axis_index("x"), lax.axis_index("y"), lax.axis_index("c")
    chip = 2 * xi + yi
    me = 2 * chip + ci

    w_in_b = _cast_bf16(w_in, 512, "cast_w_in")
    w_out_b = _cast_bf16(w_out, 512, "cast_w_out")
    gathers = []
    for i in range(DEPTH):
        lands = [lax.dynamic_update_slice(lax.empty((4,) + a.shape[1:], a.dtype), a[i][None], (chip, 0, 0)) for a in (w_in_b, w_out_b)]
        gathers.append(_split_start(None, lands, f"gather_start{i}"))
    all_started = gathers[0][3] + gathers[1][3] + gathers[2][3] + gathers[3][3]

    def weights_of(i, after):
        send_sems, recv_sems, thru, _ = gathers[i]
        if i == 0:
            after = all_started + mod[:1, :LANES]
        g_in, g_out = _split_wait(send_sems, recv_sems, thru, 2, after, f"gather_wait{i}")
        return _w_in_padded(g_in, "w_in_padded")[None], g_out.reshape(1, 2048, D)

    scatters = [None] * DEPTH

    def grads_done(i, dwa, dwb, dwo):
        blk_in = _grad_blocks(dwa, dwb, "grad_blocks")
        blk_out = _cast_bf16(dwo.reshape(4, 512, D), 512, "cast_dw_out")
        lands = [lax.empty(blk_in.shape, blk_in.dtype), lax.empty(blk_out.shape, blk_out.dtype)]
        scatters[i] = _split_start([blk_in, blk_out], lands, f"scatter_start{i}")
        return scatters[i][3]

    g0 = _allgather8(_pack([c, conv_w]), "gather_c")
    c_all = g0[:, :8, :].reshape(8, D)
    conv_w_full = jnp.concatenate([g0[2 * k, 8:56, :].reshape(DEPTH, 4, CONV_CH // 4) for k in range(4)], axis=-1)

    ada_b_mine = lax.dynamic_slice_in_dim(ada_b, 768 * chip, 768, axis=1).reshape(DEPTH, 1, 768)
    gm = _allgather8(_mod_part(c_all, ada_w, ada_b_mine, "mod_part").reshape(DEPTH * 8, 768), "gather_mod")
    gm = gm.reshape(4, 2, DEPTH, 8, 768)[:, 0]
    mod = lax.dynamic_index_in_dim(gm, me, axis=2, keepdims=False).transpose(1, 0, 2).reshape(DEPTH, 3 * D)

    ltile, dx, dmod, small = _local_step(x[0], loss_target[0], mod, weights_of, grads_done, pre_norm_w, post_norm_w, conv_w_full,
                                         conv_b, dt_bias, a_log, d_skip, ssm_norm_w, sinks)

    packed = _pack([dmod] + [g for layer in small for g in layer] + [ltile[0]])
    gs = _allgather8(packed, "gather_small")
    tot = _sum_blocks(gs[:, None], packed.shape[0], "sum_small")[0]
    parts = _unpack(tot, [(DEPTH, 3 * D)] + list(_SMALL) * DEPTH + [(LANES,)])
    g_ada_b, loss = parts[0], parts[-1][0]
    per_layer = [parts[1 + len(_SMALL) * i:1 + len(_SMALL) * (i + 1)] for i in range(DEPTH)]
    g_pre, g_post, g_cw, g_cb, g_dtb, g_al, g_dsk, g_nw, g_sk = [jnp.stack([per_layer[i][j] for i in range(DEPTH)]) for j in range(len(_SMALL))]
    g_pre, g_post, g_cb, g_nw = g_pre[:, 0], g_post[:, 0], g_cb[:, 0], g_nw[:, 0]
    g_cw = lax.dynamic_slice_in_dim(g_cw, (CONV_CH // 4) * chip, CONV_CH // 4, axis=2)

    dmod_all = gs[:, :(DEPTH * 3 * D) // LANES, :].reshape(8, DEPTH, 3 * D).transpose(1, 0, 2)
    dmod_mine = lax.dynamic_slice_in_dim(dmod_all, 768 * chip, 768, axis=2)
    c_t = jnp.pad(c_all.T, ((0, 0), (0, LANES - 8)))
    g_ada_w = _ada_grad(c_t, dmod_mine, "ada_grad")

    res = {}
    res["ada_w"] = _adamw(ada_w, [g_ada_w], m_ada_w, v_ada_w, 512, "adamw_ada_w")
    names = ["ada_b", "pre_norm_w", "post_norm_w", "conv_w", "conv_b", "dt_bias", "a_log", "d_skip", "ssm_norm_w", "sinks"]
    ws = [ada_b, pre_norm_w, post_norm_w, conv_w, conv_b, dt_bias, a_log, d_skip, ssm_norm_w, sinks]
    gsm = [g_ada_b, g_pre, g_post, g_cw, g_cb, g_dtb, g_al, g_dsk, g_nw, g_sk]
    ms = [m_ada_b, m_pre_norm_w, m_post_norm_w, m_conv_w, m_conv_b, m_dt_bias, m_a_log, m_d_skip, m_ssm_norm_w, m_sinks]
    vs = [v_ada_b, v_pre_norm_w, v_post_norm_w, v_conv_w, v_conv_b, v_dt_bias, v_a_log, v_d_skip, v_ssm_norm_w, v_sinks]
    pw_, pg_, pm_, pv_ = _pack(ws), _pack(gsm), _pack(ms), _pack(vs)
    small_out = _adamw(pw_[None], [pg_[None]], pm_[None], pv_[None], pw_.shape[0], "adamw_small")

    others_done = small_out[1][0, :8] + res["ada_w"][1][0, :8, :LANES]
    landed = [_split_wait(*scatters[i][:3], 2, others_done, f"scatter_wait{i}") for i in range(DEPTH)]
    p_in = _sum_chips([d[2] for d in landed], [d[0] for d in landed], 128, "sum_w_in")
    p_out = _sum_chips([d[3] for d in landed], [d[1] for d in landed], 256, "sum_w_out")
    s_in, s_out = _sibling_swap([p_in, p_out], "swap_partials")
    res["w_in"] = _adamw(w_in, [p_in, s_in], m_w_in, v_w_in, 256, "adamw_w_in")
    res["w_out"] = _adamw(w_out, [p_out, s_out], m_w_out, v_w_out, 512, "adamw_w_out")
    shapes = [w.shape for w in ws]
    for kind in range(4):
        for nm, a in zip(names, _unpack(small_out[kind][0], shapes)):
            res.setdefault(nm, [None] * 4)[kind] = a
    order = ["ada_w", "ada_b", "pre_norm_w", "post_norm_w", "w_in", "conv_w", "conv_b", "dt_bias", "a_log", "d_skip", "ssm_norm_w", "sinks", "w_out"]
    return (loss, dx[None], *[res[n][0] for n in order], *[res[n][1] for n in order], *[res[n][2] for n in order], *[res[n][3] for n in order])
```

```python
import math

import jax
import jax.numpy as jnp
from jax import lax
from jax.experimental import pallas as pl
from jax.experimental.pallas import tpu as pltpu

F32 = jnp.float32
MXU = jnp.bfloat16
HI = lax.Precision.HIGHEST
MESH = pl.DeviceIdType.MESH

SEQ = 4096
D = 1024
DEPTH = 4
HD = 64
QK_SCALE = HD ** -0.5
LANES = 128
BLK = 128
DILS = (1, 4, 16)
NEG = -1e30
EPS = 1e-6
MIB = 1024 * 1024

NP = 6144
QA, KA, VA, ZA = 0, 512, 1024, 1536
ZB, XBC = 2048, 3072
QC, ZC, KC, VC = 4608, 5120, 5632, 5760
DTC = 5888
IN_COLS = 5904
SHARD_IN = IN_COLS // 4
CONV_CH = 1536
TM = 512

ADAM_LR, ADAM_B1, ADAM_B2, ADAM_EPS, ADAM_WD, ADAM_STEP = 0.001, 0.9, 0.999, 1e-08, 0.01, 10

NT = (((1,), (1,)), ((), ()))
TN = (((0,), (0,)), ((), ()))


def _cp(vmem_mib=48):
    return pltpu.CompilerParams(vmem_limit_bytes=vmem_mib * MIB)


def _sds(shape, dtype=F32):
    return jax.ShapeDtypeStruct(shape, dtype)


def _full(shape):
    n = len(shape)
    return pl.BlockSpec(shape, lambda *_: (0,) * n)


def _mm(a, b, dims=None):
    if dims is None:
        return jnp.dot(a.astype(MXU), b.astype(MXU), preferred_element_type=F32)
    return lax.dot_general(a.astype(MXU), b.astype(MXU), dims, preferred_element_type=F32)


def _sigmoid(x):
    return 1.0 / (1.0 + jnp.exp(-x))


def _silu(x):
    return x * _sigmoid(x)


def _dsilu(x):
    s = _sigmoid(x)
    return s * (1.0 + x * (1.0 - s))


def _softplus(x):
    ax = jnp.where(x >= 0, x, -x)
    return jnp.maximum(x, 0.0) + jnp.log1p(jnp.exp(-ax))


def _half_masks():
    lane = lax.broadcasted_iota(jnp.int32, (1, LANES), 1)
    m0 = (lane < HD).astype(F32)
    return m0, 1.0 - m0


def _allgather8(v, name):
    r, cc = v.shape

    def body(v_ref, out_ref, send_sems, recv_sems):
        x, y, c = lax.axis_index("x"), lax.axis_index("y"), lax.axis_index("c")
        me = 4 * x + 2 * y + c
        out_ref[me] = v_ref[...]
        peers = []
        for k in range(1, 8):
            px = 1 - x if k & 4 else x
            py = 1 - y if k & 2 else y
            pc = 1 - c if k & 1 else c
            peers.append((px, py, pc))
        sends = []
        for k, peer in enumerate(peers):
            cp = pltpu.make_async_remote_copy(src_ref=v_ref, dst_ref=out_ref.at[me], send_sem=send_sems.at[k],
                                              recv_sem=recv_sems.at[k], device_id=peer, device_id_type=MESH)
            cp.start()
            sends.append(cp)
        for k, (px, py, pc) in enumerate(peers):
            pltpu.make_async_remote_copy(src_ref=v_ref, dst_ref=out_ref.at[4 * px + 2 * py + pc], send_sem=send_sems.at[k],
                                         recv_sem=recv_sems.at[k], device_id=(px, py, pc), device_id_type=MESH).wait_recv()
        for cp in sends:
            cp.wait_send()

    return pl.pallas_call(
        body, name=name, out_shape=_sds((8, r, cc)),
        in_specs=[pl.BlockSpec(memory_space=pltpu.VMEM)], out_specs=pl.BlockSpec(memory_space=pltpu.VMEM),
        scratch_shapes=[pltpu.SemaphoreType.DMA((7,)), pltpu.SemaphoreType.DMA((7,))],
        compiler_params=_cp(32),
    )(v)


_HBM = pl.BlockSpec(memory_space=pltpu.HBM)
_SEM = pl.BlockSpec(memory_space=pltpu.SEMAPHORE)
_EFFECT = pltpu.SideEffectType.DATAFLOW_SIDE_EFFECTING


def _chip_copies(src_refs, land_refs, send_sems, recv_sems):
    x, y, c = lax.axis_index("x"), lax.axis_index("y"), lax.axis_index("c")
    mine = 2 * x + y
    out = []
    for i, land in enumerate(land_refs):
        for j, (px, py) in enumerate([(1 - x, y), (x, 1 - y), (1 - x, 1 - y)]):
            src = src_refs[i].at[2 * px + py] if src_refs else land.at[mine]
            mk = lambda dst, i=i, j=j, src=src, px=px, py=py: pltpu.make_async_remote_copy(
                src_ref=src, dst_ref=dst, send_sem=send_sems.at[3 * i + j], recv_sem=recv_sems.at[3 * i + j],
                device_id=(px, py, c), device_id_type=MESH)
            out.append((mk(land.at[mine]), mk(land.at[2 * px + py])))
    return out


def _split_start(srcs, lands, name):
    ops = list(srcs or []) + list(lands)
    ns, n = len(srcs or []), len(lands)

    def body(*refs):
        src_refs, land_refs = refs[:ns], refs[ns:ns + n]
        send_sems, recv_sems = refs[ns + n], refs[ns + n + 1]
        for mine_out, _ in _chip_copies(src_refs, land_refs, send_sems, recv_sems):
            mine_out.start()
        refs[-1][...] = jnp.zeros_like(refs[-1])

    sems = pltpu.SemaphoreType.DMA((3 * n,))
    res = pl.pallas_call(
        body, name=name, out_shape=(sems, sems) + tuple(pltpu.HBM(a.shape, a.dtype) for a in ops) + (_sds((8, LANES)),),
        in_specs=[_HBM] * len(ops), out_specs=(_SEM, _SEM) + (_HBM,) * len(ops) + (pl.BlockSpec(memory_space=pltpu.VMEM),),
        input_output_aliases={k: 2 + k for k in range(len(ops))},
        compiler_params=pltpu.CompilerParams(has_side_effects=_EFFECT),
    )(*[pltpu.with_memory_space_constraint(a, pltpu.HBM) for a in ops])
    return res[0], res[1], list(res[2:2 + len(ops)]), res[-1]


def _split_wait(send_sems, recv_sems, thru, n, after, name):
    ns = len(thru) - n

    def body(*refs):
        src_refs, land_refs = refs[:ns], refs[ns:ns + n]
        for mine_out, arriving in _chip_copies(src_refs, land_refs, refs[ns + n], refs[ns + n + 1]):
            mine_out.wait_send()
            arriving.wait_recv()

    res = pl.pallas_call(
        body, name=name, out_shape=tuple(pltpu.HBM(a.shape, a.dtype) for a in thru),
        in_specs=[_HBM] * len(thru) + [_SEM, _SEM, pl.BlockSpec(memory_space=pl.ANY)], out_specs=(_HBM,) * len(thru),
        input_output_aliases={k: k for k in range(len(thru))},
        compiler_params=pltpu.CompilerParams(has_side_effects=_EFFECT),
    )(*thru, send_sems, recv_sems, after)
    return list(res)


def _sibling_swap(arrs, name):
    n = len(arrs)

    def body(*refs):
        ins, outs_, (send_sems, recv_sems) = refs[:n], refs[n:2 * n], refs[2 * n:]
        sib = (lax.axis_index("x"), lax.axis_index("y"), 1 - lax.axis_index("c"))
        cps = [pltpu.make_async_remote_copy(src_ref=ins[i], dst_ref=outs_[i], send_sem=send_sems.at[i], recv_sem=recv_sems.at[i],
                                            device_id=sib, device_id_type=MESH) for i in range(n)]
        for cp in cps:
            cp.start()
        for cp in cps:
            cp.wait_recv()
        for cp in cps:
            cp.wait_send()

    hbm = pl.BlockSpec(memory_space=pltpu.HBM)
    return pl.pallas_call(
        body, name=name, out_shape=tuple(_sds(a.shape, a.dtype) for a in arrs), in_specs=[hbm] * n, out_specs=tuple([hbm] * n),
        scratch_shapes=[pltpu.SemaphoreType.DMA((n,)), pltpu.SemaphoreType.DMA((n,))],
    )(*arrs)


def _tile_spec(rows, cc):
    return pl.BlockSpec((None, rows, cc), lambda l, i: (l, i, 0))


def _cast_bf16(a, rows, name):
    nl, r, cc = a.shape

    def body(a_ref, o_ref):
        o_ref[...] = a_ref[...].astype(jnp.bfloat16)

    return pl.pallas_call(body, name=name, out_shape=_sds((nl, r, cc), jnp.bfloat16), grid=(nl, r // rows),
                          in_specs=[_tile_spec(rows, cc)], out_specs=_tile_spec(rows, cc), compiler_params=_cp())(a)


def _sum_blocks(a, rows, name):
    k, nl, r, cc = a.shape

    def body(a_ref, o_ref):
        acc = a_ref[0].astype(F32)
        for j in range(1, k):
            acc = acc + a_ref[j].astype(F32)
        o_ref[...] = acc

    return pl.pallas_call(body, name=name, out_shape=_sds((nl, r, cc)), grid=(nl, r // rows),
                          in_specs=[pl.BlockSpec((k, None, rows, cc), lambda l, i: (0, l, i, 0))],
                          out_specs=_tile_spec(rows, cc), compiler_params=_cp())(a)


def _sum_chips(lands, srcs, rows, name):
    nl = len(lands)
    _, r, cc = lands[0].shape

    def body(*refs):
        land_refs, src_refs, o_ref = refs[:nl], refs[nl:2 * nl], refs[2 * nl]
        mine = 2 * lax.axis_index("x") + lax.axis_index("y")
        for j in range(nl):
            @pl.when(pl.program_id(0) == j)
            def _(j=j):
                own = src_refs[j][mine].astype(F32)
                acc = None
                for k in range(4):
                    term = jnp.where(mine == k, own, land_refs[j][k].astype(F32))
                    acc = term if acc is None else acc + term
                o_ref[...] = acc

    specs = [pl.BlockSpec((4, rows, cc), lambda l, i, j=j: (0, jnp.where(l == j, i, 0), 0)) for j in range(nl)]
    return pl.pallas_call(body, name=name, out_shape=_sds((nl, r, cc)), grid=(nl, r // rows),
                          in_specs=specs + specs, out_specs=_tile_spec(rows, cc), compiler_params=_cp())(*lands, *srcs)


def _adamw(w, parts, m, v, rows, name):
    nl, r, cc = w.shape
    np_ = len(parts)
    c1 = 1.0 / (1.0 - ADAM_B1 ** ADAM_STEP)
    c2 = 1.0 / (1.0 - ADAM_B2 ** ADAM_STEP)

    def body(*refs):
        w_ref, p_refs, (m_ref, v_ref, g_ref, d_ref, nm_ref, nv_ref) = refs[0], refs[1:1 + np_], refs[1 + np_:]
        g = p_refs[0][...]
        for p_ref in p_refs[1:]:
            g = g + p_ref[...]
        nm = ADAM_B1 * m_ref[...] + (1.0 - ADAM_B1) * g
        nv = ADAM_B2 * v_ref[...] + (1.0 - ADAM_B2) * (g * g)
        g_ref[...] = g
        nm_ref[...] = nm
        nv_ref[...] = nv
        d_ref[...] = -ADAM_LR * ((nm * c1) / (jnp.sqrt(nv * c2) + ADAM_EPS) + ADAM_WD * w_ref[...])

    spec = _tile_spec(rows, cc)
    return pl.pallas_call(body, name=name, out_shape=(_sds((nl, r, cc)),) * 4, grid=(nl, r // rows),
                          in_specs=[spec] * (3 + np_), out_specs=(spec,) * 4, compiler_params=_cp())(w, *parts, m, v)


_BIAS = pltpu.VMEM((2, 2 * BLK, 2 * BLK), F32)


def _fill_band_bias(bias_ref):
    qi = lax.broadcasted_iota(jnp.int32, (2 * BLK, 2 * BLK), 0) & (BLK - 1)
    kj = lax.broadcasted_iota(jnp.int32, (2 * BLK, 2 * BLK), 1)
    dist = BLK + qi - kj
    band = (dist >= 0) & (dist <= BLK)
    bias_ref[0] = jnp.where(band, 0.0, NEG)
    bias_ref[1] = jnp.where(band & (kj >= BLK), 0.0, NEG)


class _HeadStack:
    def __init__(self, group):
        self.m0, self.m1 = _half_masks()
        self.group = group
        if group is not None:
            self.kv_mask = (self.m0, self.m1)[group]

    def _swap_half(self, t, a):
        return t if a == self.group else pltpu.roll(t, HD, axis=1)

    def stack(self, t):
        t0, t1 = t * self.m0, t * self.m1
        if self.group is not None:
            t0, t1 = self._swap_half(t0, 0), self._swap_half(t1, 1)
        return jnp.concatenate([t0, t1], axis=0)

    def unstack(self, ts):
        if self.group is None:
            return ts[:BLK] * self.m0 + ts[BLK:] * self.m1
        return self._swap_half(ts[:BLK] * self.kv_mask, 0) + self._swap_half(ts[BLK:] * self.kv_mask, 1)


def _rows(st, dil):
    if dil == 1:
        return pl.ds(pl.multiple_of(st, BLK), BLK)
    return pl.ds(st, BLK, stride=dil)


def _block_pos(n, dil):
    nb = SEQ // (dil * BLK)
    r, b = n // nb, n % nb
    hp = (b > 0).astype(jnp.int32)
    st = r + dil * BLK * b
    return st, st - dil * BLK * hp, 1 - hp


def _attn_fwd(proj, qblk, kblk, vblk, dils, gqa, sink_x, name):
    has_sink = sink_x is not None

    def body(*refs):
        if has_sink:
            q_ref, k_ref, v_ref, s_ref, o_ref, lse_ref, m_scr, z_scr, bias_scr = refs
        else:
            q_ref, k_ref, v_ref, o_ref, lse_ref, m_scr, z_scr, bias_scr = refs

        @pl.when(pl.program_id(0) == 0)
        def _():
            _fill_band_bias(bias_scr)
        o_ref[...] = jnp.zeros_like(o_ref)
        if has_sink:
            z_scr[...] = jnp.ones_like(z_scr)
            m_scr[...] = jnp.broadcast_to(s_ref[...], m_scr.shape)
        else:
            z_scr[...] = jnp.zeros_like(z_scr)
            m_scr[...] = jnp.full_like(m_scr, NEG)

        def step(n, carry, dil, heads):
            m0, m1 = heads.m0, heads.m1
            st, stp, first = _block_pos(n, dil)
            rq, rp = _rows(st, dil), _rows(stp, dil)
            kk = jnp.concatenate([k_ref[rp, :], k_ref[rq, :]], axis=0)
            vv = jnp.concatenate([v_ref[rp, :], v_ref[rq, :]], axis=0)
            s = _mm(heads.stack(q_ref[rq, :] * QK_SCALE), kk, NT) + bias_scr[first]
            m = jnp.max(s, axis=1, keepdims=True)
            p = jnp.exp(s - m)
            l = jnp.sum(p, axis=1, keepdims=True)
            o_pair = heads.unstack(_mm(p, vv))
            m_pair = m[:BLK] * m0 + m[BLK:] * m1
            l_pair = l[:BLK] * m0 + l[BLK:] * m1
            m_old = m_scr[rq, :]
            m_new = jnp.maximum(m_old, m_pair)
            alpha, beta = jnp.exp(m_old - m_new), jnp.exp(m_pair - m_new)
            o_ref[rq, :] = o_ref[rq, :] * alpha + o_pair * beta
            z_scr[rq, :] = z_scr[rq, :] * alpha + l_pair * beta
            m_scr[rq, :] = m_new
            return carry

        def blocks(heads):
            for dil in dils:
                lax.fori_loop(0, SEQ // BLK, lambda n, carry, dil=dil: step(n, carry, dil, heads), 0, unroll=8)

        if gqa:
            for grp in range(2):
                pl.when(pl.program_id(0) // 2 == grp)(lambda grp=grp: blocks(_HeadStack(grp)))
        else:
            blocks(_HeadStack(None))

        def fin(t, carry):
            rt = pl.ds(pl.multiple_of(t * TM, TM), TM)
            z = z_scr[rt, :]
            o_ref[rt, :] = o_ref[rt, :] / z
            lse_ref[rt, :] = m_scr[rt, :] + jnp.log(z)
            return carry
        lax.fori_loop(0, SEQ // TM, fin, 0)

    col = lambda blk: pl.BlockSpec((SEQ, LANES), lambda p, blk=blk: (0, blk + p))
    kv = (lambda blk: pl.BlockSpec((SEQ, LANES), lambda p, blk=blk: (0, blk))) if gqa else col
    in_specs = [col(qblk), kv(kblk), kv(vblk)]
    args = [proj, proj, proj]
    if has_sink:
        in_specs.append(pl.BlockSpec((1, LANES), lambda p: (0, p)))
        args.append(sink_x)
    out = pl.BlockSpec((SEQ, LANES), lambda p: (0, p))
    return pl.pallas_call(body, name=name, out_shape=(_sds((SEQ, 512)), _sds((SEQ, 512))), grid=(4,),
                          in_specs=in_specs, out_specs=(out, out),
                          scratch_shapes=[pltpu.VMEM((SEQ, LANES), F32), pltpu.VMEM((SEQ, LANES), F32), _BIAS],
                          compiler_params=_cp(48))(*args)


def _attn_bwd(proj, qblk, kblk, vblk, do, o, lse, dils, gqa, sink_x, name):
    has_sink = sink_x is not None

    def body(*refs):
        if has_sink:
            q_ref, k_ref, v_ref, do_ref, o_ref, lse_ref, s_ref, dq_ref, dk_ref, dv_ref, ds_ref, bias_scr = refs
        else:
            q_ref, k_ref, v_ref, do_ref, o_ref, lse_ref, dq_ref, dk_ref, dv_ref, bias_scr = refs
        pid = pl.program_id(0)

        @pl.when(pid == 0)
        def _():
            _fill_band_bias(bias_scr)
        dq_ref[...] = jnp.zeros_like(dq_ref)
        if gqa:
            @pl.when(pid == 0)
            def _():
                dk_ref[...] = jnp.zeros_like(dk_ref)
                dv_ref[...] = jnp.zeros_like(dv_ref)
        else:
            dk_ref[...] = jnp.zeros_like(dk_ref)
            dv_ref[...] = jnp.zeros_like(dv_ref)

        def step(n, carry, dil, heads):
            m0, m1 = heads.m0, heads.m1
            st, stp, first = _block_pos(n, dil)
            rq, rp = _rows(st, dil), _rows(stp, dil)
            do_, lse_ = do_ref[rq, :], lse_ref[rq, :]
            kk = jnp.concatenate([k_ref[rp, :], k_ref[rq, :]], axis=0)
            vv = jnp.concatenate([v_ref[rp, :], v_ref[rq, :]], axis=0)
            qs, dos = heads.stack(q_ref[rq, :] * QK_SCALE), heads.stack(do_)
            doo = do_ * o_ref[rq, :]
            delta = jnp.concatenate([jnp.sum(doo * m0, axis=1, keepdims=True), jnp.sum(doo * m1, axis=1, keepdims=True)], axis=0)
            lse_s = jnp.concatenate([lse_[:, 0:1], lse_[:, HD:HD + 1]], axis=0)
            p = jnp.exp(_mm(qs, kk, NT) + bias_scr[first] - lse_s)
            ds = p * (_mm(dos, vv, NT) - delta)
            dq_ref[rq, :] += heads.unstack(_mm(ds, kk)) * QK_SCALE
            dk_sum, dv_sum = _mm(ds, qs, TN), _mm(p, dos, TN)
            dk_ref[rp, :] += dk_sum[:BLK]
            dk_ref[rq, :] += dk_sum[BLK:]
            dv_ref[rp, :] += dv_sum[:BLK]
            dv_ref[rq, :] += dv_sum[BLK:]
            return carry

        def blocks(heads):
            for dil in dils:
                lax.fori_loop(0, SEQ // BLK, lambda n, carry, dil=dil: step(n, carry, dil, heads), 0, unroll=4)

        if gqa:
            for grp in range(2):
                pl.when(pid // 2 == grp)(lambda grp=grp: blocks(_HeadStack(grp)))
        else:
            blocks(_HeadStack(None))

        if has_sink:
            m0, m1 = _half_masks()

            def sink_rows(t, acc):
                rt = pl.ds(pl.multiple_of(t * TM, TM), TM)
                return acc - jnp.sum(jnp.exp(s_ref[...] - lse_ref[rt, :]) * (do_ref[rt, :] * o_ref[rt, :]), axis=0, keepdims=True)
            acc = lax.fori_loop(0, SEQ // TM, sink_rows, jnp.zeros((1, LANES), F32))
            per_head = jnp.sum(acc * m0, axis=1, keepdims=True) * m0 + jnp.sum(acc * m1, axis=1, keepdims=True) * m1
            ds_ref[0] = jnp.broadcast_to(per_head, (8, LANES))

    col = lambda blk: pl.BlockSpec((SEQ, LANES), lambda p, blk=blk: (0, blk + p))
    kv = (lambda blk: pl.BlockSpec((SEQ, LANES), lambda p, blk=blk: (0, blk))) if gqa else col
    pair = pl.BlockSpec((SEQ, LANES), lambda p: (0, p))
    in_specs = [col(qblk), kv(kblk), kv(vblk), pair, pair, pair]
    args = [proj, proj, proj, do, o, lse]
    kvw = LANES if gqa else 512
    kv_out = pl.BlockSpec((SEQ, LANES), lambda p: (0, 0)) if gqa else pair
    out_shape = [_sds((SEQ, 512)), _sds((SEQ, kvw)), _sds((SEQ, kvw))]
    out_specs = [pair, kv_out, kv_out]
    if has_sink:
        in_specs.append(pl.BlockSpec((1, LANES), lambda p: (0, p)))
        args.append(sink_x)
        out_shape.append(_sds((4, 8, LANES)))
        out_specs.append(pl.BlockSpec((1, 8, LANES), lambda p: (p, 0, 0)))
    return pl.pallas_call(body, name=name, out_shape=tuple(out_shape), grid=(4,), in_specs=in_specs,
                          out_specs=tuple(out_specs), scratch_shapes=[_BIAS], compiler_params=_cp(56))(*args)


def _shift_down(v, k):
    row = lax.broadcasted_iota(jnp.int32, v.shape, 0)
    return jnp.where(row >= k, pltpu.roll(v, k, axis=0), 0.0)


def _shift_up(v, k):
    n = v.shape[0]
    row = lax.broadcasted_iota(jnp.int32, v.shape, 0)
    return jnp.where(row < n - k, pltpu.roll(v, n - k, axis=0), 0.0)


def _conv_pre(x, w_ref, b_ref):
    u = b_ref[...] + x * w_ref[3:4, :]
    for k in range(1, 4):
        u = u + _shift_down(x, k) * w_ref[3 - k:4 - k, :]
    return u


def _conv_fwd(proj, w, b, name):
    def body(x_ref, w_ref, b_ref, o_ref):
        o_ref[...] = _silu(_conv_pre(x_ref[...], w_ref, b_ref))

    nblk = CONV_CH // LANES
    return pl.pallas_call(body, name=name, out_shape=_sds((SEQ, CONV_CH)), grid=(nblk,),
                          in_specs=[pl.BlockSpec((SEQ, LANES), lambda j: (0, XBC // LANES + j)),
                                    pl.BlockSpec((4, LANES), lambda j: (0, j)), pl.BlockSpec((1, LANES), lambda j: (0, j))],
                          out_specs=pl.BlockSpec((SEQ, LANES), lambda j: (0, j)), compiler_params=_cp())(proj, w, b)


def _conv_bwd(proj, dact, w, b, name):
    def body(x_ref, da_ref, w_ref, b_ref, dx_ref, dw_ref, db_ref):
        x = x_ref[...]
        du = da_ref[...] * _dsilu(_conv_pre(x, w_ref, b_ref))
        dx = du * w_ref[3:4, :]
        for k in range(1, 4):
            dx = dx + _shift_up(du, k) * w_ref[3 - k:4 - k, :]
        dx_ref[...] = dx
        db_ref[...] = jnp.sum(du, axis=0, keepdims=True)
        dw_ref[3:4, :] = jnp.sum(du * x, axis=0, keepdims=True)
        for k in range(1, 4):
            dw_ref[3 - k:4 - k, :] = jnp.sum(du * _shift_down(x, k), axis=0, keepdims=True)

    nblk = CONV_CH // LANES
    blk = pl.BlockSpec((SEQ, LANES), lambda j: (0, j))
    wspec, bspec = pl.BlockSpec((4, LANES), lambda j: (0, j)), pl.BlockSpec((1, LANES), lambda j: (0, j))
    return pl.pallas_call(body, name=name, out_shape=(_sds((SEQ, CONV_CH)), _sds((4, CONV_CH)), _sds((1, CONV_CH))), grid=(nblk,),
                          in_specs=[pl.BlockSpec((SEQ, LANES), lambda j: (0, XBC // LANES + j)), blk, wspec, bspec],
                          out_specs=(blk, wspec, bspec), compiler_params=_cp())(proj, dact, w, b)


def _ssd_chunk(xs, bm, cm, dtr, z, hs, al16, dtb, dskx, nw):
    m0, m1 = _half_masks()
    row = lax.broadcasted_iota(jnp.int32, (BLK, BLK), 0)
    col = lax.broadcasted_iota(jnp.int32, (BLK, BLK), 1)
    causal = row >= col
    tril = causal.astype(F32)
    lane = lax.broadcasted_iota(jnp.int32, (1, LANES), 1)
    sub = lax.broadcasted_iota(jnp.int32, (BLK, 1), 0)
    last_row = (sub == BLK - 1).astype(F32)
    dt = jnp.where(lane < 16, _softplus(dtr + dtb), 0.0)
    a16 = -jnp.exp(al16)
    acum = jnp.dot(tril, dt * a16, precision=HI, preferred_element_type=F32)
    acum_t = acum.T
    gmat = [_mm(cm[g], bm[g], NT) for g in range(2)]
    ys, hn = [], []
    for p in range(8):
        g = p // 4
        pick = [(lane == 2 * p + a).astype(F32) for a in range(2)]
        col_h = [jnp.sum(acum * pick[a], axis=1, keepdims=True) for a in range(2)]
        dt_x = sum(jnp.sum(dt * pick[a], axis=1, keepdims=True) * msk for a, msk in enumerate((m0, m1)))
        ac_x = col_h[0] * m0 + col_h[1] * m1
        a_end = jnp.sum(ac_x * last_row, axis=0, keepdims=True)
        xdt = xs[p] * dt_x
        y = _mm(cm[g], hs[p]) * jnp.exp(ac_x)
        for a, msk in enumerate((m0, m1)):
            row_h = jnp.sum(acum_t * (sub == 2 * p + a).astype(F32), axis=0, keepdims=True)
            decay = jnp.exp(jnp.where(causal, col_h[a] - row_h, NEG))
            y = y + _mm(gmat[g] * decay, xdt * msk)
        st = _mm(bm[g], xdt * jnp.exp(a_end - ac_x), TN)
        hn.append(hs[p] * jnp.exp(a_end) + st)
        y = y + dskx[p] * xs[p]
        ys.append(y * _silu(z[p]))
    out = []
    for g in range(2):
        ms = sum(jnp.sum(ys[p] * ys[p], axis=1, keepdims=True) for p in range(4 * g, 4 * g + 4)) * (1.0 / 512)
        rstd = lax.rsqrt(ms + EPS)
        out += [ys[p] * rstd * nw[p] for p in range(4 * g, 4 * g + 4)]
    return out, hn


def _tiles(ref, n, off=0):
    return [ref[:, off + LANES * p:off + LANES * (p + 1)] for p in range(n)]


def _ssd_load(xbc_ref, z_ref, dt_ref, al16_ref, dtb_ref, dsk_ref, nw_ref):
    return (_tiles(xbc_ref, 8), _tiles(xbc_ref, 2, 1024), _tiles(xbc_ref, 2, 1280), dt_ref[...], _tiles(z_ref, 8)), \
           (al16_ref[...], dtb_ref[...], _tiles(dsk_ref, 8), _tiles(nw_ref, 8))


_NCH = SEQ // BLK


def _ssd_param_specs():
    return [_full((1, LANES)), _full((1, LANES)), _full((1, 1024)), _full((1, 1024))]


def _ssd_fwd(xbc_act, proj, al16, dtb, dskx, nw, name):
    def body(xbc_ref, z_ref, dt_ref, al16_ref, dtb_ref, dsk_ref, nw_ref, y_ref, hin_ref, h_scr):
        @pl.when(pl.program_id(0) == 0)
        def _():
            h_scr[...] = jnp.zeros_like(h_scr)
        acts, params = _ssd_load(xbc_ref, z_ref, dt_ref, al16_ref, dtb_ref, dsk_ref, nw_ref)
        hs = _tiles(h_scr, 8)
        hin_ref[0] = h_scr[...]
        ys, hn = _ssd_chunk(*acts, hs, *params)
        for p in range(8):
            y_ref[:, LANES * p:LANES * (p + 1)] = ys[p]
            h_scr[:, LANES * p:LANES * (p + 1)] = hn[p]

    return pl.pallas_call(
        body, name=name, out_shape=(_sds((SEQ, 1024)), _sds((_NCH, BLK, 1024))), grid=(_NCH,),
        in_specs=[pl.BlockSpec((BLK, CONV_CH), lambda c: (c, 0)), pl.BlockSpec((BLK, 1024), lambda c: (c, ZB // 1024)),
                  pl.BlockSpec((BLK, LANES), lambda c: (c, DTC // LANES))] + _ssd_param_specs(),
        out_specs=(pl.BlockSpec((BLK, 1024), lambda c: (c, 0)), pl.BlockSpec((1, BLK, 1024), lambda c: (c, 0, 0))),
        scratch_shapes=[pltpu.VMEM((BLK, 1024), F32)], compiler_params=_cp())(xbc_act, proj, proj, al16, dtb, dskx, nw)


def _ssd_bwd(xbc_act, proj, hin, dyb, al16, dtb, dskx, nw, name):
    def body(xbc_ref, z_ref, dt_ref, hin_ref, dy_ref, al16_ref, dtb_ref, dsk_ref, nw_ref,
             dxbc_ref, dz_ref, ddt_ref, dal16_ref, ddtb_ref, ddsk_ref, dnw_ref, dh_scr):
        @pl.when(pl.program_id(0) == 0)
        def _():
            dh_scr[...] = jnp.zeros_like(dh_scr)
            for r in (dal16_ref, ddtb_ref, ddsk_ref, dnw_ref):
                r[...] = jnp.zeros_like(r)
        acts, params = _ssd_load(xbc_ref, z_ref, dt_ref, al16_ref, dtb_ref, dsk_ref, nw_ref)
        hs = [hin_ref[0, :, LANES * p:LANES * (p + 1)] for p in range(8)]
        _, vjp = jax.vjp(lambda a, h, q: _ssd_chunk(*a, h, *q), acts, hs, params)
        (dxs, dbm, dcm, ddt, dz), dhs, (dal16, ddtb, ddsk, dnw) = vjp((_tiles(dy_ref, 8), _tiles(dh_scr, 8)))
        for p in range(8):
            cols = slice(LANES * p, LANES * (p + 1))
            dxbc_ref[:, cols] = dxs[p]
            dz_ref[:, cols] = dz[p]
            dh_scr[:, cols] = dhs[p]
            ddsk_ref[:, cols] += ddsk[p]
            dnw_ref[:, cols] += dnw[p]
        for g in range(2):
            dxbc_ref[:, 1024 + LANES * g:1024 + LANES * (g + 1)] = dbm[g]
            dxbc_ref[:, 1280 + LANES * g:1280 + LANES * (g + 1)] = dcm[g]
        ddt_ref[...] = ddt
        dal16_ref[...] += dal16
        ddtb_ref[...] += ddtb

    rev = lambda c: _NCH - 1 - c
    return pl.pallas_call(
        body, name=name,
        out_shape=(_sds((SEQ, CONV_CH)), _sds((SEQ, 1024)), _sds((SEQ, LANES)),
                   _sds((1, LANES)), _sds((1, LANES)), _sds((1, 1024)), _sds((1, 1024))),
        grid=(_NCH,),
        in_specs=[pl.BlockSpec((BLK, CONV_CH), lambda c: (rev(c), 0)), pl.BlockSpec((BLK, 1024), lambda c: (rev(c), ZB // 1024)),
                  pl.BlockSpec((BLK, LANES), lambda c: (rev(c), DTC // LANES)), pl.BlockSpec((1, BLK, 1024), lambda c: (rev(c), 0, 0)),
                  pl.BlockSpec((BLK, 1024), lambda c: (rev(c), 0))] + _ssd_param_specs(),
        out_specs=(pl.BlockSpec((BLK, CONV_CH), lambda c: (rev(c), 0)), pl.BlockSpec((BLK, 1024), lambda c: (rev(c), 0)),
                   pl.BlockSpec((BLK, LANES), lambda c: (rev(c), 0)),
                   _full((1, LANES)), _full((1, LANES)), _full((1, 1024)), _full((1, 1024))),
        scratch_shapes=[pltpu.VMEM((BLK, 1024), F32)], compiler_params=_cp())(xbc_act, proj, proj, hin, dyb, al16, dtb, dskx, nw)


def _rstd(v):
    return lax.rsqrt(jnp.mean(v * v, axis=1, keepdims=True) + EPS)


def _rms_bwd(dn, n, rstd):
    return rstd * (dn - n * jnp.mean(dn * n, axis=1, keepdims=True))


_VEC = _full((1, D))


def _layer_spec(layer):
    return pl.BlockSpec((None, 2048, D), lambda *_: (layer, 0, 0))

_ROW = pl.BlockSpec((TM, D), lambda i, *_: (i, 0))


def _proj_fwd(x, pre_w, scale, shift, w, layer, name):
    tn, ni = 1024, SEQ // TM

    def body(x_ref, pw_ref, sc_ref, sh_ref, w_ref, o_ref, h_ref, h_scr):
        rows = pl.ds(pl.multiple_of(pl.program_id(1) * TM, TM), TM)

        @pl.when(pl.program_id(0) == 0)
        def _():
            xv = x_ref[...]
            h = ((xv * _rstd(xv) * pw_ref[...]) * (1.0 + sc_ref[...]) + sh_ref[...]).astype(h_ref.dtype)
            h_scr[rows, :] = h
            h_ref[...] = h
        o_ref[...] = jnp.dot(h_scr[rows, :], w_ref[...].astype(MXU), preferred_element_type=F32)

    first_pass = pl.BlockSpec((TM, D), lambda j, i: (jnp.where(j == 0, i, ni - 1), 0))
    return pl.pallas_call(body, name=name, out_shape=(_sds((SEQ, NP)), _sds((SEQ, D), MXU)), grid=(NP // tn, ni),
                          in_specs=[first_pass, _VEC, _VEC, _VEC, pl.BlockSpec((None, D, tn), lambda j, i: (layer, 0, j))],
                          out_specs=(pl.BlockSpec((TM, tn), lambda j, i: (i, j)), first_pass),
                          scratch_shapes=[pltpu.VMEM((SEQ, D), MXU)], compiler_params=_cp())(x, pre_w, scale, shift, w)


_HALF = pl.BlockSpec((TM, 512), lambda i: (i, 0))
_Z_A = pl.BlockSpec((TM, 512), lambda i: (i, ZA // 512))
_Z_C = pl.BlockSpec((TM, 512), lambda i: (i, ZC // 512))


def _out_fwd(o_a, yb, o_c, proj, w, layer, x, gate, post_w, name):
    def body(oa_ref, yb_ref, oc_ref, za_ref, zc_ref, w_ref, x_ref, g_ref, pw_ref, xn_ref, y_ref):
        y = (_mm(oa_ref[...] * _silu(za_ref[...]), w_ref[0:512, :]) + _mm(yb_ref[...], w_ref[512:1536, :])
             + _mm(oc_ref[...] * _silu(zc_ref[...]), w_ref[1536:2048, :]))
        y_ref[...] = y
        xn_ref[...] = x_ref[...] + g_ref[...] * (y * _rstd(y) * pw_ref[...])

    return pl.pallas_call(body, name=name, out_shape=(_sds((SEQ, D)), _sds((SEQ, D))), grid=(SEQ // TM,),
                          in_specs=[_HALF, _ROW, _HALF, _Z_A, _Z_C, _layer_spec(layer), _ROW, _VEC, _VEC],
                          out_specs=(_ROW, _ROW), compiler_params=_cp())(o_a, yb, o_c, proj, proj, w, x, gate, post_w)


def _post_bwd(dxo, y, gate, post_w, name):
    def body(dx_ref, y_ref, g_ref, pw_ref, dy_ref, dg_ref, dpw_ref):
        @pl.when(pl.program_id(0) == 0)
        def _():
            dg_ref[...] = jnp.zeros_like(dg_ref)
            dpw_ref[...] = jnp.zeros_like(dpw_ref)
        dx, y = dx_ref[...], y_ref[...]
        rstd = _rstd(y)
        n = y * rstd
        dg_ref[...] += jnp.sum(dx * (n * pw_ref[...]), axis=0, keepdims=True)
        dr = dx * g_ref[...]
        dpw_ref[...] += jnp.sum(dr * n, axis=0, keepdims=True)
        dy_ref[...] = _rms_bwd(dr * pw_ref[...], n, rstd)

    return pl.pallas_call(body, name=name, out_shape=(_sds((SEQ, D)), _sds((1, D)), _sds((1, D))), grid=(SEQ // TM,),
                          in_specs=[_ROW, _ROW, _VEC, _VEC], out_specs=(_ROW, _VEC, _VEC), compiler_params=_cp())(dxo, y, gate, post_w)


def _dymix(dy, w, layer, o_a, o_c, proj, name):
    def body(dy_ref, w_ref, oa_ref, oc_ref, za_ref, zc_ref, doa_ref, dza_ref, b_ref, doc_ref, dzc_ref):
        dy = dy_ref[...]
        b_ref[...] = _mm(dy, w_ref[512:1536, :], NT)
        for rows, o_ref, z_ref, do_ref, dz_ref in ((slice(0, 512), oa_ref, za_ref, doa_ref, dza_ref),
                                                   (slice(1536, 2048), oc_ref, zc_ref, doc_ref, dzc_ref)):
            dyg, z = _mm(dy, w_ref[rows, :], NT), z_ref[...]
            do_ref[...] = dyg * _silu(z)
            dz_ref[...] = dyg * o_ref[...] * _dsilu(z)

    return pl.pallas_call(body, name=name, out_shape=(_sds((SEQ, 512)), _sds((SEQ, 512)), _sds((SEQ, D)), _sds((SEQ, 512)), _sds((SEQ, 512))),
                          grid=(SEQ // TM,), in_specs=[_ROW, _layer_spec(layer), _HALF, _HALF, _Z_A, _Z_C],
                          out_specs=(_HALF, _HALF, _ROW, _HALF, _HALF), compiler_params=_cp())(dy, w, o_a, o_c, proj, proj)


def _dwout(o_a, yb, o_c, proj, dy, name):
    def body(oa_ref, yb_ref, oc_ref, za_ref, zc_ref, dy_ref, o_ref):
        @pl.when(pl.program_id(0) == 0)
        def _():
            o_ref[...] = jnp.zeros_like(o_ref)
        dy = dy_ref[...]
        o_ref[0:512, :] += _mm(oa_ref[...] * _silu(za_ref[...]), dy, TN)
        o_ref[512:1536, :] += _mm(yb_ref[...], dy, TN)
        o_ref[1536:2048, :] += _mm(oc_ref[...] * _silu(zc_ref[...]), dy, TN)

    return pl.pallas_call(body, name=name, out_shape=_sds((2048, D)), grid=(SEQ // TM,),
                          in_specs=[_HALF, _ROW, _HALF, _Z_A, _Z_C, _ROW], out_specs=_full((2048, D)),
                          compiler_params=_cp())(o_a, yb, o_c, proj, proj, dy)


def _dwin(h, pieces, name):
    n = len(pieces)
    widths = [p.shape[1] for p in pieces]
    half = NP // 2

    def body(*refs):
        h_ref, p_refs, o_ref = refs[0], refs[1:1 + n], refs[1 + n]

        @pl.when(pl.program_id(0) == 0)
        def _():
            o_ref[...] = jnp.zeros_like(o_ref)
        hv, c0 = h_ref[...], 0
        for p_ref, wd in zip(p_refs, widths):
            o_ref[:, c0:c0 + wd] += _mm(hv, p_ref[...], TN)
            c0 += wd

    return pl.pallas_call(body, name=name, out_shape=_sds((D, half)), grid=(SEQ // TM,),
                          in_specs=[_ROW] + [pl.BlockSpec((TM, wd), lambda k: (k, 0)) for wd in widths],
                          out_specs=_full((D, half)), compiler_params=_cp(56))(h, *pieces)


_TMH = 256


def _dh_bwd(pieces, w, x, pre_w, scale, dxo, name):
    n = len(pieces)
    widths = [p.shape[1] for p in pieces]

    def body(*refs):
        p_refs, (w_ref, x_ref, pw_ref, sc_ref, dxo_ref, dx_ref, dsh_ref, dsc_ref, dpw_ref) = refs[:n], refs[n:]

        @pl.when(pl.program_id(0) == 0)
        def _():
            for r in (dsh_ref, dsc_ref, dpw_ref):
                r[...] = jnp.zeros_like(r)
        dh, c0 = 0.0, 0
        for p_ref, wd in zip(p_refs, widths):
            dh = dh + _mm(p_ref[...], w_ref[:, c0:c0 + wd], NT)
            c0 += wd
        xv = x_ref[...]
        rstd = _rstd(xv)
        nrm = xv * rstd
        dsh_ref[...] += jnp.sum(dh, axis=0, keepdims=True)
        dsc_ref[...] += jnp.sum(dh * (nrm * pw_ref[...]), axis=0, keepdims=True)
        dhn = dh * (1.0 + sc_ref[...])
        dpw_ref[...] += jnp.sum(dhn * nrm, axis=0, keepdims=True)
        dx_ref[...] = _rms_bwd(dhn * pw_ref[...], nrm, rstd) + dxo_ref[...]

    row = pl.BlockSpec((_TMH, D), lambda i: (i, 0))
    return pl.pallas_call(body, name=name, out_shape=(_sds((SEQ, D)), _sds((1, D)), _sds((1, D)), _sds((1, D))),
                          grid=(SEQ // _TMH,),
                          in_specs=[pl.BlockSpec((_TMH, wd), lambda i: (i, 0)) for wd in widths]
                          + [pl.BlockSpec((None, D, NP), lambda i: (0, 0, 0)), row, _VEC, _VEC, row],
                          out_specs=(row, _VEC, _VEC, _VEC), compiler_params=_cp(56))(*pieces, w, x, pre_w, scale, dxo)


def _w_in_padded(land, name):
    rows = 128

    def body(l_ref, o_ref):
        o_ref[...] = _pad_cols(jnp.concatenate([l_ref[k] for k in range(4)], axis=1))

    return pl.pallas_call(body, name=name, out_shape=_sds((D, NP), land.dtype), grid=(D // rows,),
                          in_specs=[pl.BlockSpec((4, rows, SHARD_IN), lambda i: (0, i, 0))],
                          out_specs=pl.BlockSpec((rows, NP), lambda i: (i, 0)), compiler_params=_cp())(land)


def _grad_blocks(dwa, dwb, name):
    rows = 128

    def body(a_ref, b_ref, o_ref):
        g = _unpad_cols(jnp.concatenate([a_ref[...], b_ref[...]], axis=1))
        for k in range(4):
            o_ref[k] = g[:, SHARD_IN * k:SHARD_IN * (k + 1)].astype(o_ref.dtype)

    half = pl.BlockSpec((rows, NP // 2), lambda i: (i, 0))
    return pl.pallas_call(body, name=name, out_shape=_sds((4, D, SHARD_IN), jnp.bfloat16), grid=(D // rows,),
                          in_specs=[half, half], out_specs=pl.BlockSpec((4, rows, SHARD_IN), lambda i: (0, i, 0)),
                          compiler_params=_cp())(dwa, dwb)


def _loss_bwd(xf, tgt, name):
    def body(x_ref, t_ref, dx_ref, l_ref):
        @pl.when(pl.program_id(0) == 0)
        def _():
            l_ref[...] = jnp.zeros_like(l_ref)
        e = x_ref[...] - t_ref[...]
        dx_ref[...] = e * (1.0 / D)
        l_ref[...] += 0.5 * jnp.sum(jnp.mean(e * e, axis=1, keepdims=True), axis=0, keepdims=True)

    return pl.pallas_call(body, name=name, out_shape=(_sds((SEQ, D)), _sds((8, LANES))), grid=(SEQ // TM,),
                          in_specs=[_ROW, _ROW], out_specs=(_ROW, _full((8, LANES))), compiler_params=_cp())(xf, tgt)


def _mod_part(c_all, ada_w, ada_b, name):
    def body(c_ref, w_ref, b_ref, o_ref):
        o_ref[0] = _mm(_silu(c_ref[...]), w_ref[0]) + b_ref[0]

    return pl.pallas_call(body, name=name, out_shape=_sds((DEPTH, 8, 768)), grid=(DEPTH,),
                          in_specs=[_full((8, D)), pl.BlockSpec((1, D, 768), lambda i: (i, 0, 0)), pl.BlockSpec((1, 1, 768), lambda i: (i, 0, 0))],
                          out_specs=pl.BlockSpec((1, 8, 768), lambda i: (i, 0, 0)), compiler_params=_cp())(c_all, ada_w, ada_b)


def _ada_grad(c_t, dmod, name):
    def body(c_ref, d_ref, o_ref):
        ca = _silu(c_ref[...])
        dm = d_ref[0]
        acc = ca[:, 0:1] * dm[0:1, :]
        for s in range(1, 8):
            acc = acc + ca[:, s:s + 1] * dm[s:s + 1, :]
        o_ref[0] = acc

    return pl.pallas_call(body, name=name, out_shape=_sds((DEPTH, D, 768)), grid=(DEPTH,),
                          in_specs=[_full((D, LANES)), pl.BlockSpec((1, 8, 768), lambda i: (i, 0, 0))],
                          out_specs=pl.BlockSpec((1, D, 768), lambda i: (i, 0, 0)), compiler_params=_cp())(c_t, dmod)


def _pack(parts):
    flat = []
    for p in parts:
        f = p.reshape(-1)
        flat.append(jnp.pad(f, (0, (-f.size) % LANES)))
    v = jnp.concatenate(flat)
    return jnp.pad(v, (0, (-v.size) % (8 * LANES))).reshape(-1, LANES)


def _unpack(v, shapes):
    v = v.reshape(-1)
    out, off = [], 0
    for s in shapes:
        n = math.prod(s)
        out.append(v[off:off + n].reshape(s))
        off += n + (-n) % LANES
    return out


_GIVEN_DT, _GIVEN_C = 4608, 4624


def _pad_cols(w):
    return jnp.concatenate([w[..., :_GIVEN_DT], w[..., _GIVEN_C:], w[..., _GIVEN_DT:_GIVEN_C],
                            jnp.zeros(w.shape[:-1] + (NP - IN_COLS,), w.dtype)], axis=-1)


def _unpad_cols(w):
    return jnp.concatenate([w[..., :_GIVEN_DT], w[..., DTC:DTC + 16], w[..., _GIVEN_DT:DTC]], axis=-1)


def _pad_lanes(v):
    return jnp.pad(v, (0, LANES - v.shape[0])).reshape(1, LANES)


def _local_step(x2, tgt, mod, weights_of, grads_done, pre_w, post_w, conv_w, conv_b, dt_bias, a_log, d_skip, nw, sinks):
    saved = []
    xcur = x2
    for i in range(DEPTH):
        shift, scale, gate = mod[i:i + 1, :D], mod[i:i + 1, D:2 * D], mod[i:i + 1, 2 * D:]
        pw, qw = pre_w[i:i + 1], post_w[i:i + 1]
        w_p, w_o = weights_of(i, xcur)
        proj, h = _proj_fwd(xcur, pw, scale, shift, w_p, 0, "proj_fwd")
        o_a, lse_a = _attn_fwd(proj, QA // LANES, KA // LANES, VA // LANES, DILS, False, None, "attn_a_fwd")
        sink_x = jnp.repeat(sinks[i], HD).reshape(1, 512)
        o_c, lse_c = _attn_fwd(proj, QC // LANES, KC // LANES, VC // LANES, (1,), True, sink_x, "attn_c_fwd")
        cw, cb = conv_w[i], conv_b[i:i + 1]
        xbc_act = _conv_fwd(proj, cw, cb, "conv_fwd")
        ssd_p = (_pad_lanes(a_log[i]), _pad_lanes(dt_bias[i]), jnp.repeat(d_skip[i], HD).reshape(1, 1024), nw[i:i + 1])
        yb, hin = _ssd_fwd(xbc_act, proj, *ssd_p, "ssd_fwd")
        xnew, y = _out_fwd(o_a, yb, o_c, proj, w_o, 0, xcur, gate, qw, "out_fwd")
        saved.append((w_p, w_o, xcur, scale, gate, pw, qw, proj, h, o_a, lse_a, sink_x, o_c, lse_c, cw, cb, xbc_act, ssd_p, yb, hin, y))
        xcur = xnew
    dx, ltile = _loss_bwd(xcur, tgt, "loss")
    dmod, small = [None] * DEPTH, [None] * DEPTH
    for i in reversed(range(DEPTH)):
        w_p, w_o, xin, scale, gate, pw, qw, proj, h, o_a, lse_a, sink_x, o_c, lse_c, cw, cb, xbc_act, ssd_p, yb, hin, y = saved[i]
        dy, dgate, dpost = _post_bwd(dx, y, gate, qw, "post_bwd")
        do_a, dz_a, dyb, do_c, dz_c = _dymix(dy, w_o, 0, o_a, o_c, proj, "dymix")
        dwo = _dwout(o_a, yb, o_c, proj, dy, "dwout")
        dq_a, dk_a, dv_a = _attn_bwd(proj, QA // LANES, KA // LANES, VA // LANES, do_a, o_a, lse_a, DILS, False, None, "attn_a_bwd")
        dq_c, dk_c, dv_c, dsk = _attn_bwd(proj, QC // LANES, KC // LANES, VC // LANES, do_c, o_c, lse_c, (1,), True, sink_x, "attn_c_bwd")
        dxbc_act, dz_b, ddt, dal16, ddtb, ddsk, dnw = _ssd_bwd(xbc_act, proj, hin, dyb, *ssd_p, "ssd_bwd")
        dxbc, dcw, dcb = _conv_bwd(proj, dxbc_act, cw, cb, "conv_bwd")
        half_a, half_b = [dq_a, dk_a, dv_a, dz_a, dz_b], [dxbc, dq_c, dz_c, dk_c, dv_c, ddt]
        sent = grads_done(i, _dwin(h, half_a, "dwin_a"), _dwin(h, half_b, "dwin_b"), dwo)
        dx, dshift, dscale, dpre = _dh_bwd(half_a + half_b, w_p, xin, pw, scale + sent[0, 0], dx, "dh_bwd")
        dmod[i] = jnp.concatenate([dshift, dscale, dgate], axis=1)
        small[i] = (dpre, dpost, dcw, dcb, ddtb[0, :16], dal16[0, :16], ddsk.reshape(16, HD).sum(axis=1), dnw, dsk[:, 0, ::HD].reshape(8))
    return ltile, dx, jnp.concatenate(dmod, axis=0), small


_SMALL = ((1, D), (1, D), (4, CONV_CH), (1, CONV_CH), (16,), (16,), (16,), (1, D), (8,))


def kernel(x, c, ada_w, ada_b, pre_norm_w, post_norm_w, w_in, conv_w, conv_b, dt_bias, a_log, d_skip, ssm_norm_w, sinks, w_out, loss_target, m_ada_w, m_ada_b, m_pre_norm_w, m_post_norm_w, m_w_in, m_conv_w, m_conv_b, m_dt_bias, m_a_log, m_d_skip, m_ssm_norm_w, m_sinks, m_w_out, v_ada_w, v_ada_b, v_pre_norm_w, v_post_norm_w, v_w_in, v_conv_w, v_conv_b, v_dt_bias, v_a_log, v_d_skip, v_ssm_norm_w, v_sinks, v_w_out):
    xi, yi, ci = lax.axis_index("x"), lax.axis_index("y"), lax.axis_index("c")
    chip = 2 * xi + yi
    me = 2 * chip + ci

    w_in_b = _cast_bf16(w_in, 512, "cast_w_in")
    w_out_b = _cast_bf16(w_out, 512, "cast_w_out")
    gathers = []
    for i in range(DEPTH):
        lands = [lax.dynamic_update_slice(lax.empty((4,) + a.shape[1:], a.dtype), a[i][None], (chip, 0, 0)) for a in (w_in_b, w_out_b)]
        gathers.append(_split_start(None, lands, f"gather_start{i}"))
    all_started = gathers[0][3] + gathers[1][3] + gathers[2][3] + gathers[3][3]

    def weights_of(i, after):
        send_sems, recv_sems, thru, _ = gathers[i]
        if i == 0:
            after = all_started + mod[:1, :LANES]
        g_in, g_out = _split_wait(send_sems, recv_sems, thru, 2, after, f"gather_wait{i}")
        return _w_in_padded(g_in, "w_in_padded")[None], g_out.reshape(1, 2048, D)

    scatters = [None] * DEPTH

    def grads_done(i, dwa, dwb, dwo):
        blocks = [_grad_blocks(dwa, dwb, "grad_blocks"), _cast_bf16(dwo.reshape(4, 512, D), 512, "cast_dw_out")]
        if i == 0:
            scatters[0] = blocks
            return jnp.zeros((8, LANES), F32)
        scatters[i] = _split_start(blocks, [lax.empty(b.shape, b.dtype) for b in blocks], f"scatter_start{i}")
        return scatters[i][3]

    g0 = _allgather8(_pack([c, conv_w]), "gather_c")
    c_all = g0[:, :8, :].reshape(8, D)
    conv_w_full = jnp.concatenate([g0[2 * k, 8:56, :].reshape(DEPTH, 4, CONV_CH // 4) for k in range(4)], axis=-1)

    ada_b_mine = lax.dynamic_slice_in_dim(ada_b, 768 * chip, 768, axis=1).reshape(DEPTH, 1, 768)
    gm = _allgather8(_mod_part(c_all, ada_w, ada_b_mine, "mod_part").reshape(DEPTH * 8, 768), "gather_mod")
    gm = gm.reshape(4, 2, DEPTH, 8, 768)[:, 0]
    mod = lax.dynamic_index_in_dim(gm, me, axis=2, keepdims=False).transpose(1, 0, 2).reshape(DEPTH, 3 * D)

    ltile, dx, dmod, small = _local_step(x[0], loss_target[0], mod, weights_of, grads_done, pre_norm_w, post_norm_w, conv_w_full,
                                         conv_b, dt_bias, a_log, d_skip, ssm_norm_w, sinks)

    packed = _pack([dmod] + [g for layer in small for g in layer] + [ltile[0]])
    gs = _allgather8(packed, "gather_small")
    gathered = (gs[0, 0, 0] * 0.0).astype(jnp.bfloat16)
    scatters[0] = _split_start(scatters[0], [jnp.full(b.shape, gathered) for b in scatters[0]], "scatter_start0")
    gs = gs + scatters[0][3][0, 0]
    tot = _sum_blocks(gs[:, None], packed.shape[0], "sum_small")[0]
    parts = _unpack(tot, [(DEPTH, 3 * D)] + list(_SMALL) * DEPTH + [(LANES,)])
    g_ada_b, loss = parts[0], parts[-1][0]
    per_layer = [parts[1 + len(_SMALL) * i:1 + len(_SMALL) * (i + 1)] for i in range(DEPTH)]
    g_pre, g_post, g_cw, g_cb, g_dtb, g_al, g_dsk, g_nw, g_sk = [jnp.stack([per_layer[i][j] for i in range(DEPTH)]) for j in range(len(_SMALL))]
    g_pre, g_post, g_cb, g_nw = g_pre[:, 0], g_post[:, 0], g_cb[:, 0], g_nw[:, 0]
    g_cw = lax.dynamic_slice_in_dim(g_cw, (CONV_CH // 4) * chip, CONV_CH // 4, axis=2)

    dmod_all = gs[:, :(DEPTH * 3 * D) // LANES, :].reshape(8, DEPTH, 3 * D).transpose(1, 0, 2)
    dmod_mine = lax.dynamic_slice_in_dim(dmod_all, 768 * chip, 768, axis=2)
    c_t = jnp.pad(c_all.T, ((0, 0), (0, LANES - 8)))
    g_ada_w = _ada_grad(c_t, dmod_mine, "ada_grad")

    res = {}
    res["ada_w"] = _adamw(ada_w, [g_ada_w], m_ada_w, v_ada_w, 512, "adamw_ada_w")
    names = ["ada_b", "pre_norm_w", "post_norm_w", "conv_w", "conv_b", "dt_bias", "a_log", "d_skip", "ssm_norm_w", "sinks"]
    ws = [ada_b, pre_norm_w, post_norm_w, conv_w, conv_b, dt_bias, a_log, d_skip, ssm_norm_w, sinks]
    gsm = [g_ada_b, g_pre, g_post, g_cw, g_cb, g_dtb, g_al, g_dsk, g_nw, g_sk]
    ms = [m_ada_b, m_pre_norm_w, m_post_norm_w, m_conv_w, m_conv_b, m_dt_bias, m_a_log, m_d_skip, m_ssm_norm_w, m_sinks]
    vs = [v_ada_b, v_pre_norm_w, v_post_norm_w, v_conv_w, v_conv_b, v_dt_bias, v_a_log, v_d_skip, v_ssm_norm_w, v_sinks]
    pw_, pg_, pm_, pv_ = _pack(ws), _pack(gsm), _pack(ms), _pack(vs)
    small_out = _adamw(pw_[None], [pg_[None]], pm_[None], pv_[None], pw_.shape[0], "adamw_small")

    others_done = small_out[1][0, :8] + res["ada_w"][1][0, :8, :LANES]
    landed = [_split_wait(*scatters[i][:3], 2, others_done, f"scatter_wait{i}") for i in range(DEPTH)]
    p_in = _sum_chips([d[2] for d in landed], [d[0] for d in landed], 128, "sum_w_in")
    p_out = _sum_chips([d[3] for d in landed], [d[1] for d in landed], 256, "sum_w_out")
    s_in, s_out = _sibling_swap([p_in, p_out], "swap_partials")
    res["w_in"] = _adamw(w_in, [p_in, s_in], m_w_in, v_w_in, 256, "adamw_w_in")
    res["w_out"] = _adamw(w_out, [p_out, s_out], m_w_out, v_w_out, 512, "adamw_w_out")
    shapes = [w.shape for w in ws]
    for kind in range(4):
        for nm, a in zip(names, _unpack(small_out[kind][0], shapes)):
            res.setdefault(nm, [None] * 4)[kind] = a
    order = ["ada_w", "ada_b", "pre_norm_w", "post_norm_w", "w_in", "conv_w", "conv_b", "dt_bias", "a_log", "d_skip", "ssm_norm_w", "sinks", "w_out"]
    return (loss, dx[None], *[res[n][0] for n in order], *[res[n][1] for n in order], *[res[n][2] for n in order], *[res[n][3] for n in order])
```

```python
import math

import jax
import jax.numpy as jnp
from jax import lax
from jax.experimental import pallas as pl
from jax.experimental.pallas import tpu as pltpu

F32 = jnp.float32
MXU = jnp.bfloat16
HI = lax.Precision.HIGHEST
MESH = pl.DeviceIdType.MESH

SEQ = 4096
D = 1024
DEPTH = 4
HD = 64
QK_SCALE = HD ** -0.5
LANES = 128
BLK = 128
DILS = (1, 4, 16)
NEG = -1e30
EPS = 1e-6
MIB = 1024 * 1024

NP = 6144
QA, KA, VA, ZA = 0, 512, 1024, 1536
ZB, XBC = 2048, 3072
QC, ZC, KC, VC = 4608, 5120, 5632, 5760
DTC = 5888
IN_COLS = 5904
SHARD_IN = IN_COLS // 4
CONV_CH = 1536
TM = 512

ADAM_LR, ADAM_B1, ADAM_B2, ADAM_EPS, ADAM_WD, ADAM_STEP = 0.001, 0.9, 0.999, 1e-08, 0.01, 10

NT = (((1,), (1,)), ((), ()))
TN = (((0,), (0,)), ((), ()))


def _cp(vmem_mib=48):
    return pltpu.CompilerParams(vmem_limit_bytes=vmem_mib * MIB)


def _sds(shape, dtype=F32):
    return jax.ShapeDtypeStruct(shape, dtype)


def _full(shape):
    n = len(shape)
    return pl.BlockSpec(shape, lambda *_: (0,) * n)


def _mm(a, b, dims=None):
    if dims is None:
        return jnp.dot(a.astype(MXU), b.astype(MXU), preferred_element_type=F32)
    return lax.dot_general(a.astype(MXU), b.astype(MXU), dims, preferred_element_type=F32)


def _sigmoid(x):
    return 1.0 / (1.0 + jnp.exp(-x))


def _silu(x):
    return x * _sigmoid(x)


def _dsilu(x):
    s = _sigmoid(x)
    return s * (1.0 + x * (1.0 - s))


def _softplus(x):
    ax = jnp.where(x >= 0, x, -x)
    return jnp.maximum(x, 0.0) + jnp.log1p(jnp.exp(-ax))


def _half_masks():
    lane = lax.broadcasted_iota(jnp.int32, (1, LANES), 1)
    m0 = (lane < HD).astype(F32)
    return m0, 1.0 - m0


def _allgather8(v, name):
    r, cc = v.shape

    def body(v_ref, out_ref, send_sems, recv_sems):
        x, y, c = lax.axis_index("x"), lax.axis_index("y"), lax.axis_index("c")
        me = 4 * x + 2 * y + c
        out_ref[me] = v_ref[...]
        peers = []
        for k in range(1, 8):
            px = 1 - x if k & 4 else x
            py = 1 - y if k & 2 else y
            pc = 1 - c if k & 1 else c
            peers.append((px, py, pc))
        sends = []
        for k, peer in enumerate(peers):
            cp = pltpu.make_async_remote_copy(src_ref=v_ref, dst_ref=out_ref.at[me], send_sem=send_sems.at[k],
                                              recv_sem=recv_sems.at[k], device_id=peer, device_id_type=MESH)
            cp.start()
            sends.append(cp)
        for k, (px, py, pc) in enumerate(peers):
            pltpu.make_async_remote_copy(src_ref=v_ref, dst_ref=out_ref.at[4 * px + 2 * py + pc], send_sem=send_sems.at[k],
                                         recv_sem=recv_sems.at[k], device_id=(px, py, pc), device_id_type=MESH).wait_recv()
        for cp in sends:
            cp.wait_send()

    return pl.pallas_call(
        body, name=name, out_shape=_sds((8, r, cc)),
        in_specs=[pl.BlockSpec(memory_space=pltpu.VMEM)], out_specs=pl.BlockSpec(memory_space=pltpu.VMEM),
        scratch_shapes=[pltpu.SemaphoreType.DMA((7,)), pltpu.SemaphoreType.DMA((7,))],
        compiler_params=_cp(32),
    )(v)


_HBM = pl.BlockSpec(memory_space=pltpu.HBM)
_SEM = pl.BlockSpec(memory_space=pltpu.SEMAPHORE)
_EFFECT = pltpu.SideEffectType.DATAFLOW_SIDE_EFFECTING


def _chip_copies(src_refs, land_refs, send_sems, recv_sems):
    x, y, c = lax.axis_index("x"), lax.axis_index("y"), lax.axis_index("c")
    mine = 2 * x + y
    out = []
    for i, land in enumerate(land_refs):
        for j, (px, py) in enumerate([(1 - x, y), (x, 1 - y), (1 - x, 1 - y)]):
            src = src_refs[i].at[2 * px + py] if src_refs else land.at[mine]
            mk = lambda dst, i=i, j=j, src=src, px=px, py=py: pltpu.make_async_remote_copy(
                src_ref=src, dst_ref=dst, send_sem=send_sems.at[3 * i + j], recv_sem=recv_sems.at[3 * i + j],
                device_id=(px, py, c), device_id_type=MESH)
            out.append((mk(land.at[mine]), mk(land.at[2 * px + py])))
    return out


def _split_start(srcs, lands, name):
    ops = list(srcs or []) + list(lands)
    ns, n = len(srcs or []), len(lands)

    def body(*refs):
        src_refs, land_refs = refs[:ns], refs[ns:ns + n]
        send_sems, recv_sems = refs[ns + n], refs[ns + n + 1]
        for mine_out, _ in _chip_copies(src_refs, land_refs, send_sems, recv_sems):
            mine_out.start()
        refs[-1][...] = jnp.zeros_like(refs[-1])

    sems = pltpu.SemaphoreType.DMA((3 * n,))
    res = pl.pallas_call(
        body, name=name, out_shape=(sems, sems) + tuple(pltpu.HBM(a.shape, a.dtype) for a in ops) + (_sds((8, LANES)),),
        in_specs=[_HBM] * len(ops), out_specs=(_SEM, _SEM) + (_HBM,) * len(ops) + (pl.BlockSpec(memory_space=pltpu.VMEM),),
        input_output_aliases={k: 2 + k for k in range(len(ops))},
        compiler_params=pltpu.CompilerParams(has_side_effects=_EFFECT),
    )(*[pltpu.with_memory_space_constraint(a, pltpu.HBM) for a in ops])
    return res[0], res[1], list(res[2:2 + len(ops)]), res[-1]


def _split_wait(send_sems, recv_sems, thru, n, after, name):
    ns = len(thru) - n

    def body(*refs):
        src_refs, land_refs = refs[:ns], refs[ns:ns + n]
        for mine_out, arriving in _chip_copies(src_refs, land_refs, refs[ns + n], refs[ns + n + 1]):
            mine_out.wait_send()
            arriving.wait_recv()

    res = pl.pallas_call(
        body, name=name, out_shape=tuple(pltpu.HBM(a.shape, a.dtype) for a in thru),
        in_specs=[_HBM] * len(thru) + [_SEM, _SEM, pl.BlockSpec(memory_space=pl.ANY)], out_specs=(_HBM,) * len(thru),
        input_output_aliases={k: k for k in range(len(thru))},
        compiler_params=pltpu.CompilerParams(has_side_effects=_EFFECT),
    )(*thru, send_sems, recv_sems, after)
    return list(res)


def _sibling_swap(arrs, name):
    n = len(arrs)

    def body(*refs):
        ins, outs_, (send_sems, recv_sems) = refs[:n], refs[n:2 * n], refs[2 * n:]
        sib = (lax.axis_index("x"), lax.axis_index("y"), 1 - lax.axis_index("c"))
        cps = [pltpu.make_async_remote_copy(src_ref=ins[i], dst_ref=outs_[i], send_sem=send_sems.at[i], recv_sem=recv_sems.at[i],
                                            device_id=sib, device_id_type=MESH) for i in range(n)]
        for cp in cps:
            cp.start()
        for cp in cps:
            cp.wait_recv()
        for cp in cps:
            cp.wait_send()

    hbm = pl.BlockSpec(memory_space=pltpu.HBM)
    return pl.pallas_call(
        body, name=name, out_shape=tuple(_sds(a.shape, a.dtype) for a in arrs), in_specs=[hbm] * n, out_specs=tuple([hbm] * n),
        scratch_shapes=[pltpu.SemaphoreType.DMA((n,)), pltpu.SemaphoreType.DMA((n,))],
    )(*arrs)


def _tile_spec(rows, cc):
    return pl.BlockSpec((None, rows, cc), lambda l, i: (l, i, 0))


def _cast_bf16(a, rows, name):
    nl, r, cc = a.shape

    def body(a_ref, o_ref):
        o_ref[...] = a_ref[...].astype(jnp.bfloat16)

    return pl.pallas_call(body, name=name, out_shape=_sds((nl, r, cc), jnp.bfloat16), grid=(nl, r // rows),
                          in_specs=[_tile_spec(rows, cc)], out_specs=_tile_spec(rows, cc), compiler_params=_cp())(a)


def _sum_blocks(a, rows, name):
    k, nl, r, cc = a.shape

    def body(a_ref, o_ref):
        acc = a_ref[0].astype(F32)
        for j in range(1, k):
            acc = acc + a_ref[j].astype(F32)
        o_ref[...] = acc

    return pl.pallas_call(body, name=name, out_shape=_sds((nl, r, cc)), grid=(nl, r // rows),
                          in_specs=[pl.BlockSpec((k, None, rows, cc), lambda l, i: (0, l, i, 0))],
                          out_specs=_tile_spec(rows, cc), compiler_params=_cp())(a)


def _sum_chips(lands, srcs, rows, name):
    nl = len(lands)
    _, r, cc = lands[0].shape

    def body(*refs):
        land_refs, src_refs, o_ref = refs[:nl], refs[nl:2 * nl], refs[2 * nl]
        mine = 2 * lax.axis_index("x") + lax.axis_index("y")
        for j in range(nl):
            @pl.when(pl.program_id(0) == j)
            def _(j=j):
                own = src_refs[j][mine].astype(F32)
                acc = None
                for k in range(4):
                    term = jnp.where(mine == k, own, land_refs[j][k].astype(F32))
                    acc = term if acc is None else acc + term
                o_ref[...] = acc

    specs = [pl.BlockSpec((4, rows, cc), lambda l, i, j=j: (0, jnp.where(l == j, i, 0), 0)) for j in range(nl)]
    return pl.pallas_call(body, name=name, out_shape=_sds((nl, r, cc)), grid=(nl, r // rows),
                          in_specs=specs + specs, out_specs=_tile_spec(rows, cc), compiler_params=_cp())(*lands, *srcs)


def _adamw(w, parts, m, v, rows, name):
    nl, r, cc = w.shape
    np_ = len(parts)
    c1 = 1.0 / (1.0 - ADAM_B1 ** ADAM_STEP)
    c2 = 1.0 / (1.0 - ADAM_B2 ** ADAM_STEP)

    def body(*refs):
        w_ref, p_refs, (m_ref, v_ref, g_ref, d_ref, nm_ref, nv_ref) = refs[0], refs[1:1 + np_], refs[1 + np_:]
        g = p_refs[0][...]
        for p_ref in p_refs[1:]:
            g = g + p_ref[...]
        nm = ADAM_B1 * m_ref[...] + (1.0 - ADAM_B1) * g
        nv = ADAM_B2 * v_ref[...] + (1.0 - ADAM_B2) * (g * g)
        g_ref[...] = g
        nm_ref[...] = nm
        nv_ref[...] = nv
        d_ref[...] = -ADAM_LR * ((nm * c1) / (jnp.sqrt(nv * c2) + ADAM_EPS) + ADAM_WD * w_ref[...])

    spec = _tile_spec(rows, cc)
    return pl.pallas_call(body, name=name, out_shape=(_sds((nl, r, cc)),) * 4, grid=(nl, r // rows),
                          in_specs=[spec] * (3 + np_), out_specs=(spec,) * 4, compiler_params=_cp())(w, *parts, m, v)


_BIAS = pltpu.VMEM((2, 2 * BLK, 2 * BLK), F32)


def _fill_band_bias(bias_ref):
    qi = lax.broadcasted_iota(jnp.int32, (2 * BLK, 2 * BLK), 0) & (BLK - 1)
    kj = lax.broadcasted_iota(jnp.int32, (2 * BLK, 2 * BLK), 1)
    dist = BLK + qi - kj
    band = (dist >= 0) & (dist <= BLK)
    bias_ref[0] = jnp.where(band, 0.0, NEG)
    bias_ref[1] = jnp.where(band & (kj >= BLK), 0.0, NEG)


class _HeadStack:
    def __init__(self, group):
        self.m0, self.m1 = _half_masks()
        self.group = group
        if group is not None:
            self.kv_mask = (self.m0, self.m1)[group]

    def _swap_half(self, t, a):
        return t if a == self.group else pltpu.roll(t, HD, axis=1)

    def stack(self, t):
        t0, t1 = t * self.m0, t * self.m1
        if self.group is not None:
            t0, t1 = self._swap_half(t0, 0), self._swap_half(t1, 1)
        return jnp.concatenate([t0, t1], axis=0)

    def unstack(self, ts):
        if self.group is None:
            return ts[:BLK] * self.m0 + ts[BLK:] * self.m1
        return self._swap_half(ts[:BLK] * self.kv_mask, 0) + self._swap_half(ts[BLK:] * self.kv_mask, 1)


def _rows(st, dil):
    if dil == 1:
        return pl.ds(pl.multiple_of(st, BLK), BLK)
    return pl.ds(st, BLK, stride=dil)


def _block_pos(n, dil):
    nb = SEQ // (dil * BLK)
    r, b = n // nb, n % nb
    hp = (b > 0).astype(jnp.int32)
    st = r + dil * BLK * b
    return st, st - dil * BLK * hp, 1 - hp


def _attn_fwd(proj, qblk, kblk, vblk, dils, gqa, sink_x, name):
    has_sink = sink_x is not None

    def body(*refs):
        if has_sink:
            q_ref, k_ref, v_ref, s_ref, o_ref, lse_ref, m_scr, z_scr, bias_scr = refs
        else:
            q_ref, k_ref, v_ref, o_ref, lse_ref, m_scr, z_scr, bias_scr = refs

        @pl.when(pl.program_id(0) == 0)
        def _():
            _fill_band_bias(bias_scr)
        o_ref[...] = jnp.zeros_like(o_ref)
        if has_sink:
            z_scr[...] = jnp.ones_like(z_scr)
            m_scr[...] = jnp.broadcast_to(s_ref[...], m_scr.shape)
        else:
            z_scr[...] = jnp.zeros_like(z_scr)
            m_scr[...] = jnp.full_like(m_scr, NEG)

        def step(n, carry, dil, heads):
            m0, m1 = heads.m0, heads.m1
            st, stp, first = _block_pos(n, dil)
            rq, rp = _rows(st, dil), _rows(stp, dil)
            kk = jnp.concatenate([k_ref[rp, :], k_ref[rq, :]], axis=0)
            vv = jnp.concatenate([v_ref[rp, :], v_ref[rq, :]], axis=0)
            s = _mm(heads.stack(q_ref[rq, :] * QK_SCALE), kk, NT) + bias_scr[first]
            m = jnp.max(s, axis=1, keepdims=True)
            p = jnp.exp(s - m)
            l = jnp.sum(p, axis=1, keepdims=True)
            o_pair = heads.unstack(_mm(p, vv))
            m_pair = m[:BLK] * m0 + m[BLK:] * m1
            l_pair = l[:BLK] * m0 + l[BLK:] * m1
            m_old = m_scr[rq, :]
            m_new = jnp.maximum(m_old, m_pair)
            alpha, beta = jnp.exp(m_old - m_new), jnp.exp(m_pair - m_new)
            o_ref[rq, :] = o_ref[rq, :] * alpha + o_pair * beta
            z_scr[rq, :] = z_scr[rq, :] * alpha + l_pair * beta
            m_scr[rq, :] = m_new
            return carry

        def blocks(heads):
            for dil in dils:
                lax.fori_loop(0, SEQ // BLK, lambda n, carry, dil=dil: step(n, carry, dil, heads), 0, unroll=8)

        if gqa:
            for grp in range(2):
                pl.when(pl.program_id(0) // 2 == grp)(lambda grp=grp: blocks(_HeadStack(grp)))
        else:
            blocks(_HeadStack(None))

        def fin(t, carry):
            rt = pl.ds(pl.multiple_of(t * TM, TM), TM)
            z = z_scr[rt, :]
            o_ref[rt, :] = o_ref[rt, :] / z
            lse_ref[rt, :] = m_scr[rt, :] + jnp.log(z)
            return carry
        lax.fori_loop(0, SEQ // TM, fin, 0)

    col = lambda blk: pl.BlockSpec((SEQ, LANES), lambda p, blk=blk: (0, blk + p))
    kv = (lambda blk: pl.BlockSpec((SEQ, LANES), lambda p, blk=blk: (0, blk))) if gqa else col
    in_specs = [col(qblk), kv(kblk), kv(vblk)]
    args = [proj, proj, proj]
    if has_sink:
        in_specs.append(pl.BlockSpec((1, LANES), lambda p: (0, p)))
        args.append(sink_x)
    out = pl.BlockSpec((SEQ, LANES), lambda p: (0, p))
    return pl.pallas_call(body, name=name, out_shape=(_sds((SEQ, 512)), _sds((SEQ, 512))), grid=(4,),
                          in_specs=in_specs, out_specs=(out, out),
                          scratch_shapes=[pltpu.VMEM((SEQ, LANES), F32), pltpu.VMEM((SEQ, LANES), F32), _BIAS],
                          compiler_params=_cp(48))(*args)


def _attn_bwd(proj, qblk, kblk, vblk, do, o, lse, dils, gqa, sink_x, name):
    has_sink = sink_x is not None

    def body(*refs):
        if has_sink:
            q_ref, k_ref, v_ref, do_ref, o_ref, lse_ref, s_ref, dq_ref, dk_ref, dv_ref, ds_ref, bias_scr = refs
        else:
            q_ref, k_ref, v_ref, do_ref, o_ref, lse_ref, dq_ref, dk_ref, dv_ref, bias_scr = refs
        pid = pl.program_id(0)

        @pl.when(pid == 0)
        def _():
            _fill_band_bias(bias_scr)
        dq_ref[...] = jnp.zeros_like(dq_ref)
        if gqa:
            @pl.when(pid == 0)
            def _():
                dk_ref[...] = jnp.zeros_like(dk_ref)
                dv_ref[...] = jnp.zeros_like(dv_ref)
        else:
            dk_ref[...] = jnp.zeros_like(dk_ref)
            dv_ref[...] = jnp.zeros_like(dv_ref)

        def step(n, carry, dil, heads):
            m0, m1 = heads.m0, heads.m1
            st, stp, first = _block_pos(n, dil)
            rq, rp = _rows(st, dil), _rows(stp, dil)
            do_, lse_ = do_ref[rq, :], lse_ref[rq, :]
            kk = jnp.concatenate([k_ref[rp, :], k_ref[rq, :]], axis=0)
            vv = jnp.concatenate([v_ref[rp, :], v_ref[rq, :]], axis=0)
            qs, dos = heads.stack(q_ref[rq, :] * QK_SCALE), heads.stack(do_)
            doo = do_ * o_ref[rq, :]
            delta = jnp.concatenate([jnp.sum(doo * m0, axis=1, keepdims=True), jnp.sum(doo * m1, axis=1, keepdims=True)], axis=0)
            lse_s = jnp.concatenate([lse_[:, 0:1], lse_[:, HD:HD + 1]], axis=0)
            p = jnp.exp(_mm(qs, kk, NT) + bias_scr[first] - lse_s)
            ds = p * (_mm(dos, vv, NT) - delta)
            dq_ref[rq, :] += heads.unstack(_mm(ds, kk)) * QK_SCALE
            dk_sum, dv_sum = _mm(ds, qs, TN), _mm(p, dos, TN)
            dk_ref[rp, :] += dk_sum[:BLK]
            dk_ref[rq, :] += dk_sum[BLK:]
            dv_ref[rp, :] += dv_sum[:BLK]
            dv_ref[rq, :] += dv_sum[BLK:]
            return carry

        def blocks(heads):
            for dil in dils:
                lax.fori_loop(0, SEQ // BLK, lambda n, carry, dil=dil: step(n, carry, dil, heads), 0, unroll=4)

        if gqa:
            for grp in range(2):
                pl.when(pid // 2 == grp)(lambda grp=grp: blocks(_HeadStack(grp)))
        else:
            blocks(_HeadStack(None))

        if has_sink:
            m0, m1 = _half_masks()

            def sink_rows(t, acc):
                rt = pl.ds(pl.multiple_of(t * TM, TM), TM)
                return acc - jnp.sum(jnp.exp(s_ref[...] - lse_ref[rt, :]) * (do_ref[rt, :] * o_ref[rt, :]), axis=0, keepdims=True)
            acc = lax.fori_loop(0, SEQ // TM, sink_rows, jnp.zeros((1, LANES), F32))
            per_head = jnp.sum(acc * m0, axis=1, keepdims=True) * m0 + jnp.sum(acc * m1, axis=1, keepdims=True) * m1
            ds_ref[0] = jnp.broadcast_to(per_head, (8, LANES))

    col = lambda blk: pl.BlockSpec((SEQ, LANES), lambda p, blk=blk: (0, blk + p))
    kv = (lambda blk: pl.BlockSpec((SEQ, LANES), lambda p, blk=blk: (0, blk))) if gqa else col
    pair = pl.BlockSpec((SEQ, LANES), lambda p: (0, p))
    in_specs = [col(qblk), kv(kblk), kv(vblk), pair, pair, pair]
    args = [proj, proj, proj, do, o, lse]
    kvw = LANES if gqa else 512
    kv_out = pl.BlockSpec((SEQ, LANES), lambda p: (0, 0)) if gqa else pair
    out_shape = [_sds((SEQ, 512)), _sds((SEQ, kvw)), _sds((SEQ, kvw))]
    out_specs = [pair, kv_out, kv_out]
    if has_sink:
        in_specs.append(pl.BlockSpec((1, LANES), lambda p: (0, p)))
        args.append(sink_x)
        out_shape.append(_sds((4, 8, LANES)))
        out_specs.append(pl.BlockSpec((1, 8, LANES), lambda p: (p, 0, 0)))
    return pl.pallas_call(body, name=name, out_shape=tuple(out_shape), grid=(4,), in_specs=in_specs,
                          out_specs=tuple(out_specs), scratch_shapes=[_BIAS], compiler_params=_cp(56))(*args)


_CT = 128


def _rows_before(x_ref, t, k):
    if t == 0:
        return jnp.concatenate([jnp.zeros((k, LANES), F32), x_ref[0:_CT - k, :]], axis=0)
    return x_ref[t * _CT - k:(t + 1) * _CT - k, :]


def _conv_pre(x_ref, w_ref, b_ref, t):
    taps = [x_ref[t * _CT:(t + 1) * _CT, :]] + [_rows_before(x_ref, t, k) for k in range(1, 4)]
    u = b_ref[...] + taps[0] * w_ref[3:4, :]
    for k in range(1, 4):
        u = u + taps[k] * w_ref[3 - k:4 - k, :]
    return u, taps


def _conv_fwd(proj, w, b, name):
    def body(x_ref, w_ref, b_ref, o_ref):
        for t in range(SEQ // _CT):
            o_ref[t * _CT:(t + 1) * _CT, :] = _silu(_conv_pre(x_ref, w_ref, b_ref, t)[0])

    nblk = CONV_CH // LANES
    return pl.pallas_call(body, name=name, out_shape=_sds((SEQ, CONV_CH)), grid=(nblk,),
                          in_specs=[pl.BlockSpec((SEQ, LANES), lambda j: (0, XBC // LANES + j)),
                                    pl.BlockSpec((4, LANES), lambda j: (0, j)), pl.BlockSpec((1, LANES), lambda j: (0, j))],
                          out_specs=pl.BlockSpec((SEQ, LANES), lambda j: (0, j)), compiler_params=_cp())(proj, w, b)


def _conv_bwd(proj, dact, w, b, name):
    def body(x_ref, da_ref, w_ref, b_ref, dx_ref, dw_ref, db_ref, du_scr):
        du_scr[SEQ:SEQ + 8, :] = jnp.zeros((8, LANES), F32)
        db = jnp.zeros((1, LANES), F32)
        dws = [jnp.zeros((1, LANES), F32)] * 4
        for t in range(SEQ // _CT):
            u, taps = _conv_pre(x_ref, w_ref, b_ref, t)
            du = da_ref[t * _CT:(t + 1) * _CT, :] * _dsilu(u)
            du_scr[t * _CT:(t + 1) * _CT, :] = du
            db = db + jnp.sum(du, axis=0, keepdims=True)
            dws = [dws[k] + jnp.sum(du * taps[k], axis=0, keepdims=True) for k in range(4)]
        db_ref[...] = db
        for k in range(4):
            dw_ref[3 - k:4 - k, :] = dws[k]
        for t in range(SEQ // _CT):
            dx = du_scr[t * _CT:(t + 1) * _CT, :] * w_ref[3:4, :]
            for k in range(1, 4):
                dx = dx + du_scr[t * _CT + k:(t + 1) * _CT + k, :] * w_ref[3 - k:4 - k, :]
            dx_ref[t * _CT:(t + 1) * _CT, :] = dx

    nblk = CONV_CH // LANES
    blk = pl.BlockSpec((SEQ, LANES), lambda j: (0, j))
    wspec, bspec = pl.BlockSpec((4, LANES), lambda j: (0, j)), pl.BlockSpec((1, LANES), lambda j: (0, j))
    return pl.pallas_call(body, name=name, out_shape=(_sds((SEQ, CONV_CH)), _sds((4, CONV_CH)), _sds((1, CONV_CH))), grid=(nblk,),
                          in_specs=[pl.BlockSpec((SEQ, LANES), lambda j: (0, XBC // LANES + j)), blk, wspec, bspec],
                          out_specs=(blk, wspec, bspec), scratch_shapes=[pltpu.VMEM((SEQ + 8, LANES), F32)],
                          compiler_params=_cp())(proj, dact, w, b)


def _ssd_chunk(xs, bm, cm, dtr, z, hs, al16, dtb, dskx, nw):
    m0, m1 = _half_masks()
    row = lax.broadcasted_iota(jnp.int32, (BLK, BLK), 0)
    col = lax.broadcasted_iota(jnp.int32, (BLK, BLK), 1)
    causal = row >= col
    tril = causal.astype(F32)
    lane = lax.broadcasted_iota(jnp.int32, (1, LANES), 1)
    sub = lax.broadcasted_iota(jnp.int32, (BLK, 1), 0)
    last_row = (sub == BLK - 1).astype(F32)
    dt = jnp.where(lane < 16, _softplus(dtr + dtb), 0.0)
    a16 = -jnp.exp(al16)
    acum = jnp.dot(tril, dt * a16, precision=HI, preferred_element_type=F32)
    acum_t = acum.T
    gmat = [_mm(cm[g], bm[g], NT) for g in range(2)]
    ys, hn = [], []
    for p in range(8):
        g = p // 4
        pick = [(lane == 2 * p + a).astype(F32) for a in range(2)]
        col_h = [jnp.sum(acum * pick[a], axis=1, keepdims=True) for a in range(2)]
        dt_x = sum(jnp.sum(dt * pick[a], axis=1, keepdims=True) * msk for a, msk in enumerate((m0, m1)))
        ac_x = col_h[0] * m0 + col_h[1] * m1
        a_end = jnp.sum(ac_x * last_row, axis=0, keepdims=True)
        xdt = xs[p] * dt_x
        y = _mm(cm[g], hs[p]) * jnp.exp(ac_x)
        for a, msk in enumerate((m0, m1)):
            row_h = jnp.sum(acum_t * (sub == 2 * p + a).astype(F32), axis=0, keepdims=True)
            decay = jnp.exp(jnp.where(causal, col_h[a] - row_h, NEG))
            y = y + _mm(gmat[g] * decay, xdt * msk)
        st = _mm(bm[g], xdt * jnp.exp(a_end - ac_x), TN)
        hn.append(hs[p] * jnp.exp(a_end) + st)
        y = y + dskx[p] * xs[p]
        ys.append(y * _silu(z[p]))
    out = []
    for g in range(2):
        ms = sum(jnp.sum(ys[p] * ys[p], axis=1, keepdims=True) for p in range(4 * g, 4 * g + 4)) * (1.0 / 512)
        rstd = lax.rsqrt(ms + EPS)
        out += [ys[p] * rstd * nw[p] for p in range(4 * g, 4 * g + 4)]
    return out, hn


def _tiles(ref, n, off=0):
    return [ref[:, off + LANES * p:off + LANES * (p + 1)] for p in range(n)]


def _ssd_load(xbc_ref, z_ref, dt_ref, al16_ref, dtb_ref, dsk_ref, nw_ref):
    return (_tiles(xbc_ref, 8), _tiles(xbc_ref, 2, 1024), _tiles(xbc_ref, 2, 1280), dt_ref[...], _tiles(z_ref, 8)), \
           (al16_ref[...], dtb_ref[...], _tiles(dsk_ref, 8), _tiles(nw_ref, 8))


_NCH = SEQ // BLK


def _ssd_param_specs():
    return [_full((1, LANES)), _full((1, LANES)), _full((1, 1024)), _full((1, 1024))]


def _ssd_fwd(xbc_act, proj, al16, dtb, dskx, nw, name):
    def body(xbc_ref, z_ref, dt_ref, al16_ref, dtb_ref, dsk_ref, nw_ref, y_ref, hin_ref, h_scr):
        @pl.when(pl.program_id(0) == 0)
        def _():
            h_scr[...] = jnp.zeros_like(h_scr)
        acts, params = _ssd_load(xbc_ref, z_ref, dt_ref, al16_ref, dtb_ref, dsk_ref, nw_ref)
        hs = _tiles(h_scr, 8)
        hin_ref[0] = h_scr[...]
        ys, hn = _ssd_chunk(*acts, hs, *params)
        for p in range(8):
            y_ref[:, LANES * p:LANES * (p + 1)] = ys[p]
            h_scr[:, LANES * p:LANES * (p + 1)] = hn[p]

    return pl.pallas_call(
        body, name=name, out_shape=(_sds((SEQ, 1024)), _sds((_NCH, BLK, 1024))), grid=(_NCH,),
        in_specs=[pl.BlockSpec((BLK, CONV_CH), lambda c: (c, 0)), pl.BlockSpec((BLK, 1024), lambda c: (c, ZB // 1024)),
                  pl.BlockSpec((BLK, LANES), lambda c: (c, DTC // LANES))] + _ssd_param_specs(),
        out_specs=(pl.BlockSpec((BLK, 1024), lambda c: (c, 0)), pl.BlockSpec((1, BLK, 1024), lambda c: (c, 0, 0))),
        scratch_shapes=[pltpu.VMEM((BLK, 1024), F32)], compiler_params=_cp())(xbc_act, proj, proj, al16, dtb, dskx, nw)


def _ssd_bwd(xbc_act, proj, hin, dyb, al16, dtb, dskx, nw, name):
    def body(xbc_ref, z_ref, dt_ref, hin_ref, dy_ref, al16_ref, dtb_ref, dsk_ref, nw_ref,
             dxbc_ref, dz_ref, ddt_ref, dal16_ref, ddtb_ref, ddsk_ref, dnw_ref, dh_scr):
        @pl.when(pl.program_id(0) == 0)
        def _():
            dh_scr[...] = jnp.zeros_like(dh_scr)
            for r in (dal16_ref, ddtb_ref, ddsk_ref, dnw_ref):
                r[...] = jnp.zeros_like(r)
        acts, params = _ssd_load(xbc_ref, z_ref, dt_ref, al16_ref, dtb_ref, dsk_ref, nw_ref)
        hs = [hin_ref[0, :, LANES * p:LANES * (p + 1)] for p in range(8)]
        _, vjp = jax.vjp(lambda a, h, q: _ssd_chunk(*a, h, *q), acts, hs, params)
        (dxs, dbm, dcm, ddt, dz), dhs, (dal16, ddtb, ddsk, dnw) = vjp((_tiles(dy_ref, 8), _tiles(dh_scr, 8)))
        for p in range(8):
            cols = slice(LANES * p, LANES * (p + 1))
            dxbc_ref[:, cols] = dxs[p]
            dz_ref[:, cols] = dz[p]
            dh_scr[:, cols] = dhs[p]
            ddsk_ref[:, cols] += ddsk[p]
            dnw_ref[:, cols] += dnw[p]
        for g in range(2):
            dxbc_ref[:, 1024 + LANES * g:1024 + LANES * (g + 1)] = dbm[g]
            dxbc_ref[:, 1280 + LANES * g:1280 + LANES * (g + 1)] = dcm[g]
        ddt_ref[...] = ddt
        dal16_ref[...] += dal16
        ddtb_ref[...] += ddtb

    rev = lambda c: _NCH - 1 - c
    return pl.pallas_call(
        body, name=name,
        out_shape=(_sds((SEQ, CONV_CH)), _sds((SEQ, 1024)), _sds((SEQ, LANES)),
                   _sds((1, LANES)), _sds((1, LANES)), _sds((1, 1024)), _sds((1, 1024))),
        grid=(_NCH,),
        in_specs=[pl.BlockSpec((BLK, CONV_CH), lambda c: (rev(c), 0)), pl.BlockSpec((BLK, 1024), lambda c: (rev(c), ZB // 1024)),
                  pl.BlockSpec((BLK, LANES), lambda c: (rev(c), DTC // LANES)), pl.BlockSpec((1, BLK, 1024), lambda c: (rev(c), 0, 0)),
                  pl.BlockSpec((BLK, 1024), lambda c: (rev(c), 0))] + _ssd_param_specs(),
        out_specs=(pl.BlockSpec((BLK, CONV_CH), lambda c: (rev(c), 0)), pl.BlockSpec((BLK, 1024), lambda c: (rev(c), 0)),
                   pl.BlockSpec((BLK, LANES), lambda c: (rev(c), 0)),
                   _full((1, LANES)), _full((1, LANES)), _full((1, 1024)), _full((1, 1024))),
        scratch_shapes=[pltpu.VMEM((BLK, 1024), F32)], compiler_params=_cp())(xbc_act, proj, proj, hin, dyb, al16, dtb, dskx, nw)


def _rstd(v):
    return lax.rsqrt(jnp.mean(v * v, axis=1, keepdims=True) + EPS)


def _rms_bwd(dn, n, rstd):
    return rstd * (dn - n * jnp.mean(dn * n, axis=1, keepdims=True))


_VEC = _full((1, D))


def _layer_spec(layer):
    return pl.BlockSpec((None, 2048, D), lambda *_: (layer, 0, 0))

_ROW = pl.BlockSpec((TM, D), lambda i, *_: (i, 0))


def _proj_fwd(x, pre_w, scale, shift, w, layer, name):
    tn, ni = 1024, SEQ // TM

    def body(x_ref, pw_ref, sc_ref, sh_ref, w_ref, o_ref, h_ref, h_scr):
        rows = pl.ds(pl.multiple_of(pl.program_id(1) * TM, TM), TM)

        @pl.when(pl.program_id(0) == 0)
        def _():
            xv = x_ref[...]
            h = ((xv * _rstd(xv) * pw_ref[...]) * (1.0 + sc_ref[...]) + sh_ref[...]).astype(h_ref.dtype)
            h_scr[rows, :] = h
            h_ref[...] = h
        o_ref[...] = jnp.dot(h_scr[rows, :], w_ref[...].astype(MXU), preferred_element_type=F32)

    first_pass = pl.BlockSpec((TM, D), lambda j, i: (jnp.where(j == 0, i, ni - 1), 0))
    return pl.pallas_call(body, name=name, out_shape=(_sds((SEQ, NP)), _sds((SEQ, D), MXU)), grid=(NP // tn, ni),
                          in_specs=[first_pass, _VEC, _VEC, _VEC, pl.BlockSpec((None, D, tn), lambda j, i: (layer, 0, j))],
                          out_specs=(pl.BlockSpec((TM, tn), lambda j, i: (i, j)), first_pass),
                          scratch_shapes=[pltpu.VMEM((SEQ, D), MXU)], compiler_params=_cp())(x, pre_w, scale, shift, w)


_HALF = pl.BlockSpec((TM, 512), lambda i: (i, 0))
_Z_A = pl.BlockSpec((TM, 512), lambda i: (i, ZA // 512))
_Z_C = pl.BlockSpec((TM, 512), lambda i: (i, ZC // 512))


def _out_fwd(o_a, yb, o_c, proj, w, layer, x, gate, post_w, name):
    def body(oa_ref, yb_ref, oc_ref, za_ref, zc_ref, w_ref, x_ref, g_ref, pw_ref, xn_ref, y_ref):
        y = (_mm(oa_ref[...] * _silu(za_ref[...]), w_ref[0:512, :]) + _mm(yb_ref[...], w_ref[512:1536, :])
             + _mm(oc_ref[...] * _silu(zc_ref[...]), w_ref[1536:2048, :]))
        y_ref[...] = y
        xn_ref[...] = x_ref[...] + g_ref[...] * (y * _rstd(y) * pw_ref[...])

    return pl.pallas_call(body, name=name, out_shape=(_sds((SEQ, D)), _sds((SEQ, D))), grid=(SEQ // TM,),
                          in_specs=[_HALF, _ROW, _HALF, _Z_A, _Z_C, _layer_spec(layer), _ROW, _VEC, _VEC],
                          out_specs=(_ROW, _ROW), compiler_params=_cp())(o_a, yb, o_c, proj, proj, w, x, gate, post_w)


def _post_bwd(dxo, y, gate, post_w, name):
    def body(dx_ref, y_ref, g_ref, pw_ref, dy_ref, dg_ref, dpw_ref):
        @pl.when(pl.program_id(0) == 0)
        def _():
            dg_ref[...] = jnp.zeros_like(dg_ref)
            dpw_ref[...] = jnp.zeros_like(dpw_ref)
        dx, y = dx_ref[...], y_ref[...]
        rstd = _rstd(y)
        n = y * rstd
        dg_ref[...] += jnp.sum(dx * (n * pw_ref[...]), axis=0, keepdims=True)
        dr = dx * g_ref[...]
        dpw_ref[...] += jnp.sum(dr * n, axis=0, keepdims=True)
        dy_ref[...] = _rms_bwd(dr * pw_ref[...], n, rstd)

    return pl.pallas_call(body, name=name, out_shape=(_sds((SEQ, D)), _sds((1, D)), _sds((1, D))), grid=(SEQ // TM,),
                          in_specs=[_ROW, _ROW, _VEC, _VEC], out_specs=(_ROW, _VEC, _VEC), compiler_params=_cp())(dxo, y, gate, post_w)


def _dymix(dy, w, layer, o_a, o_c, proj, name):
    def body(dy_ref, w_ref, oa_ref, oc_ref, za_ref, zc_ref, doa_ref, dza_ref, b_ref, doc_ref, dzc_ref):
        dy = dy_ref[...]
        b_ref[...] = _mm(dy, w_ref[512:1536, :], NT)
        for rows, o_ref, z_ref, do_ref, dz_ref in ((slice(0, 512), oa_ref, za_ref, doa_ref, dza_ref),
                                                   (slice(1536, 2048), oc_ref, zc_ref, doc_ref, dzc_ref)):
            dyg, z = _mm(dy, w_ref[rows, :], NT), z_ref[...]
            do_ref[...] = dyg * _silu(z)
            dz_ref[...] = dyg * o_ref[...] * _dsilu(z)

    return pl.pallas_call(body, name=name, out_shape=(_sds((SEQ, 512)), _sds((SEQ, 512)), _sds((SEQ, D)), _sds((SEQ, 512)), _sds((SEQ, 512))),
                          grid=(SEQ // TM,), in_specs=[_ROW, _layer_spec(layer), _HALF, _HALF, _Z_A, _Z_C],
                          out_specs=(_HALF, _HALF, _ROW, _HALF, _HALF), compiler_params=_cp())(dy, w, o_a, o_c, proj, proj)


def _dwout(o_a, yb, o_c, proj, dy, name):
    def body(oa_ref, yb_ref, oc_ref, za_ref, zc_ref, dy_ref, o_ref):
        @pl.when(pl.program_id(0) == 0)
        def _():
            o_ref[...] = jnp.zeros_like(o_ref)
        dy = dy_ref[...]
        o_ref[0:512, :] += _mm(oa_ref[...] * _silu(za_ref[...]), dy, TN)
        o_ref[512:1536, :] += _mm(yb_ref[...], dy, TN)
        o_ref[1536:2048, :] += _mm(oc_ref[...] * _silu(zc_ref[...]), dy, TN)

    return pl.pallas_call(body, name=name, out_shape=_sds((2048, D)), grid=(SEQ // TM,),
                          in_specs=[_HALF, _ROW, _HALF, _Z_A, _Z_C, _ROW], out_specs=_full((2048, D)),
                          compiler_params=_cp())(o_a, yb, o_c, proj, proj, dy)


def _dwin(h, pieces, name):
    n = len(pieces)
    widths = [p.shape[1] for p in pieces]
    half = NP // 2

    def body(*refs):
        h_ref, p_refs, o_ref = refs[0], refs[1:1 + n], refs[1 + n]

        @pl.when(pl.program_id(0) == 0)
        def _():
            o_ref[...] = jnp.zeros_like(o_ref)
        hv, c0 = h_ref[...], 0
        for p_ref, wd in zip(p_refs, widths):
            o_ref[:, c0:c0 + wd] += _mm(hv, p_ref[...], TN)
            c0 += wd

    return pl.pallas_call(body, name=name, out_shape=_sds((D, half)), grid=(SEQ // TM,),
                          in_specs=[_ROW] + [pl.BlockSpec((TM, wd), lambda k: (k, 0)) for wd in widths],
                          out_specs=_full((D, half)), compiler_params=_cp(56))(h, *pieces)


_TMH = 256


def _dh_bwd(pieces, w, x, pre_w, scale, dxo, name):
    n = len(pieces)
    widths = [p.shape[1] for p in pieces]

    def body(*refs):
        p_refs, (w_ref, x_ref, pw_ref, sc_ref, dxo_ref, dx_ref, dsh_ref, dsc_ref, dpw_ref) = refs[:n], refs[n:]

        @pl.when(pl.program_id(0) == 0)
        def _():
            for r in (dsh_ref, dsc_ref, dpw_ref):
                r[...] = jnp.zeros_like(r)
        dh, c0 = 0.0, 0
        for p_ref, wd in zip(p_refs, widths):
            dh = dh + _mm(p_ref[...], w_ref[:, c0:c0 + wd], NT)
            c0 += wd
        xv = x_ref[...]
        rstd = _rstd(xv)
        nrm = xv * rstd
        dsh_ref[...] += jnp.sum(dh, axis=0, keepdims=True)
        dsc_ref[...] += jnp.sum(dh * (nrm * pw_ref[...]), axis=0, keepdims=True)
        dhn = dh * (1.0 + sc_ref[...])
        dpw_ref[...] += jnp.sum(dhn * nrm, axis=0, keepdims=True)
        dx_ref[...] = _rms_bwd(dhn * pw_ref[...], nrm, rstd) + dxo_ref[...]

    row = pl.BlockSpec((_TMH, D), lambda i: (i, 0))
    return pl.pallas_call(body, name=name, out_shape=(_sds((SEQ, D)), _sds((1, D)), _sds((1, D)), _sds((1, D))),
                          grid=(SEQ // _TMH,),
                          in_specs=[pl.BlockSpec((_TMH, wd), lambda i: (i, 0)) for wd in widths]
                          + [pl.BlockSpec((None, D, NP), lambda i: (0, 0, 0)), row, _VEC, _VEC, row],
                          out_specs=(row, _VEC, _VEC, _VEC), compiler_params=_cp(56))(*pieces, w, x, pre_w, scale, dxo)


def _w_in_padded(land, name):
    rows = 128

    def body(l_ref, o_ref):
        o_ref[...] = _pad_cols(jnp.concatenate([l_ref[k] for k in range(4)], axis=1))

    return pl.pallas_call(body, name=name, out_shape=_sds((D, NP), land.dtype), grid=(D // rows,),
                          in_specs=[pl.BlockSpec((4, rows, SHARD_IN), lambda i: (0, i, 0))],
                          out_specs=pl.BlockSpec((rows, NP), lambda i: (i, 0)), compiler_params=_cp())(land)


def _grad_blocks(dwa, dwb, name):
    rows = 128

    def body(a_ref, b_ref, o_ref):
        g = _unpad_cols(jnp.concatenate([a_ref[...], b_ref[...]], axis=1))
        for k in range(4):
            o_ref[k] = g[:, SHARD_IN * k:SHARD_IN * (k + 1)].astype(o_ref.dtype)

    half = pl.BlockSpec((rows, NP // 2), lambda i: (i, 0))
    return pl.pallas_call(body, name=name, out_shape=_sds((4, D, SHARD_IN), jnp.bfloat16), grid=(D // rows,),
                          in_specs=[half, half], out_specs=pl.BlockSpec((4, rows, SHARD_IN), lambda i: (0, i, 0)),
                          compiler_params=_cp())(dwa, dwb)


def _loss_bwd(xf, tgt, name):
    def body(x_ref, t_ref, dx_ref, l_ref):
        @pl.when(pl.program_id(0) == 0)
        def _():
            l_ref[...] = jnp.zeros_like(l_ref)
        e = x_ref[...] - t_ref[...]
        dx_ref[...] = e * (1.0 / D)
        l_ref[...] += 0.5 * jnp.sum(jnp.mean(e * e, axis=1, keepdims=True), axis=0, keepdims=True)

    return pl.pallas_call(body, name=name, out_shape=(_sds((SEQ, D)), _sds((8, LANES))), grid=(SEQ // TM,),
                          in_specs=[_ROW, _ROW], out_specs=(_ROW, _full((8, LANES))), compiler_params=_cp())(xf, tgt)


def _mod_part(c_all, ada_w, ada_b, name):
    def body(c_ref, w_ref, b_ref, o_ref):
        o_ref[0] = _mm(_silu(c_ref[...]), w_ref[0]) + b_ref[0]

    return pl.pallas_call(body, name=name, out_shape=_sds((DEPTH, 8, 768)), grid=(DEPTH,),
                          in_specs=[_full((8, D)), pl.BlockSpec((1, D, 768), lambda i: (i, 0, 0)), pl.BlockSpec((1, 1, 768), lambda i: (i, 0, 0))],
                          out_specs=pl.BlockSpec((1, 8, 768), lambda i: (i, 0, 0)), compiler_params=_cp())(c_all, ada_w, ada_b)


def _ada_grad(c_t, dmod, name):
    def body(c_ref, d_ref, o_ref):
        ca = _silu(c_ref[...])
        dm = d_ref[0]
        acc = ca[:, 0:1] * dm[0:1, :]
        for s in range(1, 8):
            acc = acc + ca[:, s:s + 1] * dm[s:s + 1, :]
        o_ref[0] = acc

    return pl.pallas_call(body, name=name, out_shape=_sds((DEPTH, D, 768)), grid=(DEPTH,),
                          in_specs=[_full((D, LANES)), pl.BlockSpec((1, 8, 768), lambda i: (i, 0, 0))],
                          out_specs=pl.BlockSpec((1, D, 768), lambda i: (i, 0, 0)), compiler_params=_cp())(c_t, dmod)


def _pack(parts):
    flat = []
    for p in parts:
        f = p.reshape(-1)
        flat.append(jnp.pad(f, (0, (-f.size) % LANES)))
    v = jnp.concatenate(flat)
    return jnp.pad(v, (0, (-v.size) % (8 * LANES))).reshape(-1, LANES)


def _unpack(v, shapes):
    v = v.reshape(-1)
    out, off = [], 0
    for s in shapes:
        n = math.prod(s)
        out.append(v[off:off + n].reshape(s))
        off += n + (-n) % LANES
    return out


_GIVEN_DT, _GIVEN_C = 4608, 4624


def _pad_cols(w):
    return jnp.concatenate([w[..., :_GIVEN_DT], w[..., _GIVEN_C:], w[..., _GIVEN_DT:_GIVEN_C],
                            jnp.zeros(w.shape[:-1] + (NP - IN_COLS,), w.dtype)], axis=-1)


def _unpad_cols(w):
    return jnp.concatenate([w[..., :_GIVEN_DT], w[..., DTC:DTC + 16], w[..., _GIVEN_DT:DTC]], axis=-1)


def _pad_lanes(v):
    return jnp.pad(v, (0, LANES - v.shape[0])).reshape(1, LANES)


def _local_step(x2, tgt, mod, weights_of, grads_done, pre_w, post_w, conv_w, conv_b, dt_bias, a_log, d_skip, nw, sinks):
    saved = []
    xcur = x2
    for i in range(DEPTH):
        shift, scale, gate = mod[i:i + 1, :D], mod[i:i + 1, D:2 * D], mod[i:i + 1, 2 * D:]
        pw, qw = pre_w[i:i + 1], post_w[i:i + 1]
        w_p, w_o = weights_of(i, xcur)
        proj, h = _proj_fwd(xcur, pw, scale, shift, w_p, 0, "proj_fwd")
        o_a, lse_a = _attn_fwd(proj, QA // LANES, KA // LANES, VA // LANES, DILS, False, None, "attn_a_fwd")
        sink_x = jnp.repeat(sinks[i], HD).reshape(1, 512)
        o_c, lse_c = _attn_fwd(proj, QC // LANES, KC // LANES, VC // LANES, (1,), True, sink_x, "attn_c_fwd")
        cw, cb = conv_w[i], conv_b[i:i + 1]
        xbc_act = _conv_fwd(proj, cw, cb, "conv_fwd")
        ssd_p = (_pad_lanes(a_log[i]), _pad_lanes(dt_bias[i]), jnp.repeat(d_skip[i], HD).reshape(1, 1024), nw[i:i + 1])
        yb, hin = _ssd_fwd(xbc_act, proj, *ssd_p, "ssd_fwd")
        xnew, y = _out_fwd(o_a, yb, o_c, proj, w_o, 0, xcur, gate, qw, "out_fwd")
        saved.append((w_p, w_o, xcur, scale, gate, pw, qw, proj, h, o_a, lse_a, sink_x, o_c, lse_c, cw, cb, xbc_act, ssd_p, yb, hin, y))
        xcur = xnew
    dx, ltile = _loss_bwd(xcur, tgt, "loss")
    dmod, small = [None] * DEPTH, [None] * DEPTH
    for i in reversed(range(DEPTH)):
        w_p, w_o, xin, scale, gate, pw, qw, proj, h, o_a, lse_a, sink_x, o_c, lse_c, cw, cb, xbc_act, ssd_p, yb, hin, y = saved[i]
        dy, dgate, dpost = _post_bwd(dx, y, gate, qw, "post_bwd")
        do_a, dz_a, dyb, do_c, dz_c = _dymix(dy, w_o, 0, o_a, o_c, proj, "dymix")
        dwo = _dwout(o_a, yb, o_c, proj, dy, "dwout")
        dq_a, dk_a, dv_a = _attn_bwd(proj, QA // LANES, KA // LANES, VA // LANES, do_a, o_a, lse_a, DILS, False, None, "attn_a_bwd")
        dq_c, dk_c, dv_c, dsk = _attn_bwd(proj, QC // LANES, KC // LANES, VC // LANES, do_c, o_c, lse_c, (1,), True, sink_x, "attn_c_bwd")
        dxbc_act, dz_b, ddt, dal16, ddtb, ddsk, dnw = _ssd_bwd(xbc_act, proj, hin, dyb, *ssd_p, "ssd_bwd")
        dxbc, dcw, dcb = _conv_bwd(proj, dxbc_act, cw, cb, "conv_bwd")
        half_a, half_b = [dq_a, dk_a, dv_a, dz_a, dz_b], [dxbc, dq_c, dz_c, dk_c, dv_c, ddt]
        sent = grads_done(i, _dwin(h, half_a, "dwin_a"), _dwin(h, half_b, "dwin_b"), dwo)
        dx, dshift, dscale, dpre = _dh_bwd(half_a + half_b, w_p, xin, pw, scale + sent[0, 0], dx, "dh_bwd")
        dmod[i] = jnp.concatenate([dshift, dscale, dgate], axis=1)
        small[i] = (dpre, dpost, dcw, dcb, ddtb[0, :16], dal16[0, :16], ddsk.reshape(16, HD).sum(axis=1), dnw, dsk[:, 0, ::HD].reshape(8))
    return ltile, dx, jnp.concatenate(dmod, axis=0), small


_SMALL = ((1, D), (1, D), (4, CONV_CH), (1, CONV_CH), (16,), (16,), (16,), (1, D), (8,))


def kernel(x, c, ada_w, ada_b, pre_norm_w, post_norm_w, w_in, conv_w, conv_b, dt_bias, a_log, d_skip, ssm_norm_w, sinks, w_out, loss_target, m_ada_w, m_ada_b, m_pre_norm_w, m_post_norm_w, m_w_in, m_conv_w, m_conv_b, m_dt_bias, m_a_log, m_d_skip, m_ssm_norm_w, m_sinks, m_w_out, v_ada_w, v_ada_b, v_pre_norm_w, v_post_norm_w, v_w_in, v_conv_w, v_conv_b, v_dt_bias, v_a_log, v_d_skip, v_ssm_norm_w, v_sinks, v_w_out):
    xi, yi, ci = lax.axis_index("x"), lax.axis_index("y"), lax.axis_index("c")
    chip = 2 * xi + yi
    me = 2 * chip + ci

    w_in_b = _cast_bf16(w_in, 512, "cast_w_in")
    w_out_b = _cast_bf16(w_out, 512, "cast_w_out")
    gathers = []
    for i in range(DEPTH):
        lands = [lax.dynamic_update_slice(lax.empty((4,) + a.shape[1:], a.dtype), a[i][None], (chip, 0, 0)) for a in (w_in_b, w_out_b)]
        gathers.append(_split_start(None, lands, f"gather_start{i}"))
    all_started = gathers[0][3] + gathers[1][3] + gathers[2][3] + gathers[3][3]

    def weights_of(i, after):
        send_sems, recv_sems, thru, _ = gathers[i]
        if i == 0:
            after = all_started + mod[:1, :LANES]
        g_in, g_out = _split_wait(send_sems, recv_sems, thru, 2, after, f"gather_wait{i}")
        return _w_in_padded(g_in, "w_in_padded")[None], g_out.reshape(1, 2048, D)

    scatters = [None] * DEPTH

    def grads_done(i, dwa, dwb, dwo):
        blocks = [_grad_blocks(dwa, dwb, "grad_blocks"), _cast_bf16(dwo.reshape(4, 512, D), 512, "cast_dw_out")]
        scatters[i] = _split_start(blocks, [lax.empty(b.shape, b.dtype) for b in blocks], f"scatter_start{i}")
        return scatters[i][3]

    g0 = _allgather8(_pack([c, conv_w]), "gather_c")
    c_all = g0[:, :8, :].reshape(8, D)
    conv_w_full = jnp.concatenate([g0[2 * k, 8:56, :].reshape(DEPTH, 4, CONV_CH // 4) for k in range(4)], axis=-1)

    ada_b_mine = lax.dynamic_slice_in_dim(ada_b, 768 * chip, 768, axis=1).reshape(DEPTH, 1, 768)
    gm = _allgather8(_mod_part(c_all, ada_w, ada_b_mine, "mod_part").reshape(DEPTH * 8, 768), "gather_mod")
    gm = gm.reshape(4, 2, DEPTH, 8, 768)[:, 0]
    mod = lax.dynamic_index_in_dim(gm, me, axis=2, keepdims=False).transpose(1, 0, 2).reshape(DEPTH, 3 * D)

    ltile, dx, dmod, small = _local_step(x[0], loss_target[0], mod, weights_of, grads_done, pre_norm_w, post_norm_w, conv_w_full,
                                         conv_b, dt_bias, a_log, d_skip, ssm_norm_w, sinks)

    packed = _pack([dmod] + [g for layer in small for g in layer] + [ltile[0]])
    gs = _allgather8(packed, "gather_small")
    tot = _sum_blocks(gs[:, None], packed.shape[0], "sum_small")[0]
    parts = _unpack(tot, [(DEPTH, 3 * D)] + list(_SMALL) * DEPTH + [(LANES,)])
    g_ada_b, loss = parts[0], parts[-1][0]
    per_layer = [parts[1 + len(_SMALL) * i:1 + len(_SMALL) * (i + 1)] for i in range(DEPTH)]
    g_pre, g_post, g_cw, g_cb, g_dtb, g_al, g_dsk, g_nw, g_sk = [jnp.stack([per_layer[i][j] for i in range(DEPTH)]) for j in range(len(_SMALL))]
    g_pre, g_post, g_cb, g_nw = g_pre[:, 0], g_post[:, 0], g_cb[:, 0], g_nw[:, 0]
    g_cw = lax.dynamic_slice_in_dim(g_cw, (CONV_CH // 4) * chip, CONV_CH // 4, axis=2)

    dmod_all = gs[:, :(DEPTH * 3 * D) // LANES, :].reshape(8, DEPTH, 3 * D).transpose(1, 0, 2)
    dmod_mine = lax.dynamic_slice_in_dim(dmod_all, 768 * chip, 768, axis=2)
    c_t = jnp.pad(c_all.T, ((0, 0), (0, LANES - 8)))
    g_ada_w = _ada_grad(c_t, dmod_mine, "ada_grad")

    res = {}
    res["ada_w"] = _adamw(ada_w, [g_ada_w], m_ada_w, v_ada_w, 512, "adamw_ada_w")
    names = ["ada_b", "pre_norm_w", "post_norm_w", "conv_w", "conv_b", "dt_bias", "a_log", "d_skip", "ssm_norm_w", "sinks"]
    ws = [ada_b, pre_norm_w, post_norm_w, conv_w, conv_b, dt_bias, a_log, d_skip, ssm_norm_w, sinks]
    gsm = [g_ada_b, g_pre, g_post, g_cw, g_cb, g_dtb, g_al, g_dsk, g_nw, g_sk]
    ms = [m_ada_b, m_pre_norm_w, m_post_norm_w, m_conv_w, m_conv_b, m_dt_bias, m_a_log, m_d_skip, m_ssm_norm_w, m_sinks]
    vs = [v_ada_b, v_pre_norm_w, v_post_norm_w, v_conv_w, v_conv_b, v_dt_bias, v_a_log, v_d_skip, v_ssm_norm_w, v_sinks]
    pw_, pg_, pm_, pv_ = _pack(ws), _pack(gsm), _pack(ms), _pack(vs)
    small_out = _adamw(pw_[None], [pg_[None]], pm_[None], pv_[None], pw_.shape[0], "adamw_small")

    others_done = small_out[1][0, :8] + res["ada_w"][1][0, :8, :LANES]
    landed = [_split_wait(*scatters[i][:3], 2, others_done, f"scatter_wait{i}") for i in range(DEPTH)]
    p_in = _sum_chips([d[2] for d in landed], [d[0] for d in landed], 128, "sum_w_in")
    p_out = _sum_chips([d[3] for d in landed], [d[1] for d in landed], 256, "sum_w_out")
    s_in, s_out = _sibling_swap([p_in, p_out], "swap_partials")
    res["w_in"] = _adamw(w_in, [p_in, s_in], m_w_in, v_w_in, 256, "adamw_w_in")
    res["w_out"] = _adamw(w_out, [p_out, s_out], m_w_out, v_w_out, 512, "adamw_w_out")
    shapes = [w.shape for w in ws]
    for kind in range(4):
        for nm, a in zip(names, _unpack(small_out[kind][0], shapes)):
            res.setdefault(nm, [None] * 4)[kind] = a
    order = ["ada_w", "ada_b", "pre_norm_w", "post_norm_w", "w_in", "conv_w", "conv_b", "dt_bias", "a_log", "d_skip", "ssm_norm_w", "sinks", "w_out"]
    return (loss, dx[None], *[res[n][0] for n in order], *[res[n][1] for n in order], *[res[n][2] for n in order], *[res[n][3] for n in order])
```

```python
import math

import jax
import jax.numpy as jnp
from jax import lax
from jax.experimental import pallas as pl
from jax.experimental.pallas import tpu as pltpu

F32 = jnp.float32
MXU = jnp.bfloat16
HI = lax.Precision.HIGHEST
MESH = pl.DeviceIdType.MESH

SEQ = 4096
D = 1024
DEPTH = 4
HD = 64
QK_SCALE = HD ** -0.5
LANES = 128
BLK = 128
DILS = (1, 4, 16)
NEG = -1e30
EPS = 1e-6
MIB = 1024 * 1024

NP = 6144
QA, KA, VA, ZA = 0, 512, 1024, 1536
ZB, XBC = 2048, 3072
QC, ZC, KC, VC = 4608, 5120, 5632, 5760
DTC = 5888
IN_COLS = 5904
SHARD_IN = IN_COLS // 4
CONV_CH = 1536
TM = 512

ADAM_LR, ADAM_B1, ADAM_B2, ADAM_EPS, ADAM_WD, ADAM_STEP = 0.001, 0.9, 0.999, 1e-08, 0.01, 10

NT = (((1,), (1,)), ((), ()))
TN = (((0,), (0,)), ((), ()))


def _cp(vmem_mib=48):
    return pltpu.CompilerParams(vmem_limit_bytes=vmem_mib * MIB)


def _sds(shape, dtype=F32):
    return jax.ShapeDtypeStruct(shape, dtype)


def _full(shape):
    n = len(shape)
    return pl.BlockSpec(shape, lambda *_: (0,) * n)


def _mm(a, b, dims=None):
    if dims is None:
        return jnp.dot(a.astype(MXU), b.astype(MXU), preferred_element_type=F32)
    return lax.dot_general(a.astype(MXU), b.astype(MXU), dims, preferred_element_type=F32)


def _sigmoid(x):
    return 1.0 / (1.0 + jnp.exp(-x))


def _silu(x):
    return x * _sigmoid(x)


def _dsilu(x):
    s = _sigmoid(x)
    return s * (1.0 + x * (1.0 - s))


def _softplus(x):
    ax = jnp.where(x >= 0, x, -x)
    return jnp.maximum(x, 0.0) + jnp.log1p(jnp.exp(-ax))


def _half_masks():
    lane = lax.broadcasted_iota(jnp.int32, (1, LANES), 1)
    m0 = (lane < HD).astype(F32)
    return m0, 1.0 - m0


def _allgather8(v, name):
    r, cc = v.shape

    def body(v_ref, out_ref, send_sems, recv_sems):
        x, y, c = lax.axis_index("x"), lax.axis_index("y"), lax.axis_index("c")
        me = 4 * x + 2 * y + c
        out_ref[me] = v_ref[...]
        peers = []
        for k in range(1, 8):
            px = 1 - x if k & 4 else x
            py = 1 - y if k & 2 else y
            pc = 1 - c if k & 1 else c
            peers.append((px, py, pc))
        sends = []
        for k, peer in enumerate(peers):
            cp = pltpu.make_async_remote_copy(src_ref=v_ref, dst_ref=out_ref.at[me], send_sem=send_sems.at[k],
                                              recv_sem=recv_sems.at[k], device_id=peer, device_id_type=MESH)
            cp.start()
            sends.append(cp)
        for k, (px, py, pc) in enumerate(peers):
            pltpu.make_async_remote_copy(src_ref=v_ref, dst_ref=out_ref.at[4 * px + 2 * py + pc], send_sem=send_sems.at[k],
                                         recv_sem=recv_sems.at[k], device_id=(px, py, pc), device_id_type=MESH).wait_recv()
        for cp in sends:
            cp.wait_send()

    return pl.pallas_call(
        body, name=name, out_shape=_sds((8, r, cc)),
        in_specs=[pl.BlockSpec(memory_space=pltpu.VMEM)], out_specs=pl.BlockSpec(memory_space=pltpu.VMEM),
        scratch_shapes=[pltpu.SemaphoreType.DMA((7,)), pltpu.SemaphoreType.DMA((7,))],
        compiler_params=_cp(32),
    )(v)


_HBM = pl.BlockSpec(memory_space=pltpu.HBM)
_SEM = pl.BlockSpec(memory_space=pltpu.SEMAPHORE)
_EFFECT = pltpu.SideEffectType.DATAFLOW_SIDE_EFFECTING


def _chip_copies(src_refs, land_refs, send_sems, recv_sems):
    x, y, c = lax.axis_index("x"), lax.axis_index("y"), lax.axis_index("c")
    mine = 2 * x + y
    out = []
    for i, land in enumerate(land_refs):
        for j, (px, py) in enumerate([(1 - x, y), (x, 1 - y), (1 - x, 1 - y)]):
            src = src_refs[i].at[2 * px + py] if src_refs else land.at[mine]
            mk = lambda dst, i=i, j=j, src=src, px=px, py=py: pltpu.make_async_remote_copy(
                src_ref=src, dst_ref=dst, send_sem=send_sems.at[3 * i + j], recv_sem=recv_sems.at[3 * i + j],
                device_id=(px, py, c), device_id_type=MESH)
            out.append((mk(land.at[mine]), mk(land.at[2 * px + py])))
    return out


def _split_start(srcs, lands, name):
    ops = list(srcs or []) + list(lands)
    ns, n = len(srcs or []), len(lands)

    def body(*refs):
        src_refs, land_refs = refs[:ns], refs[ns:ns + n]
        send_sems, recv_sems = refs[ns + n], refs[ns + n + 1]
        for mine_out, _ in _chip_copies(src_refs, land_refs, send_sems, recv_sems):
            mine_out.start()
        refs[-1][...] = jnp.zeros_like(refs[-1])

    sems = pltpu.SemaphoreType.DMA((3 * n,))
    res = pl.pallas_call(
        body, name=name, out_shape=(sems, sems) + tuple(pltpu.HBM(a.shape, a.dtype) for a in ops) + (_sds((8, LANES)),),
        in_specs=[_HBM] * len(ops), out_specs=(_SEM, _SEM) + (_HBM,) * len(ops) + (pl.BlockSpec(memory_space=pltpu.VMEM),),
        input_output_aliases={k: 2 + k for k in range(len(ops))},
        compiler_params=pltpu.CompilerParams(has_side_effects=_EFFECT),
    )(*[pltpu.with_memory_space_constraint(a, pltpu.HBM) for a in ops])
    return res[0], res[1], list(res[2:2 + len(ops)]), res[-1]


def _split_wait(send_sems, recv_sems, thru, n, after, name):
    ns = len(thru) - n

    def body(*refs):
        src_refs, land_refs = refs[:ns], refs[ns:ns + n]
        for mine_out, arriving in _chip_copies(src_refs, land_refs, refs[ns + n], refs[ns + n + 1]):
            mine_out.wait_send()
            arriving.wait_recv()

    res = pl.pallas_call(
        body, name=name, out_shape=tuple(pltpu.HBM(a.shape, a.dtype) for a in thru),
        in_specs=[_HBM] * len(thru) + [_SEM, _SEM, pl.BlockSpec(memory_space=pl.ANY)], out_specs=(_HBM,) * len(thru),
        input_output_aliases={k: k for k in range(len(thru))},
        compiler_params=pltpu.CompilerParams(has_side_effects=_EFFECT),
    )(*thru, send_sems, recv_sems, after)
    return list(res)


def _sibling_swap(arrs, name):
    n = len(arrs)

    def body(*refs):
        ins, outs_, (send_sems, recv_sems) = refs[:n], refs[n:2 * n], refs[2 * n:]
        sib = (lax.axis_index("x"), lax.axis_index("y"), 1 - lax.axis_index("c"))
        cps = [pltpu.make_async_remote_copy(src_ref=ins[i], dst_ref=outs_[i], send_sem=send_sems.at[i], recv_sem=recv_sems.at[i],
                                            device_id=sib, device_id_type=MESH) for i in range(n)]
        for cp in cps:
            cp.start()
        for cp in cps:
            cp.wait_recv()
        for cp in cps:
            cp.wait_send()

    hbm = pl.BlockSpec(memory_space=pltpu.HBM)
    return pl.pallas_call(
        body, name=name, out_shape=tuple(_sds(a.shape, a.dtype) for a in arrs), in_specs=[hbm] * n, out_specs=tuple([hbm] * n),
        scratch_shapes=[pltpu.SemaphoreType.DMA((n,)), pltpu.SemaphoreType.DMA((n,))],
    )(*arrs)


def _tile_spec(rows, cc):
    return pl.BlockSpec((None, rows, cc), lambda l, i: (l, i, 0))


def _cast_bf16(a, rows, name):
    nl, r, cc = a.shape

    def body(a_ref, o_ref):
        o_ref[...] = a_ref[...].astype(jnp.bfloat16)

    return pl.pallas_call(body, name=name, out_shape=_sds((nl, r, cc), jnp.bfloat16), grid=(nl, r // rows),
                          in_specs=[_tile_spec(rows, cc)], out_specs=_tile_spec(rows, cc), compiler_params=_cp())(a)


def _sum_blocks(a, rows, name):
    k, nl, r, cc = a.shape

    def body(a_ref, o_ref):
        acc = a_ref[0].astype(F32)
        for j in range(1, k):
            acc = acc + a_ref[j].astype(F32)
        o_ref[...] = acc

    return pl.pallas_call(body, name=name, out_shape=_sds((nl, r, cc)), grid=(nl, r // rows),
                          in_specs=[pl.BlockSpec((k, None, rows, cc), lambda l, i: (0, l, i, 0))],
                          out_specs=_tile_spec(rows, cc), compiler_params=_cp())(a)


def _sum_chips(lands, srcs, rows, name):
    nl = len(lands)
    _, r, cc = lands[0].shape

    def body(*refs):
        land_refs, src_refs, o_ref = refs[:nl], refs[nl:2 * nl], refs[2 * nl]
        mine = 2 * lax.axis_index("x") + lax.axis_index("y")
        for j in range(nl):
            @pl.when(pl.program_id(0) == j)
            def _(j=j):
                own = src_refs[j][mine].astype(F32)
                acc = None
                for k in range(4):
                    term = jnp.where(mine == k, own, land_refs[j][k].astype(F32))
                    acc = term if acc is None else acc + term
                o_ref[...] = acc

    specs = [pl.BlockSpec((4, rows, cc), lambda l, i, j=j: (0, jnp.where(l == j, i, 0), 0)) for j in range(nl)]
    return pl.pallas_call(body, name=name, out_shape=_sds((nl, r, cc)), grid=(nl, r // rows),
                          in_specs=specs + specs, out_specs=_tile_spec(rows, cc), compiler_params=_cp())(*lands, *srcs)


def _adamw(w, parts, m, v, rows, name):
    nl, r, cc = w.shape
    np_ = len(parts)
    c1 = 1.0 / (1.0 - ADAM_B1 ** ADAM_STEP)
    c2 = 1.0 / (1.0 - ADAM_B2 ** ADAM_STEP)

    def body(*refs):
        w_ref, p_refs, (m_ref, v_ref, g_ref, d_ref, nm_ref, nv_ref) = refs[0], refs[1:1 + np_], refs[1 + np_:]
        g = p_refs[0][...]
        for p_ref in p_refs[1:]:
            g = g + p_ref[...]
        nm = ADAM_B1 * m_ref[...] + (1.0 - ADAM_B1) * g
        nv = ADAM_B2 * v_ref[...] + (1.0 - ADAM_B2) * (g * g)
        g_ref[...] = g
        nm_ref[...] = nm
        nv_ref[...] = nv
        d_ref[...] = -ADAM_LR * ((nm * c1) / (jnp.sqrt(nv * c2) + ADAM_EPS) + ADAM_WD * w_ref[...])

    spec = _tile_spec(rows, cc)
    return pl.pallas_call(body, name=name, out_shape=(_sds((nl, r, cc)),) * 4, grid=(nl, r // rows),
                          in_specs=[spec] * (3 + np_), out_specs=(spec,) * 4, compiler_params=_cp())(w, *parts, m, v)


_BIAS = pltpu.VMEM((2, 2 * BLK, 2 * BLK), F32)


def _fill_band_bias(bias_ref):
    qi = lax.broadcasted_iota(jnp.int32, (2 * BLK, 2 * BLK), 0) & (BLK - 1)
    kj = lax.broadcasted_iota(jnp.int32, (2 * BLK, 2 * BLK), 1)
    dist = BLK + qi - kj
    band = (dist >= 0) & (dist <= BLK)
    bias_ref[0] = jnp.where(band, 0.0, NEG)
    bias_ref[1] = jnp.where(band & (kj >= BLK), 0.0, NEG)


class _HeadStack:
    def __init__(self, group):
        self.m0, self.m1 = _half_masks()
        self.group = group
        if group is not None:
            self.kv_mask = (self.m0, self.m1)[group]

    def _swap_half(self, t, a):
        return t if a == self.group else pltpu.roll(t, HD, axis=1)

    def stack(self, t):
        t0, t1 = t * self.m0, t * self.m1
        if self.group is not None:
            t0, t1 = self._swap_half(t0, 0), self._swap_half(t1, 1)
        return jnp.concatenate([t0, t1], axis=0)

    def unstack(self, ts):
        if self.group is None:
            return ts[:BLK] * self.m0 + ts[BLK:] * self.m1
        return self._swap_half(ts[:BLK] * self.kv_mask, 0) + self._swap_half(ts[BLK:] * self.kv_mask, 1)


def _rows(st, dil):
    if dil == 1:
        return pl.ds(pl.multiple_of(st, BLK), BLK)
    return pl.ds(st, BLK, stride=dil)


def _block_pos(n, dil):
    nb = SEQ // (dil * BLK)
    r, b = n // nb, n % nb
    hp = (b > 0).astype(jnp.int32)
    st = r + dil * BLK * b
    return st, st - dil * BLK * hp, 1 - hp


def _attn_fwd(proj, qblk, kblk, vblk, dils, gqa, sink_x, name):
    has_sink = sink_x is not None

    def body(*refs):
        if has_sink:
            q_ref, k_ref, v_ref, s_ref, o_ref, lse_ref, m_scr, z_scr, bias_scr = refs
        else:
            q_ref, k_ref, v_ref, o_ref, lse_ref, m_scr, z_scr, bias_scr = refs

        @pl.when(pl.program_id(0) == 0)
        def _():
            _fill_band_bias(bias_scr)
        o_ref[...] = jnp.zeros_like(o_ref)
        if has_sink:
            z_scr[...] = jnp.ones_like(z_scr)
            m_scr[...] = jnp.broadcast_to(s_ref[...], m_scr.shape)
        else:
            z_scr[...] = jnp.zeros_like(z_scr)
            m_scr[...] = jnp.full_like(m_scr, NEG)

        def step(n, carry, dil, heads):
            m0, m1 = heads.m0, heads.m1
            st, stp, first = _block_pos(n, dil)
            rq, rp = _rows(st, dil), _rows(stp, dil)
            kk = jnp.concatenate([k_ref[rp, :], k_ref[rq, :]], axis=0)
            vv = jnp.concatenate([v_ref[rp, :], v_ref[rq, :]], axis=0)
            s = _mm(heads.stack(q_ref[rq, :] * QK_SCALE), kk, NT) + bias_scr[first]
            m = jnp.max(s, axis=1, keepdims=True)
            p = jnp.exp(s - m)
            l = jnp.sum(p, axis=1, keepdims=True)
            o_pair = heads.unstack(_mm(p, vv))
            m_pair = m[:BLK] * m0 + m[BLK:] * m1
            l_pair = l[:BLK] * m0 + l[BLK:] * m1
            m_old = m_scr[rq, :]
            m_new = jnp.maximum(m_old, m_pair)
            alpha, beta = jnp.exp(m_old - m_new), jnp.exp(m_pair - m_new)
            o_ref[rq, :] = o_ref[rq, :] * alpha + o_pair * beta
            z_scr[rq, :] = z_scr[rq, :] * alpha + l_pair * beta
            m_scr[rq, :] = m_new
            return carry

        def blocks(heads):
            for dil in dils:
                lax.fori_loop(0, SEQ // BLK, lambda n, carry, dil=dil: step(n, carry, dil, heads), 0, unroll=8)

        if gqa:
            for grp in range(2):
                pl.when(pl.program_id(0) // 2 == grp)(lambda grp=grp: blocks(_HeadStack(grp)))
        else:
            blocks(_HeadStack(None))

        def fin(t, carry):
            rt = pl.ds(pl.multiple_of(t * TM, TM), TM)
            z = z_scr[rt, :]
            o_ref[rt, :] = o_ref[rt, :] / z
            lse_ref[rt, :] = m_scr[rt, :] + jnp.log(z)
            return carry
        lax.fori_loop(0, SEQ // TM, fin, 0)

    col = lambda blk: pl.BlockSpec((SEQ, LANES), lambda p, blk=blk: (0, blk + p))
    kv = (lambda blk: pl.BlockSpec((SEQ, LANES), lambda p, blk=blk: (0, blk))) if gqa else col
    in_specs = [col(qblk), kv(kblk), kv(vblk)]
    args = [proj, proj, proj]
    if has_sink:
        in_specs.append(pl.BlockSpec((1, LANES), lambda p: (0, p)))
        args.append(sink_x)
    out = pl.BlockSpec((SEQ, LANES), lambda p: (0, p))
    return pl.pallas_call(body, name=name, out_shape=(_sds((SEQ, 512)), _sds((SEQ, 512))), grid=(4,),
                          in_specs=in_specs, out_specs=(out, out),
                          scratch_shapes=[pltpu.VMEM((SEQ, LANES), F32), pltpu.VMEM((SEQ, LANES), F32), _BIAS],
                          compiler_params=_cp(48))(*args)


def _attn_bwd(proj, qblk, kblk, vblk, do, o, lse, dils, gqa, sink_x, name):
    has_sink = sink_x is not None

    def body(*refs):
        if has_sink:
            q_ref, k_ref, v_ref, do_ref, o_ref, lse_ref, s_ref, dq_ref, dk_ref, dv_ref, ds_ref, bias_scr = refs
        else:
            q_ref, k_ref, v_ref, do_ref, o_ref, lse_ref, dq_ref, dk_ref, dv_ref, bias_scr = refs
        pid = pl.program_id(0)

        @pl.when(pid == 0)
        def _():
            _fill_band_bias(bias_scr)
        dq_ref[...] = jnp.zeros_like(dq_ref)
        if gqa:
            @pl.when(pid == 0)
            def _():
                dk_ref[...] = jnp.zeros_like(dk_ref)
                dv_ref[...] = jnp.zeros_like(dv_ref)
        else:
            dk_ref[...] = jnp.zeros_like(dk_ref)
            dv_ref[...] = jnp.zeros_like(dv_ref)

        def step(n, carry, dil, heads):
            m0, m1 = heads.m0, heads.m1
            st, stp, first = _block_pos(n, dil)
            rq, rp = _rows(st, dil), _rows(stp, dil)
            do_, lse_ = do_ref[rq, :], lse_ref[rq, :]
            kk = jnp.concatenate([k_ref[rp, :], k_ref[rq, :]], axis=0)
            vv = jnp.concatenate([v_ref[rp, :], v_ref[rq, :]], axis=0)
            qs, dos = heads.stack(q_ref[rq, :] * QK_SCALE), heads.stack(do_)
            doo = do_ * o_ref[rq, :]
            delta = jnp.concatenate([jnp.sum(doo * m0, axis=1, keepdims=True), jnp.sum(doo * m1, axis=1, keepdims=True)], axis=0)
            lse_s = jnp.concatenate([lse_[:, 0:1], lse_[:, HD:HD + 1]], axis=0)
            p = jnp.exp(_mm(qs, kk, NT) + bias_scr[first] - lse_s)
            ds = p * (_mm(dos, vv, NT) - delta)
            dq_ref[rq, :] += heads.unstack(_mm(ds, kk)) * QK_SCALE
            dk_sum, dv_sum = _mm(ds, qs, TN), _mm(p, dos, TN)
            dk_ref[rp, :] += dk_sum[:BLK]
            dk_ref[rq, :] += dk_sum[BLK:]
            dv_ref[rp, :] += dv_sum[:BLK]
            dv_ref[rq, :] += dv_sum[BLK:]
            return carry

        def blocks(heads):
            for dil in dils:
                lax.fori_loop(0, SEQ // BLK, lambda n, carry, dil=dil: step(n, carry, dil, heads), 0, unroll=4)

        if gqa:
            for grp in range(2):
                pl.when(pid // 2 == grp)(lambda grp=grp: blocks(_HeadStack(grp)))
        else:
            blocks(_HeadStack(None))

        if has_sink:
            m0, m1 = _half_masks()

            def sink_rows(t, acc):
                rt = pl.ds(pl.multiple_of(t * TM, TM), TM)
                return acc - jnp.sum(jnp.exp(s_ref[...] - lse_ref[rt, :]) * (do_ref[rt, :] * o_ref[rt, :]), axis=0, keepdims=True)
            acc = lax.fori_loop(0, SEQ // TM, sink_rows, jnp.zeros((1, LANES), F32))
            per_head = jnp.sum(acc * m0, axis=1, keepdims=True) * m0 + jnp.sum(acc * m1, axis=1, keepdims=True) * m1
            ds_ref[0] = jnp.broadcast_to(per_head, (8, LANES))

    col = lambda blk: pl.BlockSpec((SEQ, LANES), lambda p, blk=blk: (0, blk + p))
    kv = (lambda blk: pl.BlockSpec((SEQ, LANES), lambda p, blk=blk: (0, blk))) if gqa else col
    pair = pl.BlockSpec((SEQ, LANES), lambda p: (0, p))
    in_specs = [col(qblk), kv(kblk), kv(vblk), pair, pair, pair]
    args = [proj, proj, proj, do, o, lse]
    kvw = LANES if gqa else 512
    kv_out = pl.BlockSpec((SEQ, LANES), lambda p: (0, 0)) if gqa else pair
    out_shape = [_sds((SEQ, 512)), _sds((SEQ, kvw)), _sds((SEQ, kvw))]
    out_specs = [pair, kv_out, kv_out]
    if has_sink:
        in_specs.append(pl.BlockSpec((1, LANES), lambda p: (0, p)))
        args.append(sink_x)
        out_shape.append(_sds((4, 8, LANES)))
        out_specs.append(pl.BlockSpec((1, 8, LANES), lambda p: (p, 0, 0)))
    return pl.pallas_call(body, name=name, out_shape=tuple(out_shape), grid=(4,), in_specs=in_specs,
                          out_specs=tuple(out_specs), scratch_shapes=[_BIAS], compiler_params=_cp(56))(*args)


_CT = 128


def _rows_before(x_ref, t, k):
    if t == 0:
        return jnp.concatenate([jnp.zeros((k, LANES), F32), x_ref[0:_CT - k, :]], axis=0)
    return x_ref[t * _CT - k:(t + 1) * _CT - k, :]


def _conv_pre(x_ref, w_ref, b_ref, t):
    taps = [x_ref[t * _CT:(t + 1) * _CT, :]] + [_rows_before(x_ref, t, k) for k in range(1, 4)]
    u = b_ref[...] + taps[0] * w_ref[3:4, :]
    for k in range(1, 4):
        u = u + taps[k] * w_ref[3 - k:4 - k, :]
    return u, taps


def _conv_fwd(proj, w, b, name):
    def body(x_ref, w_ref, b_ref, o_ref):
        for t in range(SEQ // _CT):
            o_ref[t * _CT:(t + 1) * _CT, :] = _silu(_conv_pre(x_ref, w_ref, b_ref, t)[0])

    nblk = CONV_CH // LANES
    return pl.pallas_call(body, name=name, out_shape=_sds((SEQ, CONV_CH)), grid=(nblk,),
                          in_specs=[pl.BlockSpec((SEQ, LANES), lambda j: (0, XBC // LANES + j)),
                                    pl.BlockSpec((4, LANES), lambda j: (0, j)), pl.BlockSpec((1, LANES), lambda j: (0, j))],
                          out_specs=pl.BlockSpec((SEQ, LANES), lambda j: (0, j)), compiler_params=_cp())(proj, w, b)


def _conv_bwd(proj, dact, w, b, name):
    def body(x_ref, da_ref, w_ref, b_ref, dx_ref, dw_ref, db_ref, du_scr):
        du_scr[SEQ:SEQ + 8, :] = jnp.zeros((8, LANES), F32)
        db = jnp.zeros((1, LANES), F32)
        dws = [jnp.zeros((1, LANES), F32)] * 4
        for t in range(SEQ // _CT):
            u, taps = _conv_pre(x_ref, w_ref, b_ref, t)
            du = da_ref[t * _CT:(t + 1) * _CT, :] * _dsilu(u)
            du_scr[t * _CT:(t + 1) * _CT, :] = du
            db = db + jnp.sum(du, axis=0, keepdims=True)
            dws = [dws[k] + jnp.sum(du * taps[k], axis=0, keepdims=True) for k in range(4)]
        db_ref[...] = db
        for k in range(4):
            dw_ref[3 - k:4 - k, :] = dws[k]
        for t in range(SEQ // _CT):
            dx = du_scr[t * _CT:(t + 1) * _CT, :] * w_ref[3:4, :]
            for k in range(1, 4):
                dx = dx + du_scr[t * _CT + k:(t + 1) * _CT + k, :] * w_ref[3 - k:4 - k, :]
            dx_ref[t * _CT:(t + 1) * _CT, :] = dx.astype(dx_ref.dtype)

    nblk = CONV_CH // LANES
    blk = pl.BlockSpec((SEQ, LANES), lambda j: (0, j))
    wspec, bspec = pl.BlockSpec((4, LANES), lambda j: (0, j)), pl.BlockSpec((1, LANES), lambda j: (0, j))
    return pl.pallas_call(body, name=name, out_shape=(_sds((SEQ, CONV_CH), MXU), _sds((4, CONV_CH)), _sds((1, CONV_CH))), grid=(nblk,),
                          in_specs=[pl.BlockSpec((SEQ, LANES), lambda j: (0, XBC // LANES + j)), blk, wspec, bspec],
                          out_specs=(blk, wspec, bspec), scratch_shapes=[pltpu.VMEM((SEQ + 8, LANES), F32)],
                          compiler_params=_cp())(proj, dact, w, b)


def _ssd_chunk(xs, bm, cm, dtr, z, hs, al16, dtb, dskx, nw):
    m0, m1 = _half_masks()
    row = lax.broadcasted_iota(jnp.int32, (BLK, BLK), 0)
    col = lax.broadcasted_iota(jnp.int32, (BLK, BLK), 1)
    causal = row >= col
    tril = causal.astype(F32)
    lane = lax.broadcasted_iota(jnp.int32, (1, LANES), 1)
    sub = lax.broadcasted_iota(jnp.int32, (BLK, 1), 0)
    last_row = (sub == BLK - 1).astype(F32)
    dt = jnp.where(lane < 16, _softplus(dtr + dtb), 0.0)
    a16 = -jnp.exp(al16)
    acum = jnp.dot(tril, dt * a16, precision=HI, preferred_element_type=F32)
    acum_t = acum.T
    gmat = [_mm(cm[g], bm[g], NT) for g in range(2)]
    ys, hn = [], []
    for p in range(8):
        g = p // 4
        pick = [(lane == 2 * p + a).astype(F32) for a in range(2)]
        col_h = [jnp.sum(acum * pick[a], axis=1, keepdims=True) for a in range(2)]
        dt_x = sum(jnp.sum(dt * pick[a], axis=1, keepdims=True) * msk for a, msk in enumerate((m0, m1)))
        ac_x = col_h[0] * m0 + col_h[1] * m1
        a_end = jnp.sum(ac_x * last_row, axis=0, keepdims=True)
        xdt = xs[p] * dt_x
        y = _mm(cm[g], hs[p]) * jnp.exp(ac_x)
        for a, msk in enumerate((m0, m1)):
            row_h = jnp.sum(acum_t * (sub == 2 * p + a).astype(F32), axis=0, keepdims=True)
            decay = jnp.exp(jnp.where(causal, col_h[a] - row_h, NEG))
            y = y + _mm(gmat[g] * decay, xdt * msk)
        st = _mm(bm[g], xdt * jnp.exp(a_end - ac_x), TN)
        hn.append(hs[p] * jnp.exp(a_end) + st)
        y = y + dskx[p] * xs[p]
        ys.append(y * _silu(z[p]))
    out = []
    for g in range(2):
        ms = sum(jnp.sum(ys[p] * ys[p], axis=1, keepdims=True) for p in range(4 * g, 4 * g + 4)) * (1.0 / 512)
        rstd = lax.rsqrt(ms + EPS)
        out += [ys[p] * rstd * nw[p] for p in range(4 * g, 4 * g + 4)]
    return out, hn


def _tiles(ref, n, off=0):
    return [ref[:, off + LANES * p:off + LANES * (p + 1)] for p in range(n)]


def _ssd_load(xbc_ref, z_ref, dt_ref, al16_ref, dtb_ref, dsk_ref, nw_ref):
    return (_tiles(xbc_ref, 8), _tiles(xbc_ref, 2, 1024), _tiles(xbc_ref, 2, 1280), dt_ref[...], _tiles(z_ref, 8)), \
           (al16_ref[...], dtb_ref[...], _tiles(dsk_ref, 8), _tiles(nw_ref, 8))


_NCH = SEQ // BLK


def _ssd_param_specs():
    return [_full((1, LANES)), _full((1, LANES)), _full((1, 1024)), _full((1, 1024))]


def _ssd_fwd(xbc_act, proj, al16, dtb, dskx, nw, name):
    def body(xbc_ref, z_ref, dt_ref, al16_ref, dtb_ref, dsk_ref, nw_ref, y_ref, hin_ref, h_scr):
        @pl.when(pl.program_id(0) == 0)
        def _():
            h_scr[...] = jnp.zeros_like(h_scr)
        acts, params = _ssd_load(xbc_ref, z_ref, dt_ref, al16_ref, dtb_ref, dsk_ref, nw_ref)
        hs = _tiles(h_scr, 8)
        hin_ref[0] = h_scr[...]
        ys, hn = _ssd_chunk(*acts, hs, *params)
        for p in range(8):
            y_ref[:, LANES * p:LANES * (p + 1)] = ys[p].astype(y_ref.dtype)
            h_scr[:, LANES * p:LANES * (p + 1)] = hn[p]

    return pl.pallas_call(
        body, name=name, out_shape=(_sds((SEQ, 1024), MXU), _sds((_NCH, BLK, 1024))), grid=(_NCH,),
        in_specs=[pl.BlockSpec((BLK, CONV_CH), lambda c: (c, 0)), pl.BlockSpec((BLK, 1024), lambda c: (c, ZB // 1024)),
                  pl.BlockSpec((BLK, LANES), lambda c: (c, DTC // LANES))] + _ssd_param_specs(),
        out_specs=(pl.BlockSpec((BLK, 1024), lambda c: (c, 0)), pl.BlockSpec((1, BLK, 1024), lambda c: (c, 0, 0))),
        scratch_shapes=[pltpu.VMEM((BLK, 1024), F32)], compiler_params=_cp())(xbc_act, proj, proj, al16, dtb, dskx, nw)


def _ssd_bwd(xbc_act, proj, hin, dyb, al16, dtb, dskx, nw, name):
    def body(xbc_ref, z_ref, dt_ref, hin_ref, dy_ref, al16_ref, dtb_ref, dsk_ref, nw_ref,
             dxbc_ref, dz_ref, ddt_ref, dal16_ref, ddtb_ref, ddsk_ref, dnw_ref, dh_scr):
        @pl.when(pl.program_id(0) == 0)
        def _():
            dh_scr[...] = jnp.zeros_like(dh_scr)
            for r in (dal16_ref, ddtb_ref, ddsk_ref, dnw_ref):
                r[...] = jnp.zeros_like(r)
        acts, params = _ssd_load(xbc_ref, z_ref, dt_ref, al16_ref, dtb_ref, dsk_ref, nw_ref)
        hs = [hin_ref[0, :, LANES * p:LANES * (p + 1)] for p in range(8)]
        _, vjp = jax.vjp(lambda a, h, q: _ssd_chunk(*a, h, *q), acts, hs, params)
        (dxs, dbm, dcm, ddt, dz), dhs, (dal16, ddtb, ddsk, dnw) = vjp((_tiles(dy_ref, 8), _tiles(dh_scr, 8)))
        for p in range(8):
            cols = slice(LANES * p, LANES * (p + 1))
            dxbc_ref[:, cols] = dxs[p]
            dz_ref[:, cols] = dz[p].astype(dz_ref.dtype)
            dh_scr[:, cols] = dhs[p]
            ddsk_ref[:, cols] += ddsk[p]
            dnw_ref[:, cols] += dnw[p]
        for g in range(2):
            dxbc_ref[:, 1024 + LANES * g:1024 + LANES * (g + 1)] = dbm[g]
            dxbc_ref[:, 1280 + LANES * g:1280 + LANES * (g + 1)] = dcm[g]
        ddt_ref[...] = ddt.astype(ddt_ref.dtype)
        dal16_ref[...] += dal16
        ddtb_ref[...] += ddtb

    rev = lambda c: _NCH - 1 - c
    return pl.pallas_call(
        body, name=name,
        out_shape=(_sds((SEQ, CONV_CH)), _sds((SEQ, 1024), MXU), _sds((SEQ, LANES), MXU),
                   _sds((1, LANES)), _sds((1, LANES)), _sds((1, 1024)), _sds((1, 1024))),
        grid=(_NCH,),
        in_specs=[pl.BlockSpec((BLK, CONV_CH), lambda c: (rev(c), 0)), pl.BlockSpec((BLK, 1024), lambda c: (rev(c), ZB // 1024)),
                  pl.BlockSpec((BLK, LANES), lambda c: (rev(c), DTC // LANES)), pl.BlockSpec((1, BLK, 1024), lambda c: (rev(c), 0, 0)),
                  pl.BlockSpec((BLK, 1024), lambda c: (rev(c), 0))] + _ssd_param_specs(),
        out_specs=(pl.BlockSpec((BLK, CONV_CH), lambda c: (rev(c), 0)), pl.BlockSpec((BLK, 1024), lambda c: (rev(c), 0)),
                   pl.BlockSpec((BLK, LANES), lambda c: (rev(c), 0)),
                   _full((1, LANES)), _full((1, LANES)), _full((1, 1024)), _full((1, 1024))),
        scratch_shapes=[pltpu.VMEM((BLK, 1024), F32)], compiler_params=_cp())(xbc_act, proj, proj, hin, dyb, al16, dtb, dskx, nw)


def _rstd(v):
    return lax.rsqrt(jnp.mean(v * v, axis=1, keepdims=True) + EPS)


def _rms_bwd(dn, n, rstd):
    return rstd * (dn - n * jnp.mean(dn * n, axis=1, keepdims=True))


_VEC = _full((1, D))


def _layer_spec(layer):
    return pl.BlockSpec((None, 2048, D), lambda *_: (layer, 0, 0))

_ROW = pl.BlockSpec((TM, D), lambda i, *_: (i, 0))


def _proj_fwd(x, pre_w, scale, shift, w, layer, name):
    tn, ni = 1024, SEQ // TM

    def body(x_ref, pw_ref, sc_ref, sh_ref, w_ref, o_ref, h_ref, h_scr):
        rows = pl.ds(pl.multiple_of(pl.program_id(1) * TM, TM), TM)

        @pl.when(pl.program_id(0) == 0)
        def _():
            xv = x_ref[...]
            h = ((xv * _rstd(xv) * pw_ref[...]) * (1.0 + sc_ref[...]) + sh_ref[...]).astype(h_ref.dtype)
            h_scr[rows, :] = h
            h_ref[...] = h
        o_ref[...] = jnp.dot(h_scr[rows, :], w_ref[...].astype(MXU), preferred_element_type=F32)

    first_pass = pl.BlockSpec((TM, D), lambda j, i: (jnp.where(j == 0, i, ni - 1), 0))
    return pl.pallas_call(body, name=name, out_shape=(_sds((SEQ, NP)), _sds((SEQ, D), MXU)), grid=(NP // tn, ni),
                          in_specs=[first_pass, _VEC, _VEC, _VEC, pl.BlockSpec((None, D, tn), lambda j, i: (layer, 0, j))],
                          out_specs=(pl.BlockSpec((TM, tn), lambda j, i: (i, j)), first_pass),
                          scratch_shapes=[pltpu.VMEM((SEQ, D), MXU)], compiler_params=_cp())(x, pre_w, scale, shift, w)


_HALF = pl.BlockSpec((TM, 512), lambda i: (i, 0))
_Z_A = pl.BlockSpec((TM, 512), lambda i: (i, ZA // 512))
_Z_C = pl.BlockSpec((TM, 512), lambda i: (i, ZC // 512))


def _out_fwd(o_a, yb, o_c, proj, w, layer, x, gate, post_w, name):
    def body(oa_ref, yb_ref, oc_ref, za_ref, zc_ref, w_ref, x_ref, g_ref, pw_ref, xn_ref, y_ref):
        y = (_mm(oa_ref[...] * _silu(za_ref[...]), w_ref[0:512, :]) + _mm(yb_ref[...], w_ref[512:1536, :])
             + _mm(oc_ref[...] * _silu(zc_ref[...]), w_ref[1536:2048, :]))
        y_ref[...] = y
        xn_ref[...] = x_ref[...] + g_ref[...] * (y * _rstd(y) * pw_ref[...])

    return pl.pallas_call(body, name=name, out_shape=(_sds((SEQ, D)), _sds((SEQ, D))), grid=(SEQ // TM,),
                          in_specs=[_HALF, _ROW, _HALF, _Z_A, _Z_C, _layer_spec(layer), _ROW, _VEC, _VEC],
                          out_specs=(_ROW, _ROW), compiler_params=_cp())(o_a, yb, o_c, proj, proj, w, x, gate, post_w)


def _post_bwd(dxo, y, gate, post_w, name):
    def body(dx_ref, y_ref, g_ref, pw_ref, dy_ref, dg_ref, dpw_ref):
        @pl.when(pl.program_id(0) == 0)
        def _():
            dg_ref[...] = jnp.zeros_like(dg_ref)
            dpw_ref[...] = jnp.zeros_like(dpw_ref)
        dx, y = dx_ref[...], y_ref[...]
        rstd = _rstd(y)
        n = y * rstd
        dg_ref[...] += jnp.sum(dx * (n * pw_ref[...]), axis=0, keepdims=True)
        dr = dx * g_ref[...]
        dpw_ref[...] += jnp.sum(dr * n, axis=0, keepdims=True)
        dy_ref[...] = _rms_bwd(dr * pw_ref[...], n, rstd)

    return pl.pallas_call(body, name=name, out_shape=(_sds((SEQ, D)), _sds((1, D)), _sds((1, D))), grid=(SEQ // TM,),
                          in_specs=[_ROW, _ROW, _VEC, _VEC], out_specs=(_ROW, _VEC, _VEC), compiler_params=_cp())(dxo, y, gate, post_w)


def _dymix(dy, w, layer, o_a, o_c, proj, name):
    def body(dy_ref, w_ref, oa_ref, oc_ref, za_ref, zc_ref, doa_ref, dza_ref, b_ref, doc_ref, dzc_ref):
        dy = dy_ref[...]
        b_ref[...] = _mm(dy, w_ref[512:1536, :], NT)
        for rows, o_ref, z_ref, do_ref, dz_ref in ((slice(0, 512), oa_ref, za_ref, doa_ref, dza_ref),
                                                   (slice(1536, 2048), oc_ref, zc_ref, doc_ref, dzc_ref)):
            dyg, z = _mm(dy, w_ref[rows, :], NT), z_ref[...]
            do_ref[...] = dyg * _silu(z)
            dz_ref[...] = (dyg * o_ref[...] * _dsilu(z)).astype(dz_ref.dtype)

    return pl.pallas_call(body, name=name, out_shape=(_sds((SEQ, 512)), _sds((SEQ, 512), MXU), _sds((SEQ, D)), _sds((SEQ, 512)), _sds((SEQ, 512), MXU)),
                          grid=(SEQ // TM,), in_specs=[_ROW, _layer_spec(layer), _HALF, _HALF, _Z_A, _Z_C],
                          out_specs=(_HALF, _HALF, _ROW, _HALF, _HALF), compiler_params=_cp())(dy, w, o_a, o_c, proj, proj)


def _dwout(o_a, yb, o_c, proj, dy, name):
    def body(oa_ref, yb_ref, oc_ref, za_ref, zc_ref, dy_ref, o_ref):
        @pl.when(pl.program_id(0) == 0)
        def _():
            o_ref[...] = jnp.zeros_like(o_ref)
        dy = dy_ref[...]
        o_ref[0:512, :] += _mm(oa_ref[...] * _silu(za_ref[...]), dy, TN)
        o_ref[512:1536, :] += _mm(yb_ref[...], dy, TN)
        o_ref[1536:2048, :] += _mm(oc_ref[...] * _silu(zc_ref[...]), dy, TN)

    return pl.pallas_call(body, name=name, out_shape=_sds((2048, D)), grid=(SEQ // TM,),
                          in_specs=[_HALF, _ROW, _HALF, _Z_A, _Z_C, _ROW], out_specs=_full((2048, D)),
                          compiler_params=_cp())(o_a, yb, o_c, proj, proj, dy)


def _dwin(h, pieces, name):
    n = len(pieces)
    widths = [p.shape[1] for p in pieces]
    half = NP // 2

    def body(*refs):
        h_ref, p_refs, o_ref = refs[0], refs[1:1 + n], refs[1 + n]

        @pl.when(pl.program_id(0) == 0)
        def _():
            o_ref[...] = jnp.zeros_like(o_ref)
        hv, c0 = h_ref[...], 0
        for p_ref, wd in zip(p_refs, widths):
            o_ref[:, c0:c0 + wd] += _mm(hv, p_ref[...], TN)
            c0 += wd

    return pl.pallas_call(body, name=name, out_shape=_sds((D, half)), grid=(SEQ // TM,),
                          in_specs=[_ROW] + [pl.BlockSpec((TM, wd), lambda k: (k, 0)) for wd in widths],
                          out_specs=_full((D, half)), compiler_params=_cp(56))(h, *pieces)


_TMH = 256


def _dh_bwd(pieces, w, x, pre_w, scale, dxo, name):
    n = len(pieces)
    widths = [p.shape[1] for p in pieces]

    def body(*refs):
        p_refs, (w_ref, x_ref, pw_ref, sc_ref, dxo_ref, dx_ref, dsh_ref, dsc_ref, dpw_ref) = refs[:n], refs[n:]

        @pl.when(pl.program_id(0) == 0)
        def _():
            for r in (dsh_ref, dsc_ref, dpw_ref):
                r[...] = jnp.zeros_like(r)
        dh, c0 = 0.0, 0
        for p_ref, wd in zip(p_refs, widths):
            dh = dh + _mm(p_ref[...], w_ref[:, c0:c0 + wd], NT)
            c0 += wd
        xv = x_ref[...]
        rstd = _rstd(xv)
        nrm = xv * rstd
        dsh_ref[...] += jnp.sum(dh, axis=0, keepdims=True)
        dsc_ref[...] += jnp.sum(dh * (nrm * pw_ref[...]), axis=0, keepdims=True)
        dhn = dh * (1.0 + sc_ref[...])
        dpw_ref[...] += jnp.sum(dhn * nrm, axis=0, keepdims=True)
        dx_ref[...] = _rms_bwd(dhn * pw_ref[...], nrm, rstd) + dxo_ref[...]

    row = pl.BlockSpec((_TMH, D), lambda i: (i, 0))
    return pl.pallas_call(body, name=name, out_shape=(_sds((SEQ, D)), _sds((1, D)), _sds((1, D)), _sds((1, D))),
                          grid=(SEQ // _TMH,),
                          in_specs=[pl.BlockSpec((_TMH, wd), lambda i: (i, 0)) for wd in widths]
                          + [pl.BlockSpec((None, D, NP), lambda i: (0, 0, 0)), row, _VEC, _VEC, row],
                          out_specs=(row, _VEC, _VEC, _VEC), compiler_params=_cp(56))(*pieces, w, x, pre_w, scale, dxo)


def _w_in_padded(land, name):
    rows = 128

    def body(l_ref, o_ref):
        o_ref[...] = _pad_cols(jnp.concatenate([l_ref[k] for k in range(4)], axis=1))

    return pl.pallas_call(body, name=name, out_shape=_sds((D, NP), land.dtype), grid=(D // rows,),
                          in_specs=[pl.BlockSpec((4, rows, SHARD_IN), lambda i: (0, i, 0))],
                          out_specs=pl.BlockSpec((rows, NP), lambda i: (i, 0)), compiler_params=_cp())(land)


def _grad_blocks(dwa, dwb, name):
    rows = 128

    def body(a_ref, b_ref, o_ref):
        g = _unpad_cols(jnp.concatenate([a_ref[...], b_ref[...]], axis=1))
        for k in range(4):
            o_ref[k] = g[:, SHARD_IN * k:SHARD_IN * (k + 1)].astype(o_ref.dtype)

    half = pl.BlockSpec((rows, NP // 2), lambda i: (i, 0))
    return pl.pallas_call(body, name=name, out_shape=_sds((4, D, SHARD_IN), jnp.bfloat16), grid=(D // rows,),
                          in_specs=[half, half], out_specs=pl.BlockSpec((4, rows, SHARD_IN), lambda i: (0, i, 0)),
                          compiler_params=_cp())(dwa, dwb)


def _loss_bwd(xf, tgt, name):
    def body(x_ref, t_ref, dx_ref, l_ref):
        @pl.when(pl.program_id(0) == 0)
        def _():
            l_ref[...] = jnp.zeros_like(l_ref)
        e = x_ref[...] - t_ref[...]
        dx_ref[...] = e * (1.0 / D)
        l_ref[...] += 0.5 * jnp.sum(jnp.mean(e * e, axis=1, keepdims=True), axis=0, keepdims=True)

    return pl.pallas_call(body, name=name, out_shape=(_sds((SEQ, D)), _sds((8, LANES))), grid=(SEQ // TM,),
                          in_specs=[_ROW, _ROW], out_specs=(_ROW, _full((8, LANES))), compiler_params=_cp())(xf, tgt)


def _mod_part(c_all, ada_w, ada_b, name):
    def body(c_ref, w_ref, b_ref, o_ref):
        o_ref[0] = _mm(_silu(c_ref[...]), w_ref[0]) + b_ref[0]

    return pl.pallas_call(body, name=name, out_shape=_sds((DEPTH, 8, 768)), grid=(DEPTH,),
                          in_specs=[_full((8, D)), pl.BlockSpec((1, D, 768), lambda i: (i, 0, 0)), pl.BlockSpec((1, 1, 768), lambda i: (i, 0, 0))],
                          out_specs=pl.BlockSpec((1, 8, 768), lambda i: (i, 0, 0)), compiler_params=_cp())(c_all, ada_w, ada_b)


def _ada_grad(c_t, dmod, name):
    def body(c_ref, d_ref, o_ref):
        ca = _silu(c_ref[...])
        dm = d_ref[0]
        acc = ca[:, 0:1] * dm[0:1, :]
        for s in range(1, 8):
            acc = acc + ca[:, s:s + 1] * dm[s:s + 1, :]
        o_ref[0] = acc

    return pl.pallas_call(body, name=name, out_shape=_sds((DEPTH, D, 768)), grid=(DEPTH,),
                          in_specs=[_full((D, LANES)), pl.BlockSpec((1, 8, 768), lambda i: (i, 0, 0))],
                          out_specs=pl.BlockSpec((1, D, 768), lambda i: (i, 0, 0)), compiler_params=_cp())(c_t, dmod)


def _pack(parts):
    flat = []
    for p in parts:
        f = p.reshape(-1)
        flat.append(jnp.pad(f, (0, (-f.size) % LANES)))
    v = jnp.concatenate(flat)
    return jnp.pad(v, (0, (-v.size) % (8 * LANES))).reshape(-1, LANES)


def _unpack(v, shapes):
    v = v.reshape(-1)
    out, off = [], 0
    for s in shapes:
        n = math.prod(s)
        out.append(v[off:off + n].reshape(s))
        off += n + (-n) % LANES
    return out


_GIVEN_DT, _GIVEN_C = 4608, 4624


def _pad_cols(w):
    return jnp.concatenate([w[..., :_GIVEN_DT], w[..., _GIVEN_C:], w[..., _GIVEN_DT:_GIVEN_C],
                            jnp.zeros(w.shape[:-1] + (NP - IN_COLS,), w.dtype)], axis=-1)


def _unpad_cols(w):
    return jnp.concatenate([w[..., :_GIVEN_DT], w[..., DTC:DTC + 16], w[..., _GIVEN_DT:DTC]], axis=-1)


def _pad_lanes(v):
    return jnp.pad(v, (0, LANES - v.shape[0])).reshape(1, LANES)


def _local_step(x2, tgt, mod, weights_of, grads_done, pre_w, post_w, conv_w, conv_b, dt_bias, a_log, d_skip, nw, sinks):
    saved = []
    xcur = x2
    for i in range(DEPTH):
        shift, scale, gate = mod[i:i + 1, :D], mod[i:i + 1, D:2 * D], mod[i:i + 1, 2 * D:]
        pw, qw = pre_w[i:i + 1], post_w[i:i + 1]
        w_p, w_o = weights_of(i, xcur)
        proj, h = _proj_fwd(xcur, pw, scale, shift, w_p, 0, "proj_fwd")
        o_a, lse_a = _attn_fwd(proj, QA // LANES, KA // LANES, VA // LANES, DILS, False, None, "attn_a_fwd")
        sink_x = jnp.repeat(sinks[i], HD).reshape(1, 512)
        o_c, lse_c = _attn_fwd(proj, QC // LANES, KC // LANES, VC // LANES, (1,), True, sink_x, "attn_c_fwd")
        cw, cb = conv_w[i], conv_b[i:i + 1]
        xbc_act = _conv_fwd(proj, cw, cb, "conv_fwd")
        ssd_p = (_pad_lanes(a_log[i]), _pad_lanes(dt_bias[i]), jnp.repeat(d_skip[i], HD).reshape(1, 1024), nw[i:i + 1])
        yb, hin = _ssd_fwd(xbc_act, proj, *ssd_p, "ssd_fwd")
        xnew, y = _out_fwd(o_a, yb, o_c, proj, w_o, 0, xcur, gate, qw, "out_fwd")
        saved.append((w_p, w_o, xcur, scale, gate, pw, qw, proj, h, o_a, lse_a, sink_x, o_c, lse_c, cw, cb, xbc_act, ssd_p, yb, hin, y))
        xcur = xnew
    dx, ltile = _loss_bwd(xcur, tgt, "loss")
    dmod, small = [None] * DEPTH, [None] * DEPTH
    for i in reversed(range(DEPTH)):
        w_p, w_o, xin, scale, gate, pw, qw, proj, h, o_a, lse_a, sink_x, o_c, lse_c, cw, cb, xbc_act, ssd_p, yb, hin, y = saved[i]
        dy, dgate, dpost = _post_bwd(dx, y, gate, qw, "post_bwd")
        do_a, dz_a, dyb, do_c, dz_c = _dymix(dy, w_o, 0, o_a, o_c, proj, "dymix")
        dwo = _dwout(o_a, yb, o_c, proj, dy, "dwout")
        dq_a, dk_a, dv_a = _attn_bwd(proj, QA // LANES, KA // LANES, VA // LANES, do_a, o_a, lse_a, DILS, False, None, "attn_a_bwd")
        dq_c, dk_c, dv_c, dsk = _attn_bwd(proj, QC // LANES, KC // LANES, VC // LANES, do_c, o_c, lse_c, (1,), True, sink_x, "attn_c_bwd")
        dxbc_act, dz_b, ddt, dal16, ddtb, ddsk, dnw = _ssd_bwd(xbc_act, proj, hin, dyb, *ssd_p, "ssd_bwd")
        dxbc, dcw, dcb = _conv_bwd(proj, dxbc_act, cw, cb, "conv_bwd")
        half_a, half_b = [dq_a, dk_a, dv_a, dz_a, dz_b], [dxbc, dq_c, dz_c, dk_c, dv_c, ddt]
        if i > 0:
            scale = scale + grads_done(i, _dwin(h, half_a, "dwin_a"), _dwin(h, half_b, "dwin_b"), dwo)[0, 0]
        else:
            last = (h, half_a, half_b, dwo)
        dx, dshift, dscale, dpre = _dh_bwd(half_a + half_b, w_p, xin, pw, scale, dx, "dh_bwd")
        dmod[i] = jnp.concatenate([dshift, dscale, dgate], axis=1)
        small[i] = (dpre, dpost, dcw, dcb, ddtb[0, :16], dal16[0, :16], ddsk.reshape(16, HD).sum(axis=1), dnw, dsk[:, 0, ::HD].reshape(8))
    return ltile, dx, jnp.concatenate(dmod, axis=0), small, last


_SMALL = ((1, D), (1, D), (4, CONV_CH), (1, CONV_CH), (16,), (16,), (16,), (1, D), (8,))


def kernel(x, c, ada_w, ada_b, pre_norm_w, post_norm_w, w_in, conv_w, conv_b, dt_bias, a_log, d_skip, ssm_norm_w, sinks, w_out, loss_target, m_ada_w, m_ada_b, m_pre_norm_w, m_post_norm_w, m_w_in, m_conv_w, m_conv_b, m_dt_bias, m_a_log, m_d_skip, m_ssm_norm_w, m_sinks, m_w_out, v_ada_w, v_ada_b, v_pre_norm_w, v_post_norm_w, v_w_in, v_conv_w, v_conv_b, v_dt_bias, v_a_log, v_d_skip, v_ssm_norm_w, v_sinks, v_w_out):
    xi, yi, ci = lax.axis_index("x"), lax.axis_index("y"), lax.axis_index("c")
    chip = 2 * xi + yi
    me = 2 * chip + ci

    w_in_b = _cast_bf16(w_in, 512, "cast_w_in")
    w_out_b = _cast_bf16(w_out, 512, "cast_w_out")
    gathers = []
    for i in range(DEPTH):
        lands = [lax.dynamic_update_slice(lax.empty((4,) + a.shape[1:], a.dtype), a[i][None], (chip, 0, 0)) for a in (w_in_b, w_out_b)]
        gathers.append(_split_start(None, lands, f"gather_start{i}"))
    all_started = gathers[0][3] + gathers[1][3] + gathers[2][3] + gathers[3][3]

    def weights_of(i, after):
        send_sems, recv_sems, thru, _ = gathers[i]
        if i == 0:
            after = all_started + mod[:1, :LANES]
        g_in, g_out = _split_wait(send_sems, recv_sems, thru, 2, after, f"gather_wait{i}")
        return _w_in_padded(g_in, "w_in_padded")[None], g_out.reshape(1, 2048, D)

    scatters = [None] * DEPTH

    def grads_done(i, dwa, dwb, dwo):
        blocks = [_grad_blocks(dwa, dwb, "grad_blocks"), _cast_bf16(dwo.reshape(4, 512, D), 512, "cast_dw_out")]
        scatters[i] = _split_start(blocks, [lax.empty(b.shape, b.dtype) for b in blocks], f"scatter_start{i}")
        return scatters[i][3]

    g0 = _allgather8(_pack([c, conv_w]), "gather_c")
    c_all = g0[:, :8, :].reshape(8, D)
    conv_w_full = jnp.concatenate([g0[2 * k, 8:56, :].reshape(DEPTH, 4, CONV_CH // 4) for k in range(4)], axis=-1)

    ada_b_mine = lax.dynamic_slice_in_dim(ada_b, 768 * chip, 768, axis=1).reshape(DEPTH, 1, 768)
    gm = _allgather8(_mod_part(c_all, ada_w, ada_b_mine, "mod_part").reshape(DEPTH * 8, 768), "gather_mod")
    gm = gm.reshape(4, 2, DEPTH, 8, 768)[:, 0]
    mod = lax.dynamic_index_in_dim(gm, me, axis=2, keepdims=False).transpose(1, 0, 2).reshape(DEPTH, 3 * D)

    ltile, dx, dmod, small, (h0, half_a, half_b, dwo0) = _local_step(
        x[0], loss_target[0], mod, weights_of, grads_done, pre_norm_w, post_norm_w, conv_w_full, conv_b, dt_bias, a_log, d_skip,
        ssm_norm_w, sinks)

    packed = _pack([dmod] + [g for layer in small for g in layer] + [ltile[0]])
    gs = _allgather8(packed, "gather_small")
    h0 = h0 + (gs[0, 0, 0] * 0.0).astype(h0.dtype)
    gs = gs + grads_done(0, _dwin(h0, half_a, "dwin_a"), _dwin(h0, half_b, "dwin_b"), dwo0)[0, 0]
    tot = _sum_blocks(gs[:, None], packed.shape[0], "sum_small")[0]
    parts = _unpack(tot, [(DEPTH, 3 * D)] + list(_SMALL) * DEPTH + [(LANES,)])
    g_ada_b, loss = parts[0], parts[-1][0]
    per_layer = [parts[1 + len(_SMALL) * i:1 + len(_SMALL) * (i + 1)] for i in range(DEPTH)]
    g_pre, g_post, g_cw, g_cb, g_dtb, g_al, g_dsk, g_nw, g_sk = [jnp.stack([per_layer[i][j] for i in range(DEPTH)]) for j in range(len(_SMALL))]
    g_pre, g_post, g_cb, g_nw = g_pre[:, 0], g_post[:, 0], g_cb[:, 0], g_nw[:, 0]
    g_cw = lax.dynamic_slice_in_dim(g_cw, (CONV_CH // 4) * chip, CONV_CH // 4, axis=2)

    dmod_all = gs[:, :(DEPTH * 3 * D) // LANES, :].reshape(8, DEPTH, 3 * D).transpose(1, 0, 2)
    dmod_mine = lax.dynamic_slice_in_dim(dmod_all, 768 * chip, 768, axis=2)
    c_t = jnp.pad(c_all.T, ((0, 0), (0, LANES - 8)))
    g_ada_w = _ada_grad(c_t, dmod_mine, "ada_grad")

    res = {}
    res["ada_w"] = _adamw(ada_w, [g_ada_w], m_ada_w, v_ada_w, 512, "adamw_ada_w")
    names = ["ada_b", "pre_norm_w", "post_norm_w", "conv_w", "conv_b", "dt_bias", "a_log", "d_skip", "ssm_norm_w", "sinks"]
    ws = [ada_b, pre_norm_w, post_norm_w, conv_w, conv_b, dt_bias, a_log, d_skip, ssm_norm_w, sinks]
    gsm = [g_ada_b, g_pre, g_post, g_cw, g_cb, g_dtb, g_al, g_dsk, g_nw, g_sk]
    ms = [m_ada_b, m_pre_norm_w, m_post_norm_w, m_conv_w, m_conv_b, m_dt_bias, m_a_log, m_d_skip, m_ssm_norm_w, m_sinks]
    vs = [v_ada_b, v_pre_norm_w, v_post_norm_w, v_conv_w, v_conv_b, v_dt_bias, v_a_log, v_d_skip, v_ssm_norm_w, v_sinks]
    pw_, pg_, pm_, pv_ = _pack(ws), _pack(gsm), _pack(ms), _pack(vs)
    small_out = _adamw(pw_[None], [pg_[None]], pm_[None], pv_[None], pw_.shape[0], "adamw_small")

    others_done = small_out[1][0, :8] + res["ada_w"][1][0, :8, :LANES]
    landed = [_split_wait(*scatters[i][:3], 2, others_done, f"scatter_wait{i}") for i in range(DEPTH)]
    p_in = _sum_chips([d[2] for d in landed], [d[0] for d in landed], 128, "sum_w_in")
    p_out = _sum_chips([d[3] for d in landed], [d[1] for d in landed], 256, "sum_w_out")
    s_in, s_out = _sibling_swap([p_in, p_out], "swap_partials")
    res["w_in"] = _adamw(w_in, [p_in, s_in], m_w_in, v_w_in, 256, "adamw_w_in")
    res["w_out"] = _adamw(w_out, [p_out, s_out], m_w_out, v_w_out, 512, "adamw_w_out")
    shapes = [w.shape for w in ws]
    for kind in range(4):
        for nm, a in zip(names, _unpack(small_out[kind][0], shapes)):
            res.setdefault(nm, [None] * 4)[kind] = a
    order = ["ada_w", "ada_b", "pre_norm_w", "post_norm_w", "w_in", "conv_w", "conv_b", "dt_bias", "a_log", "d_skip", "ssm_norm_w", "sinks", "w_out"]
    return (loss, dx[None], *[res[n][0] for n in order], *[res[n][1] for n in order], *[res[n][2] for n in order], *[res[n][3] for n in order])
```

```python
import math

import jax
import jax.numpy as jnp
from jax import lax
from jax.experimental import pallas as pl
from jax.experimental.pallas import tpu as pltpu

F32 = jnp.float32
MXU = jnp.bfloat16
HI = lax.Precision.HIGHEST
MESH = pl.DeviceIdType.MESH

SEQ = 4096
D = 1024
DEPTH = 4
HD = 64
QK_SCALE = HD ** -0.5
LANES = 128
BLK = 128
DILS = (1, 4, 16)
NEG = -1e30
EPS = 1e-6
MIB = 1024 * 1024

NP = 6144
QA, KA, VA, ZA = 0, 512, 1024, 1536
ZB, XBC = 2048, 3072
QC, ZC, KC, VC = 4608, 5120, 5632, 5760
DTC = 5888
IN_COLS = 5904
SHARD_IN = IN_COLS // 4
CONV_CH = 1536
TM = 512

ADAM_LR, ADAM_B1, ADAM_B2, ADAM_EPS, ADAM_WD, ADAM_STEP = 0.001, 0.9, 0.999, 1e-08, 0.01, 10

NT = (((1,), (1,)), ((), ()))
TN = (((0,), (0,)), ((), ()))


def _cp(vmem_mib=48):
    return pltpu.CompilerParams(vmem_limit_bytes=vmem_mib * MIB)


def _sds(shape, dtype=F32):
    return jax.ShapeDtypeStruct(shape, dtype)


def _full(shape):
    n = len(shape)
    return pl.BlockSpec(shape, lambda *_: (0,) * n)


def _mm(a, b, dims=None):
    if dims is None:
        return jnp.dot(a.astype(MXU), b.astype(MXU), preferred_element_type=F32)
    return lax.dot_general(a.astype(MXU), b.astype(MXU), dims, preferred_element_type=F32)


def _sigmoid(x):
    return 1.0 / (1.0 + jnp.exp(-x))


def _silu(x):
    return x * _sigmoid(x)


def _dsilu(x):
    s = _sigmoid(x)
    return s * (1.0 + x * (1.0 - s))


def _softplus(x):
    ax = jnp.where(x >= 0, x, -x)
    return jnp.maximum(x, 0.0) + jnp.log1p(jnp.exp(-ax))


def _half_masks():
    lane = lax.broadcasted_iota(jnp.int32, (1, LANES), 1)
    m0 = (lane < HD).astype(F32)
    return m0, 1.0 - m0


def _allgather8(v, name):
    r, cc = v.shape

    def body(v_ref, out_ref, send_sems, recv_sems):
        x, y, c = lax.axis_index("x"), lax.axis_index("y"), lax.axis_index("c")
        me = 4 * x + 2 * y + c
        out_ref[me] = v_ref[...]
        peers = []
        for k in range(1, 8):
            px = 1 - x if k & 4 else x
            py = 1 - y if k & 2 else y
            pc = 1 - c if k & 1 else c
            peers.append((px, py, pc))
        sends = []
        for k, peer in enumerate(peers):
            cp = pltpu.make_async_remote_copy(src_ref=v_ref, dst_ref=out_ref.at[me], send_sem=send_sems.at[k],
                                              recv_sem=recv_sems.at[k], device_id=peer, device_id_type=MESH)
            cp.start()
            sends.append(cp)
        for k, (px, py, pc) in enumerate(peers):
            pltpu.make_async_remote_copy(src_ref=v_ref, dst_ref=out_ref.at[4 * px + 2 * py + pc], send_sem=send_sems.at[k],
                                         recv_sem=recv_sems.at[k], device_id=(px, py, pc), device_id_type=MESH).wait_recv()
        for cp in sends:
            cp.wait_send()

    return pl.pallas_call(
        body, name=name, out_shape=_sds((8, r, cc)),
        in_specs=[pl.BlockSpec(memory_space=pltpu.VMEM)], out_specs=pl.BlockSpec(memory_space=pltpu.VMEM),
        scratch_shapes=[pltpu.SemaphoreType.DMA((7,)), pltpu.SemaphoreType.DMA((7,))],
        compiler_params=_cp(32),
    )(v)


_HBM = pl.BlockSpec(memory_space=pltpu.HBM)
_SEM = pl.BlockSpec(memory_space=pltpu.SEMAPHORE)
_EFFECT = pltpu.SideEffectType.DATAFLOW_SIDE_EFFECTING


def _chip_copies(src_refs, land_refs, send_sems, recv_sems):
    x, y, c = lax.axis_index("x"), lax.axis_index("y"), lax.axis_index("c")
    mine = 2 * x + y
    out = []
    for i, land in enumerate(land_refs):
        for j, (px, py) in enumerate([(1 - x, y), (x, 1 - y), (1 - x, 1 - y)]):
            src = src_refs[i].at[2 * px + py] if src_refs else land.at[mine]
            mk = lambda dst, i=i, j=j, src=src, px=px, py=py: pltpu.make_async_remote_copy(
                src_ref=src, dst_ref=dst, send_sem=send_sems.at[3 * i + j], recv_sem=recv_sems.at[3 * i + j],
                device_id=(px, py, c), device_id_type=MESH)
            out.append((mk(land.at[mine]), mk(land.at[2 * px + py])))
    return out


def _split_start(srcs, lands, name):
    ops = list(srcs or []) + list(lands)
    ns, n = len(srcs or []), len(lands)

    def body(*refs):
        src_refs, land_refs = refs[:ns], refs[ns:ns + n]
        send_sems, recv_sems = refs[ns + n], refs[ns + n + 1]
        for mine_out, _ in _chip_copies(src_refs, land_refs, send_sems, recv_sems):
            mine_out.start()
        refs[-1][...] = jnp.zeros_like(refs[-1])

    sems = pltpu.SemaphoreType.DMA((3 * n,))
    res = pl.pallas_call(
        body, name=name, out_shape=(sems, sems) + tuple(pltpu.HBM(a.shape, a.dtype) for a in ops) + (_sds((8, LANES)),),
        in_specs=[_HBM] * len(ops), out_specs=(_SEM, _SEM) + (_HBM,) * len(ops) + (pl.BlockSpec(memory_space=pltpu.VMEM),),
        input_output_aliases={k: 2 + k for k in range(len(ops))},
        compiler_params=pltpu.CompilerParams(has_side_effects=_EFFECT),
    )(*[pltpu.with_memory_space_constraint(a, pltpu.HBM) for a in ops])
    return res[0], res[1], list(res[2:2 + len(ops)]), res[-1]


def _split_wait(send_sems, recv_sems, thru, n, after, name):
    ns = len(thru) - n

    def body(*refs):
        src_refs, land_refs = refs[:ns], refs[ns:ns + n]
        for mine_out, arriving in _chip_copies(src_refs, land_refs, refs[ns + n], refs[ns + n + 1]):
            mine_out.wait_send()
            arriving.wait_recv()

    res = pl.pallas_call(
        body, name=name, out_shape=tuple(pltpu.HBM(a.shape, a.dtype) for a in thru),
        in_specs=[_HBM] * len(thru) + [_SEM, _SEM, pl.BlockSpec(memory_space=pl.ANY)], out_specs=(_HBM,) * len(thru),
        input_output_aliases={k: k for k in range(len(thru))},
        compiler_params=pltpu.CompilerParams(has_side_effects=_EFFECT),
    )(*thru, send_sems, recv_sems, after)
    return list(res)


def _sibling_swap(arrs, name):
    n = len(arrs)

    def body(*refs):
        ins, outs_, (send_sems, recv_sems) = refs[:n], refs[n:2 * n], refs[2 * n:]
        sib = (lax.axis_index("x"), lax.axis_index("y"), 1 - lax.axis_index("c"))
        cps = [pltpu.make_async_remote_copy(src_ref=ins[i], dst_ref=outs_[i], send_sem=send_sems.at[i], recv_sem=recv_sems.at[i],
                                            device_id=sib, device_id_type=MESH) for i in range(n)]
        for cp in cps:
            cp.start()
        for cp in cps:
            cp.wait_recv()
        for cp in cps:
            cp.wait_send()

    hbm = pl.BlockSpec(memory_space=pltpu.HBM)
    return pl.pallas_call(
        body, name=name, out_shape=tuple(_sds(a.shape, a.dtype) for a in arrs), in_specs=[hbm] * n, out_specs=tuple([hbm] * n),
        scratch_shapes=[pltpu.SemaphoreType.DMA((n,)), pltpu.SemaphoreType.DMA((n,))],
    )(*arrs)


def _tile_spec(rows, cc):
    return pl.BlockSpec((None, rows, cc), lambda l, i: (l, i, 0))


def _cast_bf16(a, rows, name):
    nl, r, cc = a.shape

    def body(a_ref, o_ref):
        o_ref[...] = a_ref[...].astype(jnp.bfloat16)

    return pl.pallas_call(body, name=name, out_shape=_sds((nl, r, cc), jnp.bfloat16), grid=(nl, r // rows),
                          in_specs=[_tile_spec(rows, cc)], out_specs=_tile_spec(rows, cc), compiler_params=_cp())(a)


def _sum_blocks(a, rows, name):
    k, nl, r, cc = a.shape

    def body(a_ref, o_ref):
        acc = a_ref[0].astype(F32)
        for j in range(1, k):
            acc = acc + a_ref[j].astype(F32)
        o_ref[...] = acc

    return pl.pallas_call(body, name=name, out_shape=_sds((nl, r, cc)), grid=(nl, r // rows),
                          in_specs=[pl.BlockSpec((k, None, rows, cc), lambda l, i: (0, l, i, 0))],
                          out_specs=_tile_spec(rows, cc), compiler_params=_cp())(a)


def _sum_chips(lands, srcs, rows, name):
    nl = len(lands)
    _, r, cc = lands[0].shape

    def body(*refs):
        land_refs, src_refs, o_ref = refs[:nl], refs[nl:2 * nl], refs[2 * nl]
        mine = 2 * lax.axis_index("x") + lax.axis_index("y")
        for j in range(nl):
            @pl.when(pl.program_id(0) == j)
            def _(j=j):
                own = src_refs[j][mine].astype(F32)
                acc = None
                for k in range(4):
                    term = jnp.where(mine == k, own, land_refs[j][k].astype(F32))
                    acc = term if acc is None else acc + term
                o_ref[...] = acc

    specs = [pl.BlockSpec((4, rows, cc), lambda l, i, j=j: (0, jnp.where(l == j, i, 0), 0)) for j in range(nl)]
    return pl.pallas_call(body, name=name, out_shape=_sds((nl, r, cc)), grid=(nl, r // rows),
                          in_specs=specs + specs, out_specs=_tile_spec(rows, cc), compiler_params=_cp())(*lands, *srcs)


def _adamw(w, parts, m, v, rows, name, lead=None):
    nl, r, cc = w.shape
    np_ = len(parts)
    c1 = 1.0 / (1.0 - ADAM_B1 ** ADAM_STEP)
    c2 = 1.0 / (1.0 - ADAM_B2 ** ADAM_STEP)

    def body(*refs):
        w_ref, p_refs, (m_ref, v_ref, g_ref, d_ref, nm_ref, nv_ref) = refs[0], refs[1:1 + np_], refs[1 + np_:]
        g = p_refs[0][...]
        for p_ref in p_refs[1:]:
            g = g + p_ref[...]
        nm = ADAM_B1 * m_ref[...] + (1.0 - ADAM_B1) * g
        nv = ADAM_B2 * v_ref[...] + (1.0 - ADAM_B2) * (g * g)
        g_ref[...] = g
        nm_ref[...] = nm
        nv_ref[...] = nv
        d_ref[...] = -ADAM_LR * ((nm * c1) / (jnp.sqrt(nv * c2) + ADAM_EPS) + ADAM_WD * w_ref[...])

    if lead is None:
        spec, grid = _tile_spec(rows, cc), (nl, r // rows)
    else:
        spec, grid = pl.BlockSpec((lead, r, cc), lambda i: (i, 0, 0)), (nl // lead,)
    return pl.pallas_call(body, name=name, out_shape=(_sds((nl, r, cc)),) * 4, grid=grid,
                          in_specs=[spec] * (3 + np_), out_specs=(spec,) * 4, compiler_params=_cp())(w, *parts, m, v)


_BIAS = pltpu.VMEM((2, 2 * BLK, 2 * BLK), F32)


def _fill_band_bias(bias_ref):
    qi = lax.broadcasted_iota(jnp.int32, (2 * BLK, 2 * BLK), 0) & (BLK - 1)
    kj = lax.broadcasted_iota(jnp.int32, (2 * BLK, 2 * BLK), 1)
    dist = BLK + qi - kj
    band = (dist >= 0) & (dist <= BLK)
    bias_ref[0] = jnp.where(band, 0.0, NEG)
    bias_ref[1] = jnp.where(band & (kj >= BLK), 0.0, NEG)


class _HeadStack:
    def __init__(self, group):
        self.m0, self.m1 = _half_masks()
        self.group = group
        if group is not None:
            self.kv_mask = (self.m0, self.m1)[group]

    def _swap_half(self, t, a):
        return t if a == self.group else pltpu.roll(t, HD, axis=1)

    def stack(self, t):
        t0, t1 = t * self.m0, t * self.m1
        if self.group is not None:
            t0, t1 = self._swap_half(t0, 0), self._swap_half(t1, 1)
        return jnp.concatenate([t0, t1], axis=0)

    def unstack(self, ts):
        if self.group is None:
            return ts[:BLK] * self.m0 + ts[BLK:] * self.m1
        return self._swap_half(ts[:BLK] * self.kv_mask, 0) + self._swap_half(ts[BLK:] * self.kv_mask, 1)


def _rows(st, dil):
    if dil == 1:
        return pl.ds(pl.multiple_of(st, BLK), BLK)
    return pl.ds(st, BLK, stride=dil)


def _block_pos(n, dil):
    nb = SEQ // (dil * BLK)
    r, b = n // nb, n % nb
    hp = (b > 0).astype(jnp.int32)
    st = r + dil * BLK * b
    return st, st - dil * BLK * hp, 1 - hp


def _attn_fwd(proj, qblk, kblk, vblk, dils, gqa, sink_x, name):
    has_sink = sink_x is not None

    def body(*refs):
        if has_sink:
            q_ref, k_ref, v_ref, s_ref, o_ref, lse_ref, m_scr, z_scr, bias_scr = refs
        else:
            q_ref, k_ref, v_ref, o_ref, lse_ref, m_scr, z_scr, bias_scr = refs

        @pl.when(pl.program_id(0) == 0)
        def _():
            _fill_band_bias(bias_scr)
        o_ref[...] = jnp.zeros_like(o_ref)
        if has_sink:
            z_scr[...] = jnp.ones_like(z_scr)
            m_scr[...] = jnp.broadcast_to(s_ref[...], m_scr.shape)
        else:
            z_scr[...] = jnp.zeros_like(z_scr)
            m_scr[...] = jnp.full_like(m_scr, NEG)

        def step(n, carry, dil, heads):
            m0, m1 = heads.m0, heads.m1
            st, stp, first = _block_pos(n, dil)
            rq, rp = _rows(st, dil), _rows(stp, dil)
            kk = jnp.concatenate([k_ref[rp, :], k_ref[rq, :]], axis=0)
            vv = jnp.concatenate([v_ref[rp, :], v_ref[rq, :]], axis=0)
            s = _mm(heads.stack(q_ref[rq, :] * QK_SCALE), kk, NT) + bias_scr[first]
            m = jnp.max(s, axis=1, keepdims=True)
            p = jnp.exp(s - m)
            l = jnp.sum(p, axis=1, keepdims=True)
            o_pair = heads.unstack(_mm(p, vv))
            m_pair = m[:BLK] * m0 + m[BLK:] * m1
            l_pair = l[:BLK] * m0 + l[BLK:] * m1
            m_old = m_scr[rq, :]
            m_new = jnp.maximum(m_old, m_pair)
            alpha, beta = jnp.exp(m_old - m_new), jnp.exp(m_pair - m_new)
            o_ref[rq, :] = o_ref[rq, :] * alpha + o_pair * beta
            z_scr[rq, :] = z_scr[rq, :] * alpha + l_pair * beta
            m_scr[rq, :] = m_new
            return carry

        def blocks(heads):
            for dil in dils:
                lax.fori_loop(0, SEQ // BLK, lambda n, carry, dil=dil: step(n, carry, dil, heads), 0, unroll=8)

        if gqa:
            for grp in range(2):
                pl.when(pl.program_id(0) // 2 == grp)(lambda grp=grp: blocks(_HeadStack(grp)))
        else:
            blocks(_HeadStack(None))

        def fin(t, carry):
            rt = pl.ds(pl.multiple_of(t * TM, TM), TM)
            z = z_scr[rt, :]
            o_ref[rt, :] = o_ref[rt, :] / z
            lse_ref[rt, :] = m_scr[rt, :] + jnp.log(z)
            return carry
        lax.fori_loop(0, SEQ // TM, fin, 0)

    col = lambda blk: pl.BlockSpec((SEQ, LANES), lambda p, blk=blk: (0, blk + p))
    kv = (lambda blk: pl.BlockSpec((SEQ, LANES), lambda p, blk=blk: (0, blk))) if gqa else col
    in_specs = [col(qblk), kv(kblk), kv(vblk)]
    args = [proj, proj, proj]
    if has_sink:
        in_specs.append(pl.BlockSpec((1, LANES), lambda p: (0, p)))
        args.append(sink_x)
    out = pl.BlockSpec((SEQ, LANES), lambda p: (0, p))
    return pl.pallas_call(body, name=name, out_shape=(_sds((SEQ, 512)), _sds((SEQ, 512))), grid=(4,),
                          in_specs=in_specs, out_specs=(out, out),
                          scratch_shapes=[pltpu.VMEM((SEQ, LANES), F32), pltpu.VMEM((SEQ, LANES), F32), _BIAS],
                          compiler_params=_cp(48))(*args)


def _attn_bwd(proj, qblk, kblk, vblk, do, o, lse, dils, gqa, sink_x, name):
    has_sink = sink_x is not None

    def body(*refs):
        if has_sink:
            q_ref, k_ref, v_ref, do_ref, o_ref, lse_ref, s_ref, dq_ref, dk_ref, dv_ref, ds_ref, bias_scr = refs
        else:
            q_ref, k_ref, v_ref, do_ref, o_ref, lse_ref, dq_ref, dk_ref, dv_ref, bias_scr = refs
        pid = pl.program_id(0)

        @pl.when(pid == 0)
        def _():
            _fill_band_bias(bias_scr)
        dq_ref[...] = jnp.zeros_like(dq_ref)
        if gqa:
            @pl.when(pid == 0)
            def _():
                dk_ref[...] = jnp.zeros_like(dk_ref)
                dv_ref[...] = jnp.zeros_like(dv_ref)
        else:
            dk_ref[...] = jnp.zeros_like(dk_ref)
            dv_ref[...] = jnp.zeros_like(dv_ref)

        def step(n, carry, dil, heads):
            m0, m1 = heads.m0, heads.m1
            st, stp, first = _block_pos(n, dil)
            rq, rp = _rows(st, dil), _rows(stp, dil)
            do_, lse_ = do_ref[rq, :], lse_ref[rq, :]
            kk = jnp.concatenate([k_ref[rp, :], k_ref[rq, :]], axis=0)
            vv = jnp.concatenate([v_ref[rp, :], v_ref[rq, :]], axis=0)
            qs, dos = heads.stack(q_ref[rq, :] * QK_SCALE), heads.stack(do_)
            doo = do_ * o_ref[rq, :]
            delta = jnp.concatenate([jnp.sum(doo * m0, axis=1, keepdims=True), jnp.sum(doo * m1, axis=1, keepdims=True)], axis=0)
            lse_s = jnp.concatenate([lse_[:, 0:1], lse_[:, HD:HD + 1]], axis=0)
            p = jnp.exp(_mm(qs, kk, NT) + bias_scr[first] - lse_s)
            ds = p * (_mm(dos, vv, NT) - delta)
            dq_ref[rq, :] += heads.unstack(_mm(ds, kk)) * QK_SCALE
            dk_sum, dv_sum = _mm(ds, qs, TN), _mm(p, dos, TN)
            dk_ref[rp, :] += dk_sum[:BLK]
            dk_ref[rq, :] += dk_sum[BLK:]
            dv_ref[rp, :] += dv_sum[:BLK]
            dv_ref[rq, :] += dv_sum[BLK:]
            return carry

        def blocks(heads):
            for dil in dils:
                lax.fori_loop(0, SEQ // BLK, lambda n, carry, dil=dil: step(n, carry, dil, heads), 0, unroll=4)

        if gqa:
            for grp in range(2):
                pl.when(pid // 2 == grp)(lambda grp=grp: blocks(_HeadStack(grp)))
        else:
            blocks(_HeadStack(None))

        if has_sink:
            m0, m1 = _half_masks()

            def sink_rows(t, acc):
                rt = pl.ds(pl.multiple_of(t * TM, TM), TM)
                return acc - jnp.sum(jnp.exp(s_ref[...] - lse_ref[rt, :]) * (do_ref[rt, :] * o_ref[rt, :]), axis=0, keepdims=True)
            acc = lax.fori_loop(0, SEQ // TM, sink_rows, jnp.zeros((1, LANES), F32))
            per_head = jnp.sum(acc * m0, axis=1, keepdims=True) * m0 + jnp.sum(acc * m1, axis=1, keepdims=True) * m1
            ds_ref[0] = jnp.broadcast_to(per_head, (8, LANES))

    col = lambda blk: pl.BlockSpec((SEQ, LANES), lambda p, blk=blk: (0, blk + p))
    kv = (lambda blk: pl.BlockSpec((SEQ, LANES), lambda p, blk=blk: (0, blk))) if gqa else col
    pair = pl.BlockSpec((SEQ, LANES), lambda p: (0, p))
    in_specs = [col(qblk), kv(kblk), kv(vblk), pair, pair, pair]
    args = [proj, proj, proj, do, o, lse]
    kvw = LANES if gqa else 512
    kv_out = pl.BlockSpec((SEQ, LANES), lambda p: (0, 0)) if gqa else pair
    out_shape = [_sds((SEQ, 512)), _sds((SEQ, kvw)), _sds((SEQ, kvw))]
    out_specs = [pair, kv_out, kv_out]
    if has_sink:
        in_specs.append(pl.BlockSpec((1, LANES), lambda p: (0, p)))
        args.append(sink_x)
        out_shape.append(_sds((4, 8, LANES)))
        out_specs.append(pl.BlockSpec((1, 8, LANES), lambda p: (p, 0, 0)))
    return pl.pallas_call(body, name=name, out_shape=tuple(out_shape), grid=(4,), in_specs=in_specs,
                          out_specs=tuple(out_specs), scratch_shapes=[_BIAS], compiler_params=_cp(56))(*args)


_CT = 128


def _rows_before(x_ref, t, k):
    if t == 0:
        return jnp.concatenate([jnp.zeros((k, LANES), F32), x_ref[0:_CT - k, :]], axis=0)
    return x_ref[t * _CT - k:(t + 1) * _CT - k, :]


def _conv_pre(x_ref, w_ref, b_ref, t):
    taps = [x_ref[t * _CT:(t + 1) * _CT, :]] + [_rows_before(x_ref, t, k) for k in range(1, 4)]
    u = b_ref[...] + taps[0] * w_ref[3:4, :]
    for k in range(1, 4):
        u = u + taps[k] * w_ref[3 - k:4 - k, :]
    return u, taps


def _conv_fwd(proj, w, b, name):
    def body(x_ref, w_ref, b_ref, o_ref):
        for t in range(SEQ // _CT):
            o_ref[t * _CT:(t + 1) * _CT, :] = _silu(_conv_pre(x_ref, w_ref, b_ref, t)[0])

    nblk = CONV_CH // LANES
    return pl.pallas_call(body, name=name, out_shape=_sds((SEQ, CONV_CH)), grid=(nblk,),
                          in_specs=[pl.BlockSpec((SEQ, LANES), lambda j: (0, XBC // LANES + j)),
                                    pl.BlockSpec((4, LANES), lambda j: (0, j)), pl.BlockSpec((1, LANES), lambda j: (0, j))],
                          out_specs=pl.BlockSpec((SEQ, LANES), lambda j: (0, j)), compiler_params=_cp())(proj, w, b)


def _conv_bwd(proj, dact, w, b, name):
    def body(x_ref, da_ref, w_ref, b_ref, dx_ref, dw_ref, db_ref, du_scr):
        du_scr[SEQ:SEQ + 8, :] = jnp.zeros((8, LANES), F32)
        db = jnp.zeros((1, LANES), F32)
        dws = [jnp.zeros((1, LANES), F32)] * 4
        for t in range(SEQ // _CT):
            u, taps = _conv_pre(x_ref, w_ref, b_ref, t)
            du = da_ref[t * _CT:(t + 1) * _CT, :] * _dsilu(u)
            du_scr[t * _CT:(t + 1) * _CT, :] = du
            db = db + jnp.sum(du, axis=0, keepdims=True)
            dws = [dws[k] + jnp.sum(du * taps[k], axis=0, keepdims=True) for k in range(4)]
        db_ref[...] = db
        for k in range(4):
            dw_ref[3 - k:4 - k, :] = dws[k]
        for t in range(SEQ // _CT):
            dx = du_scr[t * _CT:(t + 1) * _CT, :] * w_ref[3:4, :]
            for k in range(1, 4):
                dx = dx + du_scr[t * _CT + k:(t + 1) * _CT + k, :] * w_ref[3 - k:4 - k, :]
            dx_ref[t * _CT:(t + 1) * _CT, :] = dx.astype(dx_ref.dtype)

    nblk = CONV_CH // LANES
    blk = pl.BlockSpec((SEQ, LANES), lambda j: (0, j))
    wspec, bspec = pl.BlockSpec((4, LANES), lambda j: (0, j)), pl.BlockSpec((1, LANES), lambda j: (0, j))
    return pl.pallas_call(body, name=name, out_shape=(_sds((SEQ, CONV_CH), MXU), _sds((4, CONV_CH)), _sds((1, CONV_CH))), grid=(nblk,),
                          in_specs=[pl.BlockSpec((SEQ, LANES), lambda j: (0, XBC // LANES + j)), blk, wspec, bspec],
                          out_specs=(blk, wspec, bspec), scratch_shapes=[pltpu.VMEM((SEQ + 8, LANES), F32)],
                          compiler_params=_cp())(proj, dact, w, b)


def _ssd_chunk(xs, bm, cm, dtr, z, hs, al16, dtb, dskx, nw):
    m0, m1 = _half_masks()
    row = lax.broadcasted_iota(jnp.int32, (BLK, BLK), 0)
    col = lax.broadcasted_iota(jnp.int32, (BLK, BLK), 1)
    causal = row >= col
    tril = causal.astype(F32)
    lane = lax.broadcasted_iota(jnp.int32, (1, LANES), 1)
    sub = lax.broadcasted_iota(jnp.int32, (BLK, 1), 0)
    last_row = (sub == BLK - 1).astype(F32)
    dt = jnp.where(lane < 16, _softplus(dtr + dtb), 0.0)
    a16 = -jnp.exp(al16)
    acum = jnp.dot(tril, dt * a16, precision=HI, preferred_element_type=F32)
    acum_t = acum.T
    gmat = [_mm(cm[g], bm[g], NT) for g in range(2)]
    ys, hn = [], []
    for p in range(8):
        g = p // 4
        pick = [(lane == 2 * p + a).astype(F32) for a in range(2)]
        col_h = [jnp.sum(acum * pick[a], axis=1, keepdims=True) for a in range(2)]
        dt_x = sum(jnp.sum(dt * pick[a], axis=1, keepdims=True) * msk for a, msk in enumerate((m0, m1)))
        ac_x = col_h[0] * m0 + col_h[1] * m1
        a_end = jnp.sum(ac_x * last_row, axis=0, keepdims=True)
        xdt = xs[p] * dt_x
        y = _mm(cm[g], hs[p]) * jnp.exp(ac_x)
        for a, msk in enumerate((m0, m1)):
            row_h = jnp.sum(acum_t * (sub == 2 * p + a).astype(F32), axis=0, keepdims=True)
            decay = jnp.exp(jnp.where(causal, col_h[a] - row_h, NEG))
            y = y + _mm(gmat[g] * decay, xdt * msk)
        st = _mm(bm[g], xdt * jnp.exp(a_end - ac_x), TN)
        hn.append(hs[p] * jnp.exp(a_end) + st)
        y = y + dskx[p] * xs[p]
        ys.append(y * _silu(z[p]))
    out = []
    for g in range(2):
        ms = sum(jnp.sum(ys[p] * ys[p], axis=1, keepdims=True) for p in range(4 * g, 4 * g + 4)) * (1.0 / 512)
        rstd = lax.rsqrt(ms + EPS)
        out += [ys[p] * rstd * nw[p] for p in range(4 * g, 4 * g + 4)]
    return out, hn


def _tiles(ref, n, off=0):
    return [ref[:, off + LANES * p:off + LANES * (p + 1)] for p in range(n)]


def _ssd_load(xbc_ref, z_ref, dt_ref, al16_ref, dtb_ref, dsk_ref, nw_ref):
    return (_tiles(xbc_ref, 8), _tiles(xbc_ref, 2, 1024), _tiles(xbc_ref, 2, 1280), dt_ref[...], _tiles(z_ref, 8)), \
           (al16_ref[...], dtb_ref[...], _tiles(dsk_ref, 8), _tiles(nw_ref, 8))


_NCH = SEQ // BLK


def _ssd_param_specs():
    return [_full((1, LANES)), _full((1, LANES)), _full((1, 1024)), _full((1, 1024))]


def _ssd_fwd(xbc_act, proj, al16, dtb, dskx, nw, name):
    def body(xbc_ref, z_ref, dt_ref, al16_ref, dtb_ref, dsk_ref, nw_ref, y_ref, hin_ref, h_scr):
        @pl.when(pl.program_id(0) == 0)
        def _():
            h_scr[...] = jnp.zeros_like(h_scr)
        acts, params = _ssd_load(xbc_ref, z_ref, dt_ref, al16_ref, dtb_ref, dsk_ref, nw_ref)
        hs = _tiles(h_scr, 8)
        hin_ref[0] = h_scr[...]
        ys, hn = _ssd_chunk(*acts, hs, *params)
        for p in range(8):
            y_ref[:, LANES * p:LANES * (p + 1)] = ys[p].astype(y_ref.dtype)
            h_scr[:, LANES * p:LANES * (p + 1)] = hn[p]

    return pl.pallas_call(
        body, name=name, out_shape=(_sds((SEQ, 1024), MXU), _sds((_NCH, BLK, 1024))), grid=(_NCH,),
        in_specs=[pl.BlockSpec((BLK, CONV_CH), lambda c: (c, 0)), pl.BlockSpec((BLK, 1024), lambda c: (c, ZB // 1024)),
                  pl.BlockSpec((BLK, LANES), lambda c: (c, DTC // LANES))] + _ssd_param_specs(),
        out_specs=(pl.BlockSpec((BLK, 1024), lambda c: (c, 0)), pl.BlockSpec((1, BLK, 1024), lambda c: (c, 0, 0))),
        scratch_shapes=[pltpu.VMEM((BLK, 1024), F32)], compiler_params=_cp())(xbc_act, proj, proj, al16, dtb, dskx, nw)


def _ssd_bwd(xbc_act, proj, hin, dyb, al16, dtb, dskx, nw, name):
    def body(xbc_ref, z_ref, dt_ref, hin_ref, dy_ref, al16_ref, dtb_ref, dsk_ref, nw_ref,
             dxbc_ref, dz_ref, ddt_ref, dal16_ref, ddtb_ref, ddsk_ref, dnw_ref, dh_scr):
        @pl.when(pl.program_id(0) == 0)
        def _():
            dh_scr[...] = jnp.zeros_like(dh_scr)
            for r in (dal16_ref, ddtb_ref, ddsk_ref, dnw_ref):
                r[...] = jnp.zeros_like(r)
        acts, params = _ssd_load(xbc_ref, z_ref, dt_ref, al16_ref, dtb_ref, dsk_ref, nw_ref)
        hs = [hin_ref[0, :, LANES * p:LANES * (p + 1)] for p in range(8)]
        _, vjp = jax.vjp(lambda a, h, q: _ssd_chunk(*a, h, *q), acts, hs, params)
        (dxs, dbm, dcm, ddt, dz), dhs, (dal16, ddtb, ddsk, dnw) = vjp((_tiles(dy_ref, 8), _tiles(dh_scr, 8)))
        for p in range(8):
            cols = slice(LANES * p, LANES * (p + 1))
            dxbc_ref[:, cols] = dxs[p]
            dz_ref[:, cols] = dz[p].astype(dz_ref.dtype)
            dh_scr[:, cols] = dhs[p]
            ddsk_ref[:, cols] += ddsk[p]
            dnw_ref[:, cols] += dnw[p]
        for g in range(2):
            dxbc_ref[:, 1024 + LANES * g:1024 + LANES * (g + 1)] = dbm[g]
            dxbc_ref[:, 1280 + LANES * g:1280 + LANES * (g + 1)] = dcm[g]
        ddt_ref[...] = ddt.astype(ddt_ref.dtype)
        dal16_ref[...] += dal16
        ddtb_ref[...] += ddtb

    rev = lambda c: _NCH - 1 - c
    return pl.pallas_call(
        body, name=name,
        out_shape=(_sds((SEQ, CONV_CH)), _sds((SEQ, 1024), MXU), _sds((SEQ, LANES), MXU),
                   _sds((1, LANES)), _sds((1, LANES)), _sds((1, 1024)), _sds((1, 1024))),
        grid=(_NCH,),
        in_specs=[pl.BlockSpec((BLK, CONV_CH), lambda c: (rev(c), 0)), pl.BlockSpec((BLK, 1024), lambda c: (rev(c), ZB // 1024)),
                  pl.BlockSpec((BLK, LANES), lambda c: (rev(c), DTC // LANES)), pl.BlockSpec((1, BLK, 1024), lambda c: (rev(c), 0, 0)),
                  pl.BlockSpec((BLK, 1024), lambda c: (rev(c), 0))] + _ssd_param_specs(),
        out_specs=(pl.BlockSpec((BLK, CONV_CH), lambda c: (rev(c), 0)), pl.BlockSpec((BLK, 1024), lambda c: (rev(c), 0)),
                   pl.BlockSpec((BLK, LANES), lambda c: (rev(c), 0)),
                   _full((1, LANES)), _full((1, LANES)), _full((1, 1024)), _full((1, 1024))),
        scratch_shapes=[pltpu.VMEM((BLK, 1024), F32)], compiler_params=_cp())(xbc_act, proj, proj, hin, dyb, al16, dtb, dskx, nw)


def _rstd(v):
    return lax.rsqrt(jnp.mean(v * v, axis=1, keepdims=True) + EPS)


def _rms_bwd(dn, n, rstd):
    return rstd * (dn - n * jnp.mean(dn * n, axis=1, keepdims=True))


_VEC = _full((1, D))


def _layer_spec(layer):
    return pl.BlockSpec((None, 2048, D), lambda *_: (layer, 0, 0))

_ROW = pl.BlockSpec((TM, D), lambda i, *_: (i, 0))


def _proj_fwd(x, pre_w, scale, shift, w, layer, name):
    tn, ni = 1024, SEQ // TM

    def body(x_ref, pw_ref, sc_ref, sh_ref, w_ref, o_ref, h_ref, h_scr):
        rows = pl.ds(pl.multiple_of(pl.program_id(1) * TM, TM), TM)

        @pl.when(pl.program_id(0) == 0)
        def _():
            xv = x_ref[...]
            h = ((xv * _rstd(xv) * pw_ref[...]) * (1.0 + sc_ref[...]) + sh_ref[...]).astype(h_ref.dtype)
            h_scr[rows, :] = h
            h_ref[...] = h
        o_ref[...] = jnp.dot(h_scr[rows, :], w_ref[...].astype(MXU), preferred_element_type=F32)

    first_pass = pl.BlockSpec((TM, D), lambda j, i: (jnp.where(j == 0, i, ni - 1), 0))
    return pl.pallas_call(body, name=name, out_shape=(_sds((SEQ, NP)), _sds((SEQ, D), MXU)), grid=(NP // tn, ni),
                          in_specs=[first_pass, _VEC, _VEC, _VEC, pl.BlockSpec((None, D, tn), lambda j, i: (layer, 0, j))],
                          out_specs=(pl.BlockSpec((TM, tn), lambda j, i: (i, j)), first_pass),
                          scratch_shapes=[pltpu.VMEM((SEQ, D), MXU)], compiler_params=_cp())(x, pre_w, scale, shift, w)


_HALF = pl.BlockSpec((TM, 512), lambda i: (i, 0))
_Z_A = pl.BlockSpec((TM, 512), lambda i: (i, ZA // 512))
_Z_C = pl.BlockSpec((TM, 512), lambda i: (i, ZC // 512))


def _out_fwd(o_a, yb, o_c, proj, w, layer, x, gate, post_w, name):
    def body(oa_ref, yb_ref, oc_ref, za_ref, zc_ref, w_ref, x_ref, g_ref, pw_ref, xn_ref, y_ref):
        y = (_mm(oa_ref[...] * _silu(za_ref[...]), w_ref[0:512, :]) + _mm(yb_ref[...], w_ref[512:1536, :])
             + _mm(oc_ref[...] * _silu(zc_ref[...]), w_ref[1536:2048, :]))
        y_ref[...] = y
        xn_ref[...] = x_ref[...] + g_ref[...] * (y * _rstd(y) * pw_ref[...])

    return pl.pallas_call(body, name=name, out_shape=(_sds((SEQ, D)), _sds((SEQ, D))), grid=(SEQ // TM,),
                          in_specs=[_HALF, _ROW, _HALF, _Z_A, _Z_C, _layer_spec(layer), _ROW, _VEC, _VEC],
                          out_specs=(_ROW, _ROW), compiler_params=_cp())(o_a, yb, o_c, proj, proj, w, x, gate, post_w)


def _post_bwd(dxo, y, gate, post_w, name):
    def body(dx_ref, y_ref, g_ref, pw_ref, dy_ref, dg_ref, dpw_ref):
        @pl.when(pl.program_id(0) == 0)
        def _():
            dg_ref[...] = jnp.zeros_like(dg_ref)
            dpw_ref[...] = jnp.zeros_like(dpw_ref)
        dx, y = dx_ref[...], y_ref[...]
        rstd = _rstd(y)
        n = y * rstd
        dg_ref[...] += jnp.sum(dx * (n * pw_ref[...]), axis=0, keepdims=True)
        dr = dx * g_ref[...]
        dpw_ref[...] += jnp.sum(dr * n, axis=0, keepdims=True)
        dy_ref[...] = _rms_bwd(dr * pw_ref[...], n, rstd)

    return pl.pallas_call(body, name=name, out_shape=(_sds((SEQ, D)), _sds((1, D)), _sds((1, D))), grid=(SEQ // TM,),
                          in_specs=[_ROW, _ROW, _VEC, _VEC], out_specs=(_ROW, _VEC, _VEC), compiler_params=_cp())(dxo, y, gate, post_w)


def _dymix(dy, w, layer, o_a, o_c, proj, name):
    def body(dy_ref, w_ref, oa_ref, oc_ref, za_ref, zc_ref, doa_ref, dza_ref, b_ref, doc_ref, dzc_ref):
        dy = dy_ref[...]
        b_ref[...] = _mm(dy, w_ref[512:1536, :], NT)
        for rows, o_ref, z_ref, do_ref, dz_ref in ((slice(0, 512), oa_ref, za_ref, doa_ref, dza_ref),
                                                   (slice(1536, 2048), oc_ref, zc_ref, doc_ref, dzc_ref)):
            dyg, z = _mm(dy, w_ref[rows, :], NT), z_ref[...]
            do_ref[...] = dyg * _silu(z)
            dz_ref[...] = (dyg * o_ref[...] * _dsilu(z)).astype(dz_ref.dtype)

    return pl.pallas_call(body, name=name, out_shape=(_sds((SEQ, 512)), _sds((SEQ, 512), MXU), _sds((SEQ, D)), _sds((SEQ, 512)), _sds((SEQ, 512), MXU)),
                          grid=(SEQ // TM,), in_specs=[_ROW, _layer_spec(layer), _HALF, _HALF, _Z_A, _Z_C],
                          out_specs=(_HALF, _HALF, _ROW, _HALF, _HALF), compiler_params=_cp())(dy, w, o_a, o_c, proj, proj)


def _dwout(o_a, yb, o_c, proj, dy, name):
    def body(oa_ref, yb_ref, oc_ref, za_ref, zc_ref, dy_ref, o_ref):
        @pl.when(pl.program_id(0) == 0)
        def _():
            o_ref[...] = jnp.zeros_like(o_ref)
        dy = dy_ref[...]
        o_ref[0:512, :] += _mm(oa_ref[...] * _silu(za_ref[...]), dy, TN)
        o_ref[512:1536, :] += _mm(yb_ref[...], dy, TN)
        o_ref[1536:2048, :] += _mm(oc_ref[...] * _silu(zc_ref[...]), dy, TN)

    return pl.pallas_call(body, name=name, out_shape=_sds((2048, D)), grid=(SEQ // TM,),
                          in_specs=[_HALF, _ROW, _HALF, _Z_A, _Z_C, _ROW], out_specs=_full((2048, D)),
                          compiler_params=_cp())(o_a, yb, o_c, proj, proj, dy)


def _dwin(h, pieces, name):
    n = len(pieces)
    widths = [p.shape[1] for p in pieces]
    half = NP // 2

    def body(*refs):
        h_ref, p_refs, o_ref = refs[0], refs[1:1 + n], refs[1 + n]

        @pl.when(pl.program_id(0) == 0)
        def _():
            o_ref[...] = jnp.zeros_like(o_ref)
        hv, c0 = h_ref[...], 0
        for p_ref, wd in zip(p_refs, widths):
            o_ref[:, c0:c0 + wd] += _mm(hv, p_ref[...], TN)
            c0 += wd

    return pl.pallas_call(body, name=name, out_shape=_sds((D, half)), grid=(SEQ // TM,),
                          in_specs=[_ROW] + [pl.BlockSpec((TM, wd), lambda k: (k, 0)) for wd in widths],
                          out_specs=_full((D, half)), compiler_params=_cp(56))(h, *pieces)


_TMH = 256


def _dh_bwd(pieces, w, x, pre_w, scale, dxo, name):
    n = len(pieces)
    widths = [p.shape[1] for p in pieces]

    def body(*refs):
        p_refs, (w_ref, x_ref, pw_ref, sc_ref, dxo_ref, dx_ref, dsh_ref, dsc_ref, dpw_ref) = refs[:n], refs[n:]

        @pl.when(pl.program_id(0) == 0)
        def _():
            for r in (dsh_ref, dsc_ref, dpw_ref):
                r[...] = jnp.zeros_like(r)
        dh, c0 = 0.0, 0
        for p_ref, wd in zip(p_refs, widths):
            dh = dh + _mm(p_ref[...], w_ref[:, c0:c0 + wd], NT)
            c0 += wd
        xv = x_ref[...]
        rstd = _rstd(xv)
        nrm = xv * rstd
        dsh_ref[...] += jnp.sum(dh, axis=0, keepdims=True)
        dsc_ref[...] += jnp.sum(dh * (nrm * pw_ref[...]), axis=0, keepdims=True)
        dhn = dh * (1.0 + sc_ref[...])
        dpw_ref[...] += jnp.sum(dhn * nrm, axis=0, keepdims=True)
        dx_ref[...] = _rms_bwd(dhn * pw_ref[...], nrm, rstd) + dxo_ref[...]

    row = pl.BlockSpec((_TMH, D), lambda i: (i, 0))
    return pl.pallas_call(body, name=name, out_shape=(_sds((SEQ, D)), _sds((1, D)), _sds((1, D)), _sds((1, D))),
                          grid=(SEQ // _TMH,),
                          in_specs=[pl.BlockSpec((_TMH, wd), lambda i: (i, 0)) for wd in widths]
                          + [pl.BlockSpec((None, D, NP), lambda i: (0, 0, 0)), row, _VEC, _VEC, row],
                          out_specs=(row, _VEC, _VEC, _VEC), compiler_params=_cp(56))(*pieces, w, x, pre_w, scale, dxo)


def _w_in_padded(land, name):
    rows = 128

    def body(l_ref, o_ref):
        o_ref[...] = _pad_cols(jnp.concatenate([l_ref[k] for k in range(4)], axis=1))

    return pl.pallas_call(body, name=name, out_shape=_sds((D, NP), land.dtype), grid=(D // rows,),
                          in_specs=[pl.BlockSpec((4, rows, SHARD_IN), lambda i: (0, i, 0))],
                          out_specs=pl.BlockSpec((rows, NP), lambda i: (i, 0)), compiler_params=_cp())(land)


def _grad_blocks(dwa, dwb, name):
    rows = 128

    def body(a_ref, b_ref, o_ref):
        g = _unpad_cols(jnp.concatenate([a_ref[...], b_ref[...]], axis=1))
        for k in range(4):
            o_ref[k] = g[:, SHARD_IN * k:SHARD_IN * (k + 1)].astype(o_ref.dtype)

    half = pl.BlockSpec((rows, NP // 2), lambda i: (i, 0))
    return pl.pallas_call(body, name=name, out_shape=_sds((4, D, SHARD_IN), jnp.bfloat16), grid=(D // rows,),
                          in_specs=[half, half], out_specs=pl.BlockSpec((4, rows, SHARD_IN), lambda i: (0, i, 0)),
                          compiler_params=_cp())(dwa, dwb)


def _loss_bwd(xf, tgt, name):
    def body(x_ref, t_ref, dx_ref, l_ref):
        @pl.when(pl.program_id(0) == 0)
        def _():
            l_ref[...] = jnp.zeros_like(l_ref)
        e = x_ref[...] - t_ref[...]
        dx_ref[...] = e * (1.0 / D)
        l_ref[...] += 0.5 * jnp.sum(jnp.mean(e * e, axis=1, keepdims=True), axis=0, keepdims=True)

    return pl.pallas_call(body, name=name, out_shape=(_sds((SEQ, D)), _sds((8, LANES))), grid=(SEQ // TM,),
                          in_specs=[_ROW, _ROW], out_specs=(_ROW, _full((8, LANES))), compiler_params=_cp())(xf, tgt)


def _mod_part(c_all, ada_w, ada_b, name):
    def body(c_ref, w_ref, b_ref, o_ref):
        o_ref[0] = _mm(_silu(c_ref[...]), w_ref[0]) + b_ref[0]

    return pl.pallas_call(body, name=name, out_shape=_sds((DEPTH, 8, 768)), grid=(DEPTH,),
                          in_specs=[_full((8, D)), pl.BlockSpec((1, D, 768), lambda i: (i, 0, 0)), pl.BlockSpec((1, 1, 768), lambda i: (i, 0, 0))],
                          out_specs=pl.BlockSpec((1, 8, 768), lambda i: (i, 0, 0)), compiler_params=_cp())(c_all, ada_w, ada_b)


def _ada_grad(c_t, dmod, name):
    def body(c_ref, d_ref, o_ref):
        ca = _silu(c_ref[...])
        dm = d_ref[0]
        acc = ca[:, 0:1] * dm[0:1, :]
        for s in range(1, 8):
            acc = acc + ca[:, s:s + 1] * dm[s:s + 1, :]
        o_ref[0] = acc

    return pl.pallas_call(body, name=name, out_shape=_sds((DEPTH, D, 768)), grid=(DEPTH,),
                          in_specs=[_full((D, LANES)), pl.BlockSpec((1, 8, 768), lambda i: (i, 0, 0))],
                          out_specs=pl.BlockSpec((1, D, 768), lambda i: (i, 0, 0)), compiler_params=_cp())(c_t, dmod)


def _pack(parts):
    flat = []
    for p in parts:
        f = p.reshape(-1)
        flat.append(jnp.pad(f, (0, (-f.size) % LANES)))
    v = jnp.concatenate(flat)
    return jnp.pad(v, (0, (-v.size) % (8 * LANES))).reshape(-1, LANES)


def _unpack(v, shapes):
    v = v.reshape(-1)
    out, off = [], 0
    for s in shapes:
        n = math.prod(s)
        out.append(v[off:off + n].reshape(s))
        off += n + (-n) % LANES
    return out


_GIVEN_DT, _GIVEN_C = 4608, 4624


def _pad_cols(w):
    return jnp.concatenate([w[..., :_GIVEN_DT], w[..., _GIVEN_C:], w[..., _GIVEN_DT:_GIVEN_C],
                            jnp.zeros(w.shape[:-1] + (NP - IN_COLS,), w.dtype)], axis=-1)


def _unpad_cols(w):
    return jnp.concatenate([w[..., :_GIVEN_DT], w[..., DTC:DTC + 16], w[..., _GIVEN_DT:DTC]], axis=-1)


def _pad_lanes(v):
    return jnp.pad(v, (0, LANES - v.shape[0])).reshape(1, LANES)


def _local_step(x2, tgt, mod, weights_of, grads_done, pre_w, post_w, conv_w, conv_b, dt_bias, a_log, d_skip, nw, sinks):
    saved = []
    xcur = x2
    for i in range(DEPTH):
        shift, scale, gate = mod[i:i + 1, :D], mod[i:i + 1, D:2 * D], mod[i:i + 1, 2 * D:]
        pw, qw = pre_w[i:i + 1], post_w[i:i + 1]
        w_p, w_o = weights_of(i, xcur)
        proj, h = _proj_fwd(xcur, pw, scale, shift, w_p, 0, "proj_fwd")
        o_a, lse_a = _attn_fwd(proj, QA // LANES, KA // LANES, VA // LANES, DILS, False, None, "attn_a_fwd")
        sink_x = jnp.repeat(sinks[i], HD).reshape(1, 512)
        o_c, lse_c = _attn_fwd(proj, QC // LANES, KC // LANES, VC // LANES, (1,), True, sink_x, "attn_c_fwd")
        cw, cb = conv_w[i], conv_b[i:i + 1]
        xbc_act = _conv_fwd(proj, cw, cb, "conv_fwd")
        ssd_p = (_pad_lanes(a_log[i]), _pad_lanes(dt_bias[i]), jnp.repeat(d_skip[i], HD).reshape(1, 1024), nw[i:i + 1])
        yb, hin = _ssd_fwd(xbc_act, proj, *ssd_p, "ssd_fwd")
        xnew, y = _out_fwd(o_a, yb, o_c, proj, w_o, 0, xcur, gate, qw, "out_fwd")
        saved.append((w_p, w_o, xcur, scale, gate, pw, qw, proj, h, o_a, lse_a, sink_x, o_c, lse_c, cw, cb, xbc_act, ssd_p, yb, hin, y))
        xcur = xnew
    dx, ltile = _loss_bwd(xcur, tgt, "loss")
    dmod, small = [None] * DEPTH, [None] * DEPTH
    for i in reversed(range(DEPTH)):
        w_p, w_o, xin, scale, gate, pw, qw, proj, h, o_a, lse_a, sink_x, o_c, lse_c, cw, cb, xbc_act, ssd_p, yb, hin, y = saved[i]
        dy, dgate, dpost = _post_bwd(dx, y, gate, qw, "post_bwd")
        do_a, dz_a, dyb, do_c, dz_c = _dymix(dy, w_o, 0, o_a, o_c, proj, "dymix")
        dwo = _dwout(o_a, yb, o_c, proj, dy, "dwout")
        dq_a, dk_a, dv_a = _attn_bwd(proj, QA // LANES, KA // LANES, VA // LANES, do_a, o_a, lse_a, DILS, False, None, "attn_a_bwd")
        dq_c, dk_c, dv_c, dsk = _attn_bwd(proj, QC // LANES, KC // LANES, VC // LANES, do_c, o_c, lse_c, (1,), True, sink_x, "attn_c_bwd")
        dxbc_act, dz_b, ddt, dal16, ddtb, ddsk, dnw = _ssd_bwd(xbc_act, proj, hin, dyb, *ssd_p, "ssd_bwd")
        dxbc, dcw, dcb = _conv_bwd(proj, dxbc_act, cw, cb, "conv_bwd")
        half_a, half_b = [dq_a, dk_a, dv_a, dz_a, dz_b], [dxbc, dq_c, dz_c, dk_c, dv_c, ddt]
        sent = grads_done(i, _dwin(h, half_a, "dwin_a"), _dwin(h, half_b, "dwin_b"), dwo)
        dx, dshift, dscale, dpre = _dh_bwd(half_a + half_b, w_p, xin, pw, scale + sent[0, 0], dx, "dh_bwd")
        dmod[i] = jnp.concatenate([dshift, dscale, dgate], axis=1)
        small[i] = (dpre, dpost, dcw, dcb, ddtb[0, :16], dal16[0, :16], ddsk.reshape(16, HD).sum(axis=1), dnw, dsk[:, 0, ::HD].reshape(8))
    return ltile, dx, jnp.concatenate(dmod, axis=0), small


_SMALL = ((1, D), (1, D), (4, CONV_CH), (1, CONV_CH), (16,), (16,), (16,), (1, D), (8,))


def kernel(x, c, ada_w, ada_b, pre_norm_w, post_norm_w, w_in, conv_w, conv_b, dt_bias, a_log, d_skip, ssm_norm_w, sinks, w_out, loss_target, m_ada_w, m_ada_b, m_pre_norm_w, m_post_norm_w, m_w_in, m_conv_w, m_conv_b, m_dt_bias, m_a_log, m_d_skip, m_ssm_norm_w, m_sinks, m_w_out, v_ada_w, v_ada_b, v_pre_norm_w, v_post_norm_w, v_w_in, v_conv_w, v_conv_b, v_dt_bias, v_a_log, v_d_skip, v_ssm_norm_w, v_sinks, v_w_out):
    xi, yi, ci = lax.axis_index("x"), lax.axis_index("y"), lax.axis_index("c")
    chip = 2 * xi + yi
    me = 2 * chip + ci

    w_in_b = _cast_bf16(w_in, 512, "cast_w_in")
    w_out_b = _cast_bf16(w_out, 512, "cast_w_out")
    gathers = []
    for i in range(DEPTH):
        lands = [lax.dynamic_update_slice(lax.empty((4,) + a.shape[1:], a.dtype), a[i][None], (chip, 0, 0)) for a in (w_in_b, w_out_b)]
        gathers.append(_split_start(None, lands, f"gather_start{i}"))
    all_started = gathers[0][3] + gathers[1][3] + gathers[2][3] + gathers[3][3]

    def weights_of(i, after):
        send_sems, recv_sems, thru, _ = gathers[i]
        if i == 0:
            after = all_started + mod[:1, :LANES]
        g_in, g_out = _split_wait(send_sems, recv_sems, thru, 2, after, f"gather_wait{i}")
        return _w_in_padded(g_in, "w_in_padded")[None], g_out.reshape(1, 2048, D)

    scatters = [None] * DEPTH

    def grads_done(i, dwa, dwb, dwo):
        blocks = [_grad_blocks(dwa, dwb, "grad_blocks"), _cast_bf16(dwo.reshape(4, 512, D), 512, "cast_dw_out")]
        scatters[i] = _split_start(blocks, [lax.empty(b.shape, b.dtype) for b in blocks], f"scatter_start{i}")
        return scatters[i][3]

    g0 = _allgather8(_pack([c, conv_w]), "gather_c")
    c_all = g0[:, :8, :].reshape(8, D)
    conv_w_full = jnp.concatenate([g0[2 * k, 8:56, :].reshape(DEPTH, 4, CONV_CH // 4) for k in range(4)], axis=-1)

    ada_b_mine = lax.dynamic_slice_in_dim(ada_b, 768 * chip, 768, axis=1).reshape(DEPTH, 1, 768)
    gm = _allgather8(_mod_part(c_all, ada_w, ada_b_mine, "mod_part").reshape(DEPTH * 8, 768), "gather_mod")
    gm = gm.reshape(4, 2, DEPTH, 8, 768)[:, 0]
    mod = lax.dynamic_index_in_dim(gm, me, axis=2, keepdims=False).transpose(1, 0, 2).reshape(DEPTH, 3 * D)

    ltile, dx, dmod, small = _local_step(x[0], loss_target[0], mod, weights_of, grads_done, pre_norm_w, post_norm_w, conv_w_full,
                                         conv_b, dt_bias, a_log, d_skip, ssm_norm_w, sinks)

    packed = _pack([dmod] + [g for layer in small for g in layer] + [ltile[0]])
    gs = _allgather8(packed, "gather_small")
    tot = _sum_blocks(gs[:, None], packed.shape[0], "sum_small")[0]
    parts = _unpack(tot, [(DEPTH, 3 * D)] + list(_SMALL) * DEPTH + [(LANES,)])
    g_ada_b, loss = parts[0], parts[-1][0]
    per_layer = [parts[1 + len(_SMALL) * i:1 + len(_SMALL) * (i + 1)] for i in range(DEPTH)]
    g_pre, g_post, g_cw, g_cb, g_dtb, g_al, g_dsk, g_nw, g_sk = [jnp.stack([per_layer[i][j] for i in range(DEPTH)]) for j in range(len(_SMALL))]
    g_pre, g_post, g_cb, g_nw = g_pre[:, 0], g_post[:, 0], g_cb[:, 0], g_nw[:, 0]
    g_cw = lax.dynamic_slice_in_dim(g_cw, (CONV_CH // 4) * chip, CONV_CH // 4, axis=2)

    dmod_all = gs[:, :(DEPTH * 3 * D) // LANES, :].reshape(8, DEPTH, 3 * D).transpose(1, 0, 2)
    dmod_mine = lax.dynamic_slice_in_dim(dmod_all, 768 * chip, 768, axis=2)
    c_t = jnp.pad(c_all.T, ((0, 0), (0, LANES - 8)))
    g_ada_w = _ada_grad(c_t, dmod_mine, "ada_grad")

    res = {}
    res["ada_w"] = _adamw(ada_w, [g_ada_w], m_ada_w, v_ada_w, 512, "adamw_ada_w")
    names = ["ada_b", "pre_norm_w", "post_norm_w", "conv_w", "conv_b", "dt_bias", "a_log", "d_skip", "ssm_norm_w", "sinks"]
    ws = [ada_b, pre_norm_w, post_norm_w, conv_w, conv_b, dt_bias, a_log, d_skip, ssm_norm_w, sinks]
    gsm = [g_ada_b, g_pre, g_post, g_cw, g_cb, g_dtb, g_al, g_dsk, g_nw, g_sk]
    ms = [m_ada_b, m_pre_norm_w, m_post_norm_w, m_conv_w, m_conv_b, m_dt_bias, m_a_log, m_d_skip, m_ssm_norm_w, m_sinks]
    vs = [v_ada_b, v_pre_norm_w, v_post_norm_w, v_conv_w, v_conv_b, v_dt_bias, v_a_log, v_d_skip, v_ssm_norm_w, v_sinks]
    pw_, pg_, pm_, pv_ = _pack(ws), _pack(gsm), _pack(ms), _pack(vs)
    small_out = _adamw(pw_[None], [pg_[None]], pm_[None], pv_[None], pw_.shape[0], "adamw_small")

    others_done = small_out[1][0, :8] + res["ada_w"][1][0, :8, :LANES]
    landed = [_split_wait(*scatters[i][:3], 2, others_done, f"scatter_wait{i}") for i in range(DEPTH)]
    p_in = _sum_chips([d[2] for d in landed], [d[0] for d in landed], 128, "sum_w_in")
    p_out = _sum_chips([d[3] for d in landed], [d[1] for d in landed], 256, "sum_w_out")
    col_major, row_major = (lambda a: jnp.transpose(a, (2, 0, 1))), (lambda a: jnp.transpose(a, (1, 2, 0)))
    p_in = col_major(p_in)
    s_in, s_out = _sibling_swap([p_in, p_out], "swap_partials")
    res["w_in"] = [row_major(a) for a in _adamw(col_major(w_in), [p_in, s_in], col_major(m_w_in), col_major(v_w_in), None,
                                                "adamw_w_in", lead=SHARD_IN // 18)]
    res["w_out"] = _adamw(w_out, [p_out, s_out], m_w_out, v_w_out, 512, "adamw_w_out")
    shapes = [w.shape for w in ws]
    for kind in range(4):
        for nm, a in zip(names, _unpack(small_out[kind][0], shapes)):
            res.setdefault(nm, [None] * 4)[kind] = a
    order = ["ada_w", "ada_b", "pre_norm_w", "post_norm_w", "w_in", "conv_w", "conv_b", "dt_bias", "a_log", "d_skip", "ssm_norm_w", "sinks", "w_out"]
    return (loss, dx[None], *[res[n][0] for n in order], *[res[n][1] for n in order], *[res[n][2] for n in order], *[res[n][3] for n in order])
```

```python
import math

import jax
import jax.numpy as jnp
from jax import lax
from jax.experimental import pallas as pl
from jax.experimental.pallas import tpu as pltpu

F32 = jnp.float32
MXU = jnp.bfloat16
HI = lax.Precision.HIGHEST
MESH = pl.DeviceIdType.MESH

SEQ = 4096
D = 1024
DEPTH = 4
HD = 64
QK_SCALE = HD ** -0.5
LANES = 128
BLK = 128
DILS = (1, 4, 16)
NEG = -1e30
EPS = 1e-6
MIB = 1024 * 1024

NP = 6144
QA, KA, VA, ZA = 0, 512, 1024, 1536
ZB, XBC = 2048, 3072
QC, ZC, KC, VC = 4608, 5120, 5632, 5760
DTC = 5888
IN_COLS = 5904
SHARD_IN = IN_COLS // 4
CONV_CH = 1536
TM = 512

ADAM_LR, ADAM_B1, ADAM_B2, ADAM_EPS, ADAM_WD, ADAM_STEP = 0.001, 0.9, 0.999, 1e-08, 0.01, 10

NT = (((1,), (1,)), ((), ()))
TN = (((0,), (0,)), ((), ()))


def _cp(vmem_mib=48):
    return pltpu.CompilerParams(vmem_limit_bytes=vmem_mib * MIB)


def _sds(shape, dtype=F32):
    return jax.ShapeDtypeStruct(shape, dtype)


def _full(shape):
    n = len(shape)
    return pl.BlockSpec(shape, lambda *_: (0,) * n)


def _mm(a, b, dims=None):
    if dims is None:
        return jnp.dot(a.astype(MXU), b.astype(MXU), preferred_element_type=F32)
    return lax.dot_general(a.astype(MXU), b.astype(MXU), dims, preferred_element_type=F32)


def _sigmoid(x):
    return 1.0 / (1.0 + jnp.exp(-x))


def _silu(x):
    return x * _sigmoid(x)


def _dsilu(x):
    s = _sigmoid(x)
    return s * (1.0 + x * (1.0 - s))


def _softplus(x):
    ax = jnp.where(x >= 0, x, -x)
    return jnp.maximum(x, 0.0) + jnp.log1p(jnp.exp(-ax))


def _half_masks():
    lane = lax.broadcasted_iota(jnp.int32, (1, LANES), 1)
    m0 = (lane < HD).astype(F32)
    return m0, 1.0 - m0


def _allgather8(v, name):
    r, cc = v.shape

    def body(v_ref, out_ref, send_sems, recv_sems):
        x, y, c = lax.axis_index("x"), lax.axis_index("y"), lax.axis_index("c")
        me = 4 * x + 2 * y + c
        out_ref[me] = v_ref[...]
        peers = []
        for k in range(1, 8):
            px = 1 - x if k & 4 else x
            py = 1 - y if k & 2 else y
            pc = 1 - c if k & 1 else c
            peers.append((px, py, pc))
        sends = []
        for k, peer in enumerate(peers):
            cp = pltpu.make_async_remote_copy(src_ref=v_ref, dst_ref=out_ref.at[me], send_sem=send_sems.at[k],
                                              recv_sem=recv_sems.at[k], device_id=peer, device_id_type=MESH)
            cp.start()
            sends.append(cp)
        for k, (px, py, pc) in enumerate(peers):
            pltpu.make_async_remote_copy(src_ref=v_ref, dst_ref=out_ref.at[4 * px + 2 * py + pc], send_sem=send_sems.at[k],
                                         recv_sem=recv_sems.at[k], device_id=(px, py, pc), device_id_type=MESH).wait_recv()
        for cp in sends:
            cp.wait_send()

    return pl.pallas_call(
        body, name=name, out_shape=_sds((8, r, cc)),
        in_specs=[pl.BlockSpec(memory_space=pltpu.VMEM)], out_specs=pl.BlockSpec(memory_space=pltpu.VMEM),
        scratch_shapes=[pltpu.SemaphoreType.DMA((7,)), pltpu.SemaphoreType.DMA((7,))],
        compiler_params=_cp(32),
    )(v)


_HBM = pl.BlockSpec(memory_space=pltpu.HBM)
_SEM = pl.BlockSpec(memory_space=pltpu.SEMAPHORE)
_EFFECT = pltpu.SideEffectType.DATAFLOW_SIDE_EFFECTING


def _chip_copies(src_refs, land_refs, send_sems, recv_sems):
    x, y, c = lax.axis_index("x"), lax.axis_index("y"), lax.axis_index("c")
    mine = 2 * x + y
    out = []
    for i, land in enumerate(land_refs):
        for j, (px, py) in enumerate([(1 - x, y), (x, 1 - y), (1 - x, 1 - y)]):
            src = src_refs[i].at[2 * px + py] if src_refs else land.at[mine]
            mk = lambda dst, i=i, j=j, src=src, px=px, py=py: pltpu.make_async_remote_copy(
                src_ref=src, dst_ref=dst, send_sem=send_sems.at[3 * i + j], recv_sem=recv_sems.at[3 * i + j],
                device_id=(px, py, c), device_id_type=MESH)
            out.append((mk(land.at[mine]), mk(land.at[2 * px + py])))
    return out


def _split_start(srcs, lands, name):
    ops = list(srcs or []) + list(lands)
    ns, n = len(srcs or []), len(lands)

    def body(*refs):
        src_refs, land_refs = refs[:ns], refs[ns:ns + n]
        send_sems, recv_sems = refs[ns + n], refs[ns + n + 1]
        for mine_out, _ in _chip_copies(src_refs, land_refs, send_sems, recv_sems):
            mine_out.start()
        refs[-1][...] = jnp.zeros_like(refs[-1])

    sems = pltpu.SemaphoreType.DMA((3 * n,))
    res = pl.pallas_call(
        body, name=name, out_shape=(sems, sems) + tuple(pltpu.HBM(a.shape, a.dtype) for a in ops) + (_sds((8, LANES)),),
        in_specs=[_HBM] * len(ops), out_specs=(_SEM, _SEM) + (_HBM,) * len(ops) + (pl.BlockSpec(memory_space=pltpu.VMEM),),
        input_output_aliases={k: 2 + k for k in range(len(ops))},
        compiler_params=pltpu.CompilerParams(has_side_effects=_EFFECT),
    )(*[pltpu.with_memory_space_constraint(a, pltpu.HBM) for a in ops])
    return res[0], res[1], list(res[2:2 + len(ops)]), res[-1]


def _split_wait(send_sems, recv_sems, thru, n, after, name):
    ns = len(thru) - n

    def body(*refs):
        src_refs, land_refs = refs[:ns], refs[ns:ns + n]
        for mine_out, arriving in _chip_copies(src_refs, land_refs, refs[ns + n], refs[ns + n + 1]):
            mine_out.wait_send()
            arriving.wait_recv()

    res = pl.pallas_call(
        body, name=name, out_shape=tuple(pltpu.HBM(a.shape, a.dtype) for a in thru),
        in_specs=[_HBM] * len(thru) + [_SEM, _SEM, pl.BlockSpec(memory_space=pl.ANY)], out_specs=(_HBM,) * len(thru),
        input_output_aliases={k: k for k in range(len(thru))},
        compiler_params=pltpu.CompilerParams(has_side_effects=_EFFECT),
    )(*thru, send_sems, recv_sems, after)
    return list(res)


def _sibling_swap(arrs, name):
    n = len(arrs)

    def body(*refs):
        ins, outs_, (send_sems, recv_sems) = refs[:n], refs[n:2 * n], refs[2 * n:]
        sib = (lax.axis_index("x"), lax.axis_index("y"), 1 - lax.axis_index("c"))
        cps = [pltpu.make_async_remote_copy(src_ref=ins[i], dst_ref=outs_[i], send_sem=send_sems.at[i], recv_sem=recv_sems.at[i],
                                            device_id=sib, device_id_type=MESH) for i in range(n)]
        for cp in cps:
            cp.start()
        for cp in cps:
            cp.wait_recv()
        for cp in cps:
            cp.wait_send()

    hbm = pl.BlockSpec(memory_space=pltpu.HBM)
    return pl.pallas_call(
        body, name=name, out_shape=tuple(_sds(a.shape, a.dtype) for a in arrs), in_specs=[hbm] * n, out_specs=tuple([hbm] * n),
        scratch_shapes=[pltpu.SemaphoreType.DMA((n,)), pltpu.SemaphoreType.DMA((n,))],
    )(*arrs)


def _tile_spec(rows, cc):
    return pl.BlockSpec((None, rows, cc), lambda l, i: (l, i, 0))


def _cast_bf16(a, rows, name):
    nl, r, cc = a.shape

    def body(a_ref, o_ref):
        o_ref[...] = a_ref[...].astype(jnp.bfloat16)

    return pl.pallas_call(body, name=name, out_shape=_sds((nl, r, cc), jnp.bfloat16), grid=(nl, r // rows),
                          in_specs=[_tile_spec(rows, cc)], out_specs=_tile_spec(rows, cc), compiler_params=_cp())(a)


def _sum_blocks(a, rows, name):
    k, nl, r, cc = a.shape

    def body(a_ref, o_ref):
        acc = a_ref[0].astype(F32)
        for j in range(1, k):
            acc = acc + a_ref[j].astype(F32)
        o_ref[...] = acc

    return pl.pallas_call(body, name=name, out_shape=_sds((nl, r, cc)), grid=(nl, r // rows),
                          in_specs=[pl.BlockSpec((k, None, rows, cc), lambda l, i: (0, l, i, 0))],
                          out_specs=_tile_spec(rows, cc), compiler_params=_cp())(a)


def _sum_chips(lands, srcs, rows, name):
    nl = len(lands)
    _, r, cc = lands[0].shape

    def body(*refs):
        land_refs, src_refs, o_ref = refs[:nl], refs[nl:2 * nl], refs[2 * nl]
        mine = 2 * lax.axis_index("x") + lax.axis_index("y")
        for j in range(nl):
            @pl.when(pl.program_id(0) == j)
            def _(j=j):
                own = src_refs[j][mine].astype(F32)
                acc = None
                for k in range(4):
                    term = jnp.where(mine == k, own, land_refs[j][k].astype(F32))
                    acc = term if acc is None else acc + term
                o_ref[...] = acc

    specs = [pl.BlockSpec((4, rows, cc), lambda l, i, j=j: (0, jnp.where(l == j, i, 0), 0)) for j in range(nl)]
    return pl.pallas_call(body, name=name, out_shape=_sds((nl, r, cc)), grid=(nl, r // rows),
                          in_specs=specs + specs, out_specs=_tile_spec(rows, cc), compiler_params=_cp())(*lands, *srcs)


def _adamw(w, parts, m, v, rows, name, lead=None):
    nl, r, cc = w.shape
    np_ = len(parts)
    c1 = 1.0 / (1.0 - ADAM_B1 ** ADAM_STEP)
    c2 = 1.0 / (1.0 - ADAM_B2 ** ADAM_STEP)

    def body(*refs):
        w_ref, p_refs, (m_ref, v_ref, g_ref, d_ref, nm_ref, nv_ref) = refs[0], refs[1:1 + np_], refs[1 + np_:]
        g = p_refs[0][...]
        for p_ref in p_refs[1:]:
            g = g + p_ref[...]
        nm = ADAM_B1 * m_ref[...] + (1.0 - ADAM_B1) * g
        nv = ADAM_B2 * v_ref[...] + (1.0 - ADAM_B2) * (g * g)
        g_ref[...] = g
        nm_ref[...] = nm
        nv_ref[...] = nv
        d_ref[...] = -ADAM_LR * ((nm * c1) / (jnp.sqrt(nv * c2) + ADAM_EPS) + ADAM_WD * w_ref[...])

    if lead is None:
        spec, grid = _tile_spec(rows, cc), (nl, r // rows)
    else:
        spec, grid = pl.BlockSpec((lead, r, cc), lambda i: (i, 0, 0)), (nl // lead,)
    return pl.pallas_call(body, name=name, out_shape=(_sds((nl, r, cc)),) * 4, grid=grid,
                          in_specs=[spec] * (3 + np_), out_specs=(spec,) * 4, compiler_params=_cp())(w, *parts, m, v)


_BIAS = pltpu.VMEM((2, 2 * BLK, 2 * BLK), F32)


def _fill_band_bias(bias_ref):
    qi = lax.broadcasted_iota(jnp.int32, (2 * BLK, 2 * BLK), 0) & (BLK - 1)
    kj = lax.broadcasted_iota(jnp.int32, (2 * BLK, 2 * BLK), 1)
    dist = BLK + qi - kj
    band = (dist >= 0) & (dist <= BLK)
    bias_ref[0] = jnp.where(band, 0.0, NEG)
    bias_ref[1] = jnp.where(band & (kj >= BLK), 0.0, NEG)


class _HeadStack:
    def __init__(self, group):
        self.m0, self.m1 = _half_masks()
        self.group = group
        if group is not None:
            self.kv_mask = (self.m0, self.m1)[group]

    def _swap_half(self, t, a):
        return t if a == self.group else pltpu.roll(t, HD, axis=1)

    def stack(self, t):
        t0, t1 = t * self.m0, t * self.m1
        if self.group is not None:
            t0, t1 = self._swap_half(t0, 0), self._swap_half(t1, 1)
        return jnp.concatenate([t0, t1], axis=0)

    def unstack(self, ts):
        if self.group is None:
            return ts[:BLK] * self.m0 + ts[BLK:] * self.m1
        return self._swap_half(ts[:BLK] * self.kv_mask, 0) + self._swap_half(ts[BLK:] * self.kv_mask, 1)


def _rows(st, dil):
    if dil == 1:
        return pl.ds(pl.multiple_of(st, BLK), BLK)
    return pl.ds(st, BLK, stride=dil)


def _block_pos(n, dil):
    nb = SEQ // (dil * BLK)
    r, b = n // nb, n % nb
    hp = (b > 0).astype(jnp.int32)
    st = r + dil * BLK * b
    return st, st - dil * BLK * hp, 1 - hp


def _attn_fwd(proj, qblk, kblk, vblk, dils, gqa, sink_x, name):
    has_sink = sink_x is not None

    def body(*refs):
        if has_sink:
            q_ref, k_ref, v_ref, s_ref, o_ref, lse_ref, m_scr, z_scr, bias_scr = refs
        else:
            q_ref, k_ref, v_ref, o_ref, lse_ref, m_scr, z_scr, bias_scr = refs

        @pl.when(pl.program_id(0) == 0)
        def _():
            _fill_band_bias(bias_scr)
        o_ref[...] = jnp.zeros_like(o_ref)
        if has_sink:
            z_scr[...] = jnp.ones_like(z_scr)
            m_scr[...] = jnp.broadcast_to(s_ref[...], m_scr.shape)
        else:
            z_scr[...] = jnp.zeros_like(z_scr)
            m_scr[...] = jnp.full_like(m_scr, NEG)

        def step(n, carry, dil, heads):
            m0, m1 = heads.m0, heads.m1
            st, stp, first = _block_pos(n, dil)
            rq, rp = _rows(st, dil), _rows(stp, dil)
            kk = jnp.concatenate([k_ref[rp, :], k_ref[rq, :]], axis=0)
            vv = jnp.concatenate([v_ref[rp, :], v_ref[rq, :]], axis=0)
            s = _mm(heads.stack(q_ref[rq, :] * QK_SCALE), kk, NT) + bias_scr[first]
            m = jnp.max(s, axis=1, keepdims=True)
            p = jnp.exp(s - m)
            l = jnp.sum(p, axis=1, keepdims=True)
            o_pair = heads.unstack(_mm(p, vv))
            m_pair = m[:BLK] * m0 + m[BLK:] * m1
            l_pair = l[:BLK] * m0 + l[BLK:] * m1
            m_old = m_scr[rq, :]
            m_new = jnp.maximum(m_old, m_pair)
            alpha, beta = jnp.exp(m_old - m_new), jnp.exp(m_pair - m_new)
            o_ref[rq, :] = o_ref[rq, :] * alpha + o_pair * beta
            z_scr[rq, :] = z_scr[rq, :] * alpha + l_pair * beta
            m_scr[rq, :] = m_new
            return carry

        def blocks(heads):
            for dil in dils:
                lax.fori_loop(0, SEQ // BLK, lambda n, carry, dil=dil: step(n, carry, dil, heads), 0, unroll=8)

        if gqa:
            for grp in range(2):
                pl.when(pl.program_id(0) // 2 == grp)(lambda grp=grp: blocks(_HeadStack(grp)))
        else:
            blocks(_HeadStack(None))

        def fin(t, carry):
            rt = pl.ds(pl.multiple_of(t * TM, TM), TM)
            z = z_scr[rt, :]
            o_ref[rt, :] = o_ref[rt, :] / z
            lse_ref[rt, :] = m_scr[rt, :] + jnp.log(z)
            return carry
        lax.fori_loop(0, SEQ // TM, fin, 0)

    col = lambda blk: pl.BlockSpec((SEQ, LANES), lambda p, blk=blk: (0, blk + p))
    kv = (lambda blk: pl.BlockSpec((SEQ, LANES), lambda p, blk=blk: (0, blk))) if gqa else col
    in_specs = [col(qblk), kv(kblk), kv(vblk)]
    args = [proj, proj, proj]
    if has_sink:
        in_specs.append(pl.BlockSpec((1, LANES), lambda p: (0, p)))
        args.append(sink_x)
    out = pl.BlockSpec((SEQ, LANES), lambda p: (0, p))
    return pl.pallas_call(body, name=name, out_shape=(_sds((SEQ, 512)), _sds((SEQ, 512))), grid=(4,),
                          in_specs=in_specs, out_specs=(out, out),
                          scratch_shapes=[pltpu.VMEM((SEQ, LANES), F32), pltpu.VMEM((SEQ, LANES), F32), _BIAS],
                          compiler_params=_cp(48))(*args)


def _attn_bwd(proj, qblk, kblk, vblk, do, o, lse, dils, gqa, sink_x, name):
    has_sink = sink_x is not None

    def body(*refs):
        if has_sink:
            q_ref, k_ref, v_ref, do_ref, o_ref, lse_ref, s_ref, dq_ref, dk_ref, dv_ref, ds_ref, bias_scr = refs
        else:
            q_ref, k_ref, v_ref, do_ref, o_ref, lse_ref, dq_ref, dk_ref, dv_ref, bias_scr = refs
        pid = pl.program_id(0)

        @pl.when(pid == 0)
        def _():
            _fill_band_bias(bias_scr)
        dq_ref[...] = jnp.zeros_like(dq_ref)
        if gqa:
            @pl.when(pid == 0)
            def _():
                dk_ref[...] = jnp.zeros_like(dk_ref)
                dv_ref[...] = jnp.zeros_like(dv_ref)
        else:
            dk_ref[...] = jnp.zeros_like(dk_ref)
            dv_ref[...] = jnp.zeros_like(dv_ref)

        def step(n, carry, dil, heads):
            m0, m1 = heads.m0, heads.m1
            st, stp, first = _block_pos(n, dil)
            rq, rp = _rows(st, dil), _rows(stp, dil)
            do_, lse_ = do_ref[rq, :], lse_ref[rq, :]
            kk = jnp.concatenate([k_ref[rp, :], k_ref[rq, :]], axis=0)
            vv = jnp.concatenate([v_ref[rp, :], v_ref[rq, :]], axis=0)
            qs, dos = heads.stack(q_ref[rq, :] * QK_SCALE), heads.stack(do_)
            doo = do_ * o_ref[rq, :]
            delta = jnp.concatenate([jnp.sum(doo * m0, axis=1, keepdims=True), jnp.sum(doo * m1, axis=1, keepdims=True)], axis=0)
            lse_s = jnp.concatenate([lse_[:, 0:1], lse_[:, HD:HD + 1]], axis=0)
            p = jnp.exp(_mm(qs, kk, NT) + bias_scr[first] - lse_s)
            ds = p * (_mm(dos, vv, NT) - delta)
            dq_ref[rq, :] += heads.unstack(_mm(ds, kk)) * QK_SCALE
            dk_sum, dv_sum = _mm(ds, qs, TN), _mm(p, dos, TN)
            dk_ref[rp, :] += dk_sum[:BLK]
            dk_ref[rq, :] += dk_sum[BLK:]
            dv_ref[rp, :] += dv_sum[:BLK]
            dv_ref[rq, :] += dv_sum[BLK:]
            return carry

        def blocks(heads):
            for dil in dils:
                lax.fori_loop(0, SEQ // BLK, lambda n, carry, dil=dil: step(n, carry, dil, heads), 0, unroll=4)

        if gqa:
            for grp in range(2):
                pl.when(pid // 2 == grp)(lambda grp=grp: blocks(_HeadStack(grp)))
        else:
            blocks(_HeadStack(None))

        if has_sink:
            m0, m1 = _half_masks()

            def sink_rows(t, acc):
                rt = pl.ds(pl.multiple_of(t * TM, TM), TM)
                return acc - jnp.sum(jnp.exp(s_ref[...] - lse_ref[rt, :]) * (do_ref[rt, :] * o_ref[rt, :]), axis=0, keepdims=True)
            acc = lax.fori_loop(0, SEQ // TM, sink_rows, jnp.zeros((1, LANES), F32))
            per_head = jnp.sum(acc * m0, axis=1, keepdims=True) * m0 + jnp.sum(acc * m1, axis=1, keepdims=True) * m1
            ds_ref[0] = jnp.broadcast_to(per_head, (8, LANES))

    col = lambda blk: pl.BlockSpec((SEQ, LANES), lambda p, blk=blk: (0, blk + p))
    kv = (lambda blk: pl.BlockSpec((SEQ, LANES), lambda p, blk=blk: (0, blk))) if gqa else col
    pair = pl.BlockSpec((SEQ, LANES), lambda p: (0, p))
    in_specs = [col(qblk), kv(kblk), kv(vblk), pair, pair, pair]
    args = [proj, proj, proj, do, o, lse]
    kvw = LANES if gqa else 512
    kv_out = pl.BlockSpec((SEQ, LANES), lambda p: (0, 0)) if gqa else pair
    out_shape = [_sds((SEQ, 512)), _sds((SEQ, kvw)), _sds((SEQ, kvw))]
    out_specs = [pair, kv_out, kv_out]
    if has_sink:
        in_specs.append(pl.BlockSpec((1, LANES), lambda p: (0, p)))
        args.append(sink_x)
        out_shape.append(_sds((4, 8, LANES)))
        out_specs.append(pl.BlockSpec((1, 8, LANES), lambda p: (p, 0, 0)))
    return pl.pallas_call(body, name=name, out_shape=tuple(out_shape), grid=(4,), in_specs=in_specs,
                          out_specs=tuple(out_specs), scratch_shapes=[_BIAS], compiler_params=_cp(56))(*args)


_CT = 128


def _rows_before(x_ref, t, k):
    if t == 0:
        return jnp.concatenate([jnp.zeros((k, LANES), F32), x_ref[0:_CT - k, :]], axis=0)
    return x_ref[t * _CT - k:(t + 1) * _CT - k, :]


def _conv_pre(x_ref, w_ref, b_ref, t):
    taps = [x_ref[t * _CT:(t + 1) * _CT, :]] + [_rows_before(x_ref, t, k) for k in range(1, 4)]
    u = b_ref[...] + taps[0] * w_ref[3:4, :]
    for k in range(1, 4):
        u = u + taps[k] * w_ref[3 - k:4 - k, :]
    return u, taps


def _conv_fwd(proj, w, b, name):
    def body(x_ref, w_ref, b_ref, o_ref):
        for t in range(SEQ // _CT):
            o_ref[t * _CT:(t + 1) * _CT, :] = _silu(_conv_pre(x_ref, w_ref, b_ref, t)[0])

    nblk = CONV_CH // LANES
    return pl.pallas_call(body, name=name, out_shape=_sds((SEQ, CONV_CH)), grid=(nblk,),
                          in_specs=[pl.BlockSpec((SEQ, LANES), lambda j: (0, XBC // LANES + j)),
                                    pl.BlockSpec((4, LANES), lambda j: (0, j)), pl.BlockSpec((1, LANES), lambda j: (0, j))],
                          out_specs=pl.BlockSpec((SEQ, LANES), lambda j: (0, j)), compiler_params=_cp())(proj, w, b)


def _conv_bwd(proj, dact, w, b, name):
    def body(x_ref, da_ref, w_ref, b_ref, dx_ref, dw_ref, db_ref, du_scr):
        du_scr[SEQ:SEQ + 8, :] = jnp.zeros((8, LANES), F32)
        db = jnp.zeros((1, LANES), F32)
        dws = [jnp.zeros((1, LANES), F32)] * 4
        for t in range(SEQ // _CT):
            u, taps = _conv_pre(x_ref, w_ref, b_ref, t)
            du = da_ref[t * _CT:(t + 1) * _CT, :] * _dsilu(u)
            du_scr[t * _CT:(t + 1) * _CT, :] = du
            db = db + jnp.sum(du, axis=0, keepdims=True)
            dws = [dws[k] + jnp.sum(du * taps[k], axis=0, keepdims=True) for k in range(4)]
        db_ref[...] = db
        for k in range(4):
            dw_ref[3 - k:4 - k, :] = dws[k]
        for t in range(SEQ // _CT):
            dx = du_scr[t * _CT:(t + 1) * _CT, :] * w_ref[3:4, :]
            for k in range(1, 4):
                dx = dx + du_scr[t * _CT + k:(t + 1) * _CT + k, :] * w_ref[3 - k:4 - k, :]
            dx_ref[t * _CT:(t + 1) * _CT, :] = dx.astype(dx_ref.dtype)

    nblk = CONV_CH // LANES
    blk = pl.BlockSpec((SEQ, LANES), lambda j: (0, j))
    wspec, bspec = pl.BlockSpec((4, LANES), lambda j: (0, j)), pl.BlockSpec((1, LANES), lambda j: (0, j))
    return pl.pallas_call(body, name=name, out_shape=(_sds((SEQ, CONV_CH), MXU), _sds((4, CONV_CH)), _sds((1, CONV_CH))), grid=(nblk,),
                          in_specs=[pl.BlockSpec((SEQ, LANES), lambda j: (0, XBC // LANES + j)), blk, wspec, bspec],
                          out_specs=(blk, wspec, bspec), scratch_shapes=[pltpu.VMEM((SEQ + 8, LANES), F32)],
                          compiler_params=_cp())(proj, dact, w, b)


def _ssd_chunk(xs, bm, cm, dtr, z, hs, al16, dtb, dskx, nw):
    m0, m1 = _half_masks()
    row = lax.broadcasted_iota(jnp.int32, (BLK, BLK), 0)
    col = lax.broadcasted_iota(jnp.int32, (BLK, BLK), 1)
    causal = row >= col
    tril = causal.astype(F32)
    lane = lax.broadcasted_iota(jnp.int32, (1, LANES), 1)
    sub = lax.broadcasted_iota(jnp.int32, (BLK, 1), 0)
    last_row = (sub == BLK - 1).astype(F32)
    dt = jnp.where(lane < 16, _softplus(dtr + dtb), 0.0)
    a16 = -jnp.exp(al16)
    acum = jnp.dot(tril, dt * a16, precision=HI, preferred_element_type=F32)
    acum_t = acum.T
    gmat = [_mm(cm[g], bm[g], NT) for g in range(2)]
    ys, hn = [], []
    for p in range(8):
        g = p // 4
        pick = [(lane == 2 * p + a).astype(F32) for a in range(2)]
        col_h = [jnp.sum(acum * pick[a], axis=1, keepdims=True) for a in range(2)]
        dt_x = sum(jnp.sum(dt * pick[a], axis=1, keepdims=True) * msk for a, msk in enumerate((m0, m1)))
        ac_x = col_h[0] * m0 + col_h[1] * m1
        a_end = jnp.sum(ac_x * last_row, axis=0, keepdims=True)
        xdt = xs[p] * dt_x
        y = _mm(cm[g], hs[p]) * jnp.exp(ac_x)
        for a, msk in enumerate((m0, m1)):
            row_h = jnp.sum(acum_t * (sub == 2 * p + a).astype(F32), axis=0, keepdims=True)
            decay = jnp.exp(jnp.where(causal, col_h[a] - row_h, NEG))
            y = y + _mm(gmat[g] * decay, xdt * msk)
        st = _mm(bm[g], xdt * jnp.exp(a_end - ac_x), TN)
        hn.append(hs[p] * jnp.exp(a_end) + st)
        y = y + dskx[p] * xs[p]
        ys.append(y * _silu(z[p]))
    out = []
    for g in range(2):
        ms = sum(jnp.sum(ys[p] * ys[p], axis=1, keepdims=True) for p in range(4 * g, 4 * g + 4)) * (1.0 / 512)
        rstd = lax.rsqrt(ms + EPS)
        out += [ys[p] * rstd * nw[p] for p in range(4 * g, 4 * g + 4)]
    return out, hn


def _tiles(ref, n, off=0):
    return [ref[:, off + LANES * p:off + LANES * (p + 1)] for p in range(n)]


def _ssd_load(xbc_ref, z_ref, dt_ref, al16_ref, dtb_ref, dsk_ref, nw_ref):
    return (_tiles(xbc_ref, 8), _tiles(xbc_ref, 2, 1024), _tiles(xbc_ref, 2, 1280), dt_ref[...], _tiles(z_ref, 8)), \
           (al16_ref[...], dtb_ref[...], _tiles(dsk_ref, 8), _tiles(nw_ref, 8))


_NCH = SEQ // BLK


def _ssd_param_specs():
    return [_full((1, LANES)), _full((1, LANES)), _full((1, 1024)), _full((1, 1024))]


def _ssd_fwd(xbc_act, proj, al16, dtb, dskx, nw, name):
    def body(xbc_ref, z_ref, dt_ref, al16_ref, dtb_ref, dsk_ref, nw_ref, y_ref, hin_ref, h_scr):
        @pl.when(pl.program_id(0) == 0)
        def _():
            h_scr[...] = jnp.zeros_like(h_scr)
        acts, params = _ssd_load(xbc_ref, z_ref, dt_ref, al16_ref, dtb_ref, dsk_ref, nw_ref)
        hs = _tiles(h_scr, 8)
        hin_ref[0] = h_scr[...]
        ys, hn = _ssd_chunk(*acts, hs, *params)
        for p in range(8):
            y_ref[:, LANES * p:LANES * (p + 1)] = ys[p].astype(y_ref.dtype)
            h_scr[:, LANES * p:LANES * (p + 1)] = hn[p]

    return pl.pallas_call(
        body, name=name, out_shape=(_sds((SEQ, 1024), MXU), _sds((_NCH, BLK, 1024))), grid=(_NCH,),
        in_specs=[pl.BlockSpec((BLK, CONV_CH), lambda c: (c, 0)), pl.BlockSpec((BLK, 1024), lambda c: (c, ZB // 1024)),
                  pl.BlockSpec((BLK, LANES), lambda c: (c, DTC // LANES))] + _ssd_param_specs(),
        out_specs=(pl.BlockSpec((BLK, 1024), lambda c: (c, 0)), pl.BlockSpec((1, BLK, 1024), lambda c: (c, 0, 0))),
        scratch_shapes=[pltpu.VMEM((BLK, 1024), F32)], compiler_params=_cp())(xbc_act, proj, proj, al16, dtb, dskx, nw)


def _ssd_bwd(xbc_act, proj, hin, dyb, al16, dtb, dskx, nw, name):
    def body(xbc_ref, z_ref, dt_ref, hin_ref, dy_ref, al16_ref, dtb_ref, dsk_ref, nw_ref,
             dxbc_ref, dz_ref, ddt_ref, dal16_ref, ddtb_ref, ddsk_ref, dnw_ref, dh_scr):
        @pl.when(pl.program_id(0) == 0)
        def _():
            dh_scr[...] = jnp.zeros_like(dh_scr)
            for r in (dal16_ref, ddtb_ref, ddsk_ref, dnw_ref):
                r[...] = jnp.zeros_like(r)
        acts, params = _ssd_load(xbc_ref, z_ref, dt_ref, al16_ref, dtb_ref, dsk_ref, nw_ref)
        hs = [hin_ref[0, :, LANES * p:LANES * (p + 1)] for p in range(8)]
        _, vjp = jax.vjp(lambda a, h, q: _ssd_chunk(*a, h, *q), acts, hs, params)
        (dxs, dbm, dcm, ddt, dz), dhs, (dal16, ddtb, ddsk, dnw) = vjp((_tiles(dy_ref, 8), _tiles(dh_scr, 8)))
        for p in range(8):
            cols = slice(LANES * p, LANES * (p + 1))
            dxbc_ref[:, cols] = dxs[p]
            dz_ref[:, cols] = dz[p].astype(dz_ref.dtype)
            dh_scr[:, cols] = dhs[p]
            ddsk_ref[:, cols] += ddsk[p]
            dnw_ref[:, cols] += dnw[p]
        for g in range(2):
            dxbc_ref[:, 1024 + LANES * g:1024 + LANES * (g + 1)] = dbm[g]
            dxbc_ref[:, 1280 + LANES * g:1280 + LANES * (g + 1)] = dcm[g]
        ddt_ref[...] = ddt.astype(ddt_ref.dtype)
        dal16_ref[...] += dal16
        ddtb_ref[...] += ddtb

    rev = lambda c: _NCH - 1 - c
    return pl.pallas_call(
        body, name=name,
        out_shape=(_sds((SEQ, CONV_CH)), _sds((SEQ, 1024), MXU), _sds((SEQ, LANES), MXU),
                   _sds((1, LANES)), _sds((1, LANES)), _sds((1, 1024)), _sds((1, 1024))),
        grid=(_NCH,),
        in_specs=[pl.BlockSpec((BLK, CONV_CH), lambda c: (rev(c), 0)), pl.BlockSpec((BLK, 1024), lambda c: (rev(c), ZB // 1024)),
                  pl.BlockSpec((BLK, LANES), lambda c: (rev(c), DTC // LANES)), pl.BlockSpec((1, BLK, 1024), lambda c: (rev(c), 0, 0)),
                  pl.BlockSpec((BLK, 1024), lambda c: (rev(c), 0))] + _ssd_param_specs(),
        out_specs=(pl.BlockSpec((BLK, CONV_CH), lambda c: (rev(c), 0)), pl.BlockSpec((BLK, 1024), lambda c: (rev(c), 0)),
                   pl.BlockSpec((BLK, LANES), lambda c: (rev(c), 0)),
                   _full((1, LANES)), _full((1, LANES)), _full((1, 1024)), _full((1, 1024))),
        scratch_shapes=[pltpu.VMEM((BLK, 1024), F32)], compiler_params=_cp())(xbc_act, proj, proj, hin, dyb, al16, dtb, dskx, nw)


def _rstd(v):
    return lax.rsqrt(jnp.mean(v * v, axis=1, keepdims=True) + EPS)


def _rms_bwd(dn, n, rstd):
    return rstd * (dn - n * jnp.mean(dn * n, axis=1, keepdims=True))


_VEC = _full((1, D))


def _layer_spec(layer):
    return pl.BlockSpec((None, 2048, D), lambda *_: (layer, 0, 0))

_ROW = pl.BlockSpec((TM, D), lambda i, *_: (i, 0))


def _proj_fwd(x, pre_w, scale, shift, w, layer, name):
    tn, ni = 1024, SEQ // TM

    def body(x_ref, pw_ref, sc_ref, sh_ref, w_ref, o_ref, h_ref, h_scr):
        rows = pl.ds(pl.multiple_of(pl.program_id(1) * TM, TM), TM)

        @pl.when(pl.program_id(0) == 0)
        def _():
            xv = x_ref[...]
            h = ((xv * _rstd(xv) * pw_ref[...]) * (1.0 + sc_ref[...]) + sh_ref[...]).astype(h_ref.dtype)
            h_scr[rows, :] = h
            h_ref[...] = h
        o_ref[...] = jnp.dot(h_scr[rows, :], w_ref[...].astype(MXU), preferred_element_type=F32)

    first_pass = pl.BlockSpec((TM, D), lambda j, i: (jnp.where(j == 0, i, ni - 1), 0))
    return pl.pallas_call(body, name=name, out_shape=(_sds((SEQ, NP)), _sds((SEQ, D), MXU)), grid=(NP // tn, ni),
                          in_specs=[first_pass, _VEC, _VEC, _VEC, pl.BlockSpec((None, D, tn), lambda j, i: (layer, 0, j))],
                          out_specs=(pl.BlockSpec((TM, tn), lambda j, i: (i, j)), first_pass),
                          scratch_shapes=[pltpu.VMEM((SEQ, D), MXU)], compiler_params=_cp())(x, pre_w, scale, shift, w)


_HALF = pl.BlockSpec((TM, 512), lambda i: (i, 0))
_Z_A = pl.BlockSpec((TM, 512), lambda i: (i, ZA // 512))
_Z_C = pl.BlockSpec((TM, 512), lambda i: (i, ZC // 512))


def _out_fwd(o_a, yb, o_c, proj, w, layer, x, gate, post_w, name):
    def body(oa_ref, yb_ref, oc_ref, za_ref, zc_ref, w_ref, x_ref, g_ref, pw_ref, xn_ref, y_ref):
        y = (_mm(oa_ref[...] * _silu(za_ref[...]), w_ref[0:512, :]) + _mm(yb_ref[...], w_ref[512:1536, :])
             + _mm(oc_ref[...] * _silu(zc_ref[...]), w_ref[1536:2048, :]))
        y_ref[...] = y
        xn_ref[...] = x_ref[...] + g_ref[...] * (y * _rstd(y) * pw_ref[...])

    return pl.pallas_call(body, name=name, out_shape=(_sds((SEQ, D)), _sds((SEQ, D))), grid=(SEQ // TM,),
                          in_specs=[_HALF, _ROW, _HALF, _Z_A, _Z_C, _layer_spec(layer), _ROW, _VEC, _VEC],
                          out_specs=(_ROW, _ROW), compiler_params=_cp())(o_a, yb, o_c, proj, proj, w, x, gate, post_w)


def _dymix(dxo, y, gate, post_w, w, layer, o_a, o_c, proj, name):
    def body(dx_ref, y_ref, g_ref, pw_ref, w_ref, oa_ref, oc_ref, za_ref, zc_ref,
             dy_ref, dg_ref, dpw_ref, doa_ref, dza_ref, b_ref, doc_ref, dzc_ref):
        @pl.when(pl.program_id(0) == 0)
        def _():
            dg_ref[...] = jnp.zeros_like(dg_ref)
            dpw_ref[...] = jnp.zeros_like(dpw_ref)
        dx, yv = dx_ref[...], y_ref[...]
        rstd = _rstd(yv)
        n = yv * rstd
        dg_ref[...] += jnp.sum(dx * (n * pw_ref[...]), axis=0, keepdims=True)
        dr = dx * g_ref[...]
        dpw_ref[...] += jnp.sum(dr * n, axis=0, keepdims=True)
        dy = _rms_bwd(dr * pw_ref[...], n, rstd)
        dy_ref[...] = dy
        b_ref[...] = _mm(dy, w_ref[512:1536, :], NT)
        for rows, o_ref, z_ref, do_ref, dz_ref in ((slice(0, 512), oa_ref, za_ref, doa_ref, dza_ref),
                                                   (slice(1536, 2048), oc_ref, zc_ref, doc_ref, dzc_ref)):
            dyg, z = _mm(dy, w_ref[rows, :], NT), z_ref[...]
            do_ref[...] = dyg * _silu(z)
            dz_ref[...] = (dyg * o_ref[...] * _dsilu(z)).astype(dz_ref.dtype)

    return pl.pallas_call(body, name=name,
                          out_shape=(_sds((SEQ, D)), _sds((1, D)), _sds((1, D)),
                                     _sds((SEQ, 512)), _sds((SEQ, 512), MXU), _sds((SEQ, D)), _sds((SEQ, 512)), _sds((SEQ, 512), MXU)),
                          grid=(SEQ // TM,), in_specs=[_ROW, _ROW, _VEC, _VEC, _layer_spec(layer), _HALF, _HALF, _Z_A, _Z_C],
                          out_specs=(_ROW, _VEC, _VEC, _HALF, _HALF, _ROW, _HALF, _HALF),
                          compiler_params=_cp())(dxo, y, gate, post_w, w, o_a, o_c, proj, proj)


def _dwout(o_a, yb, o_c, proj, dy, name):
    def body(oa_ref, yb_ref, oc_ref, za_ref, zc_ref, dy_ref, o_ref):
        @pl.when(pl.program_id(0) == 0)
        def _():
            o_ref[...] = jnp.zeros_like(o_ref)
        dy = dy_ref[...]
        o_ref[0:512, :] += _mm(oa_ref[...] * _silu(za_ref[...]), dy, TN)
        o_ref[512:1536, :] += _mm(yb_ref[...], dy, TN)
        o_ref[1536:2048, :] += _mm(oc_ref[...] * _silu(zc_ref[...]), dy, TN)

    return pl.pallas_call(body, name=name, out_shape=_sds((2048, D)), grid=(SEQ // TM,),
                          in_specs=[_HALF, _ROW, _HALF, _Z_A, _Z_C, _ROW], out_specs=_full((2048, D)),
                          compiler_params=_cp())(o_a, yb, o_c, proj, proj, dy)


def _dwin(h, pieces, name):
    n = len(pieces)
    widths = [p.shape[1] for p in pieces]
    half = NP // 2

    def body(*refs):
        h_ref, p_refs, o_ref = refs[0], refs[1:1 + n], refs[1 + n]

        @pl.when(pl.program_id(0) == 0)
        def _():
            o_ref[...] = jnp.zeros_like(o_ref)
        hv, c0 = h_ref[...], 0
        for p_ref, wd in zip(p_refs, widths):
            o_ref[:, c0:c0 + wd] += _mm(hv, p_ref[...], TN)
            c0 += wd

    return pl.pallas_call(body, name=name, out_shape=_sds((D, half)), grid=(SEQ // TM,),
                          in_specs=[_ROW] + [pl.BlockSpec((TM, wd), lambda k: (k, 0)) for wd in widths],
                          out_specs=_full((D, half)), compiler_params=_cp(56))(h, *pieces)


_TMH = 256


def _dh_bwd(pieces, w, x, pre_w, scale, dxo, name):
    n = len(pieces)
    widths = [p.shape[1] for p in pieces]

    def body(*refs):
        p_refs, (w_ref, x_ref, pw_ref, sc_ref, dxo_ref, dx_ref, dsh_ref, dsc_ref, dpw_ref) = refs[:n], refs[n:]

        @pl.when(pl.program_id(0) == 0)
        def _():
            for r in (dsh_ref, dsc_ref, dpw_ref):
                r[...] = jnp.zeros_like(r)
        dh, c0 = 0.0, 0
        for p_ref, wd in zip(p_refs, widths):
            dh = dh + _mm(p_ref[...], w_ref[:, c0:c0 + wd], NT)
            c0 += wd
        xv = x_ref[...]
        rstd = _rstd(xv)
        nrm = xv * rstd
        dsh_ref[...] += jnp.sum(dh, axis=0, keepdims=True)
        dsc_ref[...] += jnp.sum(dh * (nrm * pw_ref[...]), axis=0, keepdims=True)
        dhn = dh * (1.0 + sc_ref[...])
        dpw_ref[...] += jnp.sum(dhn * nrm, axis=0, keepdims=True)
        dx_ref[...] = _rms_bwd(dhn * pw_ref[...], nrm, rstd) + dxo_ref[...]

    row = pl.BlockSpec((_TMH, D), lambda i: (i, 0))
    return pl.pallas_call(body, name=name, out_shape=(_sds((SEQ, D)), _sds((1, D)), _sds((1, D)), _sds((1, D))),
                          grid=(SEQ // _TMH,),
                          in_specs=[pl.BlockSpec((_TMH, wd), lambda i: (i, 0)) for wd in widths]
                          + [pl.BlockSpec((None, D, NP), lambda i: (0, 0, 0)), row, _VEC, _VEC, row],
                          out_specs=(row, _VEC, _VEC, _VEC), compiler_params=_cp(56))(*pieces, w, x, pre_w, scale, dxo)


def _w_in_padded(land, name):
    rows = 128

    def body(l_ref, o_ref):
        o_ref[...] = _pad_cols(jnp.concatenate([l_ref[k] for k in range(4)], axis=1))

    return pl.pallas_call(body, name=name, out_shape=_sds((D, NP), land.dtype), grid=(D // rows,),
                          in_specs=[pl.BlockSpec((4, rows, SHARD_IN), lambda i: (0, i, 0))],
                          out_specs=pl.BlockSpec((rows, NP), lambda i: (i, 0)), compiler_params=_cp())(land)


def _grad_blocks(dwa, dwb, name):
    rows = 128

    def body(a_ref, b_ref, o_ref):
        g = _unpad_cols(jnp.concatenate([a_ref[...], b_ref[...]], axis=1))
        for k in range(4):
            o_ref[k] = g[:, SHARD_IN * k:SHARD_IN * (k + 1)].astype(o_ref.dtype)

    half = pl.BlockSpec((rows, NP // 2), lambda i: (i, 0))
    return pl.pallas_call(body, name=name, out_shape=_sds((4, D, SHARD_IN), jnp.bfloat16), grid=(D // rows,),
                          in_specs=[half, half], out_specs=pl.BlockSpec((4, rows, SHARD_IN), lambda i: (0, i, 0)),
                          compiler_params=_cp())(dwa, dwb)


def _loss_bwd(xf, tgt, name):
    def body(x_ref, t_ref, dx_ref, l_ref):
        @pl.when(pl.program_id(0) == 0)
        def _():
            l_ref[...] = jnp.zeros_like(l_ref)
        e = x_ref[...] - t_ref[...]
        dx_ref[...] = e * (1.0 / D)
        l_ref[...] += 0.5 * jnp.sum(jnp.mean(e * e, axis=1, keepdims=True), axis=0, keepdims=True)

    return pl.pallas_call(body, name=name, out_shape=(_sds((SEQ, D)), _sds((8, LANES))), grid=(SEQ // TM,),
                          in_specs=[_ROW, _ROW], out_specs=(_ROW, _full((8, LANES))), compiler_params=_cp())(xf, tgt)


def _mod_part(c_all, ada_w, ada_b, name):
    def body(c_ref, w_ref, b_ref, o_ref):
        o_ref[0] = _mm(_silu(c_ref[...]), w_ref[0]) + b_ref[0]

    return pl.pallas_call(body, name=name, out_shape=_sds((DEPTH, 8, 768)), grid=(DEPTH,),
                          in_specs=[_full((8, D)), pl.BlockSpec((1, D, 768), lambda i: (i, 0, 0)), pl.BlockSpec((1, 1, 768), lambda i: (i, 0, 0))],
                          out_specs=pl.BlockSpec((1, 8, 768), lambda i: (i, 0, 0)), compiler_params=_cp())(c_all, ada_w, ada_b)


def _ada_grad(c_t, dmod, name):
    def body(c_ref, d_ref, o_ref):
        ca = _silu(c_ref[...])
        dm = d_ref[0]
        acc = ca[:, 0:1] * dm[0:1, :]
        for s in range(1, 8):
            acc = acc + ca[:, s:s + 1] * dm[s:s + 1, :]
        o_ref[0] = acc

    return pl.pallas_call(body, name=name, out_shape=_sds((DEPTH, D, 768)), grid=(DEPTH,),
                          in_specs=[_full((D, LANES)), pl.BlockSpec((1, 8, 768), lambda i: (i, 0, 0))],
                          out_specs=pl.BlockSpec((1, D, 768), lambda i: (i, 0, 0)), compiler_params=_cp())(c_t, dmod)


def _pack(parts):
    flat = []
    for p in parts:
        f = p.reshape(-1)
        flat.append(jnp.pad(f, (0, (-f.size) % LANES)))
    v = jnp.concatenate(flat)
    return jnp.pad(v, (0, (-v.size) % (8 * LANES))).reshape(-1, LANES)


def _unpack(v, shapes):
    v = v.reshape(-1)
    out, off = [], 0
    for s in shapes:
        n = math.prod(s)
        out.append(v[off:off + n].reshape(s))
        off += n + (-n) % LANES
    return out


_GIVEN_DT, _GIVEN_C = 4608, 4624


def _pad_cols(w):
    return jnp.concatenate([w[..., :_GIVEN_DT], w[..., _GIVEN_C:], w[..., _GIVEN_DT:_GIVEN_C],
                            jnp.zeros(w.shape[:-1] + (NP - IN_COLS,), w.dtype)], axis=-1)


def _unpad_cols(w):
    return jnp.concatenate([w[..., :_GIVEN_DT], w[..., DTC:DTC + 16], w[..., _GIVEN_DT:DTC]], axis=-1)


def _pad_lanes(v):
    return jnp.pad(v, (0, LANES - v.shape[0])).reshape(1, LANES)


def _local_step(x2, tgt, mod, weights_of, grads_done, pre_w, post_w, conv_w, conv_b, dt_bias, a_log, d_skip, nw, sinks):
    saved = []
    xcur = x2
    for i in range(DEPTH):
        shift, scale, gate = mod[i:i + 1, :D], mod[i:i + 1, D:2 * D], mod[i:i + 1, 2 * D:]
        pw, qw = pre_w[i:i + 1], post_w[i:i + 1]
        w_p, w_o = weights_of(i, xcur)
        proj, h = _proj_fwd(xcur, pw, scale, shift, w_p, 0, "proj_fwd")
        o_a, lse_a = _attn_fwd(proj, QA // LANES, KA // LANES, VA // LANES, DILS, False, None, "attn_a_fwd")
        sink_x = jnp.repeat(sinks[i], HD).reshape(1, 512)
        o_c, lse_c = _attn_fwd(proj, QC // LANES, KC // LANES, VC // LANES, (1,), True, sink_x, "attn_c_fwd")
        cw, cb = conv_w[i], conv_b[i:i + 1]
        xbc_act = _conv_fwd(proj, cw, cb, "conv_fwd")
        ssd_p = (_pad_lanes(a_log[i]), _pad_lanes(dt_bias[i]), jnp.repeat(d_skip[i], HD).reshape(1, 1024), nw[i:i + 1])
        yb, hin = _ssd_fwd(xbc_act, proj, *ssd_p, "ssd_fwd")
        xnew, y = _out_fwd(o_a, yb, o_c, proj, w_o, 0, xcur, gate, qw, "out_fwd")
        saved.append((w_p, w_o, xcur, scale, gate, pw, qw, proj, h, o_a, lse_a, sink_x, o_c, lse_c, cw, cb, xbc_act, ssd_p, yb, hin, y))
        xcur = xnew
    dx, ltile = _loss_bwd(xcur, tgt, "loss")
    dmod, small = [None] * DEPTH, [None] * DEPTH
    for i in reversed(range(DEPTH)):
        w_p, w_o, xin, scale, gate, pw, qw, proj, h, o_a, lse_a, sink_x, o_c, lse_c, cw, cb, xbc_act, ssd_p, yb, hin, y = saved[i]
        dy, dgate, dpost, do_a, dz_a, dyb, do_c, dz_c = _dymix(dx, y, gate, qw, w_o, 0, o_a, o_c, proj, "dymix")
        dwo = _dwout(o_a, yb, o_c, proj, dy, "dwout")
        dq_a, dk_a, dv_a = _attn_bwd(proj, QA // LANES, KA // LANES, VA // LANES, do_a, o_a, lse_a, DILS, False, None, "attn_a_bwd")
        dq_c, dk_c, dv_c, dsk = _attn_bwd(proj, QC // LANES, KC // LANES, VC // LANES, do_c, o_c, lse_c, (1,), True, sink_x, "attn_c_bwd")
        dxbc_act, dz_b, ddt, dal16, ddtb, ddsk, dnw = _ssd_bwd(xbc_act, proj, hin, dyb, *ssd_p, "ssd_bwd")
        dxbc, dcw, dcb = _conv_bwd(proj, dxbc_act, cw, cb, "conv_bwd")
        half_a, half_b = [dq_a, dk_a, dv_a, dz_a, dz_b], [dxbc, dq_c, dz_c, dk_c, dv_c, ddt]
        sent = grads_done(i, _dwin(h, half_a, "dwin_a"), _dwin(h, half_b, "dwin_b"), dwo)
        dx, dshift, dscale, dpre = _dh_bwd(half_a + half_b, w_p, xin, pw, scale + sent[0, 0], dx, "dh_bwd")
        dmod[i] = jnp.concatenate([dshift, dscale, dgate], axis=1)
        small[i] = (dpre, dpost, dcw, dcb, ddtb[0, :16], dal16[0, :16], ddsk.reshape(16, HD).sum(axis=1), dnw, dsk[:, 0, ::HD].reshape(8))
    return ltile, dx, jnp.concatenate(dmod, axis=0), small


_SMALL = ((1, D), (1, D), (4, CONV_CH), (1, CONV_CH), (16,), (16,), (16,), (1, D), (8,))


def kernel(x, c, ada_w, ada_b, pre_norm_w, post_norm_w, w_in, conv_w, conv_b, dt_bias, a_log, d_skip, ssm_norm_w, sinks, w_out, loss_target, m_ada_w, m_ada_b, m_pre_norm_w, m_post_norm_w, m_w_in, m_conv_w, m_conv_b, m_dt_bias, m_a_log, m_d_skip, m_ssm_norm_w, m_sinks, m_w_out, v_ada_w, v_ada_b, v_pre_norm_w, v_post_norm_w, v_w_in, v_conv_w, v_conv_b, v_dt_bias, v_a_log, v_d_skip, v_ssm_norm_w, v_sinks, v_w_out):
    xi, yi, ci = lax.axis_index("x"), lax.axis_index("y"), lax.axis_index("c")
    chip = 2 * xi + yi
    me = 2 * chip + ci

    w_in_b = _cast_bf16(w_in, 512, "cast_w_in")
    w_out_b = _cast_bf16(w_out, 512, "cast_w_out")
    gathers = []
    for i in range(DEPTH):
        lands = [lax.dynamic_update_slice(lax.empty((4,) + a.shape[1:], a.dtype), a[i][None], (chip, 0, 0)) for a in (w_in_b, w_out_b)]
        gathers.append(_split_start(None, lands, f"gather_start{i}"))
    all_started = gathers[0][3] + gathers[1][3] + gathers[2][3] + gathers[3][3]

    def weights_of(i, after):
        send_sems, recv_sems, thru, _ = gathers[i]
        if i == 0:
            after = all_started + mod[:1, :LANES]
        g_in, g_out = _split_wait(send_sems, recv_sems, thru, 2, after, f"gather_wait{i}")
        return _w_in_padded(g_in, "w_in_padded")[None], g_out.reshape(1, 2048, D)

    scatters = [None] * DEPTH

    def grads_done(i, dwa, dwb, dwo):
        blocks = [_grad_blocks(dwa, dwb, "grad_blocks"), _cast_bf16(dwo.reshape(4, 512, D), 512, "cast_dw_out")]
        scatters[i] = _split_start(blocks, [lax.empty(b.shape, b.dtype) for b in blocks], f"scatter_start{i}")
        return scatters[i][3]

    g0 = _allgather8(_pack([c, conv_w]), "gather_c")
    c_all = g0[:, :8, :].reshape(8, D)
    conv_w_full = jnp.concatenate([g0[2 * k, 8:56, :].reshape(DEPTH, 4, CONV_CH // 4) for k in range(4)], axis=-1)

    ada_b_mine = lax.dynamic_slice_in_dim(ada_b, 768 * chip, 768, axis=1).reshape(DEPTH, 1, 768)
    gm = _allgather8(_mod_part(c_all, ada_w, ada_b_mine, "mod_part").reshape(DEPTH * 8, 768), "gather_mod")
    gm = gm.reshape(4, 2, DEPTH, 8, 768)[:, 0]
    mod = lax.dynamic_index_in_dim(gm, me, axis=2, keepdims=False).transpose(1, 0, 2).reshape(DEPTH, 3 * D)

    ltile, dx, dmod, small = _local_step(x[0], loss_target[0], mod, weights_of, grads_done, pre_norm_w, post_norm_w, conv_w_full,
                                         conv_b, dt_bias, a_log, d_skip, ssm_norm_w, sinks)

    packed = _pack([dmod] + [g for layer in small for g in layer] + [ltile[0]])
    gs = _allgather8(packed, "gather_small")
    tot = _sum_blocks(gs[:, None], packed.shape[0], "sum_small")[0]
    parts = _unpack(tot, [(DEPTH, 3 * D)] + list(_SMALL) * DEPTH + [(LANES,)])
    g_ada_b, loss = parts[0], parts[-1][0]
    per_layer = [parts[1 + len(_SMALL) * i:1 + len(_SMALL) * (i + 1)] for i in range(DEPTH)]
    g_pre, g_post, g_cw, g_cb, g_dtb, g_al, g_dsk, g_nw, g_sk = [jnp.stack([per_layer[i][j] for i in range(DEPTH)]) for j in range(len(_SMALL))]
    g_pre, g_post, g_cb, g_nw = g_pre[:, 0], g_post[:, 0], g_cb[:, 0], g_nw[:, 0]
    g_cw = lax.dynamic_slice_in_dim(g_cw, (CONV_CH // 4) * chip, CONV_CH // 4, axis=2)

    dmod_all = gs[:, :(DEPTH * 3 * D) // LANES, :].reshape(8, DEPTH, 3 * D).transpose(1, 0, 2)
    dmod_mine = lax.dynamic_slice_in_dim(dmod_all, 768 * chip, 768, axis=2)
    c_t = jnp.pad(c_all.T, ((0, 0), (0, LANES - 8)))
    g_ada_w = _ada_grad(c_t, dmod_mine, "ada_grad")

    res = {}
    res["ada_w"] = _adamw(ada_w, [g_ada_w], m_ada_w, v_ada_w, 512, "adamw_ada_w")
    names = ["ada_b", "pre_norm_w", "post_norm_w", "conv_w", "conv_b", "dt_bias", "a_log", "d_skip", "ssm_norm_w", "sinks"]
    ws = [ada_b, pre_norm_w, post_norm_w, conv_w, conv_b, dt_bias, a_log, d_skip, ssm_norm_w, sinks]
    gsm = [g_ada_b, g_pre, g_post, g_cw, g_cb, g_dtb, g_al, g_dsk, g_nw, g_sk]
    ms = [m_ada_b, m_pre_norm_w, m_post_norm_w, m_conv_w, m_conv_b, m_dt_bias, m_a_log, m_d_skip, m_ssm_norm_w, m_sinks]
    vs = [v_ada_b, v_pre_norm_w, v_post_norm_w, v_conv_w, v_conv_b, v_dt_bias, v_a_log, v_d_skip, v_ssm_norm_w, v_sinks]
    pw_, pg_, pm_, pv_ = _pack(ws), _pack(gsm), _pack(ms), _pack(vs)
    small_out = _adamw(pw_[None], [pg_[None]], pm_[None], pv_[None], pw_.shape[0], "adamw_small")

    others_done = small_out[1][0, :8] + res["ada_w"][1][0, :8, :LANES]
    landed = [_split_wait(*scatters[i][:3], 2, others_done, f"scatter_wait{i}") for i in range(DEPTH)]
    p_in = _sum_chips([d[2] for d in landed], [d[0] for d in landed], 128, "sum_w_in")
    p_out = _sum_chips([d[3] for d in landed], [d[1] for d in landed], 256, "sum_w_out")
    col_major, row_major = (lambda a: jnp.transpose(a, (2, 0, 1))), (lambda a: jnp.transpose(a, (1, 2, 0)))
    p_in = col_major(p_in)
    s_in, s_out = _sibling_swap([p_in, p_out], "swap_partials")
    res["w_in"] = [row_major(a) for a in _adamw(col_major(w_in), [p_in, s_in], col_major(m_w_in), col_major(v_w_in), None,
                                                "adamw_w_in", lead=SHARD_IN // 18)]
    res["w_out"] = _adamw(w_out, [p_out, s_out], m_w_out, v_w_out, 512, "adamw_w_out")
    shapes = [w.shape for w in ws]
    for kind in range(4):
        for nm, a in zip(names, _unpack(small_out[kind][0], shapes)):
            res.setdefault(nm, [None] * 4)[kind] = a
    order = ["ada_w", "ada_b", "pre_norm_w", "post_norm_w", "w_in", "conv_w", "conv_b", "dt_bias", "a_log", "d_skip", "ssm_norm_w", "sinks", "w_out"]
    return (loss, dx[None], *[res[n][0] for n in order], *[res[n][1] for n in order], *[res[n][2] for n in order], *[res[n][3] for n in order])
```

```python
import math

import jax
import jax.numpy as jnp
from jax import lax
from jax.experimental import pallas as pl
from jax.experimental.pallas import tpu as pltpu

F32 = jnp.float32
MXU = jnp.bfloat16
HI = lax.Precision.HIGHEST
MESH = pl.DeviceIdType.MESH

SEQ = 4096
D = 1024
DEPTH = 4
HD = 64
QK_SCALE = HD ** -0.5
LANES = 128
BLK = 128
DILS = (1, 4, 16)
NEG = -1e30
EPS = 1e-6
MIB = 1024 * 1024

NP = 6144
QA, KA, VA, ZA = 0, 512, 1024, 1536
ZB, XBC = 2048, 3072
QC, ZC, KC, VC = 4608, 5120, 5632, 5760
DTC = 5888
IN_COLS = 5904
SHARD_IN = IN_COLS // 4
CONV_CH = 1536
TM = 512

ADAM_LR, ADAM_B1, ADAM_B2, ADAM_EPS, ADAM_WD, ADAM_STEP = 0.001, 0.9, 0.999, 1e-08, 0.01, 10

NT = (((1,), (1,)), ((), ()))
TN = (((0,), (0,)), ((), ()))


def _cp(vmem_mib=48):
    return pltpu.CompilerParams(vmem_limit_bytes=vmem_mib * MIB)


def _sds(shape, dtype=F32):
    return jax.ShapeDtypeStruct(shape, dtype)


def _full(shape):
    n = len(shape)
    return pl.BlockSpec(shape, lambda *_: (0,) * n)


def _mm(a, b, dims=None):
    if dims is None:
        return jnp.dot(a.astype(MXU), b.astype(MXU), preferred_element_type=F32)
    return lax.dot_general(a.astype(MXU), b.astype(MXU), dims, preferred_element_type=F32)


def _sigmoid(x):
    return 1.0 / (1.0 + jnp.exp(-x))


def _silu(x):
    return x * _sigmoid(x)


def _dsilu(x):
    s = _sigmoid(x)
    return s * (1.0 + x * (1.0 - s))


def _softplus(x):
    ax = jnp.where(x >= 0, x, -x)
    return jnp.maximum(x, 0.0) + jnp.log1p(jnp.exp(-ax))


def _half_masks():
    lane = lax.broadcasted_iota(jnp.int32, (1, LANES), 1)
    m0 = (lane < HD).astype(F32)
    return m0, 1.0 - m0


def _allgather8(v, name):
    r, cc = v.shape

    def body(v_ref, out_ref, send_sems, recv_sems):
        x, y, c = lax.axis_index("x"), lax.axis_index("y"), lax.axis_index("c")
        me = 4 * x + 2 * y + c
        out_ref[me] = v_ref[...]
        peers = []
        for k in range(1, 8):
            px = 1 - x if k & 4 else x
            py = 1 - y if k & 2 else y
            pc = 1 - c if k & 1 else c
            peers.append((px, py, pc))
        sends = []
        for k, peer in enumerate(peers):
            cp = pltpu.make_async_remote_copy(src_ref=v_ref, dst_ref=out_ref.at[me], send_sem=send_sems.at[k],
                                              recv_sem=recv_sems.at[k], device_id=peer, device_id_type=MESH)
            cp.start()
            sends.append(cp)
        for k, (px, py, pc) in enumerate(peers):
            pltpu.make_async_remote_copy(src_ref=v_ref, dst_ref=out_ref.at[4 * px + 2 * py + pc], send_sem=send_sems.at[k],
                                         recv_sem=recv_sems.at[k], device_id=(px, py, pc), device_id_type=MESH).wait_recv()
        for cp in sends:
            cp.wait_send()

    return pl.pallas_call(
        body, name=name, out_shape=_sds((8, r, cc)),
        in_specs=[pl.BlockSpec(memory_space=pltpu.VMEM)], out_specs=pl.BlockSpec(memory_space=pltpu.VMEM),
        scratch_shapes=[pltpu.SemaphoreType.DMA((7,)), pltpu.SemaphoreType.DMA((7,))],
        compiler_params=_cp(32),
    )(v)


_HBM = pl.BlockSpec(memory_space=pltpu.HBM)
_SEM = pl.BlockSpec(memory_space=pltpu.SEMAPHORE)
_EFFECT = pltpu.SideEffectType.DATAFLOW_SIDE_EFFECTING


def _chip_copies(src_refs, land_refs, send_sems, recv_sems, part="whole"):
    x, y, c = lax.axis_index("x"), lax.axis_index("y"), lax.axis_index("c")
    mine = 2 * x + y
    out = []
    for i, land in enumerate(land_refs):
        half = land.shape[1] // 2
        own, others = pl.ds(pl.multiple_of(c * half, half), half), pl.ds(pl.multiple_of((1 - c) * half, half), half)
        for j, (px, py) in enumerate([(1 - x, y), (x, 1 - y), (1 - x, 1 - y)]):
            slot, peer = 2 * px + py, (px, py, c)
            if part == "whole":
                src = src_refs[i].at[slot] if src_refs else land.at[mine]
                there, here = land.at[mine], land.at[slot]
            elif part == "half":
                src = there = land.at[mine].at[own]
                here = land.at[slot].at[own]
            else:
                src = there = land.at[slot].at[own]
                here, peer = land.at[slot].at[others], (x, y, 1 - c)
            mk = lambda dst, i=i, j=j, src=src, peer=peer: pltpu.make_async_remote_copy(
                src_ref=src, dst_ref=dst, send_sem=send_sems.at[3 * i + j], recv_sem=recv_sems.at[3 * i + j],
                device_id=peer, device_id_type=MESH)
            out.append((mk(there), mk(here)))
    return out


def _split_start(srcs, lands, name, part="whole"):
    ops = list(srcs or []) + list(lands)
    ns, n = len(srcs or []), len(lands)

    def body(*refs):
        src_refs, land_refs = refs[:ns], refs[ns:ns + n]
        send_sems, recv_sems = refs[ns + n], refs[ns + n + 1]
        for mine_out, _ in _chip_copies(src_refs, land_refs, send_sems, recv_sems, part):
            mine_out.start()
        refs[-1][...] = jnp.zeros_like(refs[-1])

    sems = pltpu.SemaphoreType.DMA((3 * n,))
    res = pl.pallas_call(
        body, name=name, out_shape=(sems, sems) + tuple(pltpu.HBM(a.shape, a.dtype) for a in ops) + (_sds((8, LANES)),),
        in_specs=[_HBM] * len(ops), out_specs=(_SEM, _SEM) + (_HBM,) * len(ops) + (pl.BlockSpec(memory_space=pltpu.VMEM),),
        input_output_aliases={k: 2 + k for k in range(len(ops))},
        compiler_params=pltpu.CompilerParams(has_side_effects=_EFFECT),
    )(*[pltpu.with_memory_space_constraint(a, pltpu.HBM) for a in ops])
    return res[0], res[1], list(res[2:2 + len(ops)]), res[-1]


def _split_wait(send_sems, recv_sems, thru, n, after, name, part="whole"):
    ns = len(thru) - n

    def body(*refs):
        src_refs, land_refs = refs[:ns], refs[ns:ns + n]
        for mine_out, arriving in _chip_copies(src_refs, land_refs, refs[ns + n], refs[ns + n + 1], part):
            mine_out.wait_send()
            arriving.wait_recv()

    res = pl.pallas_call(
        body, name=name, out_shape=tuple(pltpu.HBM(a.shape, a.dtype) for a in thru),
        in_specs=[_HBM] * len(thru) + [_SEM, _SEM, pl.BlockSpec(memory_space=pl.ANY)], out_specs=(_HBM,) * len(thru),
        input_output_aliases={k: k for k in range(len(thru))},
        compiler_params=pltpu.CompilerParams(has_side_effects=_EFFECT),
    )(*thru, send_sems, recv_sems, after)
    return list(res)


def _sibling_swap(arrs, name):
    n = len(arrs)

    def body(*refs):
        ins, outs_, (send_sems, recv_sems) = refs[:n], refs[n:2 * n], refs[2 * n:]
        sib = (lax.axis_index("x"), lax.axis_index("y"), 1 - lax.axis_index("c"))
        cps = [pltpu.make_async_remote_copy(src_ref=ins[i], dst_ref=outs_[i], send_sem=send_sems.at[i], recv_sem=recv_sems.at[i],
                                            device_id=sib, device_id_type=MESH) for i in range(n)]
        for cp in cps:
            cp.start()
        for cp in cps:
            cp.wait_recv()
        for cp in cps:
            cp.wait_send()

    hbm = pl.BlockSpec(memory_space=pltpu.HBM)
    return pl.pallas_call(
        body, name=name, out_shape=tuple(_sds(a.shape, a.dtype) for a in arrs), in_specs=[hbm] * n, out_specs=tuple([hbm] * n),
        scratch_shapes=[pltpu.SemaphoreType.DMA((n,)), pltpu.SemaphoreType.DMA((n,))],
    )(*arrs)


def _tile_spec(rows, cc):
    return pl.BlockSpec((None, rows, cc), lambda l, i: (l, i, 0))


def _cast_bf16(a, rows, name):
    nl, r, cc = a.shape

    def body(a_ref, o_ref):
        o_ref[...] = a_ref[...].astype(jnp.bfloat16)

    return pl.pallas_call(body, name=name, out_shape=_sds((nl, r, cc), jnp.bfloat16), grid=(nl, r // rows),
                          in_specs=[_tile_spec(rows, cc)], out_specs=_tile_spec(rows, cc), compiler_params=_cp())(a)


def _sum_blocks(a, rows, name):
    k, nl, r, cc = a.shape

    def body(a_ref, o_ref):
        acc = a_ref[0].astype(F32)
        for j in range(1, k):
            acc = acc + a_ref[j].astype(F32)
        o_ref[...] = acc

    return pl.pallas_call(body, name=name, out_shape=_sds((nl, r, cc)), grid=(nl, r // rows),
                          in_specs=[pl.BlockSpec((k, None, rows, cc), lambda l, i: (0, l, i, 0))],
                          out_specs=_tile_spec(rows, cc), compiler_params=_cp())(a)


def _sum_chips(lands, srcs, rows, name):
    nl = len(lands)
    _, r, cc = lands[0].shape

    def body(*refs):
        land_refs, src_refs, o_ref = refs[:nl], refs[nl:2 * nl], refs[2 * nl]
        mine = 2 * lax.axis_index("x") + lax.axis_index("y")
        for j in range(nl):
            @pl.when(pl.program_id(0) == j)
            def _(j=j):
                own = src_refs[j][mine].astype(F32)
                acc = None
                for k in range(4):
                    term = jnp.where(mine == k, own, land_refs[j][k].astype(F32))
                    acc = term if acc is None else acc + term
                o_ref[...] = acc

    specs = [pl.BlockSpec((4, rows, cc), lambda l, i, j=j: (0, jnp.where(l == j, i, 0), 0)) for j in range(nl)]
    return pl.pallas_call(body, name=name, out_shape=_sds((nl, r, cc)), grid=(nl, r // rows),
                          in_specs=specs + specs, out_specs=_tile_spec(rows, cc), compiler_params=_cp())(*lands, *srcs)


def _adamw(w, parts, m, v, rows, name, lead=None):
    nl, r, cc = w.shape
    np_ = len(parts)
    c1 = 1.0 / (1.0 - ADAM_B1 ** ADAM_STEP)
    c2 = 1.0 / (1.0 - ADAM_B2 ** ADAM_STEP)

    def body(*refs):
        w_ref, p_refs, (m_ref, v_ref, g_ref, d_ref, nm_ref, nv_ref) = refs[0], refs[1:1 + np_], refs[1 + np_:]
        g = p_refs[0][...]
        for p_ref in p_refs[1:]:
            g = g + p_ref[...]
        nm = ADAM_B1 * m_ref[...] + (1.0 - ADAM_B1) * g
        nv = ADAM_B2 * v_ref[...] + (1.0 - ADAM_B2) * (g * g)
        g_ref[...] = g
        nm_ref[...] = nm
        nv_ref[...] = nv
        d_ref[...] = -ADAM_LR * ((nm * c1) / (jnp.sqrt(nv * c2) + ADAM_EPS) + ADAM_WD * w_ref[...])

    if lead is None:
        spec, grid = _tile_spec(rows, cc), (nl, r // rows)
    else:
        spec, grid = pl.BlockSpec((lead, r, cc), lambda i: (i, 0, 0)), (nl // lead,)
    return pl.pallas_call(body, name=name, out_shape=(_sds((nl, r, cc)),) * 4, grid=grid,
                          in_specs=[spec] * (3 + np_), out_specs=(spec,) * 4, compiler_params=_cp())(w, *parts, m, v)


_BIAS = pltpu.VMEM((2, 2 * BLK, 2 * BLK), F32)


def _fill_band_bias(bias_ref):
    qi = lax.broadcasted_iota(jnp.int32, (2 * BLK, 2 * BLK), 0) & (BLK - 1)
    kj = lax.broadcasted_iota(jnp.int32, (2 * BLK, 2 * BLK), 1)
    dist = BLK + qi - kj
    band = (dist >= 0) & (dist <= BLK)
    bias_ref[0] = jnp.where(band, 0.0, NEG)
    bias_ref[1] = jnp.where(band & (kj >= BLK), 0.0, NEG)


class _HeadStack:
    def __init__(self, group):
        self.m0, self.m1 = _half_masks()
        self.group = group
        if group is not None:
            self.kv_mask = (self.m0, self.m1)[group]

    def _swap_half(self, t, a):
        return t if a == self.group else pltpu.roll(t, HD, axis=1)

    def stack(self, t):
        t0, t1 = t * self.m0, t * self.m1
        if self.group is not None:
            t0, t1 = self._swap_half(t0, 0), self._swap_half(t1, 1)
        return jnp.concatenate([t0, t1], axis=0)

    def unstack(self, ts):
        if self.group is None:
            return ts[:BLK] * self.m0 + ts[BLK:] * self.m1
        return self._swap_half(ts[:BLK] * self.kv_mask, 0) + self._swap_half(ts[BLK:] * self.kv_mask, 1)


def _rows(st, dil):
    if dil == 1:
        return pl.ds(pl.multiple_of(st, BLK), BLK)
    return pl.ds(st, BLK, stride=dil)


def _block_pos(n, dil):
    nb = SEQ // (dil * BLK)
    r, b = n // nb, n % nb
    hp = (b > 0).astype(jnp.int32)
    st = r + dil * BLK * b
    return st, st - dil * BLK * hp, 1 - hp


def _attn_fwd(proj, qblk, kblk, vblk, dils, gqa, sink_x, name):
    has_sink = sink_x is not None

    def body(*refs):
        if has_sink:
            q_ref, k_ref, v_ref, s_ref, o_ref, lse_ref, m_scr, z_scr, bias_scr = refs
        else:
            q_ref, k_ref, v_ref, o_ref, lse_ref, m_scr, z_scr, bias_scr = refs

        @pl.when(pl.program_id(0) == 0)
        def _():
            _fill_band_bias(bias_scr)
        o_ref[...] = jnp.zeros_like(o_ref)
        if has_sink:
            z_scr[...] = jnp.ones_like(z_scr)
            m_scr[...] = jnp.broadcast_to(s_ref[...], m_scr.shape)
        else:
            z_scr[...] = jnp.zeros_like(z_scr)
            m_scr[...] = jnp.full_like(m_scr, NEG)

        def step(n, carry, dil, heads):
            m0, m1 = heads.m0, heads.m1
            st, stp, first = _block_pos(n, dil)
            rq, rp = _rows(st, dil), _rows(stp, dil)
            kk = jnp.concatenate([k_ref[rp, :], k_ref[rq, :]], axis=0)
            vv = jnp.concatenate([v_ref[rp, :], v_ref[rq, :]], axis=0)
            s = _mm(heads.stack(q_ref[rq, :] * QK_SCALE), kk, NT) + bias_scr[first]
            m = jnp.max(s, axis=1, keepdims=True)
            p = jnp.exp(s - m)
            l = jnp.sum(p, axis=1, keepdims=True)
            o_pair = heads.unstack(_mm(p, vv))
            m_pair = m[:BLK] * m0 + m[BLK:] * m1
            l_pair = l[:BLK] * m0 + l[BLK:] * m1
            m_old = m_scr[rq, :]
            m_new = jnp.maximum(m_old, m_pair)
            alpha, beta = jnp.exp(m_old - m_new), jnp.exp(m_pair - m_new)
            o_ref[rq, :] = o_ref[rq, :] * alpha + o_pair * beta
            z_scr[rq, :] = z_scr[rq, :] * alpha + l_pair * beta
            m_scr[rq, :] = m_new
            return carry

        def blocks(heads):
            for dil in dils:
                lax.fori_loop(0, SEQ // BLK, lambda n, carry, dil=dil: step(n, carry, dil, heads), 0, unroll=8)

        if gqa:
            for grp in range(2):
                pl.when(pl.program_id(0) // 2 == grp)(lambda grp=grp: blocks(_HeadStack(grp)))
        else:
            blocks(_HeadStack(None))

        def fin(t, carry):
            rt = pl.ds(pl.multiple_of(t * TM, TM), TM)
            z = z_scr[rt, :]
            o_ref[rt, :] = o_ref[rt, :] / z
            lse_ref[rt, :] = m_scr[rt, :] + jnp.log(z)
            return carry
        lax.fori_loop(0, SEQ // TM, fin, 0)

    col = lambda blk: pl.BlockSpec((SEQ, LANES), lambda p, blk=blk: (0, blk + p))
    kv = (lambda blk: pl.BlockSpec((SEQ, LANES), lambda p, blk=blk: (0, blk))) if gqa else col
    in_specs = [col(qblk), kv(kblk), kv(vblk)]
    args = [proj, proj, proj]
    if has_sink:
        in_specs.append(pl.BlockSpec((1, LANES), lambda p: (0, p)))
        args.append(sink_x)
    out = pl.BlockSpec((SEQ, LANES), lambda p: (0, p))
    return pl.pallas_call(body, name=name, out_shape=(_sds((SEQ, 512)), _sds((SEQ, 512))), grid=(4,),
                          in_specs=in_specs, out_specs=(out, out),
                          scratch_shapes=[pltpu.VMEM((SEQ, LANES), F32), pltpu.VMEM((SEQ, LANES), F32), _BIAS],
                          compiler_params=_cp(48))(*args)


def _attn_bwd(proj, qblk, kblk, vblk, do, o, lse, dils, gqa, sink_x, name):
    has_sink = sink_x is not None

    def body(*refs):
        if has_sink:
            q_ref, k_ref, v_ref, do_ref, o_ref, lse_ref, s_ref, dq_ref, dk_ref, dv_ref, ds_ref, bias_scr = refs
        else:
            q_ref, k_ref, v_ref, do_ref, o_ref, lse_ref, dq_ref, dk_ref, dv_ref, bias_scr = refs
        pid = pl.program_id(0)

        @pl.when(pid == 0)
        def _():
            _fill_band_bias(bias_scr)
        dq_ref[...] = jnp.zeros_like(dq_ref)
        if gqa:
            @pl.when(pid == 0)
            def _():
                dk_ref[...] = jnp.zeros_like(dk_ref)
                dv_ref[...] = jnp.zeros_like(dv_ref)
        else:
            dk_ref[...] = jnp.zeros_like(dk_ref)
            dv_ref[...] = jnp.zeros_like(dv_ref)

        def step(n, carry, dil, heads):
            m0, m1 = heads.m0, heads.m1
            st, stp, first = _block_pos(n, dil)
            rq, rp = _rows(st, dil), _rows(stp, dil)
            do_, lse_ = do_ref[rq, :], lse_ref[rq, :]
            kk = jnp.concatenate([k_ref[rp, :], k_ref[rq, :]], axis=0)
            vv = jnp.concatenate([v_ref[rp, :], v_ref[rq, :]], axis=0)
            qs, dos = heads.stack(q_ref[rq, :] * QK_SCALE), heads.stack(do_)
            doo = do_ * o_ref[rq, :]
            delta = jnp.concatenate([jnp.sum(doo * m0, axis=1, keepdims=True), jnp.sum(doo * m1, axis=1, keepdims=True)], axis=0)
            lse_s = jnp.concatenate([lse_[:, 0:1], lse_[:, HD:HD + 1]], axis=0)
            p = jnp.exp(_mm(qs, kk, NT) + bias_scr[first] - lse_s)
            ds = p * (_mm(dos, vv, NT) - delta)
            dq_ref[rq, :] += heads.unstack(_mm(ds, kk)) * QK_SCALE
            dk_sum, dv_sum = _mm(ds, qs, TN), _mm(p, dos, TN)
            dk_ref[rp, :] += dk_sum[:BLK]
            dk_ref[rq, :] += dk_sum[BLK:]
            dv_ref[rp, :] += dv_sum[:BLK]
            dv_ref[rq, :] += dv_sum[BLK:]
            return carry

        def blocks(heads):
            for dil in dils:
                lax.fori_loop(0, SEQ // BLK, lambda n, carry, dil=dil: step(n, carry, dil, heads), 0, unroll=4)

        if gqa:
            for grp in range(2):
                pl.when(pid // 2 == grp)(lambda grp=grp: blocks(_HeadStack(grp)))
        else:
            blocks(_HeadStack(None))

        if has_sink:
            m0, m1 = _half_masks()

            def sink_rows(t, acc):
                rt = pl.ds(pl.multiple_of(t * TM, TM), TM)
                return acc - jnp.sum(jnp.exp(s_ref[...] - lse_ref[rt, :]) * (do_ref[rt, :] * o_ref[rt, :]), axis=0, keepdims=True)
            acc = lax.fori_loop(0, SEQ // TM, sink_rows, jnp.zeros((1, LANES), F32))
            per_head = jnp.sum(acc * m0, axis=1, keepdims=True) * m0 + jnp.sum(acc * m1, axis=1, keepdims=True) * m1
            ds_ref[0] = jnp.broadcast_to(per_head, (8, LANES))

    col = lambda blk: pl.BlockSpec((SEQ, LANES), lambda p, blk=blk: (0, blk + p))
    kv = (lambda blk: pl.BlockSpec((SEQ, LANES), lambda p, blk=blk: (0, blk))) if gqa else col
    pair = pl.BlockSpec((SEQ, LANES), lambda p: (0, p))
    in_specs = [col(qblk), kv(kblk), kv(vblk), pair, pair, pair]
    args = [proj, proj, proj, do, o, lse]
    kvw = LANES if gqa else 512
    kv_out = pl.BlockSpec((SEQ, LANES), lambda p: (0, 0)) if gqa else pair
    out_shape = [_sds((SEQ, 512)), _sds((SEQ, kvw)), _sds((SEQ, kvw))]
    out_specs = [pair, kv_out, kv_out]
    if has_sink:
        in_specs.append(pl.BlockSpec((1, LANES), lambda p: (0, p)))
        args.append(sink_x)
        out_shape.append(_sds((4, 8, LANES)))
        out_specs.append(pl.BlockSpec((1, 8, LANES), lambda p: (p, 0, 0)))
    return pl.pallas_call(body, name=name, out_shape=tuple(out_shape), grid=(4,), in_specs=in_specs,
                          out_specs=tuple(out_specs), scratch_shapes=[_BIAS], compiler_params=_cp(56))(*args)


_CT = 128


def _rows_before(x_ref, t, k):
    if t == 0:
        return jnp.concatenate([jnp.zeros((k, LANES), F32), x_ref[0:_CT - k, :]], axis=0)
    return x_ref[t * _CT - k:(t + 1) * _CT - k, :]


def _conv_pre(x_ref, w_ref, b_ref, t):
    taps = [x_ref[t * _CT:(t + 1) * _CT, :]] + [_rows_before(x_ref, t, k) for k in range(1, 4)]
    u = b_ref[...] + taps[0] * w_ref[3:4, :]
    for k in range(1, 4):
        u = u + taps[k] * w_ref[3 - k:4 - k, :]
    return u, taps


def _conv_fwd(proj, w, b, name):
    def body(x_ref, w_ref, b_ref, o_ref):
        for t in range(SEQ // _CT):
            o_ref[t * _CT:(t + 1) * _CT, :] = _silu(_conv_pre(x_ref, w_ref, b_ref, t)[0])

    nblk = CONV_CH // LANES
    return pl.pallas_call(body, name=name, out_shape=_sds((SEQ, CONV_CH)), grid=(nblk,),
                          in_specs=[pl.BlockSpec((SEQ, LANES), lambda j: (0, XBC // LANES + j)),
                                    pl.BlockSpec((4, LANES), lambda j: (0, j)), pl.BlockSpec((1, LANES), lambda j: (0, j))],
                          out_specs=pl.BlockSpec((SEQ, LANES), lambda j: (0, j)), compiler_params=_cp())(proj, w, b)


def _conv_bwd(proj, dact, w, b, name):
    def body(x_ref, da_ref, w_ref, b_ref, dx_ref, dw_ref, db_ref, du_scr):
        du_scr[SEQ:SEQ + 8, :] = jnp.zeros((8, LANES), F32)
        db = jnp.zeros((1, LANES), F32)
        dws = [jnp.zeros((1, LANES), F32)] * 4
        for t in range(SEQ // _CT):
            u, taps = _conv_pre(x_ref, w_ref, b_ref, t)
            du = da_ref[t * _CT:(t + 1) * _CT, :] * _dsilu(u)
            du_scr[t * _CT:(t + 1) * _CT, :] = du
            db = db + jnp.sum(du, axis=0, keepdims=True)
            dws = [dws[k] + jnp.sum(du * taps[k], axis=0, keepdims=True) for k in range(4)]
        db_ref[...] = db
        for k in range(4):
            dw_ref[3 - k:4 - k, :] = dws[k]
        for t in range(SEQ // _CT):
            dx = du_scr[t * _CT:(t + 1) * _CT, :] * w_ref[3:4, :]
            for k in range(1, 4):
                dx = dx + du_scr[t * _CT + k:(t + 1) * _CT + k, :] * w_ref[3 - k:4 - k, :]
            dx_ref[t * _CT:(t + 1) * _CT, :] = dx.astype(dx_ref.dtype)

    nblk = CONV_CH // LANES
    blk = pl.BlockSpec((SEQ, LANES), lambda j: (0, j))
    wspec, bspec = pl.BlockSpec((4, LANES), lambda j: (0, j)), pl.BlockSpec((1, LANES), lambda j: (0, j))
    return pl.pallas_call(body, name=name, out_shape=(_sds((SEQ, CONV_CH), MXU), _sds((4, CONV_CH)), _sds((1, CONV_CH))), grid=(nblk,),
                          in_specs=[pl.BlockSpec((SEQ, LANES), lambda j: (0, XBC // LANES + j)), blk, wspec, bspec],
                          out_specs=(blk, wspec, bspec), scratch_shapes=[pltpu.VMEM((SEQ + 8, LANES), F32)],
                          compiler_params=_cp())(proj, dact, w, b)


def _ssd_chunk(xs, bm, cm, dtr, z, hs, al16, dtb, dskx, nw):
    m0, m1 = _half_masks()
    row = lax.broadcasted_iota(jnp.int32, (BLK, BLK), 0)
    col = lax.broadcasted_iota(jnp.int32, (BLK, BLK), 1)
    causal = row >= col
    tril = causal.astype(F32)
    lane = lax.broadcasted_iota(jnp.int32, (1, LANES), 1)
    sub = lax.broadcasted_iota(jnp.int32, (BLK, 1), 0)
    last_row = (sub == BLK - 1).astype(F32)
    dt = jnp.where(lane < 16, _softplus(dtr + dtb), 0.0)
    a16 = -jnp.exp(al16)
    acum = jnp.dot(tril, dt * a16, precision=HI, preferred_element_type=F32)
    acum_t = acum.T
    gmat = [_mm(cm[g], bm[g], NT) for g in range(2)]
    ys, hn = [], []
    for p in range(8):
        g = p // 4
        pick = [(lane == 2 * p + a).astype(F32) for a in range(2)]
        col_h = [jnp.sum(acum * pick[a], axis=1, keepdims=True) for a in range(2)]
        dt_x = sum(jnp.sum(dt * pick[a], axis=1, keepdims=True) * msk for a, msk in enumerate((m0, m1)))
        ac_x = col_h[0] * m0 + col_h[1] * m1
        a_end = jnp.sum(ac_x * last_row, axis=0, keepdims=True)
        xdt = xs[p] * dt_x
        y = _mm(cm[g], hs[p]) * jnp.exp(ac_x)
        for a, msk in enumerate((m0, m1)):
            row_h = jnp.sum(acum_t * (sub == 2 * p + a).astype(F32), axis=0, keepdims=True)
            decay = jnp.exp(jnp.where(causal, col_h[a] - row_h, NEG))
            y = y + _mm(gmat[g] * decay, xdt * msk)
        st = _mm(bm[g], xdt * jnp.exp(a_end - ac_x), TN)
        hn.append(hs[p] * jnp.exp(a_end) + st)
        y = y + dskx[p] * xs[p]
        ys.append(y * _silu(z[p]))
    out = []
    for g in range(2):
        ms = sum(jnp.sum(ys[p] * ys[p], axis=1, keepdims=True) for p in range(4 * g, 4 * g + 4)) * (1.0 / 512)
        rstd = lax.rsqrt(ms + EPS)
        out += [ys[p] * rstd * nw[p] for p in range(4 * g, 4 * g + 4)]
    return out, hn


def _tiles(ref, n, off=0):
    return [ref[:, off + LANES * p:off + LANES * (p + 1)] for p in range(n)]


def _ssd_load(xbc_ref, z_ref, dt_ref, al16_ref, dtb_ref, dsk_ref, nw_ref):
    return (_tiles(xbc_ref, 8), _tiles(xbc_ref, 2, 1024), _tiles(xbc_ref, 2, 1280), dt_ref[...], _tiles(z_ref, 8)), \
           (al16_ref[...], dtb_ref[...], _tiles(dsk_ref, 8), _tiles(nw_ref, 8))


_NCH = SEQ // BLK


def _ssd_param_specs():
    return [_full((1, LANES)), _full((1, LANES)), _full((1, 1024)), _full((1, 1024))]


def _ssd_fwd(xbc_act, proj, al16, dtb, dskx, nw, name):
    def body(xbc_ref, z_ref, dt_ref, al16_ref, dtb_ref, dsk_ref, nw_ref, y_ref, hin_ref, h_scr):
        @pl.when(pl.program_id(0) == 0)
        def _():
            h_scr[...] = jnp.zeros_like(h_scr)
        acts, params = _ssd_load(xbc_ref, z_ref, dt_ref, al16_ref, dtb_ref, dsk_ref, nw_ref)
        hs = _tiles(h_scr, 8)
        hin_ref[0] = h_scr[...]
        ys, hn = _ssd_chunk(*acts, hs, *params)
        for p in range(8):
            y_ref[:, LANES * p:LANES * (p + 1)] = ys[p].astype(y_ref.dtype)
            h_scr[:, LANES * p:LANES * (p + 1)] = hn[p]

    return pl.pallas_call(
        body, name=name, out_shape=(_sds((SEQ, 1024), MXU), _sds((_NCH, BLK, 1024))), grid=(_NCH,),
        in_specs=[pl.BlockSpec((BLK, CONV_CH), lambda c: (c, 0)), pl.BlockSpec((BLK, 1024), lambda c: (c, ZB // 1024)),
                  pl.BlockSpec((BLK, LANES), lambda c: (c, DTC // LANES))] + _ssd_param_specs(),
        out_specs=(pl.BlockSpec((BLK, 1024), lambda c: (c, 0)), pl.BlockSpec((1, BLK, 1024), lambda c: (c, 0, 0))),
        scratch_shapes=[pltpu.VMEM((BLK, 1024), F32)], compiler_params=_cp())(xbc_act, proj, proj, al16, dtb, dskx, nw)


def _ssd_bwd(xbc_act, proj, hin, dyb, al16, dtb, dskx, nw, name):
    def body(xbc_ref, z_ref, dt_ref, hin_ref, dy_ref, al16_ref, dtb_ref, dsk_ref, nw_ref,
             dxbc_ref, dz_ref, ddt_ref, dal16_ref, ddtb_ref, ddsk_ref, dnw_ref, dh_scr):
        @pl.when(pl.program_id(0) == 0)
        def _():
            dh_scr[...] = jnp.zeros_like(dh_scr)
            for r in (dal16_ref, ddtb_ref, ddsk_ref, dnw_ref):
                r[...] = jnp.zeros_like(r)
        acts, params = _ssd_load(xbc_ref, z_ref, dt_ref, al16_ref, dtb_ref, dsk_ref, nw_ref)
        hs = [hin_ref[0, :, LANES * p:LANES * (p + 1)] for p in range(8)]
        _, vjp = jax.vjp(lambda a, h, q: _ssd_chunk(*a, h, *q), acts, hs, params)
        (dxs, dbm, dcm, ddt, dz), dhs, (dal16, ddtb, ddsk, dnw) = vjp((_tiles(dy_ref, 8), _tiles(dh_scr, 8)))
        for p in range(8):
            cols = slice(LANES * p, LANES * (p + 1))
            dxbc_ref[:, cols] = dxs[p]
            dz_ref[:, cols] = dz[p].astype(dz_ref.dtype)
            dh_scr[:, cols] = dhs[p]
            ddsk_ref[:, cols] += ddsk[p]
            dnw_ref[:, cols] += dnw[p]
        for g in range(2):
            dxbc_ref[:, 1024 + LANES * g:1024 + LANES * (g + 1)] = dbm[g]
            dxbc_ref[:, 1280 + LANES * g:1280 + LANES * (g + 1)] = dcm[g]
        ddt_ref[...] = ddt.astype(ddt_ref.dtype)
        dal16_ref[...] += dal16
        ddtb_ref[...] += ddtb

    rev = lambda c: _NCH - 1 - c
    return pl.pallas_call(
        body, name=name,
        out_shape=(_sds((SEQ, CONV_CH)), _sds((SEQ, 1024), MXU), _sds((SEQ, LANES), MXU),
                   _sds((1, LANES)), _sds((1, LANES)), _sds((1, 1024)), _sds((1, 1024))),
        grid=(_NCH,),
        in_specs=[pl.BlockSpec((BLK, CONV_CH), lambda c: (rev(c), 0)), pl.BlockSpec((BLK, 1024), lambda c: (rev(c), ZB // 1024)),
                  pl.BlockSpec((BLK, LANES), lambda c: (rev(c), DTC // LANES)), pl.BlockSpec((1, BLK, 1024), lambda c: (rev(c), 0, 0)),
                  pl.BlockSpec((BLK, 1024), lambda c: (rev(c), 0))] + _ssd_param_specs(),
        out_specs=(pl.BlockSpec((BLK, CONV_CH), lambda c: (rev(c), 0)), pl.BlockSpec((BLK, 1024), lambda c: (rev(c), 0)),
                   pl.BlockSpec((BLK, LANES), lambda c: (rev(c), 0)),
                   _full((1, LANES)), _full((1, LANES)), _full((1, 1024)), _full((1, 1024))),
        scratch_shapes=[pltpu.VMEM((BLK, 1024), F32)], compiler_params=_cp())(xbc_act, proj, proj, hin, dyb, al16, dtb, dskx, nw)


def _rstd(v):
    return lax.rsqrt(jnp.mean(v * v, axis=1, keepdims=True) + EPS)


def _rms_bwd(dn, n, rstd):
    return rstd * (dn - n * jnp.mean(dn * n, axis=1, keepdims=True))


_VEC = _full((1, D))


def _layer_spec(layer):
    return pl.BlockSpec((None, 2048, D), lambda *_: (layer, 0, 0))

_ROW = pl.BlockSpec((TM, D), lambda i, *_: (i, 0))


def _proj_fwd(x, pre_w, scale, shift, w, layer, name):
    tn, ni = 1024, SEQ // TM

    def body(x_ref, pw_ref, sc_ref, sh_ref, w_ref, o_ref, h_ref, h_scr):
        rows = pl.ds(pl.multiple_of(pl.program_id(1) * TM, TM), TM)

        @pl.when(pl.program_id(0) == 0)
        def _():
            xv = x_ref[...]
            h = ((xv * _rstd(xv) * pw_ref[...]) * (1.0 + sc_ref[...]) + sh_ref[...]).astype(h_ref.dtype)
            h_scr[rows, :] = h
            h_ref[...] = h
        o_ref[...] = jnp.dot(h_scr[rows, :], w_ref[...].astype(MXU), preferred_element_type=F32)

    first_pass = pl.BlockSpec((TM, D), lambda j, i: (jnp.where(j == 0, i, ni - 1), 0))
    return pl.pallas_call(body, name=name, out_shape=(_sds((SEQ, NP)), _sds((SEQ, D), MXU)), grid=(NP // tn, ni),
                          in_specs=[first_pass, _VEC, _VEC, _VEC, pl.BlockSpec((None, D, tn), lambda j, i: (layer, 0, j))],
                          out_specs=(pl.BlockSpec((TM, tn), lambda j, i: (i, j)), first_pass),
                          scratch_shapes=[pltpu.VMEM((SEQ, D), MXU)], compiler_params=_cp())(x, pre_w, scale, shift, w)


_HALF = pl.BlockSpec((TM, 512), lambda i: (i, 0))
_Z_A = pl.BlockSpec((TM, 512), lambda i: (i, ZA // 512))
_Z_C = pl.BlockSpec((TM, 512), lambda i: (i, ZC // 512))


def _out_fwd(o_a, yb, o_c, proj, w, layer, x, gate, post_w, name):
    def body(oa_ref, yb_ref, oc_ref, za_ref, zc_ref, w_ref, x_ref, g_ref, pw_ref, xn_ref, y_ref):
        y = (_mm(oa_ref[...] * _silu(za_ref[...]), w_ref[0:512, :]) + _mm(yb_ref[...], w_ref[512:1536, :])
             + _mm(oc_ref[...] * _silu(zc_ref[...]), w_ref[1536:2048, :]))
        y_ref[...] = y
        xn_ref[...] = x_ref[...] + g_ref[...] * (y * _rstd(y) * pw_ref[...])

    return pl.pallas_call(body, name=name, out_shape=(_sds((SEQ, D)), _sds((SEQ, D))), grid=(SEQ // TM,),
                          in_specs=[_HALF, _ROW, _HALF, _Z_A, _Z_C, _layer_spec(layer), _ROW, _VEC, _VEC],
                          out_specs=(_ROW, _ROW), compiler_params=_cp())(o_a, yb, o_c, proj, proj, w, x, gate, post_w)


def _dymix(dxo, y, gate, post_w, w, layer, o_a, o_c, proj, name):
    def body(dx_ref, y_ref, g_ref, pw_ref, w_ref, oa_ref, oc_ref, za_ref, zc_ref,
             dy_ref, dg_ref, dpw_ref, doa_ref, dza_ref, b_ref, doc_ref, dzc_ref):
        @pl.when(pl.program_id(0) == 0)
        def _():
            dg_ref[...] = jnp.zeros_like(dg_ref)
            dpw_ref[...] = jnp.zeros_like(dpw_ref)
        dx, yv = dx_ref[...], y_ref[...]
        rstd = _rstd(yv)
        n = yv * rstd
        dg_ref[...] += jnp.sum(dx * (n * pw_ref[...]), axis=0, keepdims=True)
        dr = dx * g_ref[...]
        dpw_ref[...] += jnp.sum(dr * n, axis=0, keepdims=True)
        dy = _rms_bwd(dr * pw_ref[...], n, rstd)
        dy_ref[...] = dy
        b_ref[...] = _mm(dy, w_ref[512:1536, :], NT)
        for rows, o_ref, z_ref, do_ref, dz_ref in ((slice(0, 512), oa_ref, za_ref, doa_ref, dza_ref),
                                                   (slice(1536, 2048), oc_ref, zc_ref, doc_ref, dzc_ref)):
            dyg, z = _mm(dy, w_ref[rows, :], NT), z_ref[...]
            do_ref[...] = dyg * _silu(z)
            dz_ref[...] = (dyg * o_ref[...] * _dsilu(z)).astype(dz_ref.dtype)

    return pl.pallas_call(body, name=name,
                          out_shape=(_sds((SEQ, D)), _sds((1, D)), _sds((1, D)),
                                     _sds((SEQ, 512)), _sds((SEQ, 512), MXU), _sds((SEQ, D)), _sds((SEQ, 512)), _sds((SEQ, 512), MXU)),
                          grid=(SEQ // TM,), in_specs=[_ROW, _ROW, _VEC, _VEC, _layer_spec(layer), _HALF, _HALF, _Z_A, _Z_C],
                          out_specs=(_ROW, _VEC, _VEC, _HALF, _HALF, _ROW, _HALF, _HALF),
                          compiler_params=_cp())(dxo, y, gate, post_w, w, o_a, o_c, proj, proj)


def _dwout(o_a, yb, o_c, proj, dy, name):
    def body(oa_ref, yb_ref, oc_ref, za_ref, zc_ref, dy_ref, o_ref):
        @pl.when(pl.program_id(0) == 0)
        def _():
            o_ref[...] = jnp.zeros_like(o_ref)
        dy = dy_ref[...]
        o_ref[0:512, :] += _mm(oa_ref[...] * _silu(za_ref[...]), dy, TN)
        o_ref[512:1536, :] += _mm(yb_ref[...], dy, TN)
        o_ref[1536:2048, :] += _mm(oc_ref[...] * _silu(zc_ref[...]), dy, TN)

    return pl.pallas_call(body, name=name, out_shape=_sds((2048, D)), grid=(SEQ // TM,),
                          in_specs=[_HALF, _ROW, _HALF, _Z_A, _Z_C, _ROW], out_specs=_full((2048, D)),
                          compiler_params=_cp())(o_a, yb, o_c, proj, proj, dy)


def _dwin(h, pieces, name):
    n = len(pieces)
    widths = [p.shape[1] for p in pieces]
    half = NP // 2

    def body(*refs):
        h_ref, p_refs, o_ref = refs[0], refs[1:1 + n], refs[1 + n]

        @pl.when(pl.program_id(0) == 0)
        def _():
            o_ref[...] = jnp.zeros_like(o_ref)
        hv, c0 = h_ref[...], 0
        for p_ref, wd in zip(p_refs, widths):
            o_ref[:, c0:c0 + wd] += _mm(hv, p_ref[...], TN)
            c0 += wd

    return pl.pallas_call(body, name=name, out_shape=_sds((D, half)), grid=(SEQ // TM,),
                          in_specs=[_ROW] + [pl.BlockSpec((TM, wd), lambda k: (k, 0)) for wd in widths],
                          out_specs=_full((D, half)), compiler_params=_cp(56))(h, *pieces)


_TMH = 256


def _dh_bwd(pieces, w, x, pre_w, scale, dxo, name):
    n = len(pieces)
    widths = [p.shape[1] for p in pieces]

    def body(*refs):
        p_refs, (w_ref, x_ref, pw_ref, sc_ref, dxo_ref, dx_ref, dsh_ref, dsc_ref, dpw_ref) = refs[:n], refs[n:]

        @pl.when(pl.program_id(0) == 0)
        def _():
            for r in (dsh_ref, dsc_ref, dpw_ref):
                r[...] = jnp.zeros_like(r)
        dh, c0 = 0.0, 0
        for p_ref, wd in zip(p_refs, widths):
            dh = dh + _mm(p_ref[...], w_ref[:, c0:c0 + wd], NT)
            c0 += wd
        xv = x_ref[...]
        rstd = _rstd(xv)
        nrm = xv * rstd
        dsh_ref[...] += jnp.sum(dh, axis=0, keepdims=True)
        dsc_ref[...] += jnp.sum(dh * (nrm * pw_ref[...]), axis=0, keepdims=True)
        dhn = dh * (1.0 + sc_ref[...])
        dpw_ref[...] += jnp.sum(dhn * nrm, axis=0, keepdims=True)
        dx_ref[...] = _rms_bwd(dhn * pw_ref[...], nrm, rstd) + dxo_ref[...]

    row = pl.BlockSpec((_TMH, D), lambda i: (i, 0))
    return pl.pallas_call(body, name=name, out_shape=(_sds((SEQ, D)), _sds((1, D)), _sds((1, D)), _sds((1, D))),
                          grid=(SEQ // _TMH,),
                          in_specs=[pl.BlockSpec((_TMH, wd), lambda i: (i, 0)) for wd in widths]
                          + [pl.BlockSpec((None, D, NP), lambda i: (0, 0, 0)), row, _VEC, _VEC, row],
                          out_specs=(row, _VEC, _VEC, _VEC), compiler_params=_cp(56))(*pieces, w, x, pre_w, scale, dxo)


def _w_in_padded(land, name):
    rows = 128

    def body(l_ref, o_ref):
        o_ref[...] = _pad_cols(jnp.concatenate([l_ref[k] for k in range(4)], axis=1))

    return pl.pallas_call(body, name=name, out_shape=_sds((D, NP), land.dtype), grid=(D // rows,),
                          in_specs=[pl.BlockSpec((4, rows, SHARD_IN), lambda i: (0, i, 0))],
                          out_specs=pl.BlockSpec((rows, NP), lambda i: (i, 0)), compiler_params=_cp())(land)


def _grad_blocks(dwa, dwb, name):
    rows = 128

    def body(a_ref, b_ref, o_ref):
        g = _unpad_cols(jnp.concatenate([a_ref[...], b_ref[...]], axis=1))
        for k in range(4):
            o_ref[k] = g[:, SHARD_IN * k:SHARD_IN * (k + 1)].astype(o_ref.dtype)

    half = pl.BlockSpec((rows, NP // 2), lambda i: (i, 0))
    return pl.pallas_call(body, name=name, out_shape=_sds((4, D, SHARD_IN), jnp.bfloat16), grid=(D // rows,),
                          in_specs=[half, half], out_specs=pl.BlockSpec((4, rows, SHARD_IN), lambda i: (0, i, 0)),
                          compiler_params=_cp())(dwa, dwb)


def _loss_bwd(xf, tgt, name):
    def body(x_ref, t_ref, dx_ref, l_ref):
        @pl.when(pl.program_id(0) == 0)
        def _():
            l_ref[...] = jnp.zeros_like(l_ref)
        e = x_ref[...] - t_ref[...]
        dx_ref[...] = e * (1.0 / D)
        l_ref[...] += 0.5 * jnp.sum(jnp.mean(e * e, axis=1, keepdims=True), axis=0, keepdims=True)

    return pl.pallas_call(body, name=name, out_shape=(_sds((SEQ, D)), _sds((8, LANES))), grid=(SEQ // TM,),
                          in_specs=[_ROW, _ROW], out_specs=(_ROW, _full((8, LANES))), compiler_params=_cp())(xf, tgt)


def _mod_part(c_all, ada_w, ada_b, name):
    def body(c_ref, w_ref, b_ref, o_ref):
        o_ref[0] = _mm(_silu(c_ref[...]), w_ref[0]) + b_ref[0]

    return pl.pallas_call(body, name=name, out_shape=_sds((DEPTH, 8, 768)), grid=(DEPTH,),
                          in_specs=[_full((8, D)), pl.BlockSpec((1, D, 768), lambda i: (i, 0, 0)), pl.BlockSpec((1, 1, 768), lambda i: (i, 0, 0))],
                          out_specs=pl.BlockSpec((1, 8, 768), lambda i: (i, 0, 0)), compiler_params=_cp())(c_all, ada_w, ada_b)


def _ada_grad(c_t, dmod, name):
    def body(c_ref, d_ref, o_ref):
        ca = _silu(c_ref[...])
        dm = d_ref[0]
        acc = ca[:, 0:1] * dm[0:1, :]
        for s in range(1, 8):
            acc = acc + ca[:, s:s + 1] * dm[s:s + 1, :]
        o_ref[0] = acc

    return pl.pallas_call(body, name=name, out_shape=_sds((DEPTH, D, 768)), grid=(DEPTH,),
                          in_specs=[_full((D, LANES)), pl.BlockSpec((1, 8, 768), lambda i: (i, 0, 0))],
                          out_specs=pl.BlockSpec((1, D, 768), lambda i: (i, 0, 0)), compiler_params=_cp())(c_t, dmod)


def _pack(parts):
    flat = []
    for p in parts:
        f = p.reshape(-1)
        flat.append(jnp.pad(f, (0, (-f.size) % LANES)))
    v = jnp.concatenate(flat)
    return jnp.pad(v, (0, (-v.size) % (8 * LANES))).reshape(-1, LANES)


def _unpack(v, shapes):
    v = v.reshape(-1)
    out, off = [], 0
    for s in shapes:
        n = math.prod(s)
        out.append(v[off:off + n].reshape(s))
        off += n + (-n) % LANES
    return out


_GIVEN_DT, _GIVEN_C = 4608, 4624


def _pad_cols(w):
    return jnp.concatenate([w[..., :_GIVEN_DT], w[..., _GIVEN_C:], w[..., _GIVEN_DT:_GIVEN_C],
                            jnp.zeros(w.shape[:-1] + (NP - IN_COLS,), w.dtype)], axis=-1)


def _unpad_cols(w):
    return jnp.concatenate([w[..., :_GIVEN_DT], w[..., DTC:DTC + 16], w[..., _GIVEN_DT:DTC]], axis=-1)


def _pad_lanes(v):
    return jnp.pad(v, (0, LANES - v.shape[0])).reshape(1, LANES)


def _local_step(x2, tgt, mod, weights_of, grads_done, pre_w, post_w, conv_w, conv_b, dt_bias, a_log, d_skip, nw, sinks):
    saved = []
    xcur = x2
    for i in range(DEPTH):
        shift, scale, gate = mod[i:i + 1, :D], mod[i:i + 1, D:2 * D], mod[i:i + 1, 2 * D:]
        pw, qw = pre_w[i:i + 1], post_w[i:i + 1]
        w_p, w_o = weights_of(i, xcur)
        proj, h = _proj_fwd(xcur, pw, scale, shift, w_p, 0, "proj_fwd")
        o_a, lse_a = _attn_fwd(proj, QA // LANES, KA // LANES, VA // LANES, DILS, False, None, "attn_a_fwd")
        sink_x = jnp.repeat(sinks[i], HD).reshape(1, 512)
        o_c, lse_c = _attn_fwd(proj, QC // LANES, KC // LANES, VC // LANES, (1,), True, sink_x, "attn_c_fwd")
        cw, cb = conv_w[i], conv_b[i:i + 1]
        xbc_act = _conv_fwd(proj, cw, cb, "conv_fwd")
        ssd_p = (_pad_lanes(a_log[i]), _pad_lanes(dt_bias[i]), jnp.repeat(d_skip[i], HD).reshape(1, 1024), nw[i:i + 1])
        yb, hin = _ssd_fwd(xbc_act, proj, *ssd_p, "ssd_fwd")
        xnew, y = _out_fwd(o_a, yb, o_c, proj, w_o, 0, xcur, gate, qw, "out_fwd")
        saved.append((w_p, w_o, xcur, scale, gate, pw, qw, proj, h, o_a, lse_a, sink_x, o_c, lse_c, cw, cb, xbc_act, ssd_p, yb, hin, y))
        xcur = xnew
    dx, ltile = _loss_bwd(xcur, tgt, "loss")
    dmod, small = [None] * DEPTH, [None] * DEPTH
    for i in reversed(range(DEPTH)):
        w_p, w_o, xin, scale, gate, pw, qw, proj, h, o_a, lse_a, sink_x, o_c, lse_c, cw, cb, xbc_act, ssd_p, yb, hin, y = saved[i]
        dy, dgate, dpost, do_a, dz_a, dyb, do_c, dz_c = _dymix(dx, y, gate, qw, w_o, 0, o_a, o_c, proj, "dymix")
        dwo = _dwout(o_a, yb, o_c, proj, dy, "dwout")
        dq_a, dk_a, dv_a = _attn_bwd(proj, QA // LANES, KA // LANES, VA // LANES, do_a, o_a, lse_a, DILS, False, None, "attn_a_bwd")
        dq_c, dk_c, dv_c, dsk = _attn_bwd(proj, QC // LANES, KC // LANES, VC // LANES, do_c, o_c, lse_c, (1,), True, sink_x, "attn_c_bwd")
        dxbc_act, dz_b, ddt, dal16, ddtb, ddsk, dnw = _ssd_bwd(xbc_act, proj, hin, dyb, *ssd_p, "ssd_bwd")
        dxbc, dcw, dcb = _conv_bwd(proj, dxbc_act, cw, cb, "conv_bwd")
        half_a, half_b = [dq_a, dk_a, dv_a, dz_a, dz_b], [dxbc, dq_c, dz_c, dk_c, dv_c, ddt]
        sent = grads_done(i, _dwin(h, half_a, "dwin_a"), _dwin(h, half_b, "dwin_b"), dwo)
        dx, dshift, dscale, dpre = _dh_bwd(half_a + half_b, w_p, xin, pw, scale + sent[0, 0], dx, "dh_bwd")
        dmod[i] = jnp.concatenate([dshift, dscale, dgate], axis=1)
        small[i] = (dpre, dpost, dcw, dcb, ddtb[0, :16], dal16[0, :16], ddsk.reshape(16, HD).sum(axis=1), dnw, dsk[:, 0, ::HD].reshape(8))
    return ltile, dx, jnp.concatenate(dmod, axis=0), small


_SMALL = ((1, D), (1, D), (4, CONV_CH), (1, CONV_CH), (16,), (16,), (16,), (1, D), (8,))


def kernel(x, c, ada_w, ada_b, pre_norm_w, post_norm_w, w_in, conv_w, conv_b, dt_bias, a_log, d_skip, ssm_norm_w, sinks, w_out, loss_target, m_ada_w, m_ada_b, m_pre_norm_w, m_post_norm_w, m_w_in, m_conv_w, m_conv_b, m_dt_bias, m_a_log, m_d_skip, m_ssm_norm_w, m_sinks, m_w_out, v_ada_w, v_ada_b, v_pre_norm_w, v_post_norm_w, v_w_in, v_conv_w, v_conv_b, v_dt_bias, v_a_log, v_d_skip, v_ssm_norm_w, v_sinks, v_w_out):
    xi, yi, ci = lax.axis_index("x"), lax.axis_index("y"), lax.axis_index("c")
    chip = 2 * xi + yi
    me = 2 * chip + ci

    w_in_b = _cast_bf16(w_in, 512, "cast_w_in")
    w_out_b = _cast_bf16(w_out, 512, "cast_w_out")
    gathers = []
    for i in range(DEPTH):
        lands = [lax.dynamic_update_slice(lax.empty((4,) + a.shape[1:], a.dtype), a[i][None], (chip, 0, 0)) for a in (w_in_b, w_out_b)]
        gathers.append(_split_start(None, lands, f"gather_start{i}", "half" if i == 0 else "whole"))
    all_started = gathers[0][3] + gathers[1][3] + gathers[2][3] + gathers[3][3]

    def weights_of(i, after):
        send_sems, recv_sems, thru, _ = gathers[i]
        if i == 0:
            halves = _split_wait(send_sems, recv_sems, thru, 2, all_started + mod[:1, :LANES], "gather_wait0", "half")
            send_sems, recv_sems, thru, after = _split_start(None, halves, "share_start0", "sibling")
            g_in, g_out = _split_wait(send_sems, recv_sems, thru, 2, after, "share_wait0", "sibling")
        else:
            g_in, g_out = _split_wait(send_sems, recv_sems, thru, 2, after, f"gather_wait{i}")
        return _w_in_padded(g_in, "w_in_padded")[None], g_out.reshape(1, 2048, D)

    scatters = [None] * DEPTH

    def grads_done(i, dwa, dwb, dwo):
        blocks = [_grad_blocks(dwa, dwb, "grad_blocks"), _cast_bf16(dwo.reshape(4, 512, D), 512, "cast_dw_out")]
        scatters[i] = _split_start(blocks, [lax.empty(b.shape, b.dtype) for b in blocks], f"scatter_start{i}")
        return scatters[i][3]

    g0 = _allgather8(_pack([c, conv_w]), "gather_c")
    c_all = g0[:, :8, :].reshape(8, D)
    conv_w_full = jnp.concatenate([g0[2 * k, 8:56, :].reshape(DEPTH, 4, CONV_CH // 4) for k in range(4)], axis=-1)

    ada_b_mine = lax.dynamic_slice_in_dim(ada_b, 768 * chip, 768, axis=1).reshape(DEPTH, 1, 768)
    gm = _allgather8(_mod_part(c_all, ada_w, ada_b_mine, "mod_part").reshape(DEPTH * 8, 768), "gather_mod")
    gm = gm.reshape(4, 2, DEPTH, 8, 768)[:, 0]
    mod = lax.dynamic_index_in_dim(gm, me, axis=2, keepdims=False).transpose(1, 0, 2).reshape(DEPTH, 3 * D)

    ltile, dx, dmod, small = _local_step(x[0], loss_target[0], mod, weights_of, grads_done, pre_norm_w, post_norm_w, conv_w_full,
                                         conv_b, dt_bias, a_log, d_skip, ssm_norm_w, sinks)

    packed = _pack([dmod] + [g for layer in small for g in layer] + [ltile[0]])
    gs = _allgather8(packed, "gather_small")
    tot = _sum_blocks(gs[:, None], packed.shape[0], "sum_small")[0]
    parts = _unpack(tot, [(DEPTH, 3 * D)] + list(_SMALL) * DEPTH + [(LANES,)])
    g_ada_b, loss = parts[0], parts[-1][0]
    per_layer = [parts[1 + len(_SMALL) * i:1 + len(_SMALL) * (i + 1)] for i in range(DEPTH)]
    g_pre, g_post, g_cw, g_cb, g_dtb, g_al, g_dsk, g_nw, g_sk = [jnp.stack([per_layer[i][j] for i in range(DEPTH)]) for j in range(len(_SMALL))]
    g_pre, g_post, g_cb, g_nw = g_pre[:, 0], g_post[:, 0], g_cb[:, 0], g_nw[:, 0]
    g_cw = lax.dynamic_slice_in_dim(g_cw, (CONV_CH // 4) * chip, CONV_CH // 4, axis=2)

    dmod_all = gs[:, :(DEPTH * 3 * D) // LANES, :].reshape(8, DEPTH, 3 * D).transpose(1, 0, 2)
    dmod_mine = lax.dynamic_slice_in_dim(dmod_all, 768 * chip, 768, axis=2)
    c_t = jnp.pad(c_all.T, ((0, 0), (0, LANES - 8)))
    g_ada_w = _ada_grad(c_t, dmod_mine, "ada_grad")

    res = {}
    res["ada_w"] = _adamw(ada_w, [g_ada_w], m_ada_w, v_ada_w, 512, "adamw_ada_w")
    names = ["ada_b", "pre_norm_w", "post_norm_w", "conv_w", "conv_b", "dt_bias", "a_log", "d_skip", "ssm_norm_w", "sinks"]
    ws = [ada_b, pre_norm_w, post_norm_w, conv_w, conv_b, dt_bias, a_log, d_skip, ssm_norm_w, sinks]
    gsm = [g_ada_b, g_pre, g_post, g_cw, g_cb, g_dtb, g_al, g_dsk, g_nw, g_sk]
    ms = [m_ada_b, m_pre_norm_w, m_post_norm_w, m_conv_w, m_conv_b, m_dt_bias, m_a_log, m_d_skip, m_ssm_norm_w, m_sinks]
    vs = [v_ada_b, v_pre_norm_w, v_post_norm_w, v_conv_w, v_conv_b, v_dt_bias, v_a_log, v_d_skip, v_ssm_norm_w, v_sinks]
    pw_, pg_, pm_, pv_ = _pack(ws), _pack(gsm), _pack(ms), _pack(vs)
    small_out = _adamw(pw_[None], [pg_[None]], pm_[None], pv_[None], pw_.shape[0], "adamw_small")

    others_done = small_out[1][0, :8] + res["ada_w"][1][0, :8, :LANES]
    landed = [_split_wait(*scatters[i][:3], 2, others_done, f"scatter_wait{i}") for i in range(DEPTH)]
    p_in = _sum_chips([d[2] for d in landed], [d[0] for d in landed], 128, "sum_w_in")
    p_out = _sum_chips([d[3] for d in landed], [d[1] for d in landed], 256, "sum_w_out")
    col_major, row_major = (lambda a: jnp.transpose(a, (2, 0, 1))), (lambda a: jnp.transpose(a, (1, 2, 0)))
    p_in = col_major(p_in)
    s_in, s_out = _sibling_swap([p_in, p_out], "swap_partials")
    res["w_in"] = [row_major(a) for a in _adamw(col_major(w_in), [p_in, s_in], col_major(m_w_in), col_major(v_w_in), None,
                                                "adamw_w_in", lead=SHARD_IN // 18)]
    res["w_out"] = _adamw(w_out, [p_out, s_out], m_w_out, v_w_out, 512, "adamw_w_out")
    shapes = [w.shape for w in ws]
    for kind in range(4):
        for nm, a in zip(names, _unpack(small_out[kind][0], shapes)):
            res.setdefault(nm, [None] * 4)[kind] = a
    order = ["ada_w", "ada_b", "pre_norm_w", "post_norm_w", "w_in", "conv_w", "conv_b", "dt_bias", "a_log", "d_skip", "ssm_norm_w", "sinks", "w_out"]
    return (loss, dx[None], *[res[n][0] for n in order], *[res[n][1] for n in order], *[res[n][2] for n in order], *[res[n][3] for n in order])
```

```python
import math

import jax
import jax.numpy as jnp
from jax import lax
from jax.experimental import pallas as pl
from jax.experimental.pallas import tpu as pltpu

F32 = jnp.float32
MXU = jnp.bfloat16
HI = lax.Precision.HIGHEST
MESH = pl.DeviceIdType.MESH

SEQ = 4096
D = 1024
DEPTH = 4
HD = 64
QK_SCALE = HD ** -0.5
LANES = 128
BLK = 128
DILS = (1, 4, 16)
NEG = -1e30
EPS = 1e-6
MIB = 1024 * 1024

NP = 6144
QA, KA, VA, ZA = 0, 512, 1024, 1536
ZB, XBC = 2048, 3072
QC, ZC, KC, VC = 4608, 5120, 5632, 5760
DTC = 5888
IN_COLS = 5904
SHARD_IN = IN_COLS // 4
CONV_CH = 1536
TM = 512

ADAM_LR, ADAM_B1, ADAM_B2, ADAM_EPS, ADAM_WD, ADAM_STEP = 0.001, 0.9, 0.999, 1e-08, 0.01, 10

NT = (((1,), (1,)), ((), ()))
TN = (((0,), (0,)), ((), ()))


def _cp(vmem_mib=48):
    return pltpu.CompilerParams(vmem_limit_bytes=vmem_mib * MIB)


def _sds(shape, dtype=F32):
    return jax.ShapeDtypeStruct(shape, dtype)


def _full(shape):
    n = len(shape)
    return pl.BlockSpec(shape, lambda *_: (0,) * n)


def _mm(a, b, dims=None):
    if dims is None:
        return jnp.dot(a.astype(MXU), b.astype(MXU), preferred_element_type=F32)
    return lax.dot_general(a.astype(MXU), b.astype(MXU), dims, preferred_element_type=F32)


def _sigmoid(x):
    return 1.0 / (1.0 + jnp.exp(-x))


def _silu(x):
    return x * _sigmoid(x)


def _dsilu(x):
    s = _sigmoid(x)
    return s * (1.0 + x * (1.0 - s))


def _softplus(x):
    ax = jnp.where(x >= 0, x, -x)
    return jnp.maximum(x, 0.0) + jnp.log1p(jnp.exp(-ax))


def _half_masks():
    lane = lax.broadcasted_iota(jnp.int32, (1, LANES), 1)
    m0 = (lane < HD).astype(F32)
    return m0, 1.0 - m0


def _allgather8(v, name):
    r, cc = v.shape

    def body(v_ref, out_ref, send_sems, recv_sems):
        x, y, c = lax.axis_index("x"), lax.axis_index("y"), lax.axis_index("c")
        me = 4 * x + 2 * y + c
        out_ref[me] = v_ref[...]
        peers = []
        for k in range(1, 8):
            px = 1 - x if k & 4 else x
            py = 1 - y if k & 2 else y
            pc = 1 - c if k & 1 else c
            peers.append((px, py, pc))
        sends = []
        for k, peer in enumerate(peers):
            cp = pltpu.make_async_remote_copy(src_ref=v_ref, dst_ref=out_ref.at[me], send_sem=send_sems.at[k],
                                              recv_sem=recv_sems.at[k], device_id=peer, device_id_type=MESH)
            cp.start()
            sends.append(cp)
        for k, (px, py, pc) in enumerate(peers):
            pltpu.make_async_remote_copy(src_ref=v_ref, dst_ref=out_ref.at[4 * px + 2 * py + pc], send_sem=send_sems.at[k],
                                         recv_sem=recv_sems.at[k], device_id=(px, py, pc), device_id_type=MESH).wait_recv()
        for cp in sends:
            cp.wait_send()

    return pl.pallas_call(
        body, name=name, out_shape=_sds((8, r, cc)),
        in_specs=[pl.BlockSpec(memory_space=pltpu.VMEM)], out_specs=pl.BlockSpec(memory_space=pltpu.VMEM),
        scratch_shapes=[pltpu.SemaphoreType.DMA((7,)), pltpu.SemaphoreType.DMA((7,))],
        compiler_params=_cp(32),
    )(v)


_HBM = pl.BlockSpec(memory_space=pltpu.HBM)
_SEM = pl.BlockSpec(memory_space=pltpu.SEMAPHORE)
_EFFECT = pltpu.SideEffectType.DATAFLOW_SIDE_EFFECTING


def _chip_copies(src_refs, land_refs, send_sems, recv_sems, part="whole"):
    x, y, c = lax.axis_index("x"), lax.axis_index("y"), lax.axis_index("c")
    mine = 2 * x + y
    out = []
    for i, land in enumerate(land_refs):
        half = land.shape[1] // 2
        own, others = pl.ds(pl.multiple_of(c * half, half), half), pl.ds(pl.multiple_of((1 - c) * half, half), half)
        for j, (px, py) in enumerate([(1 - x, y), (x, 1 - y), (1 - x, 1 - y)]):
            slot, peer = 2 * px + py, (px, py, c)
            if part == "whole":
                src = src_refs[i].at[slot] if src_refs else land.at[mine]
                there, here = land.at[mine], land.at[slot]
            elif part == "half":
                src = there = land.at[mine].at[own]
                here = land.at[slot].at[own]
            else:
                src = there = land.at[slot].at[own]
                here, peer = land.at[slot].at[others], (x, y, 1 - c)
            mk = lambda dst, i=i, j=j, src=src, peer=peer: pltpu.make_async_remote_copy(
                src_ref=src, dst_ref=dst, send_sem=send_sems.at[3 * i + j], recv_sem=recv_sems.at[3 * i + j],
                device_id=peer, device_id_type=MESH)
            out.append((mk(there), mk(here)))
    return out


def _split_start(srcs, lands, name, part="whole"):
    ops = list(srcs or []) + list(lands)
    ns, n = len(srcs or []), len(lands)

    def body(*refs):
        src_refs, land_refs = refs[:ns], refs[ns:ns + n]
        send_sems, recv_sems = refs[ns + n], refs[ns + n + 1]
        for mine_out, _ in _chip_copies(src_refs, land_refs, send_sems, recv_sems, part):
            mine_out.start()
        refs[-1][...] = jnp.zeros_like(refs[-1])

    sems = pltpu.SemaphoreType.DMA((3 * n,))
    res = pl.pallas_call(
        body, name=name, out_shape=(sems, sems) + tuple(pltpu.HBM(a.shape, a.dtype) for a in ops) + (_sds((8, LANES)),),
        in_specs=[_HBM] * len(ops), out_specs=(_SEM, _SEM) + (_HBM,) * len(ops) + (pl.BlockSpec(memory_space=pltpu.VMEM),),
        input_output_aliases={k: 2 + k for k in range(len(ops))},
        compiler_params=pltpu.CompilerParams(has_side_effects=_EFFECT),
    )(*[pltpu.with_memory_space_constraint(a, pltpu.HBM) for a in ops])
    return res[0], res[1], list(res[2:2 + len(ops)]), res[-1]


def _split_wait(send_sems, recv_sems, thru, n, after, name, part="whole"):
    ns = len(thru) - n

    def body(*refs):
        src_refs, land_refs = refs[:ns], refs[ns:ns + n]
        for mine_out, arriving in _chip_copies(src_refs, land_refs, refs[ns + n], refs[ns + n + 1], part):
            mine_out.wait_send()
            arriving.wait_recv()

    res = pl.pallas_call(
        body, name=name, out_shape=tuple(pltpu.HBM(a.shape, a.dtype) for a in thru),
        in_specs=[_HBM] * len(thru) + [_SEM, _SEM, pl.BlockSpec(memory_space=pl.ANY)], out_specs=(_HBM,) * len(thru),
        input_output_aliases={k: k for k in range(len(thru))},
        compiler_params=pltpu.CompilerParams(has_side_effects=_EFFECT),
    )(*thru, send_sems, recv_sems, after)
    return list(res)


def _to_sibling(src_ref, land_ref, send_sem, recv_sem):
    sib = (lax.axis_index("x"), lax.axis_index("y"), 1 - lax.axis_index("c"))
    return pltpu.make_async_remote_copy(src_ref=src_ref, dst_ref=land_ref, send_sem=send_sem, recv_sem=recv_sem,
                                        device_id=sib, device_id_type=MESH)


def _sibling_start(a, name):
    def body(a_ref, land_ref, send_sem, recv_sem, a_thru, land_thru, token):
        _to_sibling(a_ref, land_ref, send_sem, recv_sem).start()
        token[...] = jnp.zeros_like(token)

    sem = pltpu.SemaphoreType.DMA(())
    return pl.pallas_call(
        body, name=name, out_shape=(sem, sem, pltpu.HBM(a.shape, a.dtype), pltpu.HBM(a.shape, a.dtype), _sds((8, LANES))),
        in_specs=[_HBM, _HBM], out_specs=(_SEM, _SEM, _HBM, _HBM, pl.BlockSpec(memory_space=pltpu.VMEM)),
        input_output_aliases={0: 2, 1: 3}, compiler_params=pltpu.CompilerParams(has_side_effects=_EFFECT),
    )(pltpu.with_memory_space_constraint(a, pltpu.HBM), pltpu.with_memory_space_constraint(lax.empty(a.shape, a.dtype), pltpu.HBM))


def _sibling_wait(send_sem, recv_sem, a_thru, land_thru, after, name):
    def body(a_ref, land_ref, send_sem, recv_sem, after_ref, a_out, land_out):
        copy = _to_sibling(a_ref, land_ref, send_sem, recv_sem)
        copy.wait_send()
        copy.wait_recv()

    return pl.pallas_call(
        body, name=name, out_shape=(pltpu.HBM(a_thru.shape, a_thru.dtype), pltpu.HBM(a_thru.shape, a_thru.dtype)),
        in_specs=[_HBM, _HBM, _SEM, _SEM, pl.BlockSpec(memory_space=pl.ANY)], out_specs=(_HBM, _HBM),
        input_output_aliases={0: 0, 1: 1}, compiler_params=pltpu.CompilerParams(has_side_effects=_EFFECT),
    )(a_thru, land_thru, send_sem, recv_sem, after)


def _tile_spec(rows, cc):
    return pl.BlockSpec((None, rows, cc), lambda l, i: (l, i, 0))


def _cast_bf16(a, rows, name):
    nl, r, cc = a.shape

    def body(a_ref, o_ref):
        o_ref[...] = a_ref[...].astype(jnp.bfloat16)

    return pl.pallas_call(body, name=name, out_shape=_sds((nl, r, cc), jnp.bfloat16), grid=(nl, r // rows),
                          in_specs=[_tile_spec(rows, cc)], out_specs=_tile_spec(rows, cc), compiler_params=_cp())(a)


def _sum_blocks(a, rows, name):
    k, nl, r, cc = a.shape

    def body(a_ref, o_ref):
        acc = a_ref[0].astype(F32)
        for j in range(1, k):
            acc = acc + a_ref[j].astype(F32)
        o_ref[...] = acc

    return pl.pallas_call(body, name=name, out_shape=_sds((nl, r, cc)), grid=(nl, r // rows),
                          in_specs=[pl.BlockSpec((k, None, rows, cc), lambda l, i: (0, l, i, 0))],
                          out_specs=_tile_spec(rows, cc), compiler_params=_cp())(a)


def _sum_chips(lands, srcs, rows, name):
    nl = len(lands)
    _, r, cc = lands[0].shape

    def body(*refs):
        land_refs, src_refs, o_ref = refs[:nl], refs[nl:2 * nl], refs[2 * nl]
        mine = 2 * lax.axis_index("x") + lax.axis_index("y")
        for j in range(nl):
            @pl.when(pl.program_id(0) == j)
            def _(j=j):
                own = src_refs[j][mine].astype(F32)
                acc = None
                for k in range(4):
                    term = jnp.where(mine == k, own, land_refs[j][k].astype(F32))
                    acc = term if acc is None else acc + term
                o_ref[...] = acc

    specs = [pl.BlockSpec((4, rows, cc), lambda l, i, j=j: (0, jnp.where(l == j, i, 0), 0)) for j in range(nl)]
    return pl.pallas_call(body, name=name, out_shape=_sds((nl, r, cc)), grid=(nl, r // rows),
                          in_specs=specs + specs, out_specs=_tile_spec(rows, cc), compiler_params=_cp())(*lands, *srcs)


def _adamw(w, parts, m, v, rows, name, lead=None):
    nl, r, cc = w.shape
    np_ = len(parts)
    c1 = 1.0 / (1.0 - ADAM_B1 ** ADAM_STEP)
    c2 = 1.0 / (1.0 - ADAM_B2 ** ADAM_STEP)

    def body(*refs):
        w_ref, p_refs, (m_ref, v_ref, g_ref, d_ref, nm_ref, nv_ref) = refs[0], refs[1:1 + np_], refs[1 + np_:]
        g = p_refs[0][...]
        for p_ref in p_refs[1:]:
            g = g + p_ref[...]
        nm = ADAM_B1 * m_ref[...] + (1.0 - ADAM_B1) * g
        nv = ADAM_B2 * v_ref[...] + (1.0 - ADAM_B2) * (g * g)
        g_ref[...] = g
        nm_ref[...] = nm
        nv_ref[...] = nv
        d_ref[...] = -ADAM_LR * ((nm * c1) / (jnp.sqrt(nv * c2) + ADAM_EPS) + ADAM_WD * w_ref[...])

    if lead is None:
        spec, grid = _tile_spec(rows, cc), (nl, r // rows)
    else:
        spec, grid = pl.BlockSpec((lead, r, cc), lambda i: (i, 0, 0)), (nl // lead,)
    return pl.pallas_call(body, name=name, out_shape=(_sds((nl, r, cc)),) * 4, grid=grid,
                          in_specs=[spec] * (3 + np_), out_specs=(spec,) * 4, compiler_params=_cp())(w, *parts, m, v)


_BIAS = pltpu.VMEM((2, 2 * BLK, 2 * BLK), F32)


def _fill_band_bias(bias_ref):
    qi = lax.broadcasted_iota(jnp.int32, (2 * BLK, 2 * BLK), 0) & (BLK - 1)
    kj = lax.broadcasted_iota(jnp.int32, (2 * BLK, 2 * BLK), 1)
    dist = BLK + qi - kj
    band = (dist >= 0) & (dist <= BLK)
    bias_ref[0] = jnp.where(band, 0.0, NEG)
    bias_ref[1] = jnp.where(band & (kj >= BLK), 0.0, NEG)


class _HeadStack:
    def __init__(self, group):
        self.m0, self.m1 = _half_masks()
        self.group = group
        if group is not None:
            self.kv_mask = (self.m0, self.m1)[group]

    def _swap_half(self, t, a):
        return t if a == self.group else pltpu.roll(t, HD, axis=1)

    def stack(self, t):
        t0, t1 = t * self.m0, t * self.m1
        if self.group is not None:
            t0, t1 = self._swap_half(t0, 0), self._swap_half(t1, 1)
        return jnp.concatenate([t0, t1], axis=0)

    def unstack(self, ts):
        if self.group is None:
            return ts[:BLK] * self.m0 + ts[BLK:] * self.m1
        return self._swap_half(ts[:BLK] * self.kv_mask, 0) + self._swap_half(ts[BLK:] * self.kv_mask, 1)


def _rows(st, dil):
    if dil == 1:
        return pl.ds(pl.multiple_of(st, BLK), BLK)
    return pl.ds(st, BLK, stride=dil)


def _block_pos(n, dil):
    nb = SEQ // (dil * BLK)
    r, b = n // nb, n % nb
    hp = (b > 0).astype(jnp.int32)
    st = r + dil * BLK * b
    return st, st - dil * BLK * hp, 1 - hp


def _attn_fwd(proj, qblk, kblk, vblk, dils, gqa, sink_x, name):
    has_sink = sink_x is not None

    def body(*refs):
        if has_sink:
            q_ref, k_ref, v_ref, s_ref, o_ref, lse_ref, m_scr, z_scr, bias_scr = refs
        else:
            q_ref, k_ref, v_ref, o_ref, lse_ref, m_scr, z_scr, bias_scr = refs

        @pl.when(pl.program_id(0) == 0)
        def _():
            _fill_band_bias(bias_scr)
        o_ref[...] = jnp.zeros_like(o_ref)
        if has_sink:
            z_scr[...] = jnp.ones_like(z_scr)
            m_scr[...] = jnp.broadcast_to(s_ref[...], m_scr.shape)
        else:
            z_scr[...] = jnp.zeros_like(z_scr)
            m_scr[...] = jnp.full_like(m_scr, NEG)

        def step(n, carry, dil, heads):
            m0, m1 = heads.m0, heads.m1
            st, stp, first = _block_pos(n, dil)
            rq, rp = _rows(st, dil), _rows(stp, dil)
            kk = jnp.concatenate([k_ref[rp, :], k_ref[rq, :]], axis=0)
            vv = jnp.concatenate([v_ref[rp, :], v_ref[rq, :]], axis=0)
            s = _mm(heads.stack(q_ref[rq, :] * QK_SCALE), kk, NT) + bias_scr[first]
            m = jnp.max(s, axis=1, keepdims=True)
            p = jnp.exp(s - m)
            l = jnp.sum(p, axis=1, keepdims=True)
            o_pair = heads.unstack(_mm(p, vv))
            m_pair = m[:BLK] * m0 + m[BLK:] * m1
            l_pair = l[:BLK] * m0 + l[BLK:] * m1
            m_old = m_scr[rq, :]
            m_new = jnp.maximum(m_old, m_pair)
            alpha, beta = jnp.exp(m_old - m_new), jnp.exp(m_pair - m_new)
            o_ref[rq, :] = o_ref[rq, :] * alpha + o_pair * beta
            z_scr[rq, :] = z_scr[rq, :] * alpha + l_pair * beta
            m_scr[rq, :] = m_new
            return carry

        def blocks(heads):
            for dil in dils:
                lax.fori_loop(0, SEQ // BLK, lambda n, carry, dil=dil: step(n, carry, dil, heads), 0, unroll=8)

        if gqa:
            for grp in range(2):
                pl.when(pl.program_id(0) // 2 == grp)(lambda grp=grp: blocks(_HeadStack(grp)))
        else:
            blocks(_HeadStack(None))

        def fin(t, carry):
            rt = pl.ds(pl.multiple_of(t * TM, TM), TM)
            z = z_scr[rt, :]
            o_ref[rt, :] = o_ref[rt, :] / z
            lse_ref[rt, :] = m_scr[rt, :] + jnp.log(z)
            return carry
        lax.fori_loop(0, SEQ // TM, fin, 0)

    col = lambda blk: pl.BlockSpec((SEQ, LANES), lambda p, blk=blk: (0, blk + p))
    kv = (lambda blk: pl.BlockSpec((SEQ, LANES), lambda p, blk=blk: (0, blk))) if gqa else col
    in_specs = [col(qblk), kv(kblk), kv(vblk)]
    args = [proj, proj, proj]
    if has_sink:
        in_specs.append(pl.BlockSpec((1, LANES), lambda p: (0, p)))
        args.append(sink_x)
    out = pl.BlockSpec((SEQ, LANES), lambda p: (0, p))
    return pl.pallas_call(body, name=name, out_shape=(_sds((SEQ, 512)), _sds((SEQ, 512))), grid=(4,),
                          in_specs=in_specs, out_specs=(out, out),
                          scratch_shapes=[pltpu.VMEM((SEQ, LANES), F32), pltpu.VMEM((SEQ, LANES), F32), _BIAS],
                          compiler_params=_cp(48))(*args)


def _attn_bwd(proj, qblk, kblk, vblk, do, o, lse, dils, gqa, sink_x, name):
    has_sink = sink_x is not None

    def body(*refs):
        if has_sink:
            q_ref, k_ref, v_ref, do_ref, o_ref, lse_ref, s_ref, dq_ref, dk_ref, dv_ref, ds_ref, bias_scr = refs
        else:
            q_ref, k_ref, v_ref, do_ref, o_ref, lse_ref, dq_ref, dk_ref, dv_ref, bias_scr = refs
        pid = pl.program_id(0)

        @pl.when(pid == 0)
        def _():
            _fill_band_bias(bias_scr)
        dq_ref[...] = jnp.zeros_like(dq_ref)
        if gqa:
            @pl.when(pid == 0)
            def _():
                dk_ref[...] = jnp.zeros_like(dk_ref)
                dv_ref[...] = jnp.zeros_like(dv_ref)
        else:
            dk_ref[...] = jnp.zeros_like(dk_ref)
            dv_ref[...] = jnp.zeros_like(dv_ref)

        def step(n, carry, dil, heads):
            m0, m1 = heads.m0, heads.m1
            st, stp, first = _block_pos(n, dil)
            rq, rp = _rows(st, dil), _rows(stp, dil)
            do_, lse_ = do_ref[rq, :], lse_ref[rq, :]
            kk = jnp.concatenate([k_ref[rp, :], k_ref[rq, :]], axis=0)
            vv = jnp.concatenate([v_ref[rp, :], v_ref[rq, :]], axis=0)
            qs, dos = heads.stack(q_ref[rq, :] * QK_SCALE), heads.stack(do_)
            doo = do_ * o_ref[rq, :]
            delta = jnp.concatenate([jnp.sum(doo * m0, axis=1, keepdims=True), jnp.sum(doo * m1, axis=1, keepdims=True)], axis=0)
            lse_s = jnp.concatenate([lse_[:, 0:1], lse_[:, HD:HD + 1]], axis=0)
            p = jnp.exp(_mm(qs, kk, NT) + bias_scr[first] - lse_s)
            ds = p * (_mm(dos, vv, NT) - delta)
            dq_ref[rq, :] += heads.unstack(_mm(ds, kk)) * QK_SCALE
            dk_sum, dv_sum = _mm(ds, qs, TN), _mm(p, dos, TN)
            dk_ref[rp, :] += dk_sum[:BLK]
            dk_ref[rq, :] += dk_sum[BLK:]
            dv_ref[rp, :] += dv_sum[:BLK]
            dv_ref[rq, :] += dv_sum[BLK:]
            return carry

        def blocks(heads):
            for dil in dils:
                lax.fori_loop(0, SEQ // BLK, lambda n, carry, dil=dil: step(n, carry, dil, heads), 0, unroll=4)

        if gqa:
            for grp in range(2):
                pl.when(pid // 2 == grp)(lambda grp=grp: blocks(_HeadStack(grp)))
        else:
            blocks(_HeadStack(None))

        if has_sink:
            m0, m1 = _half_masks()

            def sink_rows(t, acc):
                rt = pl.ds(pl.multiple_of(t * TM, TM), TM)
                return acc - jnp.sum(jnp.exp(s_ref[...] - lse_ref[rt, :]) * (do_ref[rt, :] * o_ref[rt, :]), axis=0, keepdims=True)
            acc = lax.fori_loop(0, SEQ // TM, sink_rows, jnp.zeros((1, LANES), F32))
            per_head = jnp.sum(acc * m0, axis=1, keepdims=True) * m0 + jnp.sum(acc * m1, axis=1, keepdims=True) * m1
            ds_ref[0] = jnp.broadcast_to(per_head, (8, LANES))

    col = lambda blk: pl.BlockSpec((SEQ, LANES), lambda p, blk=blk: (0, blk + p))
    kv = (lambda blk: pl.BlockSpec((SEQ, LANES), lambda p, blk=blk: (0, blk))) if gqa else col
    pair = pl.BlockSpec((SEQ, LANES), lambda p: (0, p))
    in_specs = [col(qblk), kv(kblk), kv(vblk), pair, pair, pair]
    args = [proj, proj, proj, do, o, lse]
    kvw = LANES if gqa else 512
    kv_out = pl.BlockSpec((SEQ, LANES), lambda p: (0, 0)) if gqa else pair
    out_shape = [_sds((SEQ, 512)), _sds((SEQ, kvw)), _sds((SEQ, kvw))]
    out_specs = [pair, kv_out, kv_out]
    if has_sink:
        in_specs.append(pl.BlockSpec((1, LANES), lambda p: (0, p)))
        args.append(sink_x)
        out_shape.append(_sds((4, 8, LANES)))
        out_specs.append(pl.BlockSpec((1, 8, LANES), lambda p: (p, 0, 0)))
    return pl.pallas_call(body, name=name, out_shape=tuple(out_shape), grid=(4,), in_specs=in_specs,
                          out_specs=tuple(out_specs), scratch_shapes=[_BIAS], compiler_params=_cp(56))(*args)


_CT = 128


def _rows_before(x_ref, t, k):
    if t == 0:
        return jnp.concatenate([jnp.zeros((k, LANES), F32), x_ref[0:_CT - k, :]], axis=0)
    return x_ref[t * _CT - k:(t + 1) * _CT - k, :]


def _conv_pre(x_ref, w_ref, b_ref, t):
    taps = [x_ref[t * _CT:(t + 1) * _CT, :]] + [_rows_before(x_ref, t, k) for k in range(1, 4)]
    u = b_ref[...] + taps[0] * w_ref[3:4, :]
    for k in range(1, 4):
        u = u + taps[k] * w_ref[3 - k:4 - k, :]
    return u, taps


def _conv_fwd(proj, w, b, name):
    def body(x_ref, w_ref, b_ref, o_ref):
        for t in range(SEQ // _CT):
            o_ref[t * _CT:(t + 1) * _CT, :] = _silu(_conv_pre(x_ref, w_ref, b_ref, t)[0])

    nblk = CONV_CH // LANES
    return pl.pallas_call(body, name=name, out_shape=_sds((SEQ, CONV_CH)), grid=(nblk,),
                          in_specs=[pl.BlockSpec((SEQ, LANES), lambda j: (0, XBC // LANES + j)),
                                    pl.BlockSpec((4, LANES), lambda j: (0, j)), pl.BlockSpec((1, LANES), lambda j: (0, j))],
                          out_specs=pl.BlockSpec((SEQ, LANES), lambda j: (0, j)), compiler_params=_cp())(proj, w, b)


def _conv_bwd(proj, dact, w, b, name):
    def body(x_ref, da_ref, w_ref, b_ref, dx_ref, dw_ref, db_ref, du_scr):
        du_scr[SEQ:SEQ + 8, :] = jnp.zeros((8, LANES), F32)
        db = jnp.zeros((1, LANES), F32)
        dws = [jnp.zeros((1, LANES), F32)] * 4
        for t in range(SEQ // _CT):
            u, taps = _conv_pre(x_ref, w_ref, b_ref, t)
            du = da_ref[t * _CT:(t + 1) * _CT, :] * _dsilu(u)
            du_scr[t * _CT:(t + 1) * _CT, :] = du
            db = db + jnp.sum(du, axis=0, keepdims=True)
            dws = [dws[k] + jnp.sum(du * taps[k], axis=0, keepdims=True) for k in range(4)]
        db_ref[...] = db
        for k in range(4):
            dw_ref[3 - k:4 - k, :] = dws[k]
        for t in range(SEQ // _CT):
            dx = du_scr[t * _CT:(t + 1) * _CT, :] * w_ref[3:4, :]
            for k in range(1, 4):
                dx = dx + du_scr[t * _CT + k:(t + 1) * _CT + k, :] * w_ref[3 - k:4 - k, :]
            dx_ref[t * _CT:(t + 1) * _CT, :] = dx.astype(dx_ref.dtype)

    nblk = CONV_CH // LANES
    blk = pl.BlockSpec((SEQ, LANES), lambda j: (0, j))
    wspec, bspec = pl.BlockSpec((4, LANES), lambda j: (0, j)), pl.BlockSpec((1, LANES), lambda j: (0, j))
    return pl.pallas_call(body, name=name, out_shape=(_sds((SEQ, CONV_CH), MXU), _sds((4, CONV_CH)), _sds((1, CONV_CH))), grid=(nblk,),
                          in_specs=[pl.BlockSpec((SEQ, LANES), lambda j: (0, XBC // LANES + j)), blk, wspec, bspec],
                          out_specs=(blk, wspec, bspec), scratch_shapes=[pltpu.VMEM((SEQ + 8, LANES), F32)],
                          compiler_params=_cp())(proj, dact, w, b)


def _ssd_chunk(xs, bm, cm, dtr, z, hs, al16, dtb, dskx, nw):
    m0, m1 = _half_masks()
    row = lax.broadcasted_iota(jnp.int32, (BLK, BLK), 0)
    col = lax.broadcasted_iota(jnp.int32, (BLK, BLK), 1)
    causal = row >= col
    tril = causal.astype(F32)
    lane = lax.broadcasted_iota(jnp.int32, (1, LANES), 1)
    sub = lax.broadcasted_iota(jnp.int32, (BLK, 1), 0)
    last_row = (sub == BLK - 1).astype(F32)
    dt = jnp.where(lane < 16, _softplus(dtr + dtb), 0.0)
    a16 = -jnp.exp(al16)
    acum = jnp.dot(tril, dt * a16, precision=HI, preferred_element_type=F32)
    acum_t = acum.T
    gmat = [_mm(cm[g], bm[g], NT) for g in range(2)]
    ys, hn = [], []
    for p in range(8):
        g = p // 4
        pick = [(lane == 2 * p + a).astype(F32) for a in range(2)]
        col_h = [jnp.sum(acum * pick[a], axis=1, keepdims=True) for a in range(2)]
        dt_x = sum(jnp.sum(dt * pick[a], axis=1, keepdims=True) * msk for a, msk in enumerate((m0, m1)))
        ac_x = col_h[0] * m0 + col_h[1] * m1
        a_end = jnp.sum(ac_x * last_row, axis=0, keepdims=True)
        xdt = xs[p] * dt_x
        y = _mm(cm[g], hs[p]) * jnp.exp(ac_x)
        for a, msk in enumerate((m0, m1)):
            row_h = jnp.sum(acum_t * (sub == 2 * p + a).astype(F32), axis=0, keepdims=True)
            decay = jnp.exp(jnp.where(causal, col_h[a] - row_h, NEG))
            y = y + _mm(gmat[g] * decay, xdt * msk)
        st = _mm(bm[g], xdt * jnp.exp(a_end - ac_x), TN)
        hn.append(hs[p] * jnp.exp(a_end) + st)
        y = y + dskx[p] * xs[p]
        ys.append(y * _silu(z[p]))
    out = []
    for g in range(2):
        ms = sum(jnp.sum(ys[p] * ys[p], axis=1, keepdims=True) for p in range(4 * g, 4 * g + 4)) * (1.0 / 512)
        rstd = lax.rsqrt(ms + EPS)
        out += [ys[p] * rstd * nw[p] for p in range(4 * g, 4 * g + 4)]
    return out, hn


def _tiles(ref, n, off=0):
    return [ref[:, off + LANES * p:off + LANES * (p + 1)] for p in range(n)]


def _ssd_load(xbc_ref, z_ref, dt_ref, al16_ref, dtb_ref, dsk_ref, nw_ref):
    return (_tiles(xbc_ref, 8), _tiles(xbc_ref, 2, 1024), _tiles(xbc_ref, 2, 1280), dt_ref[...], _tiles(z_ref, 8)), \
           (al16_ref[...], dtb_ref[...], _tiles(dsk_ref, 8), _tiles(nw_ref, 8))


_NCH = SEQ // BLK


def _ssd_param_specs():
    return [_full((1, LANES)), _full((1, LANES)), _full((1, 1024)), _full((1, 1024))]


def _ssd_fwd(xbc_act, proj, al16, dtb, dskx, nw, name):
    def body(xbc_ref, z_ref, dt_ref, al16_ref, dtb_ref, dsk_ref, nw_ref, y_ref, hin_ref, h_scr):
        @pl.when(pl.program_id(0) == 0)
        def _():
            h_scr[...] = jnp.zeros_like(h_scr)
        acts, params = _ssd_load(xbc_ref, z_ref, dt_ref, al16_ref, dtb_ref, dsk_ref, nw_ref)
        hs = _tiles(h_scr, 8)
        hin_ref[0] = h_scr[...]
        ys, hn = _ssd_chunk(*acts, hs, *params)
        for p in range(8):
            y_ref[:, LANES * p:LANES * (p + 1)] = ys[p].astype(y_ref.dtype)
            h_scr[:, LANES * p:LANES * (p + 1)] = hn[p]

    return pl.pallas_call(
        body, name=name, out_shape=(_sds((SEQ, 1024), MXU), _sds((_NCH, BLK, 1024))), grid=(_NCH,),
        in_specs=[pl.BlockSpec((BLK, CONV_CH), lambda c: (c, 0)), pl.BlockSpec((BLK, 1024), lambda c: (c, ZB // 1024)),
                  pl.BlockSpec((BLK, LANES), lambda c: (c, DTC // LANES))] + _ssd_param_specs(),
        out_specs=(pl.BlockSpec((BLK, 1024), lambda c: (c, 0)), pl.BlockSpec((1, BLK, 1024), lambda c: (c, 0, 0))),
        scratch_shapes=[pltpu.VMEM((BLK, 1024), F32)], compiler_params=_cp())(xbc_act, proj, proj, al16, dtb, dskx, nw)


def _ssd_bwd(xbc_act, proj, hin, dyb, al16, dtb, dskx, nw, name):
    def body(xbc_ref, z_ref, dt_ref, hin_ref, dy_ref, al16_ref, dtb_ref, dsk_ref, nw_ref,
             dxbc_ref, dz_ref, ddt_ref, dal16_ref, ddtb_ref, ddsk_ref, dnw_ref, dh_scr):
        @pl.when(pl.program_id(0) == 0)
        def _():
            dh_scr[...] = jnp.zeros_like(dh_scr)
            for r in (dal16_ref, ddtb_ref, ddsk_ref, dnw_ref):
                r[...] = jnp.zeros_like(r)
        acts, params = _ssd_load(xbc_ref, z_ref, dt_ref, al16_ref, dtb_ref, dsk_ref, nw_ref)
        hs = [hin_ref[0, :, LANES * p:LANES * (p + 1)] for p in range(8)]
        _, vjp = jax.vjp(lambda a, h, q: _ssd_chunk(*a, h, *q), acts, hs, params)
        (dxs, dbm, dcm, ddt, dz), dhs, (dal16, ddtb, ddsk, dnw) = vjp((_tiles(dy_ref, 8), _tiles(dh_scr, 8)))
        for p in range(8):
            cols = slice(LANES * p, LANES * (p + 1))
            dxbc_ref[:, cols] = dxs[p]
            dz_ref[:, cols] = dz[p].astype(dz_ref.dtype)
            dh_scr[:, cols] = dhs[p]
            ddsk_ref[:, cols] += ddsk[p]
            dnw_ref[:, cols] += dnw[p]
        for g in range(2):
            dxbc_ref[:, 1024 + LANES * g:1024 + LANES * (g + 1)] = dbm[g]
            dxbc_ref[:, 1280 + LANES * g:1280 + LANES * (g + 1)] = dcm[g]
        ddt_ref[...] = ddt.astype(ddt_ref.dtype)
        dal16_ref[...] += dal16
        ddtb_ref[...] += ddtb

    rev = lambda c: _NCH - 1 - c
    return pl.pallas_call(
        body, name=name,
        out_shape=(_sds((SEQ, CONV_CH)), _sds((SEQ, 1024), MXU), _sds((SEQ, LANES), MXU),
                   _sds((1, LANES)), _sds((1, LANES)), _sds((1, 1024)), _sds((1, 1024))),
        grid=(_NCH,),
        in_specs=[pl.BlockSpec((BLK, CONV_CH), lambda c: (rev(c), 0)), pl.BlockSpec((BLK, 1024), lambda c: (rev(c), ZB // 1024)),
                  pl.BlockSpec((BLK, LANES), lambda c: (rev(c), DTC // LANES)), pl.BlockSpec((1, BLK, 1024), lambda c: (rev(c), 0, 0)),
                  pl.BlockSpec((BLK, 1024), lambda c: (rev(c), 0))] + _ssd_param_specs(),
        out_specs=(pl.BlockSpec((BLK, CONV_CH), lambda c: (rev(c), 0)), pl.BlockSpec((BLK, 1024), lambda c: (rev(c), 0)),
                   pl.BlockSpec((BLK, LANES), lambda c: (rev(c), 0)),
                   _full((1, LANES)), _full((1, LANES)), _full((1, 1024)), _full((1, 1024))),
        scratch_shapes=[pltpu.VMEM((BLK, 1024), F32)], compiler_params=_cp())(xbc_act, proj, proj, hin, dyb, al16, dtb, dskx, nw)


def _rstd(v):
    return lax.rsqrt(jnp.mean(v * v, axis=1, keepdims=True) + EPS)


def _rms_bwd(dn, n, rstd):
    return rstd * (dn - n * jnp.mean(dn * n, axis=1, keepdims=True))


_VEC = _full((1, D))


def _layer_spec(layer):
    return pl.BlockSpec((None, 2048, D), lambda *_: (layer, 0, 0))

_ROW = pl.BlockSpec((TM, D), lambda i, *_: (i, 0))


def _proj_fwd(x, pre_w, scale, shift, w, layer, name):
    tn, ni = 1024, SEQ // TM

    def body(x_ref, pw_ref, sc_ref, sh_ref, w_ref, o_ref, h_ref, h_scr):
        rows = pl.ds(pl.multiple_of(pl.program_id(1) * TM, TM), TM)

        @pl.when(pl.program_id(0) == 0)
        def _():
            xv = x_ref[...]
            h = ((xv * _rstd(xv) * pw_ref[...]) * (1.0 + sc_ref[...]) + sh_ref[...]).astype(h_ref.dtype)
            h_scr[rows, :] = h
            h_ref[...] = h
        o_ref[...] = jnp.dot(h_scr[rows, :], w_ref[...].astype(MXU), preferred_element_type=F32)

    first_pass = pl.BlockSpec((TM, D), lambda j, i: (jnp.where(j == 0, i, ni - 1), 0))
    return pl.pallas_call(body, name=name, out_shape=(_sds((SEQ, NP)), _sds((SEQ, D), MXU)), grid=(NP // tn, ni),
                          in_specs=[first_pass, _VEC, _VEC, _VEC, pl.BlockSpec((None, D, tn), lambda j, i: (layer, 0, j))],
                          out_specs=(pl.BlockSpec((TM, tn), lambda j, i: (i, j)), first_pass),
                          scratch_shapes=[pltpu.VMEM((SEQ, D), MXU)], compiler_params=_cp())(x, pre_w, scale, shift, w)


_HALF = pl.BlockSpec((TM, 512), lambda i: (i, 0))
_Z_A = pl.BlockSpec((TM, 512), lambda i: (i, ZA // 512))
_Z_C = pl.BlockSpec((TM, 512), lambda i: (i, ZC // 512))


def _out_fwd(o_a, yb, o_c, proj, w, layer, x, gate, post_w, name):
    def body(oa_ref, yb_ref, oc_ref, za_ref, zc_ref, w_ref, x_ref, g_ref, pw_ref, xn_ref, y_ref):
        y = (_mm(oa_ref[...] * _silu(za_ref[...]), w_ref[0:512, :]) + _mm(yb_ref[...], w_ref[512:1536, :])
             + _mm(oc_ref[...] * _silu(zc_ref[...]), w_ref[1536:2048, :]))
        y_ref[...] = y
        xn_ref[...] = x_ref[...] + g_ref[...] * (y * _rstd(y) * pw_ref[...])

    return pl.pallas_call(body, name=name, out_shape=(_sds((SEQ, D)), _sds((SEQ, D))), grid=(SEQ // TM,),
                          in_specs=[_HALF, _ROW, _HALF, _Z_A, _Z_C, _layer_spec(layer), _ROW, _VEC, _VEC],
                          out_specs=(_ROW, _ROW), compiler_params=_cp())(o_a, yb, o_c, proj, proj, w, x, gate, post_w)


def _dymix(dxo, y, gate, post_w, w, layer, o_a, o_c, proj, name):
    def body(dx_ref, y_ref, g_ref, pw_ref, w_ref, oa_ref, oc_ref, za_ref, zc_ref,
             dy_ref, dg_ref, dpw_ref, doa_ref, dza_ref, b_ref, doc_ref, dzc_ref):
        @pl.when(pl.program_id(0) == 0)
        def _():
            dg_ref[...] = jnp.zeros_like(dg_ref)
            dpw_ref[...] = jnp.zeros_like(dpw_ref)
        dx, yv = dx_ref[...], y_ref[...]
        rstd = _rstd(yv)
        n = yv * rstd
        dg_ref[...] += jnp.sum(dx * (n * pw_ref[...]), axis=0, keepdims=True)
        dr = dx * g_ref[...]
        dpw_ref[...] += jnp.sum(dr * n, axis=0, keepdims=True)
        dy = _rms_bwd(dr * pw_ref[...], n, rstd)
        dy_ref[...] = dy
        b_ref[...] = _mm(dy, w_ref[512:1536, :], NT)
        for rows, o_ref, z_ref, do_ref, dz_ref in ((slice(0, 512), oa_ref, za_ref, doa_ref, dza_ref),
                                                   (slice(1536, 2048), oc_ref, zc_ref, doc_ref, dzc_ref)):
            dyg, z = _mm(dy, w_ref[rows, :], NT), z_ref[...]
            do_ref[...] = dyg * _silu(z)
            dz_ref[...] = (dyg * o_ref[...] * _dsilu(z)).astype(dz_ref.dtype)

    return pl.pallas_call(body, name=name,
                          out_shape=(_sds((SEQ, D)), _sds((1, D)), _sds((1, D)),
                                     _sds((SEQ, 512)), _sds((SEQ, 512), MXU), _sds((SEQ, D)), _sds((SEQ, 512)), _sds((SEQ, 512), MXU)),
                          grid=(SEQ // TM,), in_specs=[_ROW, _ROW, _VEC, _VEC, _layer_spec(layer), _HALF, _HALF, _Z_A, _Z_C],
                          out_specs=(_ROW, _VEC, _VEC, _HALF, _HALF, _ROW, _HALF, _HALF),
                          compiler_params=_cp())(dxo, y, gate, post_w, w, o_a, o_c, proj, proj)


def _dwout(o_a, yb, o_c, proj, dy, name):
    def body(oa_ref, yb_ref, oc_ref, za_ref, zc_ref, dy_ref, o_ref):
        @pl.when(pl.program_id(0) == 0)
        def _():
            o_ref[...] = jnp.zeros_like(o_ref)
        dy = dy_ref[...]
        o_ref[0:512, :] += _mm(oa_ref[...] * _silu(za_ref[...]), dy, TN)
        o_ref[512:1536, :] += _mm(yb_ref[...], dy, TN)
        o_ref[1536:2048, :] += _mm(oc_ref[...] * _silu(zc_ref[...]), dy, TN)

    return pl.pallas_call(body, name=name, out_shape=_sds((2048, D)), grid=(SEQ // TM,),
                          in_specs=[_HALF, _ROW, _HALF, _Z_A, _Z_C, _ROW], out_specs=_full((2048, D)),
                          compiler_params=_cp())(o_a, yb, o_c, proj, proj, dy)


def _dwin(h, pieces, name):
    n = len(pieces)
    widths = [p.shape[1] for p in pieces]
    half = NP // 2

    def body(*refs):
        h_ref, p_refs, o_ref = refs[0], refs[1:1 + n], refs[1 + n]

        @pl.when(pl.program_id(0) == 0)
        def _():
            o_ref[...] = jnp.zeros_like(o_ref)
        hv, c0 = h_ref[...], 0
        for p_ref, wd in zip(p_refs, widths):
            o_ref[:, c0:c0 + wd] += _mm(hv, p_ref[...], TN)
            c0 += wd

    return pl.pallas_call(body, name=name, out_shape=_sds((D, half)), grid=(SEQ // TM,),
                          in_specs=[_ROW] + [pl.BlockSpec((TM, wd), lambda k: (k, 0)) for wd in widths],
                          out_specs=_full((D, half)), compiler_params=_cp(56))(h, *pieces)


_TMH = 256


def _dh_bwd(pieces, w, x, pre_w, scale, dxo, name):
    n = len(pieces)
    widths = [p.shape[1] for p in pieces]

    def body(*refs):
        p_refs, (w_ref, x_ref, pw_ref, sc_ref, dxo_ref, dx_ref, dsh_ref, dsc_ref, dpw_ref) = refs[:n], refs[n:]

        @pl.when(pl.program_id(0) == 0)
        def _():
            for r in (dsh_ref, dsc_ref, dpw_ref):
                r[...] = jnp.zeros_like(r)
        dh, c0 = 0.0, 0
        for p_ref, wd in zip(p_refs, widths):
            dh = dh + _mm(p_ref[...], w_ref[:, c0:c0 + wd], NT)
            c0 += wd
        xv = x_ref[...]
        rstd = _rstd(xv)
        nrm = xv * rstd
        dsh_ref[...] += jnp.sum(dh, axis=0, keepdims=True)
        dsc_ref[...] += jnp.sum(dh * (nrm * pw_ref[...]), axis=0, keepdims=True)
        dhn = dh * (1.0 + sc_ref[...])
        dpw_ref[...] += jnp.sum(dhn * nrm, axis=0, keepdims=True)
        dx_ref[...] = _rms_bwd(dhn * pw_ref[...], nrm, rstd) + dxo_ref[...]

    row = pl.BlockSpec((_TMH, D), lambda i: (i, 0))
    return pl.pallas_call(body, name=name, out_shape=(_sds((SEQ, D)), _sds((1, D)), _sds((1, D)), _sds((1, D))),
                          grid=(SEQ // _TMH,),
                          in_specs=[pl.BlockSpec((_TMH, wd), lambda i: (i, 0)) for wd in widths]
                          + [pl.BlockSpec((None, D, NP), lambda i: (0, 0, 0)), row, _VEC, _VEC, row],
                          out_specs=(row, _VEC, _VEC, _VEC), compiler_params=_cp(56))(*pieces, w, x, pre_w, scale, dxo)


def _w_in_padded(land, name):
    rows = 128

    def body(l_ref, o_ref):
        o_ref[...] = _pad_cols(jnp.concatenate([l_ref[k] for k in range(4)], axis=1))

    return pl.pallas_call(body, name=name, out_shape=_sds((D, NP), land.dtype), grid=(D // rows,),
                          in_specs=[pl.BlockSpec((4, rows, SHARD_IN), lambda i: (0, i, 0))],
                          out_specs=pl.BlockSpec((rows, NP), lambda i: (i, 0)), compiler_params=_cp())(land)


def _grad_blocks(dwa, dwb, name):
    rows = 128

    def body(a_ref, b_ref, o_ref):
        g = _unpad_cols(jnp.concatenate([a_ref[...], b_ref[...]], axis=1))
        for k in range(4):
            o_ref[k] = g[:, SHARD_IN * k:SHARD_IN * (k + 1)].astype(o_ref.dtype)

    half = pl.BlockSpec((rows, NP // 2), lambda i: (i, 0))
    return pl.pallas_call(body, name=name, out_shape=_sds((4, D, SHARD_IN), jnp.bfloat16), grid=(D // rows,),
                          in_specs=[half, half], out_specs=pl.BlockSpec((4, rows, SHARD_IN), lambda i: (0, i, 0)),
                          compiler_params=_cp())(dwa, dwb)


def _loss_bwd(xf, tgt, name):
    def body(x_ref, t_ref, dx_ref, l_ref):
        @pl.when(pl.program_id(0) == 0)
        def _():
            l_ref[...] = jnp.zeros_like(l_ref)
        e = x_ref[...] - t_ref[...]
        dx_ref[...] = e * (1.0 / D)
        l_ref[...] += 0.5 * jnp.sum(jnp.mean(e * e, axis=1, keepdims=True), axis=0, keepdims=True)

    return pl.pallas_call(body, name=name, out_shape=(_sds((SEQ, D)), _sds((8, LANES))), grid=(SEQ // TM,),
                          in_specs=[_ROW, _ROW], out_specs=(_ROW, _full((8, LANES))), compiler_params=_cp())(xf, tgt)


def _mod_part(c_all, ada_w, ada_b, name):
    def body(c_ref, w_ref, b_ref, o_ref):
        o_ref[0] = _mm(_silu(c_ref[...]), w_ref[0]) + b_ref[0]

    return pl.pallas_call(body, name=name, out_shape=_sds((DEPTH, 8, 768)), grid=(DEPTH,),
                          in_specs=[_full((8, D)), pl.BlockSpec((1, D, 768), lambda i: (i, 0, 0)), pl.BlockSpec((1, 1, 768), lambda i: (i, 0, 0))],
                          out_specs=pl.BlockSpec((1, 8, 768), lambda i: (i, 0, 0)), compiler_params=_cp())(c_all, ada_w, ada_b)


def _ada_grad(c_t, dmod, name):
    def body(c_ref, d_ref, o_ref):
        ca = _silu(c_ref[...])
        dm = d_ref[0]
        acc = ca[:, 0:1] * dm[0:1, :]
        for s in range(1, 8):
            acc = acc + ca[:, s:s + 1] * dm[s:s + 1, :]
        o_ref[0] = acc

    return pl.pallas_call(body, name=name, out_shape=_sds((DEPTH, D, 768)), grid=(DEPTH,),
                          in_specs=[_full((D, LANES)), pl.BlockSpec((1, 8, 768), lambda i: (i, 0, 0))],
                          out_specs=pl.BlockSpec((1, D, 768), lambda i: (i, 0, 0)), compiler_params=_cp())(c_t, dmod)


def _pack(parts):
    flat = []
    for p in parts:
        f = p.reshape(-1)
        flat.append(jnp.pad(f, (0, (-f.size) % LANES)))
    v = jnp.concatenate(flat)
    return jnp.pad(v, (0, (-v.size) % (8 * LANES))).reshape(-1, LANES)


def _unpack(v, shapes):
    v = v.reshape(-1)
    out, off = [], 0
    for s in shapes:
        n = math.prod(s)
        out.append(v[off:off + n].reshape(s))
        off += n + (-n) % LANES
    return out


_GIVEN_DT, _GIVEN_C = 4608, 4624


def _pad_cols(w):
    return jnp.concatenate([w[..., :_GIVEN_DT], w[..., _GIVEN_C:], w[..., _GIVEN_DT:_GIVEN_C],
                            jnp.zeros(w.shape[:-1] + (NP - IN_COLS,), w.dtype)], axis=-1)


def _unpad_cols(w):
    return jnp.concatenate([w[..., :_GIVEN_DT], w[..., DTC:DTC + 16], w[..., _GIVEN_DT:DTC]], axis=-1)


def _pad_lanes(v):
    return jnp.pad(v, (0, LANES - v.shape[0])).reshape(1, LANES)


def _local_step(x2, tgt, mod, weights_of, grads_done, pre_w, post_w, conv_w, conv_b, dt_bias, a_log, d_skip, nw, sinks):
    saved = []
    xcur = x2
    for i in range(DEPTH):
        shift, scale, gate = mod[i:i + 1, :D], mod[i:i + 1, D:2 * D], mod[i:i + 1, 2 * D:]
        pw, qw = pre_w[i:i + 1], post_w[i:i + 1]
        w_p, w_o = weights_of(i, xcur)
        proj, h = _proj_fwd(xcur, pw, scale, shift, w_p, 0, "proj_fwd")
        o_a, lse_a = _attn_fwd(proj, QA // LANES, KA // LANES, VA // LANES, DILS, False, None, "attn_a_fwd")
        sink_x = jnp.repeat(sinks[i], HD).reshape(1, 512)
        o_c, lse_c = _attn_fwd(proj, QC // LANES, KC // LANES, VC // LANES, (1,), True, sink_x, "attn_c_fwd")
        cw, cb = conv_w[i], conv_b[i:i + 1]
        xbc_act = _conv_fwd(proj, cw, cb, "conv_fwd")
        ssd_p = (_pad_lanes(a_log[i]), _pad_lanes(dt_bias[i]), jnp.repeat(d_skip[i], HD).reshape(1, 1024), nw[i:i + 1])
        yb, hin = _ssd_fwd(xbc_act, proj, *ssd_p, "ssd_fwd")
        xnew, y = _out_fwd(o_a, yb, o_c, proj, w_o, 0, xcur, gate, qw, "out_fwd")
        saved.append((w_p, w_o, xcur, scale, gate, pw, qw, proj, h, o_a, lse_a, sink_x, o_c, lse_c, cw, cb, xbc_act, ssd_p, yb, hin, y))
        xcur = xnew
    dx, ltile = _loss_bwd(xcur, tgt, "loss")
    dmod, small = [None] * DEPTH, [None] * DEPTH
    for i in reversed(range(DEPTH)):
        w_p, w_o, xin, scale, gate, pw, qw, proj, h, o_a, lse_a, sink_x, o_c, lse_c, cw, cb, xbc_act, ssd_p, yb, hin, y = saved[i]
        dy, dgate, dpost, do_a, dz_a, dyb, do_c, dz_c = _dymix(dx, y, gate, qw, w_o, 0, o_a, o_c, proj, "dymix")
        dwo = _dwout(o_a, yb, o_c, proj, dy, "dwout")
        dq_a, dk_a, dv_a = _attn_bwd(proj, QA // LANES, KA // LANES, VA // LANES, do_a, o_a, lse_a, DILS, False, None, "attn_a_bwd")
        dq_c, dk_c, dv_c, dsk = _attn_bwd(proj, QC // LANES, KC // LANES, VC // LANES, do_c, o_c, lse_c, (1,), True, sink_x, "attn_c_bwd")
        dxbc_act, dz_b, ddt, dal16, ddtb, ddsk, dnw = _ssd_bwd(xbc_act, proj, hin, dyb, *ssd_p, "ssd_bwd")
        dxbc, dcw, dcb = _conv_bwd(proj, dxbc_act, cw, cb, "conv_bwd")
        half_a, half_b = [dq_a, dk_a, dv_a, dz_a, dz_b], [dxbc, dq_c, dz_c, dk_c, dv_c, ddt]
        sent = grads_done(i, _dwin(h, half_a, "dwin_a"), _dwin(h, half_b, "dwin_b"), dwo)
        dx, dshift, dscale, dpre = _dh_bwd(half_a + half_b, w_p, xin, pw, scale + sent[0, 0], dx, "dh_bwd")
        dmod[i] = jnp.concatenate([dshift, dscale, dgate], axis=1)
        small[i] = (dpre, dpost, dcw, dcb, ddtb[0, :16], dal16[0, :16], ddsk.reshape(16, HD).sum(axis=1), dnw, dsk[:, 0, ::HD].reshape(8))
    return ltile, dx, jnp.concatenate(dmod, axis=0), small


_SMALL = ((1, D), (1, D), (4, CONV_CH), (1, CONV_CH), (16,), (16,), (16,), (1, D), (8,))


def kernel(x, c, ada_w, ada_b, pre_norm_w, post_norm_w, w_in, conv_w, conv_b, dt_bias, a_log, d_skip, ssm_norm_w, sinks, w_out, loss_target, m_ada_w, m_ada_b, m_pre_norm_w, m_post_norm_w, m_w_in, m_conv_w, m_conv_b, m_dt_bias, m_a_log, m_d_skip, m_ssm_norm_w, m_sinks, m_w_out, v_ada_w, v_ada_b, v_pre_norm_w, v_post_norm_w, v_w_in, v_conv_w, v_conv_b, v_dt_bias, v_a_log, v_d_skip, v_ssm_norm_w, v_sinks, v_w_out):
    xi, yi, ci = lax.axis_index("x"), lax.axis_index("y"), lax.axis_index("c")
    chip = 2 * xi + yi
    me = 2 * chip + ci

    w_in_b = _cast_bf16(w_in, 512, "cast_w_in")
    w_out_b = _cast_bf16(w_out, 512, "cast_w_out")
    gathers = []
    for i in range(DEPTH):
        lands = [lax.dynamic_update_slice(lax.empty((4,) + a.shape[1:], a.dtype), a[i][None], (chip, 0, 0)) for a in (w_in_b, w_out_b)]
        gathers.append(_split_start(None, lands, f"gather_start{i}", "half" if i == 0 else "whole"))
    all_started = gathers[0][3] + gathers[1][3] + gathers[2][3] + gathers[3][3]

    def weights_of(i, after):
        send_sems, recv_sems, thru, _ = gathers[i]
        if i == 0:
            halves = _split_wait(send_sems, recv_sems, thru, 2, all_started + mod[:1, :LANES], "gather_wait0", "half")
            send_sems, recv_sems, thru, after = _split_start(None, halves, "share_start0", "sibling")
            g_in, g_out = _split_wait(send_sems, recv_sems, thru, 2, after, "share_wait0", "sibling")
        else:
            g_in, g_out = _split_wait(send_sems, recv_sems, thru, 2, after, f"gather_wait{i}")
        return _w_in_padded(g_in, "w_in_padded")[None], g_out.reshape(1, 2048, D)

    scatters = [None] * DEPTH

    def grads_done(i, dwa, dwb, dwo):
        blocks = [_grad_blocks(dwa, dwb, "grad_blocks"), _cast_bf16(dwo.reshape(4, 512, D), 512, "cast_dw_out")]
        scatters[i] = _split_start(blocks, [lax.empty(b.shape, b.dtype) for b in blocks], f"scatter_start{i}")
        return scatters[i][3]

    g0 = _allgather8(_pack([c, conv_w]), "gather_c")
    c_all = g0[:, :8, :].reshape(8, D)
    conv_w_full = jnp.concatenate([g0[2 * k, 8:56, :].reshape(DEPTH, 4, CONV_CH // 4) for k in range(4)], axis=-1)

    ada_b_mine = lax.dynamic_slice_in_dim(ada_b, 768 * chip, 768, axis=1).reshape(DEPTH, 1, 768)
    gm = _allgather8(_mod_part(c_all, ada_w, ada_b_mine, "mod_part").reshape(DEPTH * 8, 768), "gather_mod")
    gm = gm.reshape(4, 2, DEPTH, 8, 768)[:, 0]
    mod = lax.dynamic_index_in_dim(gm, me, axis=2, keepdims=False).transpose(1, 0, 2).reshape(DEPTH, 3 * D)

    ltile, dx, dmod, small = _local_step(x[0], loss_target[0], mod, weights_of, grads_done, pre_norm_w, post_norm_w, conv_w_full,
                                         conv_b, dt_bias, a_log, d_skip, ssm_norm_w, sinks)

    packed = _pack([dmod] + [g for layer in small for g in layer] + [ltile[0]])
    gs = _allgather8(packed, "gather_small")
    tot = _sum_blocks(gs[:, None], packed.shape[0], "sum_small")[0]
    parts = _unpack(tot, [(DEPTH, 3 * D)] + list(_SMALL) * DEPTH + [(LANES,)])
    g_ada_b, loss = parts[0], parts[-1][0]
    per_layer = [parts[1 + len(_SMALL) * i:1 + len(_SMALL) * (i + 1)] for i in range(DEPTH)]
    g_pre, g_post, g_cw, g_cb, g_dtb, g_al, g_dsk, g_nw, g_sk = [jnp.stack([per_layer[i][j] for i in range(DEPTH)]) for j in range(len(_SMALL))]
    g_pre, g_post, g_cb, g_nw = g_pre[:, 0], g_post[:, 0], g_cb[:, 0], g_nw[:, 0]
    g_cw = lax.dynamic_slice_in_dim(g_cw, (CONV_CH // 4) * chip, CONV_CH // 4, axis=2)

    dmod_all = gs[:, :(DEPTH * 3 * D) // LANES, :].reshape(8, DEPTH, 3 * D).transpose(1, 0, 2)
    dmod_mine = lax.dynamic_slice_in_dim(dmod_all, 768 * chip, 768, axis=2)
    c_t = jnp.pad(c_all.T, ((0, 0), (0, LANES - 8)))
    g_ada_w = _ada_grad(c_t, dmod_mine, "ada_grad")

    res = {}
    res["ada_w"] = _adamw(ada_w, [g_ada_w], m_ada_w, v_ada_w, 512, "adamw_ada_w")
    names = ["ada_b", "pre_norm_w", "post_norm_w", "conv_w", "conv_b", "dt_bias", "a_log", "d_skip", "ssm_norm_w", "sinks"]
    ws = [ada_b, pre_norm_w, post_norm_w, conv_w, conv_b, dt_bias, a_log, d_skip, ssm_norm_w, sinks]
    gsm = [g_ada_b, g_pre, g_post, g_cw, g_cb, g_dtb, g_al, g_dsk, g_nw, g_sk]
    ms = [m_ada_b, m_pre_norm_w, m_post_norm_w, m_conv_w, m_conv_b, m_dt_bias, m_a_log, m_d_skip, m_ssm_norm_w, m_sinks]
    vs = [v_ada_b, v_pre_norm_w, v_post_norm_w, v_conv_w, v_conv_b, v_dt_bias, v_a_log, v_d_skip, v_ssm_norm_w, v_sinks]
    pw_, pg_, pm_, pv_ = _pack(ws), _pack(gsm), _pack(ms), _pack(vs)
    small_out = _adamw(pw_[None], [pg_[None]], pm_[None], pv_[None], pw_.shape[0], "adamw_small")

    others_done = small_out[1][0, :8] + res["ada_w"][1][0, :8, :LANES]
    landed = [_split_wait(*scatters[i][:3], 2, others_done, f"scatter_wait{i}") for i in range(DEPTH)]
    swap_out = _sibling_start(_sum_chips([d[3] for d in landed], [d[1] for d in landed], 256, "sum_w_out"), "swap_out_start")
    p_in = _sum_chips([d[2] for d in landed], [d[0] for d in landed], 128, "sum_w_in")
    col_major, row_major = (lambda a: jnp.transpose(a, (2, 0, 1))), (lambda a: jnp.transpose(a, (1, 2, 0)))
    swap_in = _sibling_start(col_major(p_in) + swap_out[4][0, 0], "swap_in_start")
    p_out, s_out = _sibling_wait(*swap_out[:4], swap_in[4], "swap_out_wait")
    res["w_out"] = _adamw(w_out, [p_out, s_out], m_w_out, v_w_out, 512, "adamw_w_out")
    p_in, s_in = _sibling_wait(*swap_in[:4], res["w_out"][1], "swap_in_wait")
    res["w_in"] = [row_major(a) for a in _adamw(col_major(w_in), [p_in, s_in], col_major(m_w_in), col_major(v_w_in), None,
                                                "adamw_w_in", lead=SHARD_IN // 18)]
    shapes = [w.shape for w in ws]
    for kind in range(4):
        for nm, a in zip(names, _unpack(small_out[kind][0], shapes)):
            res.setdefault(nm, [None] * 4)[kind] = a
    order = ["ada_w", "ada_b", "pre_norm_w", "post_norm_w", "w_in", "conv_w", "conv_b", "dt_bias", "a_log", "d_skip", "ssm_norm_w", "sinks", "w_out"]
    return (loss, dx[None], *[res[n][0] for n in order], *[res[n][1] for n in order], *[res[n][2] for n in order], *[res[n][3] for n in order])
```

```python
import math

import jax
import jax.numpy as jnp
from jax import lax
from jax.experimental import pallas as pl
from jax.experimental.pallas import tpu as pltpu

F32 = jnp.float32
MXU = jnp.bfloat16
HI = lax.Precision.HIGHEST
MESH = pl.DeviceIdType.MESH

SEQ = 4096
D = 1024
DEPTH = 4
HD = 64
QK_SCALE = HD ** -0.5
LANES = 128
BLK = 128
DILS = (1, 4, 16)
NEG = -1e30
EPS = 1e-6
MIB = 1024 * 1024

NP = 6144
QA, KA, VA, ZA = 0, 512, 1024, 1536
ZB, XBC = 2048, 3072
QC, ZC, KC, VC = 4608, 5120, 5632, 5760
DTC = 5888
IN_COLS = 5904
SHARD_IN = IN_COLS // 4
CONV_CH = 1536
TM = 512

ADAM_LR, ADAM_B1, ADAM_B2, ADAM_EPS, ADAM_WD, ADAM_STEP = 0.001, 0.9, 0.999, 1e-08, 0.01, 10

NT = (((1,), (1,)), ((), ()))
TN = (((0,), (0,)), ((), ()))


def _cp(vmem_mib=48):
    return pltpu.CompilerParams(vmem_limit_bytes=vmem_mib * MIB)


def _sds(shape, dtype=F32):
    return jax.ShapeDtypeStruct(shape, dtype)


def _full(shape):
    n = len(shape)
    return pl.BlockSpec(shape, lambda *_: (0,) * n)


def _mm(a, b, dims=None):
    if dims is None:
        return jnp.dot(a.astype(MXU), b.astype(MXU), preferred_element_type=F32)
    return lax.dot_general(a.astype(MXU), b.astype(MXU), dims, preferred_element_type=F32)


def _sigmoid(x):
    return 1.0 / (1.0 + jnp.exp(-x))


def _silu(x):
    return x * _sigmoid(x)


def _dsilu(x):
    s = _sigmoid(x)
    return s * (1.0 + x * (1.0 - s))


def _softplus(x):
    ax = jnp.where(x >= 0, x, -x)
    return jnp.maximum(x, 0.0) + jnp.log1p(jnp.exp(-ax))


def _half_masks():
    lane = lax.broadcasted_iota(jnp.int32, (1, LANES), 1)
    m0 = (lane < HD).astype(F32)
    return m0, 1.0 - m0


def _allgather8(v, name):
    r, cc = v.shape

    def body(v_ref, out_ref, send_sems, recv_sems):
        x, y, c = lax.axis_index("x"), lax.axis_index("y"), lax.axis_index("c")
        me = 4 * x + 2 * y + c
        out_ref[me] = v_ref[...]
        peers = []
        for k in range(1, 8):
            px = 1 - x if k & 4 else x
            py = 1 - y if k & 2 else y
            pc = 1 - c if k & 1 else c
            peers.append((px, py, pc))
        sends = []
        for k, peer in enumerate(peers):
            cp = pltpu.make_async_remote_copy(src_ref=v_ref, dst_ref=out_ref.at[me], send_sem=send_sems.at[k],
                                              recv_sem=recv_sems.at[k], device_id=peer, device_id_type=MESH)
            cp.start()
            sends.append(cp)
        for k, (px, py, pc) in enumerate(peers):
            pltpu.make_async_remote_copy(src_ref=v_ref, dst_ref=out_ref.at[4 * px + 2 * py + pc], send_sem=send_sems.at[k],
                                         recv_sem=recv_sems.at[k], device_id=(px, py, pc), device_id_type=MESH).wait_recv()
        for cp in sends:
            cp.wait_send()

    return pl.pallas_call(
        body, name=name, out_shape=_sds((8, r, cc)),
        in_specs=[pl.BlockSpec(memory_space=pltpu.VMEM)], out_specs=pl.BlockSpec(memory_space=pltpu.VMEM),
        scratch_shapes=[pltpu.SemaphoreType.DMA((7,)), pltpu.SemaphoreType.DMA((7,))],
        compiler_params=_cp(32),
    )(v)


_HBM = pl.BlockSpec(memory_space=pltpu.HBM)
_SEM = pl.BlockSpec(memory_space=pltpu.SEMAPHORE)
_EFFECT = pltpu.SideEffectType.DATAFLOW_SIDE_EFFECTING


def _chip_copies(src_refs, land_refs, send_sems, recv_sems, part="whole"):
    x, y, c = lax.axis_index("x"), lax.axis_index("y"), lax.axis_index("c")
    mine = 2 * x + y
    out = []
    for i, land in enumerate(land_refs):
        half = land.shape[1] // 2
        own, others = pl.ds(pl.multiple_of(c * half, half), half), pl.ds(pl.multiple_of((1 - c) * half, half), half)
        for j, (px, py) in enumerate([(1 - x, y), (x, 1 - y), (1 - x, 1 - y)]):
            slot, peer = 2 * px + py, (px, py, c)
            if part == "whole":
                src = src_refs[i].at[slot] if src_refs else land.at[mine]
                there, here = land.at[mine], land.at[slot]
            elif part == "half":
                src = there = land.at[mine].at[own]
                here = land.at[slot].at[own]
            else:
                src = there = land.at[slot].at[own]
                here, peer = land.at[slot].at[others], (x, y, 1 - c)
            mk = lambda dst, i=i, j=j, src=src, peer=peer: pltpu.make_async_remote_copy(
                src_ref=src, dst_ref=dst, send_sem=send_sems.at[3 * i + j], recv_sem=recv_sems.at[3 * i + j],
                device_id=peer, device_id_type=MESH)
            out.append((mk(there), mk(here)))
    return out


def _split_start(srcs, lands, name, part="whole"):
    ops = list(srcs or []) + list(lands)
    ns, n = len(srcs or []), len(lands)

    def body(*refs):
        src_refs, land_refs = refs[:ns], refs[ns:ns + n]
        send_sems, recv_sems = refs[ns + n], refs[ns + n + 1]
        for mine_out, _ in _chip_copies(src_refs, land_refs, send_sems, recv_sems, part):
            mine_out.start()
        refs[-1][...] = jnp.zeros_like(refs[-1])

    sems = pltpu.SemaphoreType.DMA((3 * n,))
    res = pl.pallas_call(
        body, name=name, out_shape=(sems, sems) + tuple(pltpu.HBM(a.shape, a.dtype) for a in ops) + (_sds((8, LANES)),),
        in_specs=[_HBM] * len(ops), out_specs=(_SEM, _SEM) + (_HBM,) * len(ops) + (pl.BlockSpec(memory_space=pltpu.VMEM),),
        input_output_aliases={k: 2 + k for k in range(len(ops))},
        compiler_params=pltpu.CompilerParams(has_side_effects=_EFFECT),
    )(*[pltpu.with_memory_space_constraint(a, pltpu.HBM) for a in ops])
    return res[0], res[1], list(res[2:2 + len(ops)]), res[-1]


def _split_wait(send_sems, recv_sems, thru, n, after, name, part="whole"):
    ns = len(thru) - n

    def body(*refs):
        src_refs, land_refs = refs[:ns], refs[ns:ns + n]
        for mine_out, arriving in _chip_copies(src_refs, land_refs, refs[ns + n], refs[ns + n + 1], part):
            mine_out.wait_send()
            arriving.wait_recv()

    res = pl.pallas_call(
        body, name=name, out_shape=tuple(pltpu.HBM(a.shape, a.dtype) for a in thru),
        in_specs=[_HBM] * len(thru) + [_SEM, _SEM, pl.BlockSpec(memory_space=pl.ANY)], out_specs=(_HBM,) * len(thru),
        input_output_aliases={k: k for k in range(len(thru))},
        compiler_params=pltpu.CompilerParams(has_side_effects=_EFFECT),
    )(*thru, send_sems, recv_sems, after)
    return list(res)


def _sibling_swap(arrs, name):
    n = len(arrs)

    def body(*refs):
        ins, outs_, (send_sems, recv_sems) = refs[:n], refs[n:2 * n], refs[2 * n:]
        sib = (lax.axis_index("x"), lax.axis_index("y"), 1 - lax.axis_index("c"))
        cps = [pltpu.make_async_remote_copy(src_ref=ins[i], dst_ref=outs_[i], send_sem=send_sems.at[i], recv_sem=recv_sems.at[i],
                                            device_id=sib, device_id_type=MESH) for i in range(n)]
        for cp in cps:
            cp.start()
        for cp in cps:
            cp.wait_recv()
        for cp in cps:
            cp.wait_send()

    hbm = pl.BlockSpec(memory_space=pltpu.HBM)
    return pl.pallas_call(
        body, name=name, out_shape=tuple(_sds(a.shape, a.dtype) for a in arrs), in_specs=[hbm] * n, out_specs=tuple([hbm] * n),
        scratch_shapes=[pltpu.SemaphoreType.DMA((n,)), pltpu.SemaphoreType.DMA((n,))],
    )(*arrs)


def _tile_spec(rows, cc):
    return pl.BlockSpec((None, rows, cc), lambda l, i: (l, i, 0))


def _cast_bf16(a, rows, name):
    nl, r, cc = a.shape

    def body(a_ref, o_ref):
        o_ref[...] = a_ref[...].astype(jnp.bfloat16)

    return pl.pallas_call(body, name=name, out_shape=_sds((nl, r, cc), jnp.bfloat16), grid=(nl, r // rows),
                          in_specs=[_tile_spec(rows, cc)], out_specs=_tile_spec(rows, cc), compiler_params=_cp())(a)


def _sum_blocks(a, rows, name):
    k, nl, r, cc = a.shape

    def body(a_ref, o_ref):
        acc = a_ref[0].astype(F32)
        for j in range(1, k):
            acc = acc + a_ref[j].astype(F32)
        o_ref[...] = acc

    return pl.pallas_call(body, name=name, out_shape=_sds((nl, r, cc)), grid=(nl, r // rows),
                          in_specs=[pl.BlockSpec((k, None, rows, cc), lambda l, i: (0, l, i, 0))],
                          out_specs=_tile_spec(rows, cc), compiler_params=_cp())(a)


def _sum_chips(lands, srcs, rows, name):
    nl = len(lands)
    _, r, cc = lands[0].shape

    def body(*refs):
        land_refs, src_refs, o_ref = refs[:nl], refs[nl:2 * nl], refs[2 * nl]
        mine = 2 * lax.axis_index("x") + lax.axis_index("y")
        for j in range(nl):
            @pl.when(pl.program_id(0) == j)
            def _(j=j):
                own = src_refs[j][mine].astype(F32)
                acc = None
                for k in range(4):
                    term = jnp.where(mine == k, own, land_refs[j][k].astype(F32))
                    acc = term if acc is None else acc + term
                o_ref[...] = acc

    specs = [pl.BlockSpec((4, rows, cc), lambda l, i, j=j: (0, jnp.where(l == j, i, 0), 0)) for j in range(nl)]
    return pl.pallas_call(body, name=name, out_shape=_sds((nl, r, cc)), grid=(nl, r // rows),
                          in_specs=specs + specs, out_specs=_tile_spec(rows, cc), compiler_params=_cp())(*lands, *srcs)


def _adamw(w, parts, m, v, rows, name, lead=None):
    nl, r, cc = w.shape
    np_ = len(parts)
    c1 = 1.0 / (1.0 - ADAM_B1 ** ADAM_STEP)
    c2 = 1.0 / (1.0 - ADAM_B2 ** ADAM_STEP)

    def body(*refs):
        w_ref, p_refs, (m_ref, v_ref, g_ref, d_ref, nm_ref, nv_ref) = refs[0], refs[1:1 + np_], refs[1 + np_:]
        g = p_refs[0][...]
        for p_ref in p_refs[1:]:
            g = g + p_ref[...]
        nm = ADAM_B1 * m_ref[...] + (1.0 - ADAM_B1) * g
        nv = ADAM_B2 * v_ref[...] + (1.0 - ADAM_B2) * (g * g)
        g_ref[...] = g
        nm_ref[...] = nm
        nv_ref[...] = nv
        d_ref[...] = -ADAM_LR * ((nm * c1) / (jnp.sqrt(nv * c2) + ADAM_EPS) + ADAM_WD * w_ref[...])

    if lead is None:
        spec, grid = _tile_spec(rows, cc), (nl, r // rows)
    else:
        spec, grid = pl.BlockSpec((lead, r, cc), lambda i: (i, 0, 0)), (nl // lead,)
    return pl.pallas_call(body, name=name, out_shape=(_sds((nl, r, cc)),) * 4, grid=grid,
                          in_specs=[spec] * (3 + np_), out_specs=(spec,) * 4, compiler_params=_cp())(w, *parts, m, v)


_BIAS = pltpu.VMEM((2, 2 * BLK, 2 * BLK), F32)


def _fill_band_bias(bias_ref):
    qi = lax.broadcasted_iota(jnp.int32, (2 * BLK, 2 * BLK), 0) & (BLK - 1)
    kj = lax.broadcasted_iota(jnp.int32, (2 * BLK, 2 * BLK), 1)
    dist = BLK + qi - kj
    band = (dist >= 0) & (dist <= BLK)
    bias_ref[0] = jnp.where(band, 0.0, NEG)
    bias_ref[1] = jnp.where(band & (kj >= BLK), 0.0, NEG)


class _HeadStack:
    def __init__(self, group):
        self.m0, self.m1 = _half_masks()
        self.group = group
        if group is not None:
            self.kv_mask = (self.m0, self.m1)[group]

    def _swap_half(self, t, a):
        return t if a == self.group else pltpu.roll(t, HD, axis=1)

    def stack(self, t):
        t0, t1 = t * self.m0, t * self.m1
        if self.group is not None:
            t0, t1 = self._swap_half(t0, 0), self._swap_half(t1, 1)
        return jnp.concatenate([t0, t1], axis=0)

    def unstack(self, ts):
        if self.group is None:
            return ts[:BLK] * self.m0 + ts[BLK:] * self.m1
        return self._swap_half(ts[:BLK] * self.kv_mask, 0) + self._swap_half(ts[BLK:] * self.kv_mask, 1)


def _rows(st, dil):
    if dil == 1:
        return pl.ds(pl.multiple_of(st, BLK), BLK)
    return pl.ds(st, BLK, stride=dil)


def _block_pos(n, dil):
    nb = SEQ // (dil * BLK)
    r, b = n // nb, n % nb
    hp = (b > 0).astype(jnp.int32)
    st = r + dil * BLK * b
    return st, st - dil * BLK * hp, 1 - hp


def _attn_fwd(proj, qblk, kblk, vblk, dils, gqa, sink_x, name):
    has_sink = sink_x is not None

    def body(*refs):
        if has_sink:
            q_ref, k_ref, v_ref, s_ref, o_ref, lse_ref, m_scr, z_scr, bias_scr = refs
        else:
            q_ref, k_ref, v_ref, o_ref, lse_ref, m_scr, z_scr, bias_scr = refs

        @pl.when(pl.program_id(0) == 0)
        def _():
            _fill_band_bias(bias_scr)
        o_ref[...] = jnp.zeros_like(o_ref)
        if has_sink:
            z_scr[...] = jnp.ones_like(z_scr)
            m_scr[...] = jnp.broadcast_to(s_ref[...], m_scr.shape)
        else:
            z_scr[...] = jnp.zeros_like(z_scr)
            m_scr[...] = jnp.full_like(m_scr, NEG)

        def step(n, carry, dil, heads):
            m0, m1 = heads.m0, heads.m1
            st, stp, first = _block_pos(n, dil)
            rq, rp = _rows(st, dil), _rows(stp, dil)
            kk = jnp.concatenate([k_ref[rp, :], k_ref[rq, :]], axis=0)
            vv = jnp.concatenate([v_ref[rp, :], v_ref[rq, :]], axis=0)
            s = _mm(heads.stack(q_ref[rq, :] * QK_SCALE), kk, NT) + bias_scr[first]
            m = jnp.max(s, axis=1, keepdims=True)
            p = jnp.exp(s - m)
            l = jnp.sum(p, axis=1, keepdims=True)
            o_pair = heads.unstack(_mm(p, vv))
            m_pair = m[:BLK] * m0 + m[BLK:] * m1
            l_pair = l[:BLK] * m0 + l[BLK:] * m1
            m_old = m_scr[rq, :]
            m_new = jnp.maximum(m_old, m_pair)
            alpha, beta = jnp.exp(m_old - m_new), jnp.exp(m_pair - m_new)
            o_ref[rq, :] = o_ref[rq, :] * alpha + o_pair * beta
            z_scr[rq, :] = z_scr[rq, :] * alpha + l_pair * beta
            m_scr[rq, :] = m_new
            return carry

        def blocks(heads):
            for dil in dils:
                lax.fori_loop(0, SEQ // BLK, lambda n, carry, dil=dil: step(n, carry, dil, heads), 0, unroll=8)

        if gqa:
            for grp in range(2):
                pl.when(pl.program_id(0) // 2 == grp)(lambda grp=grp: blocks(_HeadStack(grp)))
        else:
            blocks(_HeadStack(None))

        def fin(t, carry):
            rt = pl.ds(pl.multiple_of(t * TM, TM), TM)
            z = z_scr[rt, :]
            o_ref[rt, :] = o_ref[rt, :] / z
            lse_ref[rt, :] = m_scr[rt, :] + jnp.log(z)
            return carry
        lax.fori_loop(0, SEQ // TM, fin, 0)

    col = lambda blk: pl.BlockSpec((SEQ, LANES), lambda p, blk=blk: (0, blk + p))
    kv = (lambda blk: pl.BlockSpec((SEQ, LANES), lambda p, blk=blk: (0, blk))) if gqa else col
    in_specs = [col(qblk), kv(kblk), kv(vblk)]
    args = [proj, proj, proj]
    if has_sink:
        in_specs.append(pl.BlockSpec((1, LANES), lambda p: (0, p)))
        args.append(sink_x)
    out = pl.BlockSpec((SEQ, LANES), lambda p: (0, p))
    return pl.pallas_call(body, name=name, out_shape=(_sds((SEQ, 512)), _sds((SEQ, 512))), grid=(4,),
                          in_specs=in_specs, out_specs=(out, out),
                          scratch_shapes=[pltpu.VMEM((SEQ, LANES), F32), pltpu.VMEM((SEQ, LANES), F32), _BIAS],
                          compiler_params=_cp(48))(*args)


def _attn_bwd(proj, qblk, kblk, vblk, do, o, lse, dils, gqa, sink_x, name):
    has_sink = sink_x is not None

    def body(*refs):
        if has_sink:
            q_ref, k_ref, v_ref, do_ref, o_ref, lse_ref, s_ref, dq_ref, dk_ref, dv_ref, ds_ref, bias_scr = refs
        else:
            q_ref, k_ref, v_ref, do_ref, o_ref, lse_ref, dq_ref, dk_ref, dv_ref, bias_scr = refs
        pid = pl.program_id(0)

        @pl.when(pid == 0)
        def _():
            _fill_band_bias(bias_scr)
        dq_ref[...] = jnp.zeros_like(dq_ref)
        if gqa:
            @pl.when(pid == 0)
            def _():
                dk_ref[...] = jnp.zeros_like(dk_ref)
                dv_ref[...] = jnp.zeros_like(dv_ref)
        else:
            dk_ref[...] = jnp.zeros_like(dk_ref)
            dv_ref[...] = jnp.zeros_like(dv_ref)

        def step(n, carry, dil, heads):
            m0, m1 = heads.m0, heads.m1
            st, stp, first = _block_pos(n, dil)
            rq, rp = _rows(st, dil), _rows(stp, dil)
            do_, lse_ = do_ref[rq, :], lse_ref[rq, :]
            kk = jnp.concatenate([k_ref[rp, :], k_ref[rq, :]], axis=0)
            vv = jnp.concatenate([v_ref[rp, :], v_ref[rq, :]], axis=0)
            qs, dos = heads.stack(q_ref[rq, :] * QK_SCALE), heads.stack(do_)
            doo = do_ * o_ref[rq, :]
            delta = jnp.concatenate([jnp.sum(doo * m0, axis=1, keepdims=True), jnp.sum(doo * m1, axis=1, keepdims=True)], axis=0)
            lse_s = jnp.concatenate([lse_[:, 0:1], lse_[:, HD:HD + 1]], axis=0)
            p = jnp.exp(_mm(qs, kk, NT) + bias_scr[first] - lse_s)
            ds = p * (_mm(dos, vv, NT) - delta)
            dq_ref[rq, :] += heads.unstack(_mm(ds, kk)) * QK_SCALE
            dk_sum, dv_sum = _mm(ds, qs, TN), _mm(p, dos, TN)
            dk_ref[rp, :] += dk_sum[:BLK]
            dk_ref[rq, :] += dk_sum[BLK:]
            dv_ref[rp, :] += dv_sum[:BLK]
            dv_ref[rq, :] += dv_sum[BLK:]
            return carry

        def blocks(heads):
            for dil in dils:
                lax.fori_loop(0, SEQ // BLK, lambda n, carry, dil=dil: step(n, carry, dil, heads), 0, unroll=4)

        if gqa:
            for grp in range(2):
                pl.when(pid // 2 == grp)(lambda grp=grp: blocks(_HeadStack(grp)))
        else:
            blocks(_HeadStack(None))

        if has_sink:
            m0, m1 = _half_masks()

            def sink_rows(t, acc):
                rt = pl.ds(pl.multiple_of(t * TM, TM), TM)
                return acc - jnp.sum(jnp.exp(s_ref[...] - lse_ref[rt, :]) * (do_ref[rt, :] * o_ref[rt, :]), axis=0, keepdims=True)
            acc = lax.fori_loop(0, SEQ // TM, sink_rows, jnp.zeros((1, LANES), F32))
            per_head = jnp.sum(acc * m0, axis=1, keepdims=True) * m0 + jnp.sum(acc * m1, axis=1, keepdims=True) * m1
            ds_ref[0] = jnp.broadcast_to(per_head, (8, LANES))

    col = lambda blk: pl.BlockSpec((SEQ, LANES), lambda p, blk=blk: (0, blk + p))
    kv = (lambda blk: pl.BlockSpec((SEQ, LANES), lambda p, blk=blk: (0, blk))) if gqa else col
    pair = pl.BlockSpec((SEQ, LANES), lambda p: (0, p))
    in_specs = [col(qblk), kv(kblk), kv(vblk), pair, pair, pair]
    args = [proj, proj, proj, do, o, lse]
    kvw = LANES if gqa else 512
    kv_out = pl.BlockSpec((SEQ, LANES), lambda p: (0, 0)) if gqa else pair
    out_shape = [_sds((SEQ, 512)), _sds((SEQ, kvw)), _sds((SEQ, kvw))]
    out_specs = [pair, kv_out, kv_out]
    if has_sink:
        in_specs.append(pl.BlockSpec((1, LANES), lambda p: (0, p)))
        args.append(sink_x)
        out_shape.append(_sds((4, 8, LANES)))
        out_specs.append(pl.BlockSpec((1, 8, LANES), lambda p: (p, 0, 0)))
    return pl.pallas_call(body, name=name, out_shape=tuple(out_shape), grid=(4,), in_specs=in_specs,
                          out_specs=tuple(out_specs), scratch_shapes=[_BIAS], compiler_params=_cp(56))(*args)


_CT = 128


def _rows_before(x_ref, t, k):
    if t == 0:
        return jnp.concatenate([jnp.zeros((k, LANES), F32), x_ref[0:_CT - k, :]], axis=0)
    return x_ref[t * _CT - k:(t + 1) * _CT - k, :]


def _conv_pre(x_ref, w_ref, b_ref, t):
    taps = [x_ref[t * _CT:(t + 1) * _CT, :]] + [_rows_before(x_ref, t, k) for k in range(1, 4)]
    u = b_ref[...] + taps[0] * w_ref[3:4, :]
    for k in range(1, 4):
        u = u + taps[k] * w_ref[3 - k:4 - k, :]
    return u, taps


def _conv_fwd(proj, w, b, name):
    def body(x_ref, w_ref, b_ref, o_ref):
        for t in range(SEQ // _CT):
            o_ref[t * _CT:(t + 1) * _CT, :] = _silu(_conv_pre(x_ref, w_ref, b_ref, t)[0])

    nblk = CONV_CH // LANES
    return pl.pallas_call(body, name=name, out_shape=_sds((SEQ, CONV_CH)), grid=(nblk,),
                          in_specs=[pl.BlockSpec((SEQ, LANES), lambda j: (0, XBC // LANES + j)),
                                    pl.BlockSpec((4, LANES), lambda j: (0, j)), pl.BlockSpec((1, LANES), lambda j: (0, j))],
                          out_specs=pl.BlockSpec((SEQ, LANES), lambda j: (0, j)), compiler_params=_cp())(proj, w, b)


def _conv_bwd(proj, dact, w, b, name):
    def body(x_ref, da_ref, w_ref, b_ref, dx_ref, dw_ref, db_ref, du_scr):
        du_scr[SEQ:SEQ + 8, :] = jnp.zeros((8, LANES), F32)
        db = jnp.zeros((1, LANES), F32)
        dws = [jnp.zeros((1, LANES), F32)] * 4
        for t in range(SEQ // _CT):
            u, taps = _conv_pre(x_ref, w_ref, b_ref, t)
            du = da_ref[t * _CT:(t + 1) * _CT, :] * _dsilu(u)
            du_scr[t * _CT:(t + 1) * _CT, :] = du
            db = db + jnp.sum(du, axis=0, keepdims=True)
            dws = [dws[k] + jnp.sum(du * taps[k], axis=0, keepdims=True) for k in range(4)]
        db_ref[...] = db
        for k in range(4):
            dw_ref[3 - k:4 - k, :] = dws[k]
        for t in range(SEQ // _CT):
            dx = du_scr[t * _CT:(t + 1) * _CT, :] * w_ref[3:4, :]
            for k in range(1, 4):
                dx = dx + du_scr[t * _CT + k:(t + 1) * _CT + k, :] * w_ref[3 - k:4 - k, :]
            dx_ref[t * _CT:(t + 1) * _CT, :] = dx.astype(dx_ref.dtype)

    nblk = CONV_CH // LANES
    blk = pl.BlockSpec((SEQ, LANES), lambda j: (0, j))
    wspec, bspec = pl.BlockSpec((4, LANES), lambda j: (0, j)), pl.BlockSpec((1, LANES), lambda j: (0, j))
    return pl.pallas_call(body, name=name, out_shape=(_sds((SEQ, CONV_CH), MXU), _sds((4, CONV_CH)), _sds((1, CONV_CH))), grid=(nblk,),
                          in_specs=[pl.BlockSpec((SEQ, LANES), lambda j: (0, XBC // LANES + j)), blk, wspec, bspec],
                          out_specs=(blk, wspec, bspec), scratch_shapes=[pltpu.VMEM((SEQ + 8, LANES), F32)],
                          compiler_params=_cp())(proj, dact, w, b)


def _ssd_chunk(xs, bm, cm, dtr, z, hs, al16, dtb, dskx, nw):
    m0, m1 = _half_masks()
    row = lax.broadcasted_iota(jnp.int32, (BLK, BLK), 0)
    col = lax.broadcasted_iota(jnp.int32, (BLK, BLK), 1)
    causal = row >= col
    tril = causal.astype(F32)
    lane = lax.broadcasted_iota(jnp.int32, (1, LANES), 1)
    sub = lax.broadcasted_iota(jnp.int32, (BLK, 1), 0)
    last_row = (sub == BLK - 1).astype(F32)
    dt = jnp.where(lane < 16, _softplus(dtr + dtb), 0.0)
    a16 = -jnp.exp(al16)
    acum = jnp.dot(tril, dt * a16, precision=HI, preferred_element_type=F32)
    acum_t = acum.T
    gmat = [_mm(cm[g], bm[g], NT) for g in range(2)]
    ys, hn = [], []
    for p in range(8):
        g = p // 4
        pick = [(lane == 2 * p + a).astype(F32) for a in range(2)]
        col_h = [jnp.sum(acum * pick[a], axis=1, keepdims=True) for a in range(2)]
        dt_x = sum(jnp.sum(dt * pick[a], axis=1, keepdims=True) * msk for a, msk in enumerate((m0, m1)))
        ac_x = col_h[0] * m0 + col_h[1] * m1
        a_end = jnp.sum(ac_x * last_row, axis=0, keepdims=True)
        xdt = xs[p] * dt_x
        y = _mm(cm[g], hs[p]) * jnp.exp(ac_x)
        for a, msk in enumerate((m0, m1)):
            row_h = jnp.sum(acum_t * (sub == 2 * p + a).astype(F32), axis=0, keepdims=True)
            decay = jnp.exp(jnp.where(causal, col_h[a] - row_h, NEG))
            y = y + _mm(gmat[g] * decay, xdt * msk)
        st = _mm(bm[g], xdt * jnp.exp(a_end - ac_x), TN)
        hn.append(hs[p] * jnp.exp(a_end) + st)
        y = y + dskx[p] * xs[p]
        ys.append(y * _silu(z[p]))
    out = []
    for g in range(2):
        ms = sum(jnp.sum(ys[p] * ys[p], axis=1, keepdims=True) for p in range(4 * g, 4 * g + 4)) * (1.0 / 512)
        rstd = lax.rsqrt(ms + EPS)
        out += [ys[p] * rstd * nw[p] for p in range(4 * g, 4 * g + 4)]
    return out, hn


def _tiles(ref, n, off=0, rows=slice(None)):
    return [ref[rows, off + LANES * p:off + LANES * (p + 1)] for p in range(n)]


def _ssd_load(xbc_ref, z_ref, dt_ref, rows):
    return (_tiles(xbc_ref, 8, 0, rows), _tiles(xbc_ref, 2, 1024, rows), _tiles(xbc_ref, 2, 1280, rows), dt_ref[rows, :],
            _tiles(z_ref, 8, 0, rows))


def _ssd_params(al16_ref, dtb_ref, dsk_ref, nw_ref):
    return al16_ref[...], dtb_ref[...], _tiles(dsk_ref, 8), _tiles(nw_ref, 8)


_NCH = SEQ // BLK
_PER_STEP = 2
_STEP_ROWS = _PER_STEP * BLK


def _ssd_param_specs():
    return [_full((1, LANES)), _full((1, LANES)), _full((1, 1024)), _full((1, 1024))]


def _ssd_fwd(xbc_act, proj, al16, dtb, dskx, nw, name):
    def body(xbc_ref, z_ref, dt_ref, al16_ref, dtb_ref, dsk_ref, nw_ref, y_ref, hin_ref, h_scr):
        @pl.when(pl.program_id(0) == 0)
        def _():
            h_scr[...] = jnp.zeros_like(h_scr)
        params = _ssd_params(al16_ref, dtb_ref, dsk_ref, nw_ref)
        hs = _tiles(h_scr, 8)
        for k in range(_PER_STEP):
            rows = slice(BLK * k, BLK * (k + 1))
            for p in range(8):
                hin_ref[k, :, LANES * p:LANES * (p + 1)] = hs[p]
            ys, hs = _ssd_chunk(*_ssd_load(xbc_ref, z_ref, dt_ref, rows), hs, *params)
            for p in range(8):
                y_ref[rows, LANES * p:LANES * (p + 1)] = ys[p].astype(y_ref.dtype)
        for p in range(8):
            h_scr[:, LANES * p:LANES * (p + 1)] = hs[p]

    return pl.pallas_call(
        body, name=name, out_shape=(_sds((SEQ, 1024), MXU), _sds((_NCH, BLK, 1024))), grid=(_NCH // _PER_STEP,),
        in_specs=[pl.BlockSpec((_STEP_ROWS, CONV_CH), lambda c: (c, 0)), pl.BlockSpec((_STEP_ROWS, 1024), lambda c: (c, ZB // 1024)),
                  pl.BlockSpec((_STEP_ROWS, LANES), lambda c: (c, DTC // LANES))] + _ssd_param_specs(),
        out_specs=(pl.BlockSpec((_STEP_ROWS, 1024), lambda c: (c, 0)), pl.BlockSpec((_PER_STEP, BLK, 1024), lambda c: (c, 0, 0))),
        scratch_shapes=[pltpu.VMEM((BLK, 1024), F32)], compiler_params=_cp())(xbc_act, proj, proj, al16, dtb, dskx, nw)


def _ssd_bwd(xbc_act, proj, hin, dyb, al16, dtb, dskx, nw, name):
    def body(xbc_ref, z_ref, dt_ref, hin_ref, dy_ref, al16_ref, dtb_ref, dsk_ref, nw_ref,
             dxbc_ref, dz_ref, ddt_ref, dal16_ref, ddtb_ref, ddsk_ref, dnw_ref, dh_scr):
        @pl.when(pl.program_id(0) == 0)
        def _():
            dh_scr[...] = jnp.zeros_like(dh_scr)
            for r in (dal16_ref, ddtb_ref, ddsk_ref, dnw_ref):
                r[...] = jnp.zeros_like(r)
        params = _ssd_params(al16_ref, dtb_ref, dsk_ref, nw_ref)
        dhs = _tiles(dh_scr, 8)
        for k in reversed(range(_PER_STEP)):
            rows = slice(BLK * k, BLK * (k + 1))
            hs = [hin_ref[k, :, LANES * p:LANES * (p + 1)] for p in range(8)]
            _, vjp = jax.vjp(lambda a, h, q: _ssd_chunk(*a, h, *q), _ssd_load(xbc_ref, z_ref, dt_ref, rows), hs, params)
            (dxs, dbm, dcm, ddt, dz), dhs, (dal16, ddtb, ddsk, dnw) = vjp((_tiles(dy_ref, 8, 0, rows), dhs))
            for p in range(8):
                cols = slice(LANES * p, LANES * (p + 1))
                dxbc_ref[rows, cols] = dxs[p]
                dz_ref[rows, cols] = dz[p].astype(dz_ref.dtype)
                ddsk_ref[:, cols] += ddsk[p]
                dnw_ref[:, cols] += dnw[p]
            for g in range(2):
                dxbc_ref[rows, 1024 + LANES * g:1024 + LANES * (g + 1)] = dbm[g]
                dxbc_ref[rows, 1280 + LANES * g:1280 + LANES * (g + 1)] = dcm[g]
            ddt_ref[rows, :] = ddt.astype(ddt_ref.dtype)
            dal16_ref[...] += dal16
            ddtb_ref[...] += ddtb
        for p in range(8):
            dh_scr[:, LANES * p:LANES * (p + 1)] = dhs[p]

    rev = lambda c: _NCH // _PER_STEP - 1 - c
    return pl.pallas_call(
        body, name=name,
        out_shape=(_sds((SEQ, CONV_CH)), _sds((SEQ, 1024), MXU), _sds((SEQ, LANES), MXU),
                   _sds((1, LANES)), _sds((1, LANES)), _sds((1, 1024)), _sds((1, 1024))),
        grid=(_NCH // _PER_STEP,),
        in_specs=[pl.BlockSpec((_STEP_ROWS, CONV_CH), lambda c: (rev(c), 0)), pl.BlockSpec((_STEP_ROWS, 1024), lambda c: (rev(c), ZB // 1024)),
                  pl.BlockSpec((_STEP_ROWS, LANES), lambda c: (rev(c), DTC // LANES)),
                  pl.BlockSpec((_PER_STEP, BLK, 1024), lambda c: (rev(c), 0, 0)),
                  pl.BlockSpec((_STEP_ROWS, 1024), lambda c: (rev(c), 0))] + _ssd_param_specs(),
        out_specs=(pl.BlockSpec((_STEP_ROWS, CONV_CH), lambda c: (rev(c), 0)), pl.BlockSpec((_STEP_ROWS, 1024), lambda c: (rev(c), 0)),
                   pl.BlockSpec((_STEP_ROWS, LANES), lambda c: (rev(c), 0)),
                   _full((1, LANES)), _full((1, LANES)), _full((1, 1024)), _full((1, 1024))),
        scratch_shapes=[pltpu.VMEM((BLK, 1024), F32)], compiler_params=_cp())(xbc_act, proj, proj, hin, dyb, al16, dtb, dskx, nw)


def _rstd(v):
    return lax.rsqrt(jnp.mean(v * v, axis=1, keepdims=True) + EPS)


def _rms_bwd(dn, n, rstd):
    return rstd * (dn - n * jnp.mean(dn * n, axis=1, keepdims=True))


_VEC = _full((1, D))


def _layer_spec(layer):
    return pl.BlockSpec((None, 2048, D), lambda *_: (layer, 0, 0))

_ROW = pl.BlockSpec((TM, D), lambda i, *_: (i, 0))


def _proj_fwd(x, pre_w, scale, shift, w, layer, name):
    tn, ni = 1024, SEQ // TM

    def body(x_ref, pw_ref, sc_ref, sh_ref, w_ref, o_ref, h_ref, h_scr):
        rows = pl.ds(pl.multiple_of(pl.program_id(1) * TM, TM), TM)

        @pl.when(pl.program_id(0) == 0)
        def _():
            xv = x_ref[...]
            h = ((xv * _rstd(xv) * pw_ref[...]) * (1.0 + sc_ref[...]) + sh_ref[...]).astype(h_ref.dtype)
            h_scr[rows, :] = h
            h_ref[...] = h
        o_ref[...] = jnp.dot(h_scr[rows, :], w_ref[...].astype(MXU), preferred_element_type=F32)

    first_pass = pl.BlockSpec((TM, D), lambda j, i: (jnp.where(j == 0, i, ni - 1), 0))
    return pl.pallas_call(body, name=name, out_shape=(_sds((SEQ, NP)), _sds((SEQ, D), MXU)), grid=(NP // tn, ni),
                          in_specs=[first_pass, _VEC, _VEC, _VEC, pl.BlockSpec((None, D, tn), lambda j, i: (layer, 0, j))],
                          out_specs=(pl.BlockSpec((TM, tn), lambda j, i: (i, j)), first_pass),
                          scratch_shapes=[pltpu.VMEM((SEQ, D), MXU)], compiler_params=_cp())(x, pre_w, scale, shift, w)


_HALF = pl.BlockSpec((TM, 512), lambda i: (i, 0))
_Z_A = pl.BlockSpec((TM, 512), lambda i: (i, ZA // 512))
_Z_C = pl.BlockSpec((TM, 512), lambda i: (i, ZC // 512))


def _out_fwd(o_a, yb, o_c, proj, w, layer, x, gate, post_w, name):
    def body(oa_ref, yb_ref, oc_ref, za_ref, zc_ref, w_ref, x_ref, g_ref, pw_ref, xn_ref, y_ref):
        y = (_mm(oa_ref[...] * _silu(za_ref[...]), w_ref[0:512, :]) + _mm(yb_ref[...], w_ref[512:1536, :])
             + _mm(oc_ref[...] * _silu(zc_ref[...]), w_ref[1536:2048, :]))
        y_ref[...] = y
        xn_ref[...] = x_ref[...] + g_ref[...] * (y * _rstd(y) * pw_ref[...])

    return pl.pallas_call(body, name=name, out_shape=(_sds((SEQ, D)), _sds((SEQ, D))), grid=(SEQ // TM,),
                          in_specs=[_HALF, _ROW, _HALF, _Z_A, _Z_C, _layer_spec(layer), _ROW, _VEC, _VEC],
                          out_specs=(_ROW, _ROW), compiler_params=_cp())(o_a, yb, o_c, proj, proj, w, x, gate, post_w)


def _dymix(dxo, y, gate, post_w, w, layer, o_a, o_c, proj, name):
    def body(dx_ref, y_ref, g_ref, pw_ref, w_ref, oa_ref, oc_ref, za_ref, zc_ref,
             dy_ref, dg_ref, dpw_ref, doa_ref, dza_ref, b_ref, doc_ref, dzc_ref):
        @pl.when(pl.program_id(0) == 0)
        def _():
            dg_ref[...] = jnp.zeros_like(dg_ref)
            dpw_ref[...] = jnp.zeros_like(dpw_ref)
        dx, yv = dx_ref[...], y_ref[...]
        rstd = _rstd(yv)
        n = yv * rstd
        dg_ref[...] += jnp.sum(dx * (n * pw_ref[...]), axis=0, keepdims=True)
        dr = dx * g_ref[...]
        dpw_ref[...] += jnp.sum(dr * n, axis=0, keepdims=True)
        dy = _rms_bwd(dr * pw_ref[...], n, rstd)
        dy_ref[...] = dy
        b_ref[...] = _mm(dy, w_ref[512:1536, :], NT)
        for rows, o_ref, z_ref, do_ref, dz_ref in ((slice(0, 512), oa_ref, za_ref, doa_ref, dza_ref),
                                                   (slice(1536, 2048), oc_ref, zc_ref, doc_ref, dzc_ref)):
            dyg, z = _mm(dy, w_ref[rows, :], NT), z_ref[...]
            do_ref[...] = dyg * _silu(z)
            dz_ref[...] = (dyg * o_ref[...] * _dsilu(z)).astype(dz_ref.dtype)

    return pl.pallas_call(body, name=name,
                          out_shape=(_sds((SEQ, D)), _sds((1, D)), _sds((1, D)),
                                     _sds((SEQ, 512)), _sds((SEQ, 512), MXU), _sds((SEQ, D)), _sds((SEQ, 512)), _sds((SEQ, 512), MXU)),
                          grid=(SEQ // TM,), in_specs=[_ROW, _ROW, _VEC, _VEC, _layer_spec(layer), _HALF, _HALF, _Z_A, _Z_C],
                          out_specs=(_ROW, _VEC, _VEC, _HALF, _HALF, _ROW, _HALF, _HALF),
                          compiler_params=_cp())(dxo, y, gate, post_w, w, o_a, o_c, proj, proj)


def _dwout(o_a, yb, o_c, proj, dy, name):
    def body(oa_ref, yb_ref, oc_ref, za_ref, zc_ref, dy_ref, o_ref):
        @pl.when(pl.program_id(0) == 0)
        def _():
            o_ref[...] = jnp.zeros_like(o_ref)
        dy = dy_ref[...]
        o_ref[0:512, :] += _mm(oa_ref[...] * _silu(za_ref[...]), dy, TN)
        o_ref[512:1536, :] += _mm(yb_ref[...], dy, TN)
        o_ref[1536:2048, :] += _mm(oc_ref[...] * _silu(zc_ref[...]), dy, TN)

    return pl.pallas_call(body, name=name, out_shape=_sds((2048, D)), grid=(SEQ // TM,),
                          in_specs=[_HALF, _ROW, _HALF, _Z_A, _Z_C, _ROW], out_specs=_full((2048, D)),
                          compiler_params=_cp())(o_a, yb, o_c, proj, proj, dy)


def _dwin(h, pieces, name):
    n = len(pieces)
    widths = [p.shape[1] for p in pieces]
    half = NP // 2

    def body(*refs):
        h_ref, p_refs, o_ref = refs[0], refs[1:1 + n], refs[1 + n]

        @pl.when(pl.program_id(0) == 0)
        def _():
            o_ref[...] = jnp.zeros_like(o_ref)
        hv, c0 = h_ref[...], 0
        for p_ref, wd in zip(p_refs, widths):
            o_ref[:, c0:c0 + wd] += _mm(hv, p_ref[...], TN)
            c0 += wd

    return pl.pallas_call(body, name=name, out_shape=_sds((D, half)), grid=(SEQ // TM,),
                          in_specs=[_ROW] + [pl.BlockSpec((TM, wd), lambda k: (k, 0)) for wd in widths],
                          out_specs=_full((D, half)), compiler_params=_cp(56))(h, *pieces)


_TMH = 256


def _dh_bwd(pieces, w, x, pre_w, scale, dxo, name):
    n = len(pieces)
    widths = [p.shape[1] for p in pieces]

    def body(*refs):
        p_refs, (w_ref, x_ref, pw_ref, sc_ref, dxo_ref, dx_ref, dsh_ref, dsc_ref, dpw_ref) = refs[:n], refs[n:]

        @pl.when(pl.program_id(0) == 0)
        def _():
            for r in (dsh_ref, dsc_ref, dpw_ref):
                r[...] = jnp.zeros_like(r)
        dh, c0 = 0.0, 0
        for p_ref, wd in zip(p_refs, widths):
            dh = dh + _mm(p_ref[...], w_ref[:, c0:c0 + wd], NT)
            c0 += wd
        xv = x_ref[...]
        rstd = _rstd(xv)
        nrm = xv * rstd
        dsh_ref[...] += jnp.sum(dh, axis=0, keepdims=True)
        dsc_ref[...] += jnp.sum(dh * (nrm * pw_ref[...]), axis=0, keepdims=True)
        dhn = dh * (1.0 + sc_ref[...])
        dpw_ref[...] += jnp.sum(dhn * nrm, axis=0, keepdims=True)
        dx_ref[...] = _rms_bwd(dhn * pw_ref[...], nrm, rstd) + dxo_ref[...]

    row = pl.BlockSpec((_TMH, D), lambda i: (i, 0))
    return pl.pallas_call(body, name=name, out_shape=(_sds((SEQ, D)), _sds((1, D)), _sds((1, D)), _sds((1, D))),
                          grid=(SEQ // _TMH,),
                          in_specs=[pl.BlockSpec((_TMH, wd), lambda i: (i, 0)) for wd in widths]
                          + [pl.BlockSpec((None, D, NP), lambda i: (0, 0, 0)), row, _VEC, _VEC, row],
                          out_specs=(row, _VEC, _VEC, _VEC), compiler_params=_cp(56))(*pieces, w, x, pre_w, scale, dxo)


def _w_in_padded(land, name):
    rows = 128

    def body(l_ref, o_ref):
        o_ref[...] = _pad_cols(jnp.concatenate([l_ref[k] for k in range(4)], axis=1))

    return pl.pallas_call(body, name=name, out_shape=_sds((D, NP), land.dtype), grid=(D // rows,),
                          in_specs=[pl.BlockSpec((4, rows, SHARD_IN), lambda i: (0, i, 0))],
                          out_specs=pl.BlockSpec((rows, NP), lambda i: (i, 0)), compiler_params=_cp())(land)


def _grad_blocks(dwa, dwb, name):
    rows = 128

    def body(a_ref, b_ref, o_ref):
        g = _unpad_cols(jnp.concatenate([a_ref[...], b_ref[...]], axis=1))
        for k in range(4):
            o_ref[k] = g[:, SHARD_IN * k:SHARD_IN * (k + 1)].astype(o_ref.dtype)

    half = pl.BlockSpec((rows, NP // 2), lambda i: (i, 0))
    return pl.pallas_call(body, name=name, out_shape=_sds((4, D, SHARD_IN), jnp.bfloat16), grid=(D // rows,),
                          in_specs=[half, half], out_specs=pl.BlockSpec((4, rows, SHARD_IN), lambda i: (0, i, 0)),
                          compiler_params=_cp())(dwa, dwb)


def _loss_bwd(xf, tgt, name):
    def body(x_ref, t_ref, dx_ref, l_ref):
        @pl.when(pl.program_id(0) == 0)
        def _():
            l_ref[...] = jnp.zeros_like(l_ref)
        e = x_ref[...] - t_ref[...]
        dx_ref[...] = e * (1.0 / D)
        l_ref[...] += 0.5 * jnp.sum(jnp.mean(e * e, axis=1, keepdims=True), axis=0, keepdims=True)

    return pl.pallas_call(body, name=name, out_shape=(_sds((SEQ, D)), _sds((8, LANES))), grid=(SEQ // TM,),
                          in_specs=[_ROW, _ROW], out_specs=(_ROW, _full((8, LANES))), compiler_params=_cp())(xf, tgt)


def _mod_part(c_all, ada_w, ada_b, name):
    def body(c_ref, w_ref, b_ref, o_ref):
        o_ref[0] = _mm(_silu(c_ref[...]), w_ref[0]) + b_ref[0]

    return pl.pallas_call(body, name=name, out_shape=_sds((DEPTH, 8, 768)), grid=(DEPTH,),
                          in_specs=[_full((8, D)), pl.BlockSpec((1, D, 768), lambda i: (i, 0, 0)), pl.BlockSpec((1, 1, 768), lambda i: (i, 0, 0))],
                          out_specs=pl.BlockSpec((1, 8, 768), lambda i: (i, 0, 0)), compiler_params=_cp())(c_all, ada_w, ada_b)


def _ada_grad(c_t, dmod, name):
    def body(c_ref, d_ref, o_ref):
        ca = _silu(c_ref[...])
        dm = d_ref[0]
        acc = ca[:, 0:1] * dm[0:1, :]
        for s in range(1, 8):
            acc = acc + ca[:, s:s + 1] * dm[s:s + 1, :]
        o_ref[0] = acc

    return pl.pallas_call(body, name=name, out_shape=_sds((DEPTH, D, 768)), grid=(DEPTH,),
                          in_specs=[_full((D, LANES)), pl.BlockSpec((1, 8, 768), lambda i: (i, 0, 0))],
                          out_specs=pl.BlockSpec((1, D, 768), lambda i: (i, 0, 0)), compiler_params=_cp())(c_t, dmod)


def _pack(parts):
    flat = []
    for p in parts:
        f = p.reshape(-1)
        flat.append(jnp.pad(f, (0, (-f.size) % LANES)))
    v = jnp.concatenate(flat)
    return jnp.pad(v, (0, (-v.size) % (8 * LANES))).reshape(-1, LANES)


def _unpack(v, shapes):
    v = v.reshape(-1)
    out, off = [], 0
    for s in shapes:
        n = math.prod(s)
        out.append(v[off:off + n].reshape(s))
        off += n + (-n) % LANES
    return out


_GIVEN_DT, _GIVEN_C = 4608, 4624


def _pad_cols(w):
    return jnp.concatenate([w[..., :_GIVEN_DT], w[..., _GIVEN_C:], w[..., _GIVEN_DT:_GIVEN_C],
                            jnp.zeros(w.shape[:-1] + (NP - IN_COLS,), w.dtype)], axis=-1)


def _unpad_cols(w):
    return jnp.concatenate([w[..., :_GIVEN_DT], w[..., DTC:DTC + 16], w[..., _GIVEN_DT:DTC]], axis=-1)


def _pad_lanes(v):
    return jnp.pad(v, (0, LANES - v.shape[0])).reshape(1, LANES)


def _local_step(x2, tgt, mod, weights_of, grads_done, pre_w, post_w, conv_w, conv_b, dt_bias, a_log, d_skip, nw, sinks):
    saved = []
    xcur = x2
    for i in range(DEPTH):
        shift, scale, gate = mod[i:i + 1, :D], mod[i:i + 1, D:2 * D], mod[i:i + 1, 2 * D:]
        pw, qw = pre_w[i:i + 1], post_w[i:i + 1]
        w_p, w_o = weights_of(i, xcur)
        proj, h = _proj_fwd(xcur, pw, scale, shift, w_p, 0, "proj_fwd")
        o_a, lse_a = _attn_fwd(proj, QA // LANES, KA // LANES, VA // LANES, DILS, False, None, "attn_a_fwd")
        sink_x = jnp.repeat(sinks[i], HD).reshape(1, 512)
        o_c, lse_c = _attn_fwd(proj, QC // LANES, KC // LANES, VC // LANES, (1,), True, sink_x, "attn_c_fwd")
        cw, cb = conv_w[i], conv_b[i:i + 1]
        xbc_act = _conv_fwd(proj, cw, cb, "conv_fwd")
        ssd_p = (_pad_lanes(a_log[i]), _pad_lanes(dt_bias[i]), jnp.repeat(d_skip[i], HD).reshape(1, 1024), nw[i:i + 1])
        yb, hin = _ssd_fwd(xbc_act, proj, *ssd_p, "ssd_fwd")
        xnew, y = _out_fwd(o_a, yb, o_c, proj, w_o, 0, xcur, gate, qw, "out_fwd")
        saved.append((w_p, w_o, xcur, scale, gate, pw, qw, proj, h, o_a, lse_a, sink_x, o_c, lse_c, cw, cb, xbc_act, ssd_p, yb, hin, y))
        xcur = xnew
    dx, ltile = _loss_bwd(xcur, tgt, "loss")
    dmod, small = [None] * DEPTH, [None] * DEPTH
    for i in reversed(range(DEPTH)):
        w_p, w_o, xin, scale, gate, pw, qw, proj, h, o_a, lse_a, sink_x, o_c, lse_c, cw, cb, xbc_act, ssd_p, yb, hin, y = saved[i]
        dy, dgate, dpost, do_a, dz_a, dyb, do_c, dz_c = _dymix(dx, y, gate, qw, w_o, 0, o_a, o_c, proj, "dymix")
        dwo = _dwout(o_a, yb, o_c, proj, dy, "dwout")
        dq_a, dk_a, dv_a = _attn_bwd(proj, QA // LANES, KA // LANES, VA // LANES, do_a, o_a, lse_a, DILS, False, None, "attn_a_bwd")
        dq_c, dk_c, dv_c, dsk = _attn_bwd(proj, QC // LANES, KC // LANES, VC // LANES, do_c, o_c, lse_c, (1,), True, sink_x, "attn_c_bwd")
        dxbc_act, dz_b, ddt, dal16, ddtb, ddsk, dnw = _ssd_bwd(xbc_act, proj, hin, dyb, *ssd_p, "ssd_bwd")
        dxbc, dcw, dcb = _conv_bwd(proj, dxbc_act, cw, cb, "conv_bwd")
        half_a, half_b = [dq_a, dk_a, dv_a, dz_a, dz_b], [dxbc, dq_c, dz_c, dk_c, dv_c, ddt]
        sent = grads_done(i, _dwin(h, half_a, "dwin_a"), _dwin(h, half_b, "dwin_b"), dwo)
        dx, dshift, dscale, dpre = _dh_bwd(half_a + half_b, w_p, xin, pw, scale + sent[0, 0], dx, "dh_bwd")
        dmod[i] = jnp.concatenate([dshift, dscale, dgate], axis=1)
        small[i] = (dpre, dpost, dcw, dcb, ddtb[0, :16], dal16[0, :16], ddsk.reshape(16, HD).sum(axis=1), dnw, dsk[:, 0, ::HD].reshape(8))
    return ltile, dx, jnp.concatenate(dmod, axis=0), small


_SMALL = ((1, D), (1, D), (4, CONV_CH), (1, CONV_CH), (16,), (16,), (16,), (1, D), (8,))


def kernel(x, c, ada_w, ada_b, pre_norm_w, post_norm_w, w_in, conv_w, conv_b, dt_bias, a_log, d_skip, ssm_norm_w, sinks, w_out, loss_target, m_ada_w, m_ada_b, m_pre_norm_w, m_post_norm_w, m_w_in, m_conv_w, m_conv_b, m_dt_bias, m_a_log, m_d_skip, m_ssm_norm_w, m_sinks, m_w_out, v_ada_w, v_ada_b, v_pre_norm_w, v_post_norm_w, v_w_in, v_conv_w, v_conv_b, v_dt_bias, v_a_log, v_d_skip, v_ssm_norm_w, v_sinks, v_w_out):
    xi, yi, ci = lax.axis_index("x"), lax.axis_index("y"), lax.axis_index("c")
    chip = 2 * xi + yi
    me = 2 * chip + ci

    w_in_b = _cast_bf16(w_in, 512, "cast_w_in")
    w_out_b = _cast_bf16(w_out, 512, "cast_w_out")
    gathers = []
    for i in range(DEPTH):
        lands = [lax.dynamic_update_slice(lax.empty((4,) + a.shape[1:], a.dtype), a[i][None], (chip, 0, 0)) for a in (w_in_b, w_out_b)]
        gathers.append(_split_start(None, lands, f"gather_start{i}", "half" if i == 0 else "whole"))
    all_started = gathers[0][3] + gathers[1][3] + gathers[2][3] + gathers[3][3]

    def weights_of(i, after):
        send_sems, recv_sems, thru, _ = gathers[i]
        if i == 0:
            halves = _split_wait(send_sems, recv_sems, thru, 2, all_started + mod[:1, :LANES], "gather_wait0", "half")
            send_sems, recv_sems, thru, after = _split_start(None, halves, "share_start0", "sibling")
            g_in, g_out = _split_wait(send_sems, recv_sems, thru, 2, after, "share_wait0", "sibling")
        else:
            g_in, g_out = _split_wait(send_sems, recv_sems, thru, 2, after, f"gather_wait{i}")
        return _w_in_padded(g_in, "w_in_padded")[None], g_out.reshape(1, 2048, D)

    scatters = [None] * DEPTH

    def grads_done(i, dwa, dwb, dwo):
        blocks = [_grad_blocks(dwa, dwb, "grad_blocks"), _cast_bf16(dwo.reshape(4, 512, D), 512, "cast_dw_out")]
        scatters[i] = _split_start(blocks, [lax.empty(b.shape, b.dtype) for b in blocks], f"scatter_start{i}")
        return scatters[i][3]

    g0 = _allgather8(_pack([c, conv_w]), "gather_c")
    c_all = g0[:, :8, :].reshape(8, D)
    conv_w_full = jnp.concatenate([g0[2 * k, 8:56, :].reshape(DEPTH, 4, CONV_CH // 4) for k in range(4)], axis=-1)

    ada_b_mine = lax.dynamic_slice_in_dim(ada_b, 768 * chip, 768, axis=1).reshape(DEPTH, 1, 768)
    gm = _allgather8(_mod_part(c_all, ada_w, ada_b_mine, "mod_part").reshape(DEPTH * 8, 768), "gather_mod")
    gm = gm.reshape(4, 2, DEPTH, 8, 768)[:, 0]
    mod = lax.dynamic_index_in_dim(gm, me, axis=2, keepdims=False).transpose(1, 0, 2).reshape(DEPTH, 3 * D)

    ltile, dx, dmod, small = _local_step(x[0], loss_target[0], mod, weights_of, grads_done, pre_norm_w, post_norm_w, conv_w_full,
                                         conv_b, dt_bias, a_log, d_skip, ssm_norm_w, sinks)

    packed = _pack([dmod] + [g for layer in small for g in layer] + [ltile[0]])
    gs = _allgather8(packed, "gather_small")
    tot = _sum_blocks(gs[:, None], packed.shape[0], "sum_small")[0]
    parts = _unpack(tot, [(DEPTH, 3 * D)] + list(_SMALL) * DEPTH + [(LANES,)])
    g_ada_b, loss = parts[0], parts[-1][0]
    per_layer = [parts[1 + len(_SMALL) * i:1 + len(_SMALL) * (i + 1)] for i in range(DEPTH)]
    g_pre, g_post, g_cw, g_cb, g_dtb, g_al, g_dsk, g_nw, g_sk = [jnp.stack([per_layer[i][j] for i in range(DEPTH)]) for j in range(len(_SMALL))]
    g_pre, g_post, g_cb, g_nw = g_pre[:, 0], g_post[:, 0], g_cb[:, 0], g_nw[:, 0]
    g_cw = lax.dynamic_slice_in_dim(g_cw, (CONV_CH // 4) * chip, CONV_CH // 4, axis=2)

    dmod_all = gs[:, :(DEPTH * 3 * D) // LANES, :].reshape(8, DEPTH, 3 * D).transpose(1, 0, 2)
    dmod_mine = lax.dynamic_slice_in_dim(dmod_all, 768 * chip, 768, axis=2)
    c_t = jnp.pad(c_all.T, ((0, 0), (0, LANES - 8)))
    g_ada_w = _ada_grad(c_t, dmod_mine, "ada_grad")

    res = {}
    res["ada_w"] = _adamw(ada_w, [g_ada_w], m_ada_w, v_ada_w, 512, "adamw_ada_w")
    names = ["ada_b", "pre_norm_w", "post_norm_w", "conv_w", "conv_b", "dt_bias", "a_log", "d_skip", "ssm_norm_w", "sinks"]
    ws = [ada_b, pre_norm_w, post_norm_w, conv_w, conv_b, dt_bias, a_log, d_skip, ssm_norm_w, sinks]
    gsm = [g_ada_b, g_pre, g_post, g_cw, g_cb, g_dtb, g_al, g_dsk, g_nw, g_sk]
    ms = [m_ada_b, m_pre_norm_w, m_post_norm_w, m_conv_w, m_conv_b, m_dt_bias, m_a_log, m_d_skip, m_ssm_norm_w, m_sinks]
    vs = [v_ada_b, v_pre_norm_w, v_post_norm_w, v_conv_w, v_conv_b, v_dt_bias, v_a_log, v_d_skip, v_ssm_norm_w, v_sinks]
    pw_, pg_, pm_, pv_ = _pack(ws), _pack(gsm), _pack(ms), _pack(vs)
    small_out = _adamw(pw_[None], [pg_[None]], pm_[None], pv_[None], pw_.shape[0], "adamw_small")

    others_done = small_out[1][0, :8] + res["ada_w"][1][0, :8, :LANES]
    landed = [_split_wait(*scatters[i][:3], 2, others_done, f"scatter_wait{i}") for i in range(DEPTH)]
    p_in = _sum_chips([d[2] for d in landed], [d[0] for d in landed], 128, "sum_w_in")
    p_out = _sum_chips([d[3] for d in landed], [d[1] for d in landed], 256, "sum_w_out")
    col_major, row_major = (lambda a: jnp.transpose(a, (2, 0, 1))), (lambda a: jnp.transpose(a, (1, 2, 0)))
    p_in = col_major(p_in)
    s_in, s_out = _sibling_swap([p_in, p_out], "swap_partials")
    res["w_in"] = [row_major(a) for a in _adamw(col_major(w_in), [p_in, s_in], col_major(m_w_in), col_major(v_w_in), None,
                                                "adamw_w_in", lead=SHARD_IN // 18)]
    res["w_out"] = _adamw(w_out, [p_out, s_out], m_w_out, v_w_out, 512, "adamw_w_out")
    shapes = [w.shape for w in ws]
    for kind in range(4):
        for nm, a in zip(names, _unpack(small_out[kind][0], shapes)):
            res.setdefault(nm, [None] * 4)[kind] = a
    order = ["ada_w", "ada_b", "pre_norm_w", "post_norm_w", "w_in", "conv_w", "conv_b", "dt_bias", "a_log", "d_skip", "ssm_norm_w", "sinks", "w_out"]
    return (loss, dx[None], *[res[n][0] for n in order], *[res[n][1] for n in order], *[res[n][2] for n in order], *[res[n][3] for n in order])
```

```python
import math

import jax
import jax.numpy as jnp
from jax import lax
from jax.experimental import pallas as pl
from jax.experimental.pallas import tpu as pltpu

F32 = jnp.float32
MXU = jnp.bfloat16
HI = lax.Precision.HIGHEST
MESH = pl.DeviceIdType.MESH

SEQ = 4096
D = 1024
DEPTH = 4
HD = 64
QK_SCALE = HD ** -0.5
LANES = 128
BLK = 128
DILS = (1, 4, 16)
NEG = -1e30
EPS = 1e-6
MIB = 1024 * 1024

NP = 6144
QA, KA, VA, ZA = 0, 512, 1024, 1536
ZB, XBC = 2048, 3072
QC, ZC, KC, VC = 4608, 5120, 5632, 5760
DTC = 5888
IN_COLS = 5904
SHARD_IN = IN_COLS // 4
CONV_CH = 1536
TM = 512

ADAM_LR, ADAM_B1, ADAM_B2, ADAM_EPS, ADAM_WD, ADAM_STEP = 0.001, 0.9, 0.999, 1e-08, 0.01, 10

NT = (((1,), (1,)), ((), ()))
TN = (((0,), (0,)), ((), ()))


def _cp(vmem_mib=48):
    return pltpu.CompilerParams(vmem_limit_bytes=vmem_mib * MIB)


def _sds(shape, dtype=F32):
    return jax.ShapeDtypeStruct(shape, dtype)


def _full(shape):
    n = len(shape)
    return pl.BlockSpec(shape, lambda *_: (0,) * n)


def _mm(a, b, dims=None):
    if dims is None:
        return jnp.dot(a.astype(MXU), b.astype(MXU), preferred_element_type=F32)
    return lax.dot_general(a.astype(MXU), b.astype(MXU), dims, preferred_element_type=F32)


def _sigmoid(x):
    return 1.0 / (1.0 + jnp.exp(-x))


def _silu(x):
    return x * _sigmoid(x)


def _dsilu(x):
    s = _sigmoid(x)
    return s * (1.0 + x * (1.0 - s))


def _softplus(x):
    ax = jnp.where(x >= 0, x, -x)
    return jnp.maximum(x, 0.0) + jnp.log1p(jnp.exp(-ax))


def _half_masks():
    lane = lax.broadcasted_iota(jnp.int32, (1, LANES), 1)
    m0 = (lane < HD).astype(F32)
    return m0, 1.0 - m0


def _allgather8(v, name):
    r, cc = v.shape

    def body(v_ref, out_ref, send_sems, recv_sems):
        x, y, c = lax.axis_index("x"), lax.axis_index("y"), lax.axis_index("c")
        me = 4 * x + 2 * y + c
        out_ref[me] = v_ref[...]
        peers = []
        for k in range(1, 8):
            px = 1 - x if k & 4 else x
            py = 1 - y if k & 2 else y
            pc = 1 - c if k & 1 else c
            peers.append((px, py, pc))
        sends = []
        for k, peer in enumerate(peers):
            cp = pltpu.make_async_remote_copy(src_ref=v_ref, dst_ref=out_ref.at[me], send_sem=send_sems.at[k],
                                              recv_sem=recv_sems.at[k], device_id=peer, device_id_type=MESH)
            cp.start()
            sends.append(cp)
        for k, (px, py, pc) in enumerate(peers):
            pltpu.make_async_remote_copy(src_ref=v_ref, dst_ref=out_ref.at[4 * px + 2 * py + pc], send_sem=send_sems.at[k],
                                         recv_sem=recv_sems.at[k], device_id=(px, py, pc), device_id_type=MESH).wait_recv()
        for cp in sends:
            cp.wait_send()

    return pl.pallas_call(
        body, name=name, out_shape=_sds((8, r, cc)),
        in_specs=[pl.BlockSpec(memory_space=pltpu.VMEM)], out_specs=pl.BlockSpec(memory_space=pltpu.VMEM),
        scratch_shapes=[pltpu.SemaphoreType.DMA((7,)), pltpu.SemaphoreType.DMA((7,))],
        compiler_params=_cp(32),
    )(v)


_HBM = pl.BlockSpec(memory_space=pltpu.HBM)
_SEM = pl.BlockSpec(memory_space=pltpu.SEMAPHORE)
_EFFECT = pltpu.SideEffectType.DATAFLOW_SIDE_EFFECTING


def _chip_copies(src_refs, land_refs, send_sems, recv_sems, part="whole"):
    x, y, c = lax.axis_index("x"), lax.axis_index("y"), lax.axis_index("c")
    mine = 2 * x + y
    out = []
    for i, land in enumerate(land_refs):
        half = land.shape[1] // 2
        own, others = pl.ds(pl.multiple_of(c * half, half), half), pl.ds(pl.multiple_of((1 - c) * half, half), half)
        for j, (px, py) in enumerate([(1 - x, y), (x, 1 - y), (1 - x, 1 - y)]):
            slot, peer = 2 * px + py, (px, py, c)
            if part == "whole":
                src = src_refs[i].at[slot] if src_refs else land.at[mine]
                there, here = land.at[mine], land.at[slot]
            elif part == "half":
                src = there = land.at[mine].at[own]
                here = land.at[slot].at[own]
            else:
                src = there = land.at[slot].at[own]
                here, peer = land.at[slot].at[others], (x, y, 1 - c)
            mk = lambda dst, i=i, j=j, src=src, peer=peer: pltpu.make_async_remote_copy(
                src_ref=src, dst_ref=dst, send_sem=send_sems.at[3 * i + j], recv_sem=recv_sems.at[3 * i + j],
                device_id=peer, device_id_type=MESH)
            out.append((mk(there), mk(here)))
    return out


def _split_start(srcs, lands, name, part="whole"):
    ops = list(srcs or []) + list(lands)
    ns, n = len(srcs or []), len(lands)

    def body(*refs):
        src_refs, land_refs = refs[:ns], refs[ns:ns + n]
        send_sems, recv_sems = refs[ns + n], refs[ns + n + 1]
        for mine_out, _ in _chip_copies(src_refs, land_refs, send_sems, recv_sems, part):
            mine_out.start()
        refs[-1][...] = jnp.zeros_like(refs[-1])

    sems = pltpu.SemaphoreType.DMA((3 * n,))
    res = pl.pallas_call(
        body, name=name, out_shape=(sems, sems) + tuple(pltpu.HBM(a.shape, a.dtype) for a in ops) + (_sds((8, LANES)),),
        in_specs=[_HBM] * len(ops), out_specs=(_SEM, _SEM) + (_HBM,) * len(ops) + (pl.BlockSpec(memory_space=pltpu.VMEM),),
        input_output_aliases={k: 2 + k for k in range(len(ops))},
        compiler_params=pltpu.CompilerParams(has_side_effects=_EFFECT),
    )(*[pltpu.with_memory_space_constraint(a, pltpu.HBM) for a in ops])
    return res[0], res[1], list(res[2:2 + len(ops)]), res[-1]


def _split_wait(send_sems, recv_sems, thru, n, after, name, part="whole"):
    ns = len(thru) - n

    def body(*refs):
        src_refs, land_refs = refs[:ns], refs[ns:ns + n]
        for mine_out, arriving in _chip_copies(src_refs, land_refs, refs[ns + n], refs[ns + n + 1], part):
            mine_out.wait_send()
            arriving.wait_recv()

    res = pl.pallas_call(
        body, name=name, out_shape=tuple(pltpu.HBM(a.shape, a.dtype) for a in thru),
        in_specs=[_HBM] * len(thru) + [_SEM, _SEM, pl.BlockSpec(memory_space=pl.ANY)], out_specs=(_HBM,) * len(thru),
        input_output_aliases={k: k for k in range(len(thru))},
        compiler_params=pltpu.CompilerParams(has_side_effects=_EFFECT),
    )(*thru, send_sems, recv_sems, after)
    return list(res)


def _sibling_swap(arrs, name):
    n = len(arrs)

    def body(*refs):
        ins, outs_, (send_sems, recv_sems) = refs[:n], refs[n:2 * n], refs[2 * n:]
        sib = (lax.axis_index("x"), lax.axis_index("y"), 1 - lax.axis_index("c"))
        cps = [pltpu.make_async_remote_copy(src_ref=ins[i], dst_ref=outs_[i], send_sem=send_sems.at[i], recv_sem=recv_sems.at[i],
                                            device_id=sib, device_id_type=MESH) for i in range(n)]
        for cp in cps:
            cp.start()
        for cp in cps:
            cp.wait_recv()
        for cp in cps:
            cp.wait_send()

    hbm = pl.BlockSpec(memory_space=pltpu.HBM)
    return pl.pallas_call(
        body, name=name, out_shape=tuple(_sds(a.shape, a.dtype) for a in arrs), in_specs=[hbm] * n, out_specs=tuple([hbm] * n),
        scratch_shapes=[pltpu.SemaphoreType.DMA((n,)), pltpu.SemaphoreType.DMA((n,))],
    )(*arrs)


def _tile_spec(rows, cc):
    return pl.BlockSpec((None, rows, cc), lambda l, i: (l, i, 0))


def _cast_bf16(a, rows, name):
    nl, r, cc = a.shape

    def body(a_ref, o_ref):
        o_ref[...] = a_ref[...].astype(jnp.bfloat16)

    return pl.pallas_call(body, name=name, out_shape=_sds((nl, r, cc), jnp.bfloat16), grid=(nl, r // rows),
                          in_specs=[_tile_spec(rows, cc)], out_specs=_tile_spec(rows, cc), compiler_params=_cp())(a)


def _sum_blocks(a, rows, name):
    k, nl, r, cc = a.shape

    def body(a_ref, o_ref):
        acc = a_ref[0].astype(F32)
        for j in range(1, k):
            acc = acc + a_ref[j].astype(F32)
        o_ref[...] = acc

    return pl.pallas_call(body, name=name, out_shape=_sds((nl, r, cc)), grid=(nl, r // rows),
                          in_specs=[pl.BlockSpec((k, None, rows, cc), lambda l, i: (0, l, i, 0))],
                          out_specs=_tile_spec(rows, cc), compiler_params=_cp())(a)


def _sum_chips(lands, srcs, rows, name):
    nl = len(lands)
    _, r, cc = lands[0].shape

    def body(*refs):
        land_refs, src_refs, o_ref = refs[:nl], refs[nl:2 * nl], refs[2 * nl]
        mine = 2 * lax.axis_index("x") + lax.axis_index("y")
        for j in range(nl):
            @pl.when(pl.program_id(0) == j)
            def _(j=j):
                own = src_refs[j][mine].astype(F32)
                acc = None
                for k in range(4):
                    term = jnp.where(mine == k, own, land_refs[j][k].astype(F32))
                    acc = term if acc is None else acc + term
                o_ref[...] = acc

    specs = [pl.BlockSpec((4, rows, cc), lambda l, i, j=j: (0, jnp.where(l == j, i, 0), 0)) for j in range(nl)]
    return pl.pallas_call(body, name=name, out_shape=_sds((nl, r, cc)), grid=(nl, r // rows),
                          in_specs=specs + specs, out_specs=_tile_spec(rows, cc), compiler_params=_cp())(*lands, *srcs)


def _adamw(w, parts, m, v, rows, name, lead=None):
    nl, r, cc = w.shape
    np_ = len(parts)
    c1 = 1.0 / (1.0 - ADAM_B1 ** ADAM_STEP)
    c2 = 1.0 / (1.0 - ADAM_B2 ** ADAM_STEP)

    def body(*refs):
        w_ref, p_refs, (m_ref, v_ref, g_ref, d_ref, nm_ref, nv_ref) = refs[0], refs[1:1 + np_], refs[1 + np_:]
        g = p_refs[0][...]
        for p_ref in p_refs[1:]:
            g = g + p_ref[...]
        nm = ADAM_B1 * m_ref[...] + (1.0 - ADAM_B1) * g
        nv = ADAM_B2 * v_ref[...] + (1.0 - ADAM_B2) * (g * g)
        g_ref[...] = g
        nm_ref[...] = nm
        nv_ref[...] = nv
        d_ref[...] = -ADAM_LR * ((nm * c1) / (jnp.sqrt(nv * c2) + ADAM_EPS) + ADAM_WD * w_ref[...])

    if lead is None:
        spec, grid = _tile_spec(rows, cc), (nl, r // rows)
    else:
        spec, grid = pl.BlockSpec((lead, r, cc), lambda i: (i, 0, 0)), (nl // lead,)
    return pl.pallas_call(body, name=name, out_shape=(_sds((nl, r, cc)),) * 4, grid=grid,
                          in_specs=[spec] * (3 + np_), out_specs=(spec,) * 4, compiler_params=_cp())(w, *parts, m, v)


_BIAS = pltpu.VMEM((2, 2 * BLK, 2 * BLK), F32)


def _fill_band_bias(bias_ref):
    qi = lax.broadcasted_iota(jnp.int32, (2 * BLK, 2 * BLK), 0) & (BLK - 1)
    kj = lax.broadcasted_iota(jnp.int32, (2 * BLK, 2 * BLK), 1)
    dist = BLK + qi - kj
    band = (dist >= 0) & (dist <= BLK)
    bias_ref[0] = jnp.where(band, 0.0, NEG)
    bias_ref[1] = jnp.where(band & (kj >= BLK), 0.0, NEG)


class _HeadStack:
    def __init__(self, group):
        self.m0, self.m1 = _half_masks()
        self.group = group
        if group is not None:
            self.kv_mask = (self.m0, self.m1)[group]

    def _swap_half(self, t, a):
        return t if a == self.group else pltpu.roll(t, HD, axis=1)

    def stack(self, t):
        t0, t1 = t * self.m0, t * self.m1
        if self.group is not None:
            t0, t1 = self._swap_half(t0, 0), self._swap_half(t1, 1)
        return jnp.concatenate([t0, t1], axis=0)

    def unstack(self, ts):
        if self.group is None:
            return ts[:BLK] * self.m0 + ts[BLK:] * self.m1
        return self._swap_half(ts[:BLK] * self.kv_mask, 0) + self._swap_half(ts[BLK:] * self.kv_mask, 1)


def _rows(st, dil):
    if dil == 1:
        return pl.ds(pl.multiple_of(st, BLK), BLK)
    return pl.ds(st, BLK, stride=dil)


def _block_pos(n, dil):
    nb = SEQ // (dil * BLK)
    r, b = n // nb, n % nb
    hp = (b > 0).astype(jnp.int32)
    st = r + dil * BLK * b
    return st, st - dil * BLK * hp, 1 - hp


def _attn_fwd(proj, qblk, kblk, vblk, dils, gqa, sink_x, name):
    has_sink = sink_x is not None

    def body(*refs):
        if has_sink:
            q_ref, k_ref, v_ref, s_ref, o_ref, lse_ref, m_scr, z_scr, bias_scr = refs
        else:
            q_ref, k_ref, v_ref, o_ref, lse_ref, m_scr, z_scr, bias_scr = refs

        @pl.when(pl.program_id(0) == 0)
        def _():
            _fill_band_bias(bias_scr)
        o_ref[...] = jnp.zeros_like(o_ref)
        if has_sink:
            z_scr[...] = jnp.ones_like(z_scr)
            m_scr[...] = jnp.broadcast_to(s_ref[...], m_scr.shape)
        else:
            z_scr[...] = jnp.zeros_like(z_scr)
            m_scr[...] = jnp.full_like(m_scr, NEG)

        def step(n, carry, dil, heads):
            m0, m1 = heads.m0, heads.m1
            st, stp, first = _block_pos(n, dil)
            rq, rp = _rows(st, dil), _rows(stp, dil)
            kk = jnp.concatenate([k_ref[rp, :], k_ref[rq, :]], axis=0)
            vv = jnp.concatenate([v_ref[rp, :], v_ref[rq, :]], axis=0)
            s = _mm(heads.stack(q_ref[rq, :] * QK_SCALE), kk, NT) + bias_scr[first]
            m = jnp.max(s, axis=1, keepdims=True)
            p = jnp.exp(s - m)
            l = jnp.sum(p, axis=1, keepdims=True)
            o_pair = heads.unstack(_mm(p, vv))
            m_pair = m[:BLK] * m0 + m[BLK:] * m1
            l_pair = l[:BLK] * m0 + l[BLK:] * m1
            m_old = m_scr[rq, :]
            m_new = jnp.maximum(m_old, m_pair)
            alpha, beta = jnp.exp(m_old - m_new), jnp.exp(m_pair - m_new)
            o_ref[rq, :] = o_ref[rq, :] * alpha + o_pair * beta
            z_scr[rq, :] = z_scr[rq, :] * alpha + l_pair * beta
            m_scr[rq, :] = m_new
            return carry

        def blocks(heads):
            for dil in dils:
                lax.fori_loop(0, SEQ // BLK, lambda n, carry, dil=dil: step(n, carry, dil, heads), 0, unroll=8)

        if gqa:
            for grp in range(2):
                pl.when(pl.program_id(0) // 2 == grp)(lambda grp=grp: blocks(_HeadStack(grp)))
        else:
            blocks(_HeadStack(None))

        def fin(t, carry):
            rt = pl.ds(pl.multiple_of(t * TM, TM), TM)
            z = z_scr[rt, :]
            o_ref[rt, :] = o_ref[rt, :] / z
            lse_ref[rt, :] = m_scr[rt, :] + jnp.log(z)
            return carry
        lax.fori_loop(0, SEQ // TM, fin, 0)

    col = lambda blk: pl.BlockSpec((SEQ, LANES), lambda p, blk=blk: (0, blk + p))
    kv = (lambda blk: pl.BlockSpec((SEQ, LANES), lambda p, blk=blk: (0, blk))) if gqa else col
    in_specs = [col(qblk), kv(kblk), kv(vblk)]
    args = [proj, proj, proj]
    if has_sink:
        in_specs.append(pl.BlockSpec((1, LANES), lambda p: (0, p)))
        args.append(sink_x)
    out = pl.BlockSpec((SEQ, LANES), lambda p: (0, p))
    return pl.pallas_call(body, name=name, out_shape=(_sds((SEQ, 512)), _sds((SEQ, 512))), grid=(4,),
                          in_specs=in_specs, out_specs=(out, out),
                          scratch_shapes=[pltpu.VMEM((SEQ, LANES), F32), pltpu.VMEM((SEQ, LANES), F32), _BIAS],
                          compiler_params=_cp(48))(*args)


def _attn_bwd(proj, qblk, kblk, vblk, do, o, lse, dils, gqa, sink_x, name):
    has_sink = sink_x is not None

    def body(*refs):
        if has_sink:
            q_ref, k_ref, v_ref, do_ref, o_ref, lse_ref, s_ref, dq_ref, dk_ref, dv_ref, ds_ref, bias_scr = refs
        else:
            q_ref, k_ref, v_ref, do_ref, o_ref, lse_ref, dq_ref, dk_ref, dv_ref, bias_scr = refs
        pid = pl.program_id(0)

        @pl.when(pid == 0)
        def _():
            _fill_band_bias(bias_scr)
        dq_ref[...] = jnp.zeros_like(dq_ref)
        if gqa:
            @pl.when(pid == 0)
            def _():
                dk_ref[...] = jnp.zeros_like(dk_ref)
                dv_ref[...] = jnp.zeros_like(dv_ref)
        else:
            dk_ref[...] = jnp.zeros_like(dk_ref)
            dv_ref[...] = jnp.zeros_like(dv_ref)

        def step(n, carry, dil, heads):
            m0, m1 = heads.m0, heads.m1
            st, stp, first = _block_pos(n, dil)
            rq, rp = _rows(st, dil), _rows(stp, dil)
            do_, lse_ = do_ref[rq, :], lse_ref[rq, :]
            kk = jnp.concatenate([k_ref[rp, :], k_ref[rq, :]], axis=0)
            vv = jnp.concatenate([v_ref[rp, :], v_ref[rq, :]], axis=0)
            qs, dos = heads.stack(q_ref[rq, :] * QK_SCALE), heads.stack(do_)
            doo = do_ * o_ref[rq, :]
            delta = jnp.concatenate([jnp.sum(doo * m0, axis=1, keepdims=True), jnp.sum(doo * m1, axis=1, keepdims=True)], axis=0)
            lse_s = jnp.concatenate([lse_[:, 0:1], lse_[:, HD:HD + 1]], axis=0)
            p = jnp.exp(_mm(qs, kk, NT) + bias_scr[first] - lse_s)
            ds = p * (_mm(dos, vv, NT) - delta)
            dq_ref[rq, :] += heads.unstack(_mm(ds, kk)) * QK_SCALE
            dk_sum, dv_sum = _mm(ds, qs, TN), _mm(p, dos, TN)
            dk_ref[rp, :] += dk_sum[:BLK]
            dk_ref[rq, :] += dk_sum[BLK:]
            dv_ref[rp, :] += dv_sum[:BLK]
            dv_ref[rq, :] += dv_sum[BLK:]
            return carry

        def blocks(heads):
            for dil in dils:
                lax.fori_loop(0, SEQ // BLK, lambda n, carry, dil=dil: step(n, carry, dil, heads), 0, unroll=4)

        if gqa:
            for grp in range(2):
                pl.when(pid // 2 == grp)(lambda grp=grp: blocks(_HeadStack(grp)))
        else:
            blocks(_HeadStack(None))

        if has_sink:
            m0, m1 = _half_masks()

            def sink_rows(t, acc):
                rt = pl.ds(pl.multiple_of(t * TM, TM), TM)
                return acc - jnp.sum(jnp.exp(s_ref[...] - lse_ref[rt, :]) * (do_ref[rt, :] * o_ref[rt, :]), axis=0, keepdims=True)
            acc = lax.fori_loop(0, SEQ // TM, sink_rows, jnp.zeros((1, LANES), F32))
            per_head = jnp.sum(acc * m0, axis=1, keepdims=True) * m0 + jnp.sum(acc * m1, axis=1, keepdims=True) * m1
            ds_ref[0] = jnp.broadcast_to(per_head, (8, LANES))

    col = lambda blk: pl.BlockSpec((SEQ, LANES), lambda p, blk=blk: (0, blk + p))
    kv = (lambda blk: pl.BlockSpec((SEQ, LANES), lambda p, blk=blk: (0, blk))) if gqa else col
    pair = pl.BlockSpec((SEQ, LANES), lambda p: (0, p))
    in_specs = [col(qblk), kv(kblk), kv(vblk), pair, pair, pair]
    args = [proj, proj, proj, do, o, lse]
    kvw = LANES if gqa else 512
    kv_out = pl.BlockSpec((SEQ, LANES), lambda p: (0, 0)) if gqa else pair
    out_shape = [_sds((SEQ, 512)), _sds((SEQ, kvw)), _sds((SEQ, kvw))]
    out_specs = [pair, kv_out, kv_out]
    if has_sink:
        in_specs.append(pl.BlockSpec((1, LANES), lambda p: (0, p)))
        args.append(sink_x)
        out_shape.append(_sds((4, 8, LANES)))
        out_specs.append(pl.BlockSpec((1, 8, LANES), lambda p: (p, 0, 0)))
    return pl.pallas_call(body, name=name, out_shape=tuple(out_shape), grid=(4,), in_specs=in_specs,
                          out_specs=tuple(out_specs), scratch_shapes=[_BIAS], compiler_params=_cp(56))(*args)


_CT = 128


def _rows_before(x_ref, t, k):
    if t == 0:
        return jnp.concatenate([jnp.zeros((k, LANES), F32), x_ref[0:_CT - k, :]], axis=0)
    return x_ref[t * _CT - k:(t + 1) * _CT - k, :]


def _conv_pre(x_ref, w_ref, b_ref, t):
    taps = [x_ref[t * _CT:(t + 1) * _CT, :]] + [_rows_before(x_ref, t, k) for k in range(1, 4)]
    u = b_ref[...] + taps[0] * w_ref[3:4, :]
    for k in range(1, 4):
        u = u + taps[k] * w_ref[3 - k:4 - k, :]
    return u, taps


def _conv_fwd(proj, w, b, name):
    def body(x_ref, w_ref, b_ref, o_ref):
        for t in range(SEQ // _CT):
            o_ref[t * _CT:(t + 1) * _CT, :] = _silu(_conv_pre(x_ref, w_ref, b_ref, t)[0])

    nblk = CONV_CH // LANES
    return pl.pallas_call(body, name=name, out_shape=_sds((SEQ, CONV_CH)), grid=(nblk,),
                          in_specs=[pl.BlockSpec((SEQ, LANES), lambda j: (0, XBC // LANES + j)),
                                    pl.BlockSpec((4, LANES), lambda j: (0, j)), pl.BlockSpec((1, LANES), lambda j: (0, j))],
                          out_specs=pl.BlockSpec((SEQ, LANES), lambda j: (0, j)), compiler_params=_cp())(proj, w, b)


def _conv_bwd(proj, dact, w, b, name):
    def body(x_ref, da_ref, w_ref, b_ref, dx_ref, dw_ref, db_ref, du_scr):
        du_scr[SEQ:SEQ + 8, :] = jnp.zeros((8, LANES), F32)
        db = jnp.zeros((1, LANES), F32)
        dws = [jnp.zeros((1, LANES), F32)] * 4
        for t in range(SEQ // _CT):
            u, taps = _conv_pre(x_ref, w_ref, b_ref, t)
            du = da_ref[t * _CT:(t + 1) * _CT, :] * _dsilu(u)
            du_scr[t * _CT:(t + 1) * _CT, :] = du
            db = db + jnp.sum(du, axis=0, keepdims=True)
            dws = [dws[k] + jnp.sum(du * taps[k], axis=0, keepdims=True) for k in range(4)]
        db_ref[...] = db
        for k in range(4):
            dw_ref[3 - k:4 - k, :] = dws[k]
        for t in range(SEQ // _CT):
            dx = du_scr[t * _CT:(t + 1) * _CT, :] * w_ref[3:4, :]
            for k in range(1, 4):
                dx = dx + du_scr[t * _CT + k:(t + 1) * _CT + k, :] * w_ref[3 - k:4 - k, :]
            dx_ref[t * _CT:(t + 1) * _CT, :] = dx.astype(dx_ref.dtype)

    nblk = CONV_CH // LANES
    blk = pl.BlockSpec((SEQ, LANES), lambda j: (0, j))
    wspec, bspec = pl.BlockSpec((4, LANES), lambda j: (0, j)), pl.BlockSpec((1, LANES), lambda j: (0, j))
    return pl.pallas_call(body, name=name, out_shape=(_sds((SEQ, CONV_CH), MXU), _sds((4, CONV_CH)), _sds((1, CONV_CH))), grid=(nblk,),
                          in_specs=[pl.BlockSpec((SEQ, LANES), lambda j: (0, XBC // LANES + j)), blk, wspec, bspec],
                          out_specs=(blk, wspec, bspec), scratch_shapes=[pltpu.VMEM((SEQ + 8, LANES), F32)],
                          compiler_params=_cp())(proj, dact, w, b)


def _ssd_chunk(xs, bm, cm, dtr, z, hs, al16, dtb, dskx, nw):
    m0, m1 = _half_masks()
    row = lax.broadcasted_iota(jnp.int32, (BLK, BLK), 0)
    col = lax.broadcasted_iota(jnp.int32, (BLK, BLK), 1)
    causal = row >= col
    tril = causal.astype(F32)
    lane = lax.broadcasted_iota(jnp.int32, (1, LANES), 1)
    sub = lax.broadcasted_iota(jnp.int32, (BLK, 1), 0)
    last_row = (sub == BLK - 1).astype(F32)
    dt = jnp.where(lane < 16, _softplus(dtr + dtb), 0.0)
    a16 = -jnp.exp(al16)
    acum = jnp.dot(tril, dt * a16, precision=HI, preferred_element_type=F32)
    acum_t = acum.T
    gmat = [_mm(cm[g], bm[g], NT) for g in range(2)]
    ys, hn = [], []
    for p in range(8):
        g = p // 4
        pick = [(lane == 2 * p + a).astype(F32) for a in range(2)]
        col_h = [jnp.sum(acum * pick[a], axis=1, keepdims=True) for a in range(2)]
        dt_x = sum(jnp.sum(dt * pick[a], axis=1, keepdims=True) * msk for a, msk in enumerate((m0, m1)))
        ac_x = col_h[0] * m0 + col_h[1] * m1
        a_end = jnp.sum(ac_x * last_row, axis=0, keepdims=True)
        xdt = xs[p] * dt_x
        y = _mm(cm[g], hs[p]) * jnp.exp(ac_x)
        for a, msk in enumerate((m0, m1)):
            row_h = jnp.sum(acum_t * (sub == 2 * p + a).astype(F32), axis=0, keepdims=True)
            decay = jnp.exp(jnp.where(causal, col_h[a] - row_h, NEG))
            y = y + _mm(gmat[g] * decay, xdt * msk)
        st = _mm(bm[g], xdt * jnp.exp(a_end - ac_x), TN)
        hn.append(hs[p] * jnp.exp(a_end) + st)
        y = y + dskx[p] * xs[p]
        ys.append(y * _silu(z[p]))
    out = []
    for g in range(2):
        ms = sum(jnp.sum(ys[p] * ys[p], axis=1, keepdims=True) for p in range(4 * g, 4 * g + 4)) * (1.0 / 512)
        rstd = lax.rsqrt(ms + EPS)
        out += [ys[p] * rstd * nw[p] for p in range(4 * g, 4 * g + 4)]
    return out, hn


def _tiles(ref, n, off=0, rows=slice(None)):
    return [ref[rows, off + LANES * p:off + LANES * (p + 1)] for p in range(n)]


def _ssd_load(xbc_ref, z_ref, dt_ref, rows):
    return (_tiles(xbc_ref, 8, 0, rows), _tiles(xbc_ref, 2, 1024, rows), _tiles(xbc_ref, 2, 1280, rows), dt_ref[rows, :],
            _tiles(z_ref, 8, 0, rows))


def _ssd_params(al16_ref, dtb_ref, dsk_ref, nw_ref):
    return al16_ref[...], dtb_ref[...], _tiles(dsk_ref, 8), _tiles(nw_ref, 8)


_NCH = SEQ // BLK
_PER_STEP = 4
_STEP_ROWS = _PER_STEP * BLK


def _ssd_param_specs():
    return [_full((1, LANES)), _full((1, LANES)), _full((1, 1024)), _full((1, 1024))]


def _ssd_fwd(xbc_act, proj, al16, dtb, dskx, nw, name):
    def body(xbc_ref, z_ref, dt_ref, al16_ref, dtb_ref, dsk_ref, nw_ref, y_ref, hin_ref, h_scr):
        @pl.when(pl.program_id(0) == 0)
        def _():
            h_scr[...] = jnp.zeros_like(h_scr)
        params = _ssd_params(al16_ref, dtb_ref, dsk_ref, nw_ref)
        hs = _tiles(h_scr, 8)
        for k in range(_PER_STEP):
            rows = slice(BLK * k, BLK * (k + 1))
            for p in range(8):
                hin_ref[k, :, LANES * p:LANES * (p + 1)] = hs[p]
            ys, hs = _ssd_chunk(*_ssd_load(xbc_ref, z_ref, dt_ref, rows), hs, *params)
            for p in range(8):
                y_ref[rows, LANES * p:LANES * (p + 1)] = ys[p].astype(y_ref.dtype)
        for p in range(8):
            h_scr[:, LANES * p:LANES * (p + 1)] = hs[p]

    return pl.pallas_call(
        body, name=name, out_shape=(_sds((SEQ, 1024), MXU), _sds((_NCH, BLK, 1024))), grid=(_NCH // _PER_STEP,),
        in_specs=[pl.BlockSpec((_STEP_ROWS, CONV_CH), lambda c: (c, 0)), pl.BlockSpec((_STEP_ROWS, 1024), lambda c: (c, ZB // 1024)),
                  pl.BlockSpec((_STEP_ROWS, LANES), lambda c: (c, DTC // LANES))] + _ssd_param_specs(),
        out_specs=(pl.BlockSpec((_STEP_ROWS, 1024), lambda c: (c, 0)), pl.BlockSpec((_PER_STEP, BLK, 1024), lambda c: (c, 0, 0))),
        scratch_shapes=[pltpu.VMEM((BLK, 1024), F32)], compiler_params=_cp())(xbc_act, proj, proj, al16, dtb, dskx, nw)


def _ssd_bwd(xbc_act, proj, hin, dyb, al16, dtb, dskx, nw, name):
    def body(xbc_ref, z_ref, dt_ref, hin_ref, dy_ref, al16_ref, dtb_ref, dsk_ref, nw_ref,
             dxbc_ref, dz_ref, ddt_ref, dal16_ref, ddtb_ref, ddsk_ref, dnw_ref, dh_scr):
        @pl.when(pl.program_id(0) == 0)
        def _():
            dh_scr[...] = jnp.zeros_like(dh_scr)
            for r in (dal16_ref, ddtb_ref, ddsk_ref, dnw_ref):
                r[...] = jnp.zeros_like(r)
        params = _ssd_params(al16_ref, dtb_ref, dsk_ref, nw_ref)
        dhs = _tiles(dh_scr, 8)
        for k in reversed(range(_PER_STEP)):
            rows = slice(BLK * k, BLK * (k + 1))
            hs = [hin_ref[k, :, LANES * p:LANES * (p + 1)] for p in range(8)]
            _, vjp = jax.vjp(lambda a, h, q: _ssd_chunk(*a, h, *q), _ssd_load(xbc_ref, z_ref, dt_ref, rows), hs, params)
            (dxs, dbm, dcm, ddt, dz), dhs, (dal16, ddtb, ddsk, dnw) = vjp((_tiles(dy_ref, 8, 0, rows), dhs))
            for p in range(8):
                cols = slice(LANES * p, LANES * (p + 1))
                dxbc_ref[rows, cols] = dxs[p]
                dz_ref[rows, cols] = dz[p].astype(dz_ref.dtype)
                ddsk_ref[:, cols] += ddsk[p]
                dnw_ref[:, cols] += dnw[p]
            for g in range(2):
                dxbc_ref[rows, 1024 + LANES * g:1024 + LANES * (g + 1)] = dbm[g]
                dxbc_ref[rows, 1280 + LANES * g:1280 + LANES * (g + 1)] = dcm[g]
            ddt_ref[rows, :] = ddt.astype(ddt_ref.dtype)
            dal16_ref[...] += dal16
            ddtb_ref[...] += ddtb
        for p in range(8):
            dh_scr[:, LANES * p:LANES * (p + 1)] = dhs[p]

    rev = lambda c: _NCH // _PER_STEP - 1 - c
    return pl.pallas_call(
        body, name=name,
        out_shape=(_sds((SEQ, CONV_CH)), _sds((SEQ, 1024), MXU), _sds((SEQ, LANES), MXU),
                   _sds((1, LANES)), _sds((1, LANES)), _sds((1, 1024)), _sds((1, 1024))),
        grid=(_NCH // _PER_STEP,),
        in_specs=[pl.BlockSpec((_STEP_ROWS, CONV_CH), lambda c: (rev(c), 0)), pl.BlockSpec((_STEP_ROWS, 1024), lambda c: (rev(c), ZB // 1024)),
                  pl.BlockSpec((_STEP_ROWS, LANES), lambda c: (rev(c), DTC // LANES)),
                  pl.BlockSpec((_PER_STEP, BLK, 1024), lambda c: (rev(c), 0, 0)),
                  pl.BlockSpec((_STEP_ROWS, 1024), lambda c: (rev(c), 0))] + _ssd_param_specs(),
        out_specs=(pl.BlockSpec((_STEP_ROWS, CONV_CH), lambda c: (rev(c), 0)), pl.BlockSpec((_STEP_ROWS, 1024), lambda c: (rev(c), 0)),
                   pl.BlockSpec((_STEP_ROWS, LANES), lambda c: (rev(c), 0)),
                   _full((1, LANES)), _full((1, LANES)), _full((1, 1024)), _full((1, 1024))),
        scratch_shapes=[pltpu.VMEM((BLK, 1024), F32)], compiler_params=_cp())(xbc_act, proj, proj, hin, dyb, al16, dtb, dskx, nw)


def _rstd(v):
    return lax.rsqrt(jnp.mean(v * v, axis=1, keepdims=True) + EPS)


def _rms_bwd(dn, n, rstd):
    return rstd * (dn - n * jnp.mean(dn * n, axis=1, keepdims=True))


_VEC = _full((1, D))


def _layer_spec(layer):
    return pl.BlockSpec((None, 2048, D), lambda *_: (layer, 0, 0))

_ROW = pl.BlockSpec((TM, D), lambda i, *_: (i, 0))


def _proj_fwd(x, pre_w, scale, shift, w, layer, name):
    tn, ni = 1024, SEQ // TM

    def body(x_ref, pw_ref, sc_ref, sh_ref, w_ref, o_ref, h_ref, h_scr):
        rows = pl.ds(pl.multiple_of(pl.program_id(1) * TM, TM), TM)

        @pl.when(pl.program_id(0) == 0)
        def _():
            xv = x_ref[...]
            h = ((xv * _rstd(xv) * pw_ref[...]) * (1.0 + sc_ref[...]) + sh_ref[...]).astype(h_ref.dtype)
            h_scr[rows, :] = h
            h_ref[...] = h
        o_ref[...] = jnp.dot(h_scr[rows, :], w_ref[...].astype(MXU), preferred_element_type=F32)

    first_pass = pl.BlockSpec((TM, D), lambda j, i: (jnp.where(j == 0, i, ni - 1), 0))
    return pl.pallas_call(body, name=name, out_shape=(_sds((SEQ, NP)), _sds((SEQ, D), MXU)), grid=(NP // tn, ni),
                          in_specs=[first_pass, _VEC, _VEC, _VEC, pl.BlockSpec((None, D, tn), lambda j, i: (layer, 0, j))],
                          out_specs=(pl.BlockSpec((TM, tn), lambda j, i: (i, j)), first_pass),
                          scratch_shapes=[pltpu.VMEM((SEQ, D), MXU)], compiler_params=_cp())(x, pre_w, scale, shift, w)


_HALF = pl.BlockSpec((TM, 512), lambda i: (i, 0))
_Z_A = pl.BlockSpec((TM, 512), lambda i: (i, ZA // 512))
_Z_C = pl.BlockSpec((TM, 512), lambda i: (i, ZC // 512))


def _out_fwd(o_a, yb, o_c, proj, w, layer, x, gate, post_w, name):
    def body(oa_ref, yb_ref, oc_ref, za_ref, zc_ref, w_ref, x_ref, g_ref, pw_ref, xn_ref, y_ref):
        y = (_mm(oa_ref[...] * _silu(za_ref[...]), w_ref[0:512, :]) + _mm(yb_ref[...], w_ref[512:1536, :])
             + _mm(oc_ref[...] * _silu(zc_ref[...]), w_ref[1536:2048, :]))
        y_ref[...] = y
        xn_ref[...] = x_ref[...] + g_ref[...] * (y * _rstd(y) * pw_ref[...])

    return pl.pallas_call(body, name=name, out_shape=(_sds((SEQ, D)), _sds((SEQ, D))), grid=(SEQ // TM,),
                          in_specs=[_HALF, _ROW, _HALF, _Z_A, _Z_C, _layer_spec(layer), _ROW, _VEC, _VEC],
                          out_specs=(_ROW, _ROW), compiler_params=_cp())(o_a, yb, o_c, proj, proj, w, x, gate, post_w)


def _dymix(dxo, y, gate, post_w, w, layer, o_a, o_c, proj, name):
    def body(dx_ref, y_ref, g_ref, pw_ref, w_ref, oa_ref, oc_ref, za_ref, zc_ref,
             dy_ref, dg_ref, dpw_ref, doa_ref, dza_ref, b_ref, doc_ref, dzc_ref):
        @pl.when(pl.program_id(0) == 0)
        def _():
            dg_ref[...] = jnp.zeros_like(dg_ref)
            dpw_ref[...] = jnp.zeros_like(dpw_ref)
        for r in (slice(0, TM // 2), slice(TM // 2, TM)):
            dx, yv = dx_ref[r, :], y_ref[r, :]
            rstd = _rstd(yv)
            n = yv * rstd
            dg_ref[...] += jnp.sum(dx * (n * pw_ref[...]), axis=0, keepdims=True)
            dr = dx * g_ref[...]
            dpw_ref[...] += jnp.sum(dr * n, axis=0, keepdims=True)
            dy = _rms_bwd(dr * pw_ref[...], n, rstd)
            dy_ref[r, :] = dy
            b_ref[r, :] = _mm(dy, w_ref[512:1536, :], NT)
            for rows, o_ref, z_ref, do_ref, dz_ref in ((slice(0, 512), oa_ref, za_ref, doa_ref, dza_ref),
                                                       (slice(1536, 2048), oc_ref, zc_ref, doc_ref, dzc_ref)):
                dyg, z = _mm(dy, w_ref[rows, :], NT), z_ref[r, :]
                do_ref[r, :] = dyg * _silu(z)
                dz_ref[r, :] = (dyg * o_ref[r, :] * _dsilu(z)).astype(dz_ref.dtype)

    return pl.pallas_call(body, name=name,
                          out_shape=(_sds((SEQ, D)), _sds((1, D)), _sds((1, D)),
                                     _sds((SEQ, 512)), _sds((SEQ, 512), MXU), _sds((SEQ, D)), _sds((SEQ, 512)), _sds((SEQ, 512), MXU)),
                          grid=(SEQ // TM,), in_specs=[_ROW, _ROW, _VEC, _VEC, _layer_spec(layer), _HALF, _HALF, _Z_A, _Z_C],
                          out_specs=(_ROW, _VEC, _VEC, _HALF, _HALF, _ROW, _HALF, _HALF),
                          compiler_params=_cp())(dxo, y, gate, post_w, w, o_a, o_c, proj, proj)


def _dwout(o_a, yb, o_c, proj, dy, name):
    def body(oa_ref, yb_ref, oc_ref, za_ref, zc_ref, dy_ref, o_ref):
        @pl.when(pl.program_id(0) == 0)
        def _():
            o_ref[...] = jnp.zeros_like(o_ref)
        dy = dy_ref[...]
        o_ref[0:512, :] += _mm(oa_ref[...] * _silu(za_ref[...]), dy, TN)
        o_ref[512:1536, :] += _mm(yb_ref[...], dy, TN)
        o_ref[1536:2048, :] += _mm(oc_ref[...] * _silu(zc_ref[...]), dy, TN)

    return pl.pallas_call(body, name=name, out_shape=_sds((2048, D)), grid=(SEQ // TM,),
                          in_specs=[_HALF, _ROW, _HALF, _Z_A, _Z_C, _ROW], out_specs=_full((2048, D)),
                          compiler_params=_cp())(o_a, yb, o_c, proj, proj, dy)


def _dwin(h, pieces, name):
    n = len(pieces)
    widths = [p.shape[1] for p in pieces]
    half = NP // 2

    def body(*refs):
        h_ref, p_refs, o_ref = refs[0], refs[1:1 + n], refs[1 + n]

        @pl.when(pl.program_id(0) == 0)
        def _():
            o_ref[...] = jnp.zeros_like(o_ref)
        hv, c0 = h_ref[...], 0
        for p_ref, wd in zip(p_refs, widths):
            o_ref[:, c0:c0 + wd] += _mm(hv, p_ref[...], TN)
            c0 += wd

    return pl.pallas_call(body, name=name, out_shape=_sds((D, half)), grid=(SEQ // TM,),
                          in_specs=[_ROW] + [pl.BlockSpec((TM, wd), lambda k: (k, 0)) for wd in widths],
                          out_specs=_full((D, half)), compiler_params=_cp(56))(h, *pieces)


_TMH = 256


def _dh_bwd(pieces, w, x, pre_w, scale, dxo, name):
    n = len(pieces)
    widths = [p.shape[1] for p in pieces]

    def body(*refs):
        p_refs, (w_ref, x_ref, pw_ref, sc_ref, dxo_ref, dx_ref, dsh_ref, dsc_ref, dpw_ref) = refs[:n], refs[n:]

        @pl.when(pl.program_id(0) == 0)
        def _():
            for r in (dsh_ref, dsc_ref, dpw_ref):
                r[...] = jnp.zeros_like(r)
        dh, c0 = 0.0, 0
        for p_ref, wd in zip(p_refs, widths):
            dh = dh + _mm(p_ref[...], w_ref[:, c0:c0 + wd], NT)
            c0 += wd
        xv = x_ref[...]
        rstd = _rstd(xv)
        nrm = xv * rstd
        dsh_ref[...] += jnp.sum(dh, axis=0, keepdims=True)
        dsc_ref[...] += jnp.sum(dh * (nrm * pw_ref[...]), axis=0, keepdims=True)
        dhn = dh * (1.0 + sc_ref[...])
        dpw_ref[...] += jnp.sum(dhn * nrm, axis=0, keepdims=True)
        dx_ref[...] = _rms_bwd(dhn * pw_ref[...], nrm, rstd) + dxo_ref[...]

    row = pl.BlockSpec((_TMH, D), lambda i: (i, 0))
    return pl.pallas_call(body, name=name, out_shape=(_sds((SEQ, D)), _sds((1, D)), _sds((1, D)), _sds((1, D))),
                          grid=(SEQ // _TMH,),
                          in_specs=[pl.BlockSpec((_TMH, wd), lambda i: (i, 0)) for wd in widths]
                          + [pl.BlockSpec((None, D, NP), lambda i: (0, 0, 0)), row, _VEC, _VEC, row],
                          out_specs=(row, _VEC, _VEC, _VEC), compiler_params=_cp(56))(*pieces, w, x, pre_w, scale, dxo)


def _w_in_padded(land, name):
    rows = 128

    def body(l_ref, o_ref):
        o_ref[...] = _pad_cols(jnp.concatenate([l_ref[k] for k in range(4)], axis=1))

    return pl.pallas_call(body, name=name, out_shape=_sds((D, NP), land.dtype), grid=(D // rows,),
                          in_specs=[pl.BlockSpec((4, rows, SHARD_IN), lambda i: (0, i, 0))],
                          out_specs=pl.BlockSpec((rows, NP), lambda i: (i, 0)), compiler_params=_cp())(land)


def _grad_blocks(dwa, dwb, name):
    rows = 128

    def body(a_ref, b_ref, o_ref):
        g = _unpad_cols(jnp.concatenate([a_ref[...], b_ref[...]], axis=1))
        for k in range(4):
            o_ref[k] = g[:, SHARD_IN * k:SHARD_IN * (k + 1)].astype(o_ref.dtype)

    half = pl.BlockSpec((rows, NP // 2), lambda i: (i, 0))
    return pl.pallas_call(body, name=name, out_shape=_sds((4, D, SHARD_IN), jnp.bfloat16), grid=(D // rows,),
                          in_specs=[half, half], out_specs=pl.BlockSpec((4, rows, SHARD_IN), lambda i: (0, i, 0)),
                          compiler_params=_cp())(dwa, dwb)


def _loss_bwd(xf, tgt, name):
    def body(x_ref, t_ref, dx_ref, l_ref):
        @pl.when(pl.program_id(0) == 0)
        def _():
            l_ref[...] = jnp.zeros_like(l_ref)
        e = x_ref[...] - t_ref[...]
        dx_ref[...] = e * (1.0 / D)
        l_ref[...] += 0.5 * jnp.sum(jnp.mean(e * e, axis=1, keepdims=True), axis=0, keepdims=True)

    return pl.pallas_call(body, name=name, out_shape=(_sds((SEQ, D)), _sds((8, LANES))), grid=(SEQ // TM,),
                          in_specs=[_ROW, _ROW], out_specs=(_ROW, _full((8, LANES))), compiler_params=_cp())(xf, tgt)


def _mod_part(c_all, ada_w, ada_b, name):
    def body(c_ref, w_ref, b_ref, o_ref):
        o_ref[0] = _mm(_silu(c_ref[...]), w_ref[0]) + b_ref[0]

    return pl.pallas_call(body, name=name, out_shape=_sds((DEPTH, 8, 768)), grid=(DEPTH,),
                          in_specs=[_full((8, D)), pl.BlockSpec((1, D, 768), lambda i: (i, 0, 0)), pl.BlockSpec((1, 1, 768), lambda i: (i, 0, 0))],
                          out_specs=pl.BlockSpec((1, 8, 768), lambda i: (i, 0, 0)), compiler_params=_cp())(c_all, ada_w, ada_b)


def _ada_grad(c_t, dmod, name):
    def body(c_ref, d_ref, o_ref):
        ca = _silu(c_ref[...])
        dm = d_ref[0]
        acc = ca[:, 0:1] * dm[0:1, :]
        for s in range(1, 8):
            acc = acc + ca[:, s:s + 1] * dm[s:s + 1, :]
        o_ref[0] = acc

    return pl.pallas_call(body, name=name, out_shape=_sds((DEPTH, D, 768)), grid=(DEPTH,),
                          in_specs=[_full((D, LANES)), pl.BlockSpec((1, 8, 768), lambda i: (i, 0, 0))],
                          out_specs=pl.BlockSpec((1, D, 768), lambda i: (i, 0, 0)), compiler_params=_cp())(c_t, dmod)


def _pack(parts):
    flat = []
    for p in parts:
        f = p.reshape(-1)
        flat.append(jnp.pad(f, (0, (-f.size) % LANES)))
    v = jnp.concatenate(flat)
    return jnp.pad(v, (0, (-v.size) % (8 * LANES))).reshape(-1, LANES)


def _unpack(v, shapes):
    v = v.reshape(-1)
    out, off = [], 0
    for s in shapes:
        n = math.prod(s)
        out.append(v[off:off + n].reshape(s))
        off += n + (-n) % LANES
    return out


_GIVEN_DT, _GIVEN_C = 4608, 4624


def _pad_cols(w):
    return jnp.concatenate([w[..., :_GIVEN_DT], w[..., _GIVEN_C:], w[..., _GIVEN_DT:_GIVEN_C],
                            jnp.zeros(w.shape[:-1] + (NP - IN_COLS,), w.dtype)], axis=-1)


def _unpad_cols(w):
    return jnp.concatenate([w[..., :_GIVEN_DT], w[..., DTC:DTC + 16], w[..., _GIVEN_DT:DTC]], axis=-1)


def _pad_lanes(v):
    return jnp.pad(v, (0, LANES - v.shape[0])).reshape(1, LANES)


def _local_step(x2, tgt, mod, weights_of, grads_done, pre_w, post_w, conv_w, conv_b, dt_bias, a_log, d_skip, nw, sinks):
    saved = []
    xcur = x2
    for i in range(DEPTH):
        shift, scale, gate = mod[i:i + 1, :D], mod[i:i + 1, D:2 * D], mod[i:i + 1, 2 * D:]
        pw, qw = pre_w[i:i + 1], post_w[i:i + 1]
        w_p, w_o = weights_of(i, xcur)
        proj, h = _proj_fwd(xcur, pw, scale, shift, w_p, 0, "proj_fwd")
        o_a, lse_a = _attn_fwd(proj, QA // LANES, KA // LANES, VA // LANES, DILS, False, None, "attn_a_fwd")
        sink_x = jnp.repeat(sinks[i], HD).reshape(1, 512)
        o_c, lse_c = _attn_fwd(proj, QC // LANES, KC // LANES, VC // LANES, (1,), True, sink_x, "attn_c_fwd")
        cw, cb = conv_w[i], conv_b[i:i + 1]
        xbc_act = _conv_fwd(proj, cw, cb, "conv_fwd")
        ssd_p = (_pad_lanes(a_log[i]), _pad_lanes(dt_bias[i]), jnp.repeat(d_skip[i], HD).reshape(1, 1024), nw[i:i + 1])
        yb, hin = _ssd_fwd(xbc_act, proj, *ssd_p, "ssd_fwd")
        xnew, y = _out_fwd(o_a, yb, o_c, proj, w_o, 0, xcur, gate, qw, "out_fwd")
        saved.append((w_p, w_o, xcur, scale, gate, pw, qw, proj, h, o_a, lse_a, sink_x, o_c, lse_c, cw, cb, xbc_act, ssd_p, yb, hin, y))
        xcur = xnew
    dx, ltile = _loss_bwd(xcur, tgt, "loss")
    dmod, small = [None] * DEPTH, [None] * DEPTH
    for i in reversed(range(DEPTH)):
        w_p, w_o, xin, scale, gate, pw, qw, proj, h, o_a, lse_a, sink_x, o_c, lse_c, cw, cb, xbc_act, ssd_p, yb, hin, y = saved[i]
        dy, dgate, dpost, do_a, dz_a, dyb, do_c, dz_c = _dymix(dx, y, gate, qw, w_o, 0, o_a, o_c, proj, "dymix")
        dwo = _dwout(o_a, yb, o_c, proj, dy, "dwout")
        dq_a, dk_a, dv_a = _attn_bwd(proj, QA // LANES, KA // LANES, VA // LANES, do_a, o_a, lse_a, DILS, False, None, "attn_a_bwd")
        dq_c, dk_c, dv_c, dsk = _attn_bwd(proj, QC // LANES, KC // LANES, VC // LANES, do_c, o_c, lse_c, (1,), True, sink_x, "attn_c_bwd")
        dxbc_act, dz_b, ddt, dal16, ddtb, ddsk, dnw = _ssd_bwd(xbc_act, proj, hin, dyb, *ssd_p, "ssd_bwd")
        dxbc, dcw, dcb = _conv_bwd(proj, dxbc_act, cw, cb, "conv_bwd")
        half_a, half_b = [dq_a, dk_a, dv_a, dz_a, dz_b], [dxbc, dq_c, dz_c, dk_c, dv_c, ddt]
        sent = grads_done(i, _dwin(h, half_a, "dwin_a"), _dwin(h, half_b, "dwin_b"), dwo)
        dx, dshift, dscale, dpre = _dh_bwd(half_a + half_b, w_p, xin, pw, scale + sent[0, 0], dx, "dh_bwd")
        dmod[i] = jnp.concatenate([dshift, dscale, dgate], axis=1)
        small[i] = (dpre, dpost, dcw, dcb, ddtb[0, :16], dal16[0, :16], ddsk.reshape(16, HD).sum(axis=1), dnw, dsk[:, 0, ::HD].reshape(8))
    return ltile, dx, jnp.concatenate(dmod, axis=0), small


_SMALL = ((1, D), (1, D), (4, CONV_CH), (1, CONV_CH), (16,), (16,), (16,), (1, D), (8,))


def kernel(x, c, ada_w, ada_b, pre_norm_w, post_norm_w, w_in, conv_w, conv_b, dt_bias, a_log, d_skip, ssm_norm_w, sinks, w_out, loss_target, m_ada_w, m_ada_b, m_pre_norm_w, m_post_norm_w, m_w_in, m_conv_w, m_conv_b, m_dt_bias, m_a_log, m_d_skip, m_ssm_norm_w, m_sinks, m_w_out, v_ada_w, v_ada_b, v_pre_norm_w, v_post_norm_w, v_w_in, v_conv_w, v_conv_b, v_dt_bias, v_a_log, v_d_skip, v_ssm_norm_w, v_sinks, v_w_out):
    xi, yi, ci = lax.axis_index("x"), lax.axis_index("y"), lax.axis_index("c")
    chip = 2 * xi + yi
    me = 2 * chip + ci

    w_in_b = _cast_bf16(w_in, 512, "cast_w_in")
    w_out_b = _cast_bf16(w_out, 512, "cast_w_out")
    gathers = []
    for i in range(DEPTH):
        lands = [lax.dynamic_update_slice(lax.empty((4,) + a.shape[1:], a.dtype), a[i][None], (chip, 0, 0)) for a in (w_in_b, w_out_b)]
        gathers.append(_split_start(None, lands, f"gather_start{i}", "half" if i == 0 else "whole"))
    all_started = gathers[0][3] + gathers[1][3] + gathers[2][3] + gathers[3][3]

    def weights_of(i, after):
        send_sems, recv_sems, thru, _ = gathers[i]
        if i == 0:
            halves = _split_wait(send_sems, recv_sems, thru, 2, all_started + mod[:1, :LANES], "gather_wait0", "half")
            send_sems, recv_sems, thru, after = _split_start(None, halves, "share_start0", "sibling")
            g_in, g_out = _split_wait(send_sems, recv_sems, thru, 2, after, "share_wait0", "sibling")
        else:
            g_in, g_out = _split_wait(send_sems, recv_sems, thru, 2, after, f"gather_wait{i}")
        return _w_in_padded(g_in, "w_in_padded")[None], g_out.reshape(1, 2048, D)

    scatters = [None] * DEPTH

    def grads_done(i, dwa, dwb, dwo):
        blocks = [_grad_blocks(dwa, dwb, "grad_blocks"), _cast_bf16(dwo.reshape(4, 512, D), 512, "cast_dw_out")]
        scatters[i] = _split_start(blocks, [lax.empty(b.shape, b.dtype) for b in blocks], f"scatter_start{i}")
        return scatters[i][3]

    g0 = _allgather8(_pack([c, conv_w]), "gather_c")
    c_all = g0[:, :8, :].reshape(8, D)
    conv_w_full = jnp.concatenate([g0[2 * k, 8:56, :].reshape(DEPTH, 4, CONV_CH // 4) for k in range(4)], axis=-1)

    ada_b_mine = lax.dynamic_slice_in_dim(ada_b, 768 * chip, 768, axis=1).reshape(DEPTH, 1, 768)
    gm = _allgather8(_mod_part(c_all, ada_w, ada_b_mine, "mod_part").reshape(DEPTH * 8, 768), "gather_mod")
    gm = gm.reshape(4, 2, DEPTH, 8, 768)[:, 0]
    mod = lax.dynamic_index_in_dim(gm, me, axis=2, keepdims=False).transpose(1, 0, 2).reshape(DEPTH, 3 * D)

    ltile, dx, dmod, small = _local_step(x[0], loss_target[0], mod, weights_of, grads_done, pre_norm_w, post_norm_w, conv_w_full,
                                         conv_b, dt_bias, a_log, d_skip, ssm_norm_w, sinks)

    packed = _pack([dmod] + [g for layer in small for g in layer] + [ltile[0]])
    gs = _allgather8(packed, "gather_small")
    tot = _sum_blocks(gs[:, None], packed.shape[0], "sum_small")[0]
    parts = _unpack(tot, [(DEPTH, 3 * D)] + list(_SMALL) * DEPTH + [(LANES,)])
    g_ada_b, loss = parts[0], parts[-1][0]
    per_layer = [parts[1 + len(_SMALL) * i:1 + len(_SMALL) * (i + 1)] for i in range(DEPTH)]
    g_pre, g_post, g_cw, g_cb, g_dtb, g_al, g_dsk, g_nw, g_sk = [jnp.stack([per_layer[i][j] for i in range(DEPTH)]) for j in range(len(_SMALL))]
    g_pre, g_post, g_cb, g_nw = g_pre[:, 0], g_post[:, 0], g_cb[:, 0], g_nw[:, 0]
    g_cw = lax.dynamic_slice_in_dim(g_cw, (CONV_CH // 4) * chip, CONV_CH // 4, axis=2)

    dmod_all = gs[:, :(DEPTH * 3 * D) // LANES, :].reshape(8, DEPTH, 3 * D).transpose(1, 0, 2)
    dmod_mine = lax.dynamic_slice_in_dim(dmod_all, 768 * chip, 768, axis=2)
    c_t = jnp.pad(c_all.T, ((0, 0), (0, LANES - 8)))
    g_ada_w = _ada_grad(c_t, dmod_mine, "ada_grad")

    res = {}
    res["ada_w"] = _adamw(ada_w, [g_ada_w], m_ada_w, v_ada_w, 512, "adamw_ada_w")
    names = ["ada_b", "pre_norm_w", "post_norm_w", "conv_w", "conv_b", "dt_bias", "a_log", "d_skip", "ssm_norm_w", "sinks"]
    ws = [ada_b, pre_norm_w, post_norm_w, conv_w, conv_b, dt_bias, a_log, d_skip, ssm_norm_w, sinks]
    gsm = [g_ada_b, g_pre, g_post, g_cw, g_cb, g_dtb, g_al, g_dsk, g_nw, g_sk]
    ms = [m_ada_b, m_pre_norm_w, m_post_norm_w, m_conv_w, m_conv_b, m_dt_bias, m_a_log, m_d_skip, m_ssm_norm_w, m_sinks]
    vs = [v_ada_b, v_pre_norm_w, v_post_norm_w, v_conv_w, v_conv_b, v_dt_bias, v_a_log, v_d_skip, v_ssm_norm_w, v_sinks]
    pw_, pg_, pm_, pv_ = _pack(ws), _pack(gsm), _pack(ms), _pack(vs)
    small_out = _adamw(pw_[None], [pg_[None]], pm_[None], pv_[None], pw_.shape[0], "adamw_small")

    others_done = small_out[1][0, :8] + res["ada_w"][1][0, :8, :LANES]
    landed = [_split_wait(*scatters[i][:3], 2, others_done, f"scatter_wait{i}") for i in range(DEPTH)]
    p_in = _sum_chips([d[2] for d in landed], [d[0] for d in landed], 128, "sum_w_in")
    p_out = _sum_chips([d[3] for d in landed], [d[1] for d in landed], 256, "sum_w_out")
    col_major, row_major = (lambda a: jnp.transpose(a, (2, 0, 1))), (lambda a: jnp.transpose(a, (1, 2, 0)))
    p_in = col_major(p_in)
    s_in, s_out = _sibling_swap([p_in, p_out], "swap_partials")
    res["w_in"] = [row_major(a) for a in _adamw(col_major(w_in), [p_in, s_in], col_major(m_w_in), col_major(v_w_in), None,
                                                "adamw_w_in", lead=SHARD_IN // 18)]
    res["w_out"] = _adamw(w_out, [p_out, s_out], m_w_out, v_w_out, 512, "adamw_w_out")
    shapes = [w.shape for w in ws]
    for kind in range(4):
        for nm, a in zip(names, _unpack(small_out[kind][0], shapes)):
            res.setdefault(nm, [None] * 4)[kind] = a
    order = ["ada_w", "ada_b", "pre_norm_w", "post_norm_w", "w_in", "conv_w", "conv_b", "dt_bias", "a_log", "d_skip", "ssm_norm_w", "sinks", "w_out"]
    return (loss, dx[None], *[res[n][0] for n in order], *[res[n][1] for n in order], *[res[n][2] for n in order], *[res[n][3] for n in order])
```

```python
import math

import jax
import jax.numpy as jnp
from jax import lax
from jax.experimental import pallas as pl
from jax.experimental.pallas import tpu as pltpu

F32 = jnp.float32
MXU = jnp.bfloat16
HI = lax.Precision.HIGHEST
MESH = pl.DeviceIdType.MESH

SEQ = 4096
D = 1024
DEPTH = 4
HD = 64
QK_SCALE = HD ** -0.5
LANES = 128
BLK = 128
DILS = (1, 4, 16)
NEG = -1e30
EPS = 1e-6
MIB = 1024 * 1024

NP = 6144
QA, KA, VA, ZA = 0, 512, 1024, 1536
ZB, XBC = 2048, 3072
QC, ZC, KC, VC = 4608, 5120, 5632, 5760
DTC = 5888
IN_COLS = 5904
SHARD_IN = IN_COLS // 4
CONV_CH = 1536
TM = 512

ADAM_LR, ADAM_B1, ADAM_B2, ADAM_EPS, ADAM_WD, ADAM_STEP = 0.001, 0.9, 0.999, 1e-08, 0.01, 10

NT = (((1,), (1,)), ((), ()))
TN = (((0,), (0,)), ((), ()))


def _cp(vmem_mib=48):
    return pltpu.CompilerParams(vmem_limit_bytes=vmem_mib * MIB)


def _sds(shape, dtype=F32):
    return jax.ShapeDtypeStruct(shape, dtype)


def _full(shape):
    n = len(shape)
    return pl.BlockSpec(shape, lambda *_: (0,) * n)


def _mm(a, b, dims=None):
    if dims is None:
        return jnp.dot(a.astype(MXU), b.astype(MXU), preferred_element_type=F32)
    return lax.dot_general(a.astype(MXU), b.astype(MXU), dims, preferred_element_type=F32)


def _sigmoid(x):
    return 1.0 / (1.0 + jnp.exp(-x))


def _silu(x):
    return x * _sigmoid(x)


def _dsilu(x):
    s = _sigmoid(x)
    return s * (1.0 + x * (1.0 - s))


def _softplus(x):
    ax = jnp.where(x >= 0, x, -x)
    return jnp.maximum(x, 0.0) + jnp.log1p(jnp.exp(-ax))


def _half_masks():
    lane = lax.broadcasted_iota(jnp.int32, (1, LANES), 1)
    m0 = (lane < HD).astype(F32)
    return m0, 1.0 - m0


def _allgather8(v, name):
    r, cc = v.shape

    def body(v_ref, out_ref, send_sems, recv_sems):
        x, y, c = lax.axis_index("x"), lax.axis_index("y"), lax.axis_index("c")
        me = 4 * x + 2 * y + c
        out_ref[me] = v_ref[...]
        peers = []
        for k in range(1, 8):
            px = 1 - x if k & 4 else x
            py = 1 - y if k & 2 else y
            pc = 1 - c if k & 1 else c
            peers.append((px, py, pc))
        sends = []
        for k, peer in enumerate(peers):
            cp = pltpu.make_async_remote_copy(src_ref=v_ref, dst_ref=out_ref.at[me], send_sem=send_sems.at[k],
                                              recv_sem=recv_sems.at[k], device_id=peer, device_id_type=MESH)
            cp.start()
            sends.append(cp)
        for k, (px, py, pc) in enumerate(peers):
            pltpu.make_async_remote_copy(src_ref=v_ref, dst_ref=out_ref.at[4 * px + 2 * py + pc], send_sem=send_sems.at[k],
                                         recv_sem=recv_sems.at[k], device_id=(px, py, pc), device_id_type=MESH).wait_recv()
        for cp in sends:
            cp.wait_send()

    return pl.pallas_call(
        body, name=name, out_shape=_sds((8, r, cc)),
        in_specs=[pl.BlockSpec(memory_space=pltpu.VMEM)], out_specs=pl.BlockSpec(memory_space=pltpu.VMEM),
        scratch_shapes=[pltpu.SemaphoreType.DMA((7,)), pltpu.SemaphoreType.DMA((7,))],
        compiler_params=_cp(32),
    )(v)


_HBM = pl.BlockSpec(memory_space=pltpu.HBM)
_SEM = pl.BlockSpec(memory_space=pltpu.SEMAPHORE)
_EFFECT = pltpu.SideEffectType.DATAFLOW_SIDE_EFFECTING


def _chip_copies(src_refs, land_refs, send_sems, recv_sems, part="whole"):
    x, y, c = lax.axis_index("x"), lax.axis_index("y"), lax.axis_index("c")
    mine = 2 * x + y
    out = []
    for i, land in enumerate(land_refs):
        half = land.shape[1] // 2
        own, others = pl.ds(pl.multiple_of(c * half, half), half), pl.ds(pl.multiple_of((1 - c) * half, half), half)
        for j, (px, py) in enumerate([(1 - x, y), (x, 1 - y), (1 - x, 1 - y)]):
            slot, peer = 2 * px + py, (px, py, c)
            if part == "whole":
                src = src_refs[i].at[slot] if src_refs else land.at[mine]
                there, here = land.at[mine], land.at[slot]
            elif part == "half":
                src = there = land.at[mine].at[own]
                here = land.at[slot].at[own]
            else:
                src = there = land.at[slot].at[own]
                here, peer = land.at[slot].at[others], (x, y, 1 - c)
            mk = lambda dst, i=i, j=j, src=src, peer=peer: pltpu.make_async_remote_copy(
                src_ref=src, dst_ref=dst, send_sem=send_sems.at[3 * i + j], recv_sem=recv_sems.at[3 * i + j],
                device_id=peer, device_id_type=MESH)
            out.append((mk(there), mk(here)))
    return out


def _split_start(srcs, lands, name, part="whole"):
    ops = list(srcs or []) + list(lands)
    ns, n = len(srcs or []), len(lands)

    def body(*refs):
        src_refs, land_refs = refs[:ns], refs[ns:ns + n]
        send_sems, recv_sems = refs[ns + n], refs[ns + n + 1]
        for mine_out, _ in _chip_copies(src_refs, land_refs, send_sems, recv_sems, part):
            mine_out.start()
        refs[-1][...] = jnp.zeros_like(refs[-1])

    sems = pltpu.SemaphoreType.DMA((3 * n,))
    res = pl.pallas_call(
        body, name=name, out_shape=(sems, sems) + tuple(pltpu.HBM(a.shape, a.dtype) for a in ops) + (_sds((8, LANES)),),
        in_specs=[_HBM] * len(ops), out_specs=(_SEM, _SEM) + (_HBM,) * len(ops) + (pl.BlockSpec(memory_space=pltpu.VMEM),),
        input_output_aliases={k: 2 + k for k in range(len(ops))},
        compiler_params=pltpu.CompilerParams(has_side_effects=_EFFECT),
    )(*[pltpu.with_memory_space_constraint(a, pltpu.HBM) for a in ops])
    return res[0], res[1], list(res[2:2 + len(ops)]), res[-1]


def _split_wait(send_sems, recv_sems, thru, n, after, name, part="whole"):
    ns = len(thru) - n

    def body(*refs):
        src_refs, land_refs = refs[:ns], refs[ns:ns + n]
        for mine_out, arriving in _chip_copies(src_refs, land_refs, refs[ns + n], refs[ns + n + 1], part):
            mine_out.wait_send()
            arriving.wait_recv()

    res = pl.pallas_call(
        body, name=name, out_shape=tuple(pltpu.HBM(a.shape, a.dtype) for a in thru),
        in_specs=[_HBM] * len(thru) + [_SEM, _SEM, pl.BlockSpec(memory_space=pl.ANY)], out_specs=(_HBM,) * len(thru),
        input_output_aliases={k: k for k in range(len(thru))},
        compiler_params=pltpu.CompilerParams(has_side_effects=_EFFECT),
    )(*thru, send_sems, recv_sems, after)
    return list(res)


def _sibling_swap(arrs, name):
    n = len(arrs)

    def body(*refs):
        ins, outs_, (send_sems, recv_sems) = refs[:n], refs[n:2 * n], refs[2 * n:]
        sib = (lax.axis_index("x"), lax.axis_index("y"), 1 - lax.axis_index("c"))
        cps = [pltpu.make_async_remote_copy(src_ref=ins[i], dst_ref=outs_[i], send_sem=send_sems.at[i], recv_sem=recv_sems.at[i],
                                            device_id=sib, device_id_type=MESH) for i in range(n)]
        for cp in cps:
            cp.start()
        for cp in cps:
            cp.wait_recv()
        for cp in cps:
            cp.wait_send()

    hbm = pl.BlockSpec(memory_space=pltpu.HBM)
    return pl.pallas_call(
        body, name=name, out_shape=tuple(_sds(a.shape, a.dtype) for a in arrs), in_specs=[hbm] * n, out_specs=tuple([hbm] * n),
        scratch_shapes=[pltpu.SemaphoreType.DMA((n,)), pltpu.SemaphoreType.DMA((n,))],
    )(*arrs)


def _tile_spec(rows, cc):
    return pl.BlockSpec((None, rows, cc), lambda l, i: (l, i, 0))


def _cast_bf16(a, rows, name):
    nl, r, cc = a.shape

    def body(a_ref, o_ref):
        o_ref[...] = a_ref[...].astype(jnp.bfloat16)

    return pl.pallas_call(body, name=name, out_shape=_sds((nl, r, cc), jnp.bfloat16), grid=(nl, r // rows),
                          in_specs=[_tile_spec(rows, cc)], out_specs=_tile_spec(rows, cc), compiler_params=_cp())(a)


def _sum_blocks(a, rows, name):
    k, nl, r, cc = a.shape

    def body(a_ref, o_ref):
        acc = a_ref[0].astype(F32)
        for j in range(1, k):
            acc = acc + a_ref[j].astype(F32)
        o_ref[...] = acc

    return pl.pallas_call(body, name=name, out_shape=_sds((nl, r, cc)), grid=(nl, r // rows),
                          in_specs=[pl.BlockSpec((k, None, rows, cc), lambda l, i: (0, l, i, 0))],
                          out_specs=_tile_spec(rows, cc), compiler_params=_cp())(a)


def _sum_chips(lands, srcs, rows, name):
    nl = len(lands)
    _, r, cc = lands[0].shape

    def body(*refs):
        land_refs, src_refs, o_ref = refs[:nl], refs[nl:2 * nl], refs[2 * nl]
        mine = 2 * lax.axis_index("x") + lax.axis_index("y")
        for j in range(nl):
            @pl.when(pl.program_id(0) == j)
            def _(j=j):
                own = src_refs[j][mine].astype(F32)
                acc = None
                for k in range(4):
                    term = jnp.where(mine == k, own, land_refs[j][k].astype(F32))
                    acc = term if acc is None else acc + term
                o_ref[...] = acc

    specs = [pl.BlockSpec((4, rows, cc), lambda l, i, j=j: (0, jnp.where(l == j, i, 0), 0)) for j in range(nl)]
    return pl.pallas_call(body, name=name, out_shape=_sds((nl, r, cc)), grid=(nl, r // rows),
                          in_specs=specs + specs, out_specs=_tile_spec(rows, cc), compiler_params=_cp())(*lands, *srcs)


def _adamw(w, parts, m, v, rows, name, lead=None):
    nl, r, cc = w.shape
    np_ = len(parts)
    c1 = 1.0 / (1.0 - ADAM_B1 ** ADAM_STEP)
    c2 = 1.0 / (1.0 - ADAM_B2 ** ADAM_STEP)

    def body(*refs):
        w_ref, p_refs, (m_ref, v_ref, g_ref, d_ref, nm_ref, nv_ref) = refs[0], refs[1:1 + np_], refs[1 + np_:]
        g = p_refs[0][...]
        for p_ref in p_refs[1:]:
            g = g + p_ref[...]
        nm = ADAM_B1 * m_ref[...] + (1.0 - ADAM_B1) * g
        nv = ADAM_B2 * v_ref[...] + (1.0 - ADAM_B2) * (g * g)
        g_ref[...] = g
        nm_ref[...] = nm
        nv_ref[...] = nv
        d_ref[...] = -ADAM_LR * ((nm * c1) / (jnp.sqrt(nv * c2) + ADAM_EPS) + ADAM_WD * w_ref[...])

    if lead is None:
        spec, grid = _tile_spec(rows, cc), (nl, r // rows)
    else:
        spec, grid = pl.BlockSpec((lead, r, cc), lambda i: (i, 0, 0)), (nl // lead,)
    return pl.pallas_call(body, name=name, out_shape=(_sds((nl, r, cc)),) * 4, grid=grid,
                          in_specs=[spec] * (3 + np_), out_specs=(spec,) * 4, compiler_params=_cp())(w, *parts, m, v)


_BIAS = pltpu.VMEM((2, 2 * BLK, 2 * BLK), F32)


def _fill_band_bias(bias_ref):
    qi = lax.broadcasted_iota(jnp.int32, (2 * BLK, 2 * BLK), 0) & (BLK - 1)
    kj = lax.broadcasted_iota(jnp.int32, (2 * BLK, 2 * BLK), 1)
    dist = BLK + qi - kj
    band = (dist >= 0) & (dist <= BLK)
    bias_ref[0] = jnp.where(band, 0.0, NEG)
    bias_ref[1] = jnp.where(band & (kj >= BLK), 0.0, NEG)


class _HeadStack:
    def __init__(self, group):
        self.m0, self.m1 = _half_masks()
        self.group = group
        if group is not None:
            self.kv_mask = (self.m0, self.m1)[group]

    def _swap_half(self, t, a):
        return t if a == self.group else pltpu.roll(t, HD, axis=1)

    def stack(self, t):
        t0, t1 = t * self.m0, t * self.m1
        if self.group is not None:
            t0, t1 = self._swap_half(t0, 0), self._swap_half(t1, 1)
        return jnp.concatenate([t0, t1], axis=0)

    def unstack(self, ts):
        if self.group is None:
            return ts[:BLK] * self.m0 + ts[BLK:] * self.m1
        return self._swap_half(ts[:BLK] * self.kv_mask, 0) + self._swap_half(ts[BLK:] * self.kv_mask, 1)


def _rows(st, dil):
    if dil == 1:
        return pl.ds(pl.multiple_of(st, BLK), BLK)
    return pl.ds(st, BLK, stride=dil)


def _block_pos(n, dil):
    nb = SEQ // (dil * BLK)
    r, b = n // nb, n % nb
    hp = (b > 0).astype(jnp.int32)
    st = r + dil * BLK * b
    return st, st - dil * BLK * hp, 1 - hp


def _attn_fwd(proj, qblk, kblk, vblk, dils, gqa, sink_x, name):
    has_sink = sink_x is not None

    def body(*refs):
        if has_sink:
            q_ref, k_ref, v_ref, s_ref, o_ref, lse_ref, m_scr, z_scr, bias_scr = refs
        else:
            q_ref, k_ref, v_ref, o_ref, lse_ref, m_scr, z_scr, bias_scr = refs

        @pl.when(pl.program_id(0) == 0)
        def _():
            _fill_band_bias(bias_scr)
        o_ref[...] = jnp.zeros_like(o_ref)
        if has_sink:
            z_scr[...] = jnp.ones_like(z_scr)
            m_scr[...] = jnp.broadcast_to(s_ref[...], m_scr.shape)
        else:
            z_scr[...] = jnp.zeros_like(z_scr)
            m_scr[...] = jnp.full_like(m_scr, NEG)

        def step(n, carry, dil, heads):
            m0, m1 = heads.m0, heads.m1
            st, stp, first = _block_pos(n, dil)
            rq, rp = _rows(st, dil), _rows(stp, dil)
            kk = jnp.concatenate([k_ref[rp, :], k_ref[rq, :]], axis=0)
            vv = jnp.concatenate([v_ref[rp, :], v_ref[rq, :]], axis=0)
            s = _mm(heads.stack(q_ref[rq, :] * QK_SCALE), kk, NT) + bias_scr[first]
            m = jnp.max(s, axis=1, keepdims=True)
            p = jnp.exp(s - m)
            l = jnp.sum(p, axis=1, keepdims=True)
            o_pair = heads.unstack(_mm(p, vv))
            m_pair = m[:BLK] * m0 + m[BLK:] * m1
            l_pair = l[:BLK] * m0 + l[BLK:] * m1
            m_old = m_scr[rq, :]
            m_new = jnp.maximum(m_old, m_pair)
            alpha, beta = jnp.exp(m_old - m_new), jnp.exp(m_pair - m_new)
            o_ref[rq, :] = o_ref[rq, :] * alpha + o_pair * beta
            z_scr[rq, :] = z_scr[rq, :] * alpha + l_pair * beta
            m_scr[rq, :] = m_new
            return carry

        def blocks(heads):
            for dil in dils:
                lax.fori_loop(0, SEQ // BLK, lambda n, carry, dil=dil: step(n, carry, dil, heads), 0, unroll=8)

        if gqa:
            for grp in range(2):
                pl.when(pl.program_id(0) // 2 == grp)(lambda grp=grp: blocks(_HeadStack(grp)))
        else:
            blocks(_HeadStack(None))

        def fin(t, carry):
            rt = pl.ds(pl.multiple_of(t * TM, TM), TM)
            z = z_scr[rt, :]
            o_ref[rt, :] = o_ref[rt, :] / z
            lse_ref[rt, :] = m_scr[rt, :] + jnp.log(z)
            return carry
        lax.fori_loop(0, SEQ // TM, fin, 0)

    col = lambda blk: pl.BlockSpec((SEQ, LANES), lambda p, blk=blk: (0, blk + p))
    kv = (lambda blk: pl.BlockSpec((SEQ, LANES), lambda p, blk=blk: (0, blk))) if gqa else col
    in_specs = [col(qblk), kv(kblk), kv(vblk)]
    args = [proj, proj, proj]
    if has_sink:
        in_specs.append(pl.BlockSpec((1, LANES), lambda p: (0, p)))
        args.append(sink_x)
    out = pl.BlockSpec((SEQ, LANES), lambda p: (0, p))
    return pl.pallas_call(body, name=name, out_shape=(_sds((SEQ, 512)), _sds((SEQ, 512))), grid=(4,),
                          in_specs=in_specs, out_specs=(out, out),
                          scratch_shapes=[pltpu.VMEM((SEQ, LANES), F32), pltpu.VMEM((SEQ, LANES), F32), _BIAS],
                          compiler_params=_cp(48))(*args)


def _attn_bwd(proj, qblk, kblk, vblk, do, o, lse, dils, gqa, sink_x, name):
    has_sink = sink_x is not None

    def body(*refs):
        if has_sink:
            q_ref, k_ref, v_ref, do_ref, o_ref, lse_ref, s_ref, dq_ref, dk_ref, dv_ref, ds_ref, bias_scr = refs
        else:
            q_ref, k_ref, v_ref, do_ref, o_ref, lse_ref, dq_ref, dk_ref, dv_ref, bias_scr = refs
        pid = pl.program_id(0)

        @pl.when(pid == 0)
        def _():
            _fill_band_bias(bias_scr)
        dq_ref[...] = jnp.zeros_like(dq_ref)
        if gqa:
            @pl.when(pid == 0)
            def _():
                dk_ref[...] = jnp.zeros_like(dk_ref)
                dv_ref[...] = jnp.zeros_like(dv_ref)
        else:
            dk_ref[...] = jnp.zeros_like(dk_ref)
            dv_ref[...] = jnp.zeros_like(dv_ref)

        def step(n, carry, dil, heads):
            m0, m1 = heads.m0, heads.m1
            st, stp, first = _block_pos(n, dil)
            rq, rp = _rows(st, dil), _rows(stp, dil)
            do_, lse_ = do_ref[rq, :], lse_ref[rq, :]
            kk = jnp.concatenate([k_ref[rp, :], k_ref[rq, :]], axis=0)
            vv = jnp.concatenate([v_ref[rp, :], v_ref[rq, :]], axis=0)
            qs, dos = heads.stack(q_ref[rq, :] * QK_SCALE), heads.stack(do_)
            doo = do_ * o_ref[rq, :]
            delta = jnp.concatenate([jnp.sum(doo * m0, axis=1, keepdims=True), jnp.sum(doo * m1, axis=1, keepdims=True)], axis=0)
            lse_s = jnp.concatenate([lse_[:, 0:1], lse_[:, HD:HD + 1]], axis=0)
            p = jnp.exp(_mm(qs, kk, NT) + bias_scr[first] - lse_s)
            ds = p * (_mm(dos, vv, NT) - delta)
            dq_ref[rq, :] += heads.unstack(_mm(ds, kk)) * QK_SCALE
            dk_sum, dv_sum = _mm(ds, qs, TN), _mm(p, dos, TN)
            dk_ref[rp, :] += dk_sum[:BLK]
            dk_ref[rq, :] += dk_sum[BLK:]
            dv_ref[rp, :] += dv_sum[:BLK]
            dv_ref[rq, :] += dv_sum[BLK:]
            return carry

        def blocks(heads):
            for dil in dils:
                lax.fori_loop(0, SEQ // BLK, lambda n, carry, dil=dil: step(n, carry, dil, heads), 0, unroll=4)

        if gqa:
            for grp in range(2):
                pl.when(pid // 2 == grp)(lambda grp=grp: blocks(_HeadStack(grp)))
        else:
            blocks(_HeadStack(None))

        if has_sink:
            m0, m1 = _half_masks()

            def sink_rows(t, acc):
                rt = pl.ds(pl.multiple_of(t * TM, TM), TM)
                return acc - jnp.sum(jnp.exp(s_ref[...] - lse_ref[rt, :]) * (do_ref[rt, :] * o_ref[rt, :]), axis=0, keepdims=True)
            acc = lax.fori_loop(0, SEQ // TM, sink_rows, jnp.zeros((1, LANES), F32))
            per_head = jnp.sum(acc * m0, axis=1, keepdims=True) * m0 + jnp.sum(acc * m1, axis=1, keepdims=True) * m1
            ds_ref[0] = jnp.broadcast_to(per_head, (8, LANES))

    col = lambda blk: pl.BlockSpec((SEQ, LANES), lambda p, blk=blk: (0, blk + p))
    kv = (lambda blk: pl.BlockSpec((SEQ, LANES), lambda p, blk=blk: (0, blk))) if gqa else col
    pair = pl.BlockSpec((SEQ, LANES), lambda p: (0, p))
    in_specs = [col(qblk), kv(kblk), kv(vblk), pair, pair, pair]
    args = [proj, proj, proj, do, o, lse]
    kvw = LANES if gqa else 512
    kv_out = pl.BlockSpec((SEQ, LANES), lambda p: (0, 0)) if gqa else pair
    out_shape = [_sds((SEQ, 512)), _sds((SEQ, kvw)), _sds((SEQ, kvw))]
    out_specs = [pair, kv_out, kv_out]
    if has_sink:
        in_specs.append(pl.BlockSpec((1, LANES), lambda p: (0, p)))
        args.append(sink_x)
        out_shape.append(_sds((4, 8, LANES)))
        out_specs.append(pl.BlockSpec((1, 8, LANES), lambda p: (p, 0, 0)))
    return pl.pallas_call(body, name=name, out_shape=tuple(out_shape), grid=(4,), in_specs=in_specs,
                          out_specs=tuple(out_specs), scratch_shapes=[_BIAS], compiler_params=_cp(56))(*args)


_CT = 128


def _rows_before(x_ref, t, k):
    if t == 0:
        return jnp.concatenate([jnp.zeros((k, LANES), F32), x_ref[0:_CT - k, :]], axis=0)
    return x_ref[t * _CT - k:(t + 1) * _CT - k, :]


def _conv_pre(x_ref, w_ref, b_ref, t):
    taps = [x_ref[t * _CT:(t + 1) * _CT, :]] + [_rows_before(x_ref, t, k) for k in range(1, 4)]
    u = b_ref[...] + taps[0] * w_ref[3:4, :]
    for k in range(1, 4):
        u = u + taps[k] * w_ref[3 - k:4 - k, :]
    return u, taps


def _conv_fwd(proj, w, b, name):
    def body(x_ref, w_ref, b_ref, o_ref):
        for t in range(SEQ // _CT):
            o_ref[t * _CT:(t + 1) * _CT, :] = _silu(_conv_pre(x_ref, w_ref, b_ref, t)[0])

    nblk = CONV_CH // LANES
    return pl.pallas_call(body, name=name, out_shape=_sds((SEQ, CONV_CH)), grid=(nblk,),
                          in_specs=[pl.BlockSpec((SEQ, LANES), lambda j: (0, XBC // LANES + j)),
                                    pl.BlockSpec((4, LANES), lambda j: (0, j)), pl.BlockSpec((1, LANES), lambda j: (0, j))],
                          out_specs=pl.BlockSpec((SEQ, LANES), lambda j: (0, j)), compiler_params=_cp())(proj, w, b)


def _conv_bwd(proj, dact, w, b, name):
    def body(x_ref, da_ref, w_ref, b_ref, dx_ref, dw_ref, db_ref, du_scr):
        du_scr[SEQ:SEQ + 8, :] = jnp.zeros((8, LANES), F32)
        db = jnp.zeros((1, LANES), F32)
        dws = [jnp.zeros((1, LANES), F32)] * 4
        for t in range(SEQ // _CT):
            u, taps = _conv_pre(x_ref, w_ref, b_ref, t)
            du = da_ref[t * _CT:(t + 1) * _CT, :] * _dsilu(u)
            du_scr[t * _CT:(t + 1) * _CT, :] = du
            db = db + jnp.sum(du, axis=0, keepdims=True)
            dws = [dws[k] + jnp.sum(du * taps[k], axis=0, keepdims=True) for k in range(4)]
        db_ref[...] = db
        for k in range(4):
            dw_ref[3 - k:4 - k, :] = dws[k]
        for t in range(SEQ // _CT):
            dx = du_scr[t * _CT:(t + 1) * _CT, :] * w_ref[3:4, :]
            for k in range(1, 4):
                dx = dx + du_scr[t * _CT + k:(t + 1) * _CT + k, :] * w_ref[3 - k:4 - k, :]
            dx_ref[t * _CT:(t + 1) * _CT, :] = dx.astype(dx_ref.dtype)

    nblk = CONV_CH // LANES
    blk = pl.BlockSpec((SEQ, LANES), lambda j: (0, j))
    wspec, bspec = pl.BlockSpec((4, LANES), lambda j: (0, j)), pl.BlockSpec((1, LANES), lambda j: (0, j))
    return pl.pallas_call(body, name=name, out_shape=(_sds((SEQ, CONV_CH), MXU), _sds((4, CONV_CH)), _sds((1, CONV_CH))), grid=(nblk,),
                          in_specs=[pl.BlockSpec((SEQ, LANES), lambda j: (0, XBC // LANES + j)), blk, wspec, bspec],
                          out_specs=(blk, wspec, bspec), scratch_shapes=[pltpu.VMEM((SEQ + 8, LANES), F32)],
                          compiler_params=_cp())(proj, dact, w, b)


def _ssd_chunk(xs, bm, cm, dtr, z, hs, al16, dtb, dskx, nw):
    m0, m1 = _half_masks()
    row = lax.broadcasted_iota(jnp.int32, (BLK, BLK), 0)
    col = lax.broadcasted_iota(jnp.int32, (BLK, BLK), 1)
    causal = row >= col
    tril = causal.astype(F32)
    lane = lax.broadcasted_iota(jnp.int32, (1, LANES), 1)
    sub = lax.broadcasted_iota(jnp.int32, (BLK, 1), 0)
    last_row = (sub == BLK - 1).astype(F32)
    dt = jnp.where(lane < 16, _softplus(dtr + dtb), 0.0)
    a16 = -jnp.exp(al16)
    acum = jnp.dot(tril, dt * a16, precision=HI, preferred_element_type=F32)
    acum_t = acum.T
    gmat = [_mm(cm[g], bm[g], NT) for g in range(2)]
    ys, hn = [], []
    for p in range(8):
        g = p // 4
        pick = [(lane == 2 * p + a).astype(F32) for a in range(2)]
        col_h = [jnp.sum(acum * pick[a], axis=1, keepdims=True) for a in range(2)]
        dt_x = sum(jnp.sum(dt * pick[a], axis=1, keepdims=True) * msk for a, msk in enumerate((m0, m1)))
        ac_x = col_h[0] * m0 + col_h[1] * m1
        a_end = jnp.sum(ac_x * last_row, axis=0, keepdims=True)
        xdt = xs[p] * dt_x
        y = _mm(cm[g], hs[p]) * jnp.exp(ac_x)
        for a, msk in enumerate((m0, m1)):
            row_h = jnp.sum(acum_t * (sub == 2 * p + a).astype(F32), axis=0, keepdims=True)
            decay = jnp.exp(jnp.where(causal, col_h[a] - row_h, NEG))
            y = y + _mm(gmat[g] * decay, xdt * msk)
        st = _mm(bm[g], xdt * jnp.exp(a_end - ac_x), TN)
        hn.append(hs[p] * jnp.exp(a_end) + st)
        y = y + dskx[p] * xs[p]
        ys.append(y * _silu(z[p]))
    out = []
    for g in range(2):
        ms = sum(jnp.sum(ys[p] * ys[p], axis=1, keepdims=True) for p in range(4 * g, 4 * g + 4)) * (1.0 / 512)
        rstd = lax.rsqrt(ms + EPS)
        out += [ys[p] * rstd * nw[p] for p in range(4 * g, 4 * g + 4)]
    return out, hn


def _tiles(ref, n, off=0, rows=slice(None)):
    return [ref[rows, off + LANES * p:off + LANES * (p + 1)] for p in range(n)]


def _ssd_load(xbc_ref, z_ref, dt_ref, rows):
    return (_tiles(xbc_ref, 8, 0, rows), _tiles(xbc_ref, 2, 1024, rows), _tiles(xbc_ref, 2, 1280, rows), dt_ref[rows, :],
            _tiles(z_ref, 8, 0, rows))


def _ssd_params(al16_ref, dtb_ref, dsk_ref, nw_ref):
    return al16_ref[...], dtb_ref[...], _tiles(dsk_ref, 8), _tiles(nw_ref, 8)


_NCH = SEQ // BLK
_PER_STEP = 2
_STEP_ROWS = _PER_STEP * BLK


def _ssd_param_specs():
    return [_full((1, LANES)), _full((1, LANES)), _full((1, 1024)), _full((1, 1024))]


def _ssd_fwd(xbc_act, proj, al16, dtb, dskx, nw, name):
    def body(xbc_ref, z_ref, dt_ref, al16_ref, dtb_ref, dsk_ref, nw_ref, y_ref, hin_ref, h_scr):
        @pl.when(pl.program_id(0) == 0)
        def _():
            h_scr[...] = jnp.zeros_like(h_scr)
        params = _ssd_params(al16_ref, dtb_ref, dsk_ref, nw_ref)
        hs = _tiles(h_scr, 8)
        for k in range(_PER_STEP):
            rows = slice(BLK * k, BLK * (k + 1))
            for p in range(8):
                hin_ref[k, :, LANES * p:LANES * (p + 1)] = hs[p]
            ys, hs = _ssd_chunk(*_ssd_load(xbc_ref, z_ref, dt_ref, rows), hs, *params)
            for p in range(8):
                y_ref[rows, LANES * p:LANES * (p + 1)] = ys[p].astype(y_ref.dtype)
        for p in range(8):
            h_scr[:, LANES * p:LANES * (p + 1)] = hs[p]

    return pl.pallas_call(
        body, name=name, out_shape=(_sds((SEQ, 1024), MXU), _sds((_NCH, BLK, 1024))), grid=(_NCH // _PER_STEP,),
        in_specs=[pl.BlockSpec((_STEP_ROWS, CONV_CH), lambda c: (c, 0)), pl.BlockSpec((_STEP_ROWS, 1024), lambda c: (c, ZB // 1024)),
                  pl.BlockSpec((_STEP_ROWS, LANES), lambda c: (c, DTC // LANES))] + _ssd_param_specs(),
        out_specs=(pl.BlockSpec((_STEP_ROWS, 1024), lambda c: (c, 0)), pl.BlockSpec((_PER_STEP, BLK, 1024), lambda c: (c, 0, 0))),
        scratch_shapes=[pltpu.VMEM((BLK, 1024), F32)], compiler_params=_cp())(xbc_act, proj, proj, al16, dtb, dskx, nw)


def _ssd_bwd(xbc_act, proj, hin, dyb, al16, dtb, dskx, nw, name):
    def body(xbc_ref, z_ref, dt_ref, hin_ref, dy_ref, al16_ref, dtb_ref, dsk_ref, nw_ref,
             dxbc_ref, dz_ref, ddt_ref, dal16_ref, ddtb_ref, ddsk_ref, dnw_ref, dh_scr):
        @pl.when(pl.program_id(0) == 0)
        def _():
            dh_scr[...] = jnp.zeros_like(dh_scr)
            for r in (dal16_ref, ddtb_ref, ddsk_ref, dnw_ref):
                r[...] = jnp.zeros_like(r)
        params = _ssd_params(al16_ref, dtb_ref, dsk_ref, nw_ref)
        dhs = _tiles(dh_scr, 8)
        for k in reversed(range(_PER_STEP)):
            rows = slice(BLK * k, BLK * (k + 1))
            hs = [hin_ref[k, :, LANES * p:LANES * (p + 1)] for p in range(8)]
            _, vjp = jax.vjp(lambda a, h, q: _ssd_chunk(*a, h, *q), _ssd_load(xbc_ref, z_ref, dt_ref, rows), hs, params)
            (dxs, dbm, dcm, ddt, dz), dhs, (dal16, ddtb, ddsk, dnw) = vjp((_tiles(dy_ref, 8, 0, rows), dhs))
            for p in range(8):
                cols = slice(LANES * p, LANES * (p + 1))
                dxbc_ref[rows, cols] = dxs[p]
                dz_ref[rows, cols] = dz[p].astype(dz_ref.dtype)
                ddsk_ref[:, cols] += ddsk[p]
                dnw_ref[:, cols] += dnw[p]
            for g in range(2):
                dxbc_ref[rows, 1024 + LANES * g:1024 + LANES * (g + 1)] = dbm[g]
                dxbc_ref[rows, 1280 + LANES * g:1280 + LANES * (g + 1)] = dcm[g]
            ddt_ref[rows, :] = ddt.astype(ddt_ref.dtype)
            dal16_ref[...] += dal16
            ddtb_ref[...] += ddtb
        for p in range(8):
            dh_scr[:, LANES * p:LANES * (p + 1)] = dhs[p]

    rev = lambda c: _NCH // _PER_STEP - 1 - c
    return pl.pallas_call(
        body, name=name,
        out_shape=(_sds((SEQ, CONV_CH)), _sds((SEQ, 1024), MXU), _sds((SEQ, LANES), MXU),
                   _sds((1, LANES)), _sds((1, LANES)), _sds((1, 1024)), _sds((1, 1024))),
        grid=(_NCH // _PER_STEP,),
        in_specs=[pl.BlockSpec((_STEP_ROWS, CONV_CH), lambda c: (rev(c), 0)), pl.BlockSpec((_STEP_ROWS, 1024), lambda c: (rev(c), ZB // 1024)),
                  pl.BlockSpec((_STEP_ROWS, LANES), lambda c: (rev(c), DTC // LANES)),
                  pl.BlockSpec((_PER_STEP, BLK, 1024), lambda c: (rev(c), 0, 0)),
                  pl.BlockSpec((_STEP_ROWS, 1024), lambda c: (rev(c), 0))] + _ssd_param_specs(),
        out_specs=(pl.BlockSpec((_STEP_ROWS, CONV_CH), lambda c: (rev(c), 0)), pl.BlockSpec((_STEP_ROWS, 1024), lambda c: (rev(c), 0)),
                   pl.BlockSpec((_STEP_ROWS, LANES), lambda c: (rev(c), 0)),
                   _full((1, LANES)), _full((1, LANES)), _full((1, 1024)), _full((1, 1024))),
        scratch_shapes=[pltpu.VMEM((BLK, 1024), F32)], compiler_params=_cp())(xbc_act, proj, proj, hin, dyb, al16, dtb, dskx, nw)


def _rstd(v):
    return lax.rsqrt(jnp.mean(v * v, axis=1, keepdims=True) + EPS)


def _rms_bwd(dn, n, rstd):
    return rstd * (dn - n * jnp.mean(dn * n, axis=1, keepdims=True))


_VEC = _full((1, D))


def _layer_spec(layer):
    return pl.BlockSpec((None, 2048, D), lambda *_: (layer, 0, 0))

_ROW = pl.BlockSpec((TM, D), lambda i, *_: (i, 0))


def _proj_fwd(x, pre_w, scale, shift, w, layer, name):
    tn, ni = 1024, SEQ // TM

    def body(x_ref, pw_ref, sc_ref, sh_ref, w_ref, o_ref, h_ref, h_scr):
        rows = pl.ds(pl.multiple_of(pl.program_id(1) * TM, TM), TM)

        @pl.when(pl.program_id(0) == 0)
        def _():
            xv = x_ref[...]
            h = ((xv * _rstd(xv) * pw_ref[...]) * (1.0 + sc_ref[...]) + sh_ref[...]).astype(h_ref.dtype)
            h_scr[rows, :] = h
            h_ref[...] = h
        o_ref[...] = jnp.dot(h_scr[rows, :], w_ref[...].astype(MXU), preferred_element_type=F32)

    first_pass = pl.BlockSpec((TM, D), lambda j, i: (jnp.where(j == 0, i, ni - 1), 0))
    return pl.pallas_call(body, name=name, out_shape=(_sds((SEQ, NP)), _sds((SEQ, D), MXU)), grid=(NP // tn, ni),
                          in_specs=[first_pass, _VEC, _VEC, _VEC, pl.BlockSpec((None, D, tn), lambda j, i: (layer, 0, j))],
                          out_specs=(pl.BlockSpec((TM, tn), lambda j, i: (i, j)), first_pass),
                          scratch_shapes=[pltpu.VMEM((SEQ, D), MXU)], compiler_params=_cp())(x, pre_w, scale, shift, w)


_HALF = pl.BlockSpec((TM, 512), lambda i: (i, 0))
_Z_A = pl.BlockSpec((TM, 512), lambda i: (i, ZA // 512))
_Z_C = pl.BlockSpec((TM, 512), lambda i: (i, ZC // 512))


def _out_fwd(o_a, yb, o_c, proj, w, layer, x, gate, post_w, name):
    def body(oa_ref, yb_ref, oc_ref, za_ref, zc_ref, w_ref, x_ref, g_ref, pw_ref, xn_ref, y_ref):
        y = (_mm(oa_ref[...] * _silu(za_ref[...]), w_ref[0:512, :]) + _mm(yb_ref[...], w_ref[512:1536, :])
             + _mm(oc_ref[...] * _silu(zc_ref[...]), w_ref[1536:2048, :]))
        y_ref[...] = y
        xn_ref[...] = x_ref[...] + g_ref[...] * (y * _rstd(y) * pw_ref[...])

    return pl.pallas_call(body, name=name, out_shape=(_sds((SEQ, D)), _sds((SEQ, D))), grid=(SEQ // TM,),
                          in_specs=[_HALF, _ROW, _HALF, _Z_A, _Z_C, _layer_spec(layer), _ROW, _VEC, _VEC],
                          out_specs=(_ROW, _ROW), compiler_params=_cp())(o_a, yb, o_c, proj, proj, w, x, gate, post_w)


def _dymix(dxo, y, gate, post_w, w, layer, o_a, o_c, proj, name):
    def body(dx_ref, y_ref, g_ref, pw_ref, w_ref, oa_ref, oc_ref, za_ref, zc_ref,
             dy_ref, dg_ref, dpw_ref, doa_ref, dza_ref, b_ref, doc_ref, dzc_ref):
        @pl.when(pl.program_id(0) == 0)
        def _():
            dg_ref[...] = jnp.zeros_like(dg_ref)
            dpw_ref[...] = jnp.zeros_like(dpw_ref)
        dx, yv = dx_ref[...], y_ref[...]
        rstd = _rstd(yv)
        n = yv * rstd
        dg_ref[...] += jnp.sum(dx * (n * pw_ref[...]), axis=0, keepdims=True)
        dr = dx * g_ref[...]
        dpw_ref[...] += jnp.sum(dr * n, axis=0, keepdims=True)
        dy = _rms_bwd(dr * pw_ref[...], n, rstd)
        dy_ref[...] = dy
        b_ref[...] = _mm(dy, w_ref[512:1536, :], NT)
        for rows, o_ref, z_ref, do_ref, dz_ref in ((slice(0, 512), oa_ref, za_ref, doa_ref, dza_ref),
                                                   (slice(1536, 2048), oc_ref, zc_ref, doc_ref, dzc_ref)):
            dyg, z = _mm(dy, w_ref[rows, :], NT), z_ref[...]
            do_ref[...] = dyg * _silu(z)
            dz_ref[...] = (dyg * o_ref[...] * _dsilu(z)).astype(dz_ref.dtype)

    return pl.pallas_call(body, name=name,
                          out_shape=(_sds((SEQ, D)), _sds((1, D)), _sds((1, D)),
                                     _sds((SEQ, 512)), _sds((SEQ, 512), MXU), _sds((SEQ, D)), _sds((SEQ, 512)), _sds((SEQ, 512), MXU)),
                          grid=(SEQ // TM,), in_specs=[_ROW, _ROW, _VEC, _VEC, _layer_spec(layer), _HALF, _HALF, _Z_A, _Z_C],
                          out_specs=(_ROW, _VEC, _VEC, _HALF, _HALF, _ROW, _HALF, _HALF),
                          compiler_params=_cp())(dxo, y, gate, post_w, w, o_a, o_c, proj, proj)


def _dwout(o_a, yb, o_c, proj, dy, name):
    def body(oa_ref, yb_ref, oc_ref, za_ref, zc_ref, dy_ref, o_ref):
        @pl.when(pl.program_id(0) == 0)
        def _():
            o_ref[...] = jnp.zeros_like(o_ref)
        dy = dy_ref[...]
        o_ref[0:512, :] += _mm(oa_ref[...] * _silu(za_ref[...]), dy, TN)
        o_ref[512:1536, :] += _mm(yb_ref[...], dy, TN)
        o_ref[1536:2048, :] += _mm(oc_ref[...] * _silu(zc_ref[...]), dy, TN)

    return pl.pallas_call(body, name=name, out_shape=_sds((2048, D)), grid=(SEQ // TM,),
                          in_specs=[_HALF, _ROW, _HALF, _Z_A, _Z_C, _ROW], out_specs=_full((2048, D)),
                          compiler_params=_cp())(o_a, yb, o_c, proj, proj, dy)


def _dwin(h, pieces, name):
    n = len(pieces)
    widths = [p.shape[1] for p in pieces]
    half = NP // 2

    def body(*refs):
        h_ref, p_refs, o_ref = refs[0], refs[1:1 + n], refs[1 + n]

        @pl.when(pl.program_id(0) == 0)
        def _():
            o_ref[...] = jnp.zeros_like(o_ref)
        hv, c0 = h_ref[...], 0
        for p_ref, wd in zip(p_refs, widths):
            o_ref[:, c0:c0 + wd] += _mm(hv, p_ref[...], TN)
            c0 += wd

    return pl.pallas_call(body, name=name, out_shape=_sds((D, half)), grid=(SEQ // TM,),
                          in_specs=[_ROW] + [pl.BlockSpec((TM, wd), lambda k: (k, 0)) for wd in widths],
                          out_specs=_full((D, half)), compiler_params=_cp(56))(h, *pieces)


_TMH = 256


def _dh_bwd(pieces, w, x, pre_w, scale, dxo, name):
    n = len(pieces)
    widths = [p.shape[1] for p in pieces]

    def body(*refs):
        p_refs, (w_ref, x_ref, pw_ref, sc_ref, dxo_ref, dx_ref, dsh_ref, dsc_ref, dpw_ref) = refs[:n], refs[n:]

        @pl.when(pl.program_id(0) == 0)
        def _():
            for r in (dsh_ref, dsc_ref, dpw_ref):
                r[...] = jnp.zeros_like(r)
        dh, c0 = 0.0, 0
        for p_ref, wd in zip(p_refs, widths):
            dh = dh + _mm(p_ref[...], w_ref[:, c0:c0 + wd], NT)
            c0 += wd
        xv = x_ref[...]
        rstd = _rstd(xv)
        nrm = xv * rstd
        dsh_ref[...] += jnp.sum(dh, axis=0, keepdims=True)
        dsc_ref[...] += jnp.sum(dh * (nrm * pw_ref[...]), axis=0, keepdims=True)
        dhn = dh * (1.0 + sc_ref[...])
        dpw_ref[...] += jnp.sum(dhn * nrm, axis=0, keepdims=True)
        dx_ref[...] = _rms_bwd(dhn * pw_ref[...], nrm, rstd) + dxo_ref[...]

    row = pl.BlockSpec((_TMH, D), lambda i: (i, 0))
    return pl.pallas_call(body, name=name, out_shape=(_sds((SEQ, D)), _sds((1, D)), _sds((1, D)), _sds((1, D))),
                          grid=(SEQ // _TMH,),
                          in_specs=[pl.BlockSpec((_TMH, wd), lambda i: (i, 0)) for wd in widths]
                          + [pl.BlockSpec((None, D, NP), lambda i: (0, 0, 0)), row, _VEC, _VEC, row],
                          out_specs=(row, _VEC, _VEC, _VEC), compiler_params=_cp(56))(*pieces, w, x, pre_w, scale, dxo)


def _w_in_padded(land, name):
    rows = 128

    def body(l_ref, o_ref):
        o_ref[...] = _pad_cols(jnp.concatenate([l_ref[k] for k in range(4)], axis=1))

    return pl.pallas_call(body, name=name, out_shape=_sds((D, NP), land.dtype), grid=(D // rows,),
                          in_specs=[pl.BlockSpec((4, rows, SHARD_IN), lambda i: (0, i, 0))],
                          out_specs=pl.BlockSpec((rows, NP), lambda i: (i, 0)), compiler_params=_cp())(land)


def _grad_blocks(dwa, dwb, name):
    rows = 128

    def body(a_ref, b_ref, o_ref):
        g = _unpad_cols(jnp.concatenate([a_ref[...], b_ref[...]], axis=1))
        for k in range(4):
            o_ref[k] = g[:, SHARD_IN * k:SHARD_IN * (k + 1)].astype(o_ref.dtype)

    half = pl.BlockSpec((rows, NP // 2), lambda i: (i, 0))
    return pl.pallas_call(body, name=name, out_shape=_sds((4, D, SHARD_IN), jnp.bfloat16), grid=(D // rows,),
                          in_specs=[half, half], out_specs=pl.BlockSpec((4, rows, SHARD_IN), lambda i: (0, i, 0)),
                          compiler_params=_cp())(dwa, dwb)


def _loss_bwd(xf, tgt, name):
    def body(x_ref, t_ref, dx_ref, l_ref):
        @pl.when(pl.program_id(0) == 0)
        def _():
            l_ref[...] = jnp.zeros_like(l_ref)
        e = x_ref[...] - t_ref[...]
        dx_ref[...] = e * (1.0 / D)
        l_ref[...] += 0.5 * jnp.sum(jnp.mean(e * e, axis=1, keepdims=True), axis=0, keepdims=True)

    return pl.pallas_call(body, name=name, out_shape=(_sds((SEQ, D)), _sds((8, LANES))), grid=(SEQ // TM,),
                          in_specs=[_ROW, _ROW], out_specs=(_ROW, _full((8, LANES))), compiler_params=_cp())(xf, tgt)


def _mod_part(c_all, ada_w, ada_b, name):
    def body(c_ref, w_ref, b_ref, o_ref):
        o_ref[0] = _mm(_silu(c_ref[...]), w_ref[0]) + b_ref[0]

    return pl.pallas_call(body, name=name, out_shape=_sds((DEPTH, 8, 768)), grid=(DEPTH,),
                          in_specs=[_full((8, D)), pl.BlockSpec((1, D, 768), lambda i: (i, 0, 0)), pl.BlockSpec((1, 1, 768), lambda i: (i, 0, 0))],
                          out_specs=pl.BlockSpec((1, 8, 768), lambda i: (i, 0, 0)), compiler_params=_cp())(c_all, ada_w, ada_b)


def _ada_grad(c_t, dmod, name):
    def body(c_ref, d_ref, o_ref):
        ca = _silu(c_ref[...])
        dm = d_ref[0]
        acc = ca[:, 0:1] * dm[0:1, :]
        for s in range(1, 8):
            acc = acc + ca[:, s:s + 1] * dm[s:s + 1, :]
        o_ref[0] = acc

    return pl.pallas_call(body, name=name, out_shape=_sds((DEPTH, D, 768)), grid=(DEPTH,),
                          in_specs=[_full((D, LANES)), pl.BlockSpec((1, 8, 768), lambda i: (i, 0, 0))],
                          out_specs=pl.BlockSpec((1, D, 768), lambda i: (i, 0, 0)), compiler_params=_cp())(c_t, dmod)


def _pack(parts):
    flat = []
    for p in parts:
        f = p.reshape(-1)
        flat.append(jnp.pad(f, (0, (-f.size) % LANES)))
    v = jnp.concatenate(flat)
    return jnp.pad(v, (0, (-v.size) % (8 * LANES))).reshape(-1, LANES)


def _unpack(v, shapes):
    v = v.reshape(-1)
    out, off = [], 0
    for s in shapes:
        n = math.prod(s)
        out.append(v[off:off + n].reshape(s))
        off += n + (-n) % LANES
    return out


_GIVEN_DT, _GIVEN_C = 4608, 4624


def _pad_cols(w):
    return jnp.concatenate([w[..., :_GIVEN_DT], w[..., _GIVEN_C:], w[..., _GIVEN_DT:_GIVEN_C],
                            jnp.zeros(w.shape[:-1] + (NP - IN_COLS,), w.dtype)], axis=-1)


def _unpad_cols(w):
    return jnp.concatenate([w[..., :_GIVEN_DT], w[..., DTC:DTC + 16], w[..., _GIVEN_DT:DTC]], axis=-1)


def _pad_lanes(v):
    return jnp.pad(v, (0, LANES - v.shape[0])).reshape(1, LANES)


def _local_step(x2, tgt, mod, weights_of, grads_done, pre_w, post_w, conv_w, conv_b, dt_bias, a_log, d_skip, nw, sinks):
    saved = []
    xcur = x2
    for i in range(DEPTH):
        shift, scale, gate = mod[i:i + 1, :D], mod[i:i + 1, D:2 * D], mod[i:i + 1, 2 * D:]
        pw, qw = pre_w[i:i + 1], post_w[i:i + 1]
        w_p, w_o = weights_of(i, xcur)
        proj, h = _proj_fwd(xcur, pw, scale, shift, w_p, 0, "proj_fwd")
        o_a, lse_a = _attn_fwd(proj, QA // LANES, KA // LANES, VA // LANES, DILS, False, None, "attn_a_fwd")
        sink_x = jnp.repeat(sinks[i], HD).reshape(1, 512)
        o_c, lse_c = _attn_fwd(proj, QC // LANES, KC // LANES, VC // LANES, (1,), True, sink_x, "attn_c_fwd")
        cw, cb = conv_w[i], conv_b[i:i + 1]
        xbc_act = _conv_fwd(proj, cw, cb, "conv_fwd")
        ssd_p = (_pad_lanes(a_log[i]), _pad_lanes(dt_bias[i]), jnp.repeat(d_skip[i], HD).reshape(1, 1024), nw[i:i + 1])
        yb, hin = _ssd_fwd(xbc_act, proj, *ssd_p, "ssd_fwd")
        xnew, y = _out_fwd(o_a, yb, o_c, proj, w_o, 0, xcur, gate, qw, "out_fwd")
        saved.append((w_p, w_o, xcur, scale, gate, pw, qw, proj, h, o_a, lse_a, sink_x, o_c, lse_c, cw, cb, xbc_act, ssd_p, yb, hin, y))
        xcur = xnew
    dx, ltile = _loss_bwd(xcur, tgt, "loss")
    dmod, small = [None] * DEPTH, [None] * DEPTH
    for i in reversed(range(DEPTH)):
        w_p, w_o, xin, scale, gate, pw, qw, proj, h, o_a, lse_a, sink_x, o_c, lse_c, cw, cb, xbc_act, ssd_p, yb, hin, y = saved[i]
        dy, dgate, dpost, do_a, dz_a, dyb, do_c, dz_c = _dymix(dx, y, gate, qw, w_o, 0, o_a, o_c, proj, "dymix")
        dwo = _dwout(o_a, yb, o_c, proj, dy, "dwout")
        dq_a, dk_a, dv_a = _attn_bwd(proj, QA // LANES, KA // LANES, VA // LANES, do_a, o_a, lse_a, DILS, False, None, "attn_a_bwd")
        dq_c, dk_c, dv_c, dsk = _attn_bwd(proj, QC // LANES, KC // LANES, VC // LANES, do_c, o_c, lse_c, (1,), True, sink_x, "attn_c_bwd")
        dxbc_act, dz_b, ddt, dal16, ddtb, ddsk, dnw = _ssd_bwd(xbc_act, proj, hin, dyb, *ssd_p, "ssd_bwd")
        dxbc, dcw, dcb = _conv_bwd(proj, dxbc_act, cw, cb, "conv_bwd")
        half_a, half_b = [dq_a, dk_a, dv_a, dz_a, dz_b], [dxbc, dq_c, dz_c, dk_c, dv_c, ddt]
        sent = grads_done(i, _dwin(h, half_a, "dwin_a"), _dwin(h, half_b, "dwin_b"), dwo)
        dx, dshift, dscale, dpre = _dh_bwd(half_a + half_b, w_p, xin, pw, scale + sent[0, 0], dx, "dh_bwd")
        dmod[i] = jnp.concatenate([dshift, dscale, dgate], axis=1)
        small[i] = (dpre, dpost, dcw, dcb, ddtb[0, :16], dal16[0, :16], ddsk.reshape(16, HD).sum(axis=1), dnw, dsk[:, 0, ::HD].reshape(8))
    return ltile, dx, jnp.concatenate(dmod, axis=0), small


_SMALL = ((1, D), (1, D), (4, CONV_CH), (1, CONV_CH), (16,), (16,), (16,), (1, D), (8,))


def kernel(x, c, ada_w, ada_b, pre_norm_w, post_norm_w, w_in, conv_w, conv_b, dt_bias, a_log, d_skip, ssm_norm_w, sinks, w_out, loss_target, m_ada_w, m_ada_b, m_pre_norm_w, m_post_norm_w, m_w_in, m_conv_w, m_conv_b, m_dt_bias, m_a_log, m_d_skip, m_ssm_norm_w, m_sinks, m_w_out, v_ada_w, v_ada_b, v_pre_norm_w, v_post_norm_w, v_w_in, v_conv_w, v_conv_b, v_dt_bias, v_a_log, v_d_skip, v_ssm_norm_w, v_sinks, v_w_out):
    xi, yi, ci = lax.axis_index("x"), lax.axis_index("y"), lax.axis_index("c")
    chip = 2 * xi + yi
    me = 2 * chip + ci

    gathers = []
    for i in range(DEPTH):
        shards = [_cast_bf16(w_in[i:i + 1], 512, "cast_w_in"), _cast_bf16(w_out[i:i + 1], 512, "cast_w_out")]
        lands = [lax.dynamic_update_slice(lax.empty((4,) + a.shape[1:], a.dtype), a, (chip, 0, 0)) for a in shards]
        gathers.append(_split_start(None, lands, f"gather_start{i}", "half" if i == 0 else "whole"))
    all_started = gathers[0][3] + gathers[1][3] + gathers[2][3] + gathers[3][3]

    def weights_of(i, after):
        send_sems, recv_sems, thru, _ = gathers[i]
        if i == 0:
            halves = _split_wait(send_sems, recv_sems, thru, 2, all_started + mod[:1, :LANES], "gather_wait0", "half")
            send_sems, recv_sems, thru, after = _split_start(None, halves, "share_start0", "sibling")
            g_in, g_out = _split_wait(send_sems, recv_sems, thru, 2, after, "share_wait0", "sibling")
        else:
            g_in, g_out = _split_wait(send_sems, recv_sems, thru, 2, after, f"gather_wait{i}")
        return _w_in_padded(g_in, "w_in_padded")[None], g_out.reshape(1, 2048, D)

    scatters = [None] * DEPTH

    def grads_done(i, dwa, dwb, dwo):
        blocks = [_grad_blocks(dwa, dwb, "grad_blocks"), _cast_bf16(dwo.reshape(4, 512, D), 512, "cast_dw_out")]
        scatters[i] = _split_start(blocks, [lax.empty(b.shape, b.dtype) for b in blocks], f"scatter_start{i}")
        return scatters[i][3]

    g0 = _allgather8(_pack([c, conv_w]), "gather_c")
    c_all = g0[:, :8, :].reshape(8, D)
    conv_w_full = jnp.concatenate([g0[2 * k, 8:56, :].reshape(DEPTH, 4, CONV_CH // 4) for k in range(4)], axis=-1)

    ada_b_mine = lax.dynamic_slice_in_dim(ada_b, 768 * chip, 768, axis=1).reshape(DEPTH, 1, 768)
    gm = _allgather8(_mod_part(c_all, ada_w, ada_b_mine, "mod_part").reshape(DEPTH * 8, 768), "gather_mod")
    gm = gm.reshape(4, 2, DEPTH, 8, 768)[:, 0]
    mod = lax.dynamic_index_in_dim(gm, me, axis=2, keepdims=False).transpose(1, 0, 2).reshape(DEPTH, 3 * D)

    ltile, dx, dmod, small = _local_step(x[0], loss_target[0], mod, weights_of, grads_done, pre_norm_w, post_norm_w, conv_w_full,
                                         conv_b, dt_bias, a_log, d_skip, ssm_norm_w, sinks)

    packed = _pack([dmod] + [g for layer in small for g in layer] + [ltile[0]])
    gs = _allgather8(packed, "gather_small")
    tot = _sum_blocks(gs[:, None], packed.shape[0], "sum_small")[0]
    parts = _unpack(tot, [(DEPTH, 3 * D)] + list(_SMALL) * DEPTH + [(LANES,)])
    g_ada_b, loss = parts[0], parts[-1][0]
    per_layer = [parts[1 + len(_SMALL) * i:1 + len(_SMALL) * (i + 1)] for i in range(DEPTH)]
    g_pre, g_post, g_cw, g_cb, g_dtb, g_al, g_dsk, g_nw, g_sk = [jnp.stack([per_layer[i][j] for i in range(DEPTH)]) for j in range(len(_SMALL))]
    g_pre, g_post, g_cb, g_nw = g_pre[:, 0], g_post[:, 0], g_cb[:, 0], g_nw[:, 0]
    g_cw = lax.dynamic_slice_in_dim(g_cw, (CONV_CH // 4) * chip, CONV_CH // 4, axis=2)

    dmod_all = gs[:, :(DEPTH * 3 * D) // LANES, :].reshape(8, DEPTH, 3 * D).transpose(1, 0, 2)
    dmod_mine = lax.dynamic_slice_in_dim(dmod_all, 768 * chip, 768, axis=2)
    c_t = jnp.pad(c_all.T, ((0, 0), (0, LANES - 8)))
    g_ada_w = _ada_grad(c_t, dmod_mine, "ada_grad")

    res = {}
    res["ada_w"] = _adamw(ada_w, [g_ada_w], m_ada_w, v_ada_w, 512, "adamw_ada_w")
    names = ["ada_b", "pre_norm_w", "post_norm_w", "conv_w", "conv_b", "dt_bias", "a_log", "d_skip", "ssm_norm_w", "sinks"]
    ws = [ada_b, pre_norm_w, post_norm_w, conv_w, conv_b, dt_bias, a_log, d_skip, ssm_norm_w, sinks]
    gsm = [g_ada_b, g_pre, g_post, g_cw, g_cb, g_dtb, g_al, g_dsk, g_nw, g_sk]
    ms = [m_ada_b, m_pre_norm_w, m_post_norm_w, m_conv_w, m_conv_b, m_dt_bias, m_a_log, m_d_skip, m_ssm_norm_w, m_sinks]
    vs = [v_ada_b, v_pre_norm_w, v_post_norm_w, v_conv_w, v_conv_b, v_dt_bias, v_a_log, v_d_skip, v_ssm_norm_w, v_sinks]
    pw_, pg_, pm_, pv_ = _pack(ws), _pack(gsm), _pack(ms), _pack(vs)
    small_out = _adamw(pw_[None], [pg_[None]], pm_[None], pv_[None], pw_.shape[0], "adamw_small")

    others_done = small_out[1][0, :8] + res["ada_w"][1][0, :8, :LANES]
    landed = [_split_wait(*scatters[i][:3], 2, others_done, f"scatter_wait{i}") for i in range(DEPTH)]
    p_in = _sum_chips([d[2] for d in landed], [d[0] for d in landed], 128, "sum_w_in")
    p_out = _sum_chips([d[3] for d in landed], [d[1] for d in landed], 256, "sum_w_out")
    col_major, row_major = (lambda a: jnp.transpose(a, (2, 0, 1))), (lambda a: jnp.transpose(a, (1, 2, 0)))
    p_in = col_major(p_in)
    s_in, s_out = _sibling_swap([p_in, p_out], "swap_partials")
    res["w_in"] = [row_major(a) for a in _adamw(col_major(w_in), [p_in, s_in], col_major(m_w_in), col_major(v_w_in), None,
                                                "adamw_w_in", lead=SHARD_IN // 18)]
    res["w_out"] = _adamw(w_out, [p_out, s_out], m_w_out, v_w_out, 512, "adamw_w_out")
    shapes = [w.shape for w in ws]
    for kind in range(4):
        for nm, a in zip(names, _unpack(small_out[kind][0], shapes)):
            res.setdefault(nm, [None] * 4)[kind] = a
    order = ["ada_w", "ada_b", "pre_norm_w", "post_norm_w", "w_in", "conv_w", "conv_b", "dt_bias", "a_log", "d_skip", "ssm_norm_w", "sinks", "w_out"]
    return (loss, dx[None], *[res[n][0] for n in order], *[res[n][1] for n in order], *[res[n][2] for n in order], *[res[n][3] for n in order])
```

```python
import math

import jax
import jax.numpy as jnp
from jax import lax
from jax.experimental import pallas as pl
from jax.experimental.pallas import tpu as pltpu

F32 = jnp.float32
MXU = jnp.bfloat16
HI = lax.Precision.HIGHEST
MESH = pl.DeviceIdType.MESH

SEQ = 4096
D = 1024
DEPTH = 4
HD = 64
QK_SCALE = HD ** -0.5
LANES = 128
BLK = 128
DILS = (1, 4, 16)
NEG = -1e30
EPS = 1e-6
MIB = 1024 * 1024

NP = 6144
QA, KA, VA, ZA = 0, 512, 1024, 1536
ZB, XBC = 2048, 3072
QC, ZC, KC, VC = 4608, 5120, 5632, 5760
DTC = 5888
IN_COLS = 5904
SHARD_IN = IN_COLS // 4
CONV_CH = 1536
TM = 512

ADAM_LR, ADAM_B1, ADAM_B2, ADAM_EPS, ADAM_WD, ADAM_STEP = 0.001, 0.9, 0.999, 1e-08, 0.01, 10

NT = (((1,), (1,)), ((), ()))
TN = (((0,), (0,)), ((), ()))


def _cp(vmem_mib=48):
    return pltpu.CompilerParams(vmem_limit_bytes=vmem_mib * MIB)


def _sds(shape, dtype=F32):
    return jax.ShapeDtypeStruct(shape, dtype)


def _full(shape):
    n = len(shape)
    return pl.BlockSpec(shape, lambda *_: (0,) * n)


def _mm(a, b, dims=None):
    if dims is None:
        return jnp.dot(a.astype(MXU), b.astype(MXU), preferred_element_type=F32)
    return lax.dot_general(a.astype(MXU), b.astype(MXU), dims, preferred_element_type=F32)


def _sigmoid(x):
    return 1.0 / (1.0 + jnp.exp(-x))


def _silu(x):
    return x * _sigmoid(x)


def _dsilu(x):
    s = _sigmoid(x)
    return s * (1.0 + x * (1.0 - s))


def _softplus(x):
    ax = jnp.where(x >= 0, x, -x)
    return jnp.maximum(x, 0.0) + jnp.log1p(jnp.exp(-ax))


def _half_masks():
    lane = lax.broadcasted_iota(jnp.int32, (1, LANES), 1)
    m0 = (lane < HD).astype(F32)
    return m0, 1.0 - m0


def _allgather8(v, name):
    r, cc = v.shape

    def body(v_ref, out_ref, send_sems, recv_sems):
        x, y, c = lax.axis_index("x"), lax.axis_index("y"), lax.axis_index("c")
        me = 4 * x + 2 * y + c
        out_ref[me] = v_ref[...]
        peers = []
        for k in range(1, 8):
            px = 1 - x if k & 4 else x
            py = 1 - y if k & 2 else y
            pc = 1 - c if k & 1 else c
            peers.append((px, py, pc))
        sends = []
        for k, peer in enumerate(peers):
            cp = pltpu.make_async_remote_copy(src_ref=v_ref, dst_ref=out_ref.at[me], send_sem=send_sems.at[k],
                                              recv_sem=recv_sems.at[k], device_id=peer, device_id_type=MESH)
            cp.start()
            sends.append(cp)
        for k, (px, py, pc) in enumerate(peers):
            pltpu.make_async_remote_copy(src_ref=v_ref, dst_ref=out_ref.at[4 * px + 2 * py + pc], send_sem=send_sems.at[k],
                                         recv_sem=recv_sems.at[k], device_id=(px, py, pc), device_id_type=MESH).wait_recv()
        for cp in sends:
            cp.wait_send()

    return pl.pallas_call(
        body, name=name, out_shape=_sds((8, r, cc)),
        in_specs=[pl.BlockSpec(memory_space=pltpu.VMEM)], out_specs=pl.BlockSpec(memory_space=pltpu.VMEM),
        scratch_shapes=[pltpu.SemaphoreType.DMA((7,)), pltpu.SemaphoreType.DMA((7,))],
        compiler_params=_cp(32),
    )(v)


_HBM = pl.BlockSpec(memory_space=pltpu.HBM)
_SEM = pl.BlockSpec(memory_space=pltpu.SEMAPHORE)
_EFFECT = pltpu.SideEffectType.DATAFLOW_SIDE_EFFECTING


def _chip_copies(src_refs, land_refs, send_sems, recv_sems, part="whole"):
    x, y, c = lax.axis_index("x"), lax.axis_index("y"), lax.axis_index("c")
    mine = 2 * x + y
    out = []
    for i, land in enumerate(land_refs):
        half = land.shape[1] // 2
        own, others = pl.ds(pl.multiple_of(c * half, half), half), pl.ds(pl.multiple_of((1 - c) * half, half), half)
        for j, (px, py) in enumerate([(1 - x, y), (x, 1 - y), (1 - x, 1 - y)]):
            slot, peer = 2 * px + py, (px, py, c)
            if part == "whole":
                src = src_refs[i].at[slot] if src_refs else land.at[mine]
                there, here = land.at[mine], land.at[slot]
            elif part == "half":
                src = there = land.at[mine].at[own]
                here = land.at[slot].at[own]
            else:
                src = there = land.at[slot].at[own]
                here, peer = land.at[slot].at[others], (x, y, 1 - c)
            mk = lambda dst, i=i, j=j, src=src, peer=peer: pltpu.make_async_remote_copy(
                src_ref=src, dst_ref=dst, send_sem=send_sems.at[3 * i + j], recv_sem=recv_sems.at[3 * i + j],
                device_id=peer, device_id_type=MESH)
            out.append((mk(there), mk(here)))
    return out


def _split_start(srcs, lands, name, part="whole"):
    ops = list(srcs or []) + list(lands)
    ns, n = len(srcs or []), len(lands)

    def body(*refs):
        src_refs, land_refs = refs[:ns], refs[ns:ns + n]
        send_sems, recv_sems = refs[ns + n], refs[ns + n + 1]
        for mine_out, _ in _chip_copies(src_refs, land_refs, send_sems, recv_sems, part):
            mine_out.start()
        refs[-1][...] = jnp.zeros_like(refs[-1])

    sems = pltpu.SemaphoreType.DMA((3 * n,))
    res = pl.pallas_call(
        body, name=name, out_shape=(sems, sems) + tuple(pltpu.HBM(a.shape, a.dtype) for a in ops) + (_sds((8, LANES)),),
        in_specs=[_HBM] * len(ops), out_specs=(_SEM, _SEM) + (_HBM,) * len(ops) + (pl.BlockSpec(memory_space=pltpu.VMEM),),
        input_output_aliases={k: 2 + k for k in range(len(ops))},
        compiler_params=pltpu.CompilerParams(has_side_effects=_EFFECT),
    )(*[pltpu.with_memory_space_constraint(a, pltpu.HBM) for a in ops])
    return res[0], res[1], list(res[2:2 + len(ops)]), res[-1]


def _split_wait(send_sems, recv_sems, thru, n, after, name, part="whole"):
    ns = len(thru) - n

    def body(*refs):
        src_refs, land_refs = refs[:ns], refs[ns:ns + n]
        for mine_out, arriving in _chip_copies(src_refs, land_refs, refs[ns + n], refs[ns + n + 1], part):
            mine_out.wait_send()
            arriving.wait_recv()

    res = pl.pallas_call(
        body, name=name, out_shape=tuple(pltpu.HBM(a.shape, a.dtype) for a in thru),
        in_specs=[_HBM] * len(thru) + [_SEM, _SEM, pl.BlockSpec(memory_space=pl.ANY)], out_specs=(_HBM,) * len(thru),
        input_output_aliases={k: k for k in range(len(thru))},
        compiler_params=pltpu.CompilerParams(has_side_effects=_EFFECT),
    )(*thru, send_sems, recv_sems, after)
    return list(res)


def _sibling_swap(arrs, name):
    n = len(arrs)

    def body(*refs):
        ins, outs_, (send_sems, recv_sems) = refs[:n], refs[n:2 * n], refs[2 * n:]
        sib = (lax.axis_index("x"), lax.axis_index("y"), 1 - lax.axis_index("c"))
        cps = [pltpu.make_async_remote_copy(src_ref=ins[i], dst_ref=outs_[i], send_sem=send_sems.at[i], recv_sem=recv_sems.at[i],
                                            device_id=sib, device_id_type=MESH) for i in range(n)]
        for cp in cps:
            cp.start()
        for cp in cps:
            cp.wait_recv()
        for cp in cps:
            cp.wait_send()

    hbm = pl.BlockSpec(memory_space=pltpu.HBM)
    return pl.pallas_call(
        body, name=name, out_shape=tuple(_sds(a.shape, a.dtype) for a in arrs), in_specs=[hbm] * n, out_specs=tuple([hbm] * n),
        scratch_shapes=[pltpu.SemaphoreType.DMA((n,)), pltpu.SemaphoreType.DMA((n,))],
    )(*arrs)


def _tile_spec(rows, cc):
    return pl.BlockSpec((None, rows, cc), lambda l, i: (l, i, 0))


def _cast_bf16(a, rows, name):
    nl, r, cc = a.shape

    def body(a_ref, o_ref):
        o_ref[...] = a_ref[...].astype(jnp.bfloat16)

    return pl.pallas_call(body, name=name, out_shape=_sds((nl, r, cc), jnp.bfloat16), grid=(nl, r // rows),
                          in_specs=[_tile_spec(rows, cc)], out_specs=_tile_spec(rows, cc), compiler_params=_cp())(a)


def _sum_blocks(a, rows, name):
    k, nl, r, cc = a.shape

    def body(a_ref, o_ref):
        acc = a_ref[0].astype(F32)
        for j in range(1, k):
            acc = acc + a_ref[j].astype(F32)
        o_ref[...] = acc

    return pl.pallas_call(body, name=name, out_shape=_sds((nl, r, cc)), grid=(nl, r // rows),
                          in_specs=[pl.BlockSpec((k, None, rows, cc), lambda l, i: (0, l, i, 0))],
                          out_specs=_tile_spec(rows, cc), compiler_params=_cp())(a)


def _sum_chips(lands, srcs, rows, name):
    nl = len(lands)
    _, r, cc = lands[0].shape

    def body(*refs):
        land_refs, src_refs, o_ref = refs[:nl], refs[nl:2 * nl], refs[2 * nl]
        mine = 2 * lax.axis_index("x") + lax.axis_index("y")
        for j in range(nl):
            @pl.when(pl.program_id(0) == j)
            def _(j=j):
                own = src_refs[j][mine].astype(F32)
                acc = None
                for k in range(4):
                    term = jnp.where(mine == k, own, land_refs[j][k].astype(F32))
                    acc = term if acc is None else acc + term
                o_ref[...] = acc

    specs = [pl.BlockSpec((4, rows, cc), lambda l, i, j=j: (0, jnp.where(l == j, i, 0), 0)) for j in range(nl)]
    return pl.pallas_call(body, name=name, out_shape=_sds((nl, r, cc)), grid=(nl, r // rows),
                          in_specs=specs + specs, out_specs=_tile_spec(rows, cc), compiler_params=_cp())(*lands, *srcs)


def _adamw(w, parts, m, v, rows, name, lead=None):
    nl, r, cc = w.shape
    np_ = len(parts)
    c1 = 1.0 / (1.0 - ADAM_B1 ** ADAM_STEP)
    c2 = 1.0 / (1.0 - ADAM_B2 ** ADAM_STEP)

    def body(*refs):
        w_ref, p_refs, (m_ref, v_ref, g_ref, d_ref, nm_ref, nv_ref) = refs[0], refs[1:1 + np_], refs[1 + np_:]
        g = p_refs[0][...]
        for p_ref in p_refs[1:]:
            g = g + p_ref[...]
        nm = ADAM_B1 * m_ref[...] + (1.0 - ADAM_B1) * g
        nv = ADAM_B2 * v_ref[...] + (1.0 - ADAM_B2) * (g * g)
        g_ref[...] = g
        nm_ref[...] = nm
        nv_ref[...] = nv
        d_ref[...] = -ADAM_LR * ((nm * c1) / (jnp.sqrt(nv * c2) + ADAM_EPS) + ADAM_WD * w_ref[...])

    if lead is None:
        spec, grid = _tile_spec(rows, cc), (nl, r // rows)
    else:
        spec, grid = pl.BlockSpec((lead, r, cc), lambda i: (i, 0, 0)), (nl // lead,)
    return pl.pallas_call(body, name=name, out_shape=(_sds((nl, r, cc)),) * 4, grid=grid,
                          in_specs=[spec] * (3 + np_), out_specs=(spec,) * 4, compiler_params=_cp())(w, *parts, m, v)


_BIAS = pltpu.VMEM((2, 2 * BLK, 2 * BLK), F32)


def _fill_band_bias(bias_ref):
    qi = lax.broadcasted_iota(jnp.int32, (2 * BLK, 2 * BLK), 0) & (BLK - 1)
    kj = lax.broadcasted_iota(jnp.int32, (2 * BLK, 2 * BLK), 1)
    dist = BLK + qi - kj
    band = (dist >= 0) & (dist <= BLK)
    bias_ref[0] = jnp.where(band, 0.0, NEG)
    bias_ref[1] = jnp.where(band & (kj >= BLK), 0.0, NEG)


class _HeadStack:
    def __init__(self, group):
        self.m0, self.m1 = _half_masks()
        self.group = group
        if group is not None:
            self.kv_mask = (self.m0, self.m1)[group]

    def _swap_half(self, t, a):
        return t if a == self.group else pltpu.roll(t, HD, axis=1)

    def stack(self, t):
        t0, t1 = t * self.m0, t * self.m1
        if self.group is not None:
            t0, t1 = self._swap_half(t0, 0), self._swap_half(t1, 1)
        return jnp.concatenate([t0, t1], axis=0)

    def unstack(self, ts):
        if self.group is None:
            return ts[:BLK] * self.m0 + ts[BLK:] * self.m1
        return self._swap_half(ts[:BLK] * self.kv_mask, 0) + self._swap_half(ts[BLK:] * self.kv_mask, 1)


def _rows(st, dil):
    if dil == 1:
        return pl.ds(pl.multiple_of(st, BLK), BLK)
    return pl.ds(st, BLK, stride=dil)


def _block_pos(n, dil):
    nb = SEQ // (dil * BLK)
    r, b = n // nb, n % nb
    hp = (b > 0).astype(jnp.int32)
    st = r + dil * BLK * b
    return st, st - dil * BLK * hp, 1 - hp


def _attn_fwd(proj, qblk, kblk, vblk, dils, gqa, sink_x, name):
    has_sink = sink_x is not None

    def body(*refs):
        if has_sink:
            q_ref, k_ref, v_ref, s_ref, o_ref, lse_ref, m_scr, z_scr, bias_scr = refs
        else:
            q_ref, k_ref, v_ref, o_ref, lse_ref, m_scr, z_scr, bias_scr = refs

        @pl.when(pl.program_id(0) == 0)
        def _():
            _fill_band_bias(bias_scr)
        o_ref[...] = jnp.zeros_like(o_ref)
        if has_sink:
            z_scr[...] = jnp.ones_like(z_scr)
            m_scr[...] = jnp.broadcast_to(s_ref[...], m_scr.shape)
        else:
            z_scr[...] = jnp.zeros_like(z_scr)
            m_scr[...] = jnp.full_like(m_scr, NEG)

        def step(n, carry, dil, heads):
            m0, m1 = heads.m0, heads.m1
            st, stp, first = _block_pos(n, dil)
            rq, rp = _rows(st, dil), _rows(stp, dil)
            kk = jnp.concatenate([k_ref[rp, :], k_ref[rq, :]], axis=0)
            vv = jnp.concatenate([v_ref[rp, :], v_ref[rq, :]], axis=0)
            s = _mm(heads.stack(q_ref[rq, :] * QK_SCALE), kk, NT) + bias_scr[first]
            m = jnp.max(s, axis=1, keepdims=True)
            p = jnp.exp(s - m)
            l = jnp.sum(p, axis=1, keepdims=True)
            o_pair = heads.unstack(_mm(p, vv))
            m_pair = m[:BLK] * m0 + m[BLK:] * m1
            l_pair = l[:BLK] * m0 + l[BLK:] * m1
            m_old = m_scr[rq, :]
            m_new = jnp.maximum(m_old, m_pair)
            alpha, beta = jnp.exp(m_old - m_new), jnp.exp(m_pair - m_new)
            o_ref[rq, :] = o_ref[rq, :] * alpha + o_pair * beta
            z_scr[rq, :] = z_scr[rq, :] * alpha + l_pair * beta
            m_scr[rq, :] = m_new
            return carry

        def blocks(heads):
            for dil in dils:
                lax.fori_loop(0, SEQ // BLK, lambda n, carry, dil=dil: step(n, carry, dil, heads), 0, unroll=16)

        if gqa:
            for grp in range(2):
                pl.when(pl.program_id(0) // 2 == grp)(lambda grp=grp: blocks(_HeadStack(grp)))
        else:
            blocks(_HeadStack(None))

        def fin(t, carry):
            rt = pl.ds(pl.multiple_of(t * TM, TM), TM)
            z = z_scr[rt, :]
            o_ref[rt, :] = o_ref[rt, :] / z
            lse_ref[rt, :] = m_scr[rt, :] + jnp.log(z)
            return carry
        lax.fori_loop(0, SEQ // TM, fin, 0)

    col = lambda blk: pl.BlockSpec((SEQ, LANES), lambda p, blk=blk: (0, blk + p))
    kv = (lambda blk: pl.BlockSpec((SEQ, LANES), lambda p, blk=blk: (0, blk))) if gqa else col
    in_specs = [col(qblk), kv(kblk), kv(vblk)]
    args = [proj, proj, proj]
    if has_sink:
        in_specs.append(pl.BlockSpec((1, LANES), lambda p: (0, p)))
        args.append(sink_x)
    out = pl.BlockSpec((SEQ, LANES), lambda p: (0, p))
    return pl.pallas_call(body, name=name, out_shape=(_sds((SEQ, 512)), _sds((SEQ, 512))), grid=(4,),
                          in_specs=in_specs, out_specs=(out, out),
                          scratch_shapes=[pltpu.VMEM((SEQ, LANES), F32), pltpu.VMEM((SEQ, LANES), F32), _BIAS],
                          compiler_params=_cp(48))(*args)


def _attn_bwd(proj, qblk, kblk, vblk, do, o, lse, dils, gqa, sink_x, name):
    has_sink = sink_x is not None

    def body(*refs):
        if has_sink:
            q_ref, k_ref, v_ref, do_ref, o_ref, lse_ref, s_ref, dq_ref, dk_ref, dv_ref, ds_ref, bias_scr = refs
        else:
            q_ref, k_ref, v_ref, do_ref, o_ref, lse_ref, dq_ref, dk_ref, dv_ref, bias_scr = refs
        pid = pl.program_id(0)

        @pl.when(pid == 0)
        def _():
            _fill_band_bias(bias_scr)
        dq_ref[...] = jnp.zeros_like(dq_ref)
        if gqa:
            @pl.when(pid == 0)
            def _():
                dk_ref[...] = jnp.zeros_like(dk_ref)
                dv_ref[...] = jnp.zeros_like(dv_ref)
        else:
            dk_ref[...] = jnp.zeros_like(dk_ref)
            dv_ref[...] = jnp.zeros_like(dv_ref)

        def step(n, carry, dil, heads):
            m0, m1 = heads.m0, heads.m1
            st, stp, first = _block_pos(n, dil)
            rq, rp = _rows(st, dil), _rows(stp, dil)
            do_, lse_ = do_ref[rq, :], lse_ref[rq, :]
            kk = jnp.concatenate([k_ref[rp, :], k_ref[rq, :]], axis=0)
            vv = jnp.concatenate([v_ref[rp, :], v_ref[rq, :]], axis=0)
            qs, dos = heads.stack(q_ref[rq, :] * QK_SCALE), heads.stack(do_)
            doo = do_ * o_ref[rq, :]
            delta = jnp.concatenate([jnp.sum(doo * m0, axis=1, keepdims=True), jnp.sum(doo * m1, axis=1, keepdims=True)], axis=0)
            lse_s = jnp.concatenate([lse_[:, 0:1], lse_[:, HD:HD + 1]], axis=0)
            p = jnp.exp(_mm(qs, kk, NT) + bias_scr[first] - lse_s)
            ds = p * (_mm(dos, vv, NT) - delta)
            dq_ref[rq, :] += heads.unstack(_mm(ds, kk)) * QK_SCALE
            dk_sum, dv_sum = _mm(ds, qs, TN), _mm(p, dos, TN)
            dk_ref[rp, :] += dk_sum[:BLK]
            dk_ref[rq, :] += dk_sum[BLK:]
            dv_ref[rp, :] += dv_sum[:BLK]
            dv_ref[rq, :] += dv_sum[BLK:]
            return carry

        def blocks(heads):
            for dil in dils:
                lax.fori_loop(0, SEQ // BLK, lambda n, carry, dil=dil: step(n, carry, dil, heads), 0, unroll=4)

        if gqa:
            for grp in range(2):
                pl.when(pid // 2 == grp)(lambda grp=grp: blocks(_HeadStack(grp)))
        else:
            blocks(_HeadStack(None))

        if has_sink:
            m0, m1 = _half_masks()

            def sink_rows(t, acc):
                rt = pl.ds(pl.multiple_of(t * TM, TM), TM)
                return acc - jnp.sum(jnp.exp(s_ref[...] - lse_ref[rt, :]) * (do_ref[rt, :] * o_ref[rt, :]), axis=0, keepdims=True)
            acc = lax.fori_loop(0, SEQ // TM, sink_rows, jnp.zeros((1, LANES), F32))
            per_head = jnp.sum(acc * m0, axis=1, keepdims=True) * m0 + jnp.sum(acc * m1, axis=1, keepdims=True) * m1
            ds_ref[0] = jnp.broadcast_to(per_head, (8, LANES))

    col = lambda blk: pl.BlockSpec((SEQ, LANES), lambda p, blk=blk: (0, blk + p))
    kv = (lambda blk: pl.BlockSpec((SEQ, LANES), lambda p, blk=blk: (0, blk))) if gqa else col
    pair = pl.BlockSpec((SEQ, LANES), lambda p: (0, p))
    in_specs = [col(qblk), kv(kblk), kv(vblk), pair, pair, pair]
    args = [proj, proj, proj, do, o, lse]
    kvw = LANES if gqa else 512
    kv_out = pl.BlockSpec((SEQ, LANES), lambda p: (0, 0)) if gqa else pair
    out_shape = [_sds((SEQ, 512)), _sds((SEQ, kvw)), _sds((SEQ, kvw))]
    out_specs = [pair, kv_out, kv_out]
    if has_sink:
        in_specs.append(pl.BlockSpec((1, LANES), lambda p: (0, p)))
        args.append(sink_x)
        out_shape.append(_sds((4, 8, LANES)))
        out_specs.append(pl.BlockSpec((1, 8, LANES), lambda p: (p, 0, 0)))
    return pl.pallas_call(body, name=name, out_shape=tuple(out_shape), grid=(4,), in_specs=in_specs,
                          out_specs=tuple(out_specs), scratch_shapes=[_BIAS], compiler_params=_cp(56))(*args)


_CT = 128


def _rows_before(x_ref, t, k):
    if t == 0:
        return jnp.concatenate([jnp.zeros((k, LANES), F32), x_ref[0:_CT - k, :]], axis=0)
    return x_ref[t * _CT - k:(t + 1) * _CT - k, :]


def _conv_pre(x_ref, w_ref, b_ref, t):
    taps = [x_ref[t * _CT:(t + 1) * _CT, :]] + [_rows_before(x_ref, t, k) for k in range(1, 4)]
    u = b_ref[...] + taps[0] * w_ref[3:4, :]
    for k in range(1, 4):
        u = u + taps[k] * w_ref[3 - k:4 - k, :]
    return u, taps


def _conv_fwd(proj, w, b, name):
    def body(x_ref, w_ref, b_ref, o_ref):
        for t in range(SEQ // _CT):
            o_ref[t * _CT:(t + 1) * _CT, :] = _silu(_conv_pre(x_ref, w_ref, b_ref, t)[0])

    nblk = CONV_CH // LANES
    return pl.pallas_call(body, name=name, out_shape=_sds((SEQ, CONV_CH)), grid=(nblk,),
                          in_specs=[pl.BlockSpec((SEQ, LANES), lambda j: (0, XBC // LANES + j)),
                                    pl.BlockSpec((4, LANES), lambda j: (0, j)), pl.BlockSpec((1, LANES), lambda j: (0, j))],
                          out_specs=pl.BlockSpec((SEQ, LANES), lambda j: (0, j)), compiler_params=_cp())(proj, w, b)


def _conv_bwd(proj, dact, w, b, name):
    def body(x_ref, da_ref, w_ref, b_ref, dx_ref, dw_ref, db_ref, du_scr):
        du_scr[SEQ:SEQ + 8, :] = jnp.zeros((8, LANES), F32)
        db = jnp.zeros((1, LANES), F32)
        dws = [jnp.zeros((1, LANES), F32)] * 4
        for t in range(SEQ // _CT):
            u, taps = _conv_pre(x_ref, w_ref, b_ref, t)
            du = da_ref[t * _CT:(t + 1) * _CT, :] * _dsilu(u)
            du_scr[t * _CT:(t + 1) * _CT, :] = du
            db = db + jnp.sum(du, axis=0, keepdims=True)
            dws = [dws[k] + jnp.sum(du * taps[k], axis=0, keepdims=True) for k in range(4)]
        db_ref[...] = db
        for k in range(4):
            dw_ref[3 - k:4 - k, :] = dws[k]
        for t in range(SEQ // _CT):
            dx = du_scr[t * _CT:(t + 1) * _CT, :] * w_ref[3:4, :]
            for k in range(1, 4):
                dx = dx + du_scr[t * _CT + k:(t + 1) * _CT + k, :] * w_ref[3 - k:4 - k, :]
            dx_ref[t * _CT:(t + 1) * _CT, :] = dx.astype(dx_ref.dtype)

    nblk = CONV_CH // LANES
    blk = pl.BlockSpec((SEQ, LANES), lambda j: (0, j))
    wspec, bspec = pl.BlockSpec((4, LANES), lambda j: (0, j)), pl.BlockSpec((1, LANES), lambda j: (0, j))
    return pl.pallas_call(body, name=name, out_shape=(_sds((SEQ, CONV_CH), MXU), _sds((4, CONV_CH)), _sds((1, CONV_CH))), grid=(nblk,),
                          in_specs=[pl.BlockSpec((SEQ, LANES), lambda j: (0, XBC // LANES + j)), blk, wspec, bspec],
                          out_specs=(blk, wspec, bspec), scratch_shapes=[pltpu.VMEM((SEQ + 8, LANES), F32)],
                          compiler_params=_cp())(proj, dact, w, b)


def _ssd_chunk(xs, bm, cm, dtr, z, hs, al16, dtb, dskx, nw):
    m0, m1 = _half_masks()
    row = lax.broadcasted_iota(jnp.int32, (BLK, BLK), 0)
    col = lax.broadcasted_iota(jnp.int32, (BLK, BLK), 1)
    causal = row >= col
    tril = causal.astype(F32)
    lane = lax.broadcasted_iota(jnp.int32, (1, LANES), 1)
    sub = lax.broadcasted_iota(jnp.int32, (BLK, 1), 0)
    last_row = (sub == BLK - 1).astype(F32)
    dt = jnp.where(lane < 16, _softplus(dtr + dtb), 0.0)
    a16 = -jnp.exp(al16)
    acum = jnp.dot(tril, dt * a16, precision=HI, preferred_element_type=F32)
    acum_t = acum.T
    gmat = [_mm(cm[g], bm[g], NT) for g in range(2)]
    ys, hn = [], []
    for p in range(8):
        g = p // 4
        pick = [(lane == 2 * p + a).astype(F32) for a in range(2)]
        col_h = [jnp.sum(acum * pick[a], axis=1, keepdims=True) for a in range(2)]
        dt_x = sum(jnp.sum(dt * pick[a], axis=1, keepdims=True) * msk for a, msk in enumerate((m0, m1)))
        ac_x = col_h[0] * m0 + col_h[1] * m1
        a_end = jnp.sum(ac_x * last_row, axis=0, keepdims=True)
        xdt = xs[p] * dt_x
        y = _mm(cm[g], hs[p]) * jnp.exp(ac_x)
        for a, msk in enumerate((m0, m1)):
            row_h = jnp.sum(acum_t * (sub == 2 * p + a).astype(F32), axis=0, keepdims=True)
            decay = jnp.exp(jnp.where(causal, col_h[a] - row_h, NEG))
            y = y + _mm(gmat[g] * decay, xdt * msk)
        st = _mm(bm[g], xdt * jnp.exp(a_end - ac_x), TN)
        hn.append(hs[p] * jnp.exp(a_end) + st)
        y = y + dskx[p] * xs[p]
        ys.append(y * _silu(z[p]))
    out = []
    for g in range(2):
        ms = sum(jnp.sum(ys[p] * ys[p], axis=1, keepdims=True) for p in range(4 * g, 4 * g + 4)) * (1.0 / 512)
        rstd = lax.rsqrt(ms + EPS)
        out += [ys[p] * rstd * nw[p] for p in range(4 * g, 4 * g + 4)]
    return out, hn


def _tiles(ref, n, off=0, rows=slice(None)):
    return [ref[rows, off + LANES * p:off + LANES * (p + 1)] for p in range(n)]


def _ssd_load(xbc_ref, z_ref, dt_ref, rows):
    return (_tiles(xbc_ref, 8, 0, rows), _tiles(xbc_ref, 2, 1024, rows), _tiles(xbc_ref, 2, 1280, rows), dt_ref[rows, :],
            _tiles(z_ref, 8, 0, rows))


def _ssd_params(al16_ref, dtb_ref, dsk_ref, nw_ref):
    return al16_ref[...], dtb_ref[...], _tiles(dsk_ref, 8), _tiles(nw_ref, 8)


_NCH = SEQ // BLK
_PER_STEP = 2
_STEP_ROWS = _PER_STEP * BLK


def _ssd_param_specs():
    return [_full((1, LANES)), _full((1, LANES)), _full((1, 1024)), _full((1, 1024))]


def _ssd_fwd(xbc_act, proj, al16, dtb, dskx, nw, name):
    def body(xbc_ref, z_ref, dt_ref, al16_ref, dtb_ref, dsk_ref, nw_ref, y_ref, hin_ref, h_scr):
        @pl.when(pl.program_id(0) == 0)
        def _():
            h_scr[...] = jnp.zeros_like(h_scr)
        params = _ssd_params(al16_ref, dtb_ref, dsk_ref, nw_ref)
        hs = _tiles(h_scr, 8)
        for k in range(_PER_STEP):
            rows = slice(BLK * k, BLK * (k + 1))
            for p in range(8):
                hin_ref[k, :, LANES * p:LANES * (p + 1)] = hs[p]
            ys, hs = _ssd_chunk(*_ssd_load(xbc_ref, z_ref, dt_ref, rows), hs, *params)
            for p in range(8):
                y_ref[rows, LANES * p:LANES * (p + 1)] = ys[p].astype(y_ref.dtype)
        for p in range(8):
            h_scr[:, LANES * p:LANES * (p + 1)] = hs[p]

    return pl.pallas_call(
        body, name=name, out_shape=(_sds((SEQ, 1024), MXU), _sds((_NCH, BLK, 1024))), grid=(_NCH // _PER_STEP,),
        in_specs=[pl.BlockSpec((_STEP_ROWS, CONV_CH), lambda c: (c, 0)), pl.BlockSpec((_STEP_ROWS, 1024), lambda c: (c, ZB // 1024)),
                  pl.BlockSpec((_STEP_ROWS, LANES), lambda c: (c, DTC // LANES))] + _ssd_param_specs(),
        out_specs=(pl.BlockSpec((_STEP_ROWS, 1024), lambda c: (c, 0)), pl.BlockSpec((_PER_STEP, BLK, 1024), lambda c: (c, 0, 0))),
        scratch_shapes=[pltpu.VMEM((BLK, 1024), F32)], compiler_params=_cp())(xbc_act, proj, proj, al16, dtb, dskx, nw)


def _ssd_bwd(xbc_act, proj, hin, dyb, al16, dtb, dskx, nw, name):
    def body(xbc_ref, z_ref, dt_ref, hin_ref, dy_ref, al16_ref, dtb_ref, dsk_ref, nw_ref,
             dxbc_ref, dz_ref, ddt_ref, dal16_ref, ddtb_ref, ddsk_ref, dnw_ref, dh_scr):
        @pl.when(pl.program_id(0) == 0)
        def _():
            dh_scr[...] = jnp.zeros_like(dh_scr)
            for r in (dal16_ref, ddtb_ref, ddsk_ref, dnw_ref):
                r[...] = jnp.zeros_like(r)
        params = _ssd_params(al16_ref, dtb_ref, dsk_ref, nw_ref)
        dhs = _tiles(dh_scr, 8)
        for k in reversed(range(_PER_STEP)):
            rows = slice(BLK * k, BLK * (k + 1))
            hs = [hin_ref[k, :, LANES * p:LANES * (p + 1)] for p in range(8)]
            _, vjp = jax.vjp(lambda a, h, q: _ssd_chunk(*a, h, *q), _ssd_load(xbc_ref, z_ref, dt_ref, rows), hs, params)
            (dxs, dbm, dcm, ddt, dz), dhs, (dal16, ddtb, ddsk, dnw) = vjp((_tiles(dy_ref, 8, 0, rows), dhs))
            for p in range(8):
                cols = slice(LANES * p, LANES * (p + 1))
                dxbc_ref[rows, cols] = dxs[p]
                dz_ref[rows, cols] = dz[p].astype(dz_ref.dtype)
                ddsk_ref[:, cols] += ddsk[p]
                dnw_ref[:, cols] += dnw[p]
            for g in range(2):
                dxbc_ref[rows, 1024 + LANES * g:1024 + LANES * (g + 1)] = dbm[g]
                dxbc_ref[rows, 1280 + LANES * g:1280 + LANES * (g + 1)] = dcm[g]
            ddt_ref[rows, :] = ddt.astype(ddt_ref.dtype)
            dal16_ref[...] += dal16
            ddtb_ref[...] += ddtb
        for p in range(8):
            dh_scr[:, LANES * p:LANES * (p + 1)] = dhs[p]

    rev = lambda c: _NCH // _PER_STEP - 1 - c
    return pl.pallas_call(
        body, name=name,
        out_shape=(_sds((SEQ, CONV_CH)), _sds((SEQ, 1024), MXU), _sds((SEQ, LANES), MXU),
                   _sds((1, LANES)), _sds((1, LANES)), _sds((1, 1024)), _sds((1, 1024))),
        grid=(_NCH // _PER_STEP,),
        in_specs=[pl.BlockSpec((_STEP_ROWS, CONV_CH), lambda c: (rev(c), 0)), pl.BlockSpec((_STEP_ROWS, 1024), lambda c: (rev(c), ZB // 1024)),
                  pl.BlockSpec((_STEP_ROWS, LANES), lambda c: (rev(c), DTC // LANES)),
                  pl.BlockSpec((_PER_STEP, BLK, 1024), lambda c: (rev(c), 0, 0)),
                  pl.BlockSpec((_STEP_ROWS, 1024), lambda c: (rev(c), 0))] + _ssd_param_specs(),
        out_specs=(pl.BlockSpec((_STEP_ROWS, CONV_CH), lambda c: (rev(c), 0)), pl.BlockSpec((_STEP_ROWS, 1024), lambda c: (rev(c), 0)),
                   pl.BlockSpec((_STEP_ROWS, LANES), lambda c: (rev(c), 0)),
                   _full((1, LANES)), _full((1, LANES)), _full((1, 1024)), _full((1, 1024))),
        scratch_shapes=[pltpu.VMEM((BLK, 1024), F32)], compiler_params=_cp())(xbc_act, proj, proj, hin, dyb, al16, dtb, dskx, nw)


def _rstd(v):
    return lax.rsqrt(jnp.mean(v * v, axis=1, keepdims=True) + EPS)


def _rms_bwd(dn, n, rstd):
    return rstd * (dn - n * jnp.mean(dn * n, axis=1, keepdims=True))


_VEC = _full((1, D))


def _layer_spec(layer):
    return pl.BlockSpec((None, 2048, D), lambda *_: (layer, 0, 0))

_ROW = pl.BlockSpec((TM, D), lambda i, *_: (i, 0))


def _proj_fwd(x, pre_w, scale, shift, w, layer, name):
    tn, ni = 1024, SEQ // TM

    def body(x_ref, pw_ref, sc_ref, sh_ref, w_ref, o_ref, h_ref, h_scr):
        rows = pl.ds(pl.multiple_of(pl.program_id(1) * TM, TM), TM)

        @pl.when(pl.program_id(0) == 0)
        def _():
            xv = x_ref[...]
            h = ((xv * _rstd(xv) * pw_ref[...]) * (1.0 + sc_ref[...]) + sh_ref[...]).astype(h_ref.dtype)
            h_scr[rows, :] = h
            h_ref[...] = h
        o_ref[...] = jnp.dot(h_scr[rows, :], w_ref[...].astype(MXU), preferred_element_type=F32)

    first_pass = pl.BlockSpec((TM, D), lambda j, i: (jnp.where(j == 0, i, ni - 1), 0))
    return pl.pallas_call(body, name=name, out_shape=(_sds((SEQ, NP)), _sds((SEQ, D), MXU)), grid=(NP // tn, ni),
                          in_specs=[first_pass, _VEC, _VEC, _VEC, pl.BlockSpec((None, D, tn), lambda j, i: (layer, 0, j))],
                          out_specs=(pl.BlockSpec((TM, tn), lambda j, i: (i, j)), first_pass),
                          scratch_shapes=[pltpu.VMEM((SEQ, D), MXU)], compiler_params=_cp())(x, pre_w, scale, shift, w)


_HALF = pl.BlockSpec((TM, 512), lambda i: (i, 0))
_Z_A = pl.BlockSpec((TM, 512), lambda i: (i, ZA // 512))
_Z_C = pl.BlockSpec((TM, 512), lambda i: (i, ZC // 512))


def _out_fwd(o_a, yb, o_c, proj, w, layer, x, gate, post_w, name):
    def body(oa_ref, yb_ref, oc_ref, za_ref, zc_ref, w_ref, x_ref, g_ref, pw_ref, xn_ref, y_ref):
        y = (_mm(oa_ref[...] * _silu(za_ref[...]), w_ref[0:512, :]) + _mm(yb_ref[...], w_ref[512:1536, :])
             + _mm(oc_ref[...] * _silu(zc_ref[...]), w_ref[1536:2048, :]))
        y_ref[...] = y
        xn_ref[...] = x_ref[...] + g_ref[...] * (y * _rstd(y) * pw_ref[...])

    return pl.pallas_call(body, name=name, out_shape=(_sds((SEQ, D)), _sds((SEQ, D))), grid=(SEQ // TM,),
                          in_specs=[_HALF, _ROW, _HALF, _Z_A, _Z_C, _layer_spec(layer), _ROW, _VEC, _VEC],
                          out_specs=(_ROW, _ROW), compiler_params=_cp())(o_a, yb, o_c, proj, proj, w, x, gate, post_w)


def _dymix(dxo, y, gate, post_w, w, layer, o_a, o_c, proj, name):
    def body(dx_ref, y_ref, g_ref, pw_ref, w_ref, oa_ref, oc_ref, za_ref, zc_ref,
             dy_ref, dg_ref, dpw_ref, doa_ref, dza_ref, b_ref, doc_ref, dzc_ref):
        @pl.when(pl.program_id(0) == 0)
        def _():
            dg_ref[...] = jnp.zeros_like(dg_ref)
            dpw_ref[...] = jnp.zeros_like(dpw_ref)
        dx, yv = dx_ref[...], y_ref[...]
        rstd = _rstd(yv)
        n = yv * rstd
        dg_ref[...] += jnp.sum(dx * (n * pw_ref[...]), axis=0, keepdims=True)
        dr = dx * g_ref[...]
        dpw_ref[...] += jnp.sum(dr * n, axis=0, keepdims=True)
        dy = _rms_bwd(dr * pw_ref[...], n, rstd)
        dy_ref[...] = dy
        b_ref[...] = _mm(dy, w_ref[512:1536, :], NT)
        for rows, o_ref, z_ref, do_ref, dz_ref in ((slice(0, 512), oa_ref, za_ref, doa_ref, dza_ref),
                                                   (slice(1536, 2048), oc_ref, zc_ref, doc_ref, dzc_ref)):
            dyg, z = _mm(dy, w_ref[rows, :], NT), z_ref[...]
            do_ref[...] = dyg * _silu(z)
            dz_ref[...] = (dyg * o_ref[...] * _dsilu(z)).astype(dz_ref.dtype)

    return pl.pallas_call(body, name=name,
                          out_shape=(_sds((SEQ, D)), _sds((1, D)), _sds((1, D)),
                                     _sds((SEQ, 512)), _sds((SEQ, 512), MXU), _sds((SEQ, D)), _sds((SEQ, 512)), _sds((SEQ, 512), MXU)),
                          grid=(SEQ // TM,), in_specs=[_ROW, _ROW, _VEC, _VEC, _layer_spec(layer), _HALF, _HALF, _Z_A, _Z_C],
                          out_specs=(_ROW, _VEC, _VEC, _HALF, _HALF, _ROW, _HALF, _HALF),
                          compiler_params=_cp())(dxo, y, gate, post_w, w, o_a, o_c, proj, proj)


def _dwout(o_a, yb, o_c, proj, dy, name):
    def body(oa_ref, yb_ref, oc_ref, za_ref, zc_ref, dy_ref, o_ref):
        @pl.when(pl.program_id(0) == 0)
        def _():
            o_ref[...] = jnp.zeros_like(o_ref)
        dy = dy_ref[...]
        o_ref[0:512, :] += _mm(oa_ref[...] * _silu(za_ref[...]), dy, TN)
        o_ref[512:1536, :] += _mm(yb_ref[...], dy, TN)
        o_ref[1536:2048, :] += _mm(oc_ref[...] * _silu(zc_ref[...]), dy, TN)

    return pl.pallas_call(body, name=name, out_shape=_sds((2048, D)), grid=(SEQ // TM,),
                          in_specs=[_HALF, _ROW, _HALF, _Z_A, _Z_C, _ROW], out_specs=_full((2048, D)),
                          compiler_params=_cp())(o_a, yb, o_c, proj, proj, dy)


def _dwin(h, pieces, name):
    n = len(pieces)
    widths = [p.shape[1] for p in pieces]
    half = NP // 2

    def body(*refs):
        h_ref, p_refs, o_ref = refs[0], refs[1:1 + n], refs[1 + n]

        @pl.when(pl.program_id(0) == 0)
        def _():
            o_ref[...] = jnp.zeros_like(o_ref)
        hv, c0 = h_ref[...], 0
        for p_ref, wd in zip(p_refs, widths):
            o_ref[:, c0:c0 + wd] += _mm(hv, p_ref[...], TN)
            c0 += wd

    return pl.pallas_call(body, name=name, out_shape=_sds((D, half)), grid=(SEQ // TM,),
                          in_specs=[_ROW] + [pl.BlockSpec((TM, wd), lambda k: (k, 0)) for wd in widths],
                          out_specs=_full((D, half)), compiler_params=_cp(56))(h, *pieces)


_TMH = 256


def _dh_bwd(pieces, w, x, pre_w, scale, dxo, name):
    n = len(pieces)
    widths = [p.shape[1] for p in pieces]

    def body(*refs):
        p_refs, (w_ref, x_ref, pw_ref, sc_ref, dxo_ref, dx_ref, dsh_ref, dsc_ref, dpw_ref) = refs[:n], refs[n:]

        @pl.when(pl.program_id(0) == 0)
        def _():
            for r in (dsh_ref, dsc_ref, dpw_ref):
                r[...] = jnp.zeros_like(r)
        dh, c0 = 0.0, 0
        for p_ref, wd in zip(p_refs, widths):
            dh = dh + _mm(p_ref[...], w_ref[:, c0:c0 + wd], NT)
            c0 += wd
        xv = x_ref[...]
        rstd = _rstd(xv)
        nrm = xv * rstd
        dsh_ref[...] += jnp.sum(dh, axis=0, keepdims=True)
        dsc_ref[...] += jnp.sum(dh * (nrm * pw_ref[...]), axis=0, keepdims=True)
        dhn = dh * (1.0 + sc_ref[...])
        dpw_ref[...] += jnp.sum(dhn * nrm, axis=0, keepdims=True)
        dx_ref[...] = _rms_bwd(dhn * pw_ref[...], nrm, rstd) + dxo_ref[...]

    row = pl.BlockSpec((_TMH, D), lambda i: (i, 0))
    return pl.pallas_call(body, name=name, out_shape=(_sds((SEQ, D)), _sds((1, D)), _sds((1, D)), _sds((1, D))),
                          grid=(SEQ // _TMH,),
                          in_specs=[pl.BlockSpec((_TMH, wd), lambda i: (i, 0)) for wd in widths]
                          + [pl.BlockSpec((None, D, NP), lambda i: (0, 0, 0)), row, _VEC, _VEC, row],
                          out_specs=(row, _VEC, _VEC, _VEC), compiler_params=_cp(56))(*pieces, w, x, pre_w, scale, dxo)


def _w_in_padded(land, name):
    rows = 128

    def body(l_ref, o_ref):
        o_ref[...] = _pad_cols(jnp.concatenate([l_ref[k] for k in range(4)], axis=1))

    return pl.pallas_call(body, name=name, out_shape=_sds((D, NP), land.dtype), grid=(D // rows,),
                          in_specs=[pl.BlockSpec((4, rows, SHARD_IN), lambda i: (0, i, 0))],
                          out_specs=pl.BlockSpec((rows, NP), lambda i: (i, 0)), compiler_params=_cp())(land)


def _grad_blocks(dwa, dwb, name):
    rows = 128

    def body(a_ref, b_ref, o_ref):
        g = _unpad_cols(jnp.concatenate([a_ref[...], b_ref[...]], axis=1))
        for k in range(4):
            o_ref[k] = g[:, SHARD_IN * k:SHARD_IN * (k + 1)].astype(o_ref.dtype)

    half = pl.BlockSpec((rows, NP // 2), lambda i: (i, 0))
    return pl.pallas_call(body, name=name, out_shape=_sds((4, D, SHARD_IN), jnp.bfloat16), grid=(D // rows,),
                          in_specs=[half, half], out_specs=pl.BlockSpec((4, rows, SHARD_IN), lambda i: (0, i, 0)),
                          compiler_params=_cp())(dwa, dwb)


def _loss_bwd(xf, tgt, name):
    def body(x_ref, t_ref, dx_ref, l_ref):
        @pl.when(pl.program_id(0) == 0)
        def _():
            l_ref[...] = jnp.zeros_like(l_ref)
        e = x_ref[...] - t_ref[...]
        dx_ref[...] = e * (1.0 / D)
        l_ref[...] += 0.5 * jnp.sum(jnp.mean(e * e, axis=1, keepdims=True), axis=0, keepdims=True)

    return pl.pallas_call(body, name=name, out_shape=(_sds((SEQ, D)), _sds((8, LANES))), grid=(SEQ // TM,),
                          in_specs=[_ROW, _ROW], out_specs=(_ROW, _full((8, LANES))), compiler_params=_cp())(xf, tgt)


def _mod_part(c_all, ada_w, ada_b, name):
    def body(c_ref, w_ref, b_ref, o_ref):
        o_ref[0] = _mm(_silu(c_ref[...]), w_ref[0]) + b_ref[0]

    return pl.pallas_call(body, name=name, out_shape=_sds((DEPTH, 8, 768)), grid=(DEPTH,),
                          in_specs=[_full((8, D)), pl.BlockSpec((1, D, 768), lambda i: (i, 0, 0)), pl.BlockSpec((1, 1, 768), lambda i: (i, 0, 0))],
                          out_specs=pl.BlockSpec((1, 8, 768), lambda i: (i, 0, 0)), compiler_params=_cp())(c_all, ada_w, ada_b)


def _ada_grad(c_t, dmod, name):
    def body(c_ref, d_ref, o_ref):
        ca = _silu(c_ref[...])
        dm = d_ref[0]
        acc = ca[:, 0:1] * dm[0:1, :]
        for s in range(1, 8):
            acc = acc + ca[:, s:s + 1] * dm[s:s + 1, :]
        o_ref[0] = acc

    return pl.pallas_call(body, name=name, out_shape=_sds((DEPTH, D, 768)), grid=(DEPTH,),
                          in_specs=[_full((D, LANES)), pl.BlockSpec((1, 8, 768), lambda i: (i, 0, 0))],
                          out_specs=pl.BlockSpec((1, D, 768), lambda i: (i, 0, 0)), compiler_params=_cp())(c_t, dmod)


def _pack(parts):
    flat = []
    for p in parts:
        f = p.reshape(-1)
        flat.append(jnp.pad(f, (0, (-f.size) % LANES)))
    v = jnp.concatenate(flat)
    return jnp.pad(v, (0, (-v.size) % (8 * LANES))).reshape(-1, LANES)


def _unpack(v, shapes):
    v = v.reshape(-1)
    out, off = [], 0
    for s in shapes:
        n = math.prod(s)
        out.append(v[off:off + n].reshape(s))
        off += n + (-n) % LANES
    return out


_GIVEN_DT, _GIVEN_C = 4608, 4624


def _pad_cols(w):
    return jnp.concatenate([w[..., :_GIVEN_DT], w[..., _GIVEN_C:], w[..., _GIVEN_DT:_GIVEN_C],
                            jnp.zeros(w.shape[:-1] + (NP - IN_COLS,), w.dtype)], axis=-1)


def _unpad_cols(w):
    return jnp.concatenate([w[..., :_GIVEN_DT], w[..., DTC:DTC + 16], w[..., _GIVEN_DT:DTC]], axis=-1)


def _pad_lanes(v):
    return jnp.pad(v, (0, LANES - v.shape[0])).reshape(1, LANES)


def _local_step(x2, tgt, mod, weights_of, grads_done, pre_w, post_w, conv_w, conv_b, dt_bias, a_log, d_skip, nw, sinks):
    saved = []
    xcur = x2
    for i in range(DEPTH):
        shift, scale, gate = mod[i:i + 1, :D], mod[i:i + 1, D:2 * D], mod[i:i + 1, 2 * D:]
        pw, qw = pre_w[i:i + 1], post_w[i:i + 1]
        w_p, w_o = weights_of(i, xcur)
        proj, h = _proj_fwd(xcur, pw, scale, shift, w_p, 0, "proj_fwd")
        o_a, lse_a = _attn_fwd(proj, QA // LANES, KA // LANES, VA // LANES, DILS, False, None, "attn_a_fwd")
        sink_x = jnp.repeat(sinks[i], HD).reshape(1, 512)
        o_c, lse_c = _attn_fwd(proj, QC // LANES, KC // LANES, VC // LANES, (1,), True, sink_x, "attn_c_fwd")
        cw, cb = conv_w[i], conv_b[i:i + 1]
        xbc_act = _conv_fwd(proj, cw, cb, "conv_fwd")
        ssd_p = (_pad_lanes(a_log[i]), _pad_lanes(dt_bias[i]), jnp.repeat(d_skip[i], HD).reshape(1, 1024), nw[i:i + 1])
        yb, hin = _ssd_fwd(xbc_act, proj, *ssd_p, "ssd_fwd")
        xnew, y = _out_fwd(o_a, yb, o_c, proj, w_o, 0, xcur, gate, qw, "out_fwd")
        saved.append((w_p, w_o, xcur, scale, gate, pw, qw, proj, h, o_a, lse_a, sink_x, o_c, lse_c, cw, cb, xbc_act, ssd_p, yb, hin, y))
        xcur = xnew
    dx, ltile = _loss_bwd(xcur, tgt, "loss")
    dmod, small = [None] * DEPTH, [None] * DEPTH
    for i in reversed(range(DEPTH)):
        w_p, w_o, xin, scale, gate, pw, qw, proj, h, o_a, lse_a, sink_x, o_c, lse_c, cw, cb, xbc_act, ssd_p, yb, hin, y = saved[i]
        dy, dgate, dpost, do_a, dz_a, dyb, do_c, dz_c = _dymix(dx, y, gate, qw, w_o, 0, o_a, o_c, proj, "dymix")
        dwo = _dwout(o_a, yb, o_c, proj, dy, "dwout")
        dq_a, dk_a, dv_a = _attn_bwd(proj, QA // LANES, KA // LANES, VA // LANES, do_a, o_a, lse_a, DILS, False, None, "attn_a_bwd")
        dq_c, dk_c, dv_c, dsk = _attn_bwd(proj, QC // LANES, KC // LANES, VC // LANES, do_c, o_c, lse_c, (1,), True, sink_x, "attn_c_bwd")
        dxbc_act, dz_b, ddt, dal16, ddtb, ddsk, dnw = _ssd_bwd(xbc_act, proj, hin, dyb, *ssd_p, "ssd_bwd")
        dxbc, dcw, dcb = _conv_bwd(proj, dxbc_act, cw, cb, "conv_bwd")
        half_a, half_b = [dq_a, dk_a, dv_a, dz_a, dz_b], [dxbc, dq_c, dz_c, dk_c, dv_c, ddt]
        sent = grads_done(i, _dwin(h, half_a, "dwin_a"), _dwin(h, half_b, "dwin_b"), dwo)
        dx, dshift, dscale, dpre = _dh_bwd(half_a + half_b, w_p, xin, pw, scale + sent[0, 0], dx, "dh_bwd")
        dmod[i] = jnp.concatenate([dshift, dscale, dgate], axis=1)
        small[i] = (dpre, dpost, dcw, dcb, ddtb[0, :16], dal16[0, :16], ddsk.reshape(16, HD).sum(axis=1), dnw, dsk[:, 0, ::HD].reshape(8))
    return ltile, dx, jnp.concatenate(dmod, axis=0), small


_SMALL = ((1, D), (1, D), (4, CONV_CH), (1, CONV_CH), (16,), (16,), (16,), (1, D), (8,))


def kernel(x, c, ada_w, ada_b, pre_norm_w, post_norm_w, w_in, conv_w, conv_b, dt_bias, a_log, d_skip, ssm_norm_w, sinks, w_out, loss_target, m_ada_w, m_ada_b, m_pre_norm_w, m_post_norm_w, m_w_in, m_conv_w, m_conv_b, m_dt_bias, m_a_log, m_d_skip, m_ssm_norm_w, m_sinks, m_w_out, v_ada_w, v_ada_b, v_pre_norm_w, v_post_norm_w, v_w_in, v_conv_w, v_conv_b, v_dt_bias, v_a_log, v_d_skip, v_ssm_norm_w, v_sinks, v_w_out):
    xi, yi, ci = lax.axis_index("x"), lax.axis_index("y"), lax.axis_index("c")
    chip = 2 * xi + yi
    me = 2 * chip + ci

    w_in_b = _cast_bf16(w_in, 512, "cast_w_in")
    w_out_b = _cast_bf16(w_out, 512, "cast_w_out")
    gathers = []
    for i in range(DEPTH):
        lands = [lax.dynamic_update_slice(lax.empty((4,) + a.shape[1:], a.dtype), a[i][None], (chip, 0, 0)) for a in (w_in_b, w_out_b)]
        gathers.append(_split_start(None, lands, f"gather_start{i}", "half" if i == 0 else "whole"))
    all_started = gathers[0][3] + gathers[1][3] + gathers[2][3] + gathers[3][3]

    def weights_of(i, after):
        send_sems, recv_sems, thru, _ = gathers[i]
        if i == 0:
            halves = _split_wait(send_sems, recv_sems, thru, 2, all_started + mod[:1, :LANES], "gather_wait0", "half")
            send_sems, recv_sems, thru, after = _split_start(None, halves, "share_start0", "sibling")
            g_in, g_out = _split_wait(send_sems, recv_sems, thru, 2, after, "share_wait0", "sibling")
        else:
            g_in, g_out = _split_wait(send_sems, recv_sems, thru, 2, after, f"gather_wait{i}")
        return _w_in_padded(g_in, "w_in_padded")[None], g_out.reshape(1, 2048, D)

    scatters = [None] * DEPTH

    def grads_done(i, dwa, dwb, dwo):
        blocks = [_grad_blocks(dwa, dwb, "grad_blocks"), _cast_bf16(dwo.reshape(4, 512, D), 512, "cast_dw_out")]
        scatters[i] = _split_start(blocks, [lax.empty(b.shape, b.dtype) for b in blocks], f"scatter_start{i}")
        return scatters[i][3]

    g0 = _allgather8(_pack([c, conv_w]), "gather_c")
    c_all = g0[:, :8, :].reshape(8, D)
    conv_w_full = jnp.concatenate([g0[2 * k, 8:56, :].reshape(DEPTH, 4, CONV_CH // 4) for k in range(4)], axis=-1)

    ada_b_mine = lax.dynamic_slice_in_dim(ada_b, 768 * chip, 768, axis=1).reshape(DEPTH, 1, 768)
    gm = _allgather8(_mod_part(c_all, ada_w, ada_b_mine, "mod_part").reshape(DEPTH * 8, 768), "gather_mod")
    gm = gm.reshape(4, 2, DEPTH, 8, 768)[:, 0]
    mod = lax.dynamic_index_in_dim(gm, me, axis=2, keepdims=False).transpose(1, 0, 2).reshape(DEPTH, 3 * D)

    ltile, dx, dmod, small = _local_step(x[0], loss_target[0], mod, weights_of, grads_done, pre_norm_w, post_norm_w, conv_w_full,
                                         conv_b, dt_bias, a_log, d_skip, ssm_norm_w, sinks)

    packed = _pack([dmod] + [g for layer in small for g in layer] + [ltile[0]])
    gs = _allgather8(packed, "gather_small")
    tot = _sum_blocks(gs[:, None], packed.shape[0], "sum_small")[0]
    parts = _unpack(tot, [(DEPTH, 3 * D)] + list(_SMALL) * DEPTH + [(LANES,)])
    g_ada_b, loss = parts[0], parts[-1][0]
    per_layer = [parts[1 + len(_SMALL) * i:1 + len(_SMALL) * (i + 1)] for i in range(DEPTH)]
    g_pre, g_post, g_cw, g_cb, g_dtb, g_al, g_dsk, g_nw, g_sk = [jnp.stack([per_layer[i][j] for i in range(DEPTH)]) for j in range(len(_SMALL))]
    g_pre, g_post, g_cb, g_nw = g_pre[:, 0], g_post[:, 0], g_cb[:, 0], g_nw[:, 0]
    g_cw = lax.dynamic_slice_in_dim(g_cw, (CONV_CH // 4) * chip, CONV_CH // 4, axis=2)

    dmod_all = gs[:, :(DEPTH * 3 * D) // LANES, :].reshape(8, DEPTH, 3 * D).transpose(1, 0, 2)
    dmod_mine = lax.dynamic_slice_in_dim(dmod_all, 768 * chip, 768, axis=2)
    c_t = jnp.pad(c_all.T, ((0, 0), (0, LANES - 8)))
    g_ada_w = _ada_grad(c_t, dmod_mine, "ada_grad")

    res = {}
    res["ada_w"] = _adamw(ada_w, [g_ada_w], m_ada_w, v_ada_w, 512, "adamw_ada_w")
    names = ["ada_b", "pre_norm_w", "post_norm_w", "conv_w", "conv_b", "dt_bias", "a_log", "d_skip", "ssm_norm_w", "sinks"]
    ws = [ada_b, pre_norm_w, post_norm_w, conv_w, conv_b, dt_bias, a_log, d_skip, ssm_norm_w, sinks]
    gsm = [g_ada_b, g_pre, g_post, g_cw, g_cb, g_dtb, g_al, g_dsk, g_nw, g_sk]
    ms = [m_ada_b, m_pre_norm_w, m_post_norm_w, m_conv_w, m_conv_b, m_dt_bias, m_a_log, m_d_skip, m_ssm_norm_w, m_sinks]
    vs = [v_ada_b, v_pre_norm_w, v_post_norm_w, v_conv_w, v_conv_b, v_dt_bias, v_a_log, v_d_skip, v_ssm_norm_w, v_sinks]
    pw_, pg_, pm_, pv_ = _pack(ws), _pack(gsm), _pack(ms), _pack(vs)
    small_out = _adamw(pw_[None], [pg_[None]], pm_[None], pv_[None], pw_.shape[0], "adamw_small")

    others_done = small_out[1][0, :8] + res["ada_w"][1][0, :8, :LANES]
    landed = [_split_wait(*scatters[i][:3], 2, others_done, f"scatter_wait{i}") for i in range(DEPTH)]
    p_in = _sum_chips([d[2] for d in landed], [d[0] for d in landed], 128, "sum_w_in")
    p_out = _sum_chips([d[3] for d in landed], [d[1] for d in landed], 256, "sum_w_out")
    col_major, row_major = (lambda a: jnp.transpose(a, (2, 0, 1))), (lambda a: jnp.transpose(a, (1, 2, 0)))
    p_in = col_major(p_in)
    s_in, s_out = _sibling_swap([p_in, p_out], "swap_partials")
    res["w_in"] = [row_major(a) for a in _adamw(col_major(w_in), [p_in, s_in], col_major(m_w_in), col_major(v_w_in), None,
                                                "adamw_w_in", lead=SHARD_IN // 18)]
    res["w_out"] = _adamw(w_out, [p_out, s_out], m_w_out, v_w_out, 512, "adamw_w_out")
    shapes = [w.shape for w in ws]
    for kind in range(4):
        for nm, a in zip(names, _unpack(small_out[kind][0], shapes)):
            res.setdefault(nm, [None] * 4)[kind] = a
    order = ["ada_w", "ada_b", "pre_norm_w", "post_norm_w", "w_in", "conv_w", "conv_b", "dt_bias", "a_log", "d_skip", "ssm_norm_w", "sinks", "w_out"]
    return (loss, dx[None], *[res[n][0] for n in order], *[res[n][1] for n in order], *[res[n][2] for n in order], *[res[n][3] for n in order])
```

```python
import math

import jax
import jax.numpy as jnp
from jax import lax
from jax.experimental import pallas as pl
from jax.experimental.pallas import tpu as pltpu

F32 = jnp.float32
MXU = jnp.bfloat16
HI = lax.Precision.HIGHEST
MESH = pl.DeviceIdType.MESH

SEQ = 4096
D = 1024
DEPTH = 4
HD = 64
QK_SCALE = HD ** -0.5
LANES = 128
BLK = 128
DILS = (1, 4, 16)
NEG = -1e30
EPS = 1e-6
MIB = 1024 * 1024

NP = 6144
QA, KA, VA, ZA = 0, 512, 1024, 1536
ZB, XBC = 2048, 3072
QC, ZC, KC, VC = 4608, 5120, 5632, 5760
DTC = 5888
IN_COLS = 5904
SHARD_IN = IN_COLS // 4
CONV_CH = 1536
TM = 512

ADAM_LR, ADAM_B1, ADAM_B2, ADAM_EPS, ADAM_WD, ADAM_STEP = 0.001, 0.9, 0.999, 1e-08, 0.01, 10

NT = (((1,), (1,)), ((), ()))
TN = (((0,), (0,)), ((), ()))


def _cp(vmem_mib=48):
    return pltpu.CompilerParams(vmem_limit_bytes=vmem_mib * MIB)


def _sds(shape, dtype=F32):
    return jax.ShapeDtypeStruct(shape, dtype)


def _full(shape):
    n = len(shape)
    return pl.BlockSpec(shape, lambda *_: (0,) * n)


def _mm(a, b, dims=None):
    if dims is None:
        return jnp.dot(a.astype(MXU), b.astype(MXU), preferred_element_type=F32)
    return lax.dot_general(a.astype(MXU), b.astype(MXU), dims, preferred_element_type=F32)


def _sigmoid(x):
    return 1.0 / (1.0 + jnp.exp(-x))


def _silu(x):
    return x * _sigmoid(x)


def _dsilu(x):
    s = _sigmoid(x)
    return s * (1.0 + x * (1.0 - s))


def _softplus(x):
    ax = jnp.where(x >= 0, x, -x)
    return jnp.maximum(x, 0.0) + jnp.log1p(jnp.exp(-ax))


def _half_masks():
    lane = lax.broadcasted_iota(jnp.int32, (1, LANES), 1)
    m0 = (lane < HD).astype(F32)
    return m0, 1.0 - m0


def _allgather8(v, name):
    r, cc = v.shape

    def body(v_ref, out_ref, send_sems, recv_sems):
        x, y, c = lax.axis_index("x"), lax.axis_index("y"), lax.axis_index("c")
        me = 4 * x + 2 * y + c
        out_ref[me] = v_ref[...]
        peers = []
        for k in range(1, 8):
            px = 1 - x if k & 4 else x
            py = 1 - y if k & 2 else y
            pc = 1 - c if k & 1 else c
            peers.append((px, py, pc))
        sends = []
        for k, peer in enumerate(peers):
            cp = pltpu.make_async_remote_copy(src_ref=v_ref, dst_ref=out_ref.at[me], send_sem=send_sems.at[k],
                                              recv_sem=recv_sems.at[k], device_id=peer, device_id_type=MESH)
            cp.start()
            sends.append(cp)
        for k, (px, py, pc) in enumerate(peers):
            pltpu.make_async_remote_copy(src_ref=v_ref, dst_ref=out_ref.at[4 * px + 2 * py + pc], send_sem=send_sems.at[k],
                                         recv_sem=recv_sems.at[k], device_id=(px, py, pc), device_id_type=MESH).wait_recv()
        for cp in sends:
            cp.wait_send()

    return pl.pallas_call(
        body, name=name, out_shape=_sds((8, r, cc)),
        in_specs=[pl.BlockSpec(memory_space=pltpu.VMEM)], out_specs=pl.BlockSpec(memory_space=pltpu.VMEM),
        scratch_shapes=[pltpu.SemaphoreType.DMA((7,)), pltpu.SemaphoreType.DMA((7,))],
        compiler_params=_cp(32),
    )(v)


_HBM = pl.BlockSpec(memory_space=pltpu.HBM)
_SEM = pl.BlockSpec(memory_space=pltpu.SEMAPHORE)
_EFFECT = pltpu.SideEffectType.DATAFLOW_SIDE_EFFECTING


def _chip_copies(src_refs, land_refs, send_sems, recv_sems, part="whole"):
    x, y, c = lax.axis_index("x"), lax.axis_index("y"), lax.axis_index("c")
    mine = 2 * x + y
    out = []
    for i, land in enumerate(land_refs):
        half = land.shape[1] // 2
        own, others = pl.ds(pl.multiple_of(c * half, half), half), pl.ds(pl.multiple_of((1 - c) * half, half), half)
        for j, (px, py) in enumerate([(1 - x, y), (x, 1 - y), (1 - x, 1 - y)]):
            slot, peer = 2 * px + py, (px, py, c)
            if part == "whole":
                src = src_refs[i].at[slot] if src_refs else land.at[mine]
                there, here = land.at[mine], land.at[slot]
            elif part == "half":
                src = there = land.at[mine].at[own]
                here = land.at[slot].at[own]
            else:
                src = there = land.at[slot].at[own]
                here, peer = land.at[slot].at[others], (x, y, 1 - c)
            mk = lambda dst, i=i, j=j, src=src, peer=peer: pltpu.make_async_remote_copy(
                src_ref=src, dst_ref=dst, send_sem=send_sems.at[3 * i + j], recv_sem=recv_sems.at[3 * i + j],
                device_id=peer, device_id_type=MESH)
            out.append((mk(there), mk(here)))
    return out


def _split_start(srcs, lands, name, part="whole"):
    ops = list(srcs or []) + list(lands)
    ns, n = len(srcs or []), len(lands)

    def body(*refs):
        src_refs, land_refs = refs[:ns], refs[ns:ns + n]
        send_sems, recv_sems = refs[ns + n], refs[ns + n + 1]
        for mine_out, _ in _chip_copies(src_refs, land_refs, send_sems, recv_sems, part):
            mine_out.start()
        refs[-1][...] = jnp.zeros_like(refs[-1])

    sems = pltpu.SemaphoreType.DMA((3 * n,))
    res = pl.pallas_call(
        body, name=name, out_shape=(sems, sems) + tuple(pltpu.HBM(a.shape, a.dtype) for a in ops) + (_sds((8, LANES)),),
        in_specs=[_HBM] * len(ops), out_specs=(_SEM, _SEM) + (_HBM,) * len(ops) + (pl.BlockSpec(memory_space=pltpu.VMEM),),
        input_output_aliases={k: 2 + k for k in range(len(ops))},
        compiler_params=pltpu.CompilerParams(has_side_effects=_EFFECT),
    )(*[pltpu.with_memory_space_constraint(a, pltpu.HBM) for a in ops])
    return res[0], res[1], list(res[2:2 + len(ops)]), res[-1]


def _split_wait(send_sems, recv_sems, thru, n, after, name, part="whole"):
    ns = len(thru) - n

    def body(*refs):
        src_refs, land_refs = refs[:ns], refs[ns:ns + n]
        for mine_out, arriving in _chip_copies(src_refs, land_refs, refs[ns + n], refs[ns + n + 1], part):
            mine_out.wait_send()
            arriving.wait_recv()

    res = pl.pallas_call(
        body, name=name, out_shape=tuple(pltpu.HBM(a.shape, a.dtype) for a in thru),
        in_specs=[_HBM] * len(thru) + [_SEM, _SEM, pl.BlockSpec(memory_space=pl.ANY)], out_specs=(_HBM,) * len(thru),
        input_output_aliases={k: k for k in range(len(thru))},
        compiler_params=pltpu.CompilerParams(has_side_effects=_EFFECT),
    )(*thru, send_sems, recv_sems, after)
    return list(res)


def _sibling_swap(arrs, name):
    n = len(arrs)

    def body(*refs):
        ins, outs_, (send_sems, recv_sems) = refs[:n], refs[n:2 * n], refs[2 * n:]
        sib = (lax.axis_index("x"), lax.axis_index("y"), 1 - lax.axis_index("c"))
        cps = [pltpu.make_async_remote_copy(src_ref=ins[i], dst_ref=outs_[i], send_sem=send_sems.at[i], recv_sem=recv_sems.at[i],
                                            device_id=sib, device_id_type=MESH) for i in range(n)]
        for cp in cps:
            cp.start()
        for cp in cps:
            cp.wait_recv()
        for cp in cps:
            cp.wait_send()

    hbm = pl.BlockSpec(memory_space=pltpu.HBM)
    return pl.pallas_call(
        body, name=name, out_shape=tuple(_sds(a.shape, a.dtype) for a in arrs), in_specs=[hbm] * n, out_specs=tuple([hbm] * n),
        scratch_shapes=[pltpu.SemaphoreType.DMA((n,)), pltpu.SemaphoreType.DMA((n,))],
    )(*arrs)


def _tile_spec(rows, cc):
    return pl.BlockSpec((None, rows, cc), lambda l, i: (l, i, 0))


def _cast_bf16(a, rows, name):
    nl, r, cc = a.shape

    def body(a_ref, o_ref):
        o_ref[...] = a_ref[...].astype(jnp.bfloat16)

    return pl.pallas_call(body, name=name, out_shape=_sds((nl, r, cc), jnp.bfloat16), grid=(nl, r // rows),
                          in_specs=[_tile_spec(rows, cc)], out_specs=_tile_spec(rows, cc), compiler_params=_cp())(a)


def _sum_blocks(a, rows, name):
    k, nl, r, cc = a.shape

    def body(a_ref, o_ref):
        acc = a_ref[0].astype(F32)
        for j in range(1, k):
            acc = acc + a_ref[j].astype(F32)
        o_ref[...] = acc

    return pl.pallas_call(body, name=name, out_shape=_sds((nl, r, cc)), grid=(nl, r // rows),
                          in_specs=[pl.BlockSpec((k, None, rows, cc), lambda l, i: (0, l, i, 0))],
                          out_specs=_tile_spec(rows, cc), compiler_params=_cp())(a)


def _sum_chips(lands, srcs, rows, name):
    nl = len(lands)
    _, r, cc = lands[0].shape

    def body(*refs):
        land_refs, src_refs, o_ref = refs[:nl], refs[nl:2 * nl], refs[2 * nl]
        mine = 2 * lax.axis_index("x") + lax.axis_index("y")
        for j in range(nl):
            @pl.when(pl.program_id(0) == j)
            def _(j=j):
                own = src_refs[j][mine].astype(F32)
                acc = None
                for k in range(4):
                    term = jnp.where(mine == k, own, land_refs[j][k].astype(F32))
                    acc = term if acc is None else acc + term
                o_ref[...] = acc

    specs = [pl.BlockSpec((4, rows, cc), lambda l, i, j=j: (0, jnp.where(l == j, i, 0), 0)) for j in range(nl)]
    return pl.pallas_call(body, name=name, out_shape=_sds((nl, r, cc)), grid=(nl, r // rows),
                          in_specs=specs + specs, out_specs=_tile_spec(rows, cc), compiler_params=_cp())(*lands, *srcs)


def _adamw(w, parts, m, v, rows, name, lead=None):
    nl, r, cc = w.shape
    np_ = len(parts)
    c1 = 1.0 / (1.0 - ADAM_B1 ** ADAM_STEP)
    c2 = 1.0 / (1.0 - ADAM_B2 ** ADAM_STEP)

    def body(*refs):
        w_ref, p_refs, (m_ref, v_ref, g_ref, d_ref, nm_ref, nv_ref) = refs[0], refs[1:1 + np_], refs[1 + np_:]
        g = p_refs[0][...]
        for p_ref in p_refs[1:]:
            g = g + p_ref[...]
        nm = ADAM_B1 * m_ref[...] + (1.0 - ADAM_B1) * g
        nv = ADAM_B2 * v_ref[...] + (1.0 - ADAM_B2) * (g * g)
        g_ref[...] = g
        nm_ref[...] = nm
        nv_ref[...] = nv
        d_ref[...] = -ADAM_LR * ((nm * c1) / (jnp.sqrt(nv * c2) + ADAM_EPS) + ADAM_WD * w_ref[...])

    if lead is None:
        spec, grid = _tile_spec(rows, cc), (nl, r // rows)
    else:
        spec, grid = pl.BlockSpec((lead, r, cc), lambda i: (i, 0, 0)), (nl // lead,)
    return pl.pallas_call(body, name=name, out_shape=(_sds((nl, r, cc)),) * 4, grid=grid,
                          in_specs=[spec] * (3 + np_), out_specs=(spec,) * 4, compiler_params=_cp())(w, *parts, m, v)


_BIAS = pltpu.VMEM((2, 2 * BLK, 2 * BLK), F32)


def _fill_band_bias(bias_ref):
    qi = lax.broadcasted_iota(jnp.int32, (2 * BLK, 2 * BLK), 0) & (BLK - 1)
    kj = lax.broadcasted_iota(jnp.int32, (2 * BLK, 2 * BLK), 1)
    dist = BLK + qi - kj
    band = (dist >= 0) & (dist <= BLK)
    bias_ref[0] = jnp.where(band, 0.0, NEG)
    bias_ref[1] = jnp.where(band & (kj >= BLK), 0.0, NEG)


class _HeadStack:
    def __init__(self, group):
        self.m0, self.m1 = _half_masks()
        self.group = group
        if group is not None:
            self.kv_mask = (self.m0, self.m1)[group]

    def _swap_half(self, t, a):
        return t if a == self.group else pltpu.roll(t, HD, axis=1)

    def stack(self, t):
        t0, t1 = t * self.m0, t * self.m1
        if self.group is not None:
            t0, t1 = self._swap_half(t0, 0), self._swap_half(t1, 1)
        return jnp.concatenate([t0, t1], axis=0)

    def unstack(self, ts):
        if self.group is None:
            return ts[:BLK] * self.m0 + ts[BLK:] * self.m1
        return self._swap_half(ts[:BLK] * self.kv_mask, 0) + self._swap_half(ts[BLK:] * self.kv_mask, 1)


def _rows(st, dil):
    if dil == 1:
        return pl.ds(pl.multiple_of(st, BLK), BLK)
    return pl.ds(st, BLK, stride=dil)


def _block_pos(n, dil):
    nb = SEQ // (dil * BLK)
    r, b = n // nb, n % nb
    hp = (b > 0).astype(jnp.int32)
    st = r + dil * BLK * b
    return st, st - dil * BLK * hp, 1 - hp


def _attn_fwd(proj, qblk, kblk, vblk, dils, gqa, sink_x, name):
    has_sink = sink_x is not None

    def body(*refs):
        if has_sink:
            q_ref, k_ref, v_ref, s_ref, o_ref, lse_ref, m_scr, z_scr, bias_scr = refs
        else:
            q_ref, k_ref, v_ref, o_ref, lse_ref, m_scr, z_scr, bias_scr = refs

        @pl.when(pl.program_id(0) == 0)
        def _():
            _fill_band_bias(bias_scr)
        o_ref[...] = jnp.zeros_like(o_ref)
        if has_sink:
            z_scr[...] = jnp.ones_like(z_scr)
            m_scr[...] = jnp.broadcast_to(s_ref[...], m_scr.shape)
        else:
            z_scr[...] = jnp.zeros_like(z_scr)
            m_scr[...] = jnp.full_like(m_scr, NEG)

        def step(n, carry, dil, heads):
            m0, m1 = heads.m0, heads.m1
            st, stp, first = _block_pos(n, dil)
            rq, rp = _rows(st, dil), _rows(stp, dil)
            kk = jnp.concatenate([k_ref[rp, :], k_ref[rq, :]], axis=0)
            vv = jnp.concatenate([v_ref[rp, :], v_ref[rq, :]], axis=0)
            s = _mm(heads.stack(q_ref[rq, :] * QK_SCALE), kk, NT) + bias_scr[first]
            m = jnp.max(s, axis=1, keepdims=True)
            p = jnp.exp(s - m)
            l = jnp.sum(p, axis=1, keepdims=True)
            o_pair = heads.unstack(_mm(p, vv))
            m_pair = m[:BLK] * m0 + m[BLK:] * m1
            l_pair = l[:BLK] * m0 + l[BLK:] * m1
            m_old = m_scr[rq, :]
            m_new = jnp.maximum(m_old, m_pair)
            alpha, beta = jnp.exp(m_old - m_new), jnp.exp(m_pair - m_new)
            o_ref[rq, :] = o_ref[rq, :] * alpha + o_pair * beta
            z_scr[rq, :] = z_scr[rq, :] * alpha + l_pair * beta
            m_scr[rq, :] = m_new
            return carry

        def blocks(heads):
            for dil in dils:
                lax.fori_loop(0, SEQ // BLK, lambda n, carry, dil=dil: step(n, carry, dil, heads), 0, unroll=8 if gqa else 16)

        if gqa:
            for grp in range(2):
                pl.when(pl.program_id(0) // 2 == grp)(lambda grp=grp: blocks(_HeadStack(grp)))
        else:
            blocks(_HeadStack(None))

        def fin(t, carry):
            rt = pl.ds(pl.multiple_of(t * TM, TM), TM)
            z = z_scr[rt, :]
            o_ref[rt, :] = o_ref[rt, :] / z
            lse_ref[rt, :] = m_scr[rt, :] + jnp.log(z)
            return carry
        lax.fori_loop(0, SEQ // TM, fin, 0)

    col = lambda blk: pl.BlockSpec((SEQ, LANES), lambda p, blk=blk: (0, blk + p))
    kv = (lambda blk: pl.BlockSpec((SEQ, LANES), lambda p, blk=blk: (0, blk))) if gqa else col
    in_specs = [col(qblk), kv(kblk), kv(vblk)]
    args = [proj, proj, proj]
    if has_sink:
        in_specs.append(pl.BlockSpec((1, LANES), lambda p: (0, p)))
        args.append(sink_x)
    out = pl.BlockSpec((SEQ, LANES), lambda p: (0, p))
    return pl.pallas_call(body, name=name, out_shape=(_sds((SEQ, 512)), _sds((SEQ, 512))), grid=(4,),
                          in_specs=in_specs, out_specs=(out, out),
                          scratch_shapes=[pltpu.VMEM((SEQ, LANES), F32), pltpu.VMEM((SEQ, LANES), F32), _BIAS],
                          compiler_params=_cp(48))(*args)


def _attn_bwd(proj, qblk, kblk, vblk, do, o, lse, dils, gqa, sink_x, name):
    has_sink = sink_x is not None

    def body(*refs):
        if has_sink:
            q_ref, k_ref, v_ref, do_ref, o_ref, lse_ref, s_ref, dq_ref, dk_ref, dv_ref, ds_ref, bias_scr = refs
        else:
            q_ref, k_ref, v_ref, do_ref, o_ref, lse_ref, dq_ref, dk_ref, dv_ref, bias_scr = refs
        pid = pl.program_id(0)

        @pl.when(pid == 0)
        def _():
            _fill_band_bias(bias_scr)
        dq_ref[...] = jnp.zeros_like(dq_ref)
        if gqa:
            @pl.when(pid == 0)
            def _():
                dk_ref[...] = jnp.zeros_like(dk_ref)
                dv_ref[...] = jnp.zeros_like(dv_ref)
        else:
            dk_ref[...] = jnp.zeros_like(dk_ref)
            dv_ref[...] = jnp.zeros_like(dv_ref)

        def step(n, carry, dil, heads):
            m0, m1 = heads.m0, heads.m1
            st, stp, first = _block_pos(n, dil)
            rq, rp = _rows(st, dil), _rows(stp, dil)
            do_, lse_ = do_ref[rq, :], lse_ref[rq, :]
            kk = jnp.concatenate([k_ref[rp, :], k_ref[rq, :]], axis=0)
            vv = jnp.concatenate([v_ref[rp, :], v_ref[rq, :]], axis=0)
            qs, dos = heads.stack(q_ref[rq, :] * QK_SCALE), heads.stack(do_)
            doo = do_ * o_ref[rq, :]
            delta = jnp.concatenate([jnp.sum(doo * m0, axis=1, keepdims=True), jnp.sum(doo * m1, axis=1, keepdims=True)], axis=0)
            lse_s = jnp.concatenate([lse_[:, 0:1], lse_[:, HD:HD + 1]], axis=0)
            p = jnp.exp(_mm(qs, kk, NT) + bias_scr[first] - lse_s)
            ds = p * (_mm(dos, vv, NT) - delta)
            dq_ref[rq, :] += heads.unstack(_mm(ds, kk)) * QK_SCALE
            dk_sum, dv_sum = _mm(ds, qs, TN), _mm(p, dos, TN)
            dk_ref[rp, :] += dk_sum[:BLK]
            dk_ref[rq, :] += dk_sum[BLK:]
            dv_ref[rp, :] += dv_sum[:BLK]
            dv_ref[rq, :] += dv_sum[BLK:]
            return carry

        def blocks(heads):
            for dil in dils:
                lax.fori_loop(0, SEQ // BLK, lambda n, carry, dil=dil: step(n, carry, dil, heads), 0, unroll=4)

        if gqa:
            for grp in range(2):
                pl.when(pid // 2 == grp)(lambda grp=grp: blocks(_HeadStack(grp)))
        else:
            blocks(_HeadStack(None))

        if has_sink:
            m0, m1 = _half_masks()

            def sink_rows(t, acc):
                rt = pl.ds(pl.multiple_of(t * TM, TM), TM)
                return acc - jnp.sum(jnp.exp(s_ref[...] - lse_ref[rt, :]) * (do_ref[rt, :] * o_ref[rt, :]), axis=0, keepdims=True)
            acc = lax.fori_loop(0, SEQ // TM, sink_rows, jnp.zeros((1, LANES), F32))
            per_head = jnp.sum(acc * m0, axis=1, keepdims=True) * m0 + jnp.sum(acc * m1, axis=1, keepdims=True) * m1
            ds_ref[0] = jnp.broadcast_to(per_head, (8, LANES))

    col = lambda blk: pl.BlockSpec((SEQ, LANES), lambda p, blk=blk: (0, blk + p))
    kv = (lambda blk: pl.BlockSpec((SEQ, LANES), lambda p, blk=blk: (0, blk))) if gqa else col
    pair = pl.BlockSpec((SEQ, LANES), lambda p: (0, p))
    in_specs = [col(qblk), kv(kblk), kv(vblk), pair, pair, pair]
    args = [proj, proj, proj, do, o, lse]
    kvw = LANES if gqa else 512
    kv_out = pl.BlockSpec((SEQ, LANES), lambda p: (0, 0)) if gqa else pair
    out_shape = [_sds((SEQ, 512)), _sds((SEQ, kvw)), _sds((SEQ, kvw))]
    out_specs = [pair, kv_out, kv_out]
    if has_sink:
        in_specs.append(pl.BlockSpec((1, LANES), lambda p: (0, p)))
        args.append(sink_x)
        out_shape.append(_sds((4, 8, LANES)))
        out_specs.append(pl.BlockSpec((1, 8, LANES), lambda p: (p, 0, 0)))
    return pl.pallas_call(body, name=name, out_shape=tuple(out_shape), grid=(4,), in_specs=in_specs,
                          out_specs=tuple(out_specs), scratch_shapes=[_BIAS], compiler_params=_cp(56))(*args)


_CT = 128


def _rows_before(x_ref, t, k):
    if t == 0:
        return jnp.concatenate([jnp.zeros((k, LANES), F32), x_ref[0:_CT - k, :]], axis=0)
    return x_ref[t * _CT - k:(t + 1) * _CT - k, :]


def _conv_pre(x_ref, w_ref, b_ref, t):
    taps = [x_ref[t * _CT:(t + 1) * _CT, :]] + [_rows_before(x_ref, t, k) for k in range(1, 4)]
    u = b_ref[...] + taps[0] * w_ref[3:4, :]
    for k in range(1, 4):
        u = u + taps[k] * w_ref[3 - k:4 - k, :]
    return u, taps


def _conv_fwd(proj, w, b, name):
    def body(x_ref, w_ref, b_ref, o_ref):
        for t in range(SEQ // _CT):
            o_ref[t * _CT:(t + 1) * _CT, :] = _silu(_conv_pre(x_ref, w_ref, b_ref, t)[0])

    nblk = CONV_CH // LANES
    return pl.pallas_call(body, name=name, out_shape=_sds((SEQ, CONV_CH)), grid=(nblk,),
                          in_specs=[pl.BlockSpec((SEQ, LANES), lambda j: (0, XBC // LANES + j)),
                                    pl.BlockSpec((4, LANES), lambda j: (0, j)), pl.BlockSpec((1, LANES), lambda j: (0, j))],
                          out_specs=pl.BlockSpec((SEQ, LANES), lambda j: (0, j)), compiler_params=_cp())(proj, w, b)


def _conv_bwd(proj, dact, w, b, name):
    def body(x_ref, da_ref, w_ref, b_ref, dx_ref, dw_ref, db_ref, du_scr):
        du_scr[SEQ:SEQ + 8, :] = jnp.zeros((8, LANES), F32)
        db = jnp.zeros((1, LANES), F32)
        dws = [jnp.zeros((1, LANES), F32)] * 4
        for t in range(SEQ // _CT):
            u, taps = _conv_pre(x_ref, w_ref, b_ref, t)
            du = da_ref[t * _CT:(t + 1) * _CT, :] * _dsilu(u)
            du_scr[t * _CT:(t + 1) * _CT, :] = du
            db = db + jnp.sum(du, axis=0, keepdims=True)
            dws = [dws[k] + jnp.sum(du * taps[k], axis=0, keepdims=True) for k in range(4)]
        db_ref[...] = db
        for k in range(4):
            dw_ref[3 - k:4 - k, :] = dws[k]
        for t in range(SEQ // _CT):
            dx = du_scr[t * _CT:(t + 1) * _CT, :] * w_ref[3:4, :]
            for k in range(1, 4):
                dx = dx + du_scr[t * _CT + k:(t + 1) * _CT + k, :] * w_ref[3 - k:4 - k, :]
            dx_ref[t * _CT:(t + 1) * _CT, :] = dx.astype(dx_ref.dtype)

    nblk = CONV_CH // LANES
    blk = pl.BlockSpec((SEQ, LANES), lambda j: (0, j))
    wspec, bspec = pl.BlockSpec((4, LANES), lambda j: (0, j)), pl.BlockSpec((1, LANES), lambda j: (0, j))
    return pl.pallas_call(body, name=name, out_shape=(_sds((SEQ, CONV_CH), MXU), _sds((4, CONV_CH)), _sds((1, CONV_CH))), grid=(nblk,),
                          in_specs=[pl.BlockSpec((SEQ, LANES), lambda j: (0, XBC // LANES + j)), blk, wspec, bspec],
                          out_specs=(blk, wspec, bspec), scratch_shapes=[pltpu.VMEM((SEQ + 8, LANES), F32)],
                          compiler_params=_cp())(proj, dact, w, b)


def _column(t, h):
    lane = lax.broadcasted_iota(jnp.int32, (1, LANES), 1)
    pick = jax.custom_vjp(lambda v: v[:, h:h + 1])
    pick.defvjp(lambda v: (v[:, h:h + 1], None), lambda _, g: (g * (lane == h).astype(F32),))
    return pick(t)


def _row(t, h):
    sub = lax.broadcasted_iota(jnp.int32, (BLK, 1), 0)
    pick = jax.custom_vjp(lambda v: v[h:h + 1, :])
    pick.defvjp(lambda v: (v[h:h + 1, :], None), lambda _, g: (g * (sub == h).astype(F32),))
    return pick(t)


def _ssd_chunk(xs, bm, cm, dtr, z, hs, al16, dtb, dskx, nw):
    m0, m1 = _half_masks()
    row = lax.broadcasted_iota(jnp.int32, (BLK, BLK), 0)
    col = lax.broadcasted_iota(jnp.int32, (BLK, BLK), 1)
    causal = row >= col
    tril = causal.astype(F32)
    lane = lax.broadcasted_iota(jnp.int32, (1, LANES), 1)
    dt = jnp.where(lane < 16, _softplus(dtr + dtb), 0.0)
    a16 = -jnp.exp(al16)
    acum = jnp.dot(tril, dt * a16, precision=HI, preferred_element_type=F32)
    acum_t = acum.T
    gmat = [_mm(cm[g], bm[g], NT) for g in range(2)]
    ys, hn = [], []
    for p in range(8):
        g = p // 4
        col_h = [_column(acum, 2 * p + a) for a in range(2)]
        dt_x = _column(dt, 2 * p) * m0 + _column(dt, 2 * p + 1) * m1
        ac_x = col_h[0] * m0 + col_h[1] * m1
        a_end = _row(ac_x, BLK - 1)
        xdt = xs[p] * dt_x
        y = _mm(cm[g], hs[p]) * jnp.exp(ac_x)
        for a, msk in enumerate((m0, m1)):
            decay = jnp.exp(jnp.where(causal, col_h[a] - _row(acum_t, 2 * p + a), NEG))
            y = y + _mm(gmat[g] * decay, xdt * msk)
        st = _mm(bm[g], xdt * jnp.exp(a_end - ac_x), TN)
        hn.append(hs[p] * jnp.exp(a_end) + st)
        y = y + dskx[p] * xs[p]
        ys.append(y * _silu(z[p]))
    out = []
    for g in range(2):
        ms = sum(jnp.sum(ys[p] * ys[p], axis=1, keepdims=True) for p in range(4 * g, 4 * g + 4)) * (1.0 / 512)
        rstd = lax.rsqrt(ms + EPS)
        out += [ys[p] * rstd * nw[p] for p in range(4 * g, 4 * g + 4)]
    return out, hn


def _tiles(ref, n, off=0, rows=slice(None)):
    return [ref[rows, off + LANES * p:off + LANES * (p + 1)] for p in range(n)]


def _ssd_load(xbc_ref, z_ref, dt_ref, rows):
    return (_tiles(xbc_ref, 8, 0, rows), _tiles(xbc_ref, 2, 1024, rows), _tiles(xbc_ref, 2, 1280, rows), dt_ref[rows, :],
            _tiles(z_ref, 8, 0, rows))


def _ssd_params(al16_ref, dtb_ref, dsk_ref, nw_ref):
    return al16_ref[...], dtb_ref[...], _tiles(dsk_ref, 8), _tiles(nw_ref, 8)


_NCH = SEQ // BLK
_PER_STEP = 2
_STEP_ROWS = _PER_STEP * BLK


def _ssd_param_specs():
    return [_full((1, LANES)), _full((1, LANES)), _full((1, 1024)), _full((1, 1024))]


def _ssd_fwd(xbc_act, proj, al16, dtb, dskx, nw, name):
    def body(xbc_ref, z_ref, dt_ref, al16_ref, dtb_ref, dsk_ref, nw_ref, y_ref, hin_ref, h_scr):
        @pl.when(pl.program_id(0) == 0)
        def _():
            h_scr[...] = jnp.zeros_like(h_scr)
        params = _ssd_params(al16_ref, dtb_ref, dsk_ref, nw_ref)
        hs = _tiles(h_scr, 8)
        for k in range(_PER_STEP):
            rows = slice(BLK * k, BLK * (k + 1))
            for p in range(8):
                hin_ref[k, :, LANES * p:LANES * (p + 1)] = hs[p]
            ys, hs = _ssd_chunk(*_ssd_load(xbc_ref, z_ref, dt_ref, rows), hs, *params)
            for p in range(8):
                y_ref[rows, LANES * p:LANES * (p + 1)] = ys[p].astype(y_ref.dtype)
        for p in range(8):
            h_scr[:, LANES * p:LANES * (p + 1)] = hs[p]

    return pl.pallas_call(
        body, name=name, out_shape=(_sds((SEQ, 1024), MXU), _sds((_NCH, BLK, 1024))), grid=(_NCH // _PER_STEP,),
        in_specs=[pl.BlockSpec((_STEP_ROWS, CONV_CH), lambda c: (c, 0)), pl.BlockSpec((_STEP_ROWS, 1024), lambda c: (c, ZB // 1024)),
                  pl.BlockSpec((_STEP_ROWS, LANES), lambda c: (c, DTC // LANES))] + _ssd_param_specs(),
        out_specs=(pl.BlockSpec((_STEP_ROWS, 1024), lambda c: (c, 0)), pl.BlockSpec((_PER_STEP, BLK, 1024), lambda c: (c, 0, 0))),
        scratch_shapes=[pltpu.VMEM((BLK, 1024), F32)], compiler_params=_cp())(xbc_act, proj, proj, al16, dtb, dskx, nw)


def _ssd_bwd(xbc_act, proj, hin, dyb, al16, dtb, dskx, nw, name):
    def body(xbc_ref, z_ref, dt_ref, hin_ref, dy_ref, al16_ref, dtb_ref, dsk_ref, nw_ref,
             dxbc_ref, dz_ref, ddt_ref, dal16_ref, ddtb_ref, ddsk_ref, dnw_ref, dh_scr):
        @pl.when(pl.program_id(0) == 0)
        def _():
            dh_scr[...] = jnp.zeros_like(dh_scr)
            for r in (dal16_ref, ddtb_ref, ddsk_ref, dnw_ref):
                r[...] = jnp.zeros_like(r)
        params = _ssd_params(al16_ref, dtb_ref, dsk_ref, nw_ref)
        dhs = _tiles(dh_scr, 8)
        for k in reversed(range(_PER_STEP)):
            rows = slice(BLK * k, BLK * (k + 1))
            hs = [hin_ref[k, :, LANES * p:LANES * (p + 1)] for p in range(8)]
            _, vjp = jax.vjp(lambda a, h, q: _ssd_chunk(*a, h, *q), _ssd_load(xbc_ref, z_ref, dt_ref, rows), hs, params)
            (dxs, dbm, dcm, ddt, dz), dhs, (dal16, ddtb, ddsk, dnw) = vjp((_tiles(dy_ref, 8, 0, rows), dhs))
            for p in range(8):
                cols = slice(LANES * p, LANES * (p + 1))
                dxbc_ref[rows, cols] = dxs[p]
                dz_ref[rows, cols] = dz[p].astype(dz_ref.dtype)
                ddsk_ref[:, cols] += ddsk[p]
                dnw_ref[:, cols] += dnw[p]
            for g in range(2):
                dxbc_ref[rows, 1024 + LANES * g:1024 + LANES * (g + 1)] = dbm[g]
                dxbc_ref[rows, 1280 + LANES * g:1280 + LANES * (g + 1)] = dcm[g]
            ddt_ref[rows, :] = ddt.astype(ddt_ref.dtype)
            dal16_ref[...] += dal16
            ddtb_ref[...] += ddtb
        for p in range(8):
            dh_scr[:, LANES * p:LANES * (p + 1)] = dhs[p]

    rev = lambda c: _NCH // _PER_STEP - 1 - c
    return pl.pallas_call(
        body, name=name,
        out_shape=(_sds((SEQ, CONV_CH)), _sds((SEQ, 1024), MXU), _sds((SEQ, LANES), MXU),
                   _sds((1, LANES)), _sds((1, LANES)), _sds((1, 1024)), _sds((1, 1024))),
        grid=(_NCH // _PER_STEP,),
        in_specs=[pl.BlockSpec((_STEP_ROWS, CONV_CH), lambda c: (rev(c), 0)), pl.BlockSpec((_STEP_ROWS, 1024), lambda c: (rev(c), ZB // 1024)),
                  pl.BlockSpec((_STEP_ROWS, LANES), lambda c: (rev(c), DTC // LANES)),
                  pl.BlockSpec((_PER_STEP, BLK, 1024), lambda c: (rev(c), 0, 0)),
                  pl.BlockSpec((_STEP_ROWS, 1024), lambda c: (rev(c), 0))] + _ssd_param_specs(),
        out_specs=(pl.BlockSpec((_STEP_ROWS, CONV_CH), lambda c: (rev(c), 0)), pl.BlockSpec((_STEP_ROWS, 1024), lambda c: (rev(c), 0)),
                   pl.BlockSpec((_STEP_ROWS, LANES), lambda c: (rev(c), 0)),
                   _full((1, LANES)), _full((1, LANES)), _full((1, 1024)), _full((1, 1024))),
        scratch_shapes=[pltpu.VMEM((BLK, 1024), F32)], compiler_params=_cp())(xbc_act, proj, proj, hin, dyb, al16, dtb, dskx, nw)


def _rstd(v):
    return lax.rsqrt(jnp.mean(v * v, axis=1, keepdims=True) + EPS)


def _rms_bwd(dn, n, rstd):
    return rstd * (dn - n * jnp.mean(dn * n, axis=1, keepdims=True))


_VEC = _full((1, D))


def _layer_spec(layer):
    return pl.BlockSpec((None, 2048, D), lambda *_: (layer, 0, 0))

_ROW = pl.BlockSpec((TM, D), lambda i, *_: (i, 0))


def _proj_fwd(x, pre_w, scale, shift, w, layer, name):
    tn, ni = 1024, SEQ // TM

    def body(x_ref, pw_ref, sc_ref, sh_ref, w_ref, o_ref, h_ref, h_scr):
        rows = pl.ds(pl.multiple_of(pl.program_id(1) * TM, TM), TM)

        @pl.when(pl.program_id(0) == 0)
        def _():
            xv = x_ref[...]
            h = ((xv * _rstd(xv) * pw_ref[...]) * (1.0 + sc_ref[...]) + sh_ref[...]).astype(h_ref.dtype)
            h_scr[rows, :] = h
            h_ref[...] = h
        o_ref[...] = jnp.dot(h_scr[rows, :], w_ref[...].astype(MXU), preferred_element_type=F32)

    first_pass = pl.BlockSpec((TM, D), lambda j, i: (jnp.where(j == 0, i, ni - 1), 0))
    return pl.pallas_call(body, name=name, out_shape=(_sds((SEQ, NP)), _sds((SEQ, D), MXU)), grid=(NP // tn, ni),
                          in_specs=[first_pass, _VEC, _VEC, _VEC, pl.BlockSpec((None, D, tn), lambda j, i: (layer, 0, j))],
                          out_specs=(pl.BlockSpec((TM, tn), lambda j, i: (i, j)), first_pass),
                          scratch_shapes=[pltpu.VMEM((SEQ, D), MXU)], compiler_params=_cp())(x, pre_w, scale, shift, w)


_HALF = pl.BlockSpec((TM, 512), lambda i: (i, 0))
_Z_A = pl.BlockSpec((TM, 512), lambda i: (i, ZA // 512))
_Z_C = pl.BlockSpec((TM, 512), lambda i: (i, ZC // 512))


def _out_fwd(o_a, yb, o_c, proj, w, layer, x, gate, post_w, name):
    def body(oa_ref, yb_ref, oc_ref, za_ref, zc_ref, w_ref, x_ref, g_ref, pw_ref, xn_ref, y_ref):
        y = (_mm(oa_ref[...] * _silu(za_ref[...]), w_ref[0:512, :]) + _mm(yb_ref[...], w_ref[512:1536, :])
             + _mm(oc_ref[...] * _silu(zc_ref[...]), w_ref[1536:2048, :]))
        y_ref[...] = y
        xn_ref[...] = x_ref[...] + g_ref[...] * (y * _rstd(y) * pw_ref[...])

    return pl.pallas_call(body, name=name, out_shape=(_sds((SEQ, D)), _sds((SEQ, D))), grid=(SEQ // TM,),
                          in_specs=[_HALF, _ROW, _HALF, _Z_A, _Z_C, _layer_spec(layer), _ROW, _VEC, _VEC],
                          out_specs=(_ROW, _ROW), compiler_params=_cp())(o_a, yb, o_c, proj, proj, w, x, gate, post_w)


def _dymix(dxo, y, gate, post_w, w, layer, o_a, o_c, proj, name):
    def body(dx_ref, y_ref, g_ref, pw_ref, w_ref, oa_ref, oc_ref, za_ref, zc_ref,
             dy_ref, dg_ref, dpw_ref, doa_ref, dza_ref, b_ref, doc_ref, dzc_ref):
        @pl.when(pl.program_id(0) == 0)
        def _():
            dg_ref[...] = jnp.zeros_like(dg_ref)
            dpw_ref[...] = jnp.zeros_like(dpw_ref)
        dx, yv = dx_ref[...], y_ref[...]
        rstd = _rstd(yv)
        n = yv * rstd
        dg_ref[...] += jnp.sum(dx * (n * pw_ref[...]), axis=0, keepdims=True)
        dr = dx * g_ref[...]
        dpw_ref[...] += jnp.sum(dr * n, axis=0, keepdims=True)
        dy = _rms_bwd(dr * pw_ref[...], n, rstd)
        dy_ref[...] = dy
        b_ref[...] = _mm(dy, w_ref[512:1536, :], NT)
        for rows, o_ref, z_ref, do_ref, dz_ref in ((slice(0, 512), oa_ref, za_ref, doa_ref, dza_ref),
                                                   (slice(1536, 2048), oc_ref, zc_ref, doc_ref, dzc_ref)):
            dyg, z = _mm(dy, w_ref[rows, :], NT), z_ref[...]
            do_ref[...] = dyg * _silu(z)
            dz_ref[...] = (dyg * o_ref[...] * _dsilu(z)).astype(dz_ref.dtype)

    return pl.pallas_call(body, name=name,
                          out_shape=(_sds((SEQ, D)), _sds((1, D)), _sds((1, D)),
                                     _sds((SEQ, 512)), _sds((SEQ, 512), MXU), _sds((SEQ, D)), _sds((SEQ, 512)), _sds((SEQ, 512), MXU)),
                          grid=(SEQ // TM,), in_specs=[_ROW, _ROW, _VEC, _VEC, _layer_spec(layer), _HALF, _HALF, _Z_A, _Z_C],
                          out_specs=(_ROW, _VEC, _VEC, _HALF, _HALF, _ROW, _HALF, _HALF),
                          compiler_params=_cp())(dxo, y, gate, post_w, w, o_a, o_c, proj, proj)


def _dwout(o_a, yb, o_c, proj, dy, name):
    def body(oa_ref, yb_ref, oc_ref, za_ref, zc_ref, dy_ref, o_ref):
        @pl.when(pl.program_id(0) == 0)
        def _():
            o_ref[...] = jnp.zeros_like(o_ref)
        dy = dy_ref[...]
        o_ref[0:512, :] += _mm(oa_ref[...] * _silu(za_ref[...]), dy, TN)
        o_ref[512:1536, :] += _mm(yb_ref[...], dy, TN)
        o_ref[1536:2048, :] += _mm(oc_ref[...] * _silu(zc_ref[...]), dy, TN)

    return pl.pallas_call(body, name=name, out_shape=_sds((2048, D)), grid=(SEQ // TM,),
                          in_specs=[_HALF, _ROW, _HALF, _Z_A, _Z_C, _ROW], out_specs=_full((2048, D)),
                          compiler_params=_cp())(o_a, yb, o_c, proj, proj, dy)


def _dwin(h, pieces, name):
    n = len(pieces)
    widths = [p.shape[1] for p in pieces]
    half = NP // 2

    def body(*refs):
        h_ref, p_refs, o_ref = refs[0], refs[1:1 + n], refs[1 + n]

        @pl.when(pl.program_id(0) == 0)
        def _():
            o_ref[...] = jnp.zeros_like(o_ref)
        hv, c0 = h_ref[...], 0
        for p_ref, wd in zip(p_refs, widths):
            o_ref[:, c0:c0 + wd] += _mm(hv, p_ref[...], TN)
            c0 += wd

    return pl.pallas_call(body, name=name, out_shape=_sds((D, half)), grid=(SEQ // TM,),
                          in_specs=[_ROW] + [pl.BlockSpec((TM, wd), lambda k: (k, 0)) for wd in widths],
                          out_specs=_full((D, half)), compiler_params=_cp(56))(h, *pieces)


_TMH = 256


def _dh_bwd(pieces, w, x, pre_w, scale, dxo, name):
    n = len(pieces)
    widths = [p.shape[1] for p in pieces]

    def body(*refs):
        p_refs, (w_ref, x_ref, pw_ref, sc_ref, dxo_ref, dx_ref, dsh_ref, dsc_ref, dpw_ref) = refs[:n], refs[n:]

        @pl.when(pl.program_id(0) == 0)
        def _():
            for r in (dsh_ref, dsc_ref, dpw_ref):
                r[...] = jnp.zeros_like(r)
        dh, c0 = 0.0, 0
        for p_ref, wd in zip(p_refs, widths):
            dh = dh + _mm(p_ref[...], w_ref[:, c0:c0 + wd], NT)
            c0 += wd
        xv = x_ref[...]
        rstd = _rstd(xv)
        nrm = xv * rstd
        dsh_ref[...] += jnp.sum(dh, axis=0, keepdims=True)
        dsc_ref[...] += jnp.sum(dh * (nrm * pw_ref[...]), axis=0, keepdims=True)
        dhn = dh * (1.0 + sc_ref[...])
        dpw_ref[...] += jnp.sum(dhn * nrm, axis=0, keepdims=True)
        dx_ref[...] = _rms_bwd(dhn * pw_ref[...], nrm, rstd) + dxo_ref[...]

    row = pl.BlockSpec((_TMH, D), lambda i: (i, 0))
    return pl.pallas_call(body, name=name, out_shape=(_sds((SEQ, D)), _sds((1, D)), _sds((1, D)), _sds((1, D))),
                          grid=(SEQ // _TMH,),
                          in_specs=[pl.BlockSpec((_TMH, wd), lambda i: (i, 0)) for wd in widths]
                          + [pl.BlockSpec((None, D, NP), lambda i: (0, 0, 0)), row, _VEC, _VEC, row],
                          out_specs=(row, _VEC, _VEC, _VEC), compiler_params=_cp(56))(*pieces, w, x, pre_w, scale, dxo)


def _w_in_padded(land, name):
    rows = 128

    def body(l_ref, o_ref):
        o_ref[...] = _pad_cols(jnp.concatenate([l_ref[k] for k in range(4)], axis=1))

    return pl.pallas_call(body, name=name, out_shape=_sds((D, NP), land.dtype), grid=(D // rows,),
                          in_specs=[pl.BlockSpec((4, rows, SHARD_IN), lambda i: (0, i, 0))],
                          out_specs=pl.BlockSpec((rows, NP), lambda i: (i, 0)), compiler_params=_cp())(land)


def _grad_blocks(dwa, dwb, name):
    rows = 128

    def body(a_ref, b_ref, o_ref):
        g = _unpad_cols(jnp.concatenate([a_ref[...], b_ref[...]], axis=1))
        for k in range(4):
            o_ref[k] = g[:, SHARD_IN * k:SHARD_IN * (k + 1)].astype(o_ref.dtype)

    half = pl.BlockSpec((rows, NP // 2), lambda i: (i, 0))
    return pl.pallas_call(body, name=name, out_shape=_sds((4, D, SHARD_IN), jnp.bfloat16), grid=(D // rows,),
                          in_specs=[half, half], out_specs=pl.BlockSpec((4, rows, SHARD_IN), lambda i: (0, i, 0)),
                          compiler_params=_cp())(dwa, dwb)


def _loss_bwd(xf, tgt, name):
    def body(x_ref, t_ref, dx_ref, l_ref):
        @pl.when(pl.program_id(0) == 0)
        def _():
            l_ref[...] = jnp.zeros_like(l_ref)
        e = x_ref[...] - t_ref[...]
        dx_ref[...] = e * (1.0 / D)
        l_ref[...] += 0.5 * jnp.sum(jnp.mean(e * e, axis=1, keepdims=True), axis=0, keepdims=True)

    return pl.pallas_call(body, name=name, out_shape=(_sds((SEQ, D)), _sds((8, LANES))), grid=(SEQ // TM,),
                          in_specs=[_ROW, _ROW], out_specs=(_ROW, _full((8, LANES))), compiler_params=_cp())(xf, tgt)


def _mod_part(c_all, ada_w, ada_b, name):
    def body(c_ref, w_ref, b_ref, o_ref):
        o_ref[0] = _mm(_silu(c_ref[...]), w_ref[0]) + b_ref[0]

    return pl.pallas_call(body, name=name, out_shape=_sds((DEPTH, 8, 768)), grid=(DEPTH,),
                          in_specs=[_full((8, D)), pl.BlockSpec((1, D, 768), lambda i: (i, 0, 0)), pl.BlockSpec((1, 1, 768), lambda i: (i, 0, 0))],
                          out_specs=pl.BlockSpec((1, 8, 768), lambda i: (i, 0, 0)), compiler_params=_cp())(c_all, ada_w, ada_b)


def _ada_grad(c_t, dmod, name):
    def body(c_ref, d_ref, o_ref):
        ca = _silu(c_ref[...])
        dm = d_ref[0]
        acc = ca[:, 0:1] * dm[0:1, :]
        for s in range(1, 8):
            acc = acc + ca[:, s:s + 1] * dm[s:s + 1, :]
        o_ref[0] = acc

    return pl.pallas_call(body, name=name, out_shape=_sds((DEPTH, D, 768)), grid=(DEPTH,),
                          in_specs=[_full((D, LANES)), pl.BlockSpec((1, 8, 768), lambda i: (i, 0, 0))],
                          out_specs=pl.BlockSpec((1, D, 768), lambda i: (i, 0, 0)), compiler_params=_cp())(c_t, dmod)


def _pack(parts):
    flat = []
    for p in parts:
        f = p.reshape(-1)
        flat.append(jnp.pad(f, (0, (-f.size) % LANES)))
    v = jnp.concatenate(flat)
    return jnp.pad(v, (0, (-v.size) % (8 * LANES))).reshape(-1, LANES)


def _unpack(v, shapes):
    v = v.reshape(-1)
    out, off = [], 0
    for s in shapes:
        n = math.prod(s)
        out.append(v[off:off + n].reshape(s))
        off += n + (-n) % LANES
    return out


_GIVEN_DT, _GIVEN_C = 4608, 4624


def _pad_cols(w):
    return jnp.concatenate([w[..., :_GIVEN_DT], w[..., _GIVEN_C:], w[..., _GIVEN_DT:_GIVEN_C],
                            jnp.zeros(w.shape[:-1] + (NP - IN_COLS,), w.dtype)], axis=-1)


def _unpad_cols(w):
    return jnp.concatenate([w[..., :_GIVEN_DT], w[..., DTC:DTC + 16], w[..., _GIVEN_DT:DTC]], axis=-1)


def _pad_lanes(v):
    return jnp.pad(v, (0, LANES - v.shape[0])).reshape(1, LANES)


def _local_step(x2, tgt, mod, weights_of, grads_done, pre_w, post_w, conv_w, conv_b, dt_bias, a_log, d_skip, nw, sinks):
    saved = []
    xcur = x2
    for i in range(DEPTH):
        shift, scale, gate = mod[i:i + 1, :D], mod[i:i + 1, D:2 * D], mod[i:i + 1, 2 * D:]
        pw, qw = pre_w[i:i + 1], post_w[i:i + 1]
        w_p, w_o = weights_of(i, xcur)
        proj, h = _proj_fwd(xcur, pw, scale, shift, w_p, 0, "proj_fwd")
        o_a, lse_a = _attn_fwd(proj, QA // LANES, KA // LANES, VA // LANES, DILS, False, None, "attn_a_fwd")
        sink_x = jnp.repeat(sinks[i], HD).reshape(1, 512)
        o_c, lse_c = _attn_fwd(proj, QC // LANES, KC // LANES, VC // LANES, (1,), True, sink_x, "attn_c_fwd")
        cw, cb = conv_w[i], conv_b[i:i + 1]
        xbc_act = _conv_fwd(proj, cw, cb, "conv_fwd")
        ssd_p = (_pad_lanes(a_log[i]), _pad_lanes(dt_bias[i]), jnp.repeat(d_skip[i], HD).reshape(1, 1024), nw[i:i + 1])
        yb, hin = _ssd_fwd(xbc_act, proj, *ssd_p, "ssd_fwd")
        xnew, y = _out_fwd(o_a, yb, o_c, proj, w_o, 0, xcur, gate, qw, "out_fwd")
        saved.append((w_p, w_o, xcur, scale, gate, pw, qw, proj, h, o_a, lse_a, sink_x, o_c, lse_c, cw, cb, xbc_act, ssd_p, yb, hin, y))
        xcur = xnew
    dx, ltile = _loss_bwd(xcur, tgt, "loss")
    dmod, small = [None] * DEPTH, [None] * DEPTH
    for i in reversed(range(DEPTH)):
        w_p, w_o, xin, scale, gate, pw, qw, proj, h, o_a, lse_a, sink_x, o_c, lse_c, cw, cb, xbc_act, ssd_p, yb, hin, y = saved[i]
        dy, dgate, dpost, do_a, dz_a, dyb, do_c, dz_c = _dymix(dx, y, gate, qw, w_o, 0, o_a, o_c, proj, "dymix")
        dwo = _dwout(o_a, yb, o_c, proj, dy, "dwout")
        dq_a, dk_a, dv_a = _attn_bwd(proj, QA // LANES, KA // LANES, VA // LANES, do_a, o_a, lse_a, DILS, False, None, "attn_a_bwd")
        dq_c, dk_c, dv_c, dsk = _attn_bwd(proj, QC // LANES, KC // LANES, VC // LANES, do_c, o_c, lse_c, (1,), True, sink_x, "attn_c_bwd")
        dxbc_act, dz_b, ddt, dal16, ddtb, ddsk, dnw = _ssd_bwd(xbc_act, proj, hin, dyb, *ssd_p, "ssd_bwd")
        dxbc, dcw, dcb = _conv_bwd(proj, dxbc_act, cw, cb, "conv_bwd")
        half_a, half_b = [dq_a, dk_a, dv_a, dz_a, dz_b], [dxbc, dq_c, dz_c, dk_c, dv_c, ddt]
        sent = grads_done(i, _dwin(h, half_a, "dwin_a"), _dwin(h, half_b, "dwin_b"), dwo)
        dx, dshift, dscale, dpre = _dh_bwd(half_a + half_b, w_p, xin, pw, scale + sent[0, 0], dx, "dh_bwd")
        dmod[i] = jnp.concatenate([dshift, dscale, dgate], axis=1)
        small[i] = (dpre, dpost, dcw, dcb, ddtb[0, :16], dal16[0, :16], ddsk.reshape(16, HD).sum(axis=1), dnw, dsk[:, 0, ::HD].reshape(8))
    return ltile, dx, jnp.concatenate(dmod, axis=0), small


_SMALL = ((1, D), (1, D), (4, CONV_CH), (1, CONV_CH), (16,), (16,), (16,), (1, D), (8,))


def kernel(x, c, ada_w, ada_b, pre_norm_w, post_norm_w, w_in, conv_w, conv_b, dt_bias, a_log, d_skip, ssm_norm_w, sinks, w_out, loss_target, m_ada_w, m_ada_b, m_pre_norm_w, m_post_norm_w, m_w_in, m_conv_w, m_conv_b, m_dt_bias, m_a_log, m_d_skip, m_ssm_norm_w, m_sinks, m_w_out, v_ada_w, v_ada_b, v_pre_norm_w, v_post_norm_w, v_w_in, v_conv_w, v_conv_b, v_dt_bias, v_a_log, v_d_skip, v_ssm_norm_w, v_sinks, v_w_out):
    xi, yi, ci = lax.axis_index("x"), lax.axis_index("y"), lax.axis_index("c")
    chip = 2 * xi + yi
    me = 2 * chip + ci

    w_in_b = _cast_bf16(w_in, 512, "cast_w_in")
    w_out_b = _cast_bf16(w_out, 512, "cast_w_out")
    gathers = []
    for i in range(DEPTH):
        lands = [lax.dynamic_update_slice(lax.empty((4,) + a.shape[1:], a.dtype), a[i][None], (chip, 0, 0)) for a in (w_in_b, w_out_b)]
        gathers.append(_split_start(None, lands, f"gather_start{i}", "half" if i == 0 else "whole"))
    all_started = gathers[0][3] + gathers[1][3] + gathers[2][3] + gathers[3][3]

    def weights_of(i, after):
        send_sems, recv_sems, thru, _ = gathers[i]
        if i == 0:
            halves = _split_wait(send_sems, recv_sems, thru, 2, all_started + mod[:1, :LANES], "gather_wait0", "half")
            send_sems, recv_sems, thru, after = _split_start(None, halves, "share_start0", "sibling")
            g_in, g_out = _split_wait(send_sems, recv_sems, thru, 2, after, "share_wait0", "sibling")
        else:
            g_in, g_out = _split_wait(send_sems, recv_sems, thru, 2, after, f"gather_wait{i}")
        return _w_in_padded(g_in, "w_in_padded")[None], g_out.reshape(1, 2048, D)

    scatters = [None] * DEPTH

    def grads_done(i, dwa, dwb, dwo):
        blocks = [_grad_blocks(dwa, dwb, "grad_blocks"), _cast_bf16(dwo.reshape(4, 512, D), 512, "cast_dw_out")]
        scatters[i] = _split_start(blocks, [lax.empty(b.shape, b.dtype) for b in blocks], f"scatter_start{i}")
        return scatters[i][3]

    g0 = _allgather8(_pack([c, conv_w]), "gather_c")
    c_all = g0[:, :8, :].reshape(8, D)
    conv_w_full = jnp.concatenate([g0[2 * k, 8:56, :].reshape(DEPTH, 4, CONV_CH // 4) for k in range(4)], axis=-1)

    ada_b_mine = lax.dynamic_slice_in_dim(ada_b, 768 * chip, 768, axis=1).reshape(DEPTH, 1, 768)
    gm = _allgather8(_mod_part(c_all, ada_w, ada_b_mine, "mod_part").reshape(DEPTH * 8, 768), "gather_mod")
    gm = gm.reshape(4, 2, DEPTH, 8, 768)[:, 0]
    mod = lax.dynamic_index_in_dim(gm, me, axis=2, keepdims=False).transpose(1, 0, 2).reshape(DEPTH, 3 * D)

    ltile, dx, dmod, small = _local_step(x[0], loss_target[0], mod, weights_of, grads_done, pre_norm_w, post_norm_w, conv_w_full,
                                         conv_b, dt_bias, a_log, d_skip, ssm_norm_w, sinks)

    packed = _pack([dmod] + [g for layer in small for g in layer] + [ltile[0]])
    gs = _allgather8(packed, "gather_small")
    tot = _sum_blocks(gs[:, None], packed.shape[0], "sum_small")[0]
    parts = _unpack(tot, [(DEPTH, 3 * D)] + list(_SMALL) * DEPTH + [(LANES,)])
    g_ada_b, loss = parts[0], parts[-1][0]
    per_layer = [parts[1 + len(_SMALL) * i:1 + len(_SMALL) * (i + 1)] for i in range(DEPTH)]
    g_pre, g_post, g_cw, g_cb, g_dtb, g_al, g_dsk, g_nw, g_sk = [jnp.stack([per_layer[i][j] for i in range(DEPTH)]) for j in range(len(_SMALL))]
    g_pre, g_post, g_cb, g_nw = g_pre[:, 0], g_post[:, 0], g_cb[:, 0], g_nw[:, 0]
    g_cw = lax.dynamic_slice_in_dim(g_cw, (CONV_CH // 4) * chip, CONV_CH // 4, axis=2)

    dmod_all = gs[:, :(DEPTH * 3 * D) // LANES, :].reshape(8, DEPTH, 3 * D).transpose(1, 0, 2)
    dmod_mine = lax.dynamic_slice_in_dim(dmod_all, 768 * chip, 768, axis=2)
    c_t = jnp.pad(c_all.T, ((0, 0), (0, LANES - 8)))
    g_ada_w = _ada_grad(c_t, dmod_mine, "ada_grad")

    res = {}
    res["ada_w"] = _adamw(ada_w, [g_ada_w], m_ada_w, v_ada_w, 512, "adamw_ada_w")
    names = ["ada_b", "pre_norm_w", "post_norm_w", "conv_w", "conv_b", "dt_bias", "a_log", "d_skip", "ssm_norm_w", "sinks"]
    ws = [ada_b, pre_norm_w, post_norm_w, conv_w, conv_b, dt_bias, a_log, d_skip, ssm_norm_w, sinks]
    gsm = [g_ada_b, g_pre, g_post, g_cw, g_cb, g_dtb, g_al, g_dsk, g_nw, g_sk]
    ms = [m_ada_b, m_pre_norm_w, m_post_norm_w, m_conv_w, m_conv_b, m_dt_bias, m_a_log, m_d_skip, m_ssm_norm_w, m_sinks]
    vs = [v_ada_b, v_pre_norm_w, v_post_norm_w, v_conv_w, v_conv_b, v_dt_bias, v_a_log, v_d_skip, v_ssm_norm_w, v_sinks]
    pw_, pg_, pm_, pv_ = _pack(ws), _pack(gsm), _pack(ms), _pack(vs)
    small_out = _adamw(pw_[None], [pg_[None]], pm_[None], pv_[None], pw_.shape[0], "adamw_small")

    others_done = small_out[1][0, :8] + res["ada_w"][1][0, :8, :LANES]
    landed = [_split_wait(*scatters[i][:3], 2, others_done, f"scatter_wait{i}") for i in range(DEPTH)]
    p_in = _sum_chips([d[2] for d in landed], [d[0] for d in landed], 128, "sum_w_in")
    p_out = _sum_chips([d[3] for d in landed], [d[1] for d in landed], 256, "sum_w_out")
    col_major, row_major = (lambda a: jnp.transpose(a, (2, 0, 1))), (lambda a: jnp.transpose(a, (1, 2, 0)))
    p_in = col_major(p_in)
    s_in, s_out = _sibling_swap([p_in, p_out], "swap_partials")
    res["w_in"] = [row_major(a) for a in _adamw(col_major(w_in), [p_in, s_in], col_major(m_w_in), col_major(v_w_in), None,
                                                "adamw_w_in", lead=SHARD_IN // 18)]
    res["w_out"] = _adamw(w_out, [p_out, s_out], m_w_out, v_w_out, 512, "adamw_w_out")
    shapes = [w.shape for w in ws]
    for kind in range(4):
        for nm, a in zip(names, _unpack(small_out[kind][0], shapes)):
            res.setdefault(nm, [None] * 4)[kind] = a
    order = ["ada_w", "ada_b", "pre_norm_w", "post_norm_w", "w_in", "conv_w", "conv_b", "dt_bias", "a_log", "d_skip", "ssm_norm_w", "sinks", "w_out"]
    return (loss, dx[None], *[res[n][0] for n in order], *[res[n][1] for n in order], *[res[n][2] for n in order], *[res[n][3] for n in order])
```

```python
import math

import jax
import jax.numpy as jnp
from jax import lax
from jax.experimental import pallas as pl
from jax.experimental.pallas import tpu as pltpu

F32 = jnp.float32
MXU = jnp.bfloat16
HI = lax.Precision.HIGHEST
MESH = pl.DeviceIdType.MESH

SEQ = 4096
D = 1024
DEPTH = 4
HD = 64
QK_SCALE = HD ** -0.5
LANES = 128
BLK = 128
DILS = (1, 4, 16)
NEG = -1e30
EPS = 1e-6
MIB = 1024 * 1024

NP = 6144
QA, KA, VA, ZA = 0, 512, 1024, 1536
ZB, XBC = 2048, 3072
QC, ZC, KC, VC = 4608, 5120, 5632, 5760
DTC = 5888
IN_COLS = 5904
SHARD_IN = IN_COLS // 4
CONV_CH = 1536
TM = 512

ADAM_LR, ADAM_B1, ADAM_B2, ADAM_EPS, ADAM_WD, ADAM_STEP = 0.001, 0.9, 0.999, 1e-08, 0.01, 10

NT = (((1,), (1,)), ((), ()))
TN = (((0,), (0,)), ((), ()))


def _cp(vmem_mib=48):
    return pltpu.CompilerParams(vmem_limit_bytes=vmem_mib * MIB)


def _sds(shape, dtype=F32):
    return jax.ShapeDtypeStruct(shape, dtype)


def _full(shape):
    n = len(shape)
    return pl.BlockSpec(shape, lambda *_: (0,) * n)


def _mm(a, b, dims=None):
    if dims is None:
        return jnp.dot(a.astype(MXU), b.astype(MXU), preferred_element_type=F32)
    return lax.dot_general(a.astype(MXU), b.astype(MXU), dims, preferred_element_type=F32)


def _sigmoid(x):
    return 1.0 / (1.0 + jnp.exp(-x))


def _silu(x):
    return x * _sigmoid(x)


def _dsilu(x):
    s = _sigmoid(x)
    return s * (1.0 + x * (1.0 - s))


def _softplus(x):
    ax = jnp.where(x >= 0, x, -x)
    return jnp.maximum(x, 0.0) + jnp.log1p(jnp.exp(-ax))


def _half_masks():
    lane = lax.broadcasted_iota(jnp.int32, (1, LANES), 1)
    m0 = (lane < HD).astype(F32)
    return m0, 1.0 - m0


def _allgather8(v, name):
    r, cc = v.shape

    def body(v_ref, out_ref, send_sems, recv_sems):
        x, y, c = lax.axis_index("x"), lax.axis_index("y"), lax.axis_index("c")
        me = 4 * x + 2 * y + c
        out_ref[me] = v_ref[...]
        peers = []
        for k in range(1, 8):
            px = 1 - x if k & 4 else x
            py = 1 - y if k & 2 else y
            pc = 1 - c if k & 1 else c
            peers.append((px, py, pc))
        sends = []
        for k, peer in enumerate(peers):
            cp = pltpu.make_async_remote_copy(src_ref=v_ref, dst_ref=out_ref.at[me], send_sem=send_sems.at[k],
                                              recv_sem=recv_sems.at[k], device_id=peer, device_id_type=MESH)
            cp.start()
            sends.append(cp)
        for k, (px, py, pc) in enumerate(peers):
            pltpu.make_async_remote_copy(src_ref=v_ref, dst_ref=out_ref.at[4 * px + 2 * py + pc], send_sem=send_sems.at[k],
                                         recv_sem=recv_sems.at[k], device_id=(px, py, pc), device_id_type=MESH).wait_recv()
        for cp in sends:
            cp.wait_send()

    return pl.pallas_call(
        body, name=name, out_shape=_sds((8, r, cc)),
        in_specs=[pl.BlockSpec(memory_space=pltpu.VMEM)], out_specs=pl.BlockSpec(memory_space=pltpu.VMEM),
        scratch_shapes=[pltpu.SemaphoreType.DMA((7,)), pltpu.SemaphoreType.DMA((7,))],
        compiler_params=_cp(32),
    )(v)


_HBM = pl.BlockSpec(memory_space=pltpu.HBM)
_SEM = pl.BlockSpec(memory_space=pltpu.SEMAPHORE)
_EFFECT = pltpu.SideEffectType.DATAFLOW_SIDE_EFFECTING


def _chip_copies(src_refs, land_refs, send_sems, recv_sems, part="whole"):
    x, y, c = lax.axis_index("x"), lax.axis_index("y"), lax.axis_index("c")
    mine = 2 * x + y
    out = []
    for i, land in enumerate(land_refs):
        half = land.shape[1] // 2
        own, others = pl.ds(pl.multiple_of(c * half, half), half), pl.ds(pl.multiple_of((1 - c) * half, half), half)
        for j, (px, py) in enumerate([(1 - x, y), (x, 1 - y), (1 - x, 1 - y)]):
            slot, peer = 2 * px + py, (px, py, c)
            if part == "whole":
                src = src_refs[i].at[slot] if src_refs else land.at[mine]
                there, here = land.at[mine], land.at[slot]
            elif part == "half":
                src = there = land.at[mine].at[own]
                here = land.at[slot].at[own]
            else:
                src = there = land.at[slot].at[own]
                here, peer = land.at[slot].at[others], (x, y, 1 - c)
            mk = lambda dst, i=i, j=j, src=src, peer=peer: pltpu.make_async_remote_copy(
                src_ref=src, dst_ref=dst, send_sem=send_sems.at[3 * i + j], recv_sem=recv_sems.at[3 * i + j],
                device_id=peer, device_id_type=MESH)
            out.append((mk(there), mk(here)))
    return out


def _split_start(srcs, lands, name, part="whole"):
    ops = list(srcs or []) + list(lands)
    ns, n = len(srcs or []), len(lands)

    def body(*refs):
        src_refs, land_refs = refs[:ns], refs[ns:ns + n]
        send_sems, recv_sems = refs[ns + n], refs[ns + n + 1]
        for mine_out, _ in _chip_copies(src_refs, land_refs, send_sems, recv_sems, part):
            mine_out.start()
        refs[-1][...] = jnp.zeros_like(refs[-1])

    sems = pltpu.SemaphoreType.DMA((3 * n,))
    res = pl.pallas_call(
        body, name=name, out_shape=(sems, sems) + tuple(pltpu.HBM(a.shape, a.dtype) for a in ops) + (_sds((8, LANES)),),
        in_specs=[_HBM] * len(ops), out_specs=(_SEM, _SEM) + (_HBM,) * len(ops) + (pl.BlockSpec(memory_space=pltpu.VMEM),),
        input_output_aliases={k: 2 + k for k in range(len(ops))},
        compiler_params=pltpu.CompilerParams(has_side_effects=_EFFECT),
    )(*[pltpu.with_memory_space_constraint(a, pltpu.HBM) for a in ops])
    return res[0], res[1], list(res[2:2 + len(ops)]), res[-1]


def _split_wait(send_sems, recv_sems, thru, n, after, name, part="whole"):
    ns = len(thru) - n

    def body(*refs):
        src_refs, land_refs = refs[:ns], refs[ns:ns + n]
        for mine_out, arriving in _chip_copies(src_refs, land_refs, refs[ns + n], refs[ns + n + 1], part):
            mine_out.wait_send()
            arriving.wait_recv()

    res = pl.pallas_call(
        body, name=name, out_shape=tuple(pltpu.HBM(a.shape, a.dtype) for a in thru),
        in_specs=[_HBM] * len(thru) + [_SEM, _SEM, pl.BlockSpec(memory_space=pl.ANY)], out_specs=(_HBM,) * len(thru),
        input_output_aliases={k: k for k in range(len(thru))},
        compiler_params=pltpu.CompilerParams(has_side_effects=_EFFECT),
    )(*thru, send_sems, recv_sems, after)
    return list(res)


def _sibling_swap(arrs, name):
    n = len(arrs)

    def body(*refs):
        ins, outs_, (send_sems, recv_sems) = refs[:n], refs[n:2 * n], refs[2 * n:]
        sib = (lax.axis_index("x"), lax.axis_index("y"), 1 - lax.axis_index("c"))
        cps = [pltpu.make_async_remote_copy(src_ref=ins[i], dst_ref=outs_[i], send_sem=send_sems.at[i], recv_sem=recv_sems.at[i],
                                            device_id=sib, device_id_type=MESH) for i in range(n)]
        for cp in cps:
            cp.start()
        for cp in cps:
            cp.wait_recv()
        for cp in cps:
            cp.wait_send()

    hbm = pl.BlockSpec(memory_space=pltpu.HBM)
    return pl.pallas_call(
        body, name=name, out_shape=tuple(_sds(a.shape, a.dtype) for a in arrs), in_specs=[hbm] * n, out_specs=tuple([hbm] * n),
        scratch_shapes=[pltpu.SemaphoreType.DMA((n,)), pltpu.SemaphoreType.DMA((n,))],
    )(*arrs)


def _tile_spec(rows, cc):
    return pl.BlockSpec((None, rows, cc), lambda l, i: (l, i, 0))


def _cast_bf16(a, rows, name):
    nl, r, cc = a.shape

    def body(a_ref, o_ref):
        o_ref[...] = a_ref[...].astype(jnp.bfloat16)

    return pl.pallas_call(body, name=name, out_shape=_sds((nl, r, cc), jnp.bfloat16), grid=(nl, r // rows),
                          in_specs=[_tile_spec(rows, cc)], out_specs=_tile_spec(rows, cc), compiler_params=_cp())(a)


def _sum_blocks(a, rows, name):
    k, nl, r, cc = a.shape

    def body(a_ref, o_ref):
        acc = a_ref[0].astype(F32)
        for j in range(1, k):
            acc = acc + a_ref[j].astype(F32)
        o_ref[...] = acc

    return pl.pallas_call(body, name=name, out_shape=_sds((nl, r, cc)), grid=(nl, r // rows),
                          in_specs=[pl.BlockSpec((k, None, rows, cc), lambda l, i: (0, l, i, 0))],
                          out_specs=_tile_spec(rows, cc), compiler_params=_cp())(a)


def _sum_chips(lands, srcs, rows, name):
    nl = len(lands)
    _, r, cc = lands[0].shape

    def body(*refs):
        land_refs, src_refs, o_ref = refs[:nl], refs[nl:2 * nl], refs[2 * nl]
        mine = 2 * lax.axis_index("x") + lax.axis_index("y")
        for j in range(nl):
            @pl.when(pl.program_id(0) == j)
            def _(j=j):
                own = src_refs[j][mine].astype(F32)
                acc = None
                for k in range(4):
                    term = jnp.where(mine == k, own, land_refs[j][k].astype(F32))
                    acc = term if acc is None else acc + term
                o_ref[...] = acc

    specs = [pl.BlockSpec((4, rows, cc), lambda l, i, j=j: (0, jnp.where(l == j, i, 0), 0)) for j in range(nl)]
    return pl.pallas_call(body, name=name, out_shape=_sds((nl, r, cc)), grid=(nl, r // rows),
                          in_specs=specs + specs, out_specs=_tile_spec(rows, cc), compiler_params=_cp())(*lands, *srcs)


def _adamw(w, parts, m, v, rows, name, lead=None):
    nl, r, cc = w.shape
    np_ = len(parts)
    c1 = 1.0 / (1.0 - ADAM_B1 ** ADAM_STEP)
    c2 = 1.0 / (1.0 - ADAM_B2 ** ADAM_STEP)

    def body(*refs):
        w_ref, p_refs, (m_ref, v_ref, g_ref, d_ref, nm_ref, nv_ref) = refs[0], refs[1:1 + np_], refs[1 + np_:]
        g = p_refs[0][...]
        for p_ref in p_refs[1:]:
            g = g + p_ref[...]
        nm = ADAM_B1 * m_ref[...] + (1.0 - ADAM_B1) * g
        nv = ADAM_B2 * v_ref[...] + (1.0 - ADAM_B2) * (g * g)
        g_ref[...] = g
        nm_ref[...] = nm
        nv_ref[...] = nv
        d_ref[...] = -ADAM_LR * ((nm * c1) / (jnp.sqrt(nv * c2) + ADAM_EPS) + ADAM_WD * w_ref[...])

    if lead is None:
        spec, grid = _tile_spec(rows, cc), (nl, r // rows)
    else:
        spec, grid = pl.BlockSpec((lead, r, cc), lambda i: (i, 0, 0)), (nl // lead,)
    return pl.pallas_call(body, name=name, out_shape=(_sds((nl, r, cc)),) * 4, grid=grid,
                          in_specs=[spec] * (3 + np_), out_specs=(spec,) * 4, compiler_params=_cp())(w, *parts, m, v)


_BIAS = pltpu.VMEM((2, 2 * BLK, 2 * BLK), F32)


def _fill_band_bias(bias_ref):
    qi = lax.broadcasted_iota(jnp.int32, (2 * BLK, 2 * BLK), 0) & (BLK - 1)
    kj = lax.broadcasted_iota(jnp.int32, (2 * BLK, 2 * BLK), 1)
    dist = BLK + qi - kj
    band = (dist >= 0) & (dist <= BLK)
    bias_ref[0] = jnp.where(band, 0.0, NEG)
    bias_ref[1] = jnp.where(band & (kj >= BLK), 0.0, NEG)


class _HeadStack:
    def __init__(self, group):
        self.m0, self.m1 = _half_masks()
        self.group = group
        if group is not None:
            self.kv_mask = (self.m0, self.m1)[group]

    def _swap_half(self, t, a):
        return t if a == self.group else pltpu.roll(t, HD, axis=1)

    def stack(self, t):
        low = lax.broadcasted_iota(jnp.int32, (1, LANES), 1) < HD
        t0, t1 = jnp.where(low, t, 0.0), jnp.where(low, 0.0, t)
        if self.group is not None:
            t0, t1 = self._swap_half(t0, 0), self._swap_half(t1, 1)
        return jnp.concatenate([t0, t1], axis=0)

    def unstack(self, ts):
        if self.group is None:
            return jnp.where(lax.broadcasted_iota(jnp.int32, (1, LANES), 1) < HD, ts[:BLK], ts[BLK:])
        return self._swap_half(ts[:BLK] * self.kv_mask, 0) + self._swap_half(ts[BLK:] * self.kv_mask, 1)


def _rows(st, dil):
    if dil == 1:
        return pl.ds(pl.multiple_of(st, BLK), BLK)
    return pl.ds(st, BLK, stride=dil)


def _block_pos(n, dil):
    nb = SEQ // (dil * BLK)
    r, b = n // nb, n % nb
    hp = (b > 0).astype(jnp.int32)
    st = r + dil * BLK * b
    return st, st - dil * BLK * hp, 1 - hp


def _attn_fwd(proj, qblk, kblk, vblk, dils, gqa, sink_x, name):
    has_sink = sink_x is not None

    def body(*refs):
        if has_sink:
            q_ref, k_ref, v_ref, s_ref, o_ref, lse_ref, m_scr, z_scr, bias_scr = refs
        else:
            q_ref, k_ref, v_ref, o_ref, lse_ref, m_scr, z_scr, bias_scr = refs

        @pl.when(pl.program_id(0) == 0)
        def _():
            _fill_band_bias(bias_scr)
        o_ref[...] = jnp.zeros_like(o_ref)
        if has_sink:
            z_scr[...] = jnp.ones_like(z_scr)
            m_scr[...] = jnp.broadcast_to(s_ref[...], m_scr.shape)
        else:
            z_scr[...] = jnp.zeros_like(z_scr)
            m_scr[...] = jnp.full_like(m_scr, NEG)

        def step(n, carry, dil, heads):
            m0, m1 = heads.m0, heads.m1
            st, stp, first = _block_pos(n, dil)
            rq, rp = _rows(st, dil), _rows(stp, dil)
            kk = jnp.concatenate([k_ref[rp, :], k_ref[rq, :]], axis=0)
            vv = jnp.concatenate([v_ref[rp, :], v_ref[rq, :]], axis=0)
            s = _mm(heads.stack(q_ref[rq, :] * QK_SCALE), kk, NT) + bias_scr[first]
            m = jnp.max(s, axis=1, keepdims=True)
            p = jnp.exp(s - m)
            l = jnp.sum(p, axis=1, keepdims=True)
            o_pair = heads.unstack(_mm(p, vv))
            m_pair = m[:BLK] * m0 + m[BLK:] * m1
            l_pair = l[:BLK] * m0 + l[BLK:] * m1
            m_old = m_scr[rq, :]
            m_new = jnp.maximum(m_old, m_pair)
            alpha, beta = jnp.exp(m_old - m_new), jnp.exp(m_pair - m_new)
            o_ref[rq, :] = o_ref[rq, :] * alpha + o_pair * beta
            z_scr[rq, :] = z_scr[rq, :] * alpha + l_pair * beta
            m_scr[rq, :] = m_new
            return carry

        def blocks(heads):
            for dil in dils:
                lax.fori_loop(0, SEQ // BLK, lambda n, carry, dil=dil: step(n, carry, dil, heads), 0, unroll=8 if gqa else 16)

        if gqa:
            for grp in range(2):
                pl.when(pl.program_id(0) // 2 == grp)(lambda grp=grp: blocks(_HeadStack(grp)))
        else:
            blocks(_HeadStack(None))

        def fin(t, carry):
            rt = pl.ds(pl.multiple_of(t * TM, TM), TM)
            z = z_scr[rt, :]
            o_ref[rt, :] = o_ref[rt, :] / z
            lse_ref[rt, :] = m_scr[rt, :] + jnp.log(z)
            return carry
        lax.fori_loop(0, SEQ // TM, fin, 0)

    col = lambda blk: pl.BlockSpec((SEQ, LANES), lambda p, blk=blk: (0, blk + p))
    kv = (lambda blk: pl.BlockSpec((SEQ, LANES), lambda p, blk=blk: (0, blk))) if gqa else col
    in_specs = [col(qblk), kv(kblk), kv(vblk)]
    args = [proj, proj, proj]
    if has_sink:
        in_specs.append(pl.BlockSpec((1, LANES), lambda p: (0, p)))
        args.append(sink_x)
    out = pl.BlockSpec((SEQ, LANES), lambda p: (0, p))
    return pl.pallas_call(body, name=name, out_shape=(_sds((SEQ, 512)), _sds((SEQ, 512))), grid=(4,),
                          in_specs=in_specs, out_specs=(out, out),
                          scratch_shapes=[pltpu.VMEM((SEQ, LANES), F32), pltpu.VMEM((SEQ, LANES), F32), _BIAS],
                          compiler_params=_cp(48))(*args)


def _attn_bwd(proj, qblk, kblk, vblk, do, o, lse, dils, gqa, sink_x, name):
    has_sink = sink_x is not None

    def body(*refs):
        if has_sink:
            q_ref, k_ref, v_ref, do_ref, o_ref, lse_ref, s_ref, dq_ref, dk_ref, dv_ref, ds_ref, bias_scr = refs
        else:
            q_ref, k_ref, v_ref, do_ref, o_ref, lse_ref, dq_ref, dk_ref, dv_ref, bias_scr = refs
        pid = pl.program_id(0)

        @pl.when(pid == 0)
        def _():
            _fill_band_bias(bias_scr)
        dq_ref[...] = jnp.zeros_like(dq_ref)
        if gqa:
            @pl.when(pid == 0)
            def _():
                dk_ref[...] = jnp.zeros_like(dk_ref)
                dv_ref[...] = jnp.zeros_like(dv_ref)
        else:
            dk_ref[...] = jnp.zeros_like(dk_ref)
            dv_ref[...] = jnp.zeros_like(dv_ref)

        def step(n, carry, dil, heads):
            m0, m1 = heads.m0, heads.m1
            st, stp, first = _block_pos(n, dil)
            rq, rp = _rows(st, dil), _rows(stp, dil)
            do_, lse_ = do_ref[rq, :], lse_ref[rq, :]
            kk = jnp.concatenate([k_ref[rp, :], k_ref[rq, :]], axis=0)
            vv = jnp.concatenate([v_ref[rp, :], v_ref[rq, :]], axis=0)
            qs, dos = heads.stack(q_ref[rq, :] * QK_SCALE), heads.stack(do_)
            doo = do_ * o_ref[rq, :]
            delta = jnp.concatenate([jnp.sum(doo * m0, axis=1, keepdims=True), jnp.sum(doo * m1, axis=1, keepdims=True)], axis=0)
            lse_s = jnp.concatenate([lse_[:, 0:1], lse_[:, HD:HD + 1]], axis=0)
            p = jnp.exp(_mm(qs, kk, NT) + bias_scr[first] - lse_s)
            ds = p * (_mm(dos, vv, NT) - delta)
            dq_ref[rq, :] += heads.unstack(_mm(ds, kk)) * QK_SCALE
            dk_sum, dv_sum = _mm(ds, qs, TN), _mm(p, dos, TN)
            dk_ref[rp, :] += dk_sum[:BLK]
            dk_ref[rq, :] += dk_sum[BLK:]
            dv_ref[rp, :] += dv_sum[:BLK]
            dv_ref[rq, :] += dv_sum[BLK:]
            return carry

        def blocks(heads):
            for dil in dils:
                lax.fori_loop(0, SEQ // BLK, lambda n, carry, dil=dil: step(n, carry, dil, heads), 0, unroll=4)

        if gqa:
            for grp in range(2):
                pl.when(pid // 2 == grp)(lambda grp=grp: blocks(_HeadStack(grp)))
        else:
            blocks(_HeadStack(None))

        if has_sink:
            m0, m1 = _half_masks()

            def sink_rows(t, acc):
                rt = pl.ds(pl.multiple_of(t * TM, TM), TM)
                return acc - jnp.sum(jnp.exp(s_ref[...] - lse_ref[rt, :]) * (do_ref[rt, :] * o_ref[rt, :]), axis=0, keepdims=True)
            acc = lax.fori_loop(0, SEQ // TM, sink_rows, jnp.zeros((1, LANES), F32))
            per_head = jnp.sum(acc * m0, axis=1, keepdims=True) * m0 + jnp.sum(acc * m1, axis=1, keepdims=True) * m1
            ds_ref[0] = jnp.broadcast_to(per_head, (8, LANES))

    col = lambda blk: pl.BlockSpec((SEQ, LANES), lambda p, blk=blk: (0, blk + p))
    kv = (lambda blk: pl.BlockSpec((SEQ, LANES), lambda p, blk=blk: (0, blk))) if gqa else col
    pair = pl.BlockSpec((SEQ, LANES), lambda p: (0, p))
    in_specs = [col(qblk), kv(kblk), kv(vblk), pair, pair, pair]
    args = [proj, proj, proj, do, o, lse]
    kvw = LANES if gqa else 512
    kv_out = pl.BlockSpec((SEQ, LANES), lambda p: (0, 0)) if gqa else pair
    out_shape = [_sds((SEQ, 512)), _sds((SEQ, kvw)), _sds((SEQ, kvw))]
    out_specs = [pair, kv_out, kv_out]
    if has_sink:
        in_specs.append(pl.BlockSpec((1, LANES), lambda p: (0, p)))
        args.append(sink_x)
        out_shape.append(_sds((4, 8, LANES)))
        out_specs.append(pl.BlockSpec((1, 8, LANES), lambda p: (p, 0, 0)))
    return pl.pallas_call(body, name=name, out_shape=tuple(out_shape), grid=(4,), in_specs=in_specs,
                          out_specs=tuple(out_specs), scratch_shapes=[_BIAS], compiler_params=_cp(56))(*args)


_CT = 128


def _rows_before(x_ref, t, k):
    if t == 0:
        return jnp.concatenate([jnp.zeros((k, LANES), F32), x_ref[0:_CT - k, :]], axis=0)
    return x_ref[t * _CT - k:(t + 1) * _CT - k, :]


def _conv_pre(x_ref, w_ref, b_ref, t):
    taps = [x_ref[t * _CT:(t + 1) * _CT, :]] + [_rows_before(x_ref, t, k) for k in range(1, 4)]
    u = b_ref[...] + taps[0] * w_ref[3:4, :]
    for k in range(1, 4):
        u = u + taps[k] * w_ref[3 - k:4 - k, :]
    return u, taps


def _conv_fwd(proj, w, b, name):
    def body(x_ref, w_ref, b_ref, o_ref):
        for t in range(SEQ // _CT):
            o_ref[t * _CT:(t + 1) * _CT, :] = _silu(_conv_pre(x_ref, w_ref, b_ref, t)[0])

    nblk = CONV_CH // LANES
    return pl.pallas_call(body, name=name, out_shape=_sds((SEQ, CONV_CH)), grid=(nblk,),
                          in_specs=[pl.BlockSpec((SEQ, LANES), lambda j: (0, XBC // LANES + j)),
                                    pl.BlockSpec((4, LANES), lambda j: (0, j)), pl.BlockSpec((1, LANES), lambda j: (0, j))],
                          out_specs=pl.BlockSpec((SEQ, LANES), lambda j: (0, j)), compiler_params=_cp())(proj, w, b)


def _conv_bwd(proj, dact, w, b, name):
    def body(x_ref, da_ref, w_ref, b_ref, dx_ref, dw_ref, db_ref, du_scr):
        du_scr[SEQ:SEQ + 8, :] = jnp.zeros((8, LANES), F32)
        db = jnp.zeros((1, LANES), F32)
        dws = [jnp.zeros((1, LANES), F32)] * 4
        for t in range(SEQ // _CT):
            u, taps = _conv_pre(x_ref, w_ref, b_ref, t)
            du = da_ref[t * _CT:(t + 1) * _CT, :] * _dsilu(u)
            du_scr[t * _CT:(t + 1) * _CT, :] = du
            db = db + jnp.sum(du, axis=0, keepdims=True)
            dws = [dws[k] + jnp.sum(du * taps[k], axis=0, keepdims=True) for k in range(4)]
        db_ref[...] = db
        for k in range(4):
            dw_ref[3 - k:4 - k, :] = dws[k]
        for t in range(SEQ // _CT):
            dx = du_scr[t * _CT:(t + 1) * _CT, :] * w_ref[3:4, :]
            for k in range(1, 4):
                dx = dx + du_scr[t * _CT + k:(t + 1) * _CT + k, :] * w_ref[3 - k:4 - k, :]
            dx_ref[t * _CT:(t + 1) * _CT, :] = dx.astype(dx_ref.dtype)

    nblk = CONV_CH // LANES
    blk = pl.BlockSpec((SEQ, LANES), lambda j: (0, j))
    wspec, bspec = pl.BlockSpec((4, LANES), lambda j: (0, j)), pl.BlockSpec((1, LANES), lambda j: (0, j))
    return pl.pallas_call(body, name=name, out_shape=(_sds((SEQ, CONV_CH), MXU), _sds((4, CONV_CH)), _sds((1, CONV_CH))), grid=(nblk,),
                          in_specs=[pl.BlockSpec((SEQ, LANES), lambda j: (0, XBC // LANES + j)), blk, wspec, bspec],
                          out_specs=(blk, wspec, bspec), scratch_shapes=[pltpu.VMEM((SEQ + 8, LANES), F32)],
                          compiler_params=_cp())(proj, dact, w, b)


def _column(t, h):
    lane = lax.broadcasted_iota(jnp.int32, (1, LANES), 1)
    pick = jax.custom_vjp(lambda v: v[:, h:h + 1])
    pick.defvjp(lambda v: (v[:, h:h + 1], None), lambda _, g: (g * (lane == h).astype(F32),))
    return pick(t)


def _row(t, h):
    sub = lax.broadcasted_iota(jnp.int32, (BLK, 1), 0)
    pick = jax.custom_vjp(lambda v: v[h:h + 1, :])
    pick.defvjp(lambda v: (v[h:h + 1, :], None), lambda _, g: (g * (sub == h).astype(F32),))
    return pick(t)


def _ssd_chunk(xs, bm, cm, dtr, z, hs, al16, dtb, dskx, nw):
    m0, m1 = _half_masks()
    row = lax.broadcasted_iota(jnp.int32, (BLK, BLK), 0)
    col = lax.broadcasted_iota(jnp.int32, (BLK, BLK), 1)
    causal = row >= col
    tril = causal.astype(F32)
    lane = lax.broadcasted_iota(jnp.int32, (1, LANES), 1)
    dt = jnp.where(lane < 16, _softplus(dtr + dtb), 0.0)
    a16 = -jnp.exp(al16)
    acum = jnp.dot(tril, dt * a16, precision=HI, preferred_element_type=F32)
    acum_t = acum.T
    gmat = [_mm(cm[g], bm[g], NT) for g in range(2)]
    ys, hn = [], []
    for p in range(8):
        g = p // 4
        col_h = [_column(acum, 2 * p + a) for a in range(2)]
        dt_x = _column(dt, 2 * p) * m0 + _column(dt, 2 * p + 1) * m1
        ac_x = col_h[0] * m0 + col_h[1] * m1
        a_end = _row(ac_x, BLK - 1)
        xdt = xs[p] * dt_x
        y = _mm(cm[g], hs[p]) * jnp.exp(ac_x)
        for a, msk in enumerate((m0, m1)):
            decay = jnp.exp(jnp.where(causal, col_h[a] - _row(acum_t, 2 * p + a), NEG))
            y = y + _mm(gmat[g] * decay, xdt * msk)
        st = _mm(bm[g], xdt * jnp.exp(a_end - ac_x), TN)
        hn.append(hs[p] * jnp.exp(a_end) + st)
        y = y + dskx[p] * xs[p]
        ys.append(y * _silu(z[p]))
    out = []
    for g in range(2):
        ms = sum(jnp.sum(ys[p] * ys[p], axis=1, keepdims=True) for p in range(4 * g, 4 * g + 4)) * (1.0 / 512)
        rstd = lax.rsqrt(ms + EPS)
        out += [ys[p] * rstd * nw[p] for p in range(4 * g, 4 * g + 4)]
    return out, hn


def _tiles(ref, n, off=0, rows=slice(None)):
    return [ref[rows, off + LANES * p:off + LANES * (p + 1)] for p in range(n)]


def _ssd_load(xbc_ref, z_ref, dt_ref, rows):
    return (_tiles(xbc_ref, 8, 0, rows), _tiles(xbc_ref, 2, 1024, rows), _tiles(xbc_ref, 2, 1280, rows), dt_ref[rows, :],
            _tiles(z_ref, 8, 0, rows))


def _ssd_params(al16_ref, dtb_ref, dsk_ref, nw_ref):
    return al16_ref[...], dtb_ref[...], _tiles(dsk_ref, 8), _tiles(nw_ref, 8)


_NCH = SEQ // BLK
_PER_STEP = 2
_STEP_ROWS = _PER_STEP * BLK


def _ssd_param_specs():
    return [_full((1, LANES)), _full((1, LANES)), _full((1, 1024)), _full((1, 1024))]


def _ssd_fwd(xbc_act, proj, al16, dtb, dskx, nw, name):
    def body(xbc_ref, z_ref, dt_ref, al16_ref, dtb_ref, dsk_ref, nw_ref, y_ref, hin_ref, h_scr):
        @pl.when(pl.program_id(0) == 0)
        def _():
            h_scr[...] = jnp.zeros_like(h_scr)
        params = _ssd_params(al16_ref, dtb_ref, dsk_ref, nw_ref)
        hs = _tiles(h_scr, 8)
        for k in range(_PER_STEP):
            rows = slice(BLK * k, BLK * (k + 1))
            for p in range(8):
                hin_ref[k, :, LANES * p:LANES * (p + 1)] = hs[p]
            ys, hs = _ssd_chunk(*_ssd_load(xbc_ref, z_ref, dt_ref, rows), hs, *params)
            for p in range(8):
                y_ref[rows, LANES * p:LANES * (p + 1)] = ys[p].astype(y_ref.dtype)
        for p in range(8):
            h_scr[:, LANES * p:LANES * (p + 1)] = hs[p]

    return pl.pallas_call(
        body, name=name, out_shape=(_sds((SEQ, 1024), MXU), _sds((_NCH, BLK, 1024))), grid=(_NCH // _PER_STEP,),
        in_specs=[pl.BlockSpec((_STEP_ROWS, CONV_CH), lambda c: (c, 0)), pl.BlockSpec((_STEP_ROWS, 1024), lambda c: (c, ZB // 1024)),
                  pl.BlockSpec((_STEP_ROWS, LANES), lambda c: (c, DTC // LANES))] + _ssd_param_specs(),
        out_specs=(pl.BlockSpec((_STEP_ROWS, 1024), lambda c: (c, 0)), pl.BlockSpec((_PER_STEP, BLK, 1024), lambda c: (c, 0, 0))),
        scratch_shapes=[pltpu.VMEM((BLK, 1024), F32)], compiler_params=_cp())(xbc_act, proj, proj, al16, dtb, dskx, nw)


def _ssd_bwd(xbc_act, proj, hin, dyb, al16, dtb, dskx, nw, name):
    def body(xbc_ref, z_ref, dt_ref, hin_ref, dy_ref, al16_ref, dtb_ref, dsk_ref, nw_ref,
             dxbc_ref, dz_ref, ddt_ref, dal16_ref, ddtb_ref, ddsk_ref, dnw_ref, dh_scr):
        @pl.when(pl.program_id(0) == 0)
        def _():
            dh_scr[...] = jnp.zeros_like(dh_scr)
            for r in (dal16_ref, ddtb_ref, ddsk_ref, dnw_ref):
                r[...] = jnp.zeros_like(r)
        params = _ssd_params(al16_ref, dtb_ref, dsk_ref, nw_ref)
        dhs = _tiles(dh_scr, 8)
        for k in reversed(range(_PER_STEP)):
            rows = slice(BLK * k, BLK * (k + 1))
            hs = [hin_ref[k, :, LANES * p:LANES * (p + 1)] for p in range(8)]
            _, vjp = jax.vjp(lambda a, h, q: _ssd_chunk(*a, h, *q), _ssd_load(xbc_ref, z_ref, dt_ref, rows), hs, params)
            (dxs, dbm, dcm, ddt, dz), dhs, (dal16, ddtb, ddsk, dnw) = vjp((_tiles(dy_ref, 8, 0, rows), dhs))
            for p in range(8):
                cols = slice(LANES * p, LANES * (p + 1))
                dxbc_ref[rows, cols] = dxs[p]
                dz_ref[rows, cols] = dz[p].astype(dz_ref.dtype)
                ddsk_ref[:, cols] += ddsk[p]
                dnw_ref[:, cols] += dnw[p]
            for g in range(2):
                dxbc_ref[rows, 1024 + LANES * g:1024 + LANES * (g + 1)] = dbm[g]
                dxbc_ref[rows, 1280 + LANES * g:1280 + LANES * (g + 1)] = dcm[g]
            ddt_ref[rows, :] = ddt.astype(ddt_ref.dtype)
            dal16_ref[...] += dal16
            ddtb_ref[...] += ddtb
        for p in range(8):
            dh_scr[:, LANES * p:LANES * (p + 1)] = dhs[p]

    rev = lambda c: _NCH // _PER_STEP - 1 - c
    return pl.pallas_call(
        body, name=name,
        out_shape=(_sds((SEQ, CONV_CH)), _sds((SEQ, 1024), MXU), _sds((SEQ, LANES), MXU),
                   _sds((1, LANES)), _sds((1, LANES)), _sds((1, 1024)), _sds((1, 1024))),
        grid=(_NCH // _PER_STEP,),
        in_specs=[pl.BlockSpec((_STEP_ROWS, CONV_CH), lambda c: (rev(c), 0)), pl.BlockSpec((_STEP_ROWS, 1024), lambda c: (rev(c), ZB // 1024)),
                  pl.BlockSpec((_STEP_ROWS, LANES), lambda c: (rev(c), DTC // LANES)),
                  pl.BlockSpec((_PER_STEP, BLK, 1024), lambda c: (rev(c), 0, 0)),
                  pl.BlockSpec((_STEP_ROWS, 1024), lambda c: (rev(c), 0))] + _ssd_param_specs(),
        out_specs=(pl.BlockSpec((_STEP_ROWS, CONV_CH), lambda c: (rev(c), 0)), pl.BlockSpec((_STEP_ROWS, 1024), lambda c: (rev(c), 0)),
                   pl.BlockSpec((_STEP_ROWS, LANES), lambda c: (rev(c), 0)),
                   _full((1, LANES)), _full((1, LANES)), _full((1, 1024)), _full((1, 1024))),
        scratch_shapes=[pltpu.VMEM((BLK, 1024), F32)], compiler_params=_cp())(xbc_act, proj, proj, hin, dyb, al16, dtb, dskx, nw)


def _rstd(v):
    return lax.rsqrt(jnp.mean(v * v, axis=1, keepdims=True) + EPS)


def _rms_bwd(dn, n, rstd):
    return rstd * (dn - n * jnp.mean(dn * n, axis=1, keepdims=True))


_VEC = _full((1, D))


def _layer_spec(layer):
    return pl.BlockSpec((None, 2048, D), lambda *_: (layer, 0, 0))

_ROW = pl.BlockSpec((TM, D), lambda i, *_: (i, 0))


def _proj_fwd(x, pre_w, scale, shift, w, layer, name):
    tn, ni = 1024, SEQ // TM

    def body(x_ref, pw_ref, sc_ref, sh_ref, w_ref, o_ref, h_ref, h_scr):
        rows = pl.ds(pl.multiple_of(pl.program_id(1) * TM, TM), TM)

        @pl.when(pl.program_id(0) == 0)
        def _():
            xv = x_ref[...]
            h = ((xv * _rstd(xv) * pw_ref[...]) * (1.0 + sc_ref[...]) + sh_ref[...]).astype(h_ref.dtype)
            h_scr[rows, :] = h
            h_ref[...] = h
        o_ref[...] = jnp.dot(h_scr[rows, :], w_ref[...].astype(MXU), preferred_element_type=F32)

    first_pass = pl.BlockSpec((TM, D), lambda j, i: (jnp.where(j == 0, i, ni - 1), 0))
    return pl.pallas_call(body, name=name, out_shape=(_sds((SEQ, NP)), _sds((SEQ, D), MXU)), grid=(NP // tn, ni),
                          in_specs=[first_pass, _VEC, _VEC, _VEC, pl.BlockSpec((None, D, tn), lambda j, i: (layer, 0, j))],
                          out_specs=(pl.BlockSpec((TM, tn), lambda j, i: (i, j)), first_pass),
                          scratch_shapes=[pltpu.VMEM((SEQ, D), MXU)], compiler_params=_cp())(x, pre_w, scale, shift, w)


_HALF = pl.BlockSpec((TM, 512), lambda i: (i, 0))
_Z_A = pl.BlockSpec((TM, 512), lambda i: (i, ZA // 512))
_Z_C = pl.BlockSpec((TM, 512), lambda i: (i, ZC // 512))


def _out_fwd(o_a, yb, o_c, proj, w, layer, x, gate, post_w, name):
    def body(oa_ref, yb_ref, oc_ref, za_ref, zc_ref, w_ref, x_ref, g_ref, pw_ref, xn_ref, y_ref):
        y = (_mm(oa_ref[...] * _silu(za_ref[...]), w_ref[0:512, :]) + _mm(yb_ref[...], w_ref[512:1536, :])
             + _mm(oc_ref[...] * _silu(zc_ref[...]), w_ref[1536:2048, :]))
        y_ref[...] = y
        xn_ref[...] = x_ref[...] + g_ref[...] * (y * _rstd(y) * pw_ref[...])

    return pl.pallas_call(body, name=name, out_shape=(_sds((SEQ, D)), _sds((SEQ, D))), grid=(SEQ // TM,),
                          in_specs=[_HALF, _ROW, _HALF, _Z_A, _Z_C, _layer_spec(layer), _ROW, _VEC, _VEC],
                          out_specs=(_ROW, _ROW), compiler_params=_cp())(o_a, yb, o_c, proj, proj, w, x, gate, post_w)


def _dymix(dxo, y, gate, post_w, w, layer, o_a, o_c, proj, name):
    def body(dx_ref, y_ref, g_ref, pw_ref, w_ref, oa_ref, oc_ref, za_ref, zc_ref,
             dy_ref, dg_ref, dpw_ref, doa_ref, dza_ref, b_ref, doc_ref, dzc_ref):
        @pl.when(pl.program_id(0) == 0)
        def _():
            dg_ref[...] = jnp.zeros_like(dg_ref)
            dpw_ref[...] = jnp.zeros_like(dpw_ref)
        dx, yv = dx_ref[...], y_ref[...]
        rstd = _rstd(yv)
        n = yv * rstd
        dg_ref[...] += jnp.sum(dx * (n * pw_ref[...]), axis=0, keepdims=True)
        dr = dx * g_ref[...]
        dpw_ref[...] += jnp.sum(dr * n, axis=0, keepdims=True)
        dy = _rms_bwd(dr * pw_ref[...], n, rstd)
        dy_ref[...] = dy
        b_ref[...] = _mm(dy, w_ref[512:1536, :], NT)
        for rows, o_ref, z_ref, do_ref, dz_ref in ((slice(0, 512), oa_ref, za_ref, doa_ref, dza_ref),
                                                   (slice(1536, 2048), oc_ref, zc_ref, doc_ref, dzc_ref)):
            dyg, z = _mm(dy, w_ref[rows, :], NT), z_ref[...]
            do_ref[...] = dyg * _silu(z)
            dz_ref[...] = (dyg * o_ref[...] * _dsilu(z)).astype(dz_ref.dtype)

    return pl.pallas_call(body, name=name,
                          out_shape=(_sds((SEQ, D)), _sds((1, D)), _sds((1, D)),
                                     _sds((SEQ, 512)), _sds((SEQ, 512), MXU), _sds((SEQ, D)), _sds((SEQ, 512)), _sds((SEQ, 512), MXU)),
                          grid=(SEQ // TM,), in_specs=[_ROW, _ROW, _VEC, _VEC, _layer_spec(layer), _HALF, _HALF, _Z_A, _Z_C],
                          out_specs=(_ROW, _VEC, _VEC, _HALF, _HALF, _ROW, _HALF, _HALF),
                          compiler_params=_cp())(dxo, y, gate, post_w, w, o_a, o_c, proj, proj)


def _dwout(o_a, yb, o_c, proj, dy, name):
    def body(oa_ref, yb_ref, oc_ref, za_ref, zc_ref, dy_ref, o_ref):
        @pl.when(pl.program_id(0) == 0)
        def _():
            o_ref[...] = jnp.zeros_like(o_ref)
        dy = dy_ref[...]
        o_ref[0:512, :] += _mm(oa_ref[...] * _silu(za_ref[...]), dy, TN)
        o_ref[512:1536, :] += _mm(yb_ref[...], dy, TN)
        o_ref[1536:2048, :] += _mm(oc_ref[...] * _silu(zc_ref[...]), dy, TN)

    return pl.pallas_call(body, name=name, out_shape=_sds((2048, D)), grid=(SEQ // TM,),
                          in_specs=[_HALF, _ROW, _HALF, _Z_A, _Z_C, _ROW], out_specs=_full((2048, D)),
                          compiler_params=_cp())(o_a, yb, o_c, proj, proj, dy)


def _dwin(h, pieces, name):
    n = len(pieces)
    widths = [p.shape[1] for p in pieces]
    half = NP // 2

    def body(*refs):
        h_ref, p_refs, o_ref = refs[0], refs[1:1 + n], refs[1 + n]

        @pl.when(pl.program_id(0) == 0)
        def _():
            o_ref[...] = jnp.zeros_like(o_ref)
        hv, c0 = h_ref[...], 0
        for p_ref, wd in zip(p_refs, widths):
            o_ref[:, c0:c0 + wd] += _mm(hv, p_ref[...], TN)
            c0 += wd

    return pl.pallas_call(body, name=name, out_shape=_sds((D, half)), grid=(SEQ // TM,),
                          in_specs=[_ROW] + [pl.BlockSpec((TM, wd), lambda k: (k, 0)) for wd in widths],
                          out_specs=_full((D, half)), compiler_params=_cp(56))(h, *pieces)


_TMH = 256


def _dh_bwd(pieces, w, x, pre_w, scale, dxo, name):
    n = len(pieces)
    widths = [p.shape[1] for p in pieces]

    def body(*refs):
        p_refs, (w_ref, x_ref, pw_ref, sc_ref, dxo_ref, dx_ref, dsh_ref, dsc_ref, dpw_ref) = refs[:n], refs[n:]

        @pl.when(pl.program_id(0) == 0)
        def _():
            for r in (dsh_ref, dsc_ref, dpw_ref):
                r[...] = jnp.zeros_like(r)
        dh, c0 = 0.0, 0
        for p_ref, wd in zip(p_refs, widths):
            dh = dh + _mm(p_ref[...], w_ref[:, c0:c0 + wd], NT)
            c0 += wd
        xv = x_ref[...]
        rstd = _rstd(xv)
        nrm = xv * rstd
        dsh_ref[...] += jnp.sum(dh, axis=0, keepdims=True)
        dsc_ref[...] += jnp.sum(dh * (nrm * pw_ref[...]), axis=0, keepdims=True)
        dhn = dh * (1.0 + sc_ref[...])
        dpw_ref[...] += jnp.sum(dhn * nrm, axis=0, keepdims=True)
        dx_ref[...] = _rms_bwd(dhn * pw_ref[...], nrm, rstd) + dxo_ref[...]

    row = pl.BlockSpec((_TMH, D), lambda i: (i, 0))
    return pl.pallas_call(body, name=name, out_shape=(_sds((SEQ, D)), _sds((1, D)), _sds((1, D)), _sds((1, D))),
                          grid=(SEQ // _TMH,),
                          in_specs=[pl.BlockSpec((_TMH, wd), lambda i: (i, 0)) for wd in widths]
                          + [pl.BlockSpec((None, D, NP), lambda i: (0, 0, 0)), row, _VEC, _VEC, row],
                          out_specs=(row, _VEC, _VEC, _VEC), compiler_params=_cp(56))(*pieces, w, x, pre_w, scale, dxo)


def _w_in_padded(land, name):
    rows = 128

    def body(l_ref, o_ref):
        o_ref[...] = _pad_cols(jnp.concatenate([l_ref[k] for k in range(4)], axis=1))

    return pl.pallas_call(body, name=name, out_shape=_sds((D, NP), land.dtype), grid=(D // rows,),
                          in_specs=[pl.BlockSpec((4, rows, SHARD_IN), lambda i: (0, i, 0))],
                          out_specs=pl.BlockSpec((rows, NP), lambda i: (i, 0)), compiler_params=_cp())(land)


def _grad_blocks(dwa, dwb, name):
    rows = 128

    def body(a_ref, b_ref, o_ref):
        g = _unpad_cols(jnp.concatenate([a_ref[...], b_ref[...]], axis=1))
        for k in range(4):
            o_ref[k] = g[:, SHARD_IN * k:SHARD_IN * (k + 1)].astype(o_ref.dtype)

    half = pl.BlockSpec((rows, NP // 2), lambda i: (i, 0))
    return pl.pallas_call(body, name=name, out_shape=_sds((4, D, SHARD_IN), jnp.bfloat16), grid=(D // rows,),
                          in_specs=[half, half], out_specs=pl.BlockSpec((4, rows, SHARD_IN), lambda i: (0, i, 0)),
                          compiler_params=_cp())(dwa, dwb)


def _loss_bwd(xf, tgt, name):
    def body(x_ref, t_ref, dx_ref, l_ref):
        @pl.when(pl.program_id(0) == 0)
        def _():
            l_ref[...] = jnp.zeros_like(l_ref)
        e = x_ref[...] - t_ref[...]
        dx_ref[...] = e * (1.0 / D)
        l_ref[...] += 0.5 * jnp.sum(jnp.mean(e * e, axis=1, keepdims=True), axis=0, keepdims=True)

    return pl.pallas_call(body, name=name, out_shape=(_sds((SEQ, D)), _sds((8, LANES))), grid=(SEQ // TM,),
                          in_specs=[_ROW, _ROW], out_specs=(_ROW, _full((8, LANES))), compiler_params=_cp())(xf, tgt)


def _mod_part(c_all, ada_w, ada_b, name):
    def body(c_ref, w_ref, b_ref, o_ref):
        o_ref[0] = _mm(_silu(c_ref[...]), w_ref[0]) + b_ref[0]

    return pl.pallas_call(body, name=name, out_shape=_sds((DEPTH, 8, 768)), grid=(DEPTH,),
                          in_specs=[_full((8, D)), pl.BlockSpec((1, D, 768), lambda i: (i, 0, 0)), pl.BlockSpec((1, 1, 768), lambda i: (i, 0, 0))],
                          out_specs=pl.BlockSpec((1, 8, 768), lambda i: (i, 0, 0)), compiler_params=_cp())(c_all, ada_w, ada_b)


def _ada_grad(c_t, dmod, name):
    def body(c_ref, d_ref, o_ref):
        ca = _silu(c_ref[...])
        dm = d_ref[0]
        acc = ca[:, 0:1] * dm[0:1, :]
        for s in range(1, 8):
            acc = acc + ca[:, s:s + 1] * dm[s:s + 1, :]
        o_ref[0] = acc

    return pl.pallas_call(body, name=name, out_shape=_sds((DEPTH, D, 768)), grid=(DEPTH,),
                          in_specs=[_full((D, LANES)), pl.BlockSpec((1, 8, 768), lambda i: (i, 0, 0))],
                          out_specs=pl.BlockSpec((1, D, 768), lambda i: (i, 0, 0)), compiler_params=_cp())(c_t, dmod)


def _pack(parts):
    flat = []
    for p in parts:
        f = p.reshape(-1)
        flat.append(jnp.pad(f, (0, (-f.size) % LANES)))
    v = jnp.concatenate(flat)
    return jnp.pad(v, (0, (-v.size) % (8 * LANES))).reshape(-1, LANES)


def _unpack(v, shapes):
    v = v.reshape(-1)
    out, off = [], 0
    for s in shapes:
        n = math.prod(s)
        out.append(v[off:off + n].reshape(s))
        off += n + (-n) % LANES
    return out


_GIVEN_DT, _GIVEN_C = 4608, 4624


def _pad_cols(w):
    return jnp.concatenate([w[..., :_GIVEN_DT], w[..., _GIVEN_C:], w[..., _GIVEN_DT:_GIVEN_C],
                            jnp.zeros(w.shape[:-1] + (NP - IN_COLS,), w.dtype)], axis=-1)


def _unpad_cols(w):
    return jnp.concatenate([w[..., :_GIVEN_DT], w[..., DTC:DTC + 16], w[..., _GIVEN_DT:DTC]], axis=-1)


def _pad_lanes(v):
    return jnp.pad(v, (0, LANES - v.shape[0])).reshape(1, LANES)


def _local_step(x2, tgt, mod, weights_of, grads_done, pre_w, post_w, conv_w, conv_b, dt_bias, a_log, d_skip, nw, sinks):
    saved = []
    xcur = x2
    for i in range(DEPTH):
        shift, scale, gate = mod[i:i + 1, :D], mod[i:i + 1, D:2 * D], mod[i:i + 1, 2 * D:]
        pw, qw = pre_w[i:i + 1], post_w[i:i + 1]
        w_p, w_o = weights_of(i, xcur)
        proj, h = _proj_fwd(xcur, pw, scale, shift, w_p, 0, "proj_fwd")
        o_a, lse_a = _attn_fwd(proj, QA // LANES, KA // LANES, VA // LANES, DILS, False, None, "attn_a_fwd")
        sink_x = jnp.repeat(sinks[i], HD).reshape(1, 512)
        o_c, lse_c = _attn_fwd(proj, QC // LANES, KC // LANES, VC // LANES, (1,), True, sink_x, "attn_c_fwd")
        cw, cb = conv_w[i], conv_b[i:i + 1]
        xbc_act = _conv_fwd(proj, cw, cb, "conv_fwd")
        ssd_p = (_pad_lanes(a_log[i]), _pad_lanes(dt_bias[i]), jnp.repeat(d_skip[i], HD).reshape(1, 1024), nw[i:i + 1])
        yb, hin = _ssd_fwd(xbc_act, proj, *ssd_p, "ssd_fwd")
        xnew, y = _out_fwd(o_a, yb, o_c, proj, w_o, 0, xcur, gate, qw, "out_fwd")
        saved.append((w_p, w_o, xcur, scale, gate, pw, qw, proj, h, o_a, lse_a, sink_x, o_c, lse_c, cw, cb, xbc_act, ssd_p, yb, hin, y))
        xcur = xnew
    dx, ltile = _loss_bwd(xcur, tgt, "loss")
    dmod, small = [None] * DEPTH, [None] * DEPTH
    for i in reversed(range(DEPTH)):
        w_p, w_o, xin, scale, gate, pw, qw, proj, h, o_a, lse_a, sink_x, o_c, lse_c, cw, cb, xbc_act, ssd_p, yb, hin, y = saved[i]
        dy, dgate, dpost, do_a, dz_a, dyb, do_c, dz_c = _dymix(dx, y, gate, qw, w_o, 0, o_a, o_c, proj, "dymix")
        dwo = _dwout(o_a, yb, o_c, proj, dy, "dwout")
        dq_a, dk_a, dv_a = _attn_bwd(proj, QA // LANES, KA // LANES, VA // LANES, do_a, o_a, lse_a, DILS, False, None, "attn_a_bwd")
        dq_c, dk_c, dv_c, dsk = _attn_bwd(proj, QC // LANES, KC // LANES, VC // LANES, do_c, o_c, lse_c, (1,), True, sink_x, "attn_c_bwd")
        dxbc_act, dz_b, ddt, dal16, ddtb, ddsk, dnw = _ssd_bwd(xbc_act, proj, hin, dyb, *ssd_p, "ssd_bwd")
        dxbc, dcw, dcb = _conv_bwd(proj, dxbc_act, cw, cb, "conv_bwd")
        half_a, half_b = [dq_a, dk_a, dv_a, dz_a, dz_b], [dxbc, dq_c, dz_c, dk_c, dv_c, ddt]
        sent = grads_done(i, _dwin(h, half_a, "dwin_a"), _dwin(h, half_b, "dwin_b"), dwo)
        dx, dshift, dscale, dpre = _dh_bwd(half_a + half_b, w_p, xin, pw, scale + sent[0, 0], dx, "dh_bwd")
        dmod[i] = jnp.concatenate([dshift, dscale, dgate], axis=1)
        small[i] = (dpre, dpost, dcw, dcb, ddtb[0, :16], dal16[0, :16], ddsk.reshape(16, HD).sum(axis=1), dnw, dsk[:, 0, ::HD].reshape(8))
    return ltile, dx, jnp.concatenate(dmod, axis=0), small


_SMALL = ((1, D), (1, D), (4, CONV_CH), (1, CONV_CH), (16,), (16,), (16,), (1, D), (8,))


def kernel(x, c, ada_w, ada_b, pre_norm_w, post_norm_w, w_in, conv_w, conv_b, dt_bias, a_log, d_skip, ssm_norm_w, sinks, w_out, loss_target, m_ada_w, m_ada_b, m_pre_norm_w, m_post_norm_w, m_w_in, m_conv_w, m_conv_b, m_dt_bias, m_a_log, m_d_skip, m_ssm_norm_w, m_sinks, m_w_out, v_ada_w, v_ada_b, v_pre_norm_w, v_post_norm_w, v_w_in, v_conv_w, v_conv_b, v_dt_bias, v_a_log, v_d_skip, v_ssm_norm_w, v_sinks, v_w_out):
    xi, yi, ci = lax.axis_index("x"), lax.axis_index("y"), lax.axis_index("c")
    chip = 2 * xi + yi
    me = 2 * chip + ci

    w_in_b = _cast_bf16(w_in, 512, "cast_w_in")
    w_out_b = _cast_bf16(w_out, 512, "cast_w_out")
    gathers = []
    for i in range(DEPTH):
        lands = [lax.dynamic_update_slice(lax.empty((4,) + a.shape[1:], a.dtype), a[i][None], (chip, 0, 0)) for a in (w_in_b, w_out_b)]
        gathers.append(_split_start(None, lands, f"gather_start{i}", "half" if i == 0 else "whole"))
    all_started = gathers[0][3] + gathers[1][3] + gathers[2][3] + gathers[3][3]

    def weights_of(i, after):
        send_sems, recv_sems, thru, _ = gathers[i]
        if i == 0:
            halves = _split_wait(send_sems, recv_sems, thru, 2, all_started + mod[:1, :LANES], "gather_wait0", "half")
            send_sems, recv_sems, thru, after = _split_start(None, halves, "share_start0", "sibling")
            g_in, g_out = _split_wait(send_sems, recv_sems, thru, 2, after, "share_wait0", "sibling")
        else:
            g_in, g_out = _split_wait(send_sems, recv_sems, thru, 2, after, f"gather_wait{i}")
        return _w_in_padded(g_in, "w_in_padded")[None], g_out.reshape(1, 2048, D)

    scatters = [None] * DEPTH

    def grads_done(i, dwa, dwb, dwo):
        blocks = [_grad_blocks(dwa, dwb, "grad_blocks"), _cast_bf16(dwo.reshape(4, 512, D), 512, "cast_dw_out")]
        scatters[i] = _split_start(blocks, [lax.empty(b.shape, b.dtype) for b in blocks], f"scatter_start{i}")
        return scatters[i][3]

    g0 = _allgather8(_pack([c, conv_w]), "gather_c")
    c_all = g0[:, :8, :].reshape(8, D)
    conv_w_full = jnp.concatenate([g0[2 * k, 8:56, :].reshape(DEPTH, 4, CONV_CH // 4) for k in range(4)], axis=-1)

    ada_b_mine = lax.dynamic_slice_in_dim(ada_b, 768 * chip, 768, axis=1).reshape(DEPTH, 1, 768)
    gm = _allgather8(_mod_part(c_all, ada_w, ada_b_mine, "mod_part").reshape(DEPTH * 8, 768), "gather_mod")
    gm = gm.reshape(4, 2, DEPTH, 8, 768)[:, 0]
    mod = lax.dynamic_index_in_dim(gm, me, axis=2, keepdims=False).transpose(1, 0, 2).reshape(DEPTH, 3 * D)

    ltile, dx, dmod, small = _local_step(x[0], loss_target[0], mod, weights_of, grads_done, pre_norm_w, post_norm_w, conv_w_full,
                                         conv_b, dt_bias, a_log, d_skip, ssm_norm_w, sinks)

    packed = _pack([dmod] + [g for layer in small for g in layer] + [ltile[0]])
    gs = _allgather8(packed, "gather_small")
    tot = _sum_blocks(gs[:, None], packed.shape[0], "sum_small")[0]
    parts = _unpack(tot, [(DEPTH, 3 * D)] + list(_SMALL) * DEPTH + [(LANES,)])
    g_ada_b, loss = parts[0], parts[-1][0]
    per_layer = [parts[1 + len(_SMALL) * i:1 + len(_SMALL) * (i + 1)] for i in range(DEPTH)]
    g_pre, g_post, g_cw, g_cb, g_dtb, g_al, g_dsk, g_nw, g_sk = [jnp.stack([per_layer[i][j] for i in range(DEPTH)]) for j in range(len(_SMALL))]
    g_pre, g_post, g_cb, g_nw = g_pre[:, 0], g_post[:, 0], g_cb[:, 0], g_nw[:, 0]
    g_cw = lax.dynamic_slice_in_dim(g_cw, (CONV_CH // 4) * chip, CONV_CH // 4, axis=2)

    dmod_all = gs[:, :(DEPTH * 3 * D) // LANES, :].reshape(8, DEPTH, 3 * D).transpose(1, 0, 2)
    dmod_mine = lax.dynamic_slice_in_dim(dmod_all, 768 * chip, 768, axis=2)
    c_t = jnp.pad(c_all.T, ((0, 0), (0, LANES - 8)))
    g_ada_w = _ada_grad(c_t, dmod_mine, "ada_grad")

    res = {}
    res["ada_w"] = _adamw(ada_w, [g_ada_w], m_ada_w, v_ada_w, 512, "adamw_ada_w")
    names = ["ada_b", "pre_norm_w", "post_norm_w", "conv_w", "conv_b", "dt_bias", "a_log", "d_skip", "ssm_norm_w", "sinks"]
    ws = [ada_b, pre_norm_w, post_norm_w, conv_w, conv_b, dt_bias, a_log, d_skip, ssm_norm_w, sinks]
    gsm = [g_ada_b, g_pre, g_post, g_cw, g_cb, g_dtb, g_al, g_dsk, g_nw, g_sk]
    ms = [m_ada_b, m_pre_norm_w, m_post_norm_w, m_conv_w, m_conv_b, m_dt_bias, m_a_log, m_d_skip, m_ssm_norm_w, m_sinks]
    vs = [v_ada_b, v_pre_norm_w, v_post_norm_w, v_conv_w, v_conv_b, v_dt_bias, v_a_log, v_d_skip, v_ssm_norm_w, v_sinks]
    pw_, pg_, pm_, pv_ = _pack(ws), _pack(gsm), _pack(ms), _pack(vs)
    small_out = _adamw(pw_[None], [pg_[None]], pm_[None], pv_[None], pw_.shape[0], "adamw_small")

    others_done = small_out[1][0, :8] + res["ada_w"][1][0, :8, :LANES]
    landed = [_split_wait(*scatters[i][:3], 2, others_done, f"scatter_wait{i}") for i in range(DEPTH)]
    p_in = _sum_chips([d[2] for d in landed], [d[0] for d in landed], 128, "sum_w_in")
    p_out = _sum_chips([d[3] for d in landed], [d[1] for d in landed], 256, "sum_w_out")
    col_major, row_major = (lambda a: jnp.transpose(a, (2, 0, 1))), (lambda a: jnp.transpose(a, (1, 2, 0)))
    p_in = col_major(p_in)
    s_in, s_out = _sibling_swap([p_in, p_out], "swap_partials")
    res["w_in"] = [row_major(a) for a in _adamw(col_major(w_in), [p_in, s_in], col_major(m_w_in), col_major(v_w_in), None,
                                                "adamw_w_in", lead=SHARD_IN // 18)]
    res["w_out"] = _adamw(w_out, [p_out, s_out], m_w_out, v_w_out, 512, "adamw_w_out")
    shapes = [w.shape for w in ws]
    for kind in range(4):
        for nm, a in zip(names, _unpack(small_out[kind][0], shapes)):
            res.setdefault(nm, [None] * 4)[kind] = a
    order = ["ada_w", "ada_b", "pre_norm_w", "post_norm_w", "w_in", "conv_w", "conv_b", "dt_bias", "a_log", "d_skip", "ssm_norm_w", "sinks", "w_out"]
    return (loss, dx[None], *[res[n][0] for n in order], *[res[n][1] for n in order], *[res[n][2] for n in order], *[res[n][3] for n in order])
```

```python
import math

import jax
import jax.numpy as jnp
from jax import lax
from jax.experimental import pallas as pl
from jax.experimental.pallas import tpu as pltpu

F32 = jnp.float32
MXU = jnp.bfloat16
HI = lax.Precision.HIGHEST
MESH = pl.DeviceIdType.MESH

SEQ = 4096
D = 1024
DEPTH = 4
HD = 64
QK_SCALE = HD ** -0.5
LANES = 128
BLK = 128
DILS = (1, 4, 16)
NEG = -1e30
EPS = 1e-6
MIB = 1024 * 1024

NP = 6144
QA, KA, VA, ZA = 0, 512, 1024, 1536
ZB, XBC = 2048, 3072
QC, ZC, KC, VC = 4608, 5120, 5632, 5760
DTC = 5888
IN_COLS = 5904
SHARD_IN = IN_COLS // 4
CONV_CH = 1536
TM = 512

ADAM_LR, ADAM_B1, ADAM_B2, ADAM_EPS, ADAM_WD, ADAM_STEP = 0.001, 0.9, 0.999, 1e-08, 0.01, 10

NT = (((1,), (1,)), ((), ()))
TN = (((0,), (0,)), ((), ()))


def _cp(vmem_mib=48):
    return pltpu.CompilerParams(vmem_limit_bytes=vmem_mib * MIB)


def _sds(shape, dtype=F32):
    return jax.ShapeDtypeStruct(shape, dtype)


def _full(shape):
    n = len(shape)
    return pl.BlockSpec(shape, lambda *_: (0,) * n)


def _mm(a, b, dims=None):
    if dims is None:
        return jnp.dot(a.astype(MXU), b.astype(MXU), preferred_element_type=F32)
    return lax.dot_general(a.astype(MXU), b.astype(MXU), dims, preferred_element_type=F32)


def _sigmoid(x):
    return 1.0 / (1.0 + jnp.exp(-x))


def _silu(x):
    return x * _sigmoid(x)


def _dsilu(x):
    s = _sigmoid(x)
    return s * (1.0 + x * (1.0 - s))


def _softplus(x):
    ax = jnp.where(x >= 0, x, -x)
    return jnp.maximum(x, 0.0) + jnp.log1p(jnp.exp(-ax))


def _half_masks():
    lane = lax.broadcasted_iota(jnp.int32, (1, LANES), 1)
    m0 = (lane < HD).astype(F32)
    return m0, 1.0 - m0


def _allgather8(v, name):
    r, cc = v.shape

    def body(v_ref, out_ref, send_sems, recv_sems):
        x, y, c = lax.axis_index("x"), lax.axis_index("y"), lax.axis_index("c")
        me = 4 * x + 2 * y + c
        out_ref[me] = v_ref[...]
        peers = []
        for k in range(1, 8):
            px = 1 - x if k & 4 else x
            py = 1 - y if k & 2 else y
            pc = 1 - c if k & 1 else c
            peers.append((px, py, pc))
        sends = []
        for k, peer in enumerate(peers):
            cp = pltpu.make_async_remote_copy(src_ref=v_ref, dst_ref=out_ref.at[me], send_sem=send_sems.at[k],
                                              recv_sem=recv_sems.at[k], device_id=peer, device_id_type=MESH)
            cp.start()
            sends.append(cp)
        for k, (px, py, pc) in enumerate(peers):
            pltpu.make_async_remote_copy(src_ref=v_ref, dst_ref=out_ref.at[4 * px + 2 * py + pc], send_sem=send_sems.at[k],
                                         recv_sem=recv_sems.at[k], device_id=(px, py, pc), device_id_type=MESH).wait_recv()
        for cp in sends:
            cp.wait_send()

    return pl.pallas_call(
        body, name=name, out_shape=_sds((8, r, cc)),
        in_specs=[pl.BlockSpec(memory_space=pltpu.VMEM)], out_specs=pl.BlockSpec(memory_space=pltpu.VMEM),
        scratch_shapes=[pltpu.SemaphoreType.DMA((7,)), pltpu.SemaphoreType.DMA((7,))],
        compiler_params=_cp(32),
    )(v)


_HBM = pl.BlockSpec(memory_space=pltpu.HBM)
_SEM = pl.BlockSpec(memory_space=pltpu.SEMAPHORE)
_EFFECT = pltpu.SideEffectType.DATAFLOW_SIDE_EFFECTING


def _chip_copies(src_refs, land_refs, send_sems, recv_sems, part="whole"):
    x, y, c = lax.axis_index("x"), lax.axis_index("y"), lax.axis_index("c")
    mine = 2 * x + y
    out = []
    for i, land in enumerate(land_refs):
        half = land.shape[1] // 2
        own, others = pl.ds(pl.multiple_of(c * half, half), half), pl.ds(pl.multiple_of((1 - c) * half, half), half)
        for j, (px, py) in enumerate([(1 - x, y), (x, 1 - y), (1 - x, 1 - y)]):
            slot, peer = 2 * px + py, (px, py, c)
            if part == "whole":
                src = src_refs[i].at[slot] if src_refs else land.at[mine]
                there, here = land.at[mine], land.at[slot]
            elif part == "half":
                src = there = land.at[mine].at[own]
                here = land.at[slot].at[own]
            else:
                src = there = land.at[slot].at[own]
                here, peer = land.at[slot].at[others], (x, y, 1 - c)
            mk = lambda dst, i=i, j=j, src=src, peer=peer: pltpu.make_async_remote_copy(
                src_ref=src, dst_ref=dst, send_sem=send_sems.at[3 * i + j], recv_sem=recv_sems.at[3 * i + j],
                device_id=peer, device_id_type=MESH)
            out.append((mk(there), mk(here)))
    return out


def _split_start(srcs, lands, name, part="whole"):
    ops = list(srcs or []) + list(lands)
    ns, n = len(srcs or []), len(lands)

    def body(*refs):
        src_refs, land_refs = refs[:ns], refs[ns:ns + n]
        send_sems, recv_sems = refs[ns + n], refs[ns + n + 1]
        for mine_out, _ in _chip_copies(src_refs, land_refs, send_sems, recv_sems, part):
            mine_out.start()
        refs[-1][...] = jnp.zeros_like(refs[-1])

    sems = pltpu.SemaphoreType.DMA((3 * n,))
    res = pl.pallas_call(
        body, name=name, out_shape=(sems, sems) + tuple(pltpu.HBM(a.shape, a.dtype) for a in ops) + (_sds((8, LANES)),),
        in_specs=[_HBM] * len(ops), out_specs=(_SEM, _SEM) + (_HBM,) * len(ops) + (pl.BlockSpec(memory_space=pltpu.VMEM),),
        input_output_aliases={k: 2 + k for k in range(len(ops))},
        compiler_params=pltpu.CompilerParams(has_side_effects=_EFFECT),
    )(*[pltpu.with_memory_space_constraint(a, pltpu.HBM) for a in ops])
    return res[0], res[1], list(res[2:2 + len(ops)]), res[-1]


def _split_wait(send_sems, recv_sems, thru, n, after, name, part="whole"):
    ns = len(thru) - n

    def body(*refs):
        src_refs, land_refs = refs[:ns], refs[ns:ns + n]
        for mine_out, arriving in _chip_copies(src_refs, land_refs, refs[ns + n], refs[ns + n + 1], part):
            mine_out.wait_send()
            arriving.wait_recv()

    res = pl.pallas_call(
        body, name=name, out_shape=tuple(pltpu.HBM(a.shape, a.dtype) for a in thru),
        in_specs=[_HBM] * len(thru) + [_SEM, _SEM, pl.BlockSpec(memory_space=pl.ANY)], out_specs=(_HBM,) * len(thru),
        input_output_aliases={k: k for k in range(len(thru))},
        compiler_params=pltpu.CompilerParams(has_side_effects=_EFFECT),
    )(*thru, send_sems, recv_sems, after)
    return list(res)


def _sibling_swap(arrs, name):
    n = len(arrs)

    def body(*refs):
        ins, outs_, (send_sems, recv_sems) = refs[:n], refs[n:2 * n], refs[2 * n:]
        sib = (lax.axis_index("x"), lax.axis_index("y"), 1 - lax.axis_index("c"))
        cps = [pltpu.make_async_remote_copy(src_ref=ins[i], dst_ref=outs_[i], send_sem=send_sems.at[i], recv_sem=recv_sems.at[i],
                                            device_id=sib, device_id_type=MESH) for i in range(n)]
        for cp in cps:
            cp.start()
        for cp in cps:
            cp.wait_recv()
        for cp in cps:
            cp.wait_send()

    hbm = pl.BlockSpec(memory_space=pltpu.HBM)
    return pl.pallas_call(
        body, name=name, out_shape=tuple(_sds(a.shape, a.dtype) for a in arrs), in_specs=[hbm] * n, out_specs=tuple([hbm] * n),
        scratch_shapes=[pltpu.SemaphoreType.DMA((n,)), pltpu.SemaphoreType.DMA((n,))],
    )(*arrs)


def _tile_spec(rows, cc):
    return pl.BlockSpec((None, rows, cc), lambda l, i: (l, i, 0))


def _cast_bf16(a, rows, name):
    nl, r, cc = a.shape

    def body(a_ref, o_ref):
        o_ref[...] = a_ref[...].astype(jnp.bfloat16)

    return pl.pallas_call(body, name=name, out_shape=_sds((nl, r, cc), jnp.bfloat16), grid=(nl, r // rows),
                          in_specs=[_tile_spec(rows, cc)], out_specs=_tile_spec(rows, cc), compiler_params=_cp())(a)


def _sum_blocks(a, rows, name):
    k, nl, r, cc = a.shape

    def body(a_ref, o_ref):
        acc = a_ref[0].astype(F32)
        for j in range(1, k):
            acc = acc + a_ref[j].astype(F32)
        o_ref[...] = acc

    return pl.pallas_call(body, name=name, out_shape=_sds((nl, r, cc)), grid=(nl, r // rows),
                          in_specs=[pl.BlockSpec((k, None, rows, cc), lambda l, i: (0, l, i, 0))],
                          out_specs=_tile_spec(rows, cc), compiler_params=_cp())(a)


def _sum_chips(lands, srcs, rows, name):
    nl = len(lands)
    _, r, cc = lands[0].shape

    def body(*refs):
        land_refs, src_refs, o_ref = refs[:nl], refs[nl:2 * nl], refs[2 * nl]
        mine = 2 * lax.axis_index("x") + lax.axis_index("y")
        for j in range(nl):
            @pl.when(pl.program_id(0) == j)
            def _(j=j):
                own = src_refs[j][mine].astype(F32)
                acc = None
                for k in range(4):
                    term = jnp.where(mine == k, own, land_refs[j][k].astype(F32))
                    acc = term if acc is None else acc + term
                o_ref[...] = acc

    specs = [pl.BlockSpec((4, rows, cc), lambda l, i, j=j: (0, jnp.where(l == j, i, 0), 0)) for j in range(nl)]
    return pl.pallas_call(body, name=name, out_shape=_sds((nl, r, cc)), grid=(nl, r // rows),
                          in_specs=specs + specs, out_specs=_tile_spec(rows, cc), compiler_params=_cp())(*lands, *srcs)


def _adamw(w, parts, m, v, rows, name, lead=None):
    nl, r, cc = w.shape
    np_ = len(parts)
    c1 = 1.0 / (1.0 - ADAM_B1 ** ADAM_STEP)
    c2 = 1.0 / (1.0 - ADAM_B2 ** ADAM_STEP)

    def body(*refs):
        w_ref, p_refs, (m_ref, v_ref, g_ref, d_ref, nm_ref, nv_ref) = refs[0], refs[1:1 + np_], refs[1 + np_:]
        g = p_refs[0][...]
        for p_ref in p_refs[1:]:
            g = g + p_ref[...]
        nm = ADAM_B1 * m_ref[...] + (1.0 - ADAM_B1) * g
        nv = ADAM_B2 * v_ref[...] + (1.0 - ADAM_B2) * (g * g)
        g_ref[...] = g
        nm_ref[...] = nm
        nv_ref[...] = nv
        d_ref[...] = -ADAM_LR * ((nm * c1) / (jnp.sqrt(nv * c2) + ADAM_EPS) + ADAM_WD * w_ref[...])

    if lead is None:
        spec, grid = _tile_spec(rows, cc), (nl, r // rows)
    else:
        spec, grid = pl.BlockSpec((lead, r, cc), lambda i: (i, 0, 0)), (nl // lead,)
    return pl.pallas_call(body, name=name, out_shape=(_sds((nl, r, cc)),) * 4, grid=grid,
                          in_specs=[spec] * (3 + np_), out_specs=(spec,) * 4, compiler_params=_cp())(w, *parts, m, v)


_BIAS = pltpu.VMEM((2, 2 * BLK, 2 * BLK), F32)


def _fill_band_bias(bias_ref):
    qi = lax.broadcasted_iota(jnp.int32, (2 * BLK, 2 * BLK), 0) & (BLK - 1)
    kj = lax.broadcasted_iota(jnp.int32, (2 * BLK, 2 * BLK), 1)
    dist = BLK + qi - kj
    band = (dist >= 0) & (dist <= BLK)
    bias_ref[0] = jnp.where(band, 0.0, NEG)
    bias_ref[1] = jnp.where(band & (kj >= BLK), 0.0, NEG)


class _HeadStack:
    def __init__(self, group):
        self.m0, self.m1 = _half_masks()
        self.group = group
        if group is not None:
            self.kv_mask = (self.m0, self.m1)[group]

    def _swap_half(self, t, a):
        return t if a == self.group else pltpu.roll(t, HD, axis=1)

    def stack(self, t):
        low = lax.broadcasted_iota(jnp.int32, (1, LANES), 1) < HD
        t0, t1 = jnp.where(low, t, 0.0), jnp.where(low, 0.0, t)
        if self.group is not None:
            t0, t1 = self._swap_half(t0, 0), self._swap_half(t1, 1)
        return jnp.concatenate([t0, t1], axis=0)

    def unstack(self, ts):
        if self.group is None:
            return jnp.where(lax.broadcasted_iota(jnp.int32, (1, LANES), 1) < HD, ts[:BLK], ts[BLK:])
        return self._swap_half(ts[:BLK] * self.kv_mask, 0) + self._swap_half(ts[BLK:] * self.kv_mask, 1)


def _rows(st, dil):
    if dil == 1:
        return pl.ds(pl.multiple_of(st, BLK), BLK)
    return pl.ds(st, BLK, stride=dil)


def _block_pos(n, dil):
    nb = SEQ // (dil * BLK)
    r, b = n // nb, n % nb
    hp = (b > 0).astype(jnp.int32)
    st = r + dil * BLK * b
    return st, st - dil * BLK * hp, 1 - hp


def _attn_fwd(proj, qblk, kblk, vblk, dils, gqa, sink_x, name):
    has_sink = sink_x is not None

    def body(*refs):
        if has_sink:
            q_ref, k_ref, v_ref, s_ref, o_ref, lse_ref, m_scr, z_scr, bias_scr = refs
        else:
            q_ref, k_ref, v_ref, o_ref, lse_ref, m_scr, z_scr, bias_scr = refs

        @pl.when(pl.program_id(0) == 0)
        def _():
            _fill_band_bias(bias_scr)
        o_ref[...] = jnp.zeros_like(o_ref)
        if has_sink:
            z_scr[...] = jnp.ones_like(z_scr)
            m_scr[...] = jnp.broadcast_to(s_ref[...], m_scr.shape)
        else:
            z_scr[...] = jnp.zeros_like(z_scr)
            m_scr[...] = jnp.full_like(m_scr, NEG)

        def step(n, carry, dil, heads):
            m0, m1 = heads.m0, heads.m1
            st, stp, first = _block_pos(n, dil)
            rq, rp = _rows(st, dil), _rows(stp, dil)
            kk = jnp.concatenate([k_ref[rp, :], k_ref[rq, :]], axis=0)
            vv = jnp.concatenate([v_ref[rp, :], v_ref[rq, :]], axis=0)
            s = _mm(heads.stack(q_ref[rq, :] * QK_SCALE), kk, NT) + bias_scr[first]
            m = jnp.max(s, axis=1, keepdims=True)
            p = jnp.exp(s - m)
            l = jnp.sum(p, axis=1, keepdims=True)
            o_pair = heads.unstack(_mm(p, vv))
            low = m0 > 0.0
            m_pair = jnp.where(low, m[:BLK], m[BLK:])
            l_pair = jnp.where(low, l[:BLK], l[BLK:])
            m_old = m_scr[rq, :]
            m_new = jnp.maximum(m_old, m_pair)
            alpha, beta = jnp.exp(m_old - m_new), jnp.exp(m_pair - m_new)
            o_ref[rq, :] = o_ref[rq, :] * alpha + o_pair * beta
            z_scr[rq, :] = z_scr[rq, :] * alpha + l_pair * beta
            m_scr[rq, :] = m_new
            return carry

        def blocks(heads):
            for dil in dils:
                lax.fori_loop(0, SEQ // BLK, lambda n, carry, dil=dil: step(n, carry, dil, heads), 0, unroll=8 if gqa else 16)

        if gqa:
            for grp in range(2):
                pl.when(pl.program_id(0) // 2 == grp)(lambda grp=grp: blocks(_HeadStack(grp)))
        else:
            blocks(_HeadStack(None))

        def fin(t, carry):
            rt = pl.ds(pl.multiple_of(t * TM, TM), TM)
            z = z_scr[rt, :]
            o_ref[rt, :] = o_ref[rt, :] / z
            lse_ref[rt, :] = m_scr[rt, :] + jnp.log(z)
            return carry
        lax.fori_loop(0, SEQ // TM, fin, 0)

    col = lambda blk: pl.BlockSpec((SEQ, LANES), lambda p, blk=blk: (0, blk + p))
    kv = (lambda blk: pl.BlockSpec((SEQ, LANES), lambda p, blk=blk: (0, blk))) if gqa else col
    in_specs = [col(qblk), kv(kblk), kv(vblk)]
    args = [proj, proj, proj]
    if has_sink:
        in_specs.append(pl.BlockSpec((1, LANES), lambda p: (0, p)))
        args.append(sink_x)
    out = pl.BlockSpec((SEQ, LANES), lambda p: (0, p))
    return pl.pallas_call(body, name=name, out_shape=(_sds((SEQ, 512)), _sds((SEQ, 512))), grid=(4,),
                          in_specs=in_specs, out_specs=(out, out),
                          scratch_shapes=[pltpu.VMEM((SEQ, LANES), F32), pltpu.VMEM((SEQ, LANES), F32), _BIAS],
                          compiler_params=_cp(48))(*args)


def _attn_bwd(proj, qblk, kblk, vblk, do, o, lse, dils, gqa, sink_x, name):
    has_sink = sink_x is not None

    def body(*refs):
        if has_sink:
            q_ref, k_ref, v_ref, do_ref, o_ref, lse_ref, s_ref, dq_ref, dk_ref, dv_ref, ds_ref, bias_scr = refs
        else:
            q_ref, k_ref, v_ref, do_ref, o_ref, lse_ref, dq_ref, dk_ref, dv_ref, bias_scr = refs
        pid = pl.program_id(0)

        @pl.when(pid == 0)
        def _():
            _fill_band_bias(bias_scr)
        dq_ref[...] = jnp.zeros_like(dq_ref)
        if gqa:
            @pl.when(pid == 0)
            def _():
                dk_ref[...] = jnp.zeros_like(dk_ref)
                dv_ref[...] = jnp.zeros_like(dv_ref)
        else:
            dk_ref[...] = jnp.zeros_like(dk_ref)
            dv_ref[...] = jnp.zeros_like(dv_ref)

        def step(n, carry, dil, heads):
            m0, m1 = heads.m0, heads.m1
            st, stp, first = _block_pos(n, dil)
            rq, rp = _rows(st, dil), _rows(stp, dil)
            do_, lse_ = do_ref[rq, :], lse_ref[rq, :]
            kk = jnp.concatenate([k_ref[rp, :], k_ref[rq, :]], axis=0)
            vv = jnp.concatenate([v_ref[rp, :], v_ref[rq, :]], axis=0)
            qs, dos = heads.stack(q_ref[rq, :] * QK_SCALE), heads.stack(do_)
            doo = do_ * o_ref[rq, :]
            delta = jnp.concatenate([jnp.sum(doo * m0, axis=1, keepdims=True), jnp.sum(doo * m1, axis=1, keepdims=True)], axis=0)
            lse_s = jnp.concatenate([lse_[:, 0:1], lse_[:, HD:HD + 1]], axis=0)
            p = jnp.exp(_mm(qs, kk, NT) + bias_scr[first] - lse_s)
            ds = p * (_mm(dos, vv, NT) - delta)
            dq_ref[rq, :] += heads.unstack(_mm(ds, kk)) * QK_SCALE
            dk_sum, dv_sum = _mm(ds, qs, TN), _mm(p, dos, TN)
            dk_ref[rp, :] += dk_sum[:BLK]
            dk_ref[rq, :] += dk_sum[BLK:]
            dv_ref[rp, :] += dv_sum[:BLK]
            dv_ref[rq, :] += dv_sum[BLK:]
            return carry

        def blocks(heads):
            for dil in dils:
                lax.fori_loop(0, SEQ // BLK, lambda n, carry, dil=dil: step(n, carry, dil, heads), 0, unroll=4)

        if gqa:
            for grp in range(2):
                pl.when(pid // 2 == grp)(lambda grp=grp: blocks(_HeadStack(grp)))
        else:
            blocks(_HeadStack(None))

        if has_sink:
            m0, m1 = _half_masks()

            def sink_rows(t, acc):
                rt = pl.ds(pl.multiple_of(t * TM, TM), TM)
                return acc - jnp.sum(jnp.exp(s_ref[...] - lse_ref[rt, :]) * (do_ref[rt, :] * o_ref[rt, :]), axis=0, keepdims=True)
            acc = lax.fori_loop(0, SEQ // TM, sink_rows, jnp.zeros((1, LANES), F32))
            per_head = jnp.sum(acc * m0, axis=1, keepdims=True) * m0 + jnp.sum(acc * m1, axis=1, keepdims=True) * m1
            ds_ref[0] = jnp.broadcast_to(per_head, (8, LANES))

    col = lambda blk: pl.BlockSpec((SEQ, LANES), lambda p, blk=blk: (0, blk + p))
    kv = (lambda blk: pl.BlockSpec((SEQ, LANES), lambda p, blk=blk: (0, blk))) if gqa else col
    pair = pl.BlockSpec((SEQ, LANES), lambda p: (0, p))
    in_specs = [col(qblk), kv(kblk), kv(vblk), pair, pair, pair]
    args = [proj, proj, proj, do, o, lse]
    kvw = LANES if gqa else 512
    kv_out = pl.BlockSpec((SEQ, LANES), lambda p: (0, 0)) if gqa else pair
    out_shape = [_sds((SEQ, 512)), _sds((SEQ, kvw)), _sds((SEQ, kvw))]
    out_specs = [pair, kv_out, kv_out]
    if has_sink:
        in_specs.append(pl.BlockSpec((1, LANES), lambda p: (0, p)))
        args.append(sink_x)
        out_shape.append(_sds((4, 8, LANES)))
        out_specs.append(pl.BlockSpec((1, 8, LANES), lambda p: (p, 0, 0)))
    return pl.pallas_call(body, name=name, out_shape=tuple(out_shape), grid=(4,), in_specs=in_specs,
                          out_specs=tuple(out_specs), scratch_shapes=[_BIAS], compiler_params=_cp(56))(*args)


_CT = 128


def _rows_before(x_ref, t, k):
    if t == 0:
        return jnp.concatenate([jnp.zeros((k, LANES), F32), x_ref[0:_CT - k, :]], axis=0)
    return x_ref[t * _CT - k:(t + 1) * _CT - k, :]


def _conv_pre(x_ref, w_ref, b_ref, t):
    taps = [x_ref[t * _CT:(t + 1) * _CT, :]] + [_rows_before(x_ref, t, k) for k in range(1, 4)]
    u = b_ref[...] + taps[0] * w_ref[3:4, :]
    for k in range(1, 4):
        u = u + taps[k] * w_ref[3 - k:4 - k, :]
    return u, taps


def _conv_fwd(proj, w, b, name):
    def body(x_ref, w_ref, b_ref, o_ref):
        for t in range(SEQ // _CT):
            o_ref[t * _CT:(t + 1) * _CT, :] = _silu(_conv_pre(x_ref, w_ref, b_ref, t)[0])

    nblk = CONV_CH // LANES
    return pl.pallas_call(body, name=name, out_shape=_sds((SEQ, CONV_CH)), grid=(nblk,),
                          in_specs=[pl.BlockSpec((SEQ, LANES), lambda j: (0, XBC // LANES + j)),
                                    pl.BlockSpec((4, LANES), lambda j: (0, j)), pl.BlockSpec((1, LANES), lambda j: (0, j))],
                          out_specs=pl.BlockSpec((SEQ, LANES), lambda j: (0, j)), compiler_params=_cp())(proj, w, b)


def _conv_bwd(proj, dact, w, b, name):
    def body(x_ref, da_ref, w_ref, b_ref, dx_ref, dw_ref, db_ref, du_scr):
        du_scr[SEQ:SEQ + 8, :] = jnp.zeros((8, LANES), F32)
        db = jnp.zeros((1, LANES), F32)
        dws = [jnp.zeros((1, LANES), F32)] * 4
        for t in range(SEQ // _CT):
            u, taps = _conv_pre(x_ref, w_ref, b_ref, t)
            du = da_ref[t * _CT:(t + 1) * _CT, :] * _dsilu(u)
            du_scr[t * _CT:(t + 1) * _CT, :] = du
            db = db + jnp.sum(du, axis=0, keepdims=True)
            dws = [dws[k] + jnp.sum(du * taps[k], axis=0, keepdims=True) for k in range(4)]
        db_ref[...] = db
        for k in range(4):
            dw_ref[3 - k:4 - k, :] = dws[k]
        for t in range(SEQ // _CT):
            dx = du_scr[t * _CT:(t + 1) * _CT, :] * w_ref[3:4, :]
            for k in range(1, 4):
                dx = dx + du_scr[t * _CT + k:(t + 1) * _CT + k, :] * w_ref[3 - k:4 - k, :]
            dx_ref[t * _CT:(t + 1) * _CT, :] = dx.astype(dx_ref.dtype)

    nblk = CONV_CH // LANES
    blk = pl.BlockSpec((SEQ, LANES), lambda j: (0, j))
    wspec, bspec = pl.BlockSpec((4, LANES), lambda j: (0, j)), pl.BlockSpec((1, LANES), lambda j: (0, j))
    return pl.pallas_call(body, name=name, out_shape=(_sds((SEQ, CONV_CH), MXU), _sds((4, CONV_CH)), _sds((1, CONV_CH))), grid=(nblk,),
                          in_specs=[pl.BlockSpec((SEQ, LANES), lambda j: (0, XBC // LANES + j)), blk, wspec, bspec],
                          out_specs=(blk, wspec, bspec), scratch_shapes=[pltpu.VMEM((SEQ + 8, LANES), F32)],
                          compiler_params=_cp())(proj, dact, w, b)


def _column(t, h):
    lane = lax.broadcasted_iota(jnp.int32, (1, LANES), 1)
    pick = jax.custom_vjp(lambda v: v[:, h:h + 1])
    pick.defvjp(lambda v: (v[:, h:h + 1], None), lambda _, g: (g * (lane == h).astype(F32),))
    return pick(t)


def _row(t, h):
    sub = lax.broadcasted_iota(jnp.int32, (BLK, 1), 0)
    pick = jax.custom_vjp(lambda v: v[h:h + 1, :])
    pick.defvjp(lambda v: (v[h:h + 1, :], None), lambda _, g: (g * (sub == h).astype(F32),))
    return pick(t)


def _ssd_chunk(xs, bm, cm, dtr, z, hs, al16, dtb, dskx, nw):
    m0, m1 = _half_masks()
    row = lax.broadcasted_iota(jnp.int32, (BLK, BLK), 0)
    col = lax.broadcasted_iota(jnp.int32, (BLK, BLK), 1)
    causal = row >= col
    tril = causal.astype(F32)
    lane = lax.broadcasted_iota(jnp.int32, (1, LANES), 1)
    dt = jnp.where(lane < 16, _softplus(dtr + dtb), 0.0)
    a16 = -jnp.exp(al16)
    acum = jnp.dot(tril, dt * a16, precision=HI, preferred_element_type=F32)
    acum_t = acum.T
    gmat = [_mm(cm[g], bm[g], NT) for g in range(2)]
    ys, hn = [], []
    for p in range(8):
        g = p // 4
        col_h = [_column(acum, 2 * p + a) for a in range(2)]
        dt_x = _column(dt, 2 * p) * m0 + _column(dt, 2 * p + 1) * m1
        ac_x = col_h[0] * m0 + col_h[1] * m1
        a_end = _row(ac_x, BLK - 1)
        xdt = xs[p] * dt_x
        y = _mm(cm[g], hs[p]) * jnp.exp(ac_x)
        for a, msk in enumerate((m0, m1)):
            decay = jnp.exp(jnp.where(causal, col_h[a] - _row(acum_t, 2 * p + a), NEG))
            y = y + _mm(gmat[g] * decay, xdt * msk)
        st = _mm(bm[g], xdt * jnp.exp(a_end - ac_x), TN)
        hn.append(hs[p] * jnp.exp(a_end) + st)
        y = y + dskx[p] * xs[p]
        ys.append(y * _silu(z[p]))
    out = []
    for g in range(2):
        ms = sum(jnp.sum(ys[p] * ys[p], axis=1, keepdims=True) for p in range(4 * g, 4 * g + 4)) * (1.0 / 512)
        rstd = lax.rsqrt(ms + EPS)
        out += [ys[p] * rstd * nw[p] for p in range(4 * g, 4 * g + 4)]
    return out, hn


def _tiles(ref, n, off=0, rows=slice(None)):
    return [ref[rows, off + LANES * p:off + LANES * (p + 1)] for p in range(n)]


def _ssd_load(xbc_ref, z_ref, dt_ref, rows):
    return (_tiles(xbc_ref, 8, 0, rows), _tiles(xbc_ref, 2, 1024, rows), _tiles(xbc_ref, 2, 1280, rows), dt_ref[rows, :],
            _tiles(z_ref, 8, 0, rows))


def _ssd_params(al16_ref, dtb_ref, dsk_ref, nw_ref):
    return al16_ref[...], dtb_ref[...], _tiles(dsk_ref, 8), _tiles(nw_ref, 8)


_NCH = SEQ // BLK
_PER_STEP = 2
_STEP_ROWS = _PER_STEP * BLK


def _ssd_param_specs():
    return [_full((1, LANES)), _full((1, LANES)), _full((1, 1024)), _full((1, 1024))]


def _ssd_fwd(xbc_act, proj, al16, dtb, dskx, nw, name):
    def body(xbc_ref, z_ref, dt_ref, al16_ref, dtb_ref, dsk_ref, nw_ref, y_ref, hin_ref, h_scr):
        @pl.when(pl.program_id(0) == 0)
        def _():
            h_scr[...] = jnp.zeros_like(h_scr)
        params = _ssd_params(al16_ref, dtb_ref, dsk_ref, nw_ref)
        hs = _tiles(h_scr, 8)
        for k in range(_PER_STEP):
            rows = slice(BLK * k, BLK * (k + 1))
            for p in range(8):
                hin_ref[k, :, LANES * p:LANES * (p + 1)] = hs[p]
            ys, hs = _ssd_chunk(*_ssd_load(xbc_ref, z_ref, dt_ref, rows), hs, *params)
            for p in range(8):
                y_ref[rows, LANES * p:LANES * (p + 1)] = ys[p].astype(y_ref.dtype)
        for p in range(8):
            h_scr[:, LANES * p:LANES * (p + 1)] = hs[p]

    return pl.pallas_call(
        body, name=name, out_shape=(_sds((SEQ, 1024), MXU), _sds((_NCH, BLK, 1024))), grid=(_NCH // _PER_STEP,),
        in_specs=[pl.BlockSpec((_STEP_ROWS, CONV_CH), lambda c: (c, 0)), pl.BlockSpec((_STEP_ROWS, 1024), lambda c: (c, ZB // 1024)),
                  pl.BlockSpec((_STEP_ROWS, LANES), lambda c: (c, DTC // LANES))] + _ssd_param_specs(),
        out_specs=(pl.BlockSpec((_STEP_ROWS, 1024), lambda c: (c, 0)), pl.BlockSpec((_PER_STEP, BLK, 1024), lambda c: (c, 0, 0))),
        scratch_shapes=[pltpu.VMEM((BLK, 1024), F32)], compiler_params=_cp())(xbc_act, proj, proj, al16, dtb, dskx, nw)


def _ssd_bwd(xbc_act, proj, hin, dyb, al16, dtb, dskx, nw, name):
    def body(xbc_ref, z_ref, dt_ref, hin_ref, dy_ref, al16_ref, dtb_ref, dsk_ref, nw_ref,
             dxbc_ref, dz_ref, ddt_ref, dal16_ref, ddtb_ref, ddsk_ref, dnw_ref, dh_scr):
        @pl.when(pl.program_id(0) == 0)
        def _():
            dh_scr[...] = jnp.zeros_like(dh_scr)
            for r in (dal16_ref, ddtb_ref, ddsk_ref, dnw_ref):
                r[...] = jnp.zeros_like(r)
        params = _ssd_params(al16_ref, dtb_ref, dsk_ref, nw_ref)
        dhs = _tiles(dh_scr, 8)
        for k in reversed(range(_PER_STEP)):
            rows = slice(BLK * k, BLK * (k + 1))
            hs = [hin_ref[k, :, LANES * p:LANES * (p + 1)] for p in range(8)]
            _, vjp = jax.vjp(lambda a, h, q: _ssd_chunk(*a, h, *q), _ssd_load(xbc_ref, z_ref, dt_ref, rows), hs, params)
            (dxs, dbm, dcm, ddt, dz), dhs, (dal16, ddtb, ddsk, dnw) = vjp((_tiles(dy_ref, 8, 0, rows), dhs))
            for p in range(8):
                cols = slice(LANES * p, LANES * (p + 1))
                dxbc_ref[rows, cols] = dxs[p]
                dz_ref[rows, cols] = dz[p].astype(dz_ref.dtype)
                ddsk_ref[:, cols] += ddsk[p]
                dnw_ref[:, cols] += dnw[p]
            for g in range(2):
                dxbc_ref[rows, 1024 + LANES * g:1024 + LANES * (g + 1)] = dbm[g]
                dxbc_ref[rows, 1280 + LANES * g:1280 + LANES * (g + 1)] = dcm[g]
            ddt_ref[rows, :] = ddt.astype(ddt_ref.dtype)
            dal16_ref[...] += dal16
            ddtb_ref[...] += ddtb
        for p in range(8):
            dh_scr[:, LANES * p:LANES * (p + 1)] = dhs[p]

    rev = lambda c: _NCH // _PER_STEP - 1 - c
    return pl.pallas_call(
        body, name=name,
        out_shape=(_sds((SEQ, CONV_CH)), _sds((SEQ, 1024), MXU), _sds((SEQ, LANES), MXU),
                   _sds((1, LANES)), _sds((1, LANES)), _sds((1, 1024)), _sds((1, 1024))),
        grid=(_NCH // _PER_STEP,),
        in_specs=[pl.BlockSpec((_STEP_ROWS, CONV_CH), lambda c: (rev(c), 0)), pl.BlockSpec((_STEP_ROWS, 1024), lambda c: (rev(c), ZB // 1024)),
                  pl.BlockSpec((_STEP_ROWS, LANES), lambda c: (rev(c), DTC // LANES)),
                  pl.BlockSpec((_PER_STEP, BLK, 1024), lambda c: (rev(c), 0, 0)),
                  pl.BlockSpec((_STEP_ROWS, 1024), lambda c: (rev(c), 0))] + _ssd_param_specs(),
        out_specs=(pl.BlockSpec((_STEP_ROWS, CONV_CH), lambda c: (rev(c), 0)), pl.BlockSpec((_STEP_ROWS, 1024), lambda c: (rev(c), 0)),
                   pl.BlockSpec((_STEP_ROWS, LANES), lambda c: (rev(c), 0)),
                   _full((1, LANES)), _full((1, LANES)), _full((1, 1024)), _full((1, 1024))),
        scratch_shapes=[pltpu.VMEM((BLK, 1024), F32)], compiler_params=_cp())(xbc_act, proj, proj, hin, dyb, al16, dtb, dskx, nw)


def _rstd(v):
    return lax.rsqrt(jnp.mean(v * v, axis=1, keepdims=True) + EPS)


def _rms_bwd(dn, n, rstd):
    return rstd * (dn - n * jnp.mean(dn * n, axis=1, keepdims=True))


_VEC = _full((1, D))


def _layer_spec(layer):
    return pl.BlockSpec((None, 2048, D), lambda *_: (layer, 0, 0))

_ROW = pl.BlockSpec((TM, D), lambda i, *_: (i, 0))


def _proj_fwd(x, pre_w, scale, shift, w, layer, name):
    tn, ni = 1024, SEQ // TM

    def body(x_ref, pw_ref, sc_ref, sh_ref, w_ref, o_ref, h_ref, h_scr):
        rows = pl.ds(pl.multiple_of(pl.program_id(1) * TM, TM), TM)

        @pl.when(pl.program_id(0) == 0)
        def _():
            xv = x_ref[...]
            h = ((xv * _rstd(xv) * pw_ref[...]) * (1.0 + sc_ref[...]) + sh_ref[...]).astype(h_ref.dtype)
            h_scr[rows, :] = h
            h_ref[...] = h
        o_ref[...] = jnp.dot(h_scr[rows, :], w_ref[...].astype(MXU), preferred_element_type=F32)

    first_pass = pl.BlockSpec((TM, D), lambda j, i: (jnp.where(j == 0, i, ni - 1), 0))
    return pl.pallas_call(body, name=name, out_shape=(_sds((SEQ, NP)), _sds((SEQ, D), MXU)), grid=(NP // tn, ni),
                          in_specs=[first_pass, _VEC, _VEC, _VEC, pl.BlockSpec((None, D, tn), lambda j, i: (layer, 0, j))],
                          out_specs=(pl.BlockSpec((TM, tn), lambda j, i: (i, j)), first_pass),
                          scratch_shapes=[pltpu.VMEM((SEQ, D), MXU)], compiler_params=_cp())(x, pre_w, scale, shift, w)


_HALF = pl.BlockSpec((TM, 512), lambda i: (i, 0))
_Z_A = pl.BlockSpec((TM, 512), lambda i: (i, ZA // 512))
_Z_C = pl.BlockSpec((TM, 512), lambda i: (i, ZC // 512))


def _out_fwd(o_a, yb, o_c, proj, w, layer, x, gate, post_w, name):
    def body(oa_ref, yb_ref, oc_ref, za_ref, zc_ref, w_ref, x_ref, g_ref, pw_ref, xn_ref, y_ref):
        y = (_mm(oa_ref[...] * _silu(za_ref[...]), w_ref[0:512, :]) + _mm(yb_ref[...], w_ref[512:1536, :])
             + _mm(oc_ref[...] * _silu(zc_ref[...]), w_ref[1536:2048, :]))
        y_ref[...] = y
        xn_ref[...] = x_ref[...] + g_ref[...] * (y * _rstd(y) * pw_ref[...])

    return pl.pallas_call(body, name=name, out_shape=(_sds((SEQ, D)), _sds((SEQ, D))), grid=(SEQ // TM,),
                          in_specs=[_HALF, _ROW, _HALF, _Z_A, _Z_C, _layer_spec(layer), _ROW, _VEC, _VEC],
                          out_specs=(_ROW, _ROW), compiler_params=_cp())(o_a, yb, o_c, proj, proj, w, x, gate, post_w)


def _dymix(dxo, y, gate, post_w, w, layer, o_a, o_c, proj, name):
    def body(dx_ref, y_ref, g_ref, pw_ref, w_ref, oa_ref, oc_ref, za_ref, zc_ref,
             dy_ref, dg_ref, dpw_ref, doa_ref, dza_ref, b_ref, doc_ref, dzc_ref):
        @pl.when(pl.program_id(0) == 0)
        def _():
            dg_ref[...] = jnp.zeros_like(dg_ref)
            dpw_ref[...] = jnp.zeros_like(dpw_ref)
        dx, yv = dx_ref[...], y_ref[...]
        rstd = _rstd(yv)
        n = yv * rstd
        dg_ref[...] += jnp.sum(dx * (n * pw_ref[...]), axis=0, keepdims=True)
        dr = dx * g_ref[...]
        dpw_ref[...] += jnp.sum(dr * n, axis=0, keepdims=True)
        dy = _rms_bwd(dr * pw_ref[...], n, rstd)
        dy_ref[...] = dy
        b_ref[...] = _mm(dy, w_ref[512:1536, :], NT)
        for rows, o_ref, z_ref, do_ref, dz_ref in ((slice(0, 512), oa_ref, za_ref, doa_ref, dza_ref),
                                                   (slice(1536, 2048), oc_ref, zc_ref, doc_ref, dzc_ref)):
            dyg, z = _mm(dy, w_ref[rows, :], NT), z_ref[...]
            do_ref[...] = dyg * _silu(z)
            dz_ref[...] = (dyg * o_ref[...] * _dsilu(z)).astype(dz_ref.dtype)

    return pl.pallas_call(body, name=name,
                          out_shape=(_sds((SEQ, D)), _sds((1, D)), _sds((1, D)),
                                     _sds((SEQ, 512)), _sds((SEQ, 512), MXU), _sds((SEQ, D)), _sds((SEQ, 512)), _sds((SEQ, 512), MXU)),
                          grid=(SEQ // TM,), in_specs=[_ROW, _ROW, _VEC, _VEC, _layer_spec(layer), _HALF, _HALF, _Z_A, _Z_C],
                          out_specs=(_ROW, _VEC, _VEC, _HALF, _HALF, _ROW, _HALF, _HALF),
                          compiler_params=_cp())(dxo, y, gate, post_w, w, o_a, o_c, proj, proj)


def _dwout(o_a, yb, o_c, proj, dy, name):
    def body(oa_ref, yb_ref, oc_ref, za_ref, zc_ref, dy_ref, o_ref):
        @pl.when(pl.program_id(0) == 0)
        def _():
            o_ref[...] = jnp.zeros_like(o_ref)
        dy = dy_ref[...]
        o_ref[0:512, :] += _mm(oa_ref[...] * _silu(za_ref[...]), dy, TN)
        o_ref[512:1536, :] += _mm(yb_ref[...], dy, TN)
        o_ref[1536:2048, :] += _mm(oc_ref[...] * _silu(zc_ref[...]), dy, TN)

    return pl.pallas_call(body, name=name, out_shape=_sds((2048, D)), grid=(SEQ // TM,),
                          in_specs=[_HALF, _ROW, _HALF, _Z_A, _Z_C, _ROW], out_specs=_full((2048, D)),
                          compiler_params=_cp())(o_a, yb, o_c, proj, proj, dy)


def _dwin(h, pieces, name):
    n = len(pieces)
    widths = [p.shape[1] for p in pieces]
    half = NP // 2

    def body(*refs):
        h_ref, p_refs, o_ref = refs[0], refs[1:1 + n], refs[1 + n]

        @pl.when(pl.program_id(0) == 0)
        def _():
            o_ref[...] = jnp.zeros_like(o_ref)
        hv, c0 = h_ref[...], 0
        for p_ref, wd in zip(p_refs, widths):
            o_ref[:, c0:c0 + wd] += _mm(hv, p_ref[...], TN)
            c0 += wd

    return pl.pallas_call(body, name=name, out_shape=_sds((D, half)), grid=(SEQ // TM,),
                          in_specs=[_ROW] + [pl.BlockSpec((TM, wd), lambda k: (k, 0)) for wd in widths],
                          out_specs=_full((D, half)), compiler_params=_cp(56))(h, *pieces)


_TMH = 256


def _dh_bwd(pieces, w, x, pre_w, scale, dxo, name):
    n = len(pieces)
    widths = [p.shape[1] for p in pieces]

    def body(*refs):
        p_refs, (w_ref, x_ref, pw_ref, sc_ref, dxo_ref, dx_ref, dsh_ref, dsc_ref, dpw_ref) = refs[:n], refs[n:]

        @pl.when(pl.program_id(0) == 0)
        def _():
            for r in (dsh_ref, dsc_ref, dpw_ref):
                r[...] = jnp.zeros_like(r)
        dh, c0 = 0.0, 0
        for p_ref, wd in zip(p_refs, widths):
            dh = dh + _mm(p_ref[...], w_ref[:, c0:c0 + wd], NT)
            c0 += wd
        xv = x_ref[...]
        rstd = _rstd(xv)
        nrm = xv * rstd
        dsh_ref[...] += jnp.sum(dh, axis=0, keepdims=True)
        dsc_ref[...] += jnp.sum(dh * (nrm * pw_ref[...]), axis=0, keepdims=True)
        dhn = dh * (1.0 + sc_ref[...])
        dpw_ref[...] += jnp.sum(dhn * nrm, axis=0, keepdims=True)
        dx_ref[...] = _rms_bwd(dhn * pw_ref[...], nrm, rstd) + dxo_ref[...]

    row = pl.BlockSpec((_TMH, D), lambda i: (i, 0))
    return pl.pallas_call(body, name=name, out_shape=(_sds((SEQ, D)), _sds((1, D)), _sds((1, D)), _sds((1, D))),
                          grid=(SEQ // _TMH,),
                          in_specs=[pl.BlockSpec((_TMH, wd), lambda i: (i, 0)) for wd in widths]
                          + [pl.BlockSpec((None, D, NP), lambda i: (0, 0, 0)), row, _VEC, _VEC, row],
                          out_specs=(row, _VEC, _VEC, _VEC), compiler_params=_cp(56))(*pieces, w, x, pre_w, scale, dxo)


def _w_in_padded(land, name):
    rows = 128

    def body(l_ref, o_ref):
        o_ref[...] = _pad_cols(jnp.concatenate([l_ref[k] for k in range(4)], axis=1))

    return pl.pallas_call(body, name=name, out_shape=_sds((D, NP), land.dtype), grid=(D // rows,),
                          in_specs=[pl.BlockSpec((4, rows, SHARD_IN), lambda i: (0, i, 0))],
                          out_specs=pl.BlockSpec((rows, NP), lambda i: (i, 0)), compiler_params=_cp())(land)


def _grad_blocks(dwa, dwb, name):
    rows = 128

    def body(a_ref, b_ref, o_ref):
        g = _unpad_cols(jnp.concatenate([a_ref[...], b_ref[...]], axis=1))
        for k in range(4):
            o_ref[k] = g[:, SHARD_IN * k:SHARD_IN * (k + 1)].astype(o_ref.dtype)

    half = pl.BlockSpec((rows, NP // 2), lambda i: (i, 0))
    return pl.pallas_call(body, name=name, out_shape=_sds((4, D, SHARD_IN), jnp.bfloat16), grid=(D // rows,),
                          in_specs=[half, half], out_specs=pl.BlockSpec((4, rows, SHARD_IN), lambda i: (0, i, 0)),
                          compiler_params=_cp())(dwa, dwb)


def _loss_bwd(xf, tgt, name):
    def body(x_ref, t_ref, dx_ref, l_ref):
        @pl.when(pl.program_id(0) == 0)
        def _():
            l_ref[...] = jnp.zeros_like(l_ref)
        e = x_ref[...] - t_ref[...]
        dx_ref[...] = e * (1.0 / D)
        l_ref[...] += 0.5 * jnp.sum(jnp.mean(e * e, axis=1, keepdims=True), axis=0, keepdims=True)

    return pl.pallas_call(body, name=name, out_shape=(_sds((SEQ, D)), _sds((8, LANES))), grid=(SEQ // TM,),
                          in_specs=[_ROW, _ROW], out_specs=(_ROW, _full((8, LANES))), compiler_params=_cp())(xf, tgt)


def _mod_part(c_all, ada_w, ada_b, name):
    def body(c_ref, w_ref, b_ref, o_ref):
        o_ref[0] = _mm(_silu(c_ref[...]), w_ref[0]) + b_ref[0]

    return pl.pallas_call(body, name=name, out_shape=_sds((DEPTH, 8, 768)), grid=(DEPTH,),
                          in_specs=[_full((8, D)), pl.BlockSpec((1, D, 768), lambda i: (i, 0, 0)), pl.BlockSpec((1, 1, 768), lambda i: (i, 0, 0))],
                          out_specs=pl.BlockSpec((1, 8, 768), lambda i: (i, 0, 0)), compiler_params=_cp())(c_all, ada_w, ada_b)


def _ada_grad(c_t, dmod, name):
    def body(c_ref, d_ref, o_ref):
        ca = _silu(c_ref[...])
        dm = d_ref[0]
        acc = ca[:, 0:1] * dm[0:1, :]
        for s in range(1, 8):
            acc = acc + ca[:, s:s + 1] * dm[s:s + 1, :]
        o_ref[0] = acc

    return pl.pallas_call(body, name=name, out_shape=_sds((DEPTH, D, 768)), grid=(DEPTH,),
                          in_specs=[_full((D, LANES)), pl.BlockSpec((1, 8, 768), lambda i: (i, 0, 0))],
                          out_specs=pl.BlockSpec((1, D, 768), lambda i: (i, 0, 0)), compiler_params=_cp())(c_t, dmod)


def _pack(parts):
    flat = []
    for p in parts:
        f = p.reshape(-1)
        flat.append(jnp.pad(f, (0, (-f.size) % LANES)))
    v = jnp.concatenate(flat)
    return jnp.pad(v, (0, (-v.size) % (8 * LANES))).reshape(-1, LANES)


def _unpack(v, shapes):
    v = v.reshape(-1)
    out, off = [], 0
    for s in shapes:
        n = math.prod(s)
        out.append(v[off:off + n].reshape(s))
        off += n + (-n) % LANES
    return out


_GIVEN_DT, _GIVEN_C = 4608, 4624


def _pad_cols(w):
    return jnp.concatenate([w[..., :_GIVEN_DT], w[..., _GIVEN_C:], w[..., _GIVEN_DT:_GIVEN_C],
                            jnp.zeros(w.shape[:-1] + (NP - IN_COLS,), w.dtype)], axis=-1)


def _unpad_cols(w):
    return jnp.concatenate([w[..., :_GIVEN_DT], w[..., DTC:DTC + 16], w[..., _GIVEN_DT:DTC]], axis=-1)


def _pad_lanes(v):
    return jnp.pad(v, (0, LANES - v.shape[0])).reshape(1, LANES)


def _local_step(x2, tgt, mod, weights_of, grads_done, pre_w, post_w, conv_w, conv_b, dt_bias, a_log, d_skip, nw, sinks):
    saved = []
    xcur = x2
    for i in range(DEPTH):
        shift, scale, gate = mod[i:i + 1, :D], mod[i:i + 1, D:2 * D], mod[i:i + 1, 2 * D:]
        pw, qw = pre_w[i:i + 1], post_w[i:i + 1]
        w_p, w_o = weights_of(i, xcur)
        proj, h = _proj_fwd(xcur, pw, scale, shift, w_p, 0, "proj_fwd")
        o_a, lse_a = _attn_fwd(proj, QA // LANES, KA // LANES, VA // LANES, DILS, False, None, "attn_a_fwd")
        sink_x = jnp.repeat(sinks[i], HD).reshape(1, 512)
        o_c, lse_c = _attn_fwd(proj, QC // LANES, KC // LANES, VC // LANES, (1,), True, sink_x, "attn_c_fwd")
        cw, cb = conv_w[i], conv_b[i:i + 1]
        xbc_act = _conv_fwd(proj, cw, cb, "conv_fwd")
        ssd_p = (_pad_lanes(a_log[i]), _pad_lanes(dt_bias[i]), jnp.repeat(d_skip[i], HD).reshape(1, 1024), nw[i:i + 1])
        yb, hin = _ssd_fwd(xbc_act, proj, *ssd_p, "ssd_fwd")
        xnew, y = _out_fwd(o_a, yb, o_c, proj, w_o, 0, xcur, gate, qw, "out_fwd")
        saved.append((w_p, w_o, xcur, scale, gate, pw, qw, proj, h, o_a, lse_a, sink_x, o_c, lse_c, cw, cb, xbc_act, ssd_p, yb, hin, y))
        xcur = xnew
    dx, ltile = _loss_bwd(xcur, tgt, "loss")
    dmod, small = [None] * DEPTH, [None] * DEPTH
    for i in reversed(range(DEPTH)):
        w_p, w_o, xin, scale, gate, pw, qw, proj, h, o_a, lse_a, sink_x, o_c, lse_c, cw, cb, xbc_act, ssd_p, yb, hin, y = saved[i]
        dy, dgate, dpost, do_a, dz_a, dyb, do_c, dz_c = _dymix(dx, y, gate, qw, w_o, 0, o_a, o_c, proj, "dymix")
        dwo = _dwout(o_a, yb, o_c, proj, dy, "dwout")
        dq_a, dk_a, dv_a = _attn_bwd(proj, QA // LANES, KA // LANES, VA // LANES, do_a, o_a, lse_a, DILS, False, None, "attn_a_bwd")
        dq_c, dk_c, dv_c, dsk = _attn_bwd(proj, QC // LANES, KC // LANES, VC // LANES, do_c, o_c, lse_c, (1,), True, sink_x, "attn_c_bwd")
        dxbc_act, dz_b, ddt, dal16, ddtb, ddsk, dnw = _ssd_bwd(xbc_act, proj, hin, dyb, *ssd_p, "ssd_bwd")
        dxbc, dcw, dcb = _conv_bwd(proj, dxbc_act, cw, cb, "conv_bwd")
        half_a, half_b = [dq_a, dk_a, dv_a, dz_a, dz_b], [dxbc, dq_c, dz_c, dk_c, dv_c, ddt]
        sent = grads_done(i, _dwin(h, half_a, "dwin_a"), _dwin(h, half_b, "dwin_b"), dwo)
        dx, dshift, dscale, dpre = _dh_bwd(half_a + half_b, w_p, xin, pw, scale + sent[0, 0], dx, "dh_bwd")
        dmod[i] = jnp.concatenate([dshift, dscale, dgate], axis=1)
        small[i] = (dpre, dpost, dcw, dcb, ddtb[0, :16], dal16[0, :16], ddsk.reshape(16, HD).sum(axis=1), dnw, dsk[:, 0, ::HD].reshape(8))
    return ltile, dx, jnp.concatenate(dmod, axis=0), small


_SMALL = ((1, D), (1, D), (4, CONV_CH), (1, CONV_CH), (16,), (16,), (16,), (1, D), (8,))


def kernel(x, c, ada_w, ada_b, pre_norm_w, post_norm_w, w_in, conv_w, conv_b, dt_bias, a_log, d_skip, ssm_norm_w, sinks, w_out, loss_target, m_ada_w, m_ada_b, m_pre_norm_w, m_post_norm_w, m_w_in, m_conv_w, m_conv_b, m_dt_bias, m_a_log, m_d_skip, m_ssm_norm_w, m_sinks, m_w_out, v_ada_w, v_ada_b, v_pre_norm_w, v_post_norm_w, v_w_in, v_conv_w, v_conv_b, v_dt_bias, v_a_log, v_d_skip, v_ssm_norm_w, v_sinks, v_w_out):
    xi, yi, ci = lax.axis_index("x"), lax.axis_index("y"), lax.axis_index("c")
    chip = 2 * xi + yi
    me = 2 * chip + ci

    w_in_b = _cast_bf16(w_in, 512, "cast_w_in")
    w_out_b = _cast_bf16(w_out, 512, "cast_w_out")
    gathers = []
    for i in range(DEPTH):
        lands = [lax.dynamic_update_slice(lax.empty((4,) + a.shape[1:], a.dtype), a[i][None], (chip, 0, 0)) for a in (w_in_b, w_out_b)]
        gathers.append(_split_start(None, lands, f"gather_start{i}", "half" if i == 0 else "whole"))
    all_started = gathers[0][3] + gathers[1][3] + gathers[2][3] + gathers[3][3]

    def weights_of(i, after):
        send_sems, recv_sems, thru, _ = gathers[i]
        if i == 0:
            halves = _split_wait(send_sems, recv_sems, thru, 2, all_started + mod[:1, :LANES], "gather_wait0", "half")
            send_sems, recv_sems, thru, after = _split_start(None, halves, "share_start0", "sibling")
            g_in, g_out = _split_wait(send_sems, recv_sems, thru, 2, after, "share_wait0", "sibling")
        else:
            g_in, g_out = _split_wait(send_sems, recv_sems, thru, 2, after, f"gather_wait{i}")
        return _w_in_padded(g_in, "w_in_padded")[None], g_out.reshape(1, 2048, D)

    scatters = [None] * DEPTH

    def grads_done(i, dwa, dwb, dwo):
        blocks = [_grad_blocks(dwa, dwb, "grad_blocks"), _cast_bf16(dwo.reshape(4, 512, D), 512, "cast_dw_out")]
        scatters[i] = _split_start(blocks, [lax.empty(b.shape, b.dtype) for b in blocks], f"scatter_start{i}")
        return scatters[i][3]

    g0 = _allgather8(_pack([c, conv_w]), "gather_c")
    c_all = g0[:, :8, :].reshape(8, D)
    conv_w_full = jnp.concatenate([g0[2 * k, 8:56, :].reshape(DEPTH, 4, CONV_CH // 4) for k in range(4)], axis=-1)

    ada_b_mine = lax.dynamic_slice_in_dim(ada_b, 768 * chip, 768, axis=1).reshape(DEPTH, 1, 768)
    gm = _allgather8(_mod_part(c_all, ada_w, ada_b_mine, "mod_part").reshape(DEPTH * 8, 768), "gather_mod")
    gm = gm.reshape(4, 2, DEPTH, 8, 768)[:, 0]
    mod = lax.dynamic_index_in_dim(gm, me, axis=2, keepdims=False).transpose(1, 0, 2).reshape(DEPTH, 3 * D)

    ltile, dx, dmod, small = _local_step(x[0], loss_target[0], mod, weights_of, grads_done, pre_norm_w, post_norm_w, conv_w_full,
                                         conv_b, dt_bias, a_log, d_skip, ssm_norm_w, sinks)

    packed = _pack([dmod] + [g for layer in small for g in layer] + [ltile[0]])
    gs = _allgather8(packed, "gather_small")
    tot = _sum_blocks(gs[:, None], packed.shape[0], "sum_small")[0]
    parts = _unpack(tot, [(DEPTH, 3 * D)] + list(_SMALL) * DEPTH + [(LANES,)])
    g_ada_b, loss = parts[0], parts[-1][0]
    per_layer = [parts[1 + len(_SMALL) * i:1 + len(_SMALL) * (i + 1)] for i in range(DEPTH)]
    g_pre, g_post, g_cw, g_cb, g_dtb, g_al, g_dsk, g_nw, g_sk = [jnp.stack([per_layer[i][j] for i in range(DEPTH)]) for j in range(len(_SMALL))]
    g_pre, g_post, g_cb, g_nw = g_pre[:, 0], g_post[:, 0], g_cb[:, 0], g_nw[:, 0]
    g_cw = lax.dynamic_slice_in_dim(g_cw, (CONV_CH // 4) * chip, CONV_CH // 4, axis=2)

    dmod_all = gs[:, :(DEPTH * 3 * D) // LANES, :].reshape(8, DEPTH, 3 * D).transpose(1, 0, 2)
    dmod_mine = lax.dynamic_slice_in_dim(dmod_all, 768 * chip, 768, axis=2)
    c_t = jnp.pad(c_all.T, ((0, 0), (0, LANES - 8)))
    g_ada_w = _ada_grad(c_t, dmod_mine, "ada_grad")

    res = {}
    res["ada_w"] = _adamw(ada_w, [g_ada_w], m_ada_w, v_ada_w, 512, "adamw_ada_w")
    names = ["ada_b", "pre_norm_w", "post_norm_w", "conv_w", "conv_b", "dt_bias", "a_log", "d_skip", "ssm_norm_w", "sinks"]
    ws = [ada_b, pre_norm_w, post_norm_w, conv_w, conv_b, dt_bias, a_log, d_skip, ssm_norm_w, sinks]
    gsm = [g_ada_b, g_pre, g_post, g_cw, g_cb, g_dtb, g_al, g_dsk, g_nw, g_sk]
    ms = [m_ada_b, m_pre_norm_w, m_post_norm_w, m_conv_w, m_conv_b, m_dt_bias, m_a_log, m_d_skip, m_ssm_norm_w, m_sinks]
    vs = [v_ada_b, v_pre_norm_w, v_post_norm_w, v_conv_w, v_conv_b, v_dt_bias, v_a_log, v_d_skip, v_ssm_norm_w, v_sinks]
    pw_, pg_, pm_, pv_ = _pack(ws), _pack(gsm), _pack(ms), _pack(vs)
    small_out = _adamw(pw_[None], [pg_[None]], pm_[None], pv_[None], pw_.shape[0], "adamw_small")

    others_done = small_out[1][0, :8] + res["ada_w"][1][0, :8, :LANES]
    landed = [_split_wait(*scatters[i][:3], 2, others_done, f"scatter_wait{i}") for i in range(DEPTH)]
    p_in = _sum_chips([d[2] for d in landed], [d[0] for d in landed], 128, "sum_w_in")
    p_out = _sum_chips([d[3] for d in landed], [d[1] for d in landed], 256, "sum_w_out")
    col_major, row_major = (lambda a: jnp.transpose(a, (2, 0, 1))), (lambda a: jnp.transpose(a, (1, 2, 0)))
    p_in = col_major(p_in)
    s_in, s_out = _sibling_swap([p_in, p_out], "swap_partials")
    res["w_in"] = [row_major(a) for a in _adamw(col_major(w_in), [p_in, s_in], col_major(m_w_in), col_major(v_w_in), None,
                                                "adamw_w_in", lead=SHARD_IN // 18)]
    res["w_out"] = _adamw(w_out, [p_out, s_out], m_w_out, v_w_out, 512, "adamw_w_out")
    shapes = [w.shape for w in ws]
    for kind in range(4):
        for nm, a in zip(names, _unpack(small_out[kind][0], shapes)):
            res.setdefault(nm, [None] * 4)[kind] = a
    order = ["ada_w", "ada_b", "pre_norm_w", "post_norm_w", "w_in", "conv_w", "conv_b", "dt_bias", "a_log", "d_skip", "ssm_norm_w", "sinks", "w_out"]
    return (loss, dx[None], *[res[n][0] for n in order], *[res[n][1] for n in order], *[res[n][2] for n in order], *[res[n][3] for n in order])
```

```python
import math

import jax
import jax.numpy as jnp
from jax import lax
from jax.experimental import pallas as pl
from jax.experimental.pallas import tpu as pltpu

F32 = jnp.float32
MXU = jnp.bfloat16
HI = lax.Precision.HIGHEST
MESH = pl.DeviceIdType.MESH

SEQ = 4096
D = 1024
DEPTH = 4
HD = 64
QK_SCALE = HD ** -0.5
LANES = 128
BLK = 128
DILS = (1, 4, 16)
NEG = -1e30
EPS = 1e-6
MIB = 1024 * 1024

NP = 6144
QA, KA, VA, ZA = 0, 512, 1024, 1536
ZB, XBC = 2048, 3072
QC, ZC, KC, VC = 4608, 5120, 5632, 5760
DTC = 5888
IN_COLS = 5904
SHARD_IN = IN_COLS // 4
CONV_CH = 1536
TM = 512

ADAM_LR, ADAM_B1, ADAM_B2, ADAM_EPS, ADAM_WD, ADAM_STEP = 0.001, 0.9, 0.999, 1e-08, 0.01, 10

NT = (((1,), (1,)), ((), ()))
TN = (((0,), (0,)), ((), ()))


def _cp(vmem_mib=48):
    return pltpu.CompilerParams(vmem_limit_bytes=vmem_mib * MIB)


def _sds(shape, dtype=F32):
    return jax.ShapeDtypeStruct(shape, dtype)


def _full(shape):
    n = len(shape)
    return pl.BlockSpec(shape, lambda *_: (0,) * n)


def _mm(a, b, dims=None):
    if dims is None:
        return jnp.dot(a.astype(MXU), b.astype(MXU), preferred_element_type=F32)
    return lax.dot_general(a.astype(MXU), b.astype(MXU), dims, preferred_element_type=F32)


def _sigmoid(x):
    return 1.0 / (1.0 + jnp.exp(-x))


def _silu(x):
    return x * _sigmoid(x)


def _dsilu(x):
    s = _sigmoid(x)
    return s * (1.0 + x * (1.0 - s))


def _softplus(x):
    ax = jnp.where(x >= 0, x, -x)
    return jnp.maximum(x, 0.0) + jnp.log1p(jnp.exp(-ax))


def _half_masks():
    lane = lax.broadcasted_iota(jnp.int32, (1, LANES), 1)
    m0 = (lane < HD).astype(F32)
    return m0, 1.0 - m0


def _allgather8(v, name):
    r, cc = v.shape

    def body(v_ref, out_ref, send_sems, recv_sems):
        x, y, c = lax.axis_index("x"), lax.axis_index("y"), lax.axis_index("c")
        me = 4 * x + 2 * y + c
        out_ref[me] = v_ref[...]
        peers = []
        for k in range(1, 8):
            px = 1 - x if k & 4 else x
            py = 1 - y if k & 2 else y
            pc = 1 - c if k & 1 else c
            peers.append((px, py, pc))
        sends = []
        for k, peer in enumerate(peers):
            cp = pltpu.make_async_remote_copy(src_ref=v_ref, dst_ref=out_ref.at[me], send_sem=send_sems.at[k],
                                              recv_sem=recv_sems.at[k], device_id=peer, device_id_type=MESH)
            cp.start()
            sends.append(cp)
        for k, (px, py, pc) in enumerate(peers):
            pltpu.make_async_remote_copy(src_ref=v_ref, dst_ref=out_ref.at[4 * px + 2 * py + pc], send_sem=send_sems.at[k],
                                         recv_sem=recv_sems.at[k], device_id=(px, py, pc), device_id_type=MESH).wait_recv()
        for cp in sends:
            cp.wait_send()

    return pl.pallas_call(
        body, name=name, out_shape=_sds((8, r, cc)),
        in_specs=[pl.BlockSpec(memory_space=pltpu.VMEM)], out_specs=pl.BlockSpec(memory_space=pltpu.VMEM),
        scratch_shapes=[pltpu.SemaphoreType.DMA((7,)), pltpu.SemaphoreType.DMA((7,))],
        compiler_params=_cp(32),
    )(v)


_HBM = pl.BlockSpec(memory_space=pltpu.HBM)
_SEM = pl.BlockSpec(memory_space=pltpu.SEMAPHORE)
_EFFECT = pltpu.SideEffectType.DATAFLOW_SIDE_EFFECTING


def _chip_copies(src_refs, land_refs, send_sems, recv_sems, part="whole"):
    x, y, c = lax.axis_index("x"), lax.axis_index("y"), lax.axis_index("c")
    mine = 2 * x + y
    out = []
    for i, land in enumerate(land_refs):
        half = land.shape[1] // 2
        own, others = pl.ds(pl.multiple_of(c * half, half), half), pl.ds(pl.multiple_of((1 - c) * half, half), half)
        for j, (px, py) in enumerate([(1 - x, y), (x, 1 - y), (1 - x, 1 - y)]):
            slot, peer = 2 * px + py, (px, py, c)
            if part == "whole":
                src = src_refs[i].at[slot] if src_refs else land.at[mine]
                there, here = land.at[mine], land.at[slot]
            elif part == "half":
                src = there = land.at[mine].at[own]
                here = land.at[slot].at[own]
            else:
                src = there = land.at[slot].at[own]
                here, peer = land.at[slot].at[others], (x, y, 1 - c)
            mk = lambda dst, i=i, j=j, src=src, peer=peer: pltpu.make_async_remote_copy(
                src_ref=src, dst_ref=dst, send_sem=send_sems.at[3 * i + j], recv_sem=recv_sems.at[3 * i + j],
                device_id=peer, device_id_type=MESH)
            out.append((mk(there), mk(here)))
    return out


def _split_start(srcs, lands, name, part="whole"):
    ops = list(srcs or []) + list(lands)
    ns, n = len(srcs or []), len(lands)

    def body(*refs):
        src_refs, land_refs = refs[:ns], refs[ns:ns + n]
        send_sems, recv_sems = refs[ns + n], refs[ns + n + 1]
        for mine_out, _ in _chip_copies(src_refs, land_refs, send_sems, recv_sems, part):
            mine_out.start()
        refs[-1][...] = jnp.zeros_like(refs[-1])

    sems = pltpu.SemaphoreType.DMA((3 * n,))
    res = pl.pallas_call(
        body, name=name, out_shape=(sems, sems) + tuple(pltpu.HBM(a.shape, a.dtype) for a in ops) + (_sds((8, LANES)),),
        in_specs=[_HBM] * len(ops), out_specs=(_SEM, _SEM) + (_HBM,) * len(ops) + (pl.BlockSpec(memory_space=pltpu.VMEM),),
        input_output_aliases={k: 2 + k for k in range(len(ops))},
        compiler_params=pltpu.CompilerParams(has_side_effects=_EFFECT),
    )(*[pltpu.with_memory_space_constraint(a, pltpu.HBM) for a in ops])
    return res[0], res[1], list(res[2:2 + len(ops)]), res[-1]


def _split_wait(send_sems, recv_sems, thru, n, after, name, part="whole"):
    ns = len(thru) - n

    def body(*refs):
        src_refs, land_refs = refs[:ns], refs[ns:ns + n]
        for mine_out, arriving in _chip_copies(src_refs, land_refs, refs[ns + n], refs[ns + n + 1], part):
            mine_out.wait_send()
            arriving.wait_recv()

    res = pl.pallas_call(
        body, name=name, out_shape=tuple(pltpu.HBM(a.shape, a.dtype) for a in thru),
        in_specs=[_HBM] * len(thru) + [_SEM, _SEM, pl.BlockSpec(memory_space=pl.ANY)], out_specs=(_HBM,) * len(thru),
        input_output_aliases={k: k for k in range(len(thru))},
        compiler_params=pltpu.CompilerParams(has_side_effects=_EFFECT),
    )(*thru, send_sems, recv_sems, after)
    return list(res)


def _sibling_swap(arrs, name):
    n = len(arrs)

    def body(*refs):
        ins, outs_, (send_sems, recv_sems) = refs[:n], refs[n:2 * n], refs[2 * n:]
        sib = (lax.axis_index("x"), lax.axis_index("y"), 1 - lax.axis_index("c"))
        cps = [pltpu.make_async_remote_copy(src_ref=ins[i], dst_ref=outs_[i], send_sem=send_sems.at[i], recv_sem=recv_sems.at[i],
                                            device_id=sib, device_id_type=MESH) for i in range(n)]
        for cp in cps:
            cp.start()
        for cp in cps:
            cp.wait_recv()
        for cp in cps:
            cp.wait_send()

    hbm = pl.BlockSpec(memory_space=pltpu.HBM)
    return pl.pallas_call(
        body, name=name, out_shape=tuple(_sds(a.shape, a.dtype) for a in arrs), in_specs=[hbm] * n, out_specs=tuple([hbm] * n),
        scratch_shapes=[pltpu.SemaphoreType.DMA((n,)), pltpu.SemaphoreType.DMA((n,))],
    )(*arrs)


def _tile_spec(rows, cc):
    return pl.BlockSpec((None, rows, cc), lambda l, i: (l, i, 0))


def _cast_bf16(a, rows, name):
    nl, r, cc = a.shape

    def body(a_ref, o_ref):
        o_ref[...] = a_ref[...].astype(jnp.bfloat16)

    return pl.pallas_call(body, name=name, out_shape=_sds((nl, r, cc), jnp.bfloat16), grid=(nl, r // rows),
                          in_specs=[_tile_spec(rows, cc)], out_specs=_tile_spec(rows, cc), compiler_params=_cp())(a)


def _sum_blocks(a, rows, name):
    k, nl, r, cc = a.shape

    def body(a_ref, o_ref):
        acc = a_ref[0].astype(F32)
        for j in range(1, k):
            acc = acc + a_ref[j].astype(F32)
        o_ref[...] = acc

    return pl.pallas_call(body, name=name, out_shape=_sds((nl, r, cc)), grid=(nl, r // rows),
                          in_specs=[pl.BlockSpec((k, None, rows, cc), lambda l, i: (0, l, i, 0))],
                          out_specs=_tile_spec(rows, cc), compiler_params=_cp())(a)


def _sum_chips(lands, srcs, rows, name):
    nl = len(lands)
    _, r, cc = lands[0].shape

    def body(*refs):
        land_refs, src_refs, o_ref = refs[:nl], refs[nl:2 * nl], refs[2 * nl]
        mine = 2 * lax.axis_index("x") + lax.axis_index("y")
        for j in range(nl):
            @pl.when(pl.program_id(0) == j)
            def _(j=j):
                own = src_refs[j][mine].astype(F32)
                acc = None
                for k in range(4):
                    term = jnp.where(mine == k, own, land_refs[j][k].astype(F32))
                    acc = term if acc is None else acc + term
                o_ref[...] = acc.astype(o_ref.dtype)

    specs = [pl.BlockSpec((4, rows, cc), lambda l, i, j=j: (0, jnp.where(l == j, i, 0), 0)) for j in range(nl)]
    return pl.pallas_call(body, name=name, out_shape=_sds((nl, r, cc), jnp.bfloat16), grid=(nl, r // rows),
                          in_specs=specs + specs, out_specs=_tile_spec(rows, cc), compiler_params=_cp())(*lands, *srcs)


def _adamw(w, parts, m, v, rows, name, lead=None):
    nl, r, cc = w.shape
    np_ = len(parts)
    c1 = 1.0 / (1.0 - ADAM_B1 ** ADAM_STEP)
    c2 = 1.0 / (1.0 - ADAM_B2 ** ADAM_STEP)

    def body(*refs):
        w_ref, p_refs, (m_ref, v_ref, g_ref, d_ref, nm_ref, nv_ref) = refs[0], refs[1:1 + np_], refs[1 + np_:]
        g = p_refs[0][...].astype(F32)
        for p_ref in p_refs[1:]:
            g = g + p_ref[...].astype(F32)
        nm = ADAM_B1 * m_ref[...] + (1.0 - ADAM_B1) * g
        nv = ADAM_B2 * v_ref[...] + (1.0 - ADAM_B2) * (g * g)
        g_ref[...] = g
        nm_ref[...] = nm
        nv_ref[...] = nv
        d_ref[...] = -ADAM_LR * ((nm * c1) / (jnp.sqrt(nv * c2) + ADAM_EPS) + ADAM_WD * w_ref[...])

    if lead is None:
        spec, grid = _tile_spec(rows, cc), (nl, r // rows)
    else:
        spec, grid = pl.BlockSpec((lead, r, cc), lambda i: (i, 0, 0)), (nl // lead,)
    return pl.pallas_call(body, name=name, out_shape=(_sds((nl, r, cc)),) * 4, grid=grid,
                          in_specs=[spec] * (3 + np_), out_specs=(spec,) * 4, compiler_params=_cp())(w, *parts, m, v)


_BIAS = pltpu.VMEM((2, 2 * BLK, 2 * BLK), F32)


def _fill_band_bias(bias_ref):
    qi = lax.broadcasted_iota(jnp.int32, (2 * BLK, 2 * BLK), 0) & (BLK - 1)
    kj = lax.broadcasted_iota(jnp.int32, (2 * BLK, 2 * BLK), 1)
    dist = BLK + qi - kj
    band = (dist >= 0) & (dist <= BLK)
    bias_ref[0] = jnp.where(band, 0.0, NEG)
    bias_ref[1] = jnp.where(band & (kj >= BLK), 0.0, NEG)


class _HeadStack:
    def __init__(self, group):
        self.m0, self.m1 = _half_masks()
        self.group = group
        if group is not None:
            self.kv_mask = (self.m0, self.m1)[group]

    def _swap_half(self, t, a):
        return t if a == self.group else pltpu.roll(t, HD, axis=1)

    def stack(self, t):
        low = lax.broadcasted_iota(jnp.int32, (1, LANES), 1) < HD
        t0, t1 = jnp.where(low, t, 0.0), jnp.where(low, 0.0, t)
        if self.group is not None:
            t0, t1 = self._swap_half(t0, 0), self._swap_half(t1, 1)
        return jnp.concatenate([t0, t1], axis=0)

    def unstack(self, ts):
        if self.group is None:
            return jnp.where(lax.broadcasted_iota(jnp.int32, (1, LANES), 1) < HD, ts[:BLK], ts[BLK:])
        return self._swap_half(ts[:BLK] * self.kv_mask, 0) + self._swap_half(ts[BLK:] * self.kv_mask, 1)


def _rows(st, dil):
    if dil == 1:
        return pl.ds(pl.multiple_of(st, BLK), BLK)
    return pl.ds(st, BLK, stride=dil)


def _block_pos(n, dil):
    nb = SEQ // (dil * BLK)
    r, b = n // nb, n % nb
    hp = (b > 0).astype(jnp.int32)
    st = r + dil * BLK * b
    return st, st - dil * BLK * hp, 1 - hp


def _attn_fwd(proj, qblk, kblk, vblk, dils, gqa, sink_x, name):
    has_sink = sink_x is not None

    def body(*refs):
        if has_sink:
            q_ref, k_ref, v_ref, s_ref, o_ref, lse_ref, m_scr, z_scr, bias_scr = refs
        else:
            q_ref, k_ref, v_ref, o_ref, lse_ref, m_scr, z_scr, bias_scr = refs

        @pl.when(pl.program_id(0) == 0)
        def _():
            _fill_band_bias(bias_scr)
        o_ref[...] = jnp.zeros_like(o_ref)
        if has_sink:
            z_scr[...] = jnp.ones_like(z_scr)
            m_scr[...] = jnp.broadcast_to(s_ref[...], m_scr.shape)
        else:
            z_scr[...] = jnp.zeros_like(z_scr)
            m_scr[...] = jnp.full_like(m_scr, NEG)

        def step(n, carry, dil, heads):
            m0, m1 = heads.m0, heads.m1
            st, stp, first = _block_pos(n, dil)
            rq, rp = _rows(st, dil), _rows(stp, dil)
            kk = jnp.concatenate([k_ref[rp, :], k_ref[rq, :]], axis=0)
            vv = jnp.concatenate([v_ref[rp, :], v_ref[rq, :]], axis=0)
            s = _mm(heads.stack(q_ref[rq, :] * QK_SCALE), kk, NT) + bias_scr[first]
            m = jnp.max(s, axis=1, keepdims=True)
            p = jnp.exp(s - m)
            l = jnp.sum(p, axis=1, keepdims=True)
            o_pair = heads.unstack(_mm(p, vv))
            m_pair = m[:BLK] * m0 + m[BLK:] * m1
            l_pair = l[:BLK] * m0 + l[BLK:] * m1
            m_old = m_scr[rq, :]
            m_new = jnp.maximum(m_old, m_pair)
            alpha, beta = jnp.exp(m_old - m_new), jnp.exp(m_pair - m_new)
            o_ref[rq, :] = o_ref[rq, :] * alpha + o_pair * beta
            z_scr[rq, :] = z_scr[rq, :] * alpha + l_pair * beta
            m_scr[rq, :] = m_new
            return carry

        def blocks(heads):
            for dil in dils:
                lax.fori_loop(0, SEQ // BLK, lambda n, carry, dil=dil: step(n, carry, dil, heads), 0, unroll=8 if gqa else 16)

        if gqa:
            for grp in range(2):
                pl.when(pl.program_id(0) // 2 == grp)(lambda grp=grp: blocks(_HeadStack(grp)))
        else:
            blocks(_HeadStack(None))

        def fin(t, carry):
            rt = pl.ds(pl.multiple_of(t * TM, TM), TM)
            z = z_scr[rt, :]
            o_ref[rt, :] = o_ref[rt, :] / z
            lse_ref[rt, :] = m_scr[rt, :] + jnp.log(z)
            return carry
        lax.fori_loop(0, SEQ // TM, fin, 0)

    col = lambda blk: pl.BlockSpec((SEQ, LANES), lambda p, blk=blk: (0, blk + p))
    kv = (lambda blk: pl.BlockSpec((SEQ, LANES), lambda p, blk=blk: (0, blk))) if gqa else col
    in_specs = [col(qblk), kv(kblk), kv(vblk)]
    args = [proj, proj, proj]
    if has_sink:
        in_specs.append(pl.BlockSpec((1, LANES), lambda p: (0, p)))
        args.append(sink_x)
    out = pl.BlockSpec((SEQ, LANES), lambda p: (0, p))
    return pl.pallas_call(body, name=name, out_shape=(_sds((SEQ, 512)), _sds((SEQ, 512))), grid=(4,),
                          in_specs=in_specs, out_specs=(out, out),
                          scratch_shapes=[pltpu.VMEM((SEQ, LANES), F32), pltpu.VMEM((SEQ, LANES), F32), _BIAS],
                          compiler_params=_cp(48))(*args)


def _attn_bwd(proj, qblk, kblk, vblk, do, o, lse, dils, gqa, sink_x, name):
    has_sink = sink_x is not None

    def body(*refs):
        if has_sink:
            q_ref, k_ref, v_ref, do_ref, o_ref, lse_ref, s_ref, dq_ref, dk_ref, dv_ref, ds_ref, bias_scr = refs
        else:
            q_ref, k_ref, v_ref, do_ref, o_ref, lse_ref, dq_ref, dk_ref, dv_ref, bias_scr = refs
        pid = pl.program_id(0)

        @pl.when(pid == 0)
        def _():
            _fill_band_bias(bias_scr)
        dq_ref[...] = jnp.zeros_like(dq_ref)
        if gqa:
            @pl.when(pid == 0)
            def _():
                dk_ref[...] = jnp.zeros_like(dk_ref)
                dv_ref[...] = jnp.zeros_like(dv_ref)
        else:
            dk_ref[...] = jnp.zeros_like(dk_ref)
            dv_ref[...] = jnp.zeros_like(dv_ref)

        def step(n, carry, dil, heads):
            m0, m1 = heads.m0, heads.m1
            st, stp, first = _block_pos(n, dil)
            rq, rp = _rows(st, dil), _rows(stp, dil)
            do_, lse_ = do_ref[rq, :], lse_ref[rq, :]
            kk = jnp.concatenate([k_ref[rp, :], k_ref[rq, :]], axis=0)
            vv = jnp.concatenate([v_ref[rp, :], v_ref[rq, :]], axis=0)
            qs, dos = heads.stack(q_ref[rq, :] * QK_SCALE), heads.stack(do_)
            doo = do_ * o_ref[rq, :]
            delta = jnp.concatenate([jnp.sum(doo * m0, axis=1, keepdims=True), jnp.sum(doo * m1, axis=1, keepdims=True)], axis=0)
            lse_s = jnp.concatenate([lse_[:, 0:1], lse_[:, HD:HD + 1]], axis=0)
            p = jnp.exp(_mm(qs, kk, NT) + bias_scr[first] - lse_s)
            ds = p * (_mm(dos, vv, NT) - delta)
            dq_ref[rq, :] += heads.unstack(_mm(ds, kk)) * QK_SCALE
            dk_sum, dv_sum = _mm(ds, qs, TN), _mm(p, dos, TN)
            dk_ref[rp, :] += dk_sum[:BLK]
            dk_ref[rq, :] += dk_sum[BLK:]
            dv_ref[rp, :] += dv_sum[:BLK]
            dv_ref[rq, :] += dv_sum[BLK:]
            return carry

        def blocks(heads):
            for dil in dils:
                lax.fori_loop(0, SEQ // BLK, lambda n, carry, dil=dil: step(n, carry, dil, heads), 0, unroll=4)

        if gqa:
            for grp in range(2):
                pl.when(pid // 2 == grp)(lambda grp=grp: blocks(_HeadStack(grp)))
        else:
            blocks(_HeadStack(None))

        if has_sink:
            m0, m1 = _half_masks()

            def sink_rows(t, acc):
                rt = pl.ds(pl.multiple_of(t * TM, TM), TM)
                return acc - jnp.sum(jnp.exp(s_ref[...] - lse_ref[rt, :]) * (do_ref[rt, :] * o_ref[rt, :]), axis=0, keepdims=True)
            acc = lax.fori_loop(0, SEQ // TM, sink_rows, jnp.zeros((1, LANES), F32))
            per_head = jnp.sum(acc * m0, axis=1, keepdims=True) * m0 + jnp.sum(acc * m1, axis=1, keepdims=True) * m1
            ds_ref[0] = jnp.broadcast_to(per_head, (8, LANES))

    col = lambda blk: pl.BlockSpec((SEQ, LANES), lambda p, blk=blk: (0, blk + p))
    kv = (lambda blk: pl.BlockSpec((SEQ, LANES), lambda p, blk=blk: (0, blk))) if gqa else col
    pair = pl.BlockSpec((SEQ, LANES), lambda p: (0, p))
    in_specs = [col(qblk), kv(kblk), kv(vblk), pair, pair, pair]
    args = [proj, proj, proj, do, o, lse]
    kvw = LANES if gqa else 512
    kv_out = pl.BlockSpec((SEQ, LANES), lambda p: (0, 0)) if gqa else pair
    out_shape = [_sds((SEQ, 512)), _sds((SEQ, kvw)), _sds((SEQ, kvw))]
    out_specs = [pair, kv_out, kv_out]
    if has_sink:
        in_specs.append(pl.BlockSpec((1, LANES), lambda p: (0, p)))
        args.append(sink_x)
        out_shape.append(_sds((4, 8, LANES)))
        out_specs.append(pl.BlockSpec((1, 8, LANES), lambda p: (p, 0, 0)))
    return pl.pallas_call(body, name=name, out_shape=tuple(out_shape), grid=(4,), in_specs=in_specs,
                          out_specs=tuple(out_specs), scratch_shapes=[_BIAS], compiler_params=_cp(56))(*args)


_CT = 128


def _rows_before(x_ref, t, k):
    if t == 0:
        return jnp.concatenate([jnp.zeros((k, LANES), F32), x_ref[0:_CT - k, :]], axis=0)
    return x_ref[t * _CT - k:(t + 1) * _CT - k, :]


def _conv_pre(x_ref, w_ref, b_ref, t):
    taps = [x_ref[t * _CT:(t + 1) * _CT, :]] + [_rows_before(x_ref, t, k) for k in range(1, 4)]
    u = b_ref[...] + taps[0] * w_ref[3:4, :]
    for k in range(1, 4):
        u = u + taps[k] * w_ref[3 - k:4 - k, :]
    return u, taps


def _conv_fwd(proj, w, b, name):
    def body(x_ref, w_ref, b_ref, o_ref):
        for t in range(SEQ // _CT):
            o_ref[t * _CT:(t + 1) * _CT, :] = _silu(_conv_pre(x_ref, w_ref, b_ref, t)[0])

    nblk = CONV_CH // LANES
    return pl.pallas_call(body, name=name, out_shape=_sds((SEQ, CONV_CH)), grid=(nblk,),
                          in_specs=[pl.BlockSpec((SEQ, LANES), lambda j: (0, XBC // LANES + j)),
                                    pl.BlockSpec((4, LANES), lambda j: (0, j)), pl.BlockSpec((1, LANES), lambda j: (0, j))],
                          out_specs=pl.BlockSpec((SEQ, LANES), lambda j: (0, j)), compiler_params=_cp())(proj, w, b)


def _conv_bwd(proj, dact, w, b, name):
    def body(x_ref, da_ref, w_ref, b_ref, dx_ref, dw_ref, db_ref, du_scr):
        du_scr[SEQ:SEQ + 8, :] = jnp.zeros((8, LANES), F32)
        db = jnp.zeros((1, LANES), F32)
        dws = [jnp.zeros((1, LANES), F32)] * 4
        for t in range(SEQ // _CT):
            u, taps = _conv_pre(x_ref, w_ref, b_ref, t)
            du = da_ref[t * _CT:(t + 1) * _CT, :] * _dsilu(u)
            du_scr[t * _CT:(t + 1) * _CT, :] = du
            db = db + jnp.sum(du, axis=0, keepdims=True)
            dws = [dws[k] + jnp.sum(du * taps[k], axis=0, keepdims=True) for k in range(4)]
        db_ref[...] = db
        for k in range(4):
            dw_ref[3 - k:4 - k, :] = dws[k]
        for t in range(SEQ // _CT):
            dx = du_scr[t * _CT:(t + 1) * _CT, :] * w_ref[3:4, :]
            for k in range(1, 4):
                dx = dx + du_scr[t * _CT + k:(t + 1) * _CT + k, :] * w_ref[3 - k:4 - k, :]
            dx_ref[t * _CT:(t + 1) * _CT, :] = dx.astype(dx_ref.dtype)

    nblk = CONV_CH // LANES
    blk = pl.BlockSpec((SEQ, LANES), lambda j: (0, j))
    wspec, bspec = pl.BlockSpec((4, LANES), lambda j: (0, j)), pl.BlockSpec((1, LANES), lambda j: (0, j))
    return pl.pallas_call(body, name=name, out_shape=(_sds((SEQ, CONV_CH), MXU), _sds((4, CONV_CH)), _sds((1, CONV_CH))), grid=(nblk,),
                          in_specs=[pl.BlockSpec((SEQ, LANES), lambda j: (0, XBC // LANES + j)), blk, wspec, bspec],
                          out_specs=(blk, wspec, bspec), scratch_shapes=[pltpu.VMEM((SEQ + 8, LANES), F32)],
                          compiler_params=_cp())(proj, dact, w, b)


def _column(t, h):
    lane = lax.broadcasted_iota(jnp.int32, (1, LANES), 1)
    pick = jax.custom_vjp(lambda v: v[:, h:h + 1])
    pick.defvjp(lambda v: (v[:, h:h + 1], None), lambda _, g: (g * (lane == h).astype(F32),))
    return pick(t)


def _row(t, h):
    sub = lax.broadcasted_iota(jnp.int32, (BLK, 1), 0)
    pick = jax.custom_vjp(lambda v: v[h:h + 1, :])
    pick.defvjp(lambda v: (v[h:h + 1, :], None), lambda _, g: (g * (sub == h).astype(F32),))
    return pick(t)


def _ssd_chunk(xs, bm, cm, dtr, z, hs, al16, dtb, dskx, nw):
    m0, m1 = _half_masks()
    row = lax.broadcasted_iota(jnp.int32, (BLK, BLK), 0)
    col = lax.broadcasted_iota(jnp.int32, (BLK, BLK), 1)
    causal = row >= col
    tril = causal.astype(F32)
    lane = lax.broadcasted_iota(jnp.int32, (1, LANES), 1)
    dt = jnp.where(lane < 16, _softplus(dtr + dtb), 0.0)
    a16 = -jnp.exp(al16)
    acum = jnp.dot(tril, dt * a16, precision=HI, preferred_element_type=F32)
    acum_t = acum.T
    gmat = [_mm(cm[g], bm[g], NT) for g in range(2)]
    ys, hn = [], []
    for p in range(8):
        g = p // 4
        col_h = [_column(acum, 2 * p + a) for a in range(2)]
        dt_x = _column(dt, 2 * p) * m0 + _column(dt, 2 * p + 1) * m1
        ac_x = col_h[0] * m0 + col_h[1] * m1
        a_end = _row(ac_x, BLK - 1)
        xdt = xs[p] * dt_x
        y = _mm(cm[g], hs[p]) * jnp.exp(ac_x)
        for a, msk in enumerate((m0, m1)):
            decay = jnp.exp(jnp.where(causal, col_h[a] - _row(acum_t, 2 * p + a), NEG))
            y = y + _mm(gmat[g] * decay, xdt * msk)
        st = _mm(bm[g], xdt * jnp.exp(a_end - ac_x), TN)
        hn.append(hs[p] * jnp.exp(a_end) + st)
        y = y + dskx[p] * xs[p]
        ys.append(y * _silu(z[p]))
    out = []
    for g in range(2):
        ms = sum(jnp.sum(ys[p] * ys[p], axis=1, keepdims=True) for p in range(4 * g, 4 * g + 4)) * (1.0 / 512)
        rstd = lax.rsqrt(ms + EPS)
        out += [ys[p] * rstd * nw[p] for p in range(4 * g, 4 * g + 4)]
    return out, hn


def _tiles(ref, n, off=0, rows=slice(None)):
    return [ref[rows, off + LANES * p:off + LANES * (p + 1)] for p in range(n)]


def _ssd_load(xbc_ref, z_ref, dt_ref, rows):
    return (_tiles(xbc_ref, 8, 0, rows), _tiles(xbc_ref, 2, 1024, rows), _tiles(xbc_ref, 2, 1280, rows), dt_ref[rows, :],
            _tiles(z_ref, 8, 0, rows))


def _ssd_params(al16_ref, dtb_ref, dsk_ref, nw_ref):
    return al16_ref[...], dtb_ref[...], _tiles(dsk_ref, 8), _tiles(nw_ref, 8)


_NCH = SEQ // BLK
_PER_STEP = 2
_STEP_ROWS = _PER_STEP * BLK


def _ssd_param_specs():
    return [_full((1, LANES)), _full((1, LANES)), _full((1, 1024)), _full((1, 1024))]


def _ssd_fwd(xbc_act, proj, al16, dtb, dskx, nw, name):
    def body(xbc_ref, z_ref, dt_ref, al16_ref, dtb_ref, dsk_ref, nw_ref, y_ref, hin_ref, h_scr):
        @pl.when(pl.program_id(0) == 0)
        def _():
            h_scr[...] = jnp.zeros_like(h_scr)
        params = _ssd_params(al16_ref, dtb_ref, dsk_ref, nw_ref)
        hs = _tiles(h_scr, 8)
        for k in range(_PER_STEP):
            rows = slice(BLK * k, BLK * (k + 1))
            for p in range(8):
                hin_ref[k, :, LANES * p:LANES * (p + 1)] = hs[p]
            ys, hs = _ssd_chunk(*_ssd_load(xbc_ref, z_ref, dt_ref, rows), hs, *params)
            for p in range(8):
                y_ref[rows, LANES * p:LANES * (p + 1)] = ys[p].astype(y_ref.dtype)
        for p in range(8):
            h_scr[:, LANES * p:LANES * (p + 1)] = hs[p]

    return pl.pallas_call(
        body, name=name, out_shape=(_sds((SEQ, 1024), MXU), _sds((_NCH, BLK, 1024))), grid=(_NCH // _PER_STEP,),
        in_specs=[pl.BlockSpec((_STEP_ROWS, CONV_CH), lambda c: (c, 0)), pl.BlockSpec((_STEP_ROWS, 1024), lambda c: (c, ZB // 1024)),
                  pl.BlockSpec((_STEP_ROWS, LANES), lambda c: (c, DTC // LANES))] + _ssd_param_specs(),
        out_specs=(pl.BlockSpec((_STEP_ROWS, 1024), lambda c: (c, 0)), pl.BlockSpec((_PER_STEP, BLK, 1024), lambda c: (c, 0, 0))),
        scratch_shapes=[pltpu.VMEM((BLK, 1024), F32)], compiler_params=_cp())(xbc_act, proj, proj, al16, dtb, dskx, nw)


def _ssd_bwd(xbc_act, proj, hin, dyb, al16, dtb, dskx, nw, name):
    def body(xbc_ref, z_ref, dt_ref, hin_ref, dy_ref, al16_ref, dtb_ref, dsk_ref, nw_ref,
             dxbc_ref, dz_ref, ddt_ref, dal16_ref, ddtb_ref, ddsk_ref, dnw_ref, dh_scr):
        @pl.when(pl.program_id(0) == 0)
        def _():
            dh_scr[...] = jnp.zeros_like(dh_scr)
            for r in (dal16_ref, ddtb_ref, ddsk_ref, dnw_ref):
                r[...] = jnp.zeros_like(r)
        params = _ssd_params(al16_ref, dtb_ref, dsk_ref, nw_ref)
        dhs = _tiles(dh_scr, 8)
        for k in reversed(range(_PER_STEP)):
            rows = slice(BLK * k, BLK * (k + 1))
            hs = [hin_ref[k, :, LANES * p:LANES * (p + 1)] for p in range(8)]
            _, vjp = jax.vjp(lambda a, h, q: _ssd_chunk(*a, h, *q), _ssd_load(xbc_ref, z_ref, dt_ref, rows), hs, params)
            (dxs, dbm, dcm, ddt, dz), dhs, (dal16, ddtb, ddsk, dnw) = vjp((_tiles(dy_ref, 8, 0, rows), dhs))
            for p in range(8):
                cols = slice(LANES * p, LANES * (p + 1))
                dxbc_ref[rows, cols] = dxs[p]
                dz_ref[rows, cols] = dz[p].astype(dz_ref.dtype)
                ddsk_ref[:, cols] += ddsk[p]
                dnw_ref[:, cols] += dnw[p]
            for g in range(2):
                dxbc_ref[rows, 1024 + LANES * g:1024 + LANES * (g + 1)] = dbm[g]
                dxbc_ref[rows, 1280 + LANES * g:1280 + LANES * (g + 1)] = dcm[g]
            ddt_ref[rows, :] = ddt.astype(ddt_ref.dtype)
            dal16_ref[...] += dal16
            ddtb_ref[...] += ddtb
        for p in range(8):
            dh_scr[:, LANES * p:LANES * (p + 1)] = dhs[p]

    rev = lambda c: _NCH // _PER_STEP - 1 - c
    return pl.pallas_call(
        body, name=name,
        out_shape=(_sds((SEQ, CONV_CH)), _sds((SEQ, 1024), MXU), _sds((SEQ, LANES), MXU),
                   _sds((1, LANES)), _sds((1, LANES)), _sds((1, 1024)), _sds((1, 1024))),
        grid=(_NCH // _PER_STEP,),
        in_specs=[pl.BlockSpec((_STEP_ROWS, CONV_CH), lambda c: (rev(c), 0)), pl.BlockSpec((_STEP_ROWS, 1024), lambda c: (rev(c), ZB // 1024)),
                  pl.BlockSpec((_STEP_ROWS, LANES), lambda c: (rev(c), DTC // LANES)),
                  pl.BlockSpec((_PER_STEP, BLK, 1024), lambda c: (rev(c), 0, 0)),
                  pl.BlockSpec((_STEP_ROWS, 1024), lambda c: (rev(c), 0))] + _ssd_param_specs(),
        out_specs=(pl.BlockSpec((_STEP_ROWS, CONV_CH), lambda c: (rev(c), 0)), pl.BlockSpec((_STEP_ROWS, 1024), lambda c: (rev(c), 0)),
                   pl.BlockSpec((_STEP_ROWS, LANES), lambda c: (rev(c), 0)),
                   _full((1, LANES)), _full((1, LANES)), _full((1, 1024)), _full((1, 1024))),
        scratch_shapes=[pltpu.VMEM((BLK, 1024), F32)], compiler_params=_cp())(xbc_act, proj, proj, hin, dyb, al16, dtb, dskx, nw)


def _rstd(v):
    return lax.rsqrt(jnp.mean(v * v, axis=1, keepdims=True) + EPS)


def _rms_bwd(dn, n, rstd):
    return rstd * (dn - n * jnp.mean(dn * n, axis=1, keepdims=True))


_VEC = _full((1, D))


def _layer_spec(layer):
    return pl.BlockSpec((None, 2048, D), lambda *_: (layer, 0, 0))

_ROW = pl.BlockSpec((TM, D), lambda i, *_: (i, 0))


def _proj_fwd(x, pre_w, scale, shift, w, layer, name):
    tn, ni = 1024, SEQ // TM

    def body(x_ref, pw_ref, sc_ref, sh_ref, w_ref, o_ref, h_ref, h_scr):
        rows = pl.ds(pl.multiple_of(pl.program_id(1) * TM, TM), TM)

        @pl.when(pl.program_id(0) == 0)
        def _():
            xv = x_ref[...]
            h = ((xv * _rstd(xv) * pw_ref[...]) * (1.0 + sc_ref[...]) + sh_ref[...]).astype(h_ref.dtype)
            h_scr[rows, :] = h
            h_ref[...] = h
        o_ref[...] = jnp.dot(h_scr[rows, :], w_ref[...].astype(MXU), preferred_element_type=F32)

    first_pass = pl.BlockSpec((TM, D), lambda j, i: (jnp.where(j == 0, i, ni - 1), 0))
    return pl.pallas_call(body, name=name, out_shape=(_sds((SEQ, NP)), _sds((SEQ, D), MXU)), grid=(NP // tn, ni),
                          in_specs=[first_pass, _VEC, _VEC, _VEC, pl.BlockSpec((None, D, tn), lambda j, i: (layer, 0, j))],
                          out_specs=(pl.BlockSpec((TM, tn), lambda j, i: (i, j)), first_pass),
                          scratch_shapes=[pltpu.VMEM((SEQ, D), MXU)], compiler_params=_cp())(x, pre_w, scale, shift, w)


_HALF = pl.BlockSpec((TM, 512), lambda i: (i, 0))
_Z_A = pl.BlockSpec((TM, 512), lambda i: (i, ZA // 512))
_Z_C = pl.BlockSpec((TM, 512), lambda i: (i, ZC // 512))


def _out_fwd(o_a, yb, o_c, proj, w, layer, x, gate, post_w, name):
    def body(oa_ref, yb_ref, oc_ref, za_ref, zc_ref, w_ref, x_ref, g_ref, pw_ref, xn_ref, y_ref):
        y = (_mm(oa_ref[...] * _silu(za_ref[...]), w_ref[0:512, :]) + _mm(yb_ref[...], w_ref[512:1536, :])
             + _mm(oc_ref[...] * _silu(zc_ref[...]), w_ref[1536:2048, :]))
        y_ref[...] = y
        xn_ref[...] = x_ref[...] + g_ref[...] * (y * _rstd(y) * pw_ref[...])

    return pl.pallas_call(body, name=name, out_shape=(_sds((SEQ, D)), _sds((SEQ, D))), grid=(SEQ // TM,),
                          in_specs=[_HALF, _ROW, _HALF, _Z_A, _Z_C, _layer_spec(layer), _ROW, _VEC, _VEC],
                          out_specs=(_ROW, _ROW), compiler_params=_cp())(o_a, yb, o_c, proj, proj, w, x, gate, post_w)


def _dymix(dxo, y, gate, post_w, w, layer, o_a, o_c, proj, name):
    def body(dx_ref, y_ref, g_ref, pw_ref, w_ref, oa_ref, oc_ref, za_ref, zc_ref,
             dy_ref, dg_ref, dpw_ref, doa_ref, dza_ref, b_ref, doc_ref, dzc_ref):
        @pl.when(pl.program_id(0) == 0)
        def _():
            dg_ref[...] = jnp.zeros_like(dg_ref)
            dpw_ref[...] = jnp.zeros_like(dpw_ref)
        dx, yv = dx_ref[...], y_ref[...]
        rstd = _rstd(yv)
        n = yv * rstd
        dg_ref[...] += jnp.sum(dx * (n * pw_ref[...]), axis=0, keepdims=True)
        dr = dx * g_ref[...]
        dpw_ref[...] += jnp.sum(dr * n, axis=0, keepdims=True)
        dy = _rms_bwd(dr * pw_ref[...], n, rstd)
        dy_ref[...] = dy
        b_ref[...] = _mm(dy, w_ref[512:1536, :], NT)
        for rows, o_ref, z_ref, do_ref, dz_ref in ((slice(0, 512), oa_ref, za_ref, doa_ref, dza_ref),
                                                   (slice(1536, 2048), oc_ref, zc_ref, doc_ref, dzc_ref)):
            dyg, z = _mm(dy, w_ref[rows, :], NT), z_ref[...]
            do_ref[...] = dyg * _silu(z)
            dz_ref[...] = (dyg * o_ref[...] * _dsilu(z)).astype(dz_ref.dtype)

    return pl.pallas_call(body, name=name,
                          out_shape=(_sds((SEQ, D)), _sds((1, D)), _sds((1, D)),
                                     _sds((SEQ, 512)), _sds((SEQ, 512), MXU), _sds((SEQ, D)), _sds((SEQ, 512)), _sds((SEQ, 512), MXU)),
                          grid=(SEQ // TM,), in_specs=[_ROW, _ROW, _VEC, _VEC, _layer_spec(layer), _HALF, _HALF, _Z_A, _Z_C],
                          out_specs=(_ROW, _VEC, _VEC, _HALF, _HALF, _ROW, _HALF, _HALF),
                          compiler_params=_cp())(dxo, y, gate, post_w, w, o_a, o_c, proj, proj)


def _dwout(o_a, yb, o_c, proj, dy, name):
    def body(oa_ref, yb_ref, oc_ref, za_ref, zc_ref, dy_ref, o_ref):
        @pl.when(pl.program_id(0) == 0)
        def _():
            o_ref[...] = jnp.zeros_like(o_ref)
        dy = dy_ref[...]
        o_ref[0:512, :] += _mm(oa_ref[...] * _silu(za_ref[...]), dy, TN)
        o_ref[512:1536, :] += _mm(yb_ref[...], dy, TN)
        o_ref[1536:2048, :] += _mm(oc_ref[...] * _silu(zc_ref[...]), dy, TN)

    return pl.pallas_call(body, name=name, out_shape=_sds((2048, D)), grid=(SEQ // TM,),
                          in_specs=[_HALF, _ROW, _HALF, _Z_A, _Z_C, _ROW], out_specs=_full((2048, D)),
                          compiler_params=_cp())(o_a, yb, o_c, proj, proj, dy)


def _dwin(h, pieces, name):
    n = len(pieces)
    widths = [p.shape[1] for p in pieces]
    half = NP // 2

    def body(*refs):
        h_ref, p_refs, o_ref = refs[0], refs[1:1 + n], refs[1 + n]

        @pl.when(pl.program_id(0) == 0)
        def _():
            o_ref[...] = jnp.zeros_like(o_ref)
        hv, c0 = h_ref[...], 0
        for p_ref, wd in zip(p_refs, widths):
            o_ref[:, c0:c0 + wd] += _mm(hv, p_ref[...], TN)
            c0 += wd

    return pl.pallas_call(body, name=name, out_shape=_sds((D, half)), grid=(SEQ // TM,),
                          in_specs=[_ROW] + [pl.BlockSpec((TM, wd), lambda k: (k, 0)) for wd in widths],
                          out_specs=_full((D, half)), compiler_params=_cp(56))(h, *pieces)


_TMH = 256


def _dh_bwd(pieces, w, x, pre_w, scale, dxo, name):
    n = len(pieces)
    widths = [p.shape[1] for p in pieces]

    def body(*refs):
        p_refs, (w_ref, x_ref, pw_ref, sc_ref, dxo_ref, dx_ref, dsh_ref, dsc_ref, dpw_ref) = refs[:n], refs[n:]

        @pl.when(pl.program_id(0) == 0)
        def _():
            for r in (dsh_ref, dsc_ref, dpw_ref):
                r[...] = jnp.zeros_like(r)
        dh, c0 = 0.0, 0
        for p_ref, wd in zip(p_refs, widths):
            dh = dh + _mm(p_ref[...], w_ref[:, c0:c0 + wd], NT)
            c0 += wd
        xv = x_ref[...]
        rstd = _rstd(xv)
        nrm = xv * rstd
        dsh_ref[...] += jnp.sum(dh, axis=0, keepdims=True)
        dsc_ref[...] += jnp.sum(dh * (nrm * pw_ref[...]), axis=0, keepdims=True)
        dhn = dh * (1.0 + sc_ref[...])
        dpw_ref[...] += jnp.sum(dhn * nrm, axis=0, keepdims=True)
        dx_ref[...] = _rms_bwd(dhn * pw_ref[...], nrm, rstd) + dxo_ref[...]

    row = pl.BlockSpec((_TMH, D), lambda i: (i, 0))
    return pl.pallas_call(body, name=name, out_shape=(_sds((SEQ, D)), _sds((1, D)), _sds((1, D)), _sds((1, D))),
                          grid=(SEQ // _TMH,),
                          in_specs=[pl.BlockSpec((_TMH, wd), lambda i: (i, 0)) for wd in widths]
                          + [pl.BlockSpec((None, D, NP), lambda i: (0, 0, 0)), row, _VEC, _VEC, row],
                          out_specs=(row, _VEC, _VEC, _VEC), compiler_params=_cp(56))(*pieces, w, x, pre_w, scale, dxo)


def _w_in_padded(land, name):
    rows = 128

    def body(l_ref, o_ref):
        o_ref[...] = _pad_cols(jnp.concatenate([l_ref[k] for k in range(4)], axis=1))

    return pl.pallas_call(body, name=name, out_shape=_sds((D, NP), land.dtype), grid=(D // rows,),
                          in_specs=[pl.BlockSpec((4, rows, SHARD_IN), lambda i: (0, i, 0))],
                          out_specs=pl.BlockSpec((rows, NP), lambda i: (i, 0)), compiler_params=_cp())(land)


def _grad_blocks(dwa, dwb, name):
    rows = 128

    def body(a_ref, b_ref, o_ref):
        g = _unpad_cols(jnp.concatenate([a_ref[...], b_ref[...]], axis=1))
        for k in range(4):
            o_ref[k] = g[:, SHARD_IN * k:SHARD_IN * (k + 1)].astype(o_ref.dtype)

    half = pl.BlockSpec((rows, NP // 2), lambda i: (i, 0))
    return pl.pallas_call(body, name=name, out_shape=_sds((4, D, SHARD_IN), jnp.bfloat16), grid=(D // rows,),
                          in_specs=[half, half], out_specs=pl.BlockSpec((4, rows, SHARD_IN), lambda i: (0, i, 0)),
                          compiler_params=_cp())(dwa, dwb)


def _loss_bwd(xf, tgt, name):
    def body(x_ref, t_ref, dx_ref, l_ref):
        @pl.when(pl.program_id(0) == 0)
        def _():
            l_ref[...] = jnp.zeros_like(l_ref)
        e = x_ref[...] - t_ref[...]
        dx_ref[...] = e * (1.0 / D)
        l_ref[...] += 0.5 * jnp.sum(jnp.mean(e * e, axis=1, keepdims=True), axis=0, keepdims=True)

    return pl.pallas_call(body, name=name, out_shape=(_sds((SEQ, D)), _sds((8, LANES))), grid=(SEQ // TM,),
                          in_specs=[_ROW, _ROW], out_specs=(_ROW, _full((8, LANES))), compiler_params=_cp())(xf, tgt)


def _mod_part(c_all, ada_w, ada_b, name):
    def body(c_ref, w_ref, b_ref, o_ref):
        o_ref[0] = _mm(_silu(c_ref[...]), w_ref[0]) + b_ref[0]

    return pl.pallas_call(body, name=name, out_shape=_sds((DEPTH, 8, 768)), grid=(DEPTH,),
                          in_specs=[_full((8, D)), pl.BlockSpec((1, D, 768), lambda i: (i, 0, 0)), pl.BlockSpec((1, 1, 768), lambda i: (i, 0, 0))],
                          out_specs=pl.BlockSpec((1, 8, 768), lambda i: (i, 0, 0)), compiler_params=_cp())(c_all, ada_w, ada_b)


def _ada_grad(c_t, dmod, name):
    def body(c_ref, d_ref, o_ref):
        ca = _silu(c_ref[...])
        dm = d_ref[0]
        acc = ca[:, 0:1] * dm[0:1, :]
        for s in range(1, 8):
            acc = acc + ca[:, s:s + 1] * dm[s:s + 1, :]
        o_ref[0] = acc

    return pl.pallas_call(body, name=name, out_shape=_sds((DEPTH, D, 768)), grid=(DEPTH,),
                          in_specs=[_full((D, LANES)), pl.BlockSpec((1, 8, 768), lambda i: (i, 0, 0))],
                          out_specs=pl.BlockSpec((1, D, 768), lambda i: (i, 0, 0)), compiler_params=_cp())(c_t, dmod)


def _pack(parts):
    flat = []
    for p in parts:
        f = p.reshape(-1)
        flat.append(jnp.pad(f, (0, (-f.size) % LANES)))
    v = jnp.concatenate(flat)
    return jnp.pad(v, (0, (-v.size) % (8 * LANES))).reshape(-1, LANES)


def _unpack(v, shapes):
    v = v.reshape(-1)
    out, off = [], 0
    for s in shapes:
        n = math.prod(s)
        out.append(v[off:off + n].reshape(s))
        off += n + (-n) % LANES
    return out


_GIVEN_DT, _GIVEN_C = 4608, 4624


def _pad_cols(w):
    return jnp.concatenate([w[..., :_GIVEN_DT], w[..., _GIVEN_C:], w[..., _GIVEN_DT:_GIVEN_C],
                            jnp.zeros(w.shape[:-1] + (NP - IN_COLS,), w.dtype)], axis=-1)


def _unpad_cols(w):
    return jnp.concatenate([w[..., :_GIVEN_DT], w[..., DTC:DTC + 16], w[..., _GIVEN_DT:DTC]], axis=-1)


def _pad_lanes(v):
    return jnp.pad(v, (0, LANES - v.shape[0])).reshape(1, LANES)


def _local_step(x2, tgt, mod, weights_of, grads_done, pre_w, post_w, conv_w, conv_b, dt_bias, a_log, d_skip, nw, sinks):
    saved = []
    xcur = x2
    for i in range(DEPTH):
        shift, scale, gate = mod[i:i + 1, :D], mod[i:i + 1, D:2 * D], mod[i:i + 1, 2 * D:]
        pw, qw = pre_w[i:i + 1], post_w[i:i + 1]
        w_p, w_o = weights_of(i, xcur)
        proj, h = _proj_fwd(xcur, pw, scale, shift, w_p, 0, "proj_fwd")
        o_a, lse_a = _attn_fwd(proj, QA // LANES, KA // LANES, VA // LANES, DILS, False, None, "attn_a_fwd")
        sink_x = jnp.repeat(sinks[i], HD).reshape(1, 512)
        o_c, lse_c = _attn_fwd(proj, QC // LANES, KC // LANES, VC // LANES, (1,), True, sink_x, "attn_c_fwd")
        cw, cb = conv_w[i], conv_b[i:i + 1]
        xbc_act = _conv_fwd(proj, cw, cb, "conv_fwd")
        ssd_p = (_pad_lanes(a_log[i]), _pad_lanes(dt_bias[i]), jnp.repeat(d_skip[i], HD).reshape(1, 1024), nw[i:i + 1])
        yb, hin = _ssd_fwd(xbc_act, proj, *ssd_p, "ssd_fwd")
        xnew, y = _out_fwd(o_a, yb, o_c, proj, w_o, 0, xcur, gate, qw, "out_fwd")
        saved.append((w_p, w_o, xcur, scale, gate, pw, qw, proj, h, o_a, lse_a, sink_x, o_c, lse_c, cw, cb, xbc_act, ssd_p, yb, hin, y))
        xcur = xnew
    dx, ltile = _loss_bwd(xcur, tgt, "loss")
    dmod, small = [None] * DEPTH, [None] * DEPTH
    for i in reversed(range(DEPTH)):
        w_p, w_o, xin, scale, gate, pw, qw, proj, h, o_a, lse_a, sink_x, o_c, lse_c, cw, cb, xbc_act, ssd_p, yb, hin, y = saved[i]
        dy, dgate, dpost, do_a, dz_a, dyb, do_c, dz_c = _dymix(dx, y, gate, qw, w_o, 0, o_a, o_c, proj, "dymix")
        dwo = _dwout(o_a, yb, o_c, proj, dy, "dwout")
        dq_a, dk_a, dv_a = _attn_bwd(proj, QA // LANES, KA // LANES, VA // LANES, do_a, o_a, lse_a, DILS, False, None, "attn_a_bwd")
        dq_c, dk_c, dv_c, dsk = _attn_bwd(proj, QC // LANES, KC // LANES, VC // LANES, do_c, o_c, lse_c, (1,), True, sink_x, "attn_c_bwd")
        dxbc_act, dz_b, ddt, dal16, ddtb, ddsk, dnw = _ssd_bwd(xbc_act, proj, hin, dyb, *ssd_p, "ssd_bwd")
        dxbc, dcw, dcb = _conv_bwd(proj, dxbc_act, cw, cb, "conv_bwd")
        half_a, half_b = [dq_a, dk_a, dv_a, dz_a, dz_b], [dxbc, dq_c, dz_c, dk_c, dv_c, ddt]
        sent = grads_done(i, _dwin(h, half_a, "dwin_a"), _dwin(h, half_b, "dwin_b"), dwo)
        dx, dshift, dscale, dpre = _dh_bwd(half_a + half_b, w_p, xin, pw, scale + sent[0, 0], dx, "dh_bwd")
        dmod[i] = jnp.concatenate([dshift, dscale, dgate], axis=1)
        small[i] = (dpre, dpost, dcw, dcb, ddtb[0, :16], dal16[0, :16], ddsk.reshape(16, HD).sum(axis=1), dnw, dsk[:, 0, ::HD].reshape(8))
    return ltile, dx, jnp.concatenate(dmod, axis=0), small


_SMALL = ((1, D), (1, D), (4, CONV_CH), (1, CONV_CH), (16,), (16,), (16,), (1, D), (8,))


def kernel(x, c, ada_w, ada_b, pre_norm_w, post_norm_w, w_in, conv_w, conv_b, dt_bias, a_log, d_skip, ssm_norm_w, sinks, w_out, loss_target, m_ada_w, m_ada_b, m_pre_norm_w, m_post_norm_w, m_w_in, m_conv_w, m_conv_b, m_dt_bias, m_a_log, m_d_skip, m_ssm_norm_w, m_sinks, m_w_out, v_ada_w, v_ada_b, v_pre_norm_w, v_post_norm_w, v_w_in, v_conv_w, v_conv_b, v_dt_bias, v_a_log, v_d_skip, v_ssm_norm_w, v_sinks, v_w_out):
    xi, yi, ci = lax.axis_index("x"), lax.axis_index("y"), lax.axis_index("c")
    chip = 2 * xi + yi
    me = 2 * chip + ci

    w_in_b = _cast_bf16(w_in, 512, "cast_w_in")
    w_out_b = _cast_bf16(w_out, 512, "cast_w_out")
    gathers = []
    for i in range(DEPTH):
        lands = [lax.dynamic_update_slice(lax.empty((4,) + a.shape[1:], a.dtype), a[i][None], (chip, 0, 0)) for a in (w_in_b, w_out_b)]
        gathers.append(_split_start(None, lands, f"gather_start{i}", "half" if i == 0 else "whole"))
    all_started = gathers[0][3] + gathers[1][3] + gathers[2][3] + gathers[3][3]

    def weights_of(i, after):
        send_sems, recv_sems, thru, _ = gathers[i]
        if i == 0:
            halves = _split_wait(send_sems, recv_sems, thru, 2, all_started + mod[:1, :LANES], "gather_wait0", "half")
            send_sems, recv_sems, thru, after = _split_start(None, halves, "share_start0", "sibling")
            g_in, g_out = _split_wait(send_sems, recv_sems, thru, 2, after, "share_wait0", "sibling")
        else:
            g_in, g_out = _split_wait(send_sems, recv_sems, thru, 2, after, f"gather_wait{i}")
        return _w_in_padded(g_in, "w_in_padded")[None], g_out.reshape(1, 2048, D)

    scatters = [None] * DEPTH

    def grads_done(i, dwa, dwb, dwo):
        blocks = [_grad_blocks(dwa, dwb, "grad_blocks"), _cast_bf16(dwo.reshape(4, 512, D), 512, "cast_dw_out")]
        scatters[i] = _split_start(blocks, [lax.empty(b.shape, b.dtype) for b in blocks], f"scatter_start{i}")
        return scatters[i][3]

    g0 = _allgather8(_pack([c, conv_w]), "gather_c")
    c_all = g0[:, :8, :].reshape(8, D)
    conv_w_full = jnp.concatenate([g0[2 * k, 8:56, :].reshape(DEPTH, 4, CONV_CH // 4) for k in range(4)], axis=-1)

    ada_b_mine = lax.dynamic_slice_in_dim(ada_b, 768 * chip, 768, axis=1).reshape(DEPTH, 1, 768)
    gm = _allgather8(_mod_part(c_all, ada_w, ada_b_mine, "mod_part").reshape(DEPTH * 8, 768), "gather_mod")
    gm = gm.reshape(4, 2, DEPTH, 8, 768)[:, 0]
    mod = lax.dynamic_index_in_dim(gm, me, axis=2, keepdims=False).transpose(1, 0, 2).reshape(DEPTH, 3 * D)

    ltile, dx, dmod, small = _local_step(x[0], loss_target[0], mod, weights_of, grads_done, pre_norm_w, post_norm_w, conv_w_full,
                                         conv_b, dt_bias, a_log, d_skip, ssm_norm_w, sinks)

    packed = _pack([dmod] + [g for layer in small for g in layer] + [ltile[0]])
    gs = _allgather8(packed, "gather_small")
    tot = _sum_blocks(gs[:, None], packed.shape[0], "sum_small")[0]
    parts = _unpack(tot, [(DEPTH, 3 * D)] + list(_SMALL) * DEPTH + [(LANES,)])
    g_ada_b, loss = parts[0], parts[-1][0]
    per_layer = [parts[1 + len(_SMALL) * i:1 + len(_SMALL) * (i + 1)] for i in range(DEPTH)]
    g_pre, g_post, g_cw, g_cb, g_dtb, g_al, g_dsk, g_nw, g_sk = [jnp.stack([per_layer[i][j] for i in range(DEPTH)]) for j in range(len(_SMALL))]
    g_pre, g_post, g_cb, g_nw = g_pre[:, 0], g_post[:, 0], g_cb[:, 0], g_nw[:, 0]
    g_cw = lax.dynamic_slice_in_dim(g_cw, (CONV_CH // 4) * chip, CONV_CH // 4, axis=2)

    dmod_all = gs[:, :(DEPTH * 3 * D) // LANES, :].reshape(8, DEPTH, 3 * D).transpose(1, 0, 2)
    dmod_mine = lax.dynamic_slice_in_dim(dmod_all, 768 * chip, 768, axis=2)
    c_t = jnp.pad(c_all.T, ((0, 0), (0, LANES - 8)))
    g_ada_w = _ada_grad(c_t, dmod_mine, "ada_grad")

    res = {}
    res["ada_w"] = _adamw(ada_w, [g_ada_w], m_ada_w, v_ada_w, 512, "adamw_ada_w")
    names = ["ada_b", "pre_norm_w", "post_norm_w", "conv_w", "conv_b", "dt_bias", "a_log", "d_skip", "ssm_norm_w", "sinks"]
    ws = [ada_b, pre_norm_w, post_norm_w, conv_w, conv_b, dt_bias, a_log, d_skip, ssm_norm_w, sinks]
    gsm = [g_ada_b, g_pre, g_post, g_cw, g_cb, g_dtb, g_al, g_dsk, g_nw, g_sk]
    ms = [m_ada_b, m_pre_norm_w, m_post_norm_w, m_conv_w, m_conv_b, m_dt_bias, m_a_log, m_d_skip, m_ssm_norm_w, m_sinks]
    vs = [v_ada_b, v_pre_norm_w, v_post_norm_w, v_conv_w, v_conv_b, v_dt_bias, v_a_log, v_d_skip, v_ssm_norm_w, v_sinks]
    pw_, pg_, pm_, pv_ = _pack(ws), _pack(gsm), _pack(ms), _pack(vs)
    small_out = _adamw(pw_[None], [pg_[None]], pm_[None], pv_[None], pw_.shape[0], "adamw_small")

    others_done = small_out[1][0, :8] + res["ada_w"][1][0, :8, :LANES]
    landed = [_split_wait(*scatters[i][:3], 2, others_done, f"scatter_wait{i}") for i in range(DEPTH)]
    p_in = _sum_chips([d[2] for d in landed], [d[0] for d in landed], 128, "sum_w_in")
    p_out = _sum_chips([d[3] for d in landed], [d[1] for d in landed], 256, "sum_w_out")
    col_major, row_major = (lambda a: jnp.transpose(a, (2, 0, 1))), (lambda a: jnp.transpose(a, (1, 2, 0)))
    p_in = col_major(p_in)
    s_in, s_out = _sibling_swap([p_in, p_out], "swap_partials")
    res["w_in"] = [row_major(a) for a in _adamw(col_major(w_in), [p_in, s_in], col_major(m_w_in), col_major(v_w_in), None,
                                                "adamw_w_in", lead=SHARD_IN // 18)]
    res["w_out"] = _adamw(w_out, [p_out, s_out], m_w_out, v_w_out, 512, "adamw_w_out")
    shapes = [w.shape for w in ws]
    for kind in range(4):
        for nm, a in zip(names, _unpack(small_out[kind][0], shapes)):
            res.setdefault(nm, [None] * 4)[kind] = a
    order = ["ada_w", "ada_b", "pre_norm_w", "post_norm_w", "w_in", "conv_w", "conv_b", "dt_bias", "a_log", "d_skip", "ssm_norm_w", "sinks", "w_out"]
    return (loss, dx[None], *[res[n][0] for n in order], *[res[n][1] for n in order], *[res[n][2] for n in order], *[res[n][3] for n in order])
```

```python
import math

import jax
import jax.numpy as jnp
from jax import lax
from jax.experimental import pallas as pl
from jax.experimental.pallas import tpu as pltpu

F32 = jnp.float32
MXU = jnp.bfloat16
HI = lax.Precision.HIGHEST
MESH = pl.DeviceIdType.MESH

SEQ = 4096
D = 1024
DEPTH = 4
HD = 64
QK_SCALE = HD ** -0.5
LANES = 128
BLK = 128
DILS = (1, 4, 16)
NEG = -1e30
EPS = 1e-6
MIB = 1024 * 1024

NP = 6144
QA, KA, VA, ZA = 0, 512, 1024, 1536
ZB, XBC = 2048, 3072
QC, ZC, KC, VC = 4608, 5120, 5632, 5760
DTC = 5888
IN_COLS = 5904
SHARD_IN = IN_COLS // 4
CONV_CH = 1536
TM = 512

ADAM_LR, ADAM_B1, ADAM_B2, ADAM_EPS, ADAM_WD, ADAM_STEP = 0.001, 0.9, 0.999, 1e-08, 0.01, 10

NT = (((1,), (1,)), ((), ()))
TN = (((0,), (0,)), ((), ()))


def _cp(vmem_mib=48):
    return pltpu.CompilerParams(vmem_limit_bytes=vmem_mib * MIB)


def _sds(shape, dtype=F32):
    return jax.ShapeDtypeStruct(shape, dtype)


def _full(shape):
    n = len(shape)
    return pl.BlockSpec(shape, lambda *_: (0,) * n)


def _mm(a, b, dims=None):
    if dims is None:
        return jnp.dot(a.astype(MXU), b.astype(MXU), preferred_element_type=F32)
    return lax.dot_general(a.astype(MXU), b.astype(MXU), dims, preferred_element_type=F32)


def _sigmoid(x):
    return 1.0 / (1.0 + jnp.exp(-x))


def _silu(x):
    return x * _sigmoid(x)


def _dsilu(x):
    s = _sigmoid(x)
    return s * (1.0 + x * (1.0 - s))


def _softplus(x):
    ax = jnp.where(x >= 0, x, -x)
    return jnp.maximum(x, 0.0) + jnp.log1p(jnp.exp(-ax))


def _half_masks():
    lane = lax.broadcasted_iota(jnp.int32, (1, LANES), 1)
    m0 = (lane < HD).astype(F32)
    return m0, 1.0 - m0


def _allgather8(v, name):
    r, cc = v.shape

    def body(v_ref, out_ref, send_sems, recv_sems):
        x, y, c = lax.axis_index("x"), lax.axis_index("y"), lax.axis_index("c")
        me = 4 * x + 2 * y + c
        out_ref[me] = v_ref[...]
        peers = []
        for k in range(1, 8):
            px = 1 - x if k & 4 else x
            py = 1 - y if k & 2 else y
            pc = 1 - c if k & 1 else c
            peers.append((px, py, pc))
        sends = []
        for k, peer in enumerate(peers):
            cp = pltpu.make_async_remote_copy(src_ref=v_ref, dst_ref=out_ref.at[me], send_sem=send_sems.at[k],
                                              recv_sem=recv_sems.at[k], device_id=peer, device_id_type=MESH)
            cp.start()
            sends.append(cp)
        for k, (px, py, pc) in enumerate(peers):
            pltpu.make_async_remote_copy(src_ref=v_ref, dst_ref=out_ref.at[4 * px + 2 * py + pc], send_sem=send_sems.at[k],
                                         recv_sem=recv_sems.at[k], device_id=(px, py, pc), device_id_type=MESH).wait_recv()
        for cp in sends:
            cp.wait_send()

    return pl.pallas_call(
        body, name=name, out_shape=_sds((8, r, cc)),
        in_specs=[pl.BlockSpec(memory_space=pltpu.VMEM)], out_specs=pl.BlockSpec(memory_space=pltpu.VMEM),
        scratch_shapes=[pltpu.SemaphoreType.DMA((7,)), pltpu.SemaphoreType.DMA((7,))],
        compiler_params=_cp(32),
    )(v)


_HBM = pl.BlockSpec(memory_space=pltpu.HBM)
_SEM = pl.BlockSpec(memory_space=pltpu.SEMAPHORE)
_EFFECT = pltpu.SideEffectType.DATAFLOW_SIDE_EFFECTING


def _chip_copies(src_refs, land_refs, send_sems, recv_sems, part="whole"):
    x, y, c = lax.axis_index("x"), lax.axis_index("y"), lax.axis_index("c")
    mine = 2 * x + y
    out = []
    for i, land in enumerate(land_refs):
        half = land.shape[1] // 2
        own, others = pl.ds(pl.multiple_of(c * half, half), half), pl.ds(pl.multiple_of((1 - c) * half, half), half)
        for j, (px, py) in enumerate([(1 - x, y), (x, 1 - y), (1 - x, 1 - y)]):
            slot, peer = 2 * px + py, (px, py, c)
            if part == "whole":
                src = src_refs[i].at[slot] if src_refs else land.at[mine]
                there, here = land.at[mine], land.at[slot]
            elif part == "half":
                src = there = land.at[mine].at[own]
                here = land.at[slot].at[own]
            else:
                src = there = land.at[slot].at[own]
                here, peer = land.at[slot].at[others], (x, y, 1 - c)
            mk = lambda dst, i=i, j=j, src=src, peer=peer: pltpu.make_async_remote_copy(
                src_ref=src, dst_ref=dst, send_sem=send_sems.at[3 * i + j], recv_sem=recv_sems.at[3 * i + j],
                device_id=peer, device_id_type=MESH)
            out.append((mk(there), mk(here)))
    return out


def _split_start(srcs, lands, name, part="whole"):
    ops = list(srcs or []) + list(lands)
    ns, n = len(srcs or []), len(lands)

    def body(*refs):
        src_refs, land_refs = refs[:ns], refs[ns:ns + n]
        send_sems, recv_sems = refs[ns + n], refs[ns + n + 1]
        for mine_out, _ in _chip_copies(src_refs, land_refs, send_sems, recv_sems, part):
            mine_out.start()
        refs[-1][...] = jnp.zeros_like(refs[-1])

    sems = pltpu.SemaphoreType.DMA((3 * n,))
    res = pl.pallas_call(
        body, name=name, out_shape=(sems, sems) + tuple(pltpu.HBM(a.shape, a.dtype) for a in ops) + (_sds((8, LANES)),),
        in_specs=[_HBM] * len(ops), out_specs=(_SEM, _SEM) + (_HBM,) * len(ops) + (pl.BlockSpec(memory_space=pltpu.VMEM),),
        input_output_aliases={k: 2 + k for k in range(len(ops))},
        compiler_params=pltpu.CompilerParams(has_side_effects=_EFFECT),
    )(*[pltpu.with_memory_space_constraint(a, pltpu.HBM) for a in ops])
    return res[0], res[1], list(res[2:2 + len(ops)]), res[-1]


def _split_wait(send_sems, recv_sems, thru, n, after, name, part="whole"):
    ns = len(thru) - n

    def body(*refs):
        src_refs, land_refs = refs[:ns], refs[ns:ns + n]
        for mine_out, arriving in _chip_copies(src_refs, land_refs, refs[ns + n], refs[ns + n + 1], part):
            mine_out.wait_send()
            arriving.wait_recv()

    res = pl.pallas_call(
        body, name=name, out_shape=tuple(pltpu.HBM(a.shape, a.dtype) for a in thru),
        in_specs=[_HBM] * len(thru) + [_SEM, _SEM, pl.BlockSpec(memory_space=pl.ANY)], out_specs=(_HBM,) * len(thru),
        input_output_aliases={k: k for k in range(len(thru))},
        compiler_params=pltpu.CompilerParams(has_side_effects=_EFFECT),
    )(*thru, send_sems, recv_sems, after)
    return list(res)


def _sibling_swap(arrs, name):
    n = len(arrs)

    def body(*refs):
        ins, outs_, (send_sems, recv_sems) = refs[:n], refs[n:2 * n], refs[2 * n:]
        sib = (lax.axis_index("x"), lax.axis_index("y"), 1 - lax.axis_index("c"))
        cps = [pltpu.make_async_remote_copy(src_ref=ins[i], dst_ref=outs_[i], send_sem=send_sems.at[i], recv_sem=recv_sems.at[i],
                                            device_id=sib, device_id_type=MESH) for i in range(n)]
        for cp in cps:
            cp.start()
        for cp in cps:
            cp.wait_recv()
        for cp in cps:
            cp.wait_send()

    hbm = pl.BlockSpec(memory_space=pltpu.HBM)
    return pl.pallas_call(
        body, name=name, out_shape=tuple(_sds(a.shape, a.dtype) for a in arrs), in_specs=[hbm] * n, out_specs=tuple([hbm] * n),
        scratch_shapes=[pltpu.SemaphoreType.DMA((n,)), pltpu.SemaphoreType.DMA((n,))],
    )(*arrs)


def _tile_spec(rows, cc):
    return pl.BlockSpec((None, rows, cc), lambda l, i: (l, i, 0))


def _cast_bf16(a, rows, name):
    nl, r, cc = a.shape

    def body(a_ref, o_ref):
        o_ref[...] = a_ref[...].astype(jnp.bfloat16)

    return pl.pallas_call(body, name=name, out_shape=_sds((nl, r, cc), jnp.bfloat16), grid=(nl, r // rows),
                          in_specs=[_tile_spec(rows, cc)], out_specs=_tile_spec(rows, cc), compiler_params=_cp())(a)


def _sum_blocks(a, rows, name):
    k, nl, r, cc = a.shape

    def body(a_ref, o_ref):
        acc = a_ref[0].astype(F32)
        for j in range(1, k):
            acc = acc + a_ref[j].astype(F32)
        o_ref[...] = acc

    return pl.pallas_call(body, name=name, out_shape=_sds((nl, r, cc)), grid=(nl, r // rows),
                          in_specs=[pl.BlockSpec((k, None, rows, cc), lambda l, i: (0, l, i, 0))],
                          out_specs=_tile_spec(rows, cc), compiler_params=_cp())(a)


def _sum_chips(lands, srcs, rows, name):
    nl = len(lands)
    _, r, cc = lands[0].shape

    def body(mine_ref, *refs):
        land_refs, src_refs, o_ref = refs[:nl], refs[nl:2 * nl], refs[2 * nl]
        mine = mine_ref[0]
        for j in range(nl):
            @pl.when(pl.program_id(0) == j)
            def _(j=j):
                own = src_refs[j][...].astype(F32)
                acc = None
                for k in range(4):
                    term = jnp.where(mine == k, own, land_refs[j][k].astype(F32))
                    acc = term if acc is None else acc + term
                o_ref[...] = acc.astype(o_ref.dtype)

    specs = [pl.BlockSpec((4, rows, cc), lambda l, i, mr, j=j: (0, jnp.where(l == j, i, 0), 0)) for j in range(nl)]
    own_specs = [pl.BlockSpec((None, rows, cc), lambda l, i, mr, j=j: (mr[0], jnp.where(l == j, i, 0), 0)) for j in range(nl)]
    mine = (2 * lax.axis_index("x") + lax.axis_index("y")).astype(jnp.int32).reshape(1)
    grid_spec = pltpu.PrefetchScalarGridSpec(num_scalar_prefetch=1, grid=(nl, r // rows), in_specs=specs + own_specs,
                                             out_specs=pl.BlockSpec((None, rows, cc), lambda l, i, mr: (l, i, 0)))
    return pl.pallas_call(body, name=name, out_shape=_sds((nl, r, cc), jnp.bfloat16), grid_spec=grid_spec,
                          compiler_params=_cp())(mine, *lands, *srcs)


def _adamw(w, parts, m, v, rows, name, lead=None):
    nl, r, cc = w.shape
    np_ = len(parts)
    c1 = 1.0 / (1.0 - ADAM_B1 ** ADAM_STEP)
    c2 = 1.0 / (1.0 - ADAM_B2 ** ADAM_STEP)

    def body(*refs):
        w_ref, p_refs, (m_ref, v_ref, g_ref, d_ref, nm_ref, nv_ref) = refs[0], refs[1:1 + np_], refs[1 + np_:]
        g = p_refs[0][...].astype(F32)
        for p_ref in p_refs[1:]:
            g = g + p_ref[...].astype(F32)
        nm = ADAM_B1 * m_ref[...] + (1.0 - ADAM_B1) * g
        nv = ADAM_B2 * v_ref[...] + (1.0 - ADAM_B2) * (g * g)
        g_ref[...] = g
        nm_ref[...] = nm
        nv_ref[...] = nv
        d_ref[...] = -ADAM_LR * ((nm * c1) / (jnp.sqrt(nv * c2) + ADAM_EPS) + ADAM_WD * w_ref[...])

    if lead is None:
        spec, grid = _tile_spec(rows, cc), (nl, r // rows)
    else:
        spec, grid = pl.BlockSpec((lead, r, cc), lambda i: (i, 0, 0)), (nl // lead,)
    return pl.pallas_call(body, name=name, out_shape=(_sds((nl, r, cc)),) * 4, grid=grid,
                          in_specs=[spec] * (3 + np_), out_specs=(spec,) * 4, compiler_params=_cp())(w, *parts, m, v)


_BIAS = pltpu.VMEM((2, 2 * BLK, 2 * BLK), F32)


def _fill_band_bias(bias_ref):
    qi = lax.broadcasted_iota(jnp.int32, (2 * BLK, 2 * BLK), 0) & (BLK - 1)
    kj = lax.broadcasted_iota(jnp.int32, (2 * BLK, 2 * BLK), 1)
    dist = BLK + qi - kj
    band = (dist >= 0) & (dist <= BLK)
    bias_ref[0] = jnp.where(band, 0.0, NEG)
    bias_ref[1] = jnp.where(band & (kj >= BLK), 0.0, NEG)


class _HeadStack:
    def __init__(self, group):
        self.m0, self.m1 = _half_masks()
        self.group = group
        if group is not None:
            self.kv_mask = (self.m0, self.m1)[group]

    def _swap_half(self, t, a):
        return t if a == self.group else pltpu.roll(t, HD, axis=1)

    def stack(self, t):
        low = lax.broadcasted_iota(jnp.int32, (1, LANES), 1) < HD
        t0, t1 = jnp.where(low, t, 0.0), jnp.where(low, 0.0, t)
        if self.group is not None:
            t0, t1 = self._swap_half(t0, 0), self._swap_half(t1, 1)
        return jnp.concatenate([t0, t1], axis=0)

    def unstack(self, ts):
        if self.group is None:
            return jnp.where(lax.broadcasted_iota(jnp.int32, (1, LANES), 1) < HD, ts[:BLK], ts[BLK:])
        return self._swap_half(ts[:BLK] * self.kv_mask, 0) + self._swap_half(ts[BLK:] * self.kv_mask, 1)


def _rows(st, dil):
    if dil == 1:
        return pl.ds(pl.multiple_of(st, BLK), BLK)
    return pl.ds(st, BLK, stride=dil)


def _block_pos(n, dil):
    nb = SEQ // (dil * BLK)
    r, b = n // nb, n % nb
    hp = (b > 0).astype(jnp.int32)
    st = r + dil * BLK * b
    return st, st - dil * BLK * hp, 1 - hp


def _attn_fwd(proj, qblk, kblk, vblk, dils, gqa, sink_x, name):
    has_sink = sink_x is not None

    def body(*refs):
        if has_sink:
            q_ref, k_ref, v_ref, s_ref, o_ref, lse_ref, m_scr, z_scr, bias_scr = refs
        else:
            q_ref, k_ref, v_ref, o_ref, lse_ref, m_scr, z_scr, bias_scr = refs

        @pl.when(pl.program_id(0) == 0)
        def _():
            _fill_band_bias(bias_scr)
        o_ref[...] = jnp.zeros_like(o_ref)
        if has_sink:
            z_scr[...] = jnp.ones_like(z_scr)
            m_scr[...] = jnp.broadcast_to(s_ref[...], m_scr.shape)
        else:
            z_scr[...] = jnp.zeros_like(z_scr)
            m_scr[...] = jnp.full_like(m_scr, NEG)

        def step(n, carry, dil, heads):
            m0, m1 = heads.m0, heads.m1
            st, stp, first = _block_pos(n, dil)
            rq, rp = _rows(st, dil), _rows(stp, dil)
            kk = jnp.concatenate([k_ref[rp, :], k_ref[rq, :]], axis=0)
            vv = jnp.concatenate([v_ref[rp, :], v_ref[rq, :]], axis=0)
            s = _mm(heads.stack(q_ref[rq, :] * QK_SCALE), kk, NT) + bias_scr[first]
            m = jnp.max(s, axis=1, keepdims=True)
            p = jnp.exp(s - m)
            l = jnp.sum(p, axis=1, keepdims=True)
            o_pair = heads.unstack(_mm(p, vv))
            m_pair = m[:BLK] * m0 + m[BLK:] * m1
            l_pair = l[:BLK] * m0 + l[BLK:] * m1
            m_old = m_scr[rq, :]
            m_new = jnp.maximum(m_old, m_pair)
            alpha, beta = jnp.exp(m_old - m_new), jnp.exp(m_pair - m_new)
            o_ref[rq, :] = o_ref[rq, :] * alpha + o_pair * beta
            z_scr[rq, :] = z_scr[rq, :] * alpha + l_pair * beta
            m_scr[rq, :] = m_new
            return carry

        def blocks(heads):
            for dil in dils:
                lax.fori_loop(0, SEQ // BLK, lambda n, carry, dil=dil: step(n, carry, dil, heads), 0, unroll=8 if gqa else 16)

        if gqa:
            for grp in range(2):
                pl.when(pl.program_id(0) // 2 == grp)(lambda grp=grp: blocks(_HeadStack(grp)))
        else:
            blocks(_HeadStack(None))

        def fin(t, carry):
            rt = pl.ds(pl.multiple_of(t * TM, TM), TM)
            z = z_scr[rt, :]
            o_ref[rt, :] = o_ref[rt, :] / z
            lse_ref[rt, :] = m_scr[rt, :] + jnp.log(z)
            return carry
        lax.fori_loop(0, SEQ // TM, fin, 0)

    col = lambda blk: pl.BlockSpec((SEQ, LANES), lambda p, blk=blk: (0, blk + p))
    kv = (lambda blk: pl.BlockSpec((SEQ, LANES), lambda p, blk=blk: (0, blk))) if gqa else col
    in_specs = [col(qblk), kv(kblk), kv(vblk)]
    args = [proj, proj, proj]
    if has_sink:
        in_specs.append(pl.BlockSpec((1, LANES), lambda p: (0, p)))
        args.append(sink_x)
    out = pl.BlockSpec((SEQ, LANES), lambda p: (0, p))
    return pl.pallas_call(body, name=name, out_shape=(_sds((SEQ, 512)), _sds((SEQ, 512))), grid=(4,),
                          in_specs=in_specs, out_specs=(out, out),
                          scratch_shapes=[pltpu.VMEM((SEQ, LANES), F32), pltpu.VMEM((SEQ, LANES), F32), _BIAS],
                          compiler_params=_cp(48))(*args)


def _attn_bwd(proj, qblk, kblk, vblk, do, o, lse, dils, gqa, sink_x, name):
    has_sink = sink_x is not None

    def body(*refs):
        if has_sink:
            q_ref, k_ref, v_ref, do_ref, o_ref, lse_ref, s_ref, dq_ref, dk_ref, dv_ref, ds_ref, bias_scr = refs
        else:
            q_ref, k_ref, v_ref, do_ref, o_ref, lse_ref, dq_ref, dk_ref, dv_ref, bias_scr = refs
        pid = pl.program_id(0)

        @pl.when(pid == 0)
        def _():
            _fill_band_bias(bias_scr)
        dq_ref[...] = jnp.zeros_like(dq_ref)
        if gqa:
            @pl.when(pid == 0)
            def _():
                dk_ref[...] = jnp.zeros_like(dk_ref)
                dv_ref[...] = jnp.zeros_like(dv_ref)
        else:
            dk_ref[...] = jnp.zeros_like(dk_ref)
            dv_ref[...] = jnp.zeros_like(dv_ref)

        def step(n, carry, dil, heads):
            m0, m1 = heads.m0, heads.m1
            st, stp, first = _block_pos(n, dil)
            rq, rp = _rows(st, dil), _rows(stp, dil)
            do_, lse_ = do_ref[rq, :], lse_ref[rq, :]
            kk = jnp.concatenate([k_ref[rp, :], k_ref[rq, :]], axis=0)
            vv = jnp.concatenate([v_ref[rp, :], v_ref[rq, :]], axis=0)
            qs, dos = heads.stack(q_ref[rq, :] * QK_SCALE), heads.stack(do_)
            doo = do_ * o_ref[rq, :]
            delta = jnp.concatenate([jnp.sum(doo * m0, axis=1, keepdims=True), jnp.sum(doo * m1, axis=1, keepdims=True)], axis=0)
            lse_s = jnp.concatenate([lse_[:, 0:1], lse_[:, HD:HD + 1]], axis=0)
            p = jnp.exp(_mm(qs, kk, NT) + bias_scr[first] - lse_s)
            ds = p * (_mm(dos, vv, NT) - delta)
            dq_ref[rq, :] += heads.unstack(_mm(ds, kk)) * QK_SCALE
            dk_sum, dv_sum = _mm(ds, qs, TN), _mm(p, dos, TN)
            dk_ref[rp, :] += dk_sum[:BLK]
            dk_ref[rq, :] += dk_sum[BLK:]
            dv_ref[rp, :] += dv_sum[:BLK]
            dv_ref[rq, :] += dv_sum[BLK:]
            return carry

        def blocks(heads):
            for dil in dils:
                lax.fori_loop(0, SEQ // BLK, lambda n, carry, dil=dil: step(n, carry, dil, heads), 0, unroll=4)

        if gqa:
            for grp in range(2):
                pl.when(pid // 2 == grp)(lambda grp=grp: blocks(_HeadStack(grp)))
        else:
            blocks(_HeadStack(None))

        if has_sink:
            m0, m1 = _half_masks()

            def sink_rows(t, acc):
                rt = pl.ds(pl.multiple_of(t * TM, TM), TM)
                return acc - jnp.sum(jnp.exp(s_ref[...] - lse_ref[rt, :]) * (do_ref[rt, :] * o_ref[rt, :]), axis=0, keepdims=True)
            acc = lax.fori_loop(0, SEQ // TM, sink_rows, jnp.zeros((1, LANES), F32))
            per_head = jnp.sum(acc * m0, axis=1, keepdims=True) * m0 + jnp.sum(acc * m1, axis=1, keepdims=True) * m1
            ds_ref[0] = jnp.broadcast_to(per_head, (8, LANES))

    col = lambda blk: pl.BlockSpec((SEQ, LANES), lambda p, blk=blk: (0, blk + p))
    kv = (lambda blk: pl.BlockSpec((SEQ, LANES), lambda p, blk=blk: (0, blk))) if gqa else col
    pair = pl.BlockSpec((SEQ, LANES), lambda p: (0, p))
    in_specs = [col(qblk), kv(kblk), kv(vblk), pair, pair, pair]
    args = [proj, proj, proj, do, o, lse]
    kvw = LANES if gqa else 512
    kv_out = pl.BlockSpec((SEQ, LANES), lambda p: (0, 0)) if gqa else pair
    out_shape = [_sds((SEQ, 512)), _sds((SEQ, kvw)), _sds((SEQ, kvw))]
    out_specs = [pair, kv_out, kv_out]
    if has_sink:
        in_specs.append(pl.BlockSpec((1, LANES), lambda p: (0, p)))
        args.append(sink_x)
        out_shape.append(_sds((4, 8, LANES)))
        out_specs.append(pl.BlockSpec((1, 8, LANES), lambda p: (p, 0, 0)))
    return pl.pallas_call(body, name=name, out_shape=tuple(out_shape), grid=(4,), in_specs=in_specs,
                          out_specs=tuple(out_specs), scratch_shapes=[_BIAS], compiler_params=_cp(56))(*args)


_CT = 128


def _rows_before(x_ref, t, k):
    if t == 0:
        return jnp.concatenate([jnp.zeros((k, LANES), F32), x_ref[0:_CT - k, :]], axis=0)
    return x_ref[t * _CT - k:(t + 1) * _CT - k, :]


def _conv_pre(x_ref, w_ref, b_ref, t):
    taps = [x_ref[t * _CT:(t + 1) * _CT, :]] + [_rows_before(x_ref, t, k) for k in range(1, 4)]
    u = b_ref[...] + taps[0] * w_ref[3:4, :]
    for k in range(1, 4):
        u = u + taps[k] * w_ref[3 - k:4 - k, :]
    return u, taps


def _conv_fwd(proj, w, b, name):
    def body(x_ref, w_ref, b_ref, o_ref):
        for t in range(SEQ // _CT):
            o_ref[t * _CT:(t + 1) * _CT, :] = _silu(_conv_pre(x_ref, w_ref, b_ref, t)[0])

    nblk = CONV_CH // LANES
    return pl.pallas_call(body, name=name, out_shape=_sds((SEQ, CONV_CH)), grid=(nblk,),
                          in_specs=[pl.BlockSpec((SEQ, LANES), lambda j: (0, XBC // LANES + j)),
                                    pl.BlockSpec((4, LANES), lambda j: (0, j)), pl.BlockSpec((1, LANES), lambda j: (0, j))],
                          out_specs=pl.BlockSpec((SEQ, LANES), lambda j: (0, j)), compiler_params=_cp())(proj, w, b)


def _conv_bwd(proj, dact, w, b, name):
    def body(x_ref, da_ref, w_ref, b_ref, dx_ref, dw_ref, db_ref, du_scr):
        du_scr[SEQ:SEQ + 8, :] = jnp.zeros((8, LANES), F32)
        db = jnp.zeros((1, LANES), F32)
        dws = [jnp.zeros((1, LANES), F32)] * 4
        for t in range(SEQ // _CT):
            u, taps = _conv_pre(x_ref, w_ref, b_ref, t)
            du = da_ref[t * _CT:(t + 1) * _CT, :] * _dsilu(u)
            du_scr[t * _CT:(t + 1) * _CT, :] = du
            db = db + jnp.sum(du, axis=0, keepdims=True)
            dws = [dws[k] + jnp.sum(du * taps[k], axis=0, keepdims=True) for k in range(4)]
        db_ref[...] = db
        for k in range(4):
            dw_ref[3 - k:4 - k, :] = dws[k]
        for t in range(SEQ // _CT):
            dx = du_scr[t * _CT:(t + 1) * _CT, :] * w_ref[3:4, :]
            for k in range(1, 4):
                dx = dx + du_scr[t * _CT + k:(t + 1) * _CT + k, :] * w_ref[3 - k:4 - k, :]
            dx_ref[t * _CT:(t + 1) * _CT, :] = dx.astype(dx_ref.dtype)

    nblk = CONV_CH // LANES
    blk = pl.BlockSpec((SEQ, LANES), lambda j: (0, j))
    wspec, bspec = pl.BlockSpec((4, LANES), lambda j: (0, j)), pl.BlockSpec((1, LANES), lambda j: (0, j))
    return pl.pallas_call(body, name=name, out_shape=(_sds((SEQ, CONV_CH), MXU), _sds((4, CONV_CH)), _sds((1, CONV_CH))), grid=(nblk,),
                          in_specs=[pl.BlockSpec((SEQ, LANES), lambda j: (0, XBC // LANES + j)), blk, wspec, bspec],
                          out_specs=(blk, wspec, bspec), scratch_shapes=[pltpu.VMEM((SEQ + 8, LANES), F32)],
                          compiler_params=_cp())(proj, dact, w, b)


def _column(t, h):
    lane = lax.broadcasted_iota(jnp.int32, (1, LANES), 1)
    pick = jax.custom_vjp(lambda v: v[:, h:h + 1])
    pick.defvjp(lambda v: (v[:, h:h + 1], None), lambda _, g: (g * (lane == h).astype(F32),))
    return pick(t)


def _row(t, h):
    sub = lax.broadcasted_iota(jnp.int32, (BLK, 1), 0)
    pick = jax.custom_vjp(lambda v: v[h:h + 1, :])
    pick.defvjp(lambda v: (v[h:h + 1, :], None), lambda _, g: (g * (sub == h).astype(F32),))
    return pick(t)


def _ssd_chunk(xs, bm, cm, dtr, z, hs, al16, dtb, dskx, nw):
    m0, m1 = _half_masks()
    row = lax.broadcasted_iota(jnp.int32, (BLK, BLK), 0)
    col = lax.broadcasted_iota(jnp.int32, (BLK, BLK), 1)
    causal = row >= col
    tril = causal.astype(F32)
    lane = lax.broadcasted_iota(jnp.int32, (1, LANES), 1)
    dt = jnp.where(lane < 16, _softplus(dtr + dtb), 0.0)
    a16 = -jnp.exp(al16)
    acum = jnp.dot(tril, dt * a16, precision=HI, preferred_element_type=F32)
    acum_t = acum.T
    gmat = [_mm(cm[g], bm[g], NT) for g in range(2)]
    ys, hn = [], []
    for p in range(8):
        g = p // 4
        col_h = [_column(acum, 2 * p + a) for a in range(2)]
        dt_x = _column(dt, 2 * p) * m0 + _column(dt, 2 * p + 1) * m1
        ac_x = col_h[0] * m0 + col_h[1] * m1
        a_end = _row(ac_x, BLK - 1)
        xdt = xs[p] * dt_x
        y = _mm(cm[g], hs[p]) * jnp.exp(ac_x)
        for a, msk in enumerate((m0, m1)):
            decay = jnp.exp(jnp.where(causal, col_h[a] - _row(acum_t, 2 * p + a), NEG))
            y = y + _mm(gmat[g] * decay, xdt * msk)
        st = _mm(bm[g], xdt * jnp.exp(a_end - ac_x), TN)
        hn.append(hs[p] * jnp.exp(a_end) + st)
        y = y + dskx[p] * xs[p]
        ys.append(y * _silu(z[p]))
    out = []
    for g in range(2):
        ms = sum(jnp.sum(ys[p] * ys[p], axis=1, keepdims=True) for p in range(4 * g, 4 * g + 4)) * (1.0 / 512)
        rstd = lax.rsqrt(ms + EPS)
        out += [ys[p] * rstd * nw[p] for p in range(4 * g, 4 * g + 4)]
    return out, hn


def _tiles(ref, n, off=0, rows=slice(None)):
    return [ref[rows, off + LANES * p:off + LANES * (p + 1)] for p in range(n)]


def _ssd_load(xbc_ref, z_ref, dt_ref, rows):
    return (_tiles(xbc_ref, 8, 0, rows), _tiles(xbc_ref, 2, 1024, rows), _tiles(xbc_ref, 2, 1280, rows), dt_ref[rows, :],
            _tiles(z_ref, 8, 0, rows))


def _ssd_params(al16_ref, dtb_ref, dsk_ref, nw_ref):
    return al16_ref[...], dtb_ref[...], _tiles(dsk_ref, 8), _tiles(nw_ref, 8)


_NCH = SEQ // BLK
_PER_STEP = 2
_STEP_ROWS = _PER_STEP * BLK


def _ssd_param_specs():
    return [_full((1, LANES)), _full((1, LANES)), _full((1, 1024)), _full((1, 1024))]


def _ssd_fwd(xbc_act, proj, al16, dtb, dskx, nw, name):
    def body(xbc_ref, z_ref, dt_ref, al16_ref, dtb_ref, dsk_ref, nw_ref, y_ref, hin_ref, h_scr):
        @pl.when(pl.program_id(0) == 0)
        def _():
            h_scr[...] = jnp.zeros_like(h_scr)
        params = _ssd_params(al16_ref, dtb_ref, dsk_ref, nw_ref)
        hs = _tiles(h_scr, 8)
        for k in range(_PER_STEP):
            rows = slice(BLK * k, BLK * (k + 1))
            for p in range(8):
                hin_ref[k, :, LANES * p:LANES * (p + 1)] = hs[p]
            ys, hs = _ssd_chunk(*_ssd_load(xbc_ref, z_ref, dt_ref, rows), hs, *params)
            for p in range(8):
                y_ref[rows, LANES * p:LANES * (p + 1)] = ys[p].astype(y_ref.dtype)
        for p in range(8):
            h_scr[:, LANES * p:LANES * (p + 1)] = hs[p]

    return pl.pallas_call(
        body, name=name, out_shape=(_sds((SEQ, 1024), MXU), _sds((_NCH, BLK, 1024))), grid=(_NCH // _PER_STEP,),
        in_specs=[pl.BlockSpec((_STEP_ROWS, CONV_CH), lambda c: (c, 0)), pl.BlockSpec((_STEP_ROWS, 1024), lambda c: (c, ZB // 1024)),
                  pl.BlockSpec((_STEP_ROWS, LANES), lambda c: (c, DTC // LANES))] + _ssd_param_specs(),
        out_specs=(pl.BlockSpec((_STEP_ROWS, 1024), lambda c: (c, 0)), pl.BlockSpec((_PER_STEP, BLK, 1024), lambda c: (c, 0, 0))),
        scratch_shapes=[pltpu.VMEM((BLK, 1024), F32)], compiler_params=_cp())(xbc_act, proj, proj, al16, dtb, dskx, nw)


def _ssd_bwd(xbc_act, proj, hin, dyb, al16, dtb, dskx, nw, name):
    def body(xbc_ref, z_ref, dt_ref, hin_ref, dy_ref, al16_ref, dtb_ref, dsk_ref, nw_ref,
             dxbc_ref, dz_ref, ddt_ref, dal16_ref, ddtb_ref, ddsk_ref, dnw_ref, dh_scr):
        @pl.when(pl.program_id(0) == 0)
        def _():
            dh_scr[...] = jnp.zeros_like(dh_scr)
            for r in (dal16_ref, ddtb_ref, ddsk_ref, dnw_ref):
                r[...] = jnp.zeros_like(r)
        params = _ssd_params(al16_ref, dtb_ref, dsk_ref, nw_ref)
        dhs = _tiles(dh_scr, 8)
        for k in reversed(range(_PER_STEP)):
            rows = slice(BLK * k, BLK * (k + 1))
            hs = [hin_ref[k, :, LANES * p:LANES * (p + 1)] for p in range(8)]
            _, vjp = jax.vjp(lambda a, h, q: _ssd_chunk(*a, h, *q), _ssd_load(xbc_ref, z_ref, dt_ref, rows), hs, params)
            (dxs, dbm, dcm, ddt, dz), dhs, (dal16, ddtb, ddsk, dnw) = vjp((_tiles(dy_ref, 8, 0, rows), dhs))
            for p in range(8):
                cols = slice(LANES * p, LANES * (p + 1))
                dxbc_ref[rows, cols] = dxs[p]
                dz_ref[rows, cols] = dz[p].astype(dz_ref.dtype)
                ddsk_ref[:, cols] += ddsk[p]
                dnw_ref[:, cols] += dnw[p]
            for g in range(2):
                dxbc_ref[rows, 1024 + LANES * g:1024 + LANES * (g + 1)] = dbm[g]
                dxbc_ref[rows, 1280 + LANES * g:1280 + LANES * (g + 1)] = dcm[g]
            ddt_ref[rows, :] = ddt.astype(ddt_ref.dtype)
            dal16_ref[...] += dal16
            ddtb_ref[...] += ddtb
        for p in range(8):
            dh_scr[:, LANES * p:LANES * (p + 1)] = dhs[p]

    rev = lambda c: _NCH // _PER_STEP - 1 - c
    return pl.pallas_call(
        body, name=name,
        out_shape=(_sds((SEQ, CONV_CH)), _sds((SEQ, 1024), MXU), _sds((SEQ, LANES), MXU),
                   _sds((1, LANES)), _sds((1, LANES)), _sds((1, 1024)), _sds((1, 1024))),
        grid=(_NCH // _PER_STEP,),
        in_specs=[pl.BlockSpec((_STEP_ROWS, CONV_CH), lambda c: (rev(c), 0)), pl.BlockSpec((_STEP_ROWS, 1024), lambda c: (rev(c), ZB // 1024)),
                  pl.BlockSpec((_STEP_ROWS, LANES), lambda c: (rev(c), DTC // LANES)),
                  pl.BlockSpec((_PER_STEP, BLK, 1024), lambda c: (rev(c), 0, 0)),
                  pl.BlockSpec((_STEP_ROWS, 1024), lambda c: (rev(c), 0))] + _ssd_param_specs(),
        out_specs=(pl.BlockSpec((_STEP_ROWS, CONV_CH), lambda c: (rev(c), 0)), pl.BlockSpec((_STEP_ROWS, 1024), lambda c: (rev(c), 0)),
                   pl.BlockSpec((_STEP_ROWS, LANES), lambda c: (rev(c), 0)),
                   _full((1, LANES)), _full((1, LANES)), _full((1, 1024)), _full((1, 1024))),
        scratch_shapes=[pltpu.VMEM((BLK, 1024), F32)], compiler_params=_cp())(xbc_act, proj, proj, hin, dyb, al16, dtb, dskx, nw)


def _rstd(v):
    return lax.rsqrt(jnp.mean(v * v, axis=1, keepdims=True) + EPS)


def _rms_bwd(dn, n, rstd):
    return rstd * (dn - n * jnp.mean(dn * n, axis=1, keepdims=True))


_VEC = _full((1, D))


def _layer_spec(layer):
    return pl.BlockSpec((None, 2048, D), lambda *_: (layer, 0, 0))

_ROW = pl.BlockSpec((TM, D), lambda i, *_: (i, 0))


def _proj_fwd(x, pre_w, scale, shift, w, layer, name):
    tn, ni = 1024, SEQ // TM

    def body(x_ref, pw_ref, sc_ref, sh_ref, w_ref, o_ref, h_ref, h_scr):
        rows = pl.ds(pl.multiple_of(pl.program_id(1) * TM, TM), TM)

        @pl.when(pl.program_id(0) == 0)
        def _():
            xv = x_ref[...]
            h = ((xv * _rstd(xv) * pw_ref[...]) * (1.0 + sc_ref[...]) + sh_ref[...]).astype(h_ref.dtype)
            h_scr[rows, :] = h
            h_ref[...] = h
        o_ref[...] = jnp.dot(h_scr[rows, :], w_ref[...].astype(MXU), preferred_element_type=F32)

    first_pass = pl.BlockSpec((TM, D), lambda j, i: (jnp.where(j == 0, i, ni - 1), 0))
    return pl.pallas_call(body, name=name, out_shape=(_sds((SEQ, NP)), _sds((SEQ, D), MXU)), grid=(NP // tn, ni),
                          in_specs=[first_pass, _VEC, _VEC, _VEC, pl.BlockSpec((None, D, tn), lambda j, i: (layer, 0, j))],
                          out_specs=(pl.BlockSpec((TM, tn), lambda j, i: (i, j)), first_pass),
                          scratch_shapes=[pltpu.VMEM((SEQ, D), MXU)], compiler_params=_cp())(x, pre_w, scale, shift, w)


_HALF = pl.BlockSpec((TM, 512), lambda i: (i, 0))
_Z_A = pl.BlockSpec((TM, 512), lambda i: (i, ZA // 512))
_Z_C = pl.BlockSpec((TM, 512), lambda i: (i, ZC // 512))


def _out_fwd(o_a, yb, o_c, proj, w, layer, x, gate, post_w, name):
    def body(oa_ref, yb_ref, oc_ref, za_ref, zc_ref, w_ref, x_ref, g_ref, pw_ref, xn_ref, y_ref):
        y = (_mm(oa_ref[...] * _silu(za_ref[...]), w_ref[0:512, :]) + _mm(yb_ref[...], w_ref[512:1536, :])
             + _mm(oc_ref[...] * _silu(zc_ref[...]), w_ref[1536:2048, :]))
        y_ref[...] = y
        xn_ref[...] = x_ref[...] + g_ref[...] * (y * _rstd(y) * pw_ref[...])

    return pl.pallas_call(body, name=name, out_shape=(_sds((SEQ, D)), _sds((SEQ, D))), grid=(SEQ // TM,),
                          in_specs=[_HALF, _ROW, _HALF, _Z_A, _Z_C, _layer_spec(layer), _ROW, _VEC, _VEC],
                          out_specs=(_ROW, _ROW), compiler_params=_cp())(o_a, yb, o_c, proj, proj, w, x, gate, post_w)


def _dymix(dxo, y, gate, post_w, w, layer, o_a, o_c, proj, name):
    def body(dx_ref, y_ref, g_ref, pw_ref, w_ref, oa_ref, oc_ref, za_ref, zc_ref,
             dy_ref, dg_ref, dpw_ref, doa_ref, dza_ref, b_ref, doc_ref, dzc_ref):
        @pl.when(pl.program_id(0) == 0)
        def _():
            dg_ref[...] = jnp.zeros_like(dg_ref)
            dpw_ref[...] = jnp.zeros_like(dpw_ref)
        dx, yv = dx_ref[...], y_ref[...]
        rstd = _rstd(yv)
        n = yv * rstd
        dg_ref[...] += jnp.sum(dx * (n * pw_ref[...]), axis=0, keepdims=True)
        dr = dx * g_ref[...]
        dpw_ref[...] += jnp.sum(dr * n, axis=0, keepdims=True)
        dy = _rms_bwd(dr * pw_ref[...], n, rstd)
        dy_ref[...] = dy
        b_ref[...] = _mm(dy, w_ref[512:1536, :], NT)
        for rows, o_ref, z_ref, do_ref, dz_ref in ((slice(0, 512), oa_ref, za_ref, doa_ref, dza_ref),
                                                   (slice(1536, 2048), oc_ref, zc_ref, doc_ref, dzc_ref)):
            dyg, z = _mm(dy, w_ref[rows, :], NT), z_ref[...]
            do_ref[...] = dyg * _silu(z)
            dz_ref[...] = (dyg * o_ref[...] * _dsilu(z)).astype(dz_ref.dtype)

    return pl.pallas_call(body, name=name,
                          out_shape=(_sds((SEQ, D)), _sds((1, D)), _sds((1, D)),
                                     _sds((SEQ, 512)), _sds((SEQ, 512), MXU), _sds((SEQ, D)), _sds((SEQ, 512)), _sds((SEQ, 512), MXU)),
                          grid=(SEQ // TM,), in_specs=[_ROW, _ROW, _VEC, _VEC, _layer_spec(layer), _HALF, _HALF, _Z_A, _Z_C],
                          out_specs=(_ROW, _VEC, _VEC, _HALF, _HALF, _ROW, _HALF, _HALF),
                          compiler_params=_cp())(dxo, y, gate, post_w, w, o_a, o_c, proj, proj)


def _dwout(o_a, yb, o_c, proj, dy, name):
    def body(oa_ref, yb_ref, oc_ref, za_ref, zc_ref, dy_ref, o_ref):
        @pl.when(pl.program_id(0) == 0)
        def _():
            o_ref[...] = jnp.zeros_like(o_ref)
        dy = dy_ref[...]
        o_ref[0:512, :] += _mm(oa_ref[...] * _silu(za_ref[...]), dy, TN)
        o_ref[512:1536, :] += _mm(yb_ref[...], dy, TN)
        o_ref[1536:2048, :] += _mm(oc_ref[...] * _silu(zc_ref[...]), dy, TN)

    return pl.pallas_call(body, name=name, out_shape=_sds((2048, D)), grid=(SEQ // TM,),
                          in_specs=[_HALF, _ROW, _HALF, _Z_A, _Z_C, _ROW], out_specs=_full((2048, D)),
                          compiler_params=_cp())(o_a, yb, o_c, proj, proj, dy)


def _dwin(h, pieces, name):
    n = len(pieces)
    widths = [p.shape[1] for p in pieces]
    half = NP // 2

    def body(*refs):
        h_ref, p_refs, o_ref = refs[0], refs[1:1 + n], refs[1 + n]

        @pl.when(pl.program_id(0) == 0)
        def _():
            o_ref[...] = jnp.zeros_like(o_ref)
        hv, c0 = h_ref[...], 0
        for p_ref, wd in zip(p_refs, widths):
            o_ref[:, c0:c0 + wd] += _mm(hv, p_ref[...], TN)
            c0 += wd

    return pl.pallas_call(body, name=name, out_shape=_sds((D, half)), grid=(SEQ // TM,),
                          in_specs=[_ROW] + [pl.BlockSpec((TM, wd), lambda k: (k, 0)) for wd in widths],
                          out_specs=_full((D, half)), compiler_params=_cp(56))(h, *pieces)


_TMH = 256


def _dh_bwd(pieces, w, x, pre_w, scale, dxo, name):
    n = len(pieces)
    widths = [p.shape[1] for p in pieces]

    def body(*refs):
        p_refs, (w_ref, x_ref, pw_ref, sc_ref, dxo_ref, dx_ref, dsh_ref, dsc_ref, dpw_ref) = refs[:n], refs[n:]

        @pl.when(pl.program_id(0) == 0)
        def _():
            for r in (dsh_ref, dsc_ref, dpw_ref):
                r[...] = jnp.zeros_like(r)
        dh, c0 = 0.0, 0
        for p_ref, wd in zip(p_refs, widths):
            dh = dh + _mm(p_ref[...], w_ref[:, c0:c0 + wd], NT)
            c0 += wd
        xv = x_ref[...]
        rstd = _rstd(xv)
        nrm = xv * rstd
        dsh_ref[...] += jnp.sum(dh, axis=0, keepdims=True)
        dsc_ref[...] += jnp.sum(dh * (nrm * pw_ref[...]), axis=0, keepdims=True)
        dhn = dh * (1.0 + sc_ref[...])
        dpw_ref[...] += jnp.sum(dhn * nrm, axis=0, keepdims=True)
        dx_ref[...] = _rms_bwd(dhn * pw_ref[...], nrm, rstd) + dxo_ref[...]

    row = pl.BlockSpec((_TMH, D), lambda i: (i, 0))
    return pl.pallas_call(body, name=name, out_shape=(_sds((SEQ, D)), _sds((1, D)), _sds((1, D)), _sds((1, D))),
                          grid=(SEQ // _TMH,),
                          in_specs=[pl.BlockSpec((_TMH, wd), lambda i: (i, 0)) for wd in widths]
                          + [pl.BlockSpec((None, D, NP), lambda i: (0, 0, 0)), row, _VEC, _VEC, row],
                          out_specs=(row, _VEC, _VEC, _VEC), compiler_params=_cp(56))(*pieces, w, x, pre_w, scale, dxo)


def _w_in_padded(land, name):
    rows = 128

    def body(l_ref, o_ref):
        o_ref[...] = _pad_cols(jnp.concatenate([l_ref[k] for k in range(4)], axis=1))

    return pl.pallas_call(body, name=name, out_shape=_sds((D, NP), land.dtype), grid=(D // rows,),
                          in_specs=[pl.BlockSpec((4, rows, SHARD_IN), lambda i: (0, i, 0))],
                          out_specs=pl.BlockSpec((rows, NP), lambda i: (i, 0)), compiler_params=_cp())(land)


def _grad_blocks(dwa, dwb, name):
    rows = 128

    def body(a_ref, b_ref, o_ref):
        g = _unpad_cols(jnp.concatenate([a_ref[...], b_ref[...]], axis=1))
        for k in range(4):
            o_ref[k] = g[:, SHARD_IN * k:SHARD_IN * (k + 1)].astype(o_ref.dtype)

    half = pl.BlockSpec((rows, NP // 2), lambda i: (i, 0))
    return pl.pallas_call(body, name=name, out_shape=_sds((4, D, SHARD_IN), jnp.bfloat16), grid=(D // rows,),
                          in_specs=[half, half], out_specs=pl.BlockSpec((4, rows, SHARD_IN), lambda i: (0, i, 0)),
                          compiler_params=_cp())(dwa, dwb)


def _loss_bwd(xf, tgt, name):
    def body(x_ref, t_ref, dx_ref, l_ref):
        @pl.when(pl.program_id(0) == 0)
        def _():
            l_ref[...] = jnp.zeros_like(l_ref)
        e = x_ref[...] - t_ref[...]
        dx_ref[...] = e * (1.0 / D)
        l_ref[...] += 0.5 * jnp.sum(jnp.mean(e * e, axis=1, keepdims=True), axis=0, keepdims=True)

    return pl.pallas_call(body, name=name, out_shape=(_sds((SEQ, D)), _sds((8, LANES))), grid=(SEQ // TM,),
                          in_specs=[_ROW, _ROW], out_specs=(_ROW, _full((8, LANES))), compiler_params=_cp())(xf, tgt)


def _mod_part(c_all, ada_w, ada_b, name):
    def body(c_ref, w_ref, b_ref, o_ref):
        o_ref[0] = _mm(_silu(c_ref[...]), w_ref[0]) + b_ref[0]

    return pl.pallas_call(body, name=name, out_shape=_sds((DEPTH, 8, 768)), grid=(DEPTH,),
                          in_specs=[_full((8, D)), pl.BlockSpec((1, D, 768), lambda i: (i, 0, 0)), pl.BlockSpec((1, 1, 768), lambda i: (i, 0, 0))],
                          out_specs=pl.BlockSpec((1, 8, 768), lambda i: (i, 0, 0)), compiler_params=_cp())(c_all, ada_w, ada_b)


def _ada_grad(c_t, dmod, name):
    def body(c_ref, d_ref, o_ref):
        ca = _silu(c_ref[...])
        dm = d_ref[0]
        acc = ca[:, 0:1] * dm[0:1, :]
        for s in range(1, 8):
            acc = acc + ca[:, s:s + 1] * dm[s:s + 1, :]
        o_ref[0] = acc

    return pl.pallas_call(body, name=name, out_shape=_sds((DEPTH, D, 768)), grid=(DEPTH,),
                          in_specs=[_full((D, LANES)), pl.BlockSpec((1, 8, 768), lambda i: (i, 0, 0))],
                          out_specs=pl.BlockSpec((1, D, 768), lambda i: (i, 0, 0)), compiler_params=_cp())(c_t, dmod)


def _pack(parts):
    flat = []
    for p in parts:
        f = p.reshape(-1)
        flat.append(jnp.pad(f, (0, (-f.size) % LANES)))
    v = jnp.concatenate(flat)
    return jnp.pad(v, (0, (-v.size) % (8 * LANES))).reshape(-1, LANES)


def _unpack(v, shapes):
    v = v.reshape(-1)
    out, off = [], 0
    for s in shapes:
        n = math.prod(s)
        out.append(v[off:off + n].reshape(s))
        off += n + (-n) % LANES
    return out


_GIVEN_DT, _GIVEN_C = 4608, 4624


def _pad_cols(w):
    return jnp.concatenate([w[..., :_GIVEN_DT], w[..., _GIVEN_C:], w[..., _GIVEN_DT:_GIVEN_C],
                            jnp.zeros(w.shape[:-1] + (NP - IN_COLS,), w.dtype)], axis=-1)


def _unpad_cols(w):
    return jnp.concatenate([w[..., :_GIVEN_DT], w[..., DTC:DTC + 16], w[..., _GIVEN_DT:DTC]], axis=-1)


def _pad_lanes(v):
    return jnp.pad(v, (0, LANES - v.shape[0])).reshape(1, LANES)


def _local_step(x2, tgt, mod, weights_of, grads_done, pre_w, post_w, conv_w, conv_b, dt_bias, a_log, d_skip, nw, sinks):
    saved = []
    xcur = x2
    for i in range(DEPTH):
        shift, scale, gate = mod[i:i + 1, :D], mod[i:i + 1, D:2 * D], mod[i:i + 1, 2 * D:]
        pw, qw = pre_w[i:i + 1], post_w[i:i + 1]
        w_p, w_o = weights_of(i, xcur)
        proj, h = _proj_fwd(xcur, pw, scale, shift, w_p, 0, "proj_fwd")
        o_a, lse_a = _attn_fwd(proj, QA // LANES, KA // LANES, VA // LANES, DILS, False, None, "attn_a_fwd")
        sink_x = jnp.repeat(sinks[i], HD).reshape(1, 512)
        o_c, lse_c = _attn_fwd(proj, QC // LANES, KC // LANES, VC // LANES, (1,), True, sink_x, "attn_c_fwd")
        cw, cb = conv_w[i], conv_b[i:i + 1]
        xbc_act = _conv_fwd(proj, cw, cb, "conv_fwd")
        ssd_p = (_pad_lanes(a_log[i]), _pad_lanes(dt_bias[i]), jnp.repeat(d_skip[i], HD).reshape(1, 1024), nw[i:i + 1])
        yb, hin = _ssd_fwd(xbc_act, proj, *ssd_p, "ssd_fwd")
        xnew, y = _out_fwd(o_a, yb, o_c, proj, w_o, 0, xcur, gate, qw, "out_fwd")
        saved.append((w_p, w_o, xcur, scale, gate, pw, qw, proj, h, o_a, lse_a, sink_x, o_c, lse_c, cw, cb, xbc_act, ssd_p, yb, hin, y))
        xcur = xnew
    dx, ltile = _loss_bwd(xcur, tgt, "loss")
    dmod, small = [None] * DEPTH, [None] * DEPTH
    for i in reversed(range(DEPTH)):
        w_p, w_o, xin, scale, gate, pw, qw, proj, h, o_a, lse_a, sink_x, o_c, lse_c, cw, cb, xbc_act, ssd_p, yb, hin, y = saved[i]
        dy, dgate, dpost, do_a, dz_a, dyb, do_c, dz_c = _dymix(dx, y, gate, qw, w_o, 0, o_a, o_c, proj, "dymix")
        dwo = _dwout(o_a, yb, o_c, proj, dy, "dwout")
        dq_a, dk_a, dv_a = _attn_bwd(proj, QA // LANES, KA // LANES, VA // LANES, do_a, o_a, lse_a, DILS, False, None, "attn_a_bwd")
        dq_c, dk_c, dv_c, dsk = _attn_bwd(proj, QC // LANES, KC // LANES, VC // LANES, do_c, o_c, lse_c, (1,), True, sink_x, "attn_c_bwd")
        dxbc_act, dz_b, ddt, dal16, ddtb, ddsk, dnw = _ssd_bwd(xbc_act, proj, hin, dyb, *ssd_p, "ssd_bwd")
        dxbc, dcw, dcb = _conv_bwd(proj, dxbc_act, cw, cb, "conv_bwd")
        half_a, half_b = [dq_a, dk_a, dv_a, dz_a, dz_b], [dxbc, dq_c, dz_c, dk_c, dv_c, ddt]
        sent = grads_done(i, _dwin(h, half_a, "dwin_a"), _dwin(h, half_b, "dwin_b"), dwo)
        dx, dshift, dscale, dpre = _dh_bwd(half_a + half_b, w_p, xin, pw, scale + sent[0, 0], dx, "dh_bwd")
        dmod[i] = jnp.concatenate([dshift, dscale, dgate], axis=1)
        small[i] = (dpre, dpost, dcw, dcb, ddtb[0, :16], dal16[0, :16], ddsk.reshape(16, HD).sum(axis=1), dnw, dsk[:, 0, ::HD].reshape(8))
    return ltile, dx, jnp.concatenate(dmod, axis=0), small


_SMALL = ((1, D), (1, D), (4, CONV_CH), (1, CONV_CH), (16,), (16,), (16,), (1, D), (8,))


def kernel(x, c, ada_w, ada_b, pre_norm_w, post_norm_w, w_in, conv_w, conv_b, dt_bias, a_log, d_skip, ssm_norm_w, sinks, w_out, loss_target, m_ada_w, m_ada_b, m_pre_norm_w, m_post_norm_w, m_w_in, m_conv_w, m_conv_b, m_dt_bias, m_a_log, m_d_skip, m_ssm_norm_w, m_sinks, m_w_out, v_ada_w, v_ada_b, v_pre_norm_w, v_post_norm_w, v_w_in, v_conv_w, v_conv_b, v_dt_bias, v_a_log, v_d_skip, v_ssm_norm_w, v_sinks, v_w_out):
    xi, yi, ci = lax.axis_index("x"), lax.axis_index("y"), lax.axis_index("c")
    chip = 2 * xi + yi
    me = 2 * chip + ci

    w_in_b = _cast_bf16(w_in, 512, "cast_w_in")
    w_out_b = _cast_bf16(w_out, 512, "cast_w_out")
    gathers = []
    for i in range(DEPTH):
        lands = [lax.dynamic_update_slice(lax.empty((4,) + a.shape[1:], a.dtype), a[i][None], (chip, 0, 0)) for a in (w_in_b, w_out_b)]
        gathers.append(_split_start(None, lands, f"gather_start{i}", "half" if i == 0 else "whole"))
    all_started = gathers[0][3] + gathers[1][3] + gathers[2][3] + gathers[3][3]

    def weights_of(i, after):
        send_sems, recv_sems, thru, _ = gathers[i]
        if i == 0:
            halves = _split_wait(send_sems, recv_sems, thru, 2, all_started + mod[:1, :LANES], "gather_wait0", "half")
            send_sems, recv_sems, thru, after = _split_start(None, halves, "share_start0", "sibling")
            g_in, g_out = _split_wait(send_sems, recv_sems, thru, 2, after, "share_wait0", "sibling")
        else:
            g_in, g_out = _split_wait(send_sems, recv_sems, thru, 2, after, f"gather_wait{i}")
        return _w_in_padded(g_in, "w_in_padded")[None], g_out.reshape(1, 2048, D)

    scatters = [None] * DEPTH

    def grads_done(i, dwa, dwb, dwo):
        blocks = [_grad_blocks(dwa, dwb, "grad_blocks"), _cast_bf16(dwo.reshape(4, 512, D), 512, "cast_dw_out")]
        scatters[i] = _split_start(blocks, [lax.empty(b.shape, b.dtype) for b in blocks], f"scatter_start{i}")
        return scatters[i][3]

    g0 = _allgather8(_pack([c, conv_w]), "gather_c")
    c_all = g0[:, :8, :].reshape(8, D)
    conv_w_full = jnp.concatenate([g0[2 * k, 8:56, :].reshape(DEPTH, 4, CONV_CH // 4) for k in range(4)], axis=-1)

    ada_b_mine = lax.dynamic_slice_in_dim(ada_b, 768 * chip, 768, axis=1).reshape(DEPTH, 1, 768)
    gm = _allgather8(_mod_part(c_all, ada_w, ada_b_mine, "mod_part").reshape(DEPTH * 8, 768), "gather_mod")
    gm = gm.reshape(4, 2, DEPTH, 8, 768)[:, 0]
    mod = lax.dynamic_index_in_dim(gm, me, axis=2, keepdims=False).transpose(1, 0, 2).reshape(DEPTH, 3 * D)

    ltile, dx, dmod, small = _local_step(x[0], loss_target[0], mod, weights_of, grads_done, pre_norm_w, post_norm_w, conv_w_full,
                                         conv_b, dt_bias, a_log, d_skip, ssm_norm_w, sinks)

    packed = _pack([dmod] + [g for layer in small for g in layer] + [ltile[0]])
    gs = _allgather8(packed, "gather_small")
    tot = _sum_blocks(gs[:, None], packed.shape[0], "sum_small")[0]
    parts = _unpack(tot, [(DEPTH, 3 * D)] + list(_SMALL) * DEPTH + [(LANES,)])
    g_ada_b, loss = parts[0], parts[-1][0]
    per_layer = [parts[1 + len(_SMALL) * i:1 + len(_SMALL) * (i + 1)] for i in range(DEPTH)]
    g_pre, g_post, g_cw, g_cb, g_dtb, g_al, g_dsk, g_nw, g_sk = [jnp.stack([per_layer[i][j] for i in range(DEPTH)]) for j in range(len(_SMALL))]
    g_pre, g_post, g_cb, g_nw = g_pre[:, 0], g_post[:, 0], g_cb[:, 0], g_nw[:, 0]
    g_cw = lax.dynamic_slice_in_dim(g_cw, (CONV_CH // 4) * chip, CONV_CH // 4, axis=2)

    dmod_all = gs[:, :(DEPTH * 3 * D) // LANES, :].reshape(8, DEPTH, 3 * D).transpose(1, 0, 2)
    dmod_mine = lax.dynamic_slice_in_dim(dmod_all, 768 * chip, 768, axis=2)
    c_t = jnp.pad(c_all.T, ((0, 0), (0, LANES - 8)))
    g_ada_w = _ada_grad(c_t, dmod_mine, "ada_grad")

    res = {}
    res["ada_w"] = _adamw(ada_w, [g_ada_w], m_ada_w, v_ada_w, 512, "adamw_ada_w")
    names = ["ada_b", "pre_norm_w", "post_norm_w", "conv_w", "conv_b", "dt_bias", "a_log", "d_skip", "ssm_norm_w", "sinks"]
    ws = [ada_b, pre_norm_w, post_norm_w, conv_w, conv_b, dt_bias, a_log, d_skip, ssm_norm_w, sinks]
    gsm = [g_ada_b, g_pre, g_post, g_cw, g_cb, g_dtb, g_al, g_dsk, g_nw, g_sk]
    ms = [m_ada_b, m_pre_norm_w, m_post_norm_w, m_conv_w, m_conv_b, m_dt_bias, m_a_log, m_d_skip, m_ssm_norm_w, m_sinks]
    vs = [v_ada_b, v_pre_norm_w, v_post_norm_w, v_conv_w, v_conv_b, v_dt_bias, v_a_log, v_d_skip, v_ssm_norm_w, v_sinks]
    pw_, pg_, pm_, pv_ = _pack(ws), _pack(gsm), _pack(ms), _pack(vs)
    small_out = _adamw(pw_[None], [pg_[None]], pm_[None], pv_[None], pw_.shape[0], "adamw_small")

    others_done = small_out[1][0, :8] + res["ada_w"][1][0, :8, :LANES]
    landed = [_split_wait(*scatters[i][:3], 2, others_done, f"scatter_wait{i}") for i in range(DEPTH)]
    p_in = _sum_chips([d[2] for d in landed], [d[0] for d in landed], 128, "sum_w_in")
    p_out = _sum_chips([d[3] for d in landed], [d[1] for d in landed], 256, "sum_w_out")
    col_major, row_major = (lambda a: jnp.transpose(a, (2, 0, 1))), (lambda a: jnp.transpose(a, (1, 2, 0)))
    p_in = col_major(p_in)
    s_in, s_out = _sibling_swap([p_in, p_out], "swap_partials")
    res["w_in"] = [row_major(a) for a in _adamw(col_major(w_in), [p_in, s_in], col_major(m_w_in), col_major(v_w_in), None,
                                                "adamw_w_in", lead=SHARD_IN // 18)]
    res["w_out"] = _adamw(w_out, [p_out, s_out], m_w_out, v_w_out, 512, "adamw_w_out")
    shapes = [w.shape for w in ws]
    for kind in range(4):
        for nm, a in zip(names, _unpack(small_out[kind][0], shapes)):
            res.setdefault(nm, [None] * 4)[kind] = a
    order = ["ada_w", "ada_b", "pre_norm_w", "post_norm_w", "w_in", "conv_w", "conv_b", "dt_bias", "a_log", "d_skip", "ssm_norm_w", "sinks", "w_out"]
    return (loss, dx[None], *[res[n][0] for n in order], *[res[n][1] for n in order], *[res[n][2] for n in order], *[res[n][3] for n in order])
```
